```python
import jax, jax.numpy as jnp
from jax import lax
import numpy as np

D_MODEL = 1024
BATCH = 8
SEQ = 4096
DEPTH = 2

GRID_W = 64
D_MIX = D_MODEL
GROUP_W = D_MIX // 4
HEAD_DIM = 64
CONV_A_WIDTH = 31
GQA_HEADS = GROUP_W // HEAD_DIM
GQA_KV_HEADS = 2
CHUNK = 128
SGU_GROUPS = GROUP_W // HEAD_DIM
SGU_GROUP_DIM = GROUP_W // SGU_GROUPS
MLA_HEADS = GROUP_W // HEAD_DIM
MLA_Q_LORA = 3 * D_MODEL // 16
MLA_KV_LORA = D_MODEL // 8
MLA_NOPE = 64
MLA_ROPE = 32
MLA_V = GROUP_W // MLA_HEADS
Q_BLOCK = 128
ROPE_THETA = 10000.0
D_FF = 2816
FFN_CONV_WIDTH = 3
DEEPNORM_ALPHA = (2 * DEPTH) ** 0.25
DEEPNORM_BETA = (8 * DEPTH) ** -0.25
LN_EPS = 1e-5
RMS_EPS = 1e-6

SPLIT_SIZES = (2 * GROUP_W,
               GQA_HEADS * HEAD_DIM,
               GQA_KV_HEADS * HEAD_DIM,
               GQA_KV_HEADS * HEAD_DIM,
               2 * GROUP_W,
               MLA_Q_LORA,
               MLA_KV_LORA,
               MLA_ROPE)
D_IN_PROJ = sum(SPLIT_SIZES)
SPLIT_POINTS = [sum(SPLIT_SIZES[:i + 1]) for i in range(len(SPLIT_SIZES) - 1)]

kernel_name = 'hymba_style_hybrid_encoder'


def layer_norm(x, g, b):
    xf = x.astype(jnp.float32)
    mu = jnp.mean(xf, axis=-1, keepdims=True)
    xc = xf - mu
    var = jnp.mean(xc * xc, axis=-1, keepdims=True)
    return (xc * lax.rsqrt(var + LN_EPS) * g + b).astype(x.dtype)


def rms_norm(x, g):
    xf = x.astype(jnp.float32)
    ms = jnp.mean(xf * xf, axis=-1, keepdims=True)
    return (xf * lax.rsqrt(ms + RMS_EPS) * g).astype(x.dtype)


def depthwise_conv(x, w, b):
    k = w.shape[0]
    c = x.shape[-1]
    y = lax.conv_general_dilated(x, w[:, None, :], window_strides=(1,),
                                 padding=[(k // 2, k // 2)],
                                 dimension_numbers=('NWC', 'WIO', 'NWC'),
                                 feature_group_count=c)
    return y + b


def rope_1d(x, pos):
    d = x.shape[-1]
    half = d // 2
    inv_freq = ROPE_THETA ** (-jnp.arange(half, dtype=jnp.float32) / half)
    ang = pos[:, None] * inv_freq[None, :]
    cos = jnp.cos(ang)[:, None, :]
    sin = jnp.sin(ang)[:, None, :]
    xf = x.astype(jnp.float32)
    x1, x2 = xf[..., :half], xf[..., half:]
    return jnp.concatenate([x1 * cos - x2 * sin, x2 * cos + x1 * sin], axis=-1).astype(x.dtype)


def rope_2d(x, row, col):
    h = x.shape[-1] // 2
    return jnp.concatenate([rope_1d(x[..., :h], row), rope_1d(x[..., h:], col)], axis=-1)


def blocked_attention(q, k, v, scale):
    bsz, seq, hk, g, dk = q.shape
    nb = seq // Q_BLOCK
    qb = q.reshape(bsz, nb, Q_BLOCK, hk, g, dk).transpose(1, 0, 2, 3, 4, 5)

    def one_block(qi):
        s = jnp.einsum('bqhgd,bshd->bhgqs', qi, k, preferred_element_type=jnp.float32) * scale
        p = jax.nn.softmax(s, axis=-1).astype(v.dtype)
        return jnp.einsum('bhgqs,bshd->bqhgd', p, v)

    out = lax.map(one_block, qb)
    return out.transpose(1, 0, 2, 3, 4, 5).reshape(bsz, seq, hk, g, v.shape[-1])


def _fwd_setup_inputs(seed: int = 0) -> dict:
    key = jax.random.key(seed)
    ks = iter(jax.random.split(key, 32))

    def nrm(shape, scale):
        return scale * jax.random.normal(next(ks), shape, dtype=jnp.float32)

    L = DEPTH
    return {
        'x': nrm((BATCH, SEQ, D_MODEL), 1.0),
        'ln_in_g': 1.0 + nrm((D_MODEL,), 0.02),
        'ln_in_b': nrm((D_MODEL,), 0.02),
        'w_in': nrm((L, D_MODEL, D_IN_PROJ), D_MODEL ** -0.5),
        'conv_a_w': nrm((L, CONV_A_WIDTH, GROUP_W), CONV_A_WIDTH ** -0.5),
        'conv_a_b': nrm((L, GROUP_W), 0.02),
        'ln_a_g': 1.0 + nrm((L, GROUP_W), 0.02),
        'ln_a_b': nrm((L, GROUP_W), 0.02),
        'qk_norm_q': 1.0 + nrm((L, HEAD_DIM), 0.02),
        'qk_norm_k': 1.0 + nrm((L, HEAD_DIM), 0.02),
        'sgu_ln_g': 1.0 + nrm((L, GROUP_W), 0.02),
        'sgu_ln_b': nrm((L, GROUP_W), 0.02),
        'sgu_w': nrm((L, SGU_GROUPS, CHUNK, CHUNK), CHUNK ** -0.5),
        'sgu_b': 1.0 + nrm((L, SGU_GROUPS, CHUNK), 0.02),
        'mla_q_norm': 1.0 + nrm((L, MLA_Q_LORA), 0.02),
        'mla_w_uq': nrm((L, MLA_Q_LORA, MLA_HEADS * (MLA_NOPE + MLA_ROPE)), MLA_Q_LORA ** -0.5),
        'mla_kv_norm': 1.0 + nrm((L, MLA_KV_LORA), 0.02),
        'mla_w_ukv': nrm((L, MLA_KV_LORA, MLA_HEADS * (MLA_NOPE + MLA_V)), MLA_KV_LORA ** -0.5),
        'w_out': nrm((L, D_MIX, D_MODEL), DEEPNORM_BETA * D_MIX ** -0.5),
        'ln_mix_g': 1.0 + nrm((L, D_MODEL), 0.02),
        'ln_mix_b': nrm((L, D_MODEL), 0.02),
        'ffn_w_up': nrm((L, D_MODEL, 2 * D_FF), D_MODEL ** -0.5),
        'ffn_conv_w': nrm((L, FFN_CONV_WIDTH, 2 * D_FF), FFN_CONV_WIDTH ** -0.5),
        'ffn_conv_b': nrm((L, 2 * D_FF), 0.02),
        'ffn_w_down': nrm((L, D_FF, D_MODEL), DEEPNORM_BETA * D_FF ** -0.5),
        'ln_ffn_g': 1.0 + nrm((L, D_MODEL), 0.02),
        'ln_ffn_b': nrm((L, D_MODEL), 0.02),
    }


def _fwd_reference(x, ln_in_g, ln_in_b, w_in, conv_a_w, conv_a_b, ln_a_g, ln_a_b,
              qk_norm_q, qk_norm_k, sgu_ln_g, sgu_ln_b, sgu_w, sgu_b,
              mla_q_norm, mla_w_uq, mla_kv_norm, mla_w_ukv, w_out, ln_mix_g, ln_mix_b,
              ffn_w_up, ffn_conv_w, ffn_conv_b, ffn_w_down, ln_ffn_g, ln_ffn_b):
    bsz, seq, _ = x.shape
    rows = seq // GRID_W
    row = jnp.repeat(jnp.arange(rows, dtype=jnp.float32), GRID_W)
    col = jnp.tile(jnp.arange(GRID_W, dtype=jnp.float32), rows)

    h = layer_norm(x, ln_in_g, ln_in_b)
    for l in range(DEPTH):
        proj = h @ w_in[l]
        a_in, b_q, b_k, b_v, c_in, d_cq, d_ckv, d_kr = jnp.split(proj, SPLIT_POINTS, axis=-1)

        a = a_in[..., :GROUP_W] * jax.nn.sigmoid(a_in[..., GROUP_W:])
        a = depthwise_conv(a, conv_a_w[l], conv_a_b[l])
        o_a = jax.nn.silu(layer_norm(a, ln_a_g[l], ln_a_b[l]))

        q = rms_norm(b_q.reshape(bsz, seq, GQA_HEADS, HEAD_DIM), qk_norm_q[l])
        k = rms_norm(b_k.reshape(bsz, seq, GQA_KV_HEADS, HEAD_DIM), qk_norm_k[l])
        v = b_v.reshape(bsz, seq, GQA_KV_HEADS, HEAD_DIM)
        q = rope_2d(q, row, col).reshape(bsz, seq, GQA_KV_HEADS, GQA_HEADS // GQA_KV_HEADS, HEAD_DIM)
        k = rope_2d(k, row, col)
        o_b = blocked_attention(q, k, v, HEAD_DIM ** -0.5).reshape(bsz, seq, GROUP_W)

        c = jax.nn.gelu(c_in)
        u, sv = c[..., :GROUP_W], c[..., GROUP_W:]
        sv = layer_norm(sv, sgu_ln_g[l], sgu_ln_b[l])
        sv = sv.reshape(bsz, seq // CHUNK, CHUNK, SGU_GROUPS, SGU_GROUP_DIM)
        sv = jnp.einsum('gpq,bnqgc->bnpgc', sgu_w[l], sv) + sgu_b[l].T[:, :, None]
        o_c = u * sv.reshape(bsz, seq, GROUP_W)

        qd = (rms_norm(d_cq, mla_q_norm[l]) @ mla_w_uq[l]).reshape(bsz, seq, MLA_HEADS, MLA_NOPE + MLA_ROPE)
        kvd = (rms_norm(d_ckv, mla_kv_norm[l]) @ mla_w_ukv[l]).reshape(bsz, seq, MLA_HEADS, MLA_NOPE + MLA_V)
        q_nope, q_rope = qd[..., :MLA_NOPE], qd[..., MLA_NOPE:]
        k_nope, v_d = kvd[..., :MLA_NOPE], kvd[..., MLA_NOPE:]
        k_rope = rope_2d(d_kr[:, :, None, :], row, col)
        q_full = jnp.concatenate([q_nope, rope_2d(q_rope, row, col)], axis=-1)
        k_full = jnp.concatenate([k_nope, jnp.broadcast_to(k_rope, (bsz, seq, MLA_HEADS, MLA_ROPE))], axis=-1)
        o_d = blocked_attention(q_full[:, :, :, None, :], k_full, v_d,
                                (MLA_NOPE + MLA_ROPE) ** -0.5).reshape(bsz, seq, GROUP_W)

        mix = jnp.concatenate([o_a, o_b, o_c, o_d], axis=-1) @ w_out[l]
        h = layer_norm(DEEPNORM_ALPHA * h + mix, ln_mix_g[l], ln_mix_b[l])

        up = depthwise_conv(h @ ffn_w_up[l], ffn_conv_w[l], ffn_conv_b[l])
        f = (jax.nn.silu(up[..., :D_FF]) * up[..., D_FF:]) @ ffn_w_down[l]
        h = layer_norm(DEEPNORM_ALPHA * h + f, ln_ffn_g[l], ln_ffn_b[l])
    return h


import jax as _jax
import jax.numpy as _jnp

TWIN_FORMAT = 'train_step'
FWD_PARAMS = ['x', 'ln_in_g', 'ln_in_b', 'w_in', 'conv_a_w', 'conv_a_b', 'ln_a_g', 'ln_a_b', 'qk_norm_q', 'qk_norm_k', 'sgu_ln_g', 'sgu_ln_b', 'sgu_w', 'sgu_b', 'mla_q_norm', 'mla_w_uq', 'mla_kv_norm', 'mla_w_ukv', 'w_out', 'ln_mix_g', 'ln_mix_b', 'ffn_w_up', 'ffn_conv_w', 'ffn_conv_b', 'ffn_w_down', 'ln_ffn_g', 'ln_ffn_b']
TWIN_WEIGHTS = ['ln_in_g', 'ln_in_b', 'w_in', 'conv_a_w', 'conv_a_b', 'ln_a_g', 'ln_a_b', 'qk_norm_q', 'qk_norm_k', 'sgu_ln_g', 'sgu_ln_b', 'sgu_w', 'sgu_b', 'mla_q_norm', 'mla_w_uq', 'mla_kv_norm', 'mla_w_ukv', 'w_out', 'ln_mix_g', 'ln_mix_b', 'ffn_w_up', 'ffn_conv_w', 'ffn_conv_b', 'ffn_w_down', 'ln_ffn_g', 'ln_ffn_b']
TWIN_DIFF_INPUT = 'x'
TWIN_INPUTS = ['x', 'ln_in_g', 'ln_in_b', 'w_in', 'conv_a_w', 'conv_a_b', 'ln_a_g', 'ln_a_b', 'qk_norm_q', 'qk_norm_k', 'sgu_ln_g', 'sgu_ln_b', 'sgu_w', 'sgu_b', 'mla_q_norm', 'mla_w_uq', 'mla_kv_norm', 'mla_w_ukv', 'w_out', 'ln_mix_g', 'ln_mix_b', 'ffn_w_up', 'ffn_conv_w', 'ffn_conv_b', 'ffn_w_down', 'ln_ffn_g', 'ln_ffn_b', 'loss_target', 'm_ln_in_g', 'm_ln_in_b', 'm_w_in', 'm_conv_a_w', 'm_conv_a_b', 'm_ln_a_g', 'm_ln_a_b', 'm_qk_norm_q', 'm_qk_norm_k', 'm_sgu_ln_g', 'm_sgu_ln_b', 'm_sgu_w', 'm_sgu_b', 'm_mla_q_norm', 'm_mla_w_uq', 'm_mla_kv_norm', 'm_mla_w_ukv', 'm_w_out', 'm_ln_mix_g', 'm_ln_mix_b', 'm_ffn_w_up', 'm_ffn_conv_w', 'm_ffn_conv_b', 'm_ffn_w_down', 'm_ln_ffn_g', 'm_ln_ffn_b', 'v_ln_in_g', 'v_ln_in_b', 'v_w_in', 'v_conv_a_w', 'v_conv_a_b', 'v_ln_a_g', 'v_ln_a_b', 'v_qk_norm_q', 'v_qk_norm_k', 'v_sgu_ln_g', 'v_sgu_ln_b', 'v_sgu_w', 'v_sgu_b', 'v_mla_q_norm', 'v_mla_w_uq', 'v_mla_kv_norm', 'v_mla_w_ukv', 'v_w_out', 'v_ln_mix_g', 'v_ln_mix_b', 'v_ffn_w_up', 'v_ffn_conv_w', 'v_ffn_conv_b', 'v_ffn_w_down', 'v_ln_ffn_g', 'v_ln_ffn_b']
TWIN_OUTPUTS = ['loss', 'grad_x', 'grad_ln_in_g', 'grad_ln_in_b', 'grad_w_in', 'grad_conv_a_w', 'grad_conv_a_b', 'grad_ln_a_g', 'grad_ln_a_b', 'grad_qk_norm_q', 'grad_qk_norm_k', 'grad_sgu_ln_g', 'grad_sgu_ln_b', 'grad_sgu_w', 'grad_sgu_b', 'grad_mla_q_norm', 'grad_mla_w_uq', 'grad_mla_kv_norm', 'grad_mla_w_ukv', 'grad_w_out', 'grad_ln_mix_g', 'grad_ln_mix_b', 'grad_ffn_w_up', 'grad_ffn_conv_w', 'grad_ffn_conv_b', 'grad_ffn_w_down', 'grad_ln_ffn_g', 'grad_ln_ffn_b', 'delta_ln_in_g', 'delta_ln_in_b', 'delta_w_in', 'delta_conv_a_w', 'delta_conv_a_b', 'delta_ln_a_g', 'delta_ln_a_b', 'delta_qk_norm_q', 'delta_qk_norm_k', 'delta_sgu_ln_g', 'delta_sgu_ln_b', 'delta_sgu_w', 'delta_sgu_b', 'delta_mla_q_norm', 'delta_mla_w_uq', 'delta_mla_kv_norm', 'delta_mla_w_ukv', 'delta_w_out', 'delta_ln_mix_g', 'delta_ln_mix_b', 'delta_ffn_w_up', 'delta_ffn_conv_w', 'delta_ffn_conv_b', 'delta_ffn_w_down', 'delta_ln_ffn_g', 'delta_ln_ffn_b', 'new_m_ln_in_g', 'new_m_ln_in_b', 'new_m_w_in', 'new_m_conv_a_w', 'new_m_conv_a_b', 'new_m_ln_a_g', 'new_m_ln_a_b', 'new_m_qk_norm_q', 'new_m_qk_norm_k', 'new_m_sgu_ln_g', 'new_m_sgu_ln_b', 'new_m_sgu_w', 'new_m_sgu_b', 'new_m_mla_q_norm', 'new_m_mla_w_uq', 'new_m_mla_kv_norm', 'new_m_mla_w_ukv', 'new_m_w_out', 'new_m_ln_mix_g', 'new_m_ln_mix_b', 'new_m_ffn_w_up', 'new_m_ffn_conv_w', 'new_m_ffn_conv_b', 'new_m_ffn_w_down', 'new_m_ln_ffn_g', 'new_m_ln_ffn_b', 'new_v_ln_in_g', 'new_v_ln_in_b', 'new_v_w_in', 'new_v_conv_a_w', 'new_v_conv_a_b', 'new_v_ln_a_g', 'new_v_ln_a_b', 'new_v_qk_norm_q', 'new_v_qk_norm_k', 'new_v_sgu_ln_g', 'new_v_sgu_ln_b', 'new_v_sgu_w', 'new_v_sgu_b', 'new_v_mla_q_norm', 'new_v_mla_w_uq', 'new_v_mla_kv_norm', 'new_v_mla_w_ukv', 'new_v_w_out', 'new_v_ln_mix_g', 'new_v_ln_mix_b', 'new_v_ffn_w_up', 'new_v_ffn_conv_w', 'new_v_ffn_conv_b', 'new_v_ffn_w_down', 'new_v_ln_ffn_g', 'new_v_ln_ffn_b']
TWIN_LEAF_KINDS = {'loss': 'loss', 'grad_x': 'grad_x', 'grad_ln_in_g': 'grad_w', 'grad_ln_in_b': 'grad_w', 'grad_w_in': 'grad_w', 'grad_conv_a_w': 'grad_w', 'grad_conv_a_b': 'grad_w', 'grad_ln_a_g': 'grad_w', 'grad_ln_a_b': 'grad_w', 'grad_qk_norm_q': 'grad_w', 'grad_qk_norm_k': 'grad_w', 'grad_sgu_ln_g': 'grad_w', 'grad_sgu_ln_b': 'grad_w', 'grad_sgu_w': 'grad_w', 'grad_sgu_b': 'grad_w', 'grad_mla_q_norm': 'grad_w', 'grad_mla_w_uq': 'grad_w', 'grad_mla_kv_norm': 'grad_w', 'grad_mla_w_ukv': 'grad_w', 'grad_w_out': 'grad_w', 'grad_ln_mix_g': 'grad_w', 'grad_ln_mix_b': 'grad_w', 'grad_ffn_w_up': 'grad_w', 'grad_ffn_conv_w': 'grad_w', 'grad_ffn_conv_b': 'grad_w', 'grad_ffn_w_down': 'grad_w', 'grad_ln_ffn_g': 'grad_w', 'grad_ln_ffn_b': 'grad_w', 'delta_ln_in_g': 'delta_w', 'delta_ln_in_b': 'delta_w', 'delta_w_in': 'delta_w', 'delta_conv_a_w': 'delta_w', 'delta_conv_a_b': 'delta_w', 'delta_ln_a_g': 'delta_w', 'delta_ln_a_b': 'delta_w', 'delta_qk_norm_q': 'delta_w', 'delta_qk_norm_k': 'delta_w', 'delta_sgu_ln_g': 'delta_w', 'delta_sgu_ln_b': 'delta_w', 'delta_sgu_w': 'delta_w', 'delta_sgu_b': 'delta_w', 'delta_mla_q_norm': 'delta_w', 'delta_mla_w_uq': 'delta_w', 'delta_mla_kv_norm': 'delta_w', 'delta_mla_w_ukv': 'delta_w', 'delta_w_out': 'delta_w', 'delta_ln_mix_g': 'delta_w', 'delta_ln_mix_b': 'delta_w', 'delta_ffn_w_up': 'delta_w', 'delta_ffn_conv_w': 'delta_w', 'delta_ffn_conv_b': 'delta_w', 'delta_ffn_w_down': 'delta_w', 'delta_ln_ffn_g': 'delta_w', 'delta_ln_ffn_b': 'delta_w', 'new_m_ln_in_g': 'new_m', 'new_m_ln_in_b': 'new_m', 'new_m_w_in': 'new_m', 'new_m_conv_a_w': 'new_m', 'new_m_conv_a_b': 'new_m', 'new_m_ln_a_g': 'new_m', 'new_m_ln_a_b': 'new_m', 'new_m_qk_norm_q': 'new_m', 'new_m_qk_norm_k': 'new_m', 'new_m_sgu_ln_g': 'new_m', 'new_m_sgu_ln_b': 'new_m', 'new_m_sgu_w': 'new_m', 'new_m_sgu_b': 'new_m', 'new_m_mla_q_norm': 'new_m', 'new_m_mla_w_uq': 'new_m', 'new_m_mla_kv_norm': 'new_m', 'new_m_mla_w_ukv': 'new_m', 'new_m_w_out': 'new_m', 'new_m_ln_mix_g': 'new_m', 'new_m_ln_mix_b': 'new_m', 'new_m_ffn_w_up': 'new_m', 'new_m_ffn_conv_w': 'new_m', 'new_m_ffn_conv_b': 'new_m', 'new_m_ffn_w_down': 'new_m', 'new_m_ln_ffn_g': 'new_m', 'new_m_ln_ffn_b': 'new_m', 'new_v_ln_in_g': 'new_v', 'new_v_ln_in_b': 'new_v', 'new_v_w_in': 'new_v', 'new_v_conv_a_w': 'new_v', 'new_v_conv_a_b': 'new_v', 'new_v_ln_a_g': 'new_v', 'new_v_ln_a_b': 'new_v', 'new_v_qk_norm_q': 'new_v', 'new_v_qk_norm_k': 'new_v', 'new_v_sgu_ln_g': 'new_v', 'new_v_sgu_ln_b': 'new_v', 'new_v_sgu_w': 'new_v', 'new_v_sgu_b': 'new_v', 'new_v_mla_q_norm': 'new_v', 'new_v_mla_w_uq': 'new_v', 'new_v_mla_kv_norm': 'new_v', 'new_v_mla_w_ukv': 'new_v', 'new_v_w_out': 'new_v', 'new_v_ln_mix_g': 'new_v', 'new_v_ln_mix_b': 'new_v', 'new_v_ffn_w_up': 'new_v', 'new_v_ffn_conv_w': 'new_v', 'new_v_ffn_conv_b': 'new_v', 'new_v_ffn_w_down': 'new_v', 'new_v_ln_ffn_g': 'new_v', 'new_v_ln_ffn_b': 'new_v'}


def _forward(args):
    return _fwd_reference(*[args[k] for k in FWD_PARAMS])


def _output_shape():
    out = _jax.eval_shape(lambda: _forward(_fwd_setup_inputs(0)))
    return out.shape, out.dtype

N_MICROBATCH = 1
ADAM_LR = 0.001
ADAM_B1 = 0.9
ADAM_B2 = 0.999
ADAM_EPS = 1e-08
ADAM_WD = 0.01
ADAM_STEP = 10
PER_EXAMPLE_BATCH_AXIS = {'x': 0, 'loss_target': 0}
SHARED_INPUTS = []
_WEIGHT_DTYPES = {'ln_in_g': _jnp.float32, 'ln_in_b': _jnp.float32, 'w_in': _jnp.float32, 'conv_a_w': _jnp.float32, 'conv_a_b': _jnp.float32, 'ln_a_g': _jnp.float32, 'ln_a_b': _jnp.float32, 'qk_norm_q': _jnp.float32, 'qk_norm_k': _jnp.float32, 'sgu_ln_g': _jnp.float32, 'sgu_ln_b': _jnp.float32, 'sgu_w': _jnp.float32, 'sgu_b': _jnp.float32, 'mla_q_norm': _jnp.float32, 'mla_w_uq': _jnp.float32, 'mla_kv_norm': _jnp.float32, 'mla_w_ukv': _jnp.float32, 'w_out': _jnp.float32, 'ln_mix_g': _jnp.float32, 'ln_mix_b': _jnp.float32, 'ffn_w_up': _jnp.float32, 'ffn_conv_w': _jnp.float32, 'ffn_conv_b': _jnp.float32, 'ffn_w_down': _jnp.float32, 'ln_ffn_g': _jnp.float32, 'ln_ffn_b': _jnp.float32}
MOMENT_SCALE = {'ln_in_g': 8.811419e-01, 'ln_in_b': 4.541316e-01, 'w_in': 3.316842e-02, 'conv_a_w': 3.945687e-02, 'conv_a_b': 1.495974e-01, 'ln_a_g': 6.110943e-02, 'ln_a_b': 8.239390e-02, 'qk_norm_q': 1.351797e-02, 'qk_norm_k': 1.318132e-02, 'sgu_ln_g': 4.256234e-02, 'sgu_ln_b': 4.903081e-02, 'sgu_w': 2.944947e-02, 'sgu_b': 2.828982e-02, 'mla_q_norm': 9.406644e-03, 'mla_w_uq': 6.479195e-03, 'mla_kv_norm': 1.869161e-02, 'mla_w_ukv': 8.170041e-03, 'w_out': 8.176650e-02, 'ln_mix_g': 9.891115e-01, 'ln_mix_b': 4.500696e-01, 'ffn_w_up': 2.355066e-02, 'ffn_conv_w': 2.334329e-02, 'ffn_conv_b': 2.556794e-02, 'ffn_w_down': 7.651691e-02, 'ln_ffn_g': 2.265175e+01, 'ln_ffn_b': 2.016770e+00}


def _to_microbatches(a, axis):
    t = _jnp.moveaxis(a, axis, 0)
    t = t.reshape((N_MICROBATCH, t.shape[0] // N_MICROBATCH) + t.shape[1:])
    return _jnp.moveaxis(t, 1, axis + 1)


def setup_inputs(seed: int = 0) -> dict:
    inp = _fwd_setup_inputs(seed)
    key = _jax.random.fold_in(_jax.random.key(seed), 7919)
    shape, _ = _output_shape()
    out = dict(inp)
    out["loss_target"] = _jax.random.normal(_jax.random.fold_in(key, 0), shape, _jnp.float32)
    for i, name in enumerate(TWIN_WEIGHTS):
        w = inp[name].astype(_jnp.float32)
        if MOMENT_SCALE is None:
            s = _jnp.sqrt(_jnp.mean(_jnp.square(w)) + 1e-30)
        else:
            s = MOMENT_SCALE[name]
        km, kv = _jax.random.split(_jax.random.fold_in(key, i + 1))
        out[name] = w
        out["m_" + name] = s * _jax.random.normal(km, w.shape, _jnp.float32)
        out["v_" + name] = (s * s) * _jax.random.uniform(kv, w.shape, _jnp.float32, 0.5, 1.5)
    if N_MICROBATCH > 1:
        for name, axis in PER_EXAMPLE_BATCH_AXIS.items():
            out[name] = _to_microbatches(out[name], axis)
    return {'x': out['x'], 'ln_in_g': out['ln_in_g'], 'ln_in_b': out['ln_in_b'], 'w_in': out['w_in'], 'conv_a_w': out['conv_a_w'], 'conv_a_b': out['conv_a_b'], 'ln_a_g': out['ln_a_g'], 'ln_a_b': out['ln_a_b'], 'qk_norm_q': out['qk_norm_q'], 'qk_norm_k': out['qk_norm_k'], 'sgu_ln_g': out['sgu_ln_g'], 'sgu_ln_b': out['sgu_ln_b'], 'sgu_w': out['sgu_w'], 'sgu_b': out['sgu_b'], 'mla_q_norm': out['mla_q_norm'], 'mla_w_uq': out['mla_w_uq'], 'mla_kv_norm': out['mla_kv_norm'], 'mla_w_ukv': out['mla_w_ukv'], 'w_out': out['w_out'], 'ln_mix_g': out['ln_mix_g'], 'ln_mix_b': out['ln_mix_b'], 'ffn_w_up': out['ffn_w_up'], 'ffn_conv_w': out['ffn_conv_w'], 'ffn_conv_b': out['ffn_conv_b'], 'ffn_w_down': out['ffn_w_down'], 'ln_ffn_g': out['ln_ffn_g'], 'ln_ffn_b': out['ln_ffn_b'], 'loss_target': out['loss_target'], 'm_ln_in_g': out['m_ln_in_g'], 'm_ln_in_b': out['m_ln_in_b'], 'm_w_in': out['m_w_in'], 'm_conv_a_w': out['m_conv_a_w'], 'm_conv_a_b': out['m_conv_a_b'], 'm_ln_a_g': out['m_ln_a_g'], 'm_ln_a_b': out['m_ln_a_b'], 'm_qk_norm_q': out['m_qk_norm_q'], 'm_qk_norm_k': out['m_qk_norm_k'], 'm_sgu_ln_g': out['m_sgu_ln_g'], 'm_sgu_ln_b': out['m_sgu_ln_b'], 'm_sgu_w': out['m_sgu_w'], 'm_sgu_b': out['m_sgu_b'], 'm_mla_q_norm': out['m_mla_q_norm'], 'm_mla_w_uq': out['m_mla_w_uq'], 'm_mla_kv_norm': out['m_mla_kv_norm'], 'm_mla_w_ukv': out['m_mla_w_ukv'], 'm_w_out': out['m_w_out'], 'm_ln_mix_g': out['m_ln_mix_g'], 'm_ln_mix_b': out['m_ln_mix_b'], 'm_ffn_w_up': out['m_ffn_w_up'], 'm_ffn_conv_w': out['m_ffn_conv_w'], 'm_ffn_conv_b': out['m_ffn_conv_b'], 'm_ffn_w_down': out['m_ffn_w_down'], 'm_ln_ffn_g': out['m_ln_ffn_g'], 'm_ln_ffn_b': out['m_ln_ffn_b'], 'v_ln_in_g': out['v_ln_in_g'], 'v_ln_in_b': out['v_ln_in_b'], 'v_w_in': out['v_w_in'], 'v_conv_a_w': out['v_conv_a_w'], 'v_conv_a_b': out['v_conv_a_b'], 'v_ln_a_g': out['v_ln_a_g'], 'v_ln_a_b': out['v_ln_a_b'], 'v_qk_norm_q': out['v_qk_norm_q'], 'v_qk_norm_k': out['v_qk_norm_k'], 'v_sgu_ln_g': out['v_sgu_ln_g'], 'v_sgu_ln_b': out['v_sgu_ln_b'], 'v_sgu_w': out['v_sgu_w'], 'v_sgu_b': out['v_sgu_b'], 'v_mla_q_norm': out['v_mla_q_norm'], 'v_mla_w_uq': out['v_mla_w_uq'], 'v_mla_kv_norm': out['v_mla_kv_norm'], 'v_mla_w_ukv': out['v_mla_w_ukv'], 'v_w_out': out['v_w_out'], 'v_ln_mix_g': out['v_ln_mix_g'], 'v_ln_mix_b': out['v_ln_mix_b'], 'v_ffn_w_up': out['v_ffn_w_up'], 'v_ffn_conv_w': out['v_ffn_conv_w'], 'v_ffn_conv_b': out['v_ffn_conv_b'], 'v_ffn_w_down': out['v_ffn_w_down'], 'v_ln_ffn_g': out['v_ln_ffn_g'], 'v_ln_ffn_b': out['v_ln_ffn_b']}


def _loss(weights, diff, rest, loss_target):
    with _jax.named_scope("forward"):
        args = {**rest, TWIN_DIFF_INPUT: diff, **{k: w.astype(_WEIGHT_DTYPES[k]) for k, w in weights.items()}}
        y = _forward(args)
    with _jax.named_scope("loss_head"):
        err = _jnp.square(y.astype(_jnp.float32) - loss_target)
        return 0.5 * _jnp.sum(_jnp.mean(err, axis=-1)) if err.ndim else 0.5 * err


def _adamw(w, g, m, v):
    m = ADAM_B1 * m + (1.0 - ADAM_B1) * g
    v = ADAM_B2 * v + (1.0 - ADAM_B2) * _jnp.square(g)
    m_hat = m / (1.0 - ADAM_B1 ** ADAM_STEP)
    v_hat = v / (1.0 - ADAM_B2 ** ADAM_STEP)
    delta = -ADAM_LR * (m_hat / (_jnp.sqrt(v_hat) + ADAM_EPS) + ADAM_WD * w)
    return delta, m, v


def reference(x, ln_in_g, ln_in_b, w_in, conv_a_w, conv_a_b, ln_a_g, ln_a_b, qk_norm_q, qk_norm_k, sgu_ln_g, sgu_ln_b, sgu_w, sgu_b, mla_q_norm, mla_w_uq, mla_kv_norm, mla_w_ukv, w_out, ln_mix_g, ln_mix_b, ffn_w_up, ffn_conv_w, ffn_conv_b, ffn_w_down, ln_ffn_g, ln_ffn_b, loss_target, m_ln_in_g, m_ln_in_b, m_w_in, m_conv_a_w, m_conv_a_b, m_ln_a_g, m_ln_a_b, m_qk_norm_q, m_qk_norm_k, m_sgu_ln_g, m_sgu_ln_b, m_sgu_w, m_sgu_b, m_mla_q_norm, m_mla_w_uq, m_mla_kv_norm, m_mla_w_ukv, m_w_out, m_ln_mix_g, m_ln_mix_b, m_ffn_w_up, m_ffn_conv_w, m_ffn_conv_b, m_ffn_w_down, m_ln_ffn_g, m_ln_ffn_b, v_ln_in_g, v_ln_in_b, v_w_in, v_conv_a_w, v_conv_a_b, v_ln_a_g, v_ln_a_b, v_qk_norm_q, v_qk_norm_k, v_sgu_ln_g, v_sgu_ln_b, v_sgu_w, v_sgu_b, v_mla_q_norm, v_mla_w_uq, v_mla_kv_norm, v_mla_w_ukv, v_w_out, v_ln_mix_g, v_ln_mix_b, v_ffn_w_up, v_ffn_conv_w, v_ffn_conv_b, v_ffn_w_down, v_ln_ffn_g, v_ln_ffn_b):
    given = dict(x=x, ln_in_g=ln_in_g, ln_in_b=ln_in_b, w_in=w_in, conv_a_w=conv_a_w, conv_a_b=conv_a_b, ln_a_g=ln_a_g, ln_a_b=ln_a_b, qk_norm_q=qk_norm_q, qk_norm_k=qk_norm_k, sgu_ln_g=sgu_ln_g, sgu_ln_b=sgu_ln_b, sgu_w=sgu_w, sgu_b=sgu_b, mla_q_norm=mla_q_norm, mla_w_uq=mla_w_uq, mla_kv_norm=mla_kv_norm, mla_w_ukv=mla_w_ukv, w_out=w_out, ln_mix_g=ln_mix_g, ln_mix_b=ln_mix_b, ffn_w_up=ffn_w_up, ffn_conv_w=ffn_conv_w, ffn_conv_b=ffn_conv_b, ffn_w_down=ffn_w_down, ln_ffn_g=ln_ffn_g, ln_ffn_b=ln_ffn_b, loss_target=loss_target, m_ln_in_g=m_ln_in_g, m_ln_in_b=m_ln_in_b, m_w_in=m_w_in, m_conv_a_w=m_conv_a_w, m_conv_a_b=m_conv_a_b, m_ln_a_g=m_ln_a_g, m_ln_a_b=m_ln_a_b, m_qk_norm_q=m_qk_norm_q, m_qk_norm_k=m_qk_norm_k, m_sgu_ln_g=m_sgu_ln_g, m_sgu_ln_b=m_sgu_ln_b, m_sgu_w=m_sgu_w, m_sgu_b=m_sgu_b, m_mla_q_norm=m_mla_q_norm, m_mla_w_uq=m_mla_w_uq, m_mla_kv_norm=m_mla_kv_norm, m_mla_w_ukv=m_mla_w_ukv, m_w_out=m_w_out, m_ln_mix_g=m_ln_mix_g, m_ln_mix_b=m_ln_mix_b, m_ffn_w_up=m_ffn_w_up, m_ffn_conv_w=m_ffn_conv_w, m_ffn_conv_b=m_ffn_conv_b, m_ffn_w_down=m_ffn_w_down, m_ln_ffn_g=m_ln_ffn_g, m_ln_ffn_b=m_ln_ffn_b, v_ln_in_g=v_ln_in_g, v_ln_in_b=v_ln_in_b, v_w_in=v_w_in, v_conv_a_w=v_conv_a_w, v_conv_a_b=v_conv_a_b, v_ln_a_g=v_ln_a_g, v_ln_a_b=v_ln_a_b, v_qk_norm_q=v_qk_norm_q, v_qk_norm_k=v_qk_norm_k, v_sgu_ln_g=v_sgu_ln_g, v_sgu_ln_b=v_sgu_ln_b, v_sgu_w=v_sgu_w, v_sgu_b=v_sgu_b, v_mla_q_norm=v_mla_q_norm, v_mla_w_uq=v_mla_w_uq, v_mla_kv_norm=v_mla_kv_norm, v_mla_w_ukv=v_mla_w_ukv, v_w_out=v_w_out, v_ln_mix_g=v_ln_mix_g, v_ln_mix_b=v_ln_mix_b, v_ffn_w_up=v_ffn_w_up, v_ffn_conv_w=v_ffn_conv_w, v_ffn_conv_b=v_ffn_conv_b, v_ffn_w_down=v_ffn_w_down, v_ln_ffn_g=v_ln_ffn_g, v_ln_ffn_b=v_ln_ffn_b)
    weights = {n: given[n] for n in TWIN_WEIGHTS}
    shared = {n: given[n] for n in SHARED_INPUTS}
    per_example = {n: given[n] for n in ['x']}
    grad_fn = _jax.value_and_grad(_loss, argnums=(0, 1))

    def one_microbatch(ex, loss_target):
        ex = dict(ex)
        diff = ex.pop(TWIN_DIFF_INPUT)
        return grad_fn(weights, diff, {**shared, **ex}, loss_target)

    if N_MICROBATCH == 1:
        loss, (grad_w, grad_x) = one_microbatch(per_example, given["loss_target"])
    else:
        def body(carry, xs):
            loss_sum, grad_sum = carry
            l_k, (gw_k, gx_k) = one_microbatch(xs[0], xs[1])
            with _jax.named_scope("update"):
                return (loss_sum + l_k, _jax.tree.map(_jnp.add, grad_sum, gw_k)), gx_k

        init = (_jnp.zeros((), _jnp.float32), _jax.tree.map(_jnp.zeros_like, weights))
        (loss, grad_w), grad_x = _jax.lax.scan(body, init, (per_example, given["loss_target"]))
    with _jax.named_scope("update"):
        delta_w, new_m, new_v = {}, {}, {}
        for n in TWIN_WEIGHTS:
            delta_w[n], new_m[n], new_v[n] = _adamw(weights[n], grad_w[n], given["m_" + n], given["v_" + n])
    return (loss, grad_x, *[grad_w[n] for n in TWIN_WEIGHTS], *[delta_w[n] for n in TWIN_WEIGHTS],
            *[new_m[n] for n in TWIN_WEIGHTS], *[new_v[n] for n in TWIN_WEIGHTS])
```

```python
import functools
import math

import jax
import jax.numpy as jnp
import numpy as np
from jax import lax
from jax.experimental import pallas as pl
from jax.experimental.pallas import tpu as pltpu

F32 = jnp.float32
BF16 = jnp.bfloat16

D_MODEL = 1024
DEPTH = 2
GRID_W = 64
GROUP_W = 256
HEAD_DIM = 64
CONV_A_WIDTH = 31
CHUNK = 128
MLA_Q_LORA = 192
MLA_KV_LORA = 128
MLA_NOPE = 64
MLA_ROPE = 32
D_FF = 2816
ROPE_THETA = 10000.0
DEEPNORM_ALPHA = (2 * DEPTH) ** 0.25
LN_EPS = 1e-5
RMS_EPS = 1e-6
D_IN_PROJ = 1888
PROJ_W = 2048

ADAM_LR = 0.001
ADAM_B1 = 0.9
ADAM_B2 = 0.999
ADAM_EPS = 1e-08
ADAM_WD = 0.01
ADAM_STEP = 10

VMEM_LIMIT = 56 * 1024 * 1024
MESH = pl.DeviceIdType.MESH
N_CHIPS = 4
N_DEV = 8

WEIGHT_NAMES = ['ln_in_g', 'ln_in_b', 'w_in', 'conv_a_w', 'conv_a_b', 'ln_a_g', 'ln_a_b', 'qk_norm_q', 'qk_norm_k',
                'sgu_ln_g', 'sgu_ln_b', 'sgu_w', 'sgu_b', 'mla_q_norm', 'mla_w_uq', 'mla_kv_norm', 'mla_w_ukv', 'w_out',
                'ln_mix_g', 'ln_mix_b', 'ffn_w_up', 'ffn_conv_w', 'ffn_conv_b', 'ffn_w_down', 'ln_ffn_g', 'ln_ffn_b']
SHARDED_AXIS = {'w_in': 2, 'conv_a_w': 2, 'mla_w_uq': 2, 'mla_w_ukv': 2, 'w_out': 1, 'ffn_w_up': 2, 'ffn_conv_w': 2,
                'ffn_w_down': 1}
SHARDED_NAMES = [n for n in WEIGHT_NAMES if n in SHARDED_AXIS]
REPL_NAMES = [n for n in WEIGHT_NAMES if n not in SHARDED_AXIS]
BF16_WIRE = ('w_in', 'mla_w_uq', 'mla_w_ukv', 'w_out', 'ffn_w_up', 'ffn_w_down')


def _cparams(sem):
    return pltpu.CompilerParams(dimension_semantics=sem, vmem_limit_bytes=VMEM_LIMIT)


def _pick(n, cands):
    for c in cands:
        if n % c == 0:
            return c
    return n


def mm(name, a, b, *, ta=False, tb=False, add=None, out_dtype=F32):
    m, k = (a.shape[1], a.shape[0]) if ta else a.shape
    n = b.shape[0] if tb else b.shape[1]
    assert (b.shape[1] if tb else b.shape[0]) == k
    tm = _pick(m, (512, 256, 128))
    tn = _pick(n, (1024, 1408, 512, 256, 128))
    tk = _pick(k, (1024, 1408, 512, 256, 128))
    nk = k // tk
    a_spec = pl.BlockSpec((tk, tm), lambda i, j, kk: (kk, i)) if ta else pl.BlockSpec((tm, tk), lambda i, j, kk: (i, kk))
    b_spec = pl.BlockSpec((tn, tk), lambda i, j, kk: (j, kk)) if tb else pl.BlockSpec((tk, tn), lambda i, j, kk: (kk, j))
    in_specs = [a_spec, b_spec]
    args = [a, b]
    if add is not None:
        in_specs.append(pl.BlockSpec((tm, tn), lambda i, j, kk: (i, j)))
        args.append(add)
    dims = (((0 if ta else 1,), (1 if tb else 0,)), ((), ()))

    def body(*refs):
        a_ref, b_ref = refs[0], refs[1]
        add_ref = refs[2] if add is not None else None
        o_ref, acc_ref = refs[-2], refs[-1]
        kk = pl.program_id(2)
        part = lax.dot_general(a_ref[...].astype(BF16), b_ref[...].astype(BF16), dims, preferred_element_type=F32)

        @pl.when(kk == 0)
        def _():
            acc_ref[...] = part

        @pl.when(kk > 0)
        def _():
            acc_ref[...] += part

        @pl.when(kk == nk - 1)
        def _():
            r = acc_ref[...]
            if add_ref is not None:
                r = r + add_ref[...].astype(F32)
            o_ref[...] = r.astype(out_dtype)

    return pl.pallas_call(
        body, name=name, grid=(m // tm, n // tn, nk),
        in_specs=in_specs, out_specs=pl.BlockSpec((tm, tn), lambda i, j, kk: (i, j)),
        out_shape=jax.ShapeDtypeStruct((m, n), out_dtype),
        scratch_shapes=[pltpu.VMEM((tm, tn), F32)],
        compiler_params=_cparams(("parallel", "parallel", "arbitrary")),
    )(*args)


def _cb(cb):
    return cb if callable(cb) else (lambda j, _c=cb: _c)


def tiled_call(name, fn, s, tm, ins, outs, ncol=1):
    nrow = s // tm
    in_specs, args, kinds = [], [], []
    for it in ins:
        kind = it[0]
        if kind == 'row':
            _, arr, w, cb = it
            in_specs.append(pl.BlockSpec((tm, w), lambda j, i, _c=_cb(cb): (i, _c(j))))
            args.append(arr)
            kinds.append(('row',))
        elif kind == 'halo':
            _, arr, w, cb, h = it
            r = tm // h
            nh = s // h
            in_specs.append(pl.BlockSpec((h, w), lambda j, i, _c=_cb(cb), _r=r: (jnp.maximum(i * _r - 1, 0), _c(j))))
            in_specs.append(pl.BlockSpec((tm, w), lambda j, i, _c=_cb(cb): (i, _c(j))))
            in_specs.append(pl.BlockSpec((h, w), lambda j, i, _c=_cb(cb), _r=r, _n=nh: (jnp.minimum((i + 1) * _r, _n - 1), _c(j))))
            args += [arr, arr, arr]
            kinds.append(('halo',))
        elif kind == 'full':
            arr = it[1]
            in_specs.append(pl.BlockSpec(arr.shape, lambda j, i, _n=arr.ndim: (0,) * _n))
            args.append(arr)
            kinds.append(('ref',))
        elif kind == 'col':
            _, arr, w, cb = it
            in_specs.append(pl.BlockSpec((arr.shape[0], w), lambda j, i, _c=_cb(cb): (0, _c(j))))
            args.append(arr)
            kinds.append(('ref',))
        elif kind == 'hm':
            arr = it[1]
            in_specs.append(pl.BlockSpec((arr.shape[0], tm, arr.shape[2]), lambda j, i: (0, i, 0)))
            args.append(arr)
            kinds.append(('ref',))
        else:
            raise ValueError(kind)
    out_specs, out_shapes, okinds = [], [], []
    for ot in outs:
        kind = ot[0]
        if kind == 'row':
            _, wt, w, cb, dt = ot
            out_specs.append(pl.BlockSpec((tm, w), lambda j, i, _c=_cb(cb): (i, _c(j))))
            out_shapes.append(jax.ShapeDtypeStruct((s, wt), dt))
            okinds.append('row')
        elif kind == 'hm':
            _, hh, d, dt = ot
            out_specs.append(pl.BlockSpec((hh, tm, d), lambda j, i: (0, i, 0)))
            out_shapes.append(jax.ShapeDtypeStruct((hh, s, d), dt))
            okinds.append('hm')
        elif kind == 'acc':
            shape = ot[1]
            out_specs.append(pl.BlockSpec(shape, lambda j, i, _n=len(shape): (0,) * _n))
            out_shapes.append(jax.ShapeDtypeStruct(shape, F32))
            okinds.append('acc')
        elif kind == 'colacc':
            _, r, wt, w, cb = ot
            out_specs.append(pl.BlockSpec((r, w), lambda j, i, _c=_cb(cb): (0, _c(j))))
            out_shapes.append(jax.ShapeDtypeStruct((r, wt), F32))
            okinds.append('colacc')
        else:
            raise ValueError(kind)
    n_in = len(in_specs)

    def body(*refs):
        j = pl.program_id(0)
        i = pl.program_id(1)
        in_refs, out_refs = refs[:n_in], refs[n_in:]
        items, p = [], 0
        for kd in kinds:
            if kd[0] == 'row':
                items.append(in_refs[p][...])
                p += 1
            elif kd[0] == 'halo':
                prev, cen, nxt = in_refs[p][...], in_refs[p + 1][...], in_refs[p + 2][...]
                prev = jnp.where(i == 0, jnp.zeros_like(prev), prev)
                nxt = jnp.where(i == nrow - 1, jnp.zeros_like(nxt), nxt)
                items.append(jnp.concatenate([prev, cen, nxt], axis=0))
                p += 3
            else:
                items.append(in_refs[p])
                p += 1
        for o_ref, kd in zip(out_refs, okinds):
            if kd == 'acc':
                @pl.when((i == 0) & (j == 0))
                def _(o_ref=o_ref):
                    o_ref[...] = jnp.zeros_like(o_ref)
            elif kd == 'colacc':
                @pl.when(i == 0)
                def _(o_ref=o_ref):
                    o_ref[...] = jnp.zeros_like(o_ref)
        fn(i, j, *items, *out_refs)

    res = pl.pallas_call(
        body, name=name, grid=(ncol, nrow), in_specs=in_specs, out_specs=out_specs, out_shape=out_shapes,
        compiler_params=_cparams(("arbitrary", "arbitrary")),
    )(*args)
    return res


def _sigmoid(x):
    return 1.0 / (1.0 + jnp.exp(-x))


def _ln_stats(x):
    mu = jnp.mean(x, axis=1, keepdims=True)
    xc = x - mu
    var = jnp.mean(xc * xc, axis=1, keepdims=True)
    r = lax.rsqrt(var + LN_EPS)
    return xc * r, r


def _ln_bwd(dy, xh, r, g):
    dxh = dy * g
    dx = r * (dxh - jnp.mean(dxh, axis=1, keepdims=True) - xh * jnp.mean(dxh * xh, axis=1, keepdims=True))
    return dx, jnp.sum(dy * xh, axis=0, keepdims=True), jnp.sum(dy, axis=0, keepdims=True)


def _gmean(v, gm):
    hi = v.astype(BF16)
    lo = (v - hi.astype(F32)).astype(BF16)
    return jnp.dot(hi, gm, preferred_element_type=F32) + jnp.dot(lo, gm, preferred_element_type=F32)


def _roll_l(x, sh):
    return pltpu.roll(x, sh % x.shape[1], 1)


def _rope(x, c, sa, sb, sh):
    return x * c + _roll_l(x, -sh) * sa + _roll_l(x, sh) * sb


def _rope_t(dy, c, sa, sb, sh):
    return dy * c + _roll_l(dy * sa, sh) + _roll_l(dy * sb, -sh)


def _roll_r(x, s):
    return pltpu.roll(x, (-s) % x.shape[0], 0)


_GELU_C = math.sqrt(2.0 / math.pi)


def _gelu(x):
    t = jnp.tanh(_GELU_C * (x + 0.044715 * x * x * x))
    return 0.5 * x * (1.0 + t), t


def _gelu_grad(x, t):
    return 0.5 * (1.0 + t) + 0.5 * x * (1.0 - t * t) * _GELU_C * (1.0 + 3.0 * 0.044715 * x * x)


def _dot_nt(a, b):
    return lax.dot_general(a, b, (((1,), (1,)), ((), ())), preferred_element_type=F32)


def _dot_tn(a, b):
    return lax.dot_general(a, b, (((0,), (0,)), ((), ())), preferred_element_type=F32)


def _rope_block(s, d):
    t = jnp.arange(s)
    row = (t // GRID_W).astype(F32)
    col = (t % GRID_W).astype(F32)
    half = d // 4
    inv = ROPE_THETA ** (-jnp.arange(half, dtype=F32) / half)
    z = jnp.zeros((s, half), F32)
    cs, sas, sbs = [], [], []
    for pos in (row, col):
        ang = pos[:, None] * inv[None, :]
        co, si = jnp.cos(ang), jnp.sin(ang)
        cs += [co, co]
        sas += [-si, z]
        sbs += [z, si]
    return tuple(jnp.concatenate(v, axis=1) for v in (cs, sas, sbs))


def _rope_tables(s):
    gqa = tuple(jnp.tile(a, (1, 4)) for a in _rope_block(s, HEAD_DIM))
    c32, sa32, sb32 = _rope_block(s, MLA_ROPE)

    def head128(c, fill):
        return jnp.concatenate([jnp.full((s, 64), fill, F32), c, jnp.zeros((s, 32), F32)], axis=1)

    mla = tuple(jnp.tile(a, (1, 4)) for a in (head128(c32, 1.0), head128(sa32, 0.0), head128(sb32, 0.0)))
    kr = (head128(c32, 0.0), head128(sa32, 0.0), head128(sb32, 0.0))
    return dict(gqa=gqa, mla=mla, kr=kr)


def _group_mean_matrix(w, g):
    idx = np.arange(w) // g
    return jnp.asarray((idx[:, None] == idx[None, :]).astype(np.float32) / g, dtype=BF16)


TM = 512


def ln_fwd(name, s, x, g, b, res=None, alpha=1.0):
    ins = [('row', x, D_MODEL, 0)] + ([('row', res, D_MODEL, 0)] if res is not None else []) + [('full', g), ('full', b)]

    def fn(i, j, *a):
        if res is not None:
            xv, rv, g_ref, b_ref, o_ref = a
            z = alpha * rv + xv
        else:
            xv, g_ref, b_ref, o_ref = a
            z = xv
        xh, _ = _ln_stats(z)
        o_ref[...] = xh * g_ref[...] + b_ref[...]

    return tiled_call(name, fn, s, TM, ins, [('row', D_MODEL, D_MODEL, 0, F32)])[0]


def ln_bwd(name, s, dy, x, g, res=None, alpha=1.0):
    ins = [('row', dy, D_MODEL, 0), ('row', x, D_MODEL, 0)]
    ins += ([('row', res, D_MODEL, 0)] if res is not None else []) + [('full', g)]

    def fn(i, j, *a):
        a = list(a)
        dyv = a.pop(0)
        xv = a.pop(0)
        z = alpha * a.pop(0) + xv if res is not None else xv
        g_ref, dz_ref, dg_ref, db_ref = a
        xh, r = _ln_stats(z)
        dz, dg, db = _ln_bwd(dyv, xh, r, g_ref[...])
        dz_ref[...] = dz
        dg_ref[...] += dg
        db_ref[...] += db

    return tiled_call(name, fn, s, TM, ins, [('row', D_MODEL, D_MODEL, 0, F32), ('acc', (1, D_MODEL)), ('acc', (1, D_MODEL))])


def loss_and_ln_bwd(name, s, x, res, g, b, tgt, alpha):
    ins = [('row', x, D_MODEL, 0), ('row', res, D_MODEL, 0), ('row', tgt, D_MODEL, 0), ('full', g), ('full', b)]

    def fn(i, j, xv, rv, tv, g_ref, b_ref, dz_ref, dg_ref, db_ref, loss_ref):
        xh, r = _ln_stats(alpha * rv + xv)
        y = xh * g_ref[...] + b_ref[...]
        e = y - tv
        dz, dg, db = _ln_bwd(e * (1.0 / D_MODEL), xh, r, g_ref[...])
        dz_ref[...] = dz
        dg_ref[...] += dg
        db_ref[...] += db
        loss_ref[...] += jnp.sum(jnp.sum(e * e, axis=0, keepdims=True), axis=1, keepdims=True) * (0.5 / D_MODEL)

    return tiled_call(name, fn, s, TM, ins, [('row', D_MODEL, D_MODEL, 0, F32), ('acc', (1, D_MODEL)), ('acc', (1, D_MODEL)),
                                             ('acc', (1, 128))])


HALO_A = 16


def _glu(a_in):
    return a_in[:, :GROUP_W] * _sigmoid(a_in[:, GROUP_W:])


def a_fwd(name, s, proj, cw, cb, g, b):
    def fn(i, j, a_ext, cw_ref, cb_ref, g_ref, b_ref, y_ref, o_ref):
        a = _glu(a_ext)
        acc = jnp.zeros((TM, GROUP_W), F32)
        for k in range(CONV_A_WIDTH):
            acc = acc + _roll_r(a, k - 15)[HALO_A:HALO_A + TM] * cw_ref[pl.ds(k, 1), :]
        y = acc + cb_ref[...]
        y_ref[...] = y
        xh, _ = _ln_stats(y)
        z = xh * g_ref[...] + b_ref[...]
        o_ref[...] = z * _sigmoid(z)

    ins = [('halo', proj, 2 * GROUP_W, 0, HALO_A), ('full', cw), ('full', cb), ('full', g), ('full', b)]
    return tiled_call(name, fn, s, TM, ins, [('row', GROUP_W, GROUP_W, 0, F32), ('row', GROUP_W, GROUP_W, 0, F32)])


def a_bwd1(name, s, y, dmix, g, b):
    def fn(i, j, yv, do, g_ref, b_ref, dy_ref, dg_ref, db_ref, dcb_ref):
        xh, r = _ln_stats(yv)
        z = xh * g_ref[...] + b_ref[...]
        sg = _sigmoid(z)
        dz = do * sg * (1.0 + z * (1.0 - sg))
        dy, dg, db = _ln_bwd(dz, xh, r, g_ref[...])
        dy_ref[...] = dy
        dg_ref[...] += dg
        db_ref[...] += db
        dcb_ref[...] += jnp.sum(dy, axis=0, keepdims=True)

    ins = [('row', y, GROUP_W, 0), ('row', dmix, GROUP_W, 0), ('full', g), ('full', b)]
    return tiled_call(name, fn, s, TM, ins, [('row', GROUP_W, GROUP_W, 0, F32), ('acc', (1, GROUP_W)), ('acc', (1, GROUP_W)),
                                             ('acc', (1, GROUP_W))])


def a_bwd2(name, s, proj, dy, cw):
    def fn(i, j, a_ext, dy_ext, cw_ref, da_ref, dcw_ref):
        a1, a2 = a_ext[:, :GROUP_W], a_ext[:, GROUP_W:]
        sg = _sigmoid(a2)
        a = a1 * sg
        dyc = dy_ext[HALO_A:HALO_A + TM]
        da = jnp.zeros((TM, GROUP_W), F32)
        for k in range(CONV_A_WIDTH):
            da = da + _roll_r(dy_ext, 15 - k)[HALO_A:HALO_A + TM] * cw_ref[pl.ds(k, 1), :]
            dcw_ref[pl.ds(k, 1), :] += jnp.sum(dyc * _roll_r(a, k - 15)[HALO_A:HALO_A + TM], axis=0, keepdims=True)
        a1c, sgc = a1[HALO_A:HALO_A + TM], sg[HALO_A:HALO_A + TM]
        da_ref[...] = jnp.concatenate([da * sgc, da * a1c * sgc * (1.0 - sgc)], axis=1).astype(BF16)

    ins = [('halo', proj, 2 * GROUP_W, 0, HALO_A), ('halo', dy, GROUP_W, 0, HALO_A), ('full', cw)]
    return tiled_call(name, fn, s, TM, ins, [('row', 2 * GROUP_W, 2 * GROUP_W, 0, BF16), ('acc', (32, GROUP_W))])


def b_prep_fwd(name, s, proj, gq, gk, tabs, gm):
    c, sa, sb = tabs

    def fn(i, j, qkv, cv, sav, sbv, gq_ref, gk_ref, gm_ref, q_ref, k_ref, v_ref):
        q, k, v = qkv[:, :256], qkv[:, 256:384], qkv[:, 384:]
        gmv = gm_ref[...]
        qn = q * lax.rsqrt(_gmean(q * q, gmv) + RMS_EPS) * gq_ref[...]
        kn = k * lax.rsqrt(_gmean(k * k, gmv[:128, :128]) + RMS_EPS) * gk_ref[...]
        qr = _rope(qn, cv, sav, sbv, 16)
        kr = _rope(kn, cv[:, :128], sav[:, :128], sbv[:, :128], 16)
        for h in range(4):
            q_ref[h] = qr[:, h * 64:(h + 1) * 64].astype(BF16)
        for h in range(2):
            k_ref[h] = kr[:, h * 64:(h + 1) * 64].astype(BF16)
            v_ref[h] = v[:, h * 64:(h + 1) * 64].astype(BF16)

    ins = [('row', proj, 512, 1), ('row', c, 256, 0), ('row', sa, 256, 0), ('row', sb, 256, 0), ('full', gq), ('full', gk),
           ('full', gm)]
    return tiled_call(name, fn, s, TM, ins, [('hm', 4, 64, BF16), ('hm', 2, 64, BF16), ('hm', 2, 64, BF16)])


def b_prep_bwd(name, s, proj, gq, gk, tabs, gm, dq_hm, dk_hm, dv_hm):
    c, sa, sb = tabs

    def fn(i, j, qkv, cv, sav, sbv, gq_ref, gk_ref, gm_ref, dq_ref, dk_ref, dv_ref, dp_ref, dgq_ref, dgk_ref):
        q, k = qkv[:, :256], qkv[:, 256:384]
        gmv = gm_ref[...]
        dqr = jnp.concatenate([dq_ref[h] for h in range(4)], axis=1)
        dkr = jnp.concatenate([dk_ref[h] for h in range(2)], axis=1)
        dv = jnp.concatenate([dv_ref[h] for h in range(2)], axis=1)
        outs = []
        for x, dxr, g_ref, gmx, w, dg_ref in ((q, dqr, gq_ref, gmv, 256, dgq_ref), (k, dkr, gk_ref, gmv[:128, :128], 128, dgk_ref)):
            dn = _rope_t(dxr, cv[:, :w], sav[:, :w], sbv[:, :w], 16)
            r = lax.rsqrt(_gmean(x * x, gmx) + RMS_EPS)
            gv = g_ref[...]
            dx = gv * r * dn - x * (r * r * r) * _gmean(x * gv * dn, gmx)
            dgt = jnp.sum(x * r * dn, axis=0, keepdims=True)
            dg = dgt[:, 0:64]
            for h in range(1, w // 64):
                dg = dg + dgt[:, h * 64:(h + 1) * 64]
            dg_ref[...] += dg
            outs.append(dx)
        dp_ref[...] = jnp.concatenate(outs + [dv], axis=1).astype(BF16)

    ins = [('row', proj, 512, 1), ('row', c, 256, 0), ('row', sa, 256, 0), ('row', sb, 256, 0), ('full', gq), ('full', gk),
           ('full', gm), ('hm', dq_hm), ('hm', dk_hm), ('hm', dv_hm)]
    return tiled_call(name, fn, s, TM, ins, [('row', 512, 512, 0, BF16), ('acc', (1, 64)), ('acc', (1, 64))])


TQ = 256


def attn_fwd(name, s, q_hm, k_hm, v_hm, scale):
    dk = q_hm.shape[2]
    kpb = k_hm.shape[0] // 2

    def body(q_ref, k_ref, v_ref, o_ref, l_ref):
        outs = []
        for g in range(2):
            kv = g if kpb == 2 else 0
            sc = _dot_nt(q_ref[g], k_ref[kv]) * scale
            m = jnp.max(sc, axis=1, keepdims=True)
            p = jnp.exp(sc - m)
            l = jnp.sum(p, axis=1, keepdims=True)
            o = jnp.dot(p.astype(BF16), v_ref[kv], preferred_element_type=F32)
            outs.append(o / l)
            l_ref[g] = m + jnp.log(l)
        o_ref[...] = jnp.concatenate(outs, axis=1)

    return pl.pallas_call(
        body, name=name, grid=(2, s // TQ),
        in_specs=[pl.BlockSpec((2, TQ, dk), lambda p, i: (p, i, 0)),
                  pl.BlockSpec((kpb, s, dk), lambda p, i: (p, 0, 0)),
                  pl.BlockSpec((kpb, s, 64), lambda p, i: (p, 0, 0))],
        out_specs=[pl.BlockSpec((TQ, 128), lambda p, i: (i, p)), pl.BlockSpec((2, TQ, 1), lambda p, i: (p, i, 0))],
        out_shape=[jax.ShapeDtypeStruct((s, 256), F32), jax.ShapeDtypeStruct((4, s, 1), F32)],
        compiler_params=_cparams(("arbitrary", "arbitrary")),
    )(q_hm, k_hm, v_hm)


def attn_bwd(name, s, q_hm, k_hm, v_hm, o, lse, dmixcat, dcol, scale):
    dk = q_hm.shape[2]
    nkv = k_hm.shape[0]
    shared = nkv == 2
    kvi = (lambda p, g, i: (p, 0, 0)) if shared else (lambda p, g, i: (2 * p + g, 0, 0))

    def body(q_ref, k_ref, v_ref, o_ref, l_ref, do_ref, dq_ref, dk_ref, dv_ref):
        g = pl.program_id(1)
        i = pl.program_id(2)
        first = (i == 0) & (g == 0) if shared else (i == 0)

        @pl.when(first)
        def _():
            dk_ref[...] = jnp.zeros_like(dk_ref)
            dv_ref[...] = jnp.zeros_like(dv_ref)

        dob, ob = do_ref[...], o_ref[...]
        do = jnp.where(g == 0, dob[:, :64], dob[:, 64:])
        ov = jnp.where(g == 0, ob[:, :64], ob[:, 64:])
        q, k, v = q_ref[0], k_ref[0], v_ref[0]
        sc = _dot_nt(q, k) * scale
        p = jnp.exp(sc - l_ref[0])
        dp = _dot_nt(do.astype(BF16), v)
        delta = jnp.sum(do * ov, axis=1, keepdims=True)
        ds = (p * (dp - delta) * scale).astype(BF16)
        dq_ref[0] = jnp.dot(ds, k, preferred_element_type=F32)
        dk_ref[0] += _dot_tn(ds, q)
        dv_ref[0] += _dot_tn(p.astype(BF16), do.astype(BF16))

    return pl.pallas_call(
        body, name=name, grid=(2, 2, s // TQ),
        in_specs=[pl.BlockSpec((1, TQ, dk), lambda p, g, i: (2 * p + g, i, 0)),
                  pl.BlockSpec((1, s, dk), kvi), pl.BlockSpec((1, s, 64), kvi),
                  pl.BlockSpec((TQ, 128), lambda p, g, i: (i, p)),
                  pl.BlockSpec((1, TQ, 1), lambda p, g, i: (2 * p + g, i, 0)),
                  pl.BlockSpec((TQ, 128), lambda p, g, i: (i, dcol + p))],
        out_specs=[pl.BlockSpec((1, TQ, dk), lambda p, g, i: (2 * p + g, i, 0)),
                   pl.BlockSpec((1, s, dk), kvi), pl.BlockSpec((1, s, 64), kvi)],
        out_shape=[jax.ShapeDtypeStruct((4, s, dk), F32), jax.ShapeDtypeStruct((nkv, s, dk), F32),
                   jax.ShapeDtypeStruct((nkv, s, 64), F32)],
        compiler_params=_cparams(("arbitrary", "arbitrary", "arbitrary")),
    )(q_hm, k_hm, v_hm, o, lse, dmixcat)


def _lane_group_masks(w, g):
    lane = lax.broadcasted_iota(jnp.int32, (1, w), 1)
    return [((lane >= k * g) & (lane < (k + 1) * g)).astype(F32) for k in range(w // g)]


def _sgu_gate(svn, w_ref, bias):
    masks = _lane_group_masks(GROUP_W, 64)
    outs = []
    for n in range(TM // CHUNK):
        x = svn[n * CHUNK:(n + 1) * CHUNK]
        acc = bias
        for g in range(4):
            acc = acc + jnp.dot(w_ref[g].astype(BF16), (x * masks[g]).astype(BF16), preferred_element_type=F32)
        outs.append(acc)
    return jnp.concatenate(outs, axis=0)


def c_fwd(name, s, proj, g, b, w, bias):
    def fn(i, j, cin, g_ref, b_ref, w_ref, bias_ref, o_ref):
        c, _ = _gelu(cin)
        xh, _ = _ln_stats(c[:, GROUP_W:])
        svn = xh * g_ref[...] + b_ref[...]
        o_ref[...] = c[:, :GROUP_W] * _sgu_gate(svn, w_ref, bias_ref[...])

    ins = [('row', proj, 512, 2), ('full', g), ('full', b), ('full', w), ('full', bias)]
    return tiled_call(name, fn, s, TM, ins, [('row', GROUP_W, GROUP_W, 0, F32)])[0]


def c_bwd(name, s, proj, g, b, w, wt, bias, dmixcat):
    def fn(i, j, cin, do, g_ref, b_ref, w_ref, wt_ref, bias_ref, dc_ref, dg_ref, db_ref, dw_ref, dbias_ref):
        c, t = _gelu(cin)
        u = c[:, :GROUP_W]
        xh, r = _ln_stats(c[:, GROUP_W:])
        svn = xh * g_ref[...] + b_ref[...]
        gate = _sgu_gate(svn, w_ref, bias_ref[...])
        du = do * gate
        dgate = do * u
        masks = _lane_group_masks(GROUP_W, 64)
        col = lax.broadcasted_iota(jnp.int32, (CHUNK, 128), 1)
        dsvn, dbias = [], jnp.zeros((CHUNK, 128), F32)
        for n in range(TM // CHUNK):
            dgc = dgate[n * CHUNK:(n + 1) * CHUNK]
            x = svn[n * CHUNK:(n + 1) * CHUNK]
            acc = jnp.zeros((CHUNK, GROUP_W), F32)
            for gi in range(4):
                dgm = (dgc * masks[gi]).astype(BF16)
                acc = acc + jnp.dot(wt_ref[gi].astype(BF16), dgm, preferred_element_type=F32)
                dw_ref[gi] += _dot_nt(dgm, (x * masks[gi]).astype(BF16))
                dbias = dbias + jnp.where(col == gi, jnp.sum(dgc * masks[gi], axis=1, keepdims=True), 0.0)
            dsvn.append(acc)
        dbias_ref[...] += dbias
        dsv, dg, db = _ln_bwd(jnp.concatenate(dsvn, axis=0), xh, r, g_ref[...])
        dg_ref[...] += dg
        db_ref[...] += db
        dc_ref[...] = (jnp.concatenate([du, dsv], axis=1) * _gelu_grad(cin, t)).astype(BF16)

    ins = [('row', proj, 512, 2), ('row', dmixcat, GROUP_W, 2), ('full', g), ('full', b), ('full', w), ('full', wt), ('full', bias)]
    return tiled_call(name, fn, s, TM, ins, [('row', 512, 512, 0, BF16), ('acc', (1, GROUP_W)), ('acc', (1, GROUP_W)),
                                             ('acc', (4, CHUNK, CHUNK)), ('acc', (CHUNK, 128))])


def _d_common(dpart, gq_ref, gkv_ref):
    cq, ckv, kr = dpart[:, :256], dpart[:, 256:384], dpart[:, 384:]
    rq = lax.rsqrt(jnp.sum(cq * cq, axis=1, keepdims=True) * (1.0 / MLA_Q_LORA) + RMS_EPS)
    rkv = lax.rsqrt(jnp.mean(ckv * ckv, axis=1, keepdims=True) + RMS_EPS)
    return cq, ckv, kr, rq, rkv, cq * rq * gq_ref[...], ckv * rkv * gkv_ref[...]


def d_prep_fwd(name, s, proj, gq, gkv, wuq, wukv, tm_, tkr):
    cm, sam, sbm = tm_
    ck, sak, sbk = tkr

    def fn(i, j, dpart, cmv, samv, sbmv, ckv_, sakv, sbkv, gq_ref, gkv_ref, wuq_ref, wukv_ref, q_ref, k_ref, v_ref):
        cq, ckv, kr, rq, rkv, cqn, ckvn = _d_common(dpart, gq_ref, gkv_ref)
        qd = jnp.dot(cqn.astype(BF16), wuq_ref[...], preferred_element_type=F32)
        qf = _rope(qd, cmv, samv, sbmv, 8)
        kvd = jnp.dot(ckvn.astype(BF16), wukv_ref[...], preferred_element_type=F32)
        krr = _rope(kr, ckv_, sakv, sbkv, 8)
        for h in range(4):
            q_ref[h] = qf[:, h * 128:(h + 1) * 128].astype(BF16)
            k_ref[h] = (kvd[:, h * 128:(h + 1) * 128] + krr).astype(BF16)
            v_ref[h] = kvd[:, 512 + h * 64:512 + (h + 1) * 64].astype(BF16)

    ins = [('row', proj, 512, 3), ('row', cm, 512, 0), ('row', sam, 512, 0), ('row', sbm, 512, 0), ('row', ck, 128, 0),
           ('row', sak, 128, 0), ('row', sbk, 128, 0), ('full', gq), ('full', gkv), ('full', wuq), ('full', wukv)]
    return tiled_call(name, fn, s, TM, ins, [('hm', 4, 128, BF16), ('hm', 4, 128, BF16), ('hm', 4, 64, BF16)])


def d_prep_bwd(name, s, proj, gq, gkv, wuq, wukv, tm_, tkr, dq_hm, dk_hm, dv_hm):
    cm, sam, sbm = tm_
    ck, sak, sbk = tkr

    def fn(i, j, dpart, cmv, samv, sbmv, ckv_, sakv, sbkv, gq_ref, gkv_ref, wuq_ref, wukv_ref, dq_ref, dk_ref, dv_ref,
           dd_ref, dgq_ref, dgkv_ref, dwuq_ref, dwukv_ref):
        cq, ckv, kr, rq, rkv, cqn, ckvn = _d_common(dpart, gq_ref, gkv_ref)
        dqf = jnp.concatenate([dq_ref[h] for h in range(4)], axis=1)
        dkf = [dk_ref[h] for h in range(4)]
        dqd = _rope_t(dqf, cmv, samv, sbmv, 8).astype(BF16)
        dkvd = jnp.concatenate(dkf + [dv_ref[h] for h in range(4)], axis=1).astype(BF16)
        dkr = _rope_t(dkf[0] + dkf[1] + dkf[2] + dkf[3], ckv_, sakv, sbkv, 8)
        dcqn = _dot_nt(dqd, wuq_ref[...])
        dckvn = _dot_nt(dkvd, wukv_ref[...])
        dwuq_ref[...] += _dot_tn(cqn.astype(BF16), dqd)
        dwukv_ref[...] += _dot_tn(ckvn.astype(BF16), dkvd)
        gqv, gkvv = gq_ref[...], gkv_ref[...]
        dcq = gqv * rq * dcqn - cq * (rq * rq * rq) * (jnp.sum(cq * gqv * dcqn, axis=1, keepdims=True) * (1.0 / MLA_Q_LORA))
        dckv = gkvv * rkv * dckvn - ckv * (rkv * rkv * rkv) * jnp.mean(ckv * gkvv * dckvn, axis=1, keepdims=True)
        dgq_ref[...] += jnp.sum(cq * rq * dcqn, axis=0, keepdims=True)
        dgkv_ref[...] += jnp.sum(ckv * rkv * dckvn, axis=0, keepdims=True)
        dd_ref[...] = jnp.concatenate([dcq, dckv, dkr], axis=1).astype(BF16)

    ins = [('row', proj, 512, 3), ('row', cm, 512, 0), ('row', sam, 512, 0), ('row', sbm, 512, 0), ('row', ck, 128, 0),
           ('row', sak, 128, 0), ('row', sbk, 128, 0), ('full', gq), ('full', gkv), ('full', wuq), ('full', wukv),
           ('hm', dq_hm), ('hm', dk_hm), ('hm', dv_hm)]
    return tiled_call(name, fn, s, TM, ins, [('row', 512, 512, 0, BF16), ('acc', (1, 256)), ('acc', (1, 128)),
                                             ('acc', (256, 512)), ('acc', (128, 768))])


HALO_F = 8
TN_F = 256
TM_F = 512


def _conv3(ext, w_ref):
    return _roll_r(ext, -1) * w_ref[pl.ds(0, 1), :] + ext * w_ref[pl.ds(1, 1), :] + _roll_r(ext, 1) * w_ref[pl.ds(2, 1), :]


def f_fwd(name, s, upraw, cw, cb):
    ncj = D_FF // TN_F

    def fn(i, j, xa, xg, wa_ref, wg_ref, ba_ref, bg_ref, o_ref):
        ua = (_conv3(xa, wa_ref) + ba_ref[...])[HALO_F:HALO_F + TM_F]
        ug = (_conv3(xg, wg_ref) + bg_ref[...])[HALO_F:HALO_F + TM_F]
        o_ref[...] = (ua * _sigmoid(ua) * ug).astype(BF16)

    ins = [('halo', upraw, TN_F, lambda j: j, HALO_F), ('halo', upraw, TN_F, lambda j: j + ncj, HALO_F),
           ('col', cw, TN_F, lambda j: j), ('col', cw, TN_F, lambda j: j + ncj),
           ('col', cb, TN_F, lambda j: j), ('col', cb, TN_F, lambda j: j + ncj)]
    return tiled_call(name, fn, s, TM_F, ins, [('row', D_FF, TN_F, lambda j: j, BF16)], ncol=ncj)[0]


def f_bwd(name, s, upraw, dact, cw, cb):
    ncj = D_FF // TN_F

    def fn(i, j, xa, xg, da, wa_ref, wg_ref, ba_ref, bg_ref, dxa_ref, dxg_ref, dwa_ref, dwg_ref, dba_ref, dbg_ref):
        ua = _conv3(xa, wa_ref) + ba_ref[...]
        ug = _conv3(xg, wg_ref) + bg_ref[...]
        sg = _sigmoid(ua)
        dug = da * ua * sg
        dua = da * ug * sg * (1.0 + ua * (1.0 - sg))
        cen = slice(HALO_F, HALO_F + TM_F)
        for du, x, w_ref, dx_ref, dw_ref, db_ref in ((dua, xa, wa_ref, dxa_ref, dwa_ref, dba_ref),
                                                      (dug, xg, wg_ref, dxg_ref, dwg_ref, dbg_ref)):
            dx = _roll_r(du, 1) * w_ref[pl.ds(0, 1), :] + du * w_ref[pl.ds(1, 1), :] + _roll_r(du, -1) * w_ref[pl.ds(2, 1), :]
            dx_ref[...] = dx[cen].astype(BF16)
            duc = du[cen]
            for k in range(3):
                dw_ref[pl.ds(k, 1), :] += jnp.sum(duc * _roll_r(x, k - 1)[cen], axis=0, keepdims=True)
            db_ref[...] += jnp.sum(duc, axis=0, keepdims=True)

    ins = [('halo', upraw, TN_F, lambda j: j, HALO_F), ('halo', upraw, TN_F, lambda j: j + ncj, HALO_F),
           ('halo', dact, TN_F, lambda j: j, HALO_F),
           ('col', cw, TN_F, lambda j: j), ('col', cw, TN_F, lambda j: j + ncj),
           ('col', cb, TN_F, lambda j: j), ('col', cb, TN_F, lambda j: j + ncj)]
    outs = [('row', D_FF, TN_F, lambda j: j, BF16), ('row', D_FF, TN_F, lambda j: j, BF16),
            ('colacc', 8, D_FF, TN_F, lambda j: j), ('colacc', 8, D_FF, TN_F, lambda j: j),
            ('colacc', 1, D_FF, TN_F, lambda j: j), ('colacc', 1, D_FF, TN_F, lambda j: j)]
    return tiled_call(name, fn, s, TM_F, ins, outs, ncol=ncj)


def _pad_w_in(w):
    z = lambda n: jnp.zeros((w.shape[0], n), w.dtype)
    return jnp.concatenate([w[:, :1728], z(64), w[:, 1728:1856], z(64), w[:, 1856:1888], z(32)], axis=1)


def _unpad_w_in(g):
    return jnp.concatenate([g[:, :1728], g[:, 1792:1920], g[:, 1984:2016]], axis=1)


def _pad_uq(w):
    w = w.reshape(MLA_Q_LORA, 4, 96)
    w = jnp.concatenate([w, jnp.zeros((MLA_Q_LORA, 4, 32), w.dtype)], axis=2).reshape(MLA_Q_LORA, 512)
    return jnp.concatenate([w, jnp.zeros((64, 512), w.dtype)], axis=0)


def _unpad_uq(g):
    return g[:MLA_Q_LORA].reshape(MLA_Q_LORA, 4, 128)[:, :, :96].reshape(MLA_Q_LORA, 384)


def _pad_ukv(w):
    w = w.reshape(MLA_KV_LORA, 4, 128)
    kn = jnp.concatenate([w[:, :, :64], jnp.zeros((MLA_KV_LORA, 4, 64), w.dtype)], axis=2).reshape(MLA_KV_LORA, 512)
    return jnp.concatenate([kn, w[:, :, 64:].reshape(MLA_KV_LORA, 256)], axis=1)


def _unpad_ukv(g):
    kn = g[:, :512].reshape(MLA_KV_LORA, 4, 128)[:, :, :64]
    v = g[:, 512:].reshape(MLA_KV_LORA, 4, 64)
    return jnp.concatenate([kn, v], axis=2).reshape(MLA_KV_LORA, 512)


def _row(v, pad_to=None):
    v = v.reshape(1, -1)
    if pad_to is not None and v.shape[1] < pad_to:
        v = jnp.concatenate([v, jnp.zeros((1, pad_to - v.shape[1]), v.dtype)], axis=1)
    return v


def local_step(x, tgt, W):
    s = x.shape[0]
    tabs = _rope_tables(s)
    gm = _group_mean_matrix(256, 64)
    alpha = DEEPNORM_ALPHA
    G = {}
    acts = []
    h = ln_fwd("ln_in", s, x, _row(W['ln_in_g']), _row(W['ln_in_b']))
    h0 = h
    for l in range(DEPTH):
        A = dict(h=h)
        A['wpad'] = _pad_w_in(W['w_in'][l]).astype(BF16)
        A['wout'] = W['w_out'][l].astype(BF16)
        A['wup'] = W['ffn_w_up'][l].astype(BF16)
        A['wdown'] = W['ffn_w_down'][l].astype(BF16)
        A['wuq'] = _pad_uq(W['mla_w_uq'][l]).astype(BF16)
        A['wukv'] = _pad_ukv(W['mla_w_ukv'][l]).astype(BF16)
        A['cw'] = jnp.concatenate([W['conv_a_w'][l], jnp.zeros((1, GROUP_W), F32)], axis=0)
        A['fcw'] = jnp.concatenate([W['ffn_conv_w'][l], jnp.zeros((5, 2 * D_FF), F32)], axis=0)
        A['gq'] = jnp.tile(_row(W['qk_norm_q'][l]), (1, 4))
        A['gk'] = jnp.tile(_row(W['qk_norm_k'][l]), (1, 2))
        A['sgu_bias'] = jnp.repeat(W['sgu_b'][l].T, 64, axis=1)
        A['sgu_wt'] = jnp.swapaxes(W['sgu_w'][l], 1, 2)
        proj = mm(f"proj{l}", h, A['wpad'])
        A['proj'] = proj
        A['y_a'], o_a = a_fwd(f"a_fwd{l}", s, proj, A['cw'], _row(W['conv_a_b'][l]), _row(W['ln_a_g'][l]), _row(W['ln_a_b'][l]))
        A['bq'], A['bk'], A['bv'] = b_prep_fwd(f"b_prep{l}", s, proj, A['gq'], A['gk'], tabs['gqa'], gm)
        A['o_b'], A['lse_b'] = attn_fwd(f"b_attn{l}", s, A['bq'], A['bk'], A['bv'], HEAD_DIM ** -0.5)
        o_c = c_fwd(f"c_fwd{l}", s, proj, _row(W['sgu_ln_g'][l]), _row(W['sgu_ln_b'][l]), W['sgu_w'][l], A['sgu_bias'])
        A['dq'], A['dk'], A['dv'] = d_prep_fwd(f"d_prep{l}", s, proj, _row(W['mla_q_norm'][l], 256), _row(W['mla_kv_norm'][l]),
                                               A['wuq'], A['wukv'], tabs['mla'], tabs['kr'])
        A['o_d'], A['lse_d'] = attn_fwd(f"d_attn{l}", s, A['dq'], A['dk'], A['dv'], (MLA_NOPE + MLA_ROPE) ** -0.5)
        A['mixcat'] = jnp.concatenate([o_a, A['o_b'], o_c, A['o_d']], axis=1).astype(BF16)
        A['mix'] = mm(f"out_proj{l}", A['mixcat'], A['wout'])
        h1 = ln_fwd(f"ln_mix{l}", s, A['mix'], _row(W['ln_mix_g'][l]), _row(W['ln_mix_b'][l]), res=h, alpha=alpha)
        A['h1'] = h1
        A['upraw'] = mm(f"ffn_up{l}", h1, A['wup'])
        A['act'] = f_fwd(f"f_fwd{l}", s, A['upraw'], A['fcw'], _row(W['ffn_conv_b'][l]))
        A['f'] = mm(f"ffn_down{l}", A['act'], A['wdown'])
        if l < DEPTH - 1:
            h = ln_fwd(f"ln_ffn{l}", s, A['f'], _row(W['ln_ffn_g'][l]), _row(W['ln_ffn_b'][l]), res=h1, alpha=alpha)
        acts.append(A)

    per_layer = {n: [None] * DEPTH for n in WEIGHT_NAMES if n not in ('ln_in_g', 'ln_in_b')}
    dh = None
    loss_row = None
    for l in reversed(range(DEPTH)):
        A = acts[l]
        g_ffn, b_ffn = _row(W['ln_ffn_g'][l]), _row(W['ln_ffn_b'][l])
        if l == DEPTH - 1:
            dz, dg, db, loss_row = loss_and_ln_bwd("loss", s, A['f'], A['h1'], g_ffn, b_ffn, tgt, alpha)
        else:
            dz, dg, db = ln_bwd(f"ln_ffn_bwd{l}", s, dh, A['f'], g_ffn, res=A['h1'], alpha=alpha)
        per_layer['ln_ffn_g'][l], per_layer['ln_ffn_b'][l] = dg[0], db[0]
        per_layer['ffn_w_down'][l] = mm(f"dw_down{l}", A['act'], dz, ta=True)
        dact = mm(f"d_act{l}", dz, A['wdown'], tb=True)
        dxa, dxg, dwa, dwg, dba, dbg = f_bwd(f"f_bwd{l}", s, A['upraw'], dact, A['fcw'], _row(W['ffn_conv_b'][l]))
        per_layer['ffn_conv_w'][l] = jnp.concatenate([dwa[:3], dwg[:3]], axis=1)
        per_layer['ffn_conv_b'][l] = jnp.concatenate([dba[0], dbg[0]], axis=0)
        dup = jnp.concatenate([dxa, dxg], axis=1)
        per_layer['ffn_w_up'][l] = mm(f"dw_up{l}", A['h1'], dup, ta=True)
        dh1_mm = mm(f"d_h1{l}", dup, A['wup'], tb=True)
        dz1, dg, db = _ln_mix_bwd(l, s, dz, dh1_mm, A, W, alpha)
        per_layer['ln_mix_g'][l], per_layer['ln_mix_b'][l] = dg[0], db[0]
        per_layer['w_out'][l] = mm(f"dw_out{l}", A['mixcat'], dz1, ta=True)
        dmixcat = mm(f"d_mixcat{l}", dz1, A['wout'], tb=True)
        dy_a, dg, db, dcb = a_bwd1(f"a_bwd1{l}", s, A['y_a'], dmixcat, _row(W['ln_a_g'][l]), _row(W['ln_a_b'][l]))
        per_layer['ln_a_g'][l], per_layer['ln_a_b'][l], per_layer['conv_a_b'][l] = dg[0], db[0], dcb[0]
        dp_a, dcw = a_bwd2(f"a_bwd2{l}", s, A['proj'], dy_a, A['cw'])
        per_layer['conv_a_w'][l] = dcw[:CONV_A_WIDTH]
        dq, dk, dv = attn_bwd(f"b_attn_bwd{l}", s, A['bq'], A['bk'], A['bv'], A['o_b'], A['lse_b'], dmixcat, 2, HEAD_DIM ** -0.5)
        dp_b, dgq, dgk = b_prep_bwd(f"b_prep_bwd{l}", s, A['proj'], A['gq'], A['gk'], tabs['gqa'], gm, dq, dk, dv)
        per_layer['qk_norm_q'][l], per_layer['qk_norm_k'][l] = dgq[0], dgk[0]
        dp_c, dg, db, dsw, dsb = c_bwd(f"c_bwd{l}", s, A['proj'], _row(W['sgu_ln_g'][l]), _row(W['sgu_ln_b'][l]), W['sgu_w'][l],
                                       A['sgu_wt'], A['sgu_bias'], dmixcat)
        per_layer['sgu_ln_g'][l], per_layer['sgu_ln_b'][l] = dg[0], db[0]
        per_layer['sgu_w'][l], per_layer['sgu_b'][l] = dsw, dsb[:, :4].T
        dq, dk, dv = attn_bwd(f"d_attn_bwd{l}", s, A['dq'], A['dk'], A['dv'], A['o_d'], A['lse_d'], dmixcat, 6,
                              (MLA_NOPE + MLA_ROPE) ** -0.5)
        dp_d, dgq, dgkv, dwuq, dwukv = d_prep_bwd(f"d_prep_bwd{l}", s, A['proj'], _row(W['mla_q_norm'][l], 256),
                                                  _row(W['mla_kv_norm'][l]), A['wuq'], A['wukv'], tabs['mla'], tabs['kr'],
                                                  dq, dk, dv)
        per_layer['mla_q_norm'][l], per_layer['mla_kv_norm'][l] = dgq[0, :MLA_Q_LORA], dgkv[0]
        per_layer['mla_w_uq'][l], per_layer['mla_w_ukv'][l] = _unpad_uq(dwuq), _unpad_ukv(dwukv)
        dproj = jnp.concatenate([dp_a, dp_b, dp_c, dp_d], axis=1)
        per_layer['w_in'][l] = _unpad_w_in(mm(f"dw_in{l}", A['h'], dproj, ta=True))
        dh = mm(f"d_h{l}", dproj, A['wpad'], tb=True, add=(alpha * dz1))
    dx, dg, db = ln_bwd("ln_in_bwd", s, dh, x, _row(W['ln_in_g']))
    G = {n: jnp.stack(v) for n, v in per_layer.items()}
    G['ln_in_g'], G['ln_in_b'] = dg[0], db[0]
    return loss_row, dx, G


def _ln_mix_bwd(l, s, dz, dh1_mm, A, W, alpha):
    ins = [('row', dz, D_MODEL, 0), ('row', dh1_mm, D_MODEL, 0), ('row', A['mix'], D_MODEL, 0), ('row', A['h'], D_MODEL, 0),
           ('full', _row(W['ln_mix_g'][l]))]

    def fn(i, j, dzv, dmv, xv, rv, g_ref, o_ref, dg_ref, db_ref):
        xh, r = _ln_stats(alpha * rv + xv)
        d, dg, db = _ln_bwd(alpha * dzv + dmv, xh, r, g_ref[...])
        o_ref[...] = d
        dg_ref[...] += dg
        db_ref[...] += db

    return tiled_call(f"ln_mix_bwd{l}", fn, s, TM, ins, [('row', D_MODEL, D_MODEL, 0, F32), ('acc', (1, D_MODEL)),
                                                         ('acc', (1, D_MODEL))])


ANY = pl.BlockSpec(memory_space=pl.ANY)


def _pos():
    return lax.axis_index("x"), lax.axis_index("y"), lax.axis_index("c")


def _rcopy(src, dst, ssem, rsem, dev):
    return pltpu.make_async_remote_copy(src_ref=src, dst_ref=dst, send_sem=ssem, recv_sem=rsem, device_id=dev,
                                        device_id_type=MESH)


def gather_over_chips(name, parts):
    n = len(parts)

    def body(*refs):
        srcs, dsts = refs[:n], refs[n:2 * n]
        ssems, rsems, lsems = refs[2 * n:]
        x, y, c = _pos()
        me = 2 * x + y
        chips = [(1 - x, y), (x, 1 - y), (1 - x, 1 - y)]
        local = [pltpu.make_async_copy(srcs[t], dsts[t].at[me], lsems.at[t]) for t in range(n)]
        for cp in local:
            cp.start()
        sends = []
        for j, (px, py) in enumerate(chips):
            for t in range(n):
                cp = _rcopy(srcs[t], dsts[t].at[me], ssems.at[n * j + t], rsems.at[n * j + t], (px, py, c))
                cp.start()
                sends.append(cp)
        for j, (px, py) in enumerate(chips):
            for t in range(n):
                _rcopy(srcs[t], dsts[t].at[2 * px + py], ssems.at[n * j + t], rsems.at[n * j + t], (px, py, c)).wait_recv()
        for cp in sends:
            cp.wait_send()
        for cp in local:
            cp.wait()

    return pl.pallas_call(
        body, name=name, in_specs=[ANY] * n, out_specs=[ANY] * n,
        out_shape=[jax.ShapeDtypeStruct((N_CHIPS,) + p.shape, p.dtype) for p in parts],
        scratch_shapes=[pltpu.SemaphoreType.DMA((3 * n,)), pltpu.SemaphoreType.DMA((3 * n,)), pltpu.SemaphoreType.DMA((n,))],
    )(*parts)


def swap_with_sibling(name, a):
    def body(a_ref, o_ref, ssem, rsem):
        x, y, c = _pos()
        cp = _rcopy(a_ref, o_ref, ssem, rsem, (x, y, 1 - c))
        cp.start()
        cp.wait()

    return pl.pallas_call(body, name=name, in_specs=[ANY], out_specs=ANY, out_shape=jax.ShapeDtypeStruct(a.shape, a.dtype),
                          scratch_shapes=[pltpu.SemaphoreType.DMA(()), pltpu.SemaphoreType.DMA(())])(a)


def scatter_over_chips(name, p):
    def body(p_ref, o_ref, ssems, rsems, lsem):
        x, y, c = _pos()
        me = 2 * x + y
        chips = [(1 - x, y), (x, 1 - y), (1 - x, 1 - y)]
        loc = pltpu.make_async_copy(p_ref.at[me], o_ref.at[me], lsem)
        loc.start()
        sends = []
        for j, (px, py) in enumerate(chips):
            cp = _rcopy(p_ref.at[2 * px + py], o_ref.at[me], ssems.at[j], rsems.at[j], (px, py, c))
            cp.start()
            sends.append(cp)
        for j, (px, py) in enumerate(chips):
            _rcopy(p_ref.at[me], o_ref.at[2 * px + py], ssems.at[j], rsems.at[j], (px, py, c)).wait_recv()
        for cp in sends:
            cp.wait_send()
        loc.wait()

    return pl.pallas_call(body, name=name, in_specs=[ANY], out_specs=ANY, out_shape=jax.ShapeDtypeStruct(p.shape, p.dtype),
                          scratch_shapes=[pltpu.SemaphoreType.DMA((3,)), pltpu.SemaphoreType.DMA((3,)), pltpu.SemaphoreType.DMA(())])(p)


def gather_over_devices(name, a):
    def body(a_ref, o_ref, ssems, rsems, lsem):
        x, y, c = _pos()
        me = 4 * x + 2 * y + c
        loc = pltpu.make_async_copy(a_ref, o_ref.at[me], lsem)
        loc.start()
        peers = [((x + (k >> 2)) % 2, (y + ((k >> 1) & 1)) % 2, (c + (k & 1)) % 2) for k in range(1, N_DEV)]
        sends = []
        for j, dev in enumerate(peers):
            cp = _rcopy(a_ref, o_ref.at[me], ssems.at[j], rsems.at[j], dev)
            cp.start()
            sends.append(cp)
        for j, (px, py, pc) in enumerate(peers):
            _rcopy(a_ref, o_ref.at[4 * px + 2 * py + pc], ssems.at[j], rsems.at[j], (px, py, pc)).wait_recv()
        for cp in sends:
            cp.wait_send()
        loc.wait()

    return pl.pallas_call(body, name=name, in_specs=[ANY], out_specs=ANY, out_shape=jax.ShapeDtypeStruct((N_DEV,) + a.shape, a.dtype),
                          scratch_shapes=[pltpu.SemaphoreType.DMA((N_DEV - 1,)), pltpu.SemaphoreType.DMA((N_DEV - 1,)),
                                          pltpu.SemaphoreType.DMA(())])(a)


LANES = 1024
PACK_TR = 368


def add2(name, a, b):
    r = a.shape[0]
    tr = PACK_TR if r % PACK_TR == 0 else r

    def body(a_ref, b_ref, o_ref):
        o_ref[...] = a_ref[...] + b_ref[...]

    spec = pl.BlockSpec((tr, LANES), lambda i: (i, 0))
    return pl.pallas_call(body, name=name, grid=(r // tr,), in_specs=[spec, spec], out_specs=spec,
                          out_shape=jax.ShapeDtypeStruct(a.shape, F32), compiler_params=_cparams(("parallel",)))(a, b)


def sum_leading(name, a):
    k, r, _ = a.shape
    tr = PACK_TR if r % PACK_TR == 0 else r

    def body(a_ref, o_ref):
        acc = a_ref[0]
        for q in range(1, k):
            acc = acc + a_ref[q]
        o_ref[...] = acc

    return pl.pallas_call(body, name=name, grid=(r // tr,), in_specs=[pl.BlockSpec((k, tr, LANES), lambda i: (0, i, 0))],
                          out_specs=pl.BlockSpec((tr, LANES), lambda i: (i, 0)), out_shape=jax.ShapeDtypeStruct((r, LANES), F32),
                          compiler_params=_cparams(("parallel",)))(a)


def adamw(name, g, w, m, v):
    r = g.shape[0]
    tr = PACK_TR if r % PACK_TR == 0 else r

    def body(g_ref, w_ref, m_ref, v_ref, d_ref, nm_ref, nv_ref):
        gv = g_ref[...]
        mn = ADAM_B1 * m_ref[...] + (1.0 - ADAM_B1) * gv
        vn = ADAM_B2 * v_ref[...] + (1.0 - ADAM_B2) * (gv * gv)
        m_hat = mn / (1.0 - ADAM_B1 ** ADAM_STEP)
        v_hat = vn / (1.0 - ADAM_B2 ** ADAM_STEP)
        d_ref[...] = -ADAM_LR * (m_hat / (jnp.sqrt(v_hat) + ADAM_EPS) + ADAM_WD * w_ref[...])
        nm_ref[...] = mn
        nv_ref[...] = vn

    spec = pl.BlockSpec((tr, LANES), lambda i: (i, 0))
    sds = jax.ShapeDtypeStruct(g.shape, F32)
    return pl.pallas_call(body, name=name, grid=(r // tr,), in_specs=[spec] * 4, out_specs=[spec] * 3, out_shape=[sds] * 3,
                          compiler_params=_cparams(("parallel",)))(g, w, m, v)


def _pack(arrs, rows, dtype=F32):
    flat = jnp.concatenate([a.reshape(-1).astype(dtype) for a in arrs])
    pad = rows * LANES - flat.shape[0]
    assert pad >= 0
    if pad:
        flat = jnp.concatenate([flat, jnp.zeros((pad,), dtype)])
    return flat.reshape(rows, LANES)


def _unpack(pack, shapes):
    flat = pack.reshape(-1)
    out, off = [], 0
    for shp in shapes:
        n = int(np.prod(shp))
        out.append(flat[off:off + n].reshape(shp))
        off += n
    return out


SHARD_HALF_ROWS = 2944
SMALL_ROWS = 160


def kernel(x, ln_in_g, ln_in_b, w_in, conv_a_w, conv_a_b, ln_a_g, ln_a_b, qk_norm_q, qk_norm_k, sgu_ln_g, sgu_ln_b, sgu_w, sgu_b, mla_q_norm, mla_w_uq, mla_kv_norm, mla_w_ukv, w_out, ln_mix_g, ln_mix_b, ffn_w_up, ffn_conv_w, ffn_conv_b, ffn_w_down, ln_ffn_g, ln_ffn_b, loss_target, m_ln_in_g, m_ln_in_b, m_w_in, m_conv_a_w, m_conv_a_b, m_ln_a_g, m_ln_a_b, m_qk_norm_q, m_qk_norm_k, m_sgu_ln_g, m_sgu_ln_b, m_sgu_w, m_sgu_b, m_mla_q_norm, m_mla_w_uq, m_mla_kv_norm, m_mla_w_ukv, m_w_out, m_ln_mix_g, m_ln_mix_b, m_ffn_w_up, m_ffn_conv_w, m_ffn_conv_b, m_ffn_w_down, m_ln_ffn_g, m_ln_ffn_b, v_ln_in_g, v_ln_in_b, v_w_in, v_conv_a_w, v_conv_a_b, v_ln_a_g, v_ln_a_b, v_qk_norm_q, v_qk_norm_k, v_sgu_ln_g, v_sgu_ln_b, v_sgu_w, v_sgu_b, v_mla_q_norm, v_mla_w_uq, v_mla_kv_norm, v_mla_w_ukv, v_w_out, v_ln_mix_g, v_ln_mix_b, v_ffn_w_up, v_ffn_conv_w, v_ffn_conv_b, v_ffn_w_down, v_ln_ffn_g, v_ln_ffn_b):
    loc = dict(locals())
    w_loc = {n: loc[n] for n in WEIGHT_NAMES}
    m_loc = {n: loc["m_" + n] for n in WEIGHT_NAMES}
    v_loc = {n: loc["v_" + n] for n in WEIGHT_NAMES}
    c = lax.axis_index("c")

    wire_b = [n for n in SHARDED_NAMES if n in BF16_WIRE]
    wire_f = [n for n in SHARDED_NAMES if n not in BF16_WIRE]
    nb = sum(int(np.prod(w_loc[n].shape)) for n in wire_b)
    nf = sum(int(np.prod(w_loc[n].shape)) for n in wire_f)
    pb = _pack([w_loc[n] for n in wire_b], -(-nb // LANES), BF16)
    pf = _pack([w_loc[n] for n in wire_f], -(-nf // LANES), F32)
    gb, gf = gather_over_chips("gather_weights", [pb, pf])
    W = {n: w_loc[n] for n in REPL_NAMES}
    for names, g in ((wire_b, gb), (wire_f, gf)):
        per_chip = [_unpack(g[q], [w_loc[n].shape for n in names]) for q in range(N_CHIPS)]
        for t, n in enumerate(names):
            W[n] = jnp.concatenate([per_chip[q][t] for q in range(N_CHIPS)], axis=SHARDED_AXIS[n])

    loss_row, dx, G = local_step(x[0], loss_target[0], W)
    loss = lax.psum(loss_row[0, 0], ("x", "y", "c"))

    rows = []
    for q in range(N_CHIPS):
        parts = []
        for n in SHARDED_NAMES:
            ax = SHARDED_AXIS[n]
            wdt = w_loc[n].shape[ax]
            parts.append(lax.slice_in_dim(G[n], q * wdt, (q + 1) * wdt, axis=ax))
        rows.append(_pack(parts, 2 * SHARD_HALF_ROWS).reshape(2, SHARD_HALF_ROWS, LANES))
    p4 = jnp.stack(rows)
    mine = lax.dynamic_index_in_dim(p4, c, axis=1, keepdims=False)
    other = lax.dynamic_index_in_dim(p4, 1 - c, axis=1, keepdims=False)
    from_sib = swap_with_sibling("pair_exchange", other)
    flat4 = (N_CHIPS * SHARD_HALF_ROWS, LANES)
    pair_sum = add2("pair_sum", mine.reshape(flat4), from_sib.reshape(flat4)).reshape(N_CHIPS, SHARD_HALF_ROWS, LANES)
    from_chips = scatter_over_chips("chip_exchange", pair_sum)
    half = sum_leading("chip_sum", from_chips)
    sib_half = swap_with_sibling("half_exchange", half)
    g_shard = jnp.where(c == 0, jnp.concatenate([half, sib_half]), jnp.concatenate([sib_half, half]))

    g_small = sum_leading("small_sum", gather_over_devices("small_gather", _pack([G[n] for n in REPL_NAMES], SMALL_ROWS)))

    sh_shapes = [w_loc[n].shape for n in SHARDED_NAMES]
    rp_shapes = [w_loc[n].shape for n in REPL_NAMES]
    big = adamw("adamw_sharded", g_shard, *[_pack([d[n] for n in SHARDED_NAMES], 2 * SHARD_HALF_ROWS) for d in (w_loc, m_loc, v_loc)])
    small = adamw("adamw_small", g_small, *[_pack([d[n] for n in REPL_NAMES], SMALL_ROWS) for d in (w_loc, m_loc, v_loc)])
    outs = {}
    for key, pk_big, pk_small in (("grad", g_shard, g_small), ("delta", big[0], small[0]), ("new_m", big[1], small[1]),
                                  ("new_v", big[2], small[2])):
        d = dict(zip(SHARDED_NAMES, _unpack(pk_big, sh_shapes)))
        d.update(zip(REPL_NAMES, _unpack(pk_small, rp_shapes)))
        outs[key] = d
    return (loss, dx[None], *[outs["grad"][n] for n in WEIGHT_NAMES], *[outs["delta"][n] for n in WEIGHT_NAMES],
            *[outs["new_m"][n] for n in WEIGHT_NAMES], *[outs["new_v"][n] for n in WEIGHT_NAMES])
```

```python
import functools
import math

import jax
import jax.numpy as jnp
import numpy as np
from jax import lax
from jax.experimental import pallas as pl
from jax.experimental.pallas import tpu as pltpu

F32 = jnp.float32
BF16 = jnp.bfloat16

D_MODEL = 1024
DEPTH = 2
GRID_W = 64
GROUP_W = 256
HEAD_DIM = 64
CONV_A_WIDTH = 31
CHUNK = 128
MLA_Q_LORA = 192
MLA_KV_LORA = 128
MLA_NOPE = 64
MLA_ROPE = 32
D_FF = 2816
ROPE_THETA = 10000.0
DEEPNORM_ALPHA = (2 * DEPTH) ** 0.25
LN_EPS = 1e-5
RMS_EPS = 1e-6
D_IN_PROJ = 1888
PROJ_W = 2048

ADAM_LR = 0.001
ADAM_B1 = 0.9
ADAM_B2 = 0.999
ADAM_EPS = 1e-08
ADAM_WD = 0.01
ADAM_STEP = 10

VMEM_LIMIT = 56 * 1024 * 1024
MESH = pl.DeviceIdType.MESH
N_CHIPS = 4
N_DEV = 8

WEIGHT_NAMES = ['ln_in_g', 'ln_in_b', 'w_in', 'conv_a_w', 'conv_a_b', 'ln_a_g', 'ln_a_b', 'qk_norm_q', 'qk_norm_k',
                'sgu_ln_g', 'sgu_ln_b', 'sgu_w', 'sgu_b', 'mla_q_norm', 'mla_w_uq', 'mla_kv_norm', 'mla_w_ukv', 'w_out',
                'ln_mix_g', 'ln_mix_b', 'ffn_w_up', 'ffn_conv_w', 'ffn_conv_b', 'ffn_w_down', 'ln_ffn_g', 'ln_ffn_b']
SHARDED_AXIS = {'w_in': 2, 'conv_a_w': 2, 'mla_w_uq': 2, 'mla_w_ukv': 2, 'w_out': 1, 'ffn_w_up': 2, 'ffn_conv_w': 2,
                'ffn_w_down': 1}
SHARDED_NAMES = [n for n in WEIGHT_NAMES if n in SHARDED_AXIS]
REPL_NAMES = [n for n in WEIGHT_NAMES if n not in SHARDED_AXIS]
BF16_WIRE = ('w_in', 'mla_w_uq', 'mla_w_ukv', 'w_out', 'ffn_w_up', 'ffn_w_down')


def _cparams(sem):
    return pltpu.CompilerParams(dimension_semantics=sem, vmem_limit_bytes=VMEM_LIMIT)


def _pick(n, cands):
    for c in cands:
        if n % c == 0:
            return c
    return n


def mm(name, a, b, *, ta=False, tb=False, add=None, add_scale=1.0, out_dtype=F32, chip_major=False):
    m, k = (a.shape[1], a.shape[0]) if ta else a.shape
    n = b.shape[0] if tb else b.shape[1]
    assert (b.shape[1] if tb else b.shape[0]) == k
    tm = _pick(m, (512, 256, 128))
    tn = n // N_CHIPS if chip_major else _pick(n, (1024, 1408, 512, 256, 128))
    tk = _pick(k, (1024, 1408, 512, 256, 128))
    nk = k // tk
    a_spec = pl.BlockSpec((tk, tm), lambda i, j, kk: (kk, i)) if ta else pl.BlockSpec((tm, tk), lambda i, j, kk: (i, kk))
    b_spec = pl.BlockSpec((tn, tk), lambda i, j, kk: (j, kk)) if tb else pl.BlockSpec((tk, tn), lambda i, j, kk: (kk, j))
    in_specs = [a_spec, b_spec]
    args = [a, b]
    if add is not None:
        in_specs.append(pl.BlockSpec((tm, tn), lambda i, j, kk: (i, j)))
        args.append(add)
    dims = (((0 if ta else 1,), (1 if tb else 0,)), ((), ()))

    def body(*refs):
        a_ref, b_ref = refs[0], refs[1]
        add_ref = refs[2] if add is not None else None
        o_ref, acc_ref = refs[-2], refs[-1]
        kk = pl.program_id(2)
        part = lax.dot_general(a_ref[...].astype(BF16), b_ref[...].astype(BF16), dims, preferred_element_type=F32)

        @pl.when(kk == 0)
        def _():
            acc_ref[...] = part

        @pl.when(kk > 0)
        def _():
            acc_ref[...] += part

        @pl.when(kk == nk - 1)
        def _():
            r = acc_ref[...]
            if add_ref is not None:
                r = r + add_scale * add_ref[...].astype(F32)
            o_ref[...] = r.astype(out_dtype)

    if chip_major:
        out_spec = pl.BlockSpec((None, tm, tn), lambda i, j, kk: (j, i, 0))
        out_shape = jax.ShapeDtypeStruct((N_CHIPS, m, tn), out_dtype)
    else:
        out_spec = pl.BlockSpec((tm, tn), lambda i, j, kk: (i, j))
        out_shape = jax.ShapeDtypeStruct((m, n), out_dtype)
    return pl.pallas_call(
        body, name=name, grid=(m // tm, n // tn, nk),
        in_specs=in_specs, out_specs=out_spec, out_shape=out_shape,
        scratch_shapes=[pltpu.VMEM((tm, tn), F32)],
        compiler_params=_cparams(("parallel", "parallel", "arbitrary")),
    )(*args)


def _cb(cb):
    return cb if callable(cb) else (lambda j, _c=cb: _c)


def tiled_call(name, fn, s, tm, ins, outs, ncol=1):
    nrow = s // tm
    in_specs, args, kinds = [], [], []
    for it in ins:
        kind = it[0]
        if kind == 'row':
            _, arr, w, cb = it
            in_specs.append(pl.BlockSpec((tm, w), lambda j, i, _c=_cb(cb): (i, _c(j))))
            args.append(arr)
            kinds.append(('row',))
        elif kind == 'halo':
            _, arr, w, cb, h = it
            r = tm // h
            nh = s // h
            in_specs.append(pl.BlockSpec((h, w), lambda j, i, _c=_cb(cb), _r=r: (jnp.maximum(i * _r - 1, 0), _c(j))))
            in_specs.append(pl.BlockSpec((tm, w), lambda j, i, _c=_cb(cb): (i, _c(j))))
            in_specs.append(pl.BlockSpec((h, w), lambda j, i, _c=_cb(cb), _r=r, _n=nh: (jnp.minimum((i + 1) * _r, _n - 1), _c(j))))
            args += [arr, arr, arr]
            kinds.append(('halo',))
        elif kind == 'full':
            arr = it[1]
            in_specs.append(pl.BlockSpec(arr.shape, lambda j, i, _n=arr.ndim: (0,) * _n))
            args.append(arr)
            kinds.append(('ref',))
        elif kind == 'col':
            _, arr, w, cb = it
            in_specs.append(pl.BlockSpec((arr.shape[0], w), lambda j, i, _c=_cb(cb): (0, _c(j))))
            args.append(arr)
            kinds.append(('ref',))
        elif kind == 'hm':
            arr = it[1]
            in_specs.append(pl.BlockSpec((arr.shape[0], tm, arr.shape[2]), lambda j, i: (0, i, 0)))
            args.append(arr)
            kinds.append(('ref',))
        else:
            raise ValueError(kind)
    out_specs, out_shapes, okinds = [], [], []
    for ot in outs:
        kind = ot[0]
        if kind == 'row':
            _, wt, w, cb, dt = ot
            out_specs.append(pl.BlockSpec((tm, w), lambda j, i, _c=_cb(cb): (i, _c(j))))
            out_shapes.append(jax.ShapeDtypeStruct((s, wt), dt))
            okinds.append('row')
        elif kind == 'hm':
            _, hh, d, dt = ot
            out_specs.append(pl.BlockSpec((hh, tm, d), lambda j, i: (0, i, 0)))
            out_shapes.append(jax.ShapeDtypeStruct((hh, s, d), dt))
            okinds.append('hm')
        elif kind == 'acc':
            shape = ot[1]
            out_specs.append(pl.BlockSpec(shape, lambda j, i, _n=len(shape): (0,) * _n))
            out_shapes.append(jax.ShapeDtypeStruct(shape, F32))
            okinds.append('acc')
        elif kind == 'colacc':
            _, r, wt, w, cb = ot
            out_specs.append(pl.BlockSpec((r, w), lambda j, i, _c=_cb(cb): (0, _c(j))))
            out_shapes.append(jax.ShapeDtypeStruct((r, wt), F32))
            okinds.append('colacc')
        else:
            raise ValueError(kind)
    n_in = len(in_specs)

    def body(*refs):
        j = pl.program_id(0)
        i = pl.program_id(1)
        in_refs, out_refs = refs[:n_in], refs[n_in:]
        items, p = [], 0
        for kd in kinds:
            if kd[0] == 'row':
                items.append(in_refs[p][...])
                p += 1
            elif kd[0] == 'halo':
                prev, cen, nxt = in_refs[p][...], in_refs[p + 1][...], in_refs[p + 2][...]
                prev = jnp.where(i == 0, jnp.zeros_like(prev), prev)
                nxt = jnp.where(i == nrow - 1, jnp.zeros_like(nxt), nxt)
                items.append(jnp.concatenate([prev, cen, nxt], axis=0))
                p += 3
            else:
                items.append(in_refs[p])
                p += 1
        for o_ref, kd in zip(out_refs, okinds):
            if kd == 'acc':
                @pl.when((i == 0) & (j == 0))
                def _(o_ref=o_ref):
                    o_ref[...] = jnp.zeros_like(o_ref)
            elif kd == 'colacc':
                @pl.when(i == 0)
                def _(o_ref=o_ref):
                    o_ref[...] = jnp.zeros_like(o_ref)
        fn(i, j, *items, *out_refs)

    res = pl.pallas_call(
        body, name=name, grid=(ncol, nrow), in_specs=in_specs, out_specs=out_specs, out_shape=out_shapes,
        compiler_params=_cparams(("arbitrary", "arbitrary")),
    )(*args)
    return res


def _sigmoid(x):
    return 1.0 / (1.0 + jnp.exp(-x))


def _ln_stats(x):
    mu = jnp.mean(x, axis=1, keepdims=True)
    xc = x - mu
    var = jnp.mean(xc * xc, axis=1, keepdims=True)
    r = lax.rsqrt(var + LN_EPS)
    return xc * r, r


def _ln_bwd(dy, xh, r, g):
    dxh = dy * g
    dx = r * (dxh - jnp.mean(dxh, axis=1, keepdims=True) - xh * jnp.mean(dxh * xh, axis=1, keepdims=True))
    return dx, jnp.sum(dy * xh, axis=0, keepdims=True), jnp.sum(dy, axis=0, keepdims=True)


def _gmean(v, gm):
    hi = v.astype(BF16)
    lo = (v - hi.astype(F32)).astype(BF16)
    return jnp.dot(hi, gm, preferred_element_type=F32) + jnp.dot(lo, gm, preferred_element_type=F32)


def _roll_l(x, sh):
    return pltpu.roll(x, sh % x.shape[1], 1)


def _rope(x, c, sa, sb, sh):
    return x * c + _roll_l(x, -sh) * sa + _roll_l(x, sh) * sb


def _rope_t(dy, c, sa, sb, sh):
    return dy * c + _roll_l(dy * sa, sh) + _roll_l(dy * sb, -sh)


def _roll_r(x, s):
    return pltpu.roll(x, (-s) % x.shape[0], 0)


_GELU_C = math.sqrt(2.0 / math.pi)


def _gelu(x):
    t = jnp.tanh(_GELU_C * (x + 0.044715 * x * x * x))
    return 0.5 * x * (1.0 + t), t


def _gelu_grad(x, t):
    return 0.5 * (1.0 + t) + 0.5 * x * (1.0 - t * t) * _GELU_C * (1.0 + 3.0 * 0.044715 * x * x)


def _dot_nt(a, b):
    return lax.dot_general(a, b, (((1,), (1,)), ((), ())), preferred_element_type=F32)


def _dot_tn(a, b):
    return lax.dot_general(a, b, (((0,), (0,)), ((), ())), preferred_element_type=F32)


def _rope_block(s, d):
    t = jnp.arange(s)
    row = (t // GRID_W).astype(F32)
    col = (t % GRID_W).astype(F32)
    half = d // 4
    inv = ROPE_THETA ** (-jnp.arange(half, dtype=F32) / half)
    z = jnp.zeros((s, half), F32)
    cs, sas, sbs = [], [], []
    for pos in (row, col):
        ang = pos[:, None] * inv[None, :]
        co, si = jnp.cos(ang), jnp.sin(ang)
        cs += [co, co]
        sas += [-si, z]
        sbs += [z, si]
    return tuple(jnp.concatenate(v, axis=1) for v in (cs, sas, sbs))


def _rope_tables(s):
    gqa = tuple(jnp.tile(a, (1, 4)) for a in _rope_block(s, HEAD_DIM))
    c32, sa32, sb32 = _rope_block(s, MLA_ROPE)

    def head128(c, fill):
        return jnp.concatenate([jnp.full((s, 64), fill, F32), c, jnp.zeros((s, 32), F32)], axis=1)

    mla = tuple(jnp.tile(a, (1, 4)) for a in (head128(c32, 1.0), head128(sa32, 0.0), head128(sb32, 0.0)))
    kr = (head128(c32, 0.0), head128(sa32, 0.0), head128(sb32, 0.0))
    return dict(gqa=gqa, mla=mla, kr=kr)


def _group_mean_matrix(w, g):
    idx = np.arange(w) // g
    return jnp.asarray((idx[:, None] == idx[None, :]).astype(np.float32) / g, dtype=BF16)


TM = 512


def ln_fwd(name, s, x, g, b, res=None, alpha=1.0):
    ins = [('row', x, D_MODEL, 0)] + ([('row', res, D_MODEL, 0)] if res is not None else []) + [('full', g), ('full', b)]

    def fn(i, j, *a):
        if res is not None:
            xv, rv, g_ref, b_ref, o_ref = a
            z = alpha * rv + xv
        else:
            xv, g_ref, b_ref, o_ref = a
            z = xv
        xh, _ = _ln_stats(z)
        o_ref[...] = xh * g_ref[...] + b_ref[...]

    return tiled_call(name, fn, s, TM, ins, [('row', D_MODEL, D_MODEL, 0, F32)])[0]


def ln_bwd(name, s, dy, x, g, res=None, alpha=1.0):
    ins = [('row', dy, D_MODEL, 0), ('row', x, D_MODEL, 0)]
    ins += ([('row', res, D_MODEL, 0)] if res is not None else []) + [('full', g)]

    def fn(i, j, *a):
        a = list(a)
        dyv = a.pop(0)
        xv = a.pop(0)
        z = alpha * a.pop(0) + xv if res is not None else xv
        g_ref, dz_ref, dg_ref, db_ref = a
        xh, r = _ln_stats(z)
        dz, dg, db = _ln_bwd(dyv, xh, r, g_ref[...])
        dz_ref[...] = dz
        dg_ref[...] += dg
        db_ref[...] += db

    return tiled_call(name, fn, s, TM, ins, [('row', D_MODEL, D_MODEL, 0, F32), ('acc', (1, D_MODEL)), ('acc', (1, D_MODEL))])


def loss_and_ln_bwd(name, s, x, res, g, b, tgt, alpha):
    ins = [('row', x, D_MODEL, 0), ('row', res, D_MODEL, 0), ('row', tgt, D_MODEL, 0), ('full', g), ('full', b)]

    def fn(i, j, xv, rv, tv, g_ref, b_ref, dz_ref, dg_ref, db_ref, loss_ref):
        xh, r = _ln_stats(alpha * rv + xv)
        y = xh * g_ref[...] + b_ref[...]
        e = y - tv
        dz, dg, db = _ln_bwd(e * (1.0 / D_MODEL), xh, r, g_ref[...])
        dz_ref[...] = dz
        dg_ref[...] += dg
        db_ref[...] += db
        loss_ref[...] += jnp.sum(jnp.sum(e * e, axis=0, keepdims=True), axis=1, keepdims=True) * (0.5 / D_MODEL)

    return tiled_call(name, fn, s, TM, ins, [('row', D_MODEL, D_MODEL, 0, F32), ('acc', (1, D_MODEL)), ('acc', (1, D_MODEL)),
                                             ('acc', (1, 128))])


HALO_A = 16


def _glu(a_in):
    return a_in[:, :GROUP_W] * _sigmoid(a_in[:, GROUP_W:])


def a_fwd(name, s, proj, cw, cb, g, b):
    def fn(i, j, a_ext, cw_ref, cb_ref, g_ref, b_ref, y_ref, o_ref):
        a = _glu(a_ext)
        acc = jnp.zeros((TM, GROUP_W), F32)
        for k in range(CONV_A_WIDTH):
            acc = acc + _roll_r(a, k - 15)[HALO_A:HALO_A + TM] * cw_ref[pl.ds(k, 1), :]
        y = acc + cb_ref[...]
        y_ref[...] = y
        xh, _ = _ln_stats(y)
        z = xh * g_ref[...] + b_ref[...]
        o_ref[...] = z * _sigmoid(z)

    ins = [('halo', proj, 2 * GROUP_W, 0, HALO_A), ('full', cw), ('full', cb), ('full', g), ('full', b)]
    return tiled_call(name, fn, s, TM, ins, [('row', GROUP_W, GROUP_W, 0, F32), ('row', GROUP_W, GROUP_W, 0, F32)])


def a_bwd1(name, s, y, dmix, g, b):
    def fn(i, j, yv, do, g_ref, b_ref, dy_ref, dg_ref, db_ref, dcb_ref):
        xh, r = _ln_stats(yv)
        z = xh * g_ref[...] + b_ref[...]
        sg = _sigmoid(z)
        dz = do * sg * (1.0 + z * (1.0 - sg))
        dy, dg, db = _ln_bwd(dz, xh, r, g_ref[...])
        dy_ref[...] = dy
        dg_ref[...] += dg
        db_ref[...] += db
        dcb_ref[...] += jnp.sum(dy, axis=0, keepdims=True)

    ins = [('row', y, GROUP_W, 0), ('row', dmix, GROUP_W, 0), ('full', g), ('full', b)]
    return tiled_call(name, fn, s, TM, ins, [('row', GROUP_W, GROUP_W, 0, F32), ('acc', (1, GROUP_W)), ('acc', (1, GROUP_W)),
                                             ('acc', (1, GROUP_W))])


def a_bwd2(name, s, proj, dy, cw):
    def fn(i, j, a_ext, dy_ext, cw_ref, da_ref, dcw_ref):
        a1, a2 = a_ext[:, :GROUP_W], a_ext[:, GROUP_W:]
        sg = _sigmoid(a2)
        a = a1 * sg
        dyc = dy_ext[HALO_A:HALO_A + TM]
        da = jnp.zeros((TM, GROUP_W), F32)
        for k in range(CONV_A_WIDTH):
            da = da + _roll_r(dy_ext, 15 - k)[HALO_A:HALO_A + TM] * cw_ref[pl.ds(k, 1), :]
            dcw_ref[pl.ds(k, 1), :] += jnp.sum(dyc * _roll_r(a, k - 15)[HALO_A:HALO_A + TM], axis=0, keepdims=True)
        a1c, sgc = a1[HALO_A:HALO_A + TM], sg[HALO_A:HALO_A + TM]
        da_ref[...] = jnp.concatenate([da * sgc, da * a1c * sgc * (1.0 - sgc)], axis=1).astype(BF16)

    ins = [('halo', proj, 2 * GROUP_W, 0, HALO_A), ('halo', dy, GROUP_W, 0, HALO_A), ('full', cw)]
    return tiled_call(name, fn, s, TM, ins, [('row', 2 * GROUP_W, 2 * GROUP_W, 0, BF16), ('acc', (32, GROUP_W))])


def b_prep_fwd(name, s, proj, gq, gk, tabs, gm):
    c, sa, sb = tabs

    def fn(i, j, qkv, cv, sav, sbv, gq_ref, gk_ref, gm_ref, q_ref, k_ref, v_ref):
        q, k, v = qkv[:, :256], qkv[:, 256:384], qkv[:, 384:]
        gmv = gm_ref[...]
        qn = q * lax.rsqrt(_gmean(q * q, gmv) + RMS_EPS) * gq_ref[...]
        kn = k * lax.rsqrt(_gmean(k * k, gmv[:128, :128]) + RMS_EPS) * gk_ref[...]
        qr = _rope(qn, cv, sav, sbv, 16)
        kr = _rope(kn, cv[:, :128], sav[:, :128], sbv[:, :128], 16)
        for h in range(4):
            q_ref[h] = qr[:, h * 64:(h + 1) * 64].astype(BF16)
        for h in range(2):
            k_ref[h] = kr[:, h * 64:(h + 1) * 64].astype(BF16)
            v_ref[h] = v[:, h * 64:(h + 1) * 64].astype(BF16)

    ins = [('row', proj, 512, 1), ('row', c, 256, 0), ('row', sa, 256, 0), ('row', sb, 256, 0), ('full', gq), ('full', gk),
           ('full', gm)]
    return tiled_call(name, fn, s, TM, ins, [('hm', 4, 64, BF16), ('hm', 2, 64, BF16), ('hm', 2, 64, BF16)])


def b_prep_bwd(name, s, proj, gq, gk, tabs, gm, dq_hm, dk_hm, dv_hm):
    c, sa, sb = tabs

    def fn(i, j, qkv, cv, sav, sbv, gq_ref, gk_ref, gm_ref, dq_ref, dk_ref, dv_ref, dp_ref, dgq_ref, dgk_ref):
        q, k = qkv[:, :256], qkv[:, 256:384]
        gmv = gm_ref[...]
        dqr = jnp.concatenate([dq_ref[h] for h in range(4)], axis=1)
        dkr = jnp.concatenate([dk_ref[h] for h in range(2)], axis=1)
        dv = jnp.concatenate([dv_ref[h] for h in range(2)], axis=1)
        outs = []
        for x, dxr, g_ref, gmx, w, dg_ref in ((q, dqr, gq_ref, gmv, 256, dgq_ref), (k, dkr, gk_ref, gmv[:128, :128], 128, dgk_ref)):
            dn = _rope_t(dxr, cv[:, :w], sav[:, :w], sbv[:, :w], 16)
            r = lax.rsqrt(_gmean(x * x, gmx) + RMS_EPS)
            gv = g_ref[...]
            dx = gv * r * dn - x * (r * r * r) * _gmean(x * gv * dn, gmx)
            dgt = jnp.sum(x * r * dn, axis=0, keepdims=True)
            dg = dgt[:, 0:64]
            for h in range(1, w // 64):
                dg = dg + dgt[:, h * 64:(h + 1) * 64]
            dg_ref[...] += dg
            outs.append(dx)
        dp_ref[...] = jnp.concatenate(outs + [dv], axis=1).astype(BF16)

    ins = [('row', proj, 512, 1), ('row', c, 256, 0), ('row', sa, 256, 0), ('row', sb, 256, 0), ('full', gq), ('full', gk),
           ('full', gm), ('hm', dq_hm), ('hm', dk_hm), ('hm', dv_hm)]
    return tiled_call(name, fn, s, TM, ins, [('row', 512, 512, 0, BF16), ('acc', (1, 64)), ('acc', (1, 64))])


TQ = 256


def attn_fwd(name, s, q_hm, k_hm, v_hm, scale):
    dk = q_hm.shape[2]
    kpb = k_hm.shape[0] // 2

    def body(q_ref, k_ref, v_ref, o_ref, l_ref):
        outs = []
        for g in range(2):
            kv = g if kpb == 2 else 0
            sc = _dot_nt(q_ref[g], k_ref[kv]) * scale
            m = jnp.max(sc, axis=1, keepdims=True)
            p = jnp.exp(sc - m)
            l = jnp.sum(p, axis=1, keepdims=True)
            o = jnp.dot(p.astype(BF16), v_ref[kv], preferred_element_type=F32)
            outs.append(o / l)
            l_ref[g] = m + jnp.log(l)
        o_ref[...] = jnp.concatenate(outs, axis=1)

    return pl.pallas_call(
        body, name=name, grid=(2, s // TQ),
        in_specs=[pl.BlockSpec((2, TQ, dk), lambda p, i: (p, i, 0)),
                  pl.BlockSpec((kpb, s, dk), lambda p, i: (p, 0, 0)),
                  pl.BlockSpec((kpb, s, 64), lambda p, i: (p, 0, 0))],
        out_specs=[pl.BlockSpec((TQ, 128), lambda p, i: (i, p)), pl.BlockSpec((2, TQ, 1), lambda p, i: (p, i, 0))],
        out_shape=[jax.ShapeDtypeStruct((s, 256), F32), jax.ShapeDtypeStruct((4, s, 1), F32)],
        compiler_params=_cparams(("arbitrary", "arbitrary")),
    )(q_hm, k_hm, v_hm)


def attn_bwd(name, s, q_hm, k_hm, v_hm, o, lse, dmixcat, dcol, scale):
    dk = q_hm.shape[2]
    nkv = k_hm.shape[0]
    shared = nkv == 2
    kvi = (lambda p, g, i: (p, 0, 0)) if shared else (lambda p, g, i: (2 * p + g, 0, 0))

    def body(q_ref, k_ref, v_ref, o_ref, l_ref, do_ref, dq_ref, dk_ref, dv_ref):
        g = pl.program_id(1)
        i = pl.program_id(2)
        first = (i == 0) & (g == 0) if shared else (i == 0)

        @pl.when(first)
        def _():
            dk_ref[...] = jnp.zeros_like(dk_ref)
            dv_ref[...] = jnp.zeros_like(dv_ref)

        dob, ob = do_ref[...], o_ref[...]
        do = jnp.where(g == 0, dob[:, :64], dob[:, 64:])
        ov = jnp.where(g == 0, ob[:, :64], ob[:, 64:])
        q, k, v = q_ref[0], k_ref[0], v_ref[0]
        sc = _dot_nt(q, k) * scale
        p = jnp.exp(sc - l_ref[0])
        dp = _dot_nt(do.astype(BF16), v)
        delta = jnp.sum(do * ov, axis=1, keepdims=True)
        ds = (p * (dp - delta) * scale).astype(BF16)
        dq_ref[0] = jnp.dot(ds, k, preferred_element_type=F32)
        dk_ref[0] += _dot_tn(ds, q)
        dv_ref[0] += _dot_tn(p.astype(BF16), do.astype(BF16))

    return pl.pallas_call(
        body, name=name, grid=(2, 2, s // TQ),
        in_specs=[pl.BlockSpec((1, TQ, dk), lambda p, g, i: (2 * p + g, i, 0)),
                  pl.BlockSpec((1, s, dk), kvi), pl.BlockSpec((1, s, 64), kvi),
                  pl.BlockSpec((TQ, 128), lambda p, g, i: (i, p)),
                  pl.BlockSpec((1, TQ, 1), lambda p, g, i: (2 * p + g, i, 0)),
                  pl.BlockSpec((TQ, 128), lambda p, g, i: (i, dcol + p))],
        out_specs=[pl.BlockSpec((1, TQ, dk), lambda p, g, i: (2 * p + g, i, 0)),
                   pl.BlockSpec((1, s, dk), kvi), pl.BlockSpec((1, s, 64), kvi)],
        out_shape=[jax.ShapeDtypeStruct((4, s, dk), F32), jax.ShapeDtypeStruct((nkv, s, dk), F32),
                   jax.ShapeDtypeStruct((nkv, s, 64), F32)],
        compiler_params=_cparams(("arbitrary", "arbitrary", "arbitrary")),
    )(q_hm, k_hm, v_hm, o, lse, dmixcat)


def _lane_group_masks(w, g):
    lane = lax.broadcasted_iota(jnp.int32, (1, w), 1)
    return [((lane >= k * g) & (lane < (k + 1) * g)).astype(F32) for k in range(w // g)]


def _sgu_gate(svn, w_ref, bias):
    masks = _lane_group_masks(GROUP_W, 64)
    outs = []
    for n in range(TM // CHUNK):
        x = svn[n * CHUNK:(n + 1) * CHUNK]
        acc = bias
        for g in range(4):
            acc = acc + jnp.dot(w_ref[g].astype(BF16), (x * masks[g]).astype(BF16), preferred_element_type=F32)
        outs.append(acc)
    return jnp.concatenate(outs, axis=0)


def c_fwd(name, s, proj, g, b, w, bias):
    def fn(i, j, cin, g_ref, b_ref, w_ref, bias_ref, o_ref):
        c, _ = _gelu(cin)
        xh, _ = _ln_stats(c[:, GROUP_W:])
        svn = xh * g_ref[...] + b_ref[...]
        o_ref[...] = c[:, :GROUP_W] * _sgu_gate(svn, w_ref, bias_ref[...])

    ins = [('row', proj, 512, 2), ('full', g), ('full', b), ('full', w), ('full', bias)]
    return tiled_call(name, fn, s, TM, ins, [('row', GROUP_W, GROUP_W, 0, F32)])[0]


def c_bwd(name, s, proj, g, b, w, wt, bias, dmixcat):
    def fn(i, j, cin, do, g_ref, b_ref, w_ref, wt_ref, bias_ref, dc_ref, dg_ref, db_ref, dw_ref, dbias_ref):
        c, t = _gelu(cin)
        u = c[:, :GROUP_W]
        xh, r = _ln_stats(c[:, GROUP_W:])
        svn = xh * g_ref[...] + b_ref[...]
        gate = _sgu_gate(svn, w_ref, bias_ref[...])
        du = do * gate
        dgate = do * u
        masks = _lane_group_masks(GROUP_W, 64)
        col = lax.broadcasted_iota(jnp.int32, (CHUNK, 128), 1)
        dsvn, dbias = [], jnp.zeros((CHUNK, 128), F32)
        for n in range(TM // CHUNK):
            dgc = dgate[n * CHUNK:(n + 1) * CHUNK]
            x = svn[n * CHUNK:(n + 1) * CHUNK]
            acc = jnp.zeros((CHUNK, GROUP_W), F32)
            for gi in range(4):
                dgm = (dgc * masks[gi]).astype(BF16)
                acc = acc + jnp.dot(wt_ref[gi].astype(BF16), dgm, preferred_element_type=F32)
                dw_ref[gi] += _dot_nt(dgm, (x * masks[gi]).astype(BF16))
                dbias = dbias + jnp.where(col == gi, jnp.sum(dgc * masks[gi], axis=1, keepdims=True), 0.0)
            dsvn.append(acc)
        dbias_ref[...] += dbias
        dsv, dg, db = _ln_bwd(jnp.concatenate(dsvn, axis=0), xh, r, g_ref[...])
        dg_ref[...] += dg
        db_ref[...] += db
        dc_ref[...] = (jnp.concatenate([du, dsv], axis=1) * _gelu_grad(cin, t)).astype(BF16)

    ins = [('row', proj, 512, 2), ('row', dmixcat, GROUP_W, 2), ('full', g), ('full', b), ('full', w), ('full', wt), ('full', bias)]
    return tiled_call(name, fn, s, TM, ins, [('row', 512, 512, 0, BF16), ('acc', (1, GROUP_W)), ('acc', (1, GROUP_W)),
                                             ('acc', (4, CHUNK, CHUNK)), ('acc', (CHUNK, 128))])


def _d_common(dpart, gq_ref, gkv_ref):
    cq, ckv, kr = dpart[:, :256], dpart[:, 256:384], dpart[:, 384:]
    rq = lax.rsqrt(jnp.sum(cq * cq, axis=1, keepdims=True) * (1.0 / MLA_Q_LORA) + RMS_EPS)
    rkv = lax.rsqrt(jnp.mean(ckv * ckv, axis=1, keepdims=True) + RMS_EPS)
    return cq, ckv, kr, rq, rkv, cq * rq * gq_ref[...], ckv * rkv * gkv_ref[...]


def d_prep_fwd(name, s, proj, gq, gkv, wuq, wukv, tm_, tkr):
    cm, sam, sbm = tm_
    ck, sak, sbk = tkr

    def fn(i, j, dpart, cmv, samv, sbmv, ckv_, sakv, sbkv, gq_ref, gkv_ref, wuq_ref, wukv_ref, q_ref, k_ref, v_ref):
        cq, ckv, kr, rq, rkv, cqn, ckvn = _d_common(dpart, gq_ref, gkv_ref)
        qd = jnp.dot(cqn.astype(BF16), wuq_ref[...], preferred_element_type=F32)
        qf = _rope(qd, cmv, samv, sbmv, 8)
        kvd = jnp.dot(ckvn.astype(BF16), wukv_ref[...], preferred_element_type=F32)
        krr = _rope(kr, ckv_, sakv, sbkv, 8)
        for h in range(4):
            q_ref[h] = qf[:, h * 128:(h + 1) * 128].astype(BF16)
            k_ref[h] = (kvd[:, h * 128:(h + 1) * 128] + krr).astype(BF16)
            v_ref[h] = kvd[:, 512 + h * 64:512 + (h + 1) * 64].astype(BF16)

    ins = [('row', proj, 512, 3), ('row', cm, 512, 0), ('row', sam, 512, 0), ('row', sbm, 512, 0), ('row', ck, 128, 0),
           ('row', sak, 128, 0), ('row', sbk, 128, 0), ('full', gq), ('full', gkv), ('full', wuq), ('full', wukv)]
    return tiled_call(name, fn, s, TM, ins, [('hm', 4, 128, BF16), ('hm', 4, 128, BF16), ('hm', 4, 64, BF16)])


def d_prep_bwd(name, s, proj, gq, gkv, wuq, wukv, tm_, tkr, dq_hm, dk_hm, dv_hm):
    cm, sam, sbm = tm_
    ck, sak, sbk = tkr

    def fn(i, j, dpart, cmv, samv, sbmv, ckv_, sakv, sbkv, gq_ref, gkv_ref, wuq_ref, wukv_ref, dq_ref, dk_ref, dv_ref,
           dd_ref, dgq_ref, dgkv_ref, dwuq_ref, dwukv_ref):
        cq, ckv, kr, rq, rkv, cqn, ckvn = _d_common(dpart, gq_ref, gkv_ref)
        dqf = jnp.concatenate([dq_ref[h] for h in range(4)], axis=1)
        dkf = [dk_ref[h] for h in range(4)]
        dqd = _rope_t(dqf, cmv, samv, sbmv, 8).astype(BF16)
        dkvd = jnp.concatenate(dkf + [dv_ref[h] for h in range(4)], axis=1).astype(BF16)
        dkr = _rope_t(dkf[0] + dkf[1] + dkf[2] + dkf[3], ckv_, sakv, sbkv, 8)
        dcqn = _dot_nt(dqd, wuq_ref[...])
        dckvn = _dot_nt(dkvd, wukv_ref[...])
        dwuq_ref[...] += _dot_tn(cqn.astype(BF16), dqd)
        dwukv_ref[...] += _dot_tn(ckvn.astype(BF16), dkvd)
        gqv, gkvv = gq_ref[...], gkv_ref[...]
        dcq = gqv * rq * dcqn - cq * (rq * rq * rq) * (jnp.sum(cq * gqv * dcqn, axis=1, keepdims=True) * (1.0 / MLA_Q_LORA))
        dckv = gkvv * rkv * dckvn - ckv * (rkv * rkv * rkv) * jnp.mean(ckv * gkvv * dckvn, axis=1, keepdims=True)
        dgq_ref[...] += jnp.sum(cq * rq * dcqn, axis=0, keepdims=True)
        dgkv_ref[...] += jnp.sum(ckv * rkv * dckvn, axis=0, keepdims=True)
        dd_ref[...] = jnp.concatenate([dcq, dckv, dkr], axis=1).astype(BF16)

    ins = [('row', proj, 512, 3), ('row', cm, 512, 0), ('row', sam, 512, 0), ('row', sbm, 512, 0), ('row', ck, 128, 0),
           ('row', sak, 128, 0), ('row', sbk, 128, 0), ('full', gq), ('full', gkv), ('full', wuq), ('full', wukv),
           ('hm', dq_hm), ('hm', dk_hm), ('hm', dv_hm)]
    return tiled_call(name, fn, s, TM, ins, [('row', 512, 512, 0, BF16), ('acc', (1, 256)), ('acc', (1, 128)),
                                             ('acc', (256, 512)), ('acc', (128, 768))])


HALO_F = 8
TN_F = 256
TM_F = 512


def _conv3(ext, w_ref):
    return _roll_r(ext, -1) * w_ref[pl.ds(0, 1), :] + ext * w_ref[pl.ds(1, 1), :] + _roll_r(ext, 1) * w_ref[pl.ds(2, 1), :]


def f_fwd(name, s, upraw, cw, cb):
    ncj = D_FF // TN_F

    def fn(i, j, xa, xg, wa_ref, wg_ref, ba_ref, bg_ref, o_ref):
        ua = (_conv3(xa, wa_ref) + ba_ref[...])[HALO_F:HALO_F + TM_F]
        ug = (_conv3(xg, wg_ref) + bg_ref[...])[HALO_F:HALO_F + TM_F]
        o_ref[...] = (ua * _sigmoid(ua) * ug).astype(BF16)

    ins = [('halo', upraw, TN_F, lambda j: j, HALO_F), ('halo', upraw, TN_F, lambda j: j + ncj, HALO_F),
           ('col', cw, TN_F, lambda j: j), ('col', cw, TN_F, lambda j: j + ncj),
           ('col', cb, TN_F, lambda j: j), ('col', cb, TN_F, lambda j: j + ncj)]
    return tiled_call(name, fn, s, TM_F, ins, [('row', D_FF, TN_F, lambda j: j, BF16)], ncol=ncj)[0]


def f_bwd(name, s, upraw, dact, cw, cb):
    ncj = D_FF // TN_F

    def fn(i, j, xa, xg, da, wa_ref, wg_ref, ba_ref, bg_ref, dxa_ref, dxg_ref, dwa_ref, dwg_ref, dba_ref, dbg_ref):
        ua = _conv3(xa, wa_ref) + ba_ref[...]
        ug = _conv3(xg, wg_ref) + bg_ref[...]
        sg = _sigmoid(ua)
        dug = da * ua * sg
        dua = da * ug * sg * (1.0 + ua * (1.0 - sg))
        cen = slice(HALO_F, HALO_F + TM_F)
        for du, x, w_ref, dx_ref, dw_ref, db_ref in ((dua, xa, wa_ref, dxa_ref, dwa_ref, dba_ref),
                                                      (dug, xg, wg_ref, dxg_ref, dwg_ref, dbg_ref)):
            dx = _roll_r(du, 1) * w_ref[pl.ds(0, 1), :] + du * w_ref[pl.ds(1, 1), :] + _roll_r(du, -1) * w_ref[pl.ds(2, 1), :]
            dx_ref[...] = dx[cen].astype(BF16)
            duc = du[cen]
            for k in range(3):
                dw_ref[pl.ds(k, 1), :] += jnp.sum(duc * _roll_r(x, k - 1)[cen], axis=0, keepdims=True)
            db_ref[...] += jnp.sum(duc, axis=0, keepdims=True)

    ins = [('halo', upraw, TN_F, lambda j: j, HALO_F), ('halo', upraw, TN_F, lambda j: j + ncj, HALO_F),
           ('halo', dact, TN_F, lambda j: j, HALO_F),
           ('col', cw, TN_F, lambda j: j), ('col', cw, TN_F, lambda j: j + ncj),
           ('col', cb, TN_F, lambda j: j), ('col', cb, TN_F, lambda j: j + ncj)]
    outs = [('row', D_FF, TN_F, lambda j: j, BF16), ('row', D_FF, TN_F, lambda j: j, BF16),
            ('colacc', 8, D_FF, TN_F, lambda j: j), ('colacc', 8, D_FF, TN_F, lambda j: j),
            ('colacc', 1, D_FF, TN_F, lambda j: j), ('colacc', 1, D_FF, TN_F, lambda j: j)]
    return tiled_call(name, fn, s, TM_F, ins, outs, ncol=ncj)


def _pad_w_in(w):
    z = lambda n: jnp.zeros((w.shape[0], n), w.dtype)
    return jnp.concatenate([w[:, :1728], z(64), w[:, 1728:1856], z(64), w[:, 1856:1888], z(32)], axis=1)


def _unpad_w_in(g):
    return jnp.concatenate([g[:, :1728], g[:, 1792:1920], g[:, 1984:2016]], axis=1)


def _pad_uq(w):
    w = w.reshape(MLA_Q_LORA, 4, 96)
    w = jnp.concatenate([w, jnp.zeros((MLA_Q_LORA, 4, 32), w.dtype)], axis=2).reshape(MLA_Q_LORA, 512)
    return jnp.concatenate([w, jnp.zeros((64, 512), w.dtype)], axis=0)


def _unpad_uq(g):
    return g[:MLA_Q_LORA].reshape(MLA_Q_LORA, 4, 128)[:, :, :96].reshape(MLA_Q_LORA, 384)


def _pad_ukv(w):
    w = w.reshape(MLA_KV_LORA, 4, 128)
    kn = jnp.concatenate([w[:, :, :64], jnp.zeros((MLA_KV_LORA, 4, 64), w.dtype)], axis=2).reshape(MLA_KV_LORA, 512)
    return jnp.concatenate([kn, w[:, :, 64:].reshape(MLA_KV_LORA, 256)], axis=1)


def _unpad_ukv(g):
    kn = g[:, :512].reshape(MLA_KV_LORA, 4, 128)[:, :, :64]
    v = g[:, 512:].reshape(MLA_KV_LORA, 4, 64)
    return jnp.concatenate([kn, v], axis=2).reshape(MLA_KV_LORA, 512)


def _row(v, pad_to=None):
    v = v.reshape(1, -1)
    if pad_to is not None and v.shape[1] < pad_to:
        v = jnp.concatenate([v, jnp.zeros((1, pad_to - v.shape[1]), v.dtype)], axis=1)
    return v


def local_step(x, tgt, W):
    s = x.shape[0]
    tabs = _rope_tables(s)
    gm = _group_mean_matrix(256, 64)
    alpha = DEEPNORM_ALPHA
    acts = []
    h = ln_fwd("ln_in", s, x, _row(W['ln_in_g']), _row(W['ln_in_b']))
    for l in range(DEPTH):
        A = dict(h=h)
        A['wpad'] = _pad_w_in(W['w_in'][l]).astype(BF16)
        A['wout'] = W['w_out'][l].astype(BF16)
        A['wup'] = W['ffn_w_up'][l].astype(BF16)
        A['wdown'] = W['ffn_w_down'][l].astype(BF16)
        A['wuq'] = _pad_uq(W['mla_w_uq'][l]).astype(BF16)
        A['wukv'] = _pad_ukv(W['mla_w_ukv'][l]).astype(BF16)
        A['cw'] = jnp.concatenate([W['conv_a_w'][l], jnp.zeros((1, GROUP_W), F32)], axis=0)
        A['fcw'] = jnp.concatenate([W['ffn_conv_w'][l], jnp.zeros((5, 2 * D_FF), F32)], axis=0)
        A['gq'] = jnp.tile(_row(W['qk_norm_q'][l]), (1, 4))
        A['gk'] = jnp.tile(_row(W['qk_norm_k'][l]), (1, 2))
        A['sgu_bias'] = jnp.repeat(W['sgu_b'][l].T, 64, axis=1)
        A['sgu_wt'] = jnp.swapaxes(W['sgu_w'][l], 1, 2)
        proj = mm(f"proj{l}", h, A['wpad'])
        A['proj'] = proj
        A['y_a'], o_a = a_fwd(f"a_fwd{l}", s, proj, A['cw'], _row(W['conv_a_b'][l]), _row(W['ln_a_g'][l]), _row(W['ln_a_b'][l]))
        A['bq'], A['bk'], A['bv'] = b_prep_fwd(f"b_prep{l}", s, proj, A['gq'], A['gk'], tabs['gqa'], gm)
        A['o_b'], A['lse_b'] = attn_fwd(f"b_attn{l}", s, A['bq'], A['bk'], A['bv'], HEAD_DIM ** -0.5)
        o_c = c_fwd(f"c_fwd{l}", s, proj, _row(W['sgu_ln_g'][l]), _row(W['sgu_ln_b'][l]), W['sgu_w'][l], A['sgu_bias'])
        A['dq'], A['dk'], A['dv'] = d_prep_fwd(f"d_prep{l}", s, proj, _row(W['mla_q_norm'][l], 256), _row(W['mla_kv_norm'][l]),
                                               A['wuq'], A['wukv'], tabs['mla'], tabs['kr'])
        A['o_d'], A['lse_d'] = attn_fwd(f"d_attn{l}", s, A['dq'], A['dk'], A['dv'], (MLA_NOPE + MLA_ROPE) ** -0.5)
        A['mixcat'] = jnp.concatenate([o_a, A['o_b'], o_c, A['o_d']], axis=1).astype(BF16)
        A['mix'] = mm(f"out_proj{l}", A['mixcat'], A['wout'])
        h1 = ln_fwd(f"ln_mix{l}", s, A['mix'], _row(W['ln_mix_g'][l]), _row(W['ln_mix_b'][l]), res=h, alpha=alpha)
        A['h1'] = h1
        A['upraw'] = mm(f"ffn_up{l}", h1, A['wup'])
        A['act'] = f_fwd(f"f_fwd{l}", s, A['upraw'], A['fcw'], _row(W['ffn_conv_b'][l]))
        A['f'] = mm(f"ffn_down{l}", A['act'], A['wdown'])
        if l < DEPTH - 1:
            h = ln_fwd(f"ln_ffn{l}", s, A['f'], _row(W['ln_ffn_g'][l]), _row(W['ln_ffn_b'][l]), res=h1, alpha=alpha)
        acts.append(A)

    per_layer = {n: [None] * DEPTH for n in WEIGHT_NAMES if n not in ('ln_in_g', 'ln_in_b')}
    dh = None
    loss_row = None
    for l in reversed(range(DEPTH)):
        A = acts[l]
        g_ffn, b_ffn = _row(W['ln_ffn_g'][l]), _row(W['ln_ffn_b'][l])
        if l == DEPTH - 1:
            dz, dg, db, loss_row = loss_and_ln_bwd("loss", s, A['f'], A['h1'], g_ffn, b_ffn, tgt, alpha)
        else:
            dz, dg, db = ln_bwd(f"ln_ffn_bwd{l}", s, dh, A['f'], g_ffn, res=A['h1'], alpha=alpha)
        per_layer['ln_ffn_g'][l], per_layer['ln_ffn_b'][l] = dg[0], db[0]
        per_layer['ffn_w_down'][l] = mm(f"dw_down{l}", A['act'], dz, ta=True).reshape(N_CHIPS, D_FF // N_CHIPS, D_MODEL)
        dact = mm(f"d_act{l}", dz, A['wdown'], tb=True)
        dxa, dxg, dwa, dwg, dba, dbg = f_bwd(f"f_bwd{l}", s, A['upraw'], dact, A['fcw'], _row(W['ffn_conv_b'][l]))
        per_layer['ffn_conv_w'][l] = jnp.concatenate([dwa[:3], dwg[:3]], axis=1)
        per_layer['ffn_conv_b'][l] = jnp.concatenate([dba[0], dbg[0]], axis=0)
        dup = jnp.concatenate([dxa, dxg], axis=1)
        per_layer['ffn_w_up'][l] = mm(f"dw_up{l}", A['h1'], dup, ta=True, chip_major=True)
        dh1_mm = mm(f"d_h1{l}", dup, A['wup'], tb=True)
        dz1, dg, db = _ln_mix_bwd(l, s, dz, dh1_mm, A, W, alpha)
        per_layer['ln_mix_g'][l], per_layer['ln_mix_b'][l] = dg[0], db[0]
        per_layer['w_out'][l] = mm(f"dw_out{l}", A['mixcat'], dz1, ta=True).reshape(N_CHIPS, D_MODEL // N_CHIPS, D_MODEL)
        dmixcat = mm(f"d_mixcat{l}", dz1, A['wout'], tb=True)
        dy_a, dg, db, dcb = a_bwd1(f"a_bwd1{l}", s, A['y_a'], dmixcat, _row(W['ln_a_g'][l]), _row(W['ln_a_b'][l]))
        per_layer['ln_a_g'][l], per_layer['ln_a_b'][l], per_layer['conv_a_b'][l] = dg[0], db[0], dcb[0]
        dp_a, dcw = a_bwd2(f"a_bwd2{l}", s, A['proj'], dy_a, A['cw'])
        per_layer['conv_a_w'][l] = dcw[:CONV_A_WIDTH]
        dq, dk, dv = attn_bwd(f"b_attn_bwd{l}", s, A['bq'], A['bk'], A['bv'], A['o_b'], A['lse_b'], dmixcat, 2, HEAD_DIM ** -0.5)
        dp_b, dgq, dgk = b_prep_bwd(f"b_prep_bwd{l}", s, A['proj'], A['gq'], A['gk'], tabs['gqa'], gm, dq, dk, dv)
        per_layer['qk_norm_q'][l], per_layer['qk_norm_k'][l] = dgq[0], dgk[0]
        dp_c, dg, db, dsw, dsb = c_bwd(f"c_bwd{l}", s, A['proj'], _row(W['sgu_ln_g'][l]), _row(W['sgu_ln_b'][l]), W['sgu_w'][l],
                                       A['sgu_wt'], A['sgu_bias'], dmixcat)
        per_layer['sgu_ln_g'][l], per_layer['sgu_ln_b'][l] = dg[0], db[0]
        per_layer['sgu_w'][l], per_layer['sgu_b'][l] = dsw, dsb[:, :4].T
        dq, dk, dv = attn_bwd(f"d_attn_bwd{l}", s, A['dq'], A['dk'], A['dv'], A['o_d'], A['lse_d'], dmixcat, 6,
                              (MLA_NOPE + MLA_ROPE) ** -0.5)
        dp_d, dgq, dgkv, dwuq, dwukv = d_prep_bwd(f"d_prep_bwd{l}", s, A['proj'], _row(W['mla_q_norm'][l], 256),
                                                  _row(W['mla_kv_norm'][l]), A['wuq'], A['wukv'], tabs['mla'], tabs['kr'],
                                                  dq, dk, dv)
        per_layer['mla_q_norm'][l], per_layer['mla_kv_norm'][l] = dgq[0, :MLA_Q_LORA], dgkv[0]
        per_layer['mla_w_uq'][l], per_layer['mla_w_ukv'][l] = _unpad_uq(dwuq), _unpad_ukv(dwukv)
        dproj = jnp.concatenate([dp_a, dp_b, dp_c, dp_d], axis=1)
        dw_in = _unpad_w_in(mm(f"dw_in{l}", A['h'], dproj, ta=True))
        per_layer['w_in'][l] = dw_in.reshape(D_MODEL, N_CHIPS, D_IN_PROJ // N_CHIPS).transpose(1, 0, 2)
        dh = mm(f"d_h{l}", dproj, A['wpad'], tb=True, add=dz1, add_scale=alpha)
    dx, dg, db = ln_bwd("ln_in_bwd", s, dh, x, _row(W['ln_in_g']))
    G = dict(per_layer)
    G['ln_in_g'], G['ln_in_b'] = dg[0], db[0]
    return loss_row, dx, G


def _ln_mix_bwd(l, s, dz, dh1_mm, A, W, alpha):
    ins = [('row', dz, D_MODEL, 0), ('row', dh1_mm, D_MODEL, 0), ('row', A['mix'], D_MODEL, 0), ('row', A['h'], D_MODEL, 0),
           ('full', _row(W['ln_mix_g'][l]))]

    def fn(i, j, dzv, dmv, xv, rv, g_ref, o_ref, dg_ref, db_ref):
        xh, r = _ln_stats(alpha * rv + xv)
        d, dg, db = _ln_bwd(alpha * dzv + dmv, xh, r, g_ref[...])
        o_ref[...] = d
        dg_ref[...] += dg
        db_ref[...] += db

    return tiled_call(f"ln_mix_bwd{l}", fn, s, TM, ins, [('row', D_MODEL, D_MODEL, 0, F32), ('acc', (1, D_MODEL)),
                                                         ('acc', (1, D_MODEL))])


ANY = pl.BlockSpec(memory_space=pl.ANY)


def _pos():
    return lax.axis_index("x"), lax.axis_index("y"), lax.axis_index("c")


def _rcopy(src, dst, ssem, rsem, dev):
    return pltpu.make_async_remote_copy(src_ref=src, dst_ref=dst, send_sem=ssem, recv_sem=rsem, device_id=dev,
                                        device_id_type=MESH)


def gather_over_chips(name, parts):
    n = len(parts)

    def body(*refs):
        srcs, dsts = refs[:n], refs[n:2 * n]
        ssems, rsems, lsems = refs[2 * n:]
        x, y, c = _pos()
        me = 2 * x + y
        chips = [(1 - x, y), (x, 1 - y), (1 - x, 1 - y)]
        local = [pltpu.make_async_copy(srcs[t], dsts[t].at[me], lsems.at[t]) for t in range(n)]
        for cp in local:
            cp.start()
        sends = []
        for j, (px, py) in enumerate(chips):
            for t in range(n):
                cp = _rcopy(srcs[t], dsts[t].at[me], ssems.at[n * j + t], rsems.at[n * j + t], (px, py, c))
                cp.start()
                sends.append(cp)
        for j, (px, py) in enumerate(chips):
            for t in range(n):
                _rcopy(srcs[t], dsts[t].at[2 * px + py], ssems.at[n * j + t], rsems.at[n * j + t], (px, py, c)).wait_recv()
        for cp in sends:
            cp.wait_send()
        for cp in local:
            cp.wait()

    return pl.pallas_call(
        body, name=name, in_specs=[ANY] * n, out_specs=[ANY] * n,
        out_shape=[jax.ShapeDtypeStruct((N_CHIPS,) + p.shape, p.dtype) for p in parts],
        scratch_shapes=[pltpu.SemaphoreType.DMA((3 * n,)), pltpu.SemaphoreType.DMA((3 * n,)), pltpu.SemaphoreType.DMA((n,))],
    )(*parts)


def pair_exchange(name, xs):
    n = len(xs)

    def body(*refs):
        srcs, dsts, (ssems, rsems) = refs[:n], refs[n:2 * n], refs[2 * n:]
        x, y, c = _pos()
        cps = []
        for t in range(n):
            a2 = xs[t].shape[1] // 2
            cp = _rcopy(srcs[t].at[:, pl.ds((1 - c) * a2, a2), :], dsts[t], ssems.at[t], rsems.at[t], (x, y, 1 - c))
            cp.start()
            cps.append(cp)
        for cp in cps:
            cp.wait()

    return pl.pallas_call(
        body, name=name, in_specs=[ANY] * n, out_specs=[ANY] * n,
        out_shape=[jax.ShapeDtypeStruct((a.shape[0], a.shape[1] // 2, a.shape[2]), a.dtype) for a in xs],
        scratch_shapes=[pltpu.SemaphoreType.DMA((n,)), pltpu.SemaphoreType.DMA((n,))])(*xs)


def chip_exchange(name, ps):
    n = len(ps)

    def body(*refs):
        srcs, dsts = refs[:n], refs[n:2 * n]
        ssems, rsems, lsems = refs[2 * n:]
        x, y, c = _pos()
        me = 2 * x + y
        chips = [(1 - x, y), (x, 1 - y), (1 - x, 1 - y)]
        local = [pltpu.make_async_copy(srcs[t].at[me], dsts[t].at[me], lsems.at[t]) for t in range(n)]
        for cp in local:
            cp.start()
        sends = []
        for j, (px, py) in enumerate(chips):
            for t in range(n):
                cp = _rcopy(srcs[t].at[2 * px + py], dsts[t].at[me], ssems.at[n * j + t], rsems.at[n * j + t], (px, py, c))
                cp.start()
                sends.append(cp)
        for j, (px, py) in enumerate(chips):
            for t in range(n):
                _rcopy(srcs[t].at[me], dsts[t].at[2 * px + py], ssems.at[n * j + t], rsems.at[n * j + t], (px, py, c)).wait_recv()
        for cp in sends:
            cp.wait_send()
        for cp in local:
            cp.wait()

    return pl.pallas_call(
        body, name=name, in_specs=[ANY] * n, out_specs=[ANY] * n,
        out_shape=[jax.ShapeDtypeStruct(p.shape, p.dtype) for p in ps],
        scratch_shapes=[pltpu.SemaphoreType.DMA((3 * n,)), pltpu.SemaphoreType.DMA((3 * n,)), pltpu.SemaphoreType.DMA((n,))])(*ps)


def half_assemble(name, fs, groups):
    n = len(fs)
    place = {}
    for o, members in enumerate(groups):
        for l, t in enumerate(members):
            place[t] = (o, l if len(members) > 1 else None)

    def body(*refs):
        srcs, dsts = refs[:n], refs[n:n + len(groups)]
        ssems, rsems, lsems = refs[n + len(groups):]
        x, y, c = _pos()

        def rows(t, half):
            o, l = place[t]
            a2 = fs[t].shape[0]
            d = dsts[o] if l is None else dsts[o].at[l]
            return d.at[pl.ds(half * a2, a2), :]

        local = [pltpu.make_async_copy(srcs[t], rows(t, c), lsems.at[t]) for t in range(n)]
        for cp in local:
            cp.start()
        sends = [_rcopy(srcs[t], rows(t, c), ssems.at[t], rsems.at[t], (x, y, 1 - c)) for t in range(n)]
        for cp in sends:
            cp.start()
        for t in range(n):
            _rcopy(srcs[t], rows(t, 1 - c), ssems.at[t], rsems.at[t], (x, y, 1 - c)).wait_recv()
        for cp in sends:
            cp.wait_send()
        for cp in local:
            cp.wait()

    out_shape = []
    for members in groups:
        a2, b = fs[members[0]].shape
        out_shape.append(jax.ShapeDtypeStruct(((len(members),) if len(members) > 1 else ()) + (2 * a2, b), F32))
    return pl.pallas_call(
        body, name=name, in_specs=[ANY] * n, out_specs=[ANY] * len(groups), out_shape=out_shape,
        scratch_shapes=[pltpu.SemaphoreType.DMA((n,)), pltpu.SemaphoreType.DMA((n,)), pltpu.SemaphoreType.DMA((n,))])(*fs)


def gather_over_devices(name, a):
    def body(a_ref, o_ref, ssems, rsems, lsem):
        x, y, c = _pos()
        me = 4 * x + 2 * y + c
        loc = pltpu.make_async_copy(a_ref, o_ref.at[me], lsem)
        loc.start()
        peers = [((x + (k >> 2)) % 2, (y + ((k >> 1) & 1)) % 2, (c + (k & 1)) % 2) for k in range(1, N_DEV)]
        sends = []
        for j, dev in enumerate(peers):
            cp = _rcopy(a_ref, o_ref.at[me], ssems.at[j], rsems.at[j], dev)
            cp.start()
            sends.append(cp)
        for j, (px, py, pc) in enumerate(peers):
            _rcopy(a_ref, o_ref.at[4 * px + 2 * py + pc], ssems.at[j], rsems.at[j], (px, py, pc)).wait_recv()
        for cp in sends:
            cp.wait_send()
        loc.wait()

    return pl.pallas_call(body, name=name, in_specs=[ANY], out_specs=ANY, out_shape=jax.ShapeDtypeStruct((N_DEV,) + a.shape, a.dtype),
                          scratch_shapes=[pltpu.SemaphoreType.DMA((N_DEV - 1,)), pltpu.SemaphoreType.DMA((N_DEV - 1,)),
                                          pltpu.SemaphoreType.DMA(())])(a)


LANES = 1024
ROW_TILE_BYTES = 2 * 1024 * 1024


def _row_tile(r, row_bytes):
    if r * row_bytes <= ROW_TILE_BYTES:
        return r
    best = None
    for t in range(16, r, 16):
        if r % t == 0 and t * row_bytes <= ROW_TILE_BYTES:
            best = t
    assert best is not None, (r, row_bytes)
    return best


def pair_sum(name, x, r1, c_arr, out_dtype):
    q, a2, b = r1.shape
    tr = _row_tile(a2, b * 4)
    nb = a2 // tr

    def body(c_ref, x_ref, r_ref, o_ref):
        o_ref[...] = (x_ref[...] + r_ref[...]).astype(out_dtype)

    grid_spec = pltpu.PrefetchScalarGridSpec(
        num_scalar_prefetch=1, grid=(q, nb),
        in_specs=[pl.BlockSpec((None, tr, b), lambda k, i, c_ref: (k, c_ref[0] * nb + i, 0)),
                  pl.BlockSpec((None, tr, b), lambda k, i, c_ref: (k, i, 0))],
        out_specs=pl.BlockSpec((None, tr, b), lambda k, i, c_ref: (k, i, 0)))
    return pl.pallas_call(body, name=name, grid_spec=grid_spec, out_shape=jax.ShapeDtypeStruct(r1.shape, out_dtype),
                          compiler_params=_cparams(("parallel", "parallel")))(c_arr, x, r1)


def sum_leading(name, a):
    k, r, b = a.shape
    tr = _row_tile(r, k * b * a.dtype.itemsize)

    def body(a_ref, o_ref):
        acc = a_ref[0].astype(F32)
        for q in range(1, k):
            acc = acc + a_ref[q].astype(F32)
        o_ref[...] = acc

    return pl.pallas_call(body, name=name, grid=(r // tr,), in_specs=[pl.BlockSpec((k, tr, b), lambda i: (0, i, 0))],
                          out_specs=pl.BlockSpec((tr, b), lambda i: (i, 0)), out_shape=jax.ShapeDtypeStruct((r, b), F32),
                          compiler_params=_cparams(("parallel",)))(a)


def adamw(name, g, w, m, v):
    r, b = g.shape
    tr = _row_tile(r, b * 4)

    def body(g_ref, w_ref, m_ref, v_ref, d_ref, nm_ref, nv_ref):
        gv = g_ref[...]
        mn = ADAM_B1 * m_ref[...] + (1.0 - ADAM_B1) * gv
        vn = ADAM_B2 * v_ref[...] + (1.0 - ADAM_B2) * (gv * gv)
        m_hat = mn / (1.0 - ADAM_B1 ** ADAM_STEP)
        v_hat = vn / (1.0 - ADAM_B2 ** ADAM_STEP)
        d_ref[...] = -ADAM_LR * (m_hat / (jnp.sqrt(v_hat) + ADAM_EPS) + ADAM_WD * w_ref[...])
        nm_ref[...] = mn
        nv_ref[...] = vn

    spec = pl.BlockSpec((tr, b), lambda i: (i, 0))
    sds = jax.ShapeDtypeStruct(g.shape, F32)
    return pl.pallas_call(body, name=name, grid=(r // tr,), in_specs=[spec] * 4, out_specs=[spec] * 3, out_shape=[sds] * 3,
                          compiler_params=_cparams(("parallel",)))(g, w, m, v)


def _pack(arrs, rows, dtype=F32):
    flat = jnp.concatenate([a.reshape(-1).astype(dtype) for a in arrs])
    pad = rows * LANES - flat.shape[0]
    assert pad >= 0
    if pad:
        flat = jnp.concatenate([flat, jnp.zeros((pad,), dtype)])
    return flat.reshape(rows, LANES)


def _unpack(pack, shapes):
    flat = pack.reshape(-1)
    out, off = [], 0
    for shp in shapes:
        n = int(np.prod(shp))
        out.append(flat[off:off + n].reshape(shp))
        off += n
    return out


BIG_NAMES = ['w_in', 'w_out', 'ffn_w_up', 'ffn_w_down']
MED_NAMES = [n for n in SHARDED_NAMES if n not in BIG_NAMES]
MED_ROWS = 96
SMALL_ROWS = 160


def kernel(x, ln_in_g, ln_in_b, w_in, conv_a_w, conv_a_b, ln_a_g, ln_a_b, qk_norm_q, qk_norm_k, sgu_ln_g, sgu_ln_b, sgu_w, sgu_b, mla_q_norm, mla_w_uq, mla_kv_norm, mla_w_ukv, w_out, ln_mix_g, ln_mix_b, ffn_w_up, ffn_conv_w, ffn_conv_b, ffn_w_down, ln_ffn_g, ln_ffn_b, loss_target, m_ln_in_g, m_ln_in_b, m_w_in, m_conv_a_w, m_conv_a_b, m_ln_a_g, m_ln_a_b, m_qk_norm_q, m_qk_norm_k, m_sgu_ln_g, m_sgu_ln_b, m_sgu_w, m_sgu_b, m_mla_q_norm, m_mla_w_uq, m_mla_kv_norm, m_mla_w_ukv, m_w_out, m_ln_mix_g, m_ln_mix_b, m_ffn_w_up, m_ffn_conv_w, m_ffn_conv_b, m_ffn_w_down, m_ln_ffn_g, m_ln_ffn_b, v_ln_in_g, v_ln_in_b, v_w_in, v_conv_a_w, v_conv_a_b, v_ln_a_g, v_ln_a_b, v_qk_norm_q, v_qk_norm_k, v_sgu_ln_g, v_sgu_ln_b, v_sgu_w, v_sgu_b, v_mla_q_norm, v_mla_w_uq, v_mla_kv_norm, v_mla_w_ukv, v_w_out, v_ln_mix_g, v_ln_mix_b, v_ffn_w_up, v_ffn_conv_w, v_ffn_conv_b, v_ffn_w_down, v_ln_ffn_g, v_ln_ffn_b):
    loc = dict(locals())
    w_loc = {n: loc[n] for n in WEIGHT_NAMES}
    m_loc = {n: loc["m_" + n] for n in WEIGHT_NAMES}
    v_loc = {n: loc["v_" + n] for n in WEIGHT_NAMES}
    c_arr = lax.axis_index("c").astype(jnp.int32).reshape(1)

    parts = [w_loc[n].astype(BF16) if n in BF16_WIRE else w_loc[n] for n in SHARDED_NAMES]
    W = {n: w_loc[n] for n in REPL_NAMES}
    for n, g in zip(SHARDED_NAMES, gather_over_chips("gather_weights", parts)):
        W[n] = jnp.concatenate([g[q] for q in range(N_CHIPS)], axis=SHARDED_AXIS[n])

    loss_row, dx, G = local_step(x[0], loss_target[0], W)
    loss = lax.psum(loss_row[0, 0], ("x", "y", "c"))

    med = []
    for q in range(N_CHIPS):
        pieces = [lax.slice_in_dim(jnp.stack(G[n]), q * w_loc[n].shape[2], (q + 1) * w_loc[n].shape[2], axis=2) for n in MED_NAMES]
        med.append(_pack(pieces, MED_ROWS))
    xs = [G[n][l] for n in BIG_NAMES for l in range(DEPTH)] + [jnp.stack(med)]
    r1 = pair_exchange("pair_exchange", xs)
    ps = [pair_sum(f"pair_sum{t}", xv, rv, c_arr, BF16) for t, (xv, rv) in enumerate(zip(xs, r1))]
    cs = chip_exchange("chip_exchange", ps)
    fs = [sum_leading(f"chip_sum{t}", cv) for t, cv in enumerate(cs)]
    groups = [[DEPTH * i + l for l in range(DEPTH)] for i in range(len(BIG_NAMES))] + [[DEPTH * len(BIG_NAMES)]]
    *g_big, g_med = half_assemble("half_exchange", fs, groups)

    small_pack = _pack([jnp.stack(G[n]) if isinstance(G[n], list) else G[n] for n in REPL_NAMES], SMALL_ROWS)
    g_small = sum_leading("small_sum", gather_over_devices("small_gather", small_pack))

    outs = {k: {} for k in ("grad", "delta", "new_m", "new_v")}
    for n, g in zip(BIG_NAMES, g_big):
        shp = w_loc[n].shape
        flat = (shp[0] * shp[1], shp[2])
        res = adamw(f"adamw_{n}", g.reshape(flat), *[d[n].reshape(flat) for d in (w_loc, m_loc, v_loc)])
        for k, v in zip(("grad", "delta", "new_m", "new_v"), (g, *res)):
            outs[k][n] = v.reshape(shp)
    for names, g, rows in ((MED_NAMES, g_med, MED_ROWS), (REPL_NAMES, g_small, SMALL_ROWS)):
        res = adamw(f"adamw_{names[0]}_pack", g, *[_pack([d[n] for n in names], rows) for d in (w_loc, m_loc, v_loc)])
        for k, v in zip(("grad", "delta", "new_m", "new_v"), (g, *res)):
            outs[k].update(zip(names, _unpack(v, [w_loc[n].shape for n in names])))
    return (loss, dx[None], *[outs["grad"][n] for n in WEIGHT_NAMES], *[outs["delta"][n] for n in WEIGHT_NAMES],
            *[outs["new_m"][n] for n in WEIGHT_NAMES], *[outs["new_v"][n] for n in WEIGHT_NAMES])
```

```python
import functools
import math

import jax
import jax.numpy as jnp
import numpy as np
from jax import lax
from jax.experimental import pallas as pl
from jax.experimental.pallas import tpu as pltpu

F32 = jnp.float32
BF16 = jnp.bfloat16

D_MODEL = 1024
DEPTH = 2
GRID_W = 64
GROUP_W = 256
HEAD_DIM = 64
CONV_A_WIDTH = 31
CHUNK = 128
MLA_Q_LORA = 192
MLA_KV_LORA = 128
MLA_NOPE = 64
MLA_ROPE = 32
D_FF = 2816
ROPE_THETA = 10000.0
DEEPNORM_ALPHA = (2 * DEPTH) ** 0.25
LN_EPS = 1e-5
RMS_EPS = 1e-6
D_IN_PROJ = 1888
PROJ_W = 2048

ADAM_LR = 0.001
ADAM_B1 = 0.9
ADAM_B2 = 0.999
ADAM_EPS = 1e-08
ADAM_WD = 0.01
ADAM_STEP = 10

VMEM_LIMIT = 56 * 1024 * 1024
MESH = pl.DeviceIdType.MESH
N_CHIPS = 4
N_DEV = 8

WEIGHT_NAMES = ['ln_in_g', 'ln_in_b', 'w_in', 'conv_a_w', 'conv_a_b', 'ln_a_g', 'ln_a_b', 'qk_norm_q', 'qk_norm_k',
                'sgu_ln_g', 'sgu_ln_b', 'sgu_w', 'sgu_b', 'mla_q_norm', 'mla_w_uq', 'mla_kv_norm', 'mla_w_ukv', 'w_out',
                'ln_mix_g', 'ln_mix_b', 'ffn_w_up', 'ffn_conv_w', 'ffn_conv_b', 'ffn_w_down', 'ln_ffn_g', 'ln_ffn_b']
SHARDED_AXIS = {'w_in': 2, 'conv_a_w': 2, 'mla_w_uq': 2, 'mla_w_ukv': 2, 'w_out': 1, 'ffn_w_up': 2, 'ffn_conv_w': 2,
                'ffn_w_down': 1}
SHARDED_NAMES = [n for n in WEIGHT_NAMES if n in SHARDED_AXIS]
REPL_NAMES = [n for n in WEIGHT_NAMES if n not in SHARDED_AXIS]
BF16_WIRE = ('w_in', 'mla_w_uq', 'mla_w_ukv', 'w_out', 'ffn_w_up', 'ffn_w_down')


def _cparams(sem):
    return pltpu.CompilerParams(dimension_semantics=sem, vmem_limit_bytes=VMEM_LIMIT)


def _pick(n, cands):
    for c in cands:
        if n % c == 0:
            return c
    return n


def mm(name, a, b, *, ta=False, tb=False, add=None, add_scale=1.0, out_dtype=F32, chip_major=False):
    m, k = (a.shape[1], a.shape[0]) if ta else a.shape
    n = b.shape[0] if tb else b.shape[1]
    assert (b.shape[1] if tb else b.shape[0]) == k
    tm = _pick(m, (512, 256, 128))
    tn = n // N_CHIPS if chip_major else _pick(n, (1024, 1408, 512, 256, 128))
    tk = _pick(k, (1024, 1408, 512, 256, 128))
    nk = k // tk
    a_spec = pl.BlockSpec((tk, tm), lambda i, j, kk: (kk, i)) if ta else pl.BlockSpec((tm, tk), lambda i, j, kk: (i, kk))
    b_spec = pl.BlockSpec((tn, tk), lambda i, j, kk: (j, kk)) if tb else pl.BlockSpec((tk, tn), lambda i, j, kk: (kk, j))
    in_specs = [a_spec, b_spec]
    args = [a, b]
    if add is not None:
        in_specs.append(pl.BlockSpec((tm, tn), lambda i, j, kk: (i, j)))
        args.append(add)
    dims = (((0 if ta else 1,), (1 if tb else 0,)), ((), ()))

    def body(*refs):
        a_ref, b_ref = refs[0], refs[1]
        add_ref = refs[2] if add is not None else None
        o_ref, acc_ref = refs[-2], refs[-1]
        kk = pl.program_id(2)
        part = lax.dot_general(a_ref[...].astype(BF16), b_ref[...].astype(BF16), dims, preferred_element_type=F32)

        @pl.when(kk == 0)
        def _():
            acc_ref[...] = part

        @pl.when(kk > 0)
        def _():
            acc_ref[...] += part

        @pl.when(kk == nk - 1)
        def _():
            r = acc_ref[...]
            if add_ref is not None:
                r = r + add_scale * add_ref[...].astype(F32)
            o_ref[...] = r.astype(out_dtype)

    if chip_major:
        out_spec = pl.BlockSpec((None, tm, tn), lambda i, j, kk: (j, i, 0))
        out_shape = jax.ShapeDtypeStruct((N_CHIPS, m, tn), out_dtype)
    else:
        out_spec = pl.BlockSpec((tm, tn), lambda i, j, kk: (i, j))
        out_shape = jax.ShapeDtypeStruct((m, n), out_dtype)
    return pl.pallas_call(
        body, name=name, grid=(m // tm, n // tn, nk),
        in_specs=in_specs, out_specs=out_spec, out_shape=out_shape,
        scratch_shapes=[pltpu.VMEM((tm, tn), F32)],
        compiler_params=_cparams(("parallel", "parallel", "arbitrary")),
    )(*args)


def _cb(cb):
    return cb if callable(cb) else (lambda j, _c=cb: _c)


def tiled_call(name, fn, s, tm, ins, outs, ncol=1):
    nrow = s // tm
    in_specs, args, kinds = [], [], []
    for it in ins:
        kind = it[0]
        if kind == 'row':
            _, arr, w, cb = it
            in_specs.append(pl.BlockSpec((tm, w), lambda j, i, _c=_cb(cb): (i, _c(j))))
            args.append(arr)
            kinds.append(('row',))
        elif kind == 'halo':
            _, arr, w, cb, h = it
            r = tm // h
            nh = s // h
            in_specs.append(pl.BlockSpec((h, w), lambda j, i, _c=_cb(cb), _r=r: (jnp.maximum(i * _r - 1, 0), _c(j))))
            in_specs.append(pl.BlockSpec((tm, w), lambda j, i, _c=_cb(cb): (i, _c(j))))
            in_specs.append(pl.BlockSpec((h, w), lambda j, i, _c=_cb(cb), _r=r, _n=nh: (jnp.minimum((i + 1) * _r, _n - 1), _c(j))))
            args += [arr, arr, arr]
            kinds.append(('halo',))
        elif kind == 'full':
            arr = it[1]
            in_specs.append(pl.BlockSpec(arr.shape, lambda j, i, _n=arr.ndim: (0,) * _n))
            args.append(arr)
            kinds.append(('ref',))
        elif kind == 'col':
            _, arr, w, cb = it
            in_specs.append(pl.BlockSpec((arr.shape[0], w), lambda j, i, _c=_cb(cb): (0, _c(j))))
            args.append(arr)
            kinds.append(('ref',))
        elif kind == 'hm':
            arr = it[1]
            in_specs.append(pl.BlockSpec((arr.shape[0], tm, arr.shape[2]), lambda j, i: (0, i, 0)))
            args.append(arr)
            kinds.append(('ref',))
        else:
            raise ValueError(kind)
    out_specs, out_shapes, okinds = [], [], []
    for ot in outs:
        kind = ot[0]
        if kind == 'row':
            _, wt, w, cb, dt = ot
            out_specs.append(pl.BlockSpec((tm, w), lambda j, i, _c=_cb(cb): (i, _c(j))))
            out_shapes.append(jax.ShapeDtypeStruct((s, wt), dt))
            okinds.append('row')
        elif kind == 'hm':
            _, hh, d, dt = ot
            out_specs.append(pl.BlockSpec((hh, tm, d), lambda j, i: (0, i, 0)))
            out_shapes.append(jax.ShapeDtypeStruct((hh, s, d), dt))
            okinds.append('hm')
        elif kind == 'acc':
            shape = ot[1]
            out_specs.append(pl.BlockSpec(shape, lambda j, i, _n=len(shape): (0,) * _n))
            out_shapes.append(jax.ShapeDtypeStruct(shape, F32))
            okinds.append('acc')
        elif kind == 'colacc':
            _, r, wt, w, cb = ot
            out_specs.append(pl.BlockSpec((r, w), lambda j, i, _c=_cb(cb): (0, _c(j))))
            out_shapes.append(jax.ShapeDtypeStruct((r, wt), F32))
            okinds.append('colacc')
        else:
            raise ValueError(kind)
    n_in = len(in_specs)

    def body(*refs):
        j = pl.program_id(0)
        i = pl.program_id(1)
        in_refs, out_refs = refs[:n_in], refs[n_in:]
        items, p = [], 0
        for kd in kinds:
            if kd[0] == 'row':
                items.append(in_refs[p][...])
                p += 1
            elif kd[0] == 'halo':
                prev, cen, nxt = in_refs[p][...], in_refs[p + 1][...], in_refs[p + 2][...]
                prev = jnp.where(i == 0, jnp.zeros_like(prev), prev)
                nxt = jnp.where(i == nrow - 1, jnp.zeros_like(nxt), nxt)
                items.append(jnp.concatenate([prev, cen, nxt], axis=0))
                p += 3
            else:
                items.append(in_refs[p])
                p += 1
        for o_ref, kd in zip(out_refs, okinds):
            if kd == 'acc':
                @pl.when((i == 0) & (j == 0))
                def _(o_ref=o_ref):
                    o_ref[...] = jnp.zeros_like(o_ref)
            elif kd == 'colacc':
                @pl.when(i == 0)
                def _(o_ref=o_ref):
                    o_ref[...] = jnp.zeros_like(o_ref)
        fn(i, j, *items, *out_refs)

    res = pl.pallas_call(
        body, name=name, grid=(ncol, nrow), in_specs=in_specs, out_specs=out_specs, out_shape=out_shapes,
        compiler_params=_cparams(("arbitrary", "arbitrary")),
    )(*args)
    return res


def _sigmoid(x):
    return 1.0 / (1.0 + jnp.exp(-x))


def _ln_stats(x):
    mu = jnp.mean(x, axis=1, keepdims=True)
    xc = x - mu
    var = jnp.mean(xc * xc, axis=1, keepdims=True)
    r = lax.rsqrt(var + LN_EPS)
    return xc * r, r


def _ln_bwd(dy, xh, r, g):
    dxh = dy * g
    dx = r * (dxh - jnp.mean(dxh, axis=1, keepdims=True) - xh * jnp.mean(dxh * xh, axis=1, keepdims=True))
    return dx, jnp.sum(dy * xh, axis=0, keepdims=True), jnp.sum(dy, axis=0, keepdims=True)


def _gmean(v, gm):
    hi = v.astype(BF16)
    lo = (v - hi.astype(F32)).astype(BF16)
    return jnp.dot(hi, gm, preferred_element_type=F32) + jnp.dot(lo, gm, preferred_element_type=F32)


def _roll_l(x, sh):
    return pltpu.roll(x, sh % x.shape[1], 1)


def _rope(x, c, sa, sb, sh):
    return x * c + _roll_l(x, -sh) * sa + _roll_l(x, sh) * sb


def _rope_t(dy, c, sa, sb, sh):
    return dy * c + _roll_l(dy * sa, sh) + _roll_l(dy * sb, -sh)


def _roll_r(x, s):
    return pltpu.roll(x, (-s) % x.shape[0], 0)


_GELU_C = math.sqrt(2.0 / math.pi)


def _gelu(x):
    t = jnp.tanh(_GELU_C * (x + 0.044715 * x * x * x))
    return 0.5 * x * (1.0 + t), t


def _gelu_grad(x, t):
    return 0.5 * (1.0 + t) + 0.5 * x * (1.0 - t * t) * _GELU_C * (1.0 + 3.0 * 0.044715 * x * x)


def _dot_nt(a, b):
    return lax.dot_general(a, b, (((1,), (1,)), ((), ())), preferred_element_type=F32)


def _dot_tn(a, b):
    return lax.dot_general(a, b, (((0,), (0,)), ((), ())), preferred_element_type=F32)


def _rope_block(s, d):
    t = jnp.arange(s)
    row = (t // GRID_W).astype(F32)
    col = (t % GRID_W).astype(F32)
    half = d // 4
    inv = ROPE_THETA ** (-jnp.arange(half, dtype=F32) / half)
    z = jnp.zeros((s, half), F32)
    cs, sas, sbs = [], [], []
    for pos in (row, col):
        ang = pos[:, None] * inv[None, :]
        co, si = jnp.cos(ang), jnp.sin(ang)
        cs += [co, co]
        sas += [-si, z]
        sbs += [z, si]
    return tuple(jnp.concatenate(v, axis=1) for v in (cs, sas, sbs))


def _rope_tables(s):
    gqa = tuple(jnp.tile(a, (1, 4)) for a in _rope_block(s, HEAD_DIM))
    c32, sa32, sb32 = _rope_block(s, MLA_ROPE)

    def head128(c, fill):
        return jnp.concatenate([jnp.full((s, 64), fill, F32), c, jnp.zeros((s, 32), F32)], axis=1)

    mla = tuple(jnp.tile(a, (1, 4)) for a in (head128(c32, 1.0), head128(sa32, 0.0), head128(sb32, 0.0)))
    kr = (head128(c32, 0.0), head128(sa32, 0.0), head128(sb32, 0.0))
    return dict(gqa=gqa, mla=mla, kr=kr)


def _group_mean_matrix(w, g):
    idx = np.arange(w) // g
    return jnp.asarray((idx[:, None] == idx[None, :]).astype(np.float32) / g, dtype=BF16)


TM = 512


def ln_fwd(name, s, x, g, b, res=None, alpha=1.0):
    ins = [('row', x, D_MODEL, 0)] + ([('row', res, D_MODEL, 0)] if res is not None else []) + [('full', g), ('full', b)]

    def fn(i, j, *a):
        if res is not None:
            xv, rv, g_ref, b_ref, o_ref = a
            z = alpha * rv + xv
        else:
            xv, g_ref, b_ref, o_ref = a
            z = xv
        xh, _ = _ln_stats(z)
        o_ref[...] = xh * g_ref[...] + b_ref[...]

    return tiled_call(name, fn, s, TM, ins, [('row', D_MODEL, D_MODEL, 0, F32)])[0]


def ln_bwd(name, s, dy, x, g, res=None, alpha=1.0):
    ins = [('row', dy, D_MODEL, 0), ('row', x, D_MODEL, 0)]
    ins += ([('row', res, D_MODEL, 0)] if res is not None else []) + [('full', g)]

    def fn(i, j, *a):
        a = list(a)
        dyv = a.pop(0)
        xv = a.pop(0)
        z = alpha * a.pop(0) + xv if res is not None else xv
        g_ref, dz_ref, dg_ref, db_ref = a
        xh, r = _ln_stats(z)
        dz, dg, db = _ln_bwd(dyv, xh, r, g_ref[...])
        dz_ref[...] = dz
        dg_ref[...] += dg
        db_ref[...] += db

    return tiled_call(name, fn, s, TM, ins, [('row', D_MODEL, D_MODEL, 0, F32), ('acc', (1, D_MODEL)), ('acc', (1, D_MODEL))])


def loss_and_ln_bwd(name, s, x, res, g, b, tgt, alpha):
    ins = [('row', x, D_MODEL, 0), ('row', res, D_MODEL, 0), ('row', tgt, D_MODEL, 0), ('full', g), ('full', b)]

    def fn(i, j, xv, rv, tv, g_ref, b_ref, dz_ref, dg_ref, db_ref, loss_ref):
        xh, r = _ln_stats(alpha * rv + xv)
        y = xh * g_ref[...] + b_ref[...]
        e = y - tv
        dz, dg, db = _ln_bwd(e * (1.0 / D_MODEL), xh, r, g_ref[...])
        dz_ref[...] = dz
        dg_ref[...] += dg
        db_ref[...] += db
        loss_ref[...] += jnp.sum(jnp.sum(e * e, axis=0, keepdims=True), axis=1, keepdims=True) * (0.5 / D_MODEL)

    return tiled_call(name, fn, s, TM, ins, [('row', D_MODEL, D_MODEL, 0, F32), ('acc', (1, D_MODEL)), ('acc', (1, D_MODEL)),
                                             ('acc', (1, 128))])


HALO_A = 16


def _glu(a_in):
    return a_in[:, :GROUP_W] * _sigmoid(a_in[:, GROUP_W:])


def a_fwd(name, s, proj, cw, cb, g, b):
    def fn(i, j, a_ext, cw_ref, cb_ref, g_ref, b_ref, y_ref, o_ref):
        a = _glu(a_ext)
        acc = jnp.zeros((TM, GROUP_W), F32)
        for k in range(CONV_A_WIDTH):
            acc = acc + _roll_r(a, k - 15)[HALO_A:HALO_A + TM] * cw_ref[pl.ds(k, 1), :]
        y = acc + cb_ref[...]
        y_ref[...] = y
        xh, _ = _ln_stats(y)
        z = xh * g_ref[...] + b_ref[...]
        o_ref[...] = z * _sigmoid(z)

    ins = [('halo', proj, 2 * GROUP_W, 0, HALO_A), ('full', cw), ('full', cb), ('full', g), ('full', b)]
    return tiled_call(name, fn, s, TM, ins, [('row', GROUP_W, GROUP_W, 0, F32), ('row', GROUP_W, GROUP_W, 0, F32)])


def a_bwd1(name, s, y, dmix, g, b):
    def fn(i, j, yv, do, g_ref, b_ref, dy_ref, dg_ref, db_ref, dcb_ref):
        xh, r = _ln_stats(yv)
        z = xh * g_ref[...] + b_ref[...]
        sg = _sigmoid(z)
        dz = do * sg * (1.0 + z * (1.0 - sg))
        dy, dg, db = _ln_bwd(dz, xh, r, g_ref[...])
        dy_ref[...] = dy
        dg_ref[...] += dg
        db_ref[...] += db
        dcb_ref[...] += jnp.sum(dy, axis=0, keepdims=True)

    ins = [('row', y, GROUP_W, 0), ('row', dmix, GROUP_W, 0), ('full', g), ('full', b)]
    return tiled_call(name, fn, s, TM, ins, [('row', GROUP_W, GROUP_W, 0, F32), ('acc', (1, GROUP_W)), ('acc', (1, GROUP_W)),
                                             ('acc', (1, GROUP_W))])


def a_bwd2(name, s, proj, dy, cw):
    def fn(i, j, a_ext, dy_ext, cw_ref, da_ref, dcw_ref):
        a1, a2 = a_ext[:, :GROUP_W], a_ext[:, GROUP_W:]
        sg = _sigmoid(a2)
        a = a1 * sg
        dyc = dy_ext[HALO_A:HALO_A + TM]
        da = jnp.zeros((TM, GROUP_W), F32)
        for k in range(CONV_A_WIDTH):
            da = da + _roll_r(dy_ext, 15 - k)[HALO_A:HALO_A + TM] * cw_ref[pl.ds(k, 1), :]
            dcw_ref[pl.ds(k, 1), :] += jnp.sum(dyc * _roll_r(a, k - 15)[HALO_A:HALO_A + TM], axis=0, keepdims=True)
        a1c, sgc = a1[HALO_A:HALO_A + TM], sg[HALO_A:HALO_A + TM]
        da_ref[...] = jnp.concatenate([da * sgc, da * a1c * sgc * (1.0 - sgc)], axis=1).astype(BF16)

    ins = [('halo', proj, 2 * GROUP_W, 0, HALO_A), ('halo', dy, GROUP_W, 0, HALO_A), ('full', cw)]
    return tiled_call(name, fn, s, TM, ins, [('row', 2 * GROUP_W, 2 * GROUP_W, 0, BF16), ('acc', (32, GROUP_W))])


def b_prep_fwd(name, s, proj, gq, gk, tabs, gm):
    c, sa, sb = tabs

    def fn(i, j, qkv, cv, sav, sbv, gq_ref, gk_ref, gm_ref, q_ref, k_ref, v_ref):
        q, k, v = qkv[:, :256], qkv[:, 256:384], qkv[:, 384:]
        gmv = gm_ref[...]
        qn = q * lax.rsqrt(_gmean(q * q, gmv) + RMS_EPS) * gq_ref[...]
        kn = k * lax.rsqrt(_gmean(k * k, gmv[:128, :128]) + RMS_EPS) * gk_ref[...]
        qr = _rope(qn, cv, sav, sbv, 16)
        kr = _rope(kn, cv[:, :128], sav[:, :128], sbv[:, :128], 16)
        for h in range(4):
            q_ref[h] = qr[:, h * 64:(h + 1) * 64].astype(BF16)
        for h in range(2):
            k_ref[h] = kr[:, h * 64:(h + 1) * 64].astype(BF16)
            v_ref[h] = v[:, h * 64:(h + 1) * 64].astype(BF16)

    ins = [('row', proj, 512, 1), ('row', c, 256, 0), ('row', sa, 256, 0), ('row', sb, 256, 0), ('full', gq), ('full', gk),
           ('full', gm)]
    return tiled_call(name, fn, s, TM, ins, [('hm', 4, 64, BF16), ('hm', 2, 64, BF16), ('hm', 2, 64, BF16)])


def b_prep_bwd(name, s, proj, gq, gk, tabs, gm, dq_hm, dk_hm, dv_hm):
    c, sa, sb = tabs

    def fn(i, j, qkv, cv, sav, sbv, gq_ref, gk_ref, gm_ref, dq_ref, dk_ref, dv_ref, dp_ref, dgq_ref, dgk_ref):
        q, k = qkv[:, :256], qkv[:, 256:384]
        gmv = gm_ref[...]
        dqr = jnp.concatenate([dq_ref[h] for h in range(4)], axis=1)
        dkr = jnp.concatenate([dk_ref[h] for h in range(2)], axis=1)
        dv = jnp.concatenate([dv_ref[h] for h in range(2)], axis=1)
        outs = []
        for x, dxr, g_ref, gmx, w, dg_ref in ((q, dqr, gq_ref, gmv, 256, dgq_ref), (k, dkr, gk_ref, gmv[:128, :128], 128, dgk_ref)):
            dn = _rope_t(dxr, cv[:, :w], sav[:, :w], sbv[:, :w], 16)
            r = lax.rsqrt(_gmean(x * x, gmx) + RMS_EPS)
            gv = g_ref[...]
            dx = gv * r * dn - x * (r * r * r) * _gmean(x * gv * dn, gmx)
            dgt = jnp.sum(x * r * dn, axis=0, keepdims=True)
            dg = dgt[:, 0:64]
            for h in range(1, w // 64):
                dg = dg + dgt[:, h * 64:(h + 1) * 64]
            dg_ref[...] += dg
            outs.append(dx)
        dp_ref[...] = jnp.concatenate(outs + [dv], axis=1).astype(BF16)

    ins = [('row', proj, 512, 1), ('row', c, 256, 0), ('row', sa, 256, 0), ('row', sb, 256, 0), ('full', gq), ('full', gk),
           ('full', gm), ('hm', dq_hm), ('hm', dk_hm), ('hm', dv_hm)]
    return tiled_call(name, fn, s, TM, ins, [('row', 512, 512, 0, BF16), ('acc', (1, 64)), ('acc', (1, 64))])


TQ = 256


def attn_fwd(name, s, q_hm, k_hm, v_hm, scale):
    dk = q_hm.shape[2]
    kpb = k_hm.shape[0] // 2

    def body(q_ref, k_ref, v_ref, o_ref, l_ref):
        outs = []
        for g in range(2):
            kv = g if kpb == 2 else 0
            sc = _dot_nt(q_ref[g], k_ref[kv]) * scale
            m = jnp.max(sc, axis=1, keepdims=True)
            p = jnp.exp(sc - m)
            l = jnp.sum(p, axis=1, keepdims=True)
            o = jnp.dot(p.astype(BF16), v_ref[kv], preferred_element_type=F32)
            outs.append(o / l)
            l_ref[g] = m + jnp.log(l)
        o_ref[...] = jnp.concatenate(outs, axis=1)

    return pl.pallas_call(
        body, name=name, grid=(2, s // TQ),
        in_specs=[pl.BlockSpec((2, TQ, dk), lambda p, i: (p, i, 0)),
                  pl.BlockSpec((kpb, s, dk), lambda p, i: (p, 0, 0)),
                  pl.BlockSpec((kpb, s, 64), lambda p, i: (p, 0, 0))],
        out_specs=[pl.BlockSpec((TQ, 128), lambda p, i: (i, p)), pl.BlockSpec((2, TQ, 1), lambda p, i: (p, i, 0))],
        out_shape=[jax.ShapeDtypeStruct((s, 256), F32), jax.ShapeDtypeStruct((4, s, 1), F32)],
        compiler_params=_cparams(("arbitrary", "arbitrary")),
    )(q_hm, k_hm, v_hm)


def attn_bwd(name, s, q_hm, k_hm, v_hm, o, lse, dmixcat, dcol, scale):
    dk = q_hm.shape[2]
    nkv = k_hm.shape[0]
    shared = nkv == 2
    kvi = (lambda p, g, i: (p, 0, 0)) if shared else (lambda p, g, i: (2 * p + g, 0, 0))

    def body(q_ref, k_ref, v_ref, o_ref, l_ref, do_ref, dq_ref, dk_ref, dv_ref):
        g = pl.program_id(1)
        i = pl.program_id(2)
        first = (i == 0) & (g == 0) if shared else (i == 0)

        @pl.when(first)
        def _():
            dk_ref[...] = jnp.zeros_like(dk_ref)
            dv_ref[...] = jnp.zeros_like(dv_ref)

        dob, ob = do_ref[...], o_ref[...]
        do = jnp.where(g == 0, dob[:, :64], dob[:, 64:])
        ov = jnp.where(g == 0, ob[:, :64], ob[:, 64:])
        q, k, v = q_ref[0], k_ref[0], v_ref[0]
        sc = _dot_nt(q, k) * scale
        p = jnp.exp(sc - l_ref[0])
        dp = _dot_nt(do.astype(BF16), v)
        delta = jnp.sum(do * ov, axis=1, keepdims=True)
        ds = (p * (dp - delta) * scale).astype(BF16)
        dq_ref[0] = jnp.dot(ds, k, preferred_element_type=F32)
        dk_ref[0] += _dot_tn(ds, q)
        dv_ref[0] += _dot_tn(p.astype(BF16), do.astype(BF16))

    return pl.pallas_call(
        body, name=name, grid=(2, 2, s // TQ),
        in_specs=[pl.BlockSpec((1, TQ, dk), lambda p, g, i: (2 * p + g, i, 0)),
                  pl.BlockSpec((1, s, dk), kvi), pl.BlockSpec((1, s, 64), kvi),
                  pl.BlockSpec((TQ, 128), lambda p, g, i: (i, p)),
                  pl.BlockSpec((1, TQ, 1), lambda p, g, i: (2 * p + g, i, 0)),
                  pl.BlockSpec((TQ, 128), lambda p, g, i: (i, dcol + p))],
        out_specs=[pl.BlockSpec((1, TQ, dk), lambda p, g, i: (2 * p + g, i, 0)),
                   pl.BlockSpec((1, s, dk), kvi), pl.BlockSpec((1, s, 64), kvi)],
        out_shape=[jax.ShapeDtypeStruct((4, s, dk), F32), jax.ShapeDtypeStruct((nkv, s, dk), F32),
                   jax.ShapeDtypeStruct((nkv, s, 64), F32)],
        compiler_params=_cparams(("arbitrary", "arbitrary", "arbitrary")),
    )(q_hm, k_hm, v_hm, o, lse, dmixcat)


def _lane_group_masks(w, g):
    lane = lax.broadcasted_iota(jnp.int32, (1, w), 1)
    return [((lane >= k * g) & (lane < (k + 1) * g)).astype(F32) for k in range(w // g)]


def _sgu_gate(svn, w_ref, bias):
    masks = _lane_group_masks(GROUP_W, 64)
    outs = []
    for n in range(TM // CHUNK):
        x = svn[n * CHUNK:(n + 1) * CHUNK]
        acc = bias
        for g in range(4):
            acc = acc + jnp.dot(w_ref[g].astype(BF16), (x * masks[g]).astype(BF16), preferred_element_type=F32)
        outs.append(acc)
    return jnp.concatenate(outs, axis=0)


def c_fwd(name, s, proj, g, b, w, bias):
    def fn(i, j, cin, g_ref, b_ref, w_ref, bias_ref, o_ref):
        c, _ = _gelu(cin)
        xh, _ = _ln_stats(c[:, GROUP_W:])
        svn = xh * g_ref[...] + b_ref[...]
        o_ref[...] = c[:, :GROUP_W] * _sgu_gate(svn, w_ref, bias_ref[...])

    ins = [('row', proj, 512, 2), ('full', g), ('full', b), ('full', w), ('full', bias)]
    return tiled_call(name, fn, s, TM, ins, [('row', GROUP_W, GROUP_W, 0, F32)])[0]


def c_bwd(name, s, proj, g, b, w, wt, bias, dmixcat):
    def fn(i, j, cin, do, g_ref, b_ref, w_ref, wt_ref, bias_ref, dc_ref, dg_ref, db_ref, dw_ref, dbias_ref):
        c, t = _gelu(cin)
        u = c[:, :GROUP_W]
        xh, r = _ln_stats(c[:, GROUP_W:])
        svn = xh * g_ref[...] + b_ref[...]
        gate = _sgu_gate(svn, w_ref, bias_ref[...])
        du = do * gate
        dgate = do * u
        masks = _lane_group_masks(GROUP_W, 64)
        col = lax.broadcasted_iota(jnp.int32, (CHUNK, 128), 1)
        dsvn, dbias = [], jnp.zeros((CHUNK, 128), F32)
        for n in range(TM // CHUNK):
            dgc = dgate[n * CHUNK:(n + 1) * CHUNK]
            x = svn[n * CHUNK:(n + 1) * CHUNK]
            acc = jnp.zeros((CHUNK, GROUP_W), F32)
            for gi in range(4):
                dgm = (dgc * masks[gi]).astype(BF16)
                acc = acc + jnp.dot(wt_ref[gi].astype(BF16), dgm, preferred_element_type=F32)
                dw_ref[gi] += _dot_nt(dgm, (x * masks[gi]).astype(BF16))
                dbias = dbias + jnp.where(col == gi, jnp.sum(dgc * masks[gi], axis=1, keepdims=True), 0.0)
            dsvn.append(acc)
        dbias_ref[...] += dbias
        dsv, dg, db = _ln_bwd(jnp.concatenate(dsvn, axis=0), xh, r, g_ref[...])
        dg_ref[...] += dg
        db_ref[...] += db
        dc_ref[...] = (jnp.concatenate([du, dsv], axis=1) * _gelu_grad(cin, t)).astype(BF16)

    ins = [('row', proj, 512, 2), ('row', dmixcat, GROUP_W, 2), ('full', g), ('full', b), ('full', w), ('full', wt), ('full', bias)]
    return tiled_call(name, fn, s, TM, ins, [('row', 512, 512, 0, BF16), ('acc', (1, GROUP_W)), ('acc', (1, GROUP_W)),
                                             ('acc', (4, CHUNK, CHUNK)), ('acc', (CHUNK, 128))])


def _d_common(dpart, gq_ref, gkv_ref):
    cq, ckv, kr = dpart[:, :256], dpart[:, 256:384], dpart[:, 384:]
    rq = lax.rsqrt(jnp.sum(cq * cq, axis=1, keepdims=True) * (1.0 / MLA_Q_LORA) + RMS_EPS)
    rkv = lax.rsqrt(jnp.mean(ckv * ckv, axis=1, keepdims=True) + RMS_EPS)
    return cq, ckv, kr, rq, rkv, cq * rq * gq_ref[...], ckv * rkv * gkv_ref[...]


def d_prep_fwd(name, s, proj, gq, gkv, wuq, wukv, tm_, tkr):
    cm, sam, sbm = tm_
    ck, sak, sbk = tkr

    def fn(i, j, dpart, cmv, samv, sbmv, ckv_, sakv, sbkv, gq_ref, gkv_ref, wuq_ref, wukv_ref, q_ref, k_ref, v_ref):
        cq, ckv, kr, rq, rkv, cqn, ckvn = _d_common(dpart, gq_ref, gkv_ref)
        qd = jnp.dot(cqn.astype(BF16), wuq_ref[...], preferred_element_type=F32)
        qf = _rope(qd, cmv, samv, sbmv, 8)
        kvd = jnp.dot(ckvn.astype(BF16), wukv_ref[...], preferred_element_type=F32)
        krr = _rope(kr, ckv_, sakv, sbkv, 8)
        for h in range(4):
            q_ref[h] = qf[:, h * 128:(h + 1) * 128].astype(BF16)
            k_ref[h] = (kvd[:, h * 128:(h + 1) * 128] + krr).astype(BF16)
            v_ref[h] = kvd[:, 512 + h * 64:512 + (h + 1) * 64].astype(BF16)

    ins = [('row', proj, 512, 3), ('row', cm, 512, 0), ('row', sam, 512, 0), ('row', sbm, 512, 0), ('row', ck, 128, 0),
           ('row', sak, 128, 0), ('row', sbk, 128, 0), ('full', gq), ('full', gkv), ('full', wuq), ('full', wukv)]
    return tiled_call(name, fn, s, TM, ins, [('hm', 4, 128, BF16), ('hm', 4, 128, BF16), ('hm', 4, 64, BF16)])


def d_prep_bwd(name, s, proj, gq, gkv, wuq, wukv, tm_, tkr, dq_hm, dk_hm, dv_hm):
    cm, sam, sbm = tm_
    ck, sak, sbk = tkr

    def fn(i, j, dpart, cmv, samv, sbmv, ckv_, sakv, sbkv, gq_ref, gkv_ref, wuq_ref, wukv_ref, dq_ref, dk_ref, dv_ref,
           dd_ref, dgq_ref, dgkv_ref, dwuq_ref, dwukv_ref):
        cq, ckv, kr, rq, rkv, cqn, ckvn = _d_common(dpart, gq_ref, gkv_ref)
        dqf = jnp.concatenate([dq_ref[h] for h in range(4)], axis=1)
        dkf = [dk_ref[h] for h in range(4)]
        dqd = _rope_t(dqf, cmv, samv, sbmv, 8).astype(BF16)
        dkvd = jnp.concatenate(dkf + [dv_ref[h] for h in range(4)], axis=1).astype(BF16)
        dkr = _rope_t(dkf[0] + dkf[1] + dkf[2] + dkf[3], ckv_, sakv, sbkv, 8)
        dcqn = _dot_nt(dqd, wuq_ref[...])
        dckvn = _dot_nt(dkvd, wukv_ref[...])
        dwuq_ref[...] += _dot_tn(cqn.astype(BF16), dqd)
        dwukv_ref[...] += _dot_tn(ckvn.astype(BF16), dkvd)
        gqv, gkvv = gq_ref[...], gkv_ref[...]
        dcq = gqv * rq * dcqn - cq * (rq * rq * rq) * (jnp.sum(cq * gqv * dcqn, axis=1, keepdims=True) * (1.0 / MLA_Q_LORA))
        dckv = gkvv * rkv * dckvn - ckv * (rkv * rkv * rkv) * jnp.mean(ckv * gkvv * dckvn, axis=1, keepdims=True)
        dgq_ref[...] += jnp.sum(cq * rq * dcqn, axis=0, keepdims=True)
        dgkv_ref[...] += jnp.sum(ckv * rkv * dckvn, axis=0, keepdims=True)
        dd_ref[...] = jnp.concatenate([dcq, dckv, dkr], axis=1).astype(BF16)

    ins = [('row', proj, 512, 3), ('row', cm, 512, 0), ('row', sam, 512, 0), ('row', sbm, 512, 0), ('row', ck, 128, 0),
           ('row', sak, 128, 0), ('row', sbk, 128, 0), ('full', gq), ('full', gkv), ('full', wuq), ('full', wukv),
           ('hm', dq_hm), ('hm', dk_hm), ('hm', dv_hm)]
    return tiled_call(name, fn, s, TM, ins, [('row', 512, 512, 0, BF16), ('acc', (1, 256)), ('acc', (1, 128)),
                                             ('acc', (256, 512)), ('acc', (128, 768))])


HALO_F = 8
TN_F = 256
TM_F = 512


def _conv3(ext, w_ref):
    return _roll_r(ext, -1) * w_ref[pl.ds(0, 1), :] + ext * w_ref[pl.ds(1, 1), :] + _roll_r(ext, 1) * w_ref[pl.ds(2, 1), :]


def f_fwd(name, s, upraw, cw, cb):
    ncj = D_FF // TN_F

    def fn(i, j, xa, xg, wa_ref, wg_ref, ba_ref, bg_ref, o_ref):
        ua = (_conv3(xa, wa_ref) + ba_ref[...])[HALO_F:HALO_F + TM_F]
        ug = (_conv3(xg, wg_ref) + bg_ref[...])[HALO_F:HALO_F + TM_F]
        o_ref[...] = (ua * _sigmoid(ua) * ug).astype(BF16)

    ins = [('halo', upraw, TN_F, lambda j: j, HALO_F), ('halo', upraw, TN_F, lambda j: j + ncj, HALO_F),
           ('col', cw, TN_F, lambda j: j), ('col', cw, TN_F, lambda j: j + ncj),
           ('col', cb, TN_F, lambda j: j), ('col', cb, TN_F, lambda j: j + ncj)]
    return tiled_call(name, fn, s, TM_F, ins, [('row', D_FF, TN_F, lambda j: j, BF16)], ncol=ncj)[0]


def f_bwd(name, s, upraw, dact, cw, cb):
    ncj = D_FF // TN_F

    def fn(i, j, xa, xg, da, wa_ref, wg_ref, ba_ref, bg_ref, dxa_ref, dxg_ref, dwa_ref, dwg_ref, dba_ref, dbg_ref):
        ua = _conv3(xa, wa_ref) + ba_ref[...]
        ug = _conv3(xg, wg_ref) + bg_ref[...]
        sg = _sigmoid(ua)
        dug = da * ua * sg
        dua = da * ug * sg * (1.0 + ua * (1.0 - sg))
        cen = slice(HALO_F, HALO_F + TM_F)
        for du, x, w_ref, dx_ref, dw_ref, db_ref in ((dua, xa, wa_ref, dxa_ref, dwa_ref, dba_ref),
                                                      (dug, xg, wg_ref, dxg_ref, dwg_ref, dbg_ref)):
            dx = _roll_r(du, 1) * w_ref[pl.ds(0, 1), :] + du * w_ref[pl.ds(1, 1), :] + _roll_r(du, -1) * w_ref[pl.ds(2, 1), :]
            dx_ref[...] = dx[cen].astype(BF16)
            duc = du[cen]
            for k in range(3):
                dw_ref[pl.ds(k, 1), :] += jnp.sum(duc * _roll_r(x, k - 1)[cen], axis=0, keepdims=True)
            db_ref[...] += jnp.sum(duc, axis=0, keepdims=True)

    ins = [('halo', upraw, TN_F, lambda j: j, HALO_F), ('halo', upraw, TN_F, lambda j: j + ncj, HALO_F),
           ('halo', dact, TN_F, lambda j: j, HALO_F),
           ('col', cw, TN_F, lambda j: j), ('col', cw, TN_F, lambda j: j + ncj),
           ('col', cb, TN_F, lambda j: j), ('col', cb, TN_F, lambda j: j + ncj)]
    outs = [('row', D_FF, TN_F, lambda j: j, BF16), ('row', D_FF, TN_F, lambda j: j, BF16),
            ('colacc', 8, D_FF, TN_F, lambda j: j), ('colacc', 8, D_FF, TN_F, lambda j: j),
            ('colacc', 1, D_FF, TN_F, lambda j: j), ('colacc', 1, D_FF, TN_F, lambda j: j)]
    return tiled_call(name, fn, s, TM_F, ins, outs, ncol=ncj)


def _pad_w_in(w):
    z = lambda n: jnp.zeros((w.shape[0], n), w.dtype)
    return jnp.concatenate([w[:, :1728], z(64), w[:, 1728:1856], z(64), w[:, 1856:1888], z(32)], axis=1)


def _unpad_w_in(g):
    return jnp.concatenate([g[:, :1728], g[:, 1792:1920], g[:, 1984:2016]], axis=1)


def _pad_uq(w):
    w = w.reshape(MLA_Q_LORA, 4, 96)
    w = jnp.concatenate([w, jnp.zeros((MLA_Q_LORA, 4, 32), w.dtype)], axis=2).reshape(MLA_Q_LORA, 512)
    return jnp.concatenate([w, jnp.zeros((64, 512), w.dtype)], axis=0)


def _unpad_uq(g):
    return g[:MLA_Q_LORA].reshape(MLA_Q_LORA, 4, 128)[:, :, :96].reshape(MLA_Q_LORA, 384)


def _pad_ukv(w):
    w = w.reshape(MLA_KV_LORA, 4, 128)
    kn = jnp.concatenate([w[:, :, :64], jnp.zeros((MLA_KV_LORA, 4, 64), w.dtype)], axis=2).reshape(MLA_KV_LORA, 512)
    return jnp.concatenate([kn, w[:, :, 64:].reshape(MLA_KV_LORA, 256)], axis=1)


def _unpad_ukv(g):
    kn = g[:, :512].reshape(MLA_KV_LORA, 4, 128)[:, :, :64]
    v = g[:, 512:].reshape(MLA_KV_LORA, 4, 64)
    return jnp.concatenate([kn, v], axis=2).reshape(MLA_KV_LORA, 512)


def _row(v, pad_to=None):
    v = v.reshape(1, -1)
    if pad_to is not None and v.shape[1] < pad_to:
        v = jnp.concatenate([v, jnp.zeros((1, pad_to - v.shape[1]), v.dtype)], axis=1)
    return v


def local_step(x, tgt, W):
    s = x.shape[0]
    tabs = _rope_tables(s)
    gm = _group_mean_matrix(256, 64)
    alpha = DEEPNORM_ALPHA
    acts = []
    h = ln_fwd("ln_in", s, x, _row(W['ln_in_g']), _row(W['ln_in_b']))
    for l in range(DEPTH):
        A = dict(h=h)
        A['wpad'] = _pad_w_in(W['w_in'][l]).astype(BF16)
        A['wout'] = W['w_out'][l].astype(BF16)
        A['wup'] = W['ffn_w_up'][l].astype(BF16)
        A['wdown'] = W['ffn_w_down'][l].astype(BF16)
        A['wuq'] = _pad_uq(W['mla_w_uq'][l]).astype(BF16)
        A['wukv'] = _pad_ukv(W['mla_w_ukv'][l]).astype(BF16)
        A['cw'] = jnp.concatenate([W['conv_a_w'][l], jnp.zeros((1, GROUP_W), F32)], axis=0)
        A['fcw'] = jnp.concatenate([W['ffn_conv_w'][l], jnp.zeros((5, 2 * D_FF), F32)], axis=0)
        A['gq'] = jnp.tile(_row(W['qk_norm_q'][l]), (1, 4))
        A['gk'] = jnp.tile(_row(W['qk_norm_k'][l]), (1, 2))
        A['sgu_bias'] = jnp.repeat(W['sgu_b'][l].T, 64, axis=1)
        A['sgu_wt'] = jnp.swapaxes(W['sgu_w'][l], 1, 2)
        proj = mm(f"proj{l}", h, A['wpad'])
        A['proj'] = proj
        A['y_a'], o_a = a_fwd(f"a_fwd{l}", s, proj, A['cw'], _row(W['conv_a_b'][l]), _row(W['ln_a_g'][l]), _row(W['ln_a_b'][l]))
        A['bq'], A['bk'], A['bv'] = b_prep_fwd(f"b_prep{l}", s, proj, A['gq'], A['gk'], tabs['gqa'], gm)
        A['o_b'], A['lse_b'] = attn_fwd(f"b_attn{l}", s, A['bq'], A['bk'], A['bv'], HEAD_DIM ** -0.5)
        o_c = c_fwd(f"c_fwd{l}", s, proj, _row(W['sgu_ln_g'][l]), _row(W['sgu_ln_b'][l]), W['sgu_w'][l], A['sgu_bias'])
        A['dq'], A['dk'], A['dv'] = d_prep_fwd(f"d_prep{l}", s, proj, _row(W['mla_q_norm'][l], 256), _row(W['mla_kv_norm'][l]),
                                               A['wuq'], A['wukv'], tabs['mla'], tabs['kr'])
        A['o_d'], A['lse_d'] = attn_fwd(f"d_attn{l}", s, A['dq'], A['dk'], A['dv'], (MLA_NOPE + MLA_ROPE) ** -0.5)
        A['mixcat'] = jnp.concatenate([o_a, A['o_b'], o_c, A['o_d']], axis=1).astype(BF16)
        A['mix'] = mm(f"out_proj{l}", A['mixcat'], A['wout'])
        h1 = ln_fwd(f"ln_mix{l}", s, A['mix'], _row(W['ln_mix_g'][l]), _row(W['ln_mix_b'][l]), res=h, alpha=alpha)
        A['h1'] = h1
        A['upraw'] = mm(f"ffn_up{l}", h1, A['wup'])
        A['act'] = f_fwd(f"f_fwd{l}", s, A['upraw'], A['fcw'], _row(W['ffn_conv_b'][l]))
        A['f'] = mm(f"ffn_down{l}", A['act'], A['wdown'])
        if l < DEPTH - 1:
            h = ln_fwd(f"ln_ffn{l}", s, A['f'], _row(W['ln_ffn_g'][l]), _row(W['ln_ffn_b'][l]), res=h1, alpha=alpha)
        acts.append(A)

    per_layer = {n: [None] * DEPTH for n in WEIGHT_NAMES if n not in ('ln_in_g', 'ln_in_b')}
    dh = None
    loss_row = None
    for l in reversed(range(DEPTH)):
        A = acts[l]
        g_ffn, b_ffn = _row(W['ln_ffn_g'][l]), _row(W['ln_ffn_b'][l])
        if l == DEPTH - 1:
            dz, dg, db, loss_row = loss_and_ln_bwd("loss", s, A['f'], A['h1'], g_ffn, b_ffn, tgt, alpha)
        else:
            dz, dg, db = ln_bwd(f"ln_ffn_bwd{l}", s, dh, A['f'], g_ffn, res=A['h1'], alpha=alpha)
        per_layer['ln_ffn_g'][l], per_layer['ln_ffn_b'][l] = dg[0], db[0]
        per_layer['ffn_w_down'][l] = mm(f"dw_down{l}", A['act'], dz, ta=True).reshape(N_CHIPS, D_FF // N_CHIPS, D_MODEL)
        dact = mm(f"d_act{l}", dz, A['wdown'], tb=True)
        dxa, dxg, dwa, dwg, dba, dbg = f_bwd(f"f_bwd{l}", s, A['upraw'], dact, A['fcw'], _row(W['ffn_conv_b'][l]))
        per_layer['ffn_conv_w'][l] = jnp.concatenate([dwa[:3], dwg[:3]], axis=1)
        per_layer['ffn_conv_b'][l] = jnp.concatenate([dba[0], dbg[0]], axis=0)
        dup = jnp.concatenate([dxa, dxg], axis=1)
        per_layer['ffn_w_up'][l] = mm(f"dw_up{l}", A['h1'], dup, ta=True, chip_major=True)
        dh1_mm = mm(f"d_h1{l}", dup, A['wup'], tb=True)
        dz1, dg, db = _ln_mix_bwd(l, s, dz, dh1_mm, A, W, alpha)
        per_layer['ln_mix_g'][l], per_layer['ln_mix_b'][l] = dg[0], db[0]
        per_layer['w_out'][l] = mm(f"dw_out{l}", A['mixcat'], dz1, ta=True).reshape(N_CHIPS, D_MODEL // N_CHIPS, D_MODEL)
        dmixcat = mm(f"d_mixcat{l}", dz1, A['wout'], tb=True)
        dy_a, dg, db, dcb = a_bwd1(f"a_bwd1{l}", s, A['y_a'], dmixcat, _row(W['ln_a_g'][l]), _row(W['ln_a_b'][l]))
        per_layer['ln_a_g'][l], per_layer['ln_a_b'][l], per_layer['conv_a_b'][l] = dg[0], db[0], dcb[0]
        dp_a, dcw = a_bwd2(f"a_bwd2{l}", s, A['proj'], dy_a, A['cw'])
        per_layer['conv_a_w'][l] = dcw[:CONV_A_WIDTH]
        dq, dk, dv = attn_bwd(f"b_attn_bwd{l}", s, A['bq'], A['bk'], A['bv'], A['o_b'], A['lse_b'], dmixcat, 2, HEAD_DIM ** -0.5)
        dp_b, dgq, dgk = b_prep_bwd(f"b_prep_bwd{l}", s, A['proj'], A['gq'], A['gk'], tabs['gqa'], gm, dq, dk, dv)
        per_layer['qk_norm_q'][l], per_layer['qk_norm_k'][l] = dgq[0], dgk[0]
        dp_c, dg, db, dsw, dsb = c_bwd(f"c_bwd{l}", s, A['proj'], _row(W['sgu_ln_g'][l]), _row(W['sgu_ln_b'][l]), W['sgu_w'][l],
                                       A['sgu_wt'], A['sgu_bias'], dmixcat)
        per_layer['sgu_ln_g'][l], per_layer['sgu_ln_b'][l] = dg[0], db[0]
        per_layer['sgu_w'][l], per_layer['sgu_b'][l] = dsw, dsb[:, :4].T
        dq, dk, dv = attn_bwd(f"d_attn_bwd{l}", s, A['dq'], A['dk'], A['dv'], A['o_d'], A['lse_d'], dmixcat, 6,
                              (MLA_NOPE + MLA_ROPE) ** -0.5)
        dp_d, dgq, dgkv, dwuq, dwukv = d_prep_bwd(f"d_prep_bwd{l}", s, A['proj'], _row(W['mla_q_norm'][l], 256),
                                                  _row(W['mla_kv_norm'][l]), A['wuq'], A['wukv'], tabs['mla'], tabs['kr'],
                                                  dq, dk, dv)
        per_layer['mla_q_norm'][l], per_layer['mla_kv_norm'][l] = dgq[0, :MLA_Q_LORA], dgkv[0]
        per_layer['mla_w_uq'][l], per_layer['mla_w_ukv'][l] = _unpad_uq(dwuq), _unpad_ukv(dwukv)
        dproj = jnp.concatenate([dp_a, dp_b, dp_c, dp_d], axis=1)
        dw_in = _unpad_w_in(mm(f"dw_in{l}", A['h'], dproj, ta=True))
        per_layer['w_in'][l] = dw_in.reshape(D_MODEL, N_CHIPS, D_IN_PROJ // N_CHIPS).transpose(1, 0, 2)
        dh = mm(f"d_h{l}", dproj, A['wpad'], tb=True, add=dz1, add_scale=alpha)
    dx, dg, db = ln_bwd("ln_in_bwd", s, dh, x, _row(W['ln_in_g']))
    G = dict(per_layer)
    G['ln_in_g'], G['ln_in_b'] = dg[0], db[0]
    return loss_row, dx, G


def _ln_mix_bwd(l, s, dz, dh1_mm, A, W, alpha):
    ins = [('row', dz, D_MODEL, 0), ('row', dh1_mm, D_MODEL, 0), ('row', A['mix'], D_MODEL, 0), ('row', A['h'], D_MODEL, 0),
           ('full', _row(W['ln_mix_g'][l]))]

    def fn(i, j, dzv, dmv, xv, rv, g_ref, o_ref, dg_ref, db_ref):
        xh, r = _ln_stats(alpha * rv + xv)
        d, dg, db = _ln_bwd(alpha * dzv + dmv, xh, r, g_ref[...])
        o_ref[...] = d
        dg_ref[...] += dg
        db_ref[...] += db

    return tiled_call(f"ln_mix_bwd{l}", fn, s, TM, ins, [('row', D_MODEL, D_MODEL, 0, F32), ('acc', (1, D_MODEL)),
                                                         ('acc', (1, D_MODEL))])


ANY = pl.BlockSpec(memory_space=pl.ANY)


def _pos():
    return lax.axis_index("x"), lax.axis_index("y"), lax.axis_index("c")


def _rcopy(src, dst, ssem, rsem, dev):
    return pltpu.make_async_remote_copy(src_ref=src, dst_ref=dst, send_sem=ssem, recv_sem=rsem, device_id=dev,
                                        device_id_type=MESH)


def gather_over_chips(name, parts):
    n = len(parts)
    split = [p.shape[1] % 32 == 0 for p in parts]

    def body(*refs):
        srcs, dsts = refs[:n], refs[n:2 * n]
        ssems, rsems, fsems, grsems = refs[2 * n:]
        x, y, c = _pos()
        me = 2 * x + y
        chips = [(1 - x, y), (x, 1 - y), (1 - x, 1 - y)]

        def rows(t, half):
            a2 = parts[t].shape[1] // 2
            return pl.ds(half * a2, a2)

        sends = []
        for j, (px, py) in enumerate(chips):
            for t in range(n):
                k = n * j + t
                if split[t]:
                    cp = _rcopy(srcs[t].at[:, rows(t, c), :], dsts[t].at[me, :, rows(t, c), :], ssems.at[k], rsems.at[k], (px, py, c))
                else:
                    cp = _rcopy(srcs[t], dsts[t].at[me], ssems.at[k], rsems.at[k], (px, py, c))
                cp.start()
                sends.append(cp)
        for j, (px, py) in enumerate(chips):
            q = 2 * px + py
            for t in range(n):
                k = n * j + t
                if split[t]:
                    got = dsts[t].at[q, :, rows(t, c), :]
                    _rcopy(got, got, ssems.at[k], rsems.at[k], (px, py, c)).wait_recv()
                    fw = _rcopy(got, got, fsems.at[k], grsems.at[k], (x, y, 1 - c))
                    fw.start()
                    sends.append(fw)
                else:
                    _rcopy(srcs[t], dsts[t].at[q], ssems.at[k], rsems.at[k], (px, py, c)).wait_recv()
        for j, (px, py) in enumerate(chips):
            q = 2 * px + py
            for t in range(n):
                if split[t]:
                    other = dsts[t].at[q, :, rows(t, 1 - c), :]
                    _rcopy(other, other, fsems.at[n * j + t], grsems.at[n * j + t], (x, y, 1 - c)).wait_recv()
        for cp in sends:
            cp.wait_send()

    return pl.pallas_call(
        body, name=name, in_specs=[ANY] * n, out_specs=[ANY] * n,
        out_shape=[jax.ShapeDtypeStruct((N_CHIPS,) + p.shape, p.dtype) for p in parts],
        scratch_shapes=[pltpu.SemaphoreType.DMA((3 * n,))] * 4,
    )(*parts)


def pair_exchange(name, xs):
    n = len(xs)

    def body(*refs):
        srcs, dsts, (ssems, rsems) = refs[:n], refs[n:2 * n], refs[2 * n:]
        x, y, c = _pos()
        cps = []
        for t in range(n):
            a2 = xs[t].shape[1] // 2
            cp = _rcopy(srcs[t].at[:, pl.ds((1 - c) * a2, a2), :], dsts[t], ssems.at[t], rsems.at[t], (x, y, 1 - c))
            cp.start()
            cps.append(cp)
        for cp in cps:
            cp.wait()

    return pl.pallas_call(
        body, name=name, in_specs=[ANY] * n, out_specs=[ANY] * n,
        out_shape=[jax.ShapeDtypeStruct((a.shape[0], a.shape[1] // 2, a.shape[2]), a.dtype) for a in xs],
        scratch_shapes=[pltpu.SemaphoreType.DMA((n,)), pltpu.SemaphoreType.DMA((n,))])(*xs)


def chip_exchange(name, ps):
    n = len(ps)

    def body(*refs):
        srcs, dsts = refs[:n], refs[n:2 * n]
        ssems, rsems = refs[2 * n:]
        x, y, c = _pos()
        me = 2 * x + y
        chips = [(1 - x, y), (x, 1 - y), (1 - x, 1 - y)]
        sends = []
        for j, (px, py) in enumerate(chips):
            for t in range(n):
                cp = _rcopy(srcs[t].at[2 * px + py], dsts[t].at[me], ssems.at[n * j + t], rsems.at[n * j + t], (px, py, c))
                cp.start()
                sends.append(cp)
        for j, (px, py) in enumerate(chips):
            for t in range(n):
                _rcopy(srcs[t].at[me], dsts[t].at[2 * px + py], ssems.at[n * j + t], rsems.at[n * j + t], (px, py, c)).wait_recv()
        for cp in sends:
            cp.wait_send()

    return pl.pallas_call(
        body, name=name, in_specs=[ANY] * n, out_specs=[ANY] * n,
        out_shape=[jax.ShapeDtypeStruct(p.shape, p.dtype) for p in ps],
        scratch_shapes=[pltpu.SemaphoreType.DMA((3 * n,)), pltpu.SemaphoreType.DMA((3 * n,))])(*ps)


def half_exchange(name, bufs):
    n = len(bufs)

    def body(*refs):
        ins, outs, (ssems, rsems) = refs[:n], refs[n:2 * n], refs[2 * n:]
        x, y, c = _pos()
        cps = []
        for t in range(n):
            a2 = bufs[t].shape[1] // 2
            mine = pl.ds(c * a2, a2)
            cp = _rcopy(ins[t].at[:, mine, :], outs[t].at[:, mine, :], ssems.at[t], rsems.at[t], (x, y, 1 - c))
            cp.start()
            cps.append(cp)
        for cp in cps:
            cp.wait()

    return pl.pallas_call(
        body, name=name, in_specs=[ANY] * n, out_specs=[ANY] * n,
        out_shape=[jax.ShapeDtypeStruct(b.shape, b.dtype) for b in bufs], input_output_aliases={t: t for t in range(n)},
        scratch_shapes=[pltpu.SemaphoreType.DMA((n,)), pltpu.SemaphoreType.DMA((n,))])(*bufs)


def gather_over_devices(name, a):
    def body(a_ref, o_ref, ssems, rsems, lsem):
        x, y, c = _pos()
        me = 4 * x + 2 * y + c
        loc = pltpu.make_async_copy(a_ref, o_ref.at[me], lsem)
        loc.start()
        peers = [((x + (k >> 2)) % 2, (y + ((k >> 1) & 1)) % 2, (c + (k & 1)) % 2) for k in range(1, N_DEV)]
        sends = []
        for j, dev in enumerate(peers):
            cp = _rcopy(a_ref, o_ref.at[me], ssems.at[j], rsems.at[j], dev)
            cp.start()
            sends.append(cp)
        for j, (px, py, pc) in enumerate(peers):
            _rcopy(a_ref, o_ref.at[4 * px + 2 * py + pc], ssems.at[j], rsems.at[j], (px, py, pc)).wait_recv()
        for cp in sends:
            cp.wait_send()
        loc.wait()

    return pl.pallas_call(body, name=name, in_specs=[ANY], out_specs=ANY, out_shape=jax.ShapeDtypeStruct((N_DEV,) + a.shape, a.dtype),
                          scratch_shapes=[pltpu.SemaphoreType.DMA((N_DEV - 1,)), pltpu.SemaphoreType.DMA((N_DEV - 1,)),
                                          pltpu.SemaphoreType.DMA(())])(a)


LANES = 1024
ROW_TILE_BYTES = 2 * 1024 * 1024


def _row_tile(r, row_bytes):
    if r * row_bytes <= ROW_TILE_BYTES:
        return r
    best = None
    for t in range(16, r, 16):
        if r % t == 0 and t * row_bytes <= ROW_TILE_BYTES:
            best = t
    assert best is not None, (r, row_bytes)
    return best


def pair_sum(name, x, r1, c_arr, out_dtype):
    q, a2, b = r1.shape
    tr = _row_tile(a2, b * 4)
    nb = a2 // tr

    def body(c_ref, x_ref, r_ref, o_ref):
        o_ref[...] = (x_ref[...] + r_ref[...]).astype(out_dtype)

    grid_spec = pltpu.PrefetchScalarGridSpec(
        num_scalar_prefetch=1, grid=(q, nb),
        in_specs=[pl.BlockSpec((None, tr, b), lambda k, i, c_ref: (k, c_ref[0] * nb + i, 0)),
                  pl.BlockSpec((None, tr, b), lambda k, i, c_ref: (k, i, 0))],
        out_specs=pl.BlockSpec((None, tr, b), lambda k, i, c_ref: (k, i, 0)))
    return pl.pallas_call(body, name=name, grid_spec=grid_spec, out_shape=jax.ShapeDtypeStruct(r1.shape, out_dtype),
                          compiler_params=_cparams(("parallel", "parallel")))(c_arr, x, r1)


def sum_leading(name, a):
    k, r, b = a.shape
    tr = _row_tile(r, k * b * a.dtype.itemsize)

    def body(a_ref, o_ref):
        acc = a_ref[0].astype(F32)
        for q in range(1, k):
            acc = acc + a_ref[q].astype(F32)
        o_ref[...] = acc

    return pl.pallas_call(body, name=name, grid=(r // tr,), in_specs=[pl.BlockSpec((k, tr, b), lambda i: (0, i, 0))],
                          out_specs=pl.BlockSpec((tr, b), lambda i: (i, 0)), out_shape=jax.ShapeDtypeStruct((r, b), F32),
                          compiler_params=_cparams(("parallel",)))(a)


def chip_sum(name, recv, own, idx):
    nl = len(recv)
    k, a2, b = recv[0].shape
    tr = _row_tile(a2, k * b * recv[0].dtype.itemsize)
    nb = a2 // tr

    def body(idx_ref, *refs):
        rs, os_, o_ref = refs[:nl], refs[nl:2 * nl], refs[2 * nl]
        l = pl.program_id(0)
        me = idx_ref[1]
        for li in range(nl):
            @pl.when(l == li)
            def _(li=li):
                acc = None
                for q in range(k):
                    term = jnp.where(me == q, os_[li][...], rs[li][q]).astype(F32)
                    acc = term if acc is None else acc + term
                o_ref[...] = acc

    def pick(li):
        return lambda l, i, idx_ref: jnp.where(l == li, i, 0)

    in_specs = [pl.BlockSpec((k, tr, b), lambda l, i, idx_ref, _p=pick(li): (0, _p(l, i, idx_ref), 0)) for li in range(nl)]
    in_specs += [pl.BlockSpec((None, tr, b), lambda l, i, idx_ref, _p=pick(li): (idx_ref[1], _p(l, i, idx_ref), 0)) for li in range(nl)]
    grid_spec = pltpu.PrefetchScalarGridSpec(
        num_scalar_prefetch=1, grid=(nl, nb), in_specs=in_specs,
        out_specs=pl.BlockSpec((None, tr, b), lambda l, i, idx_ref: (l, idx_ref[0] * nb + i, 0)))
    return pl.pallas_call(body, name=name, grid_spec=grid_spec, out_shape=jax.ShapeDtypeStruct((nl, 2 * a2, b), F32),
                          compiler_params=_cparams(("arbitrary", "arbitrary")))(idx, *recv, *own)


def adamw(name, g, w, m, v):
    r, b = g.shape
    tr = _row_tile(r, b * 4)

    def body(g_ref, w_ref, m_ref, v_ref, go_ref, d_ref, nm_ref, nv_ref):
        gv = g_ref[...]
        go_ref[...] = gv
        mn = ADAM_B1 * m_ref[...] + (1.0 - ADAM_B1) * gv
        vn = ADAM_B2 * v_ref[...] + (1.0 - ADAM_B2) * (gv * gv)
        m_hat = mn / (1.0 - ADAM_B1 ** ADAM_STEP)
        v_hat = vn / (1.0 - ADAM_B2 ** ADAM_STEP)
        d_ref[...] = -ADAM_LR * (m_hat / (jnp.sqrt(v_hat) + ADAM_EPS) + ADAM_WD * w_ref[...])
        nm_ref[...] = mn
        nv_ref[...] = vn

    spec = pl.BlockSpec((tr, b), lambda i: (i, 0))
    sds = jax.ShapeDtypeStruct(g.shape, F32)
    return pl.pallas_call(body, name=name, grid=(r // tr,), in_specs=[spec] * 4, out_specs=[spec] * 4, out_shape=[sds] * 4,
                          compiler_params=_cparams(("parallel",)))(g, w, m, v)


def _pack(arrs, rows, dtype=F32):
    flat = jnp.concatenate([a.reshape(-1).astype(dtype) for a in arrs])
    pad = rows * LANES - flat.shape[0]
    assert pad >= 0
    if pad:
        flat = jnp.concatenate([flat, jnp.zeros((pad,), dtype)])
    return flat.reshape(rows, LANES)


def _unpack(pack, shapes):
    flat = pack.reshape(-1)
    out, off = [], 0
    for shp in shapes:
        n = int(np.prod(shp))
        out.append(flat[off:off + n].reshape(shp))
        off += n
    return out


BIG_NAMES = ['w_in', 'w_out', 'ffn_w_up', 'ffn_w_down']
MED_NAMES = [n for n in SHARDED_NAMES if n not in BIG_NAMES]
MED_ROWS = 96
SMALL_ROWS = 160


def kernel(x, ln_in_g, ln_in_b, w_in, conv_a_w, conv_a_b, ln_a_g, ln_a_b, qk_norm_q, qk_norm_k, sgu_ln_g, sgu_ln_b, sgu_w, sgu_b, mla_q_norm, mla_w_uq, mla_kv_norm, mla_w_ukv, w_out, ln_mix_g, ln_mix_b, ffn_w_up, ffn_conv_w, ffn_conv_b, ffn_w_down, ln_ffn_g, ln_ffn_b, loss_target, m_ln_in_g, m_ln_in_b, m_w_in, m_conv_a_w, m_conv_a_b, m_ln_a_g, m_ln_a_b, m_qk_norm_q, m_qk_norm_k, m_sgu_ln_g, m_sgu_ln_b, m_sgu_w, m_sgu_b, m_mla_q_norm, m_mla_w_uq, m_mla_kv_norm, m_mla_w_ukv, m_w_out, m_ln_mix_g, m_ln_mix_b, m_ffn_w_up, m_ffn_conv_w, m_ffn_conv_b, m_ffn_w_down, m_ln_ffn_g, m_ln_ffn_b, v_ln_in_g, v_ln_in_b, v_w_in, v_conv_a_w, v_conv_a_b, v_ln_a_g, v_ln_a_b, v_qk_norm_q, v_qk_norm_k, v_sgu_ln_g, v_sgu_ln_b, v_sgu_w, v_sgu_b, v_mla_q_norm, v_mla_w_uq, v_mla_kv_norm, v_mla_w_ukv, v_w_out, v_ln_mix_g, v_ln_mix_b, v_ffn_w_up, v_ffn_conv_w, v_ffn_conv_b, v_ffn_w_down, v_ln_ffn_g, v_ln_ffn_b):
    loc = dict(locals())
    w_loc = {n: loc[n] for n in WEIGHT_NAMES}
    m_loc = {n: loc["m_" + n] for n in WEIGHT_NAMES}
    v_loc = {n: loc["v_" + n] for n in WEIGHT_NAMES}
    me_chip = 2 * lax.axis_index("x") + lax.axis_index("y")
    idx = jnp.stack([lax.axis_index("c"), me_chip]).astype(jnp.int32)

    parts = [w_loc[n].astype(BF16) if n in BF16_WIRE else w_loc[n] for n in SHARDED_NAMES]
    W = {n: w_loc[n] for n in REPL_NAMES}
    for n, p, g in zip(SHARDED_NAMES, parts, gather_over_chips("gather_weights", parts)):
        W[n] = jnp.concatenate([jnp.where(me_chip == q, p, g[q]) for q in range(N_CHIPS)], axis=SHARDED_AXIS[n])

    loss_row, dx, G = local_step(x[0], loss_target[0], W)
    loss = lax.psum(loss_row[0, 0], ("x", "y", "c"))

    med = []
    for q in range(N_CHIPS):
        pieces = [lax.slice_in_dim(jnp.stack(G[n]), q * w_loc[n].shape[2], (q + 1) * w_loc[n].shape[2], axis=2) for n in MED_NAMES]
        med.append(_pack(pieces, MED_ROWS))
    xs = [G[n][l] for n in BIG_NAMES for l in range(DEPTH)] + [jnp.stack(med)]
    r1 = pair_exchange("pair_exchange", xs)
    ps = [pair_sum(f"pair_sum{t}", xv, rv, idx, BF16) for t, (xv, rv) in enumerate(zip(xs, r1))]
    cs = chip_exchange("chip_exchange", ps)
    groups = [[DEPTH * i + l for l in range(DEPTH)] for i in range(len(BIG_NAMES))] + [[DEPTH * len(BIG_NAMES)]]
    halves = [chip_sum(f"chip_sum{i}", [cs[t] for t in ts], [ps[t] for t in ts], idx) for i, ts in enumerate(groups)]
    *g_big, g_med = half_exchange("half_exchange", halves)
    g_med = g_med[0]

    small_pack = _pack([jnp.stack(G[n]) if isinstance(G[n], list) else G[n] for n in REPL_NAMES], SMALL_ROWS)
    g_small = sum_leading("small_sum", gather_over_devices("small_gather", small_pack))

    outs = {k: {} for k in ("grad", "delta", "new_m", "new_v")}
    for n, g in zip(BIG_NAMES, g_big):
        shp = w_loc[n].shape
        flat = (shp[0] * shp[1], shp[2])
        res = adamw(f"adamw_{n}", g.reshape(flat), *[d[n].reshape(flat) for d in (w_loc, m_loc, v_loc)])
        for k, v in zip(("grad", "delta", "new_m", "new_v"), res):
            outs[k][n] = v.reshape(shp)
    for names, g, rows in ((MED_NAMES, g_med, MED_ROWS), (REPL_NAMES, g_small, SMALL_ROWS)):
        res = adamw(f"adamw_{names[0]}_pack", g, *[_pack([d[n] for n in names], rows) for d in (w_loc, m_loc, v_loc)])
        for k, v in zip(("grad", "delta", "new_m", "new_v"), res):
            outs[k].update(zip(names, _unpack(v, [w_loc[n].shape for n in names])))
    return (loss, dx[None], *[outs["grad"][n] for n in WEIGHT_NAMES], *[outs["delta"][n] for n in WEIGHT_NAMES],
            *[outs["new_m"][n] for n in WEIGHT_NAMES], *[outs["new_v"][n] for n in WEIGHT_NAMES])
```

```python
import functools
import math

import jax
import jax.numpy as jnp
import numpy as np
from jax import lax
from jax.experimental import pallas as pl
from jax.experimental.pallas import tpu as pltpu

F32 = jnp.float32
BF16 = jnp.bfloat16

D_MODEL = 1024
DEPTH = 2
GRID_W = 64
GROUP_W = 256
HEAD_DIM = 64
CONV_A_WIDTH = 31
CHUNK = 128
MLA_Q_LORA = 192
MLA_KV_LORA = 128
MLA_NOPE = 64
MLA_ROPE = 32
D_FF = 2816
ROPE_THETA = 10000.0
DEEPNORM_ALPHA = (2 * DEPTH) ** 0.25
LN_EPS = 1e-5
RMS_EPS = 1e-6
D_IN_PROJ = 1888
PROJ_W = 2048

ADAM_LR = 0.001
ADAM_B1 = 0.9
ADAM_B2 = 0.999
ADAM_EPS = 1e-08
ADAM_WD = 0.01
ADAM_STEP = 10

VMEM_LIMIT = 56 * 1024 * 1024
MESH = pl.DeviceIdType.MESH
N_CHIPS = 4
N_DEV = 8

WEIGHT_NAMES = ['ln_in_g', 'ln_in_b', 'w_in', 'conv_a_w', 'conv_a_b', 'ln_a_g', 'ln_a_b', 'qk_norm_q', 'qk_norm_k',
                'sgu_ln_g', 'sgu_ln_b', 'sgu_w', 'sgu_b', 'mla_q_norm', 'mla_w_uq', 'mla_kv_norm', 'mla_w_ukv', 'w_out',
                'ln_mix_g', 'ln_mix_b', 'ffn_w_up', 'ffn_conv_w', 'ffn_conv_b', 'ffn_w_down', 'ln_ffn_g', 'ln_ffn_b']
SHARDED_AXIS = {'w_in': 2, 'conv_a_w': 2, 'mla_w_uq': 2, 'mla_w_ukv': 2, 'w_out': 1, 'ffn_w_up': 2, 'ffn_conv_w': 2,
                'ffn_w_down': 1}
SHARDED_NAMES = [n for n in WEIGHT_NAMES if n in SHARDED_AXIS]
REPL_NAMES = [n for n in WEIGHT_NAMES if n not in SHARDED_AXIS]
BF16_WIRE = ('w_in', 'mla_w_uq', 'mla_w_ukv', 'w_out', 'ffn_w_up', 'ffn_w_down')


def _cparams(sem):
    return pltpu.CompilerParams(dimension_semantics=sem, vmem_limit_bytes=VMEM_LIMIT)


def _pick(n, cands):
    for c in cands:
        if n % c == 0:
            return c
    return n


def carried_call(name, body, grid, in_specs, out_specs, out_shape, scratch, semantics, args, comm=None):
    if comm is None:
        res = pl.pallas_call(body, name=name, grid=grid, in_specs=in_specs, out_specs=out_specs, out_shape=out_shape,
                             scratch_shapes=scratch, compiler_params=_cparams(semantics))(*args)
        return list(res), None
    n_in, n_out, n_scr = len(in_specs), len(out_specs), len(scratch)
    nci, nco = len(comm.operands), len(comm.out_shape)

    def carried(*refs):
        o0 = n_in + nci
        s0 = o0 + n_out + nco
        ins, cins = refs[:n_in], refs[n_in:o0]
        outs, couts = refs[o0:o0 + n_out], refs[o0 + n_out:s0]
        scr, sems = refs[s0:s0 + n_scr], refs[s0 + n_scr:]
        ids = [pl.program_id(d) for d in range(len(grid))]
        first = functools.reduce(lambda u, v: u & v, [i == 0 for i in ids])
        last = functools.reduce(lambda u, v: u & v, [i == g - 1 for i, g in zip(ids, grid)])

        @pl.when(first)
        def _():
            comm.start(cins, couts, sems)

        body(*ins, *outs, *scr)

        @pl.when(last)
        def _():
            comm.finish(cins, couts, sems)

    res = pl.pallas_call(
        carried, name=name, grid=grid, in_specs=list(in_specs) + [ANY] * nci, out_specs=list(out_specs) + [ANY] * nco,
        out_shape=list(out_shape) + comm.out_shape, scratch_shapes=list(scratch) + [pltpu.SemaphoreType.DMA((k,)) for k in comm.sems],
        input_output_aliases={n_in + i: n_out + o for i, o in comm.aliases.items()},
        compiler_params=_cparams(("arbitrary",) * len(grid)))(*args, *comm.operands)
    return list(res[:n_out]), list(res[n_out:])


def mm(name, a, b, *, ta=False, tb=False, add=None, add_scale=1.0, out_dtype=F32, chip_major=False, comm=None):
    m, k = (a.shape[1], a.shape[0]) if ta else a.shape
    n = b.shape[0] if tb else b.shape[1]
    assert (b.shape[1] if tb else b.shape[0]) == k
    tm = _pick(m, (512, 256, 128))
    tn = n // N_CHIPS if chip_major else _pick(n, (1024, 1408, 512, 256, 128))
    tk = _pick(k, (1024, 1408, 512, 256, 128))
    nk = k // tk
    a_spec = pl.BlockSpec((tk, tm), lambda i, j, kk: (kk, i)) if ta else pl.BlockSpec((tm, tk), lambda i, j, kk: (i, kk))
    b_spec = pl.BlockSpec((tn, tk), lambda i, j, kk: (j, kk)) if tb else pl.BlockSpec((tk, tn), lambda i, j, kk: (kk, j))
    in_specs = [a_spec, b_spec]
    args = [a, b]
    if add is not None:
        in_specs.append(pl.BlockSpec((tm, tn), lambda i, j, kk: (i, j)))
        args.append(add)
    dims = (((0 if ta else 1,), (1 if tb else 0,)), ((), ()))

    def body(*refs):
        a_ref, b_ref = refs[0], refs[1]
        add_ref = refs[2] if add is not None else None
        o_ref, acc_ref = refs[-2], refs[-1]
        kk = pl.program_id(2)
        part = lax.dot_general(a_ref[...].astype(BF16), b_ref[...].astype(BF16), dims, preferred_element_type=F32)

        @pl.when(kk == 0)
        def _():
            acc_ref[...] = part

        @pl.when(kk > 0)
        def _():
            acc_ref[...] += part

        @pl.when(kk == nk - 1)
        def _():
            r = acc_ref[...]
            if add_ref is not None:
                r = r + add_scale * add_ref[...].astype(F32)
            o_ref[...] = r.astype(out_dtype)

    if chip_major:
        out_spec = pl.BlockSpec((None, tm, tn), lambda i, j, kk: (j, i, 0))
        out_shape = jax.ShapeDtypeStruct((N_CHIPS, m, tn), out_dtype)
    else:
        out_spec = pl.BlockSpec((tm, tn), lambda i, j, kk: (i, j))
        out_shape = jax.ShapeDtypeStruct((m, n), out_dtype)
    res, cres = carried_call(name, body, (m // tm, n // tn, nk), in_specs, [out_spec], [out_shape], [pltpu.VMEM((tm, tn), F32)],
                             ("parallel", "parallel", "arbitrary"), args, comm)
    return res[0] if comm is None else (res[0], cres)


def _cb(cb):
    return cb if callable(cb) else (lambda j, _c=cb: _c)


def tiled_call(name, fn, s, tm, ins, outs, ncol=1):
    nrow = s // tm
    in_specs, args, kinds = [], [], []
    for it in ins:
        kind = it[0]
        if kind == 'row':
            _, arr, w, cb = it
            in_specs.append(pl.BlockSpec((tm, w), lambda j, i, _c=_cb(cb): (i, _c(j))))
            args.append(arr)
            kinds.append(('row',))
        elif kind == 'halo':
            _, arr, w, cb, h = it
            r = tm // h
            nh = s // h
            in_specs.append(pl.BlockSpec((h, w), lambda j, i, _c=_cb(cb), _r=r: (jnp.maximum(i * _r - 1, 0), _c(j))))
            in_specs.append(pl.BlockSpec((tm, w), lambda j, i, _c=_cb(cb): (i, _c(j))))
            in_specs.append(pl.BlockSpec((h, w), lambda j, i, _c=_cb(cb), _r=r, _n=nh: (jnp.minimum((i + 1) * _r, _n - 1), _c(j))))
            args += [arr, arr, arr]
            kinds.append(('halo',))
        elif kind == 'full':
            arr = it[1]
            in_specs.append(pl.BlockSpec(arr.shape, lambda j, i, _n=arr.ndim: (0,) * _n))
            args.append(arr)
            kinds.append(('ref',))
        elif kind == 'col':
            _, arr, w, cb = it
            in_specs.append(pl.BlockSpec((arr.shape[0], w), lambda j, i, _c=_cb(cb): (0, _c(j))))
            args.append(arr)
            kinds.append(('ref',))
        elif kind == 'hm':
            arr = it[1]
            in_specs.append(pl.BlockSpec((arr.shape[0], tm, arr.shape[2]), lambda j, i: (0, i, 0)))
            args.append(arr)
            kinds.append(('ref',))
        else:
            raise ValueError(kind)
    out_specs, out_shapes, okinds = [], [], []
    for ot in outs:
        kind = ot[0]
        if kind == 'row':
            _, wt, w, cb, dt = ot
            out_specs.append(pl.BlockSpec((tm, w), lambda j, i, _c=_cb(cb): (i, _c(j))))
            out_shapes.append(jax.ShapeDtypeStruct((s, wt), dt))
            okinds.append('row')
        elif kind == 'hm':
            _, hh, d, dt = ot
            out_specs.append(pl.BlockSpec((hh, tm, d), lambda j, i: (0, i, 0)))
            out_shapes.append(jax.ShapeDtypeStruct((hh, s, d), dt))
            okinds.append('hm')
        elif kind == 'acc':
            shape = ot[1]
            out_specs.append(pl.BlockSpec(shape, lambda j, i, _n=len(shape): (0,) * _n))
            out_shapes.append(jax.ShapeDtypeStruct(shape, F32))
            okinds.append('acc')
        elif kind == 'colacc':
            _, r, wt, w, cb = ot
            out_specs.append(pl.BlockSpec((r, w), lambda j, i, _c=_cb(cb): (0, _c(j))))
            out_shapes.append(jax.ShapeDtypeStruct((r, wt), F32))
            okinds.append('colacc')
        else:
            raise ValueError(kind)
    n_in = len(in_specs)

    def body(*refs):
        j = pl.program_id(0)
        i = pl.program_id(1)
        in_refs, out_refs = refs[:n_in], refs[n_in:]
        items, p = [], 0
        for kd in kinds:
            if kd[0] == 'row':
                items.append(in_refs[p][...])
                p += 1
            elif kd[0] == 'halo':
                prev, cen, nxt = in_refs[p][...], in_refs[p + 1][...], in_refs[p + 2][...]
                prev = jnp.where(i == 0, jnp.zeros_like(prev), prev)
                nxt = jnp.where(i == nrow - 1, jnp.zeros_like(nxt), nxt)
                items.append(jnp.concatenate([prev, cen, nxt], axis=0))
                p += 3
            else:
                items.append(in_refs[p])
                p += 1
        for o_ref, kd in zip(out_refs, okinds):
            if kd == 'acc':
                @pl.when((i == 0) & (j == 0))
                def _(o_ref=o_ref):
                    o_ref[...] = jnp.zeros_like(o_ref)
            elif kd == 'colacc':
                @pl.when(i == 0)
                def _(o_ref=o_ref):
                    o_ref[...] = jnp.zeros_like(o_ref)
        fn(i, j, *items, *out_refs)

    res = pl.pallas_call(
        body, name=name, grid=(ncol, nrow), in_specs=in_specs, out_specs=out_specs, out_shape=out_shapes,
        compiler_params=_cparams(("arbitrary", "arbitrary")),
    )(*args)
    return res


def _sigmoid(x):
    return 1.0 / (1.0 + jnp.exp(-x))


def _ln_stats(x):
    mu = jnp.mean(x, axis=1, keepdims=True)
    xc = x - mu
    var = jnp.mean(xc * xc, axis=1, keepdims=True)
    r = lax.rsqrt(var + LN_EPS)
    return xc * r, r


def _ln_bwd(dy, xh, r, g):
    dxh = dy * g
    dx = r * (dxh - jnp.mean(dxh, axis=1, keepdims=True) - xh * jnp.mean(dxh * xh, axis=1, keepdims=True))
    return dx, jnp.sum(dy * xh, axis=0, keepdims=True), jnp.sum(dy, axis=0, keepdims=True)


def _gmean(v, gm):
    hi = v.astype(BF16)
    lo = (v - hi.astype(F32)).astype(BF16)
    return jnp.dot(hi, gm, preferred_element_type=F32) + jnp.dot(lo, gm, preferred_element_type=F32)


def _roll_l(x, sh):
    return pltpu.roll(x, sh % x.shape[1], 1)


def _rope(x, c, sa, sb, sh):
    return x * c + _roll_l(x, -sh) * sa + _roll_l(x, sh) * sb


def _rope_t(dy, c, sa, sb, sh):
    return dy * c + _roll_l(dy * sa, sh) + _roll_l(dy * sb, -sh)


def _roll_r(x, s):
    return pltpu.roll(x, (-s) % x.shape[0], 0)


_GELU_C = math.sqrt(2.0 / math.pi)


def _gelu(x):
    t = jnp.tanh(_GELU_C * (x + 0.044715 * x * x * x))
    return 0.5 * x * (1.0 + t), t


def _gelu_grad(x, t):
    return 0.5 * (1.0 + t) + 0.5 * x * (1.0 - t * t) * _GELU_C * (1.0 + 3.0 * 0.044715 * x * x)


def _dot_nt(a, b):
    return lax.dot_general(a, b, (((1,), (1,)), ((), ())), preferred_element_type=F32)


def _dot_tn(a, b):
    return lax.dot_general(a, b, (((0,), (0,)), ((), ())), preferred_element_type=F32)


def _rope_block(s, d):
    t = jnp.arange(s)
    row = (t // GRID_W).astype(F32)
    col = (t % GRID_W).astype(F32)
    half = d // 4
    inv = ROPE_THETA ** (-jnp.arange(half, dtype=F32) / half)
    z = jnp.zeros((s, half), F32)
    cs, sas, sbs = [], [], []
    for pos in (row, col):
        ang = pos[:, None] * inv[None, :]
        co, si = jnp.cos(ang), jnp.sin(ang)
        cs += [co, co]
        sas += [-si, z]
        sbs += [z, si]
    return tuple(jnp.concatenate(v, axis=1) for v in (cs, sas, sbs))


def _rope_tables(s):
    gqa = tuple(jnp.tile(a, (1, 4)) for a in _rope_block(s, HEAD_DIM))
    c32, sa32, sb32 = _rope_block(s, MLA_ROPE)

    def head128(c, fill):
        return jnp.concatenate([jnp.full((s, 64), fill, F32), c, jnp.zeros((s, 32), F32)], axis=1)

    mla = tuple(jnp.tile(a, (1, 4)) for a in (head128(c32, 1.0), head128(sa32, 0.0), head128(sb32, 0.0)))
    kr = (head128(c32, 0.0), head128(sa32, 0.0), head128(sb32, 0.0))
    return dict(gqa=gqa, mla=mla, kr=kr)


def _group_mean_matrix(w, g):
    idx = np.arange(w) // g
    return jnp.asarray((idx[:, None] == idx[None, :]).astype(np.float32) / g, dtype=BF16)


TM = 512


def ln_fwd(name, s, x, g, b, res=None, alpha=1.0):
    ins = [('row', x, D_MODEL, 0)] + ([('row', res, D_MODEL, 0)] if res is not None else []) + [('full', g), ('full', b)]

    def fn(i, j, *a):
        if res is not None:
            xv, rv, g_ref, b_ref, o_ref = a
            z = alpha * rv + xv
        else:
            xv, g_ref, b_ref, o_ref = a
            z = xv
        xh, _ = _ln_stats(z)
        o_ref[...] = xh * g_ref[...] + b_ref[...]

    return tiled_call(name, fn, s, TM, ins, [('row', D_MODEL, D_MODEL, 0, F32)])[0]


def ln_bwd(name, s, dy, x, g, res=None, alpha=1.0):
    ins = [('row', dy, D_MODEL, 0), ('row', x, D_MODEL, 0)]
    ins += ([('row', res, D_MODEL, 0)] if res is not None else []) + [('full', g)]

    def fn(i, j, *a):
        a = list(a)
        dyv = a.pop(0)
        xv = a.pop(0)
        z = alpha * a.pop(0) + xv if res is not None else xv
        g_ref, dz_ref, dg_ref, db_ref = a
        xh, r = _ln_stats(z)
        dz, dg, db = _ln_bwd(dyv, xh, r, g_ref[...])
        dz_ref[...] = dz
        dg_ref[...] += dg
        db_ref[...] += db

    return tiled_call(name, fn, s, TM, ins, [('row', D_MODEL, D_MODEL, 0, F32), ('acc', (1, D_MODEL)), ('acc', (1, D_MODEL))])


def loss_and_ln_bwd(name, s, x, res, g, b, tgt, alpha):
    ins = [('row', x, D_MODEL, 0), ('row', res, D_MODEL, 0), ('row', tgt, D_MODEL, 0), ('full', g), ('full', b)]

    def fn(i, j, xv, rv, tv, g_ref, b_ref, dz_ref, dg_ref, db_ref, loss_ref):
        xh, r = _ln_stats(alpha * rv + xv)
        y = xh * g_ref[...] + b_ref[...]
        e = y - tv
        dz, dg, db = _ln_bwd(e * (1.0 / D_MODEL), xh, r, g_ref[...])
        dz_ref[...] = dz
        dg_ref[...] += dg
        db_ref[...] += db
        loss_ref[...] += jnp.sum(jnp.sum(e * e, axis=0, keepdims=True), axis=1, keepdims=True) * (0.5 / D_MODEL)

    return tiled_call(name, fn, s, TM, ins, [('row', D_MODEL, D_MODEL, 0, F32), ('acc', (1, D_MODEL)), ('acc', (1, D_MODEL)),
                                             ('acc', (1, 128))])


HALO_A = 16


def _glu(a_in):
    return a_in[:, :GROUP_W] * _sigmoid(a_in[:, GROUP_W:])


def a_fwd(name, s, proj, cw, cb, g, b):
    def fn(i, j, a_ext, cw_ref, cb_ref, g_ref, b_ref, y_ref, o_ref):
        a = _glu(a_ext)
        acc = jnp.zeros((TM, GROUP_W), F32)
        for k in range(CONV_A_WIDTH):
            acc = acc + _roll_r(a, k - 15)[HALO_A:HALO_A + TM] * cw_ref[pl.ds(k, 1), :]
        y = acc + cb_ref[...]
        y_ref[...] = y
        xh, _ = _ln_stats(y)
        z = xh * g_ref[...] + b_ref[...]
        o_ref[...] = z * _sigmoid(z)

    ins = [('halo', proj, 2 * GROUP_W, 0, HALO_A), ('full', cw), ('full', cb), ('full', g), ('full', b)]
    return tiled_call(name, fn, s, TM, ins, [('row', GROUP_W, GROUP_W, 0, F32), ('row', GROUP_W, GROUP_W, 0, F32)])


def a_bwd1(name, s, y, dmix, g, b):
    def fn(i, j, yv, do, g_ref, b_ref, dy_ref, dg_ref, db_ref, dcb_ref):
        xh, r = _ln_stats(yv)
        z = xh * g_ref[...] + b_ref[...]
        sg = _sigmoid(z)
        dz = do * sg * (1.0 + z * (1.0 - sg))
        dy, dg, db = _ln_bwd(dz, xh, r, g_ref[...])
        dy_ref[...] = dy
        dg_ref[...] += dg
        db_ref[...] += db
        dcb_ref[...] += jnp.sum(dy, axis=0, keepdims=True)

    ins = [('row', y, GROUP_W, 0), ('row', dmix, GROUP_W, 0), ('full', g), ('full', b)]
    return tiled_call(name, fn, s, TM, ins, [('row', GROUP_W, GROUP_W, 0, F32), ('acc', (1, GROUP_W)), ('acc', (1, GROUP_W)),
                                             ('acc', (1, GROUP_W))])


def a_bwd2(name, s, proj, dy, cw):
    def fn(i, j, a_ext, dy_ext, cw_ref, da_ref, dcw_ref):
        a1, a2 = a_ext[:, :GROUP_W], a_ext[:, GROUP_W:]
        sg = _sigmoid(a2)
        a = a1 * sg
        dyc = dy_ext[HALO_A:HALO_A + TM]
        da = jnp.zeros((TM, GROUP_W), F32)
        for k in range(CONV_A_WIDTH):
            da = da + _roll_r(dy_ext, 15 - k)[HALO_A:HALO_A + TM] * cw_ref[pl.ds(k, 1), :]
            dcw_ref[pl.ds(k, 1), :] += jnp.sum(dyc * _roll_r(a, k - 15)[HALO_A:HALO_A + TM], axis=0, keepdims=True)
        a1c, sgc = a1[HALO_A:HALO_A + TM], sg[HALO_A:HALO_A + TM]
        da_ref[...] = jnp.concatenate([da * sgc, da * a1c * sgc * (1.0 - sgc)], axis=1).astype(BF16)

    ins = [('halo', proj, 2 * GROUP_W, 0, HALO_A), ('halo', dy, GROUP_W, 0, HALO_A), ('full', cw)]
    return tiled_call(name, fn, s, TM, ins, [('row', 2 * GROUP_W, 2 * GROUP_W, 0, BF16), ('acc', (32, GROUP_W))])


def b_prep_fwd(name, s, proj, gq, gk, tabs, gm):
    c, sa, sb = tabs

    def fn(i, j, qkv, cv, sav, sbv, gq_ref, gk_ref, gm_ref, q_ref, k_ref, v_ref):
        q, k, v = qkv[:, :256], qkv[:, 256:384], qkv[:, 384:]
        gmv = gm_ref[...]
        qn = q * lax.rsqrt(_gmean(q * q, gmv) + RMS_EPS) * gq_ref[...]
        kn = k * lax.rsqrt(_gmean(k * k, gmv[:128, :128]) + RMS_EPS) * gk_ref[...]
        qr = _rope(qn, cv, sav, sbv, 16)
        kr = _rope(kn, cv[:, :128], sav[:, :128], sbv[:, :128], 16)
        for h in range(4):
            q_ref[h] = qr[:, h * 64:(h + 1) * 64].astype(BF16)
        for h in range(2):
            k_ref[h] = kr[:, h * 64:(h + 1) * 64].astype(BF16)
            v_ref[h] = v[:, h * 64:(h + 1) * 64].astype(BF16)

    ins = [('row', proj, 512, 1), ('row', c, 256, 0), ('row', sa, 256, 0), ('row', sb, 256, 0), ('full', gq), ('full', gk),
           ('full', gm)]
    return tiled_call(name, fn, s, TM, ins, [('hm', 4, 64, BF16), ('hm', 2, 64, BF16), ('hm', 2, 64, BF16)])


def b_prep_bwd(name, s, proj, gq, gk, tabs, gm, dq_hm, dk_hm, dv_hm):
    c, sa, sb = tabs

    def fn(i, j, qkv, cv, sav, sbv, gq_ref, gk_ref, gm_ref, dq_ref, dk_ref, dv_ref, dp_ref, dgq_ref, dgk_ref):
        q, k = qkv[:, :256], qkv[:, 256:384]
        gmv = gm_ref[...]
        dqr = jnp.concatenate([dq_ref[h] for h in range(4)], axis=1)
        dkr = jnp.concatenate([dk_ref[h] for h in range(2)], axis=1)
        dv = jnp.concatenate([dv_ref[h] for h in range(2)], axis=1)
        outs = []
        for x, dxr, g_ref, gmx, w, dg_ref in ((q, dqr, gq_ref, gmv, 256, dgq_ref), (k, dkr, gk_ref, gmv[:128, :128], 128, dgk_ref)):
            dn = _rope_t(dxr, cv[:, :w], sav[:, :w], sbv[:, :w], 16)
            r = lax.rsqrt(_gmean(x * x, gmx) + RMS_EPS)
            gv = g_ref[...]
            dx = gv * r * dn - x * (r * r * r) * _gmean(x * gv * dn, gmx)
            dgt = jnp.sum(x * r * dn, axis=0, keepdims=True)
            dg = dgt[:, 0:64]
            for h in range(1, w // 64):
                dg = dg + dgt[:, h * 64:(h + 1) * 64]
            dg_ref[...] += dg
            outs.append(dx)
        dp_ref[...] = jnp.concatenate(outs + [dv], axis=1).astype(BF16)

    ins = [('row', proj, 512, 1), ('row', c, 256, 0), ('row', sa, 256, 0), ('row', sb, 256, 0), ('full', gq), ('full', gk),
           ('full', gm), ('hm', dq_hm), ('hm', dk_hm), ('hm', dv_hm)]
    return tiled_call(name, fn, s, TM, ins, [('row', 512, 512, 0, BF16), ('acc', (1, 64)), ('acc', (1, 64))])


TQ = 256


def attn_fwd(name, s, q_hm, k_hm, v_hm, scale, comm=None):
    dk = q_hm.shape[2]
    kpb = k_hm.shape[0] // 2

    def body(q_ref, k_ref, v_ref, o_ref, l_ref):
        outs = []
        for g in range(2):
            kv = g if kpb == 2 else 0
            sc = _dot_nt(q_ref[g], k_ref[kv]) * scale
            m = jnp.max(sc, axis=1, keepdims=True)
            p = jnp.exp(sc - m)
            l = jnp.sum(p, axis=1, keepdims=True)
            o = jnp.dot(p.astype(BF16), v_ref[kv], preferred_element_type=F32)
            outs.append(o / l)
            l_ref[g] = m + jnp.log(l)
        o_ref[...] = jnp.concatenate(outs, axis=1)

    res, cres = carried_call(
        name, body, (2, s // TQ),
        [pl.BlockSpec((2, TQ, dk), lambda p, i: (p, i, 0)),
         pl.BlockSpec((kpb, s, dk), lambda p, i: (p, 0, 0)),
         pl.BlockSpec((kpb, s, 64), lambda p, i: (p, 0, 0))],
        [pl.BlockSpec((TQ, 128), lambda p, i: (i, p)), pl.BlockSpec((2, TQ, 1), lambda p, i: (p, i, 0))],
        [jax.ShapeDtypeStruct((s, 256), F32), jax.ShapeDtypeStruct((4, s, 1), F32)], [],
        ("arbitrary", "arbitrary"), [q_hm, k_hm, v_hm], comm)
    return res if comm is None else (res, cres)


def attn_bwd(name, s, q_hm, k_hm, v_hm, o, lse, dmixcat, dcol, scale, comm=None):
    dk = q_hm.shape[2]
    nkv = k_hm.shape[0]
    shared = nkv == 2
    kvi = (lambda p, g, i: (p, 0, 0)) if shared else (lambda p, g, i: (2 * p + g, 0, 0))

    def body(q_ref, k_ref, v_ref, o_ref, l_ref, do_ref, dq_ref, dk_ref, dv_ref):
        g = pl.program_id(1)
        i = pl.program_id(2)
        first = (i == 0) & (g == 0) if shared else (i == 0)

        @pl.when(first)
        def _():
            dk_ref[...] = jnp.zeros_like(dk_ref)
            dv_ref[...] = jnp.zeros_like(dv_ref)

        dob, ob = do_ref[...], o_ref[...]
        do = jnp.where(g == 0, dob[:, :64], dob[:, 64:])
        ov = jnp.where(g == 0, ob[:, :64], ob[:, 64:])
        q, k, v = q_ref[0], k_ref[0], v_ref[0]
        sc = _dot_nt(q, k) * scale
        p = jnp.exp(sc - l_ref[0])
        dp = _dot_nt(do.astype(BF16), v)
        delta = jnp.sum(do * ov, axis=1, keepdims=True)
        ds = (p * (dp - delta) * scale).astype(BF16)
        dq_ref[0] = jnp.dot(ds, k, preferred_element_type=F32)
        dk_ref[0] += _dot_tn(ds, q)
        dv_ref[0] += _dot_tn(p.astype(BF16), do.astype(BF16))

    res, cres = carried_call(
        name, body, (2, 2, s // TQ),
        [pl.BlockSpec((1, TQ, dk), lambda p, g, i: (2 * p + g, i, 0)),
         pl.BlockSpec((1, s, dk), kvi), pl.BlockSpec((1, s, 64), kvi),
         pl.BlockSpec((TQ, 128), lambda p, g, i: (i, p)),
         pl.BlockSpec((1, TQ, 1), lambda p, g, i: (2 * p + g, i, 0)),
         pl.BlockSpec((TQ, 128), lambda p, g, i: (i, dcol + p))],
        [pl.BlockSpec((1, TQ, dk), lambda p, g, i: (2 * p + g, i, 0)),
         pl.BlockSpec((1, s, dk), kvi), pl.BlockSpec((1, s, 64), kvi)],
        [jax.ShapeDtypeStruct((4, s, dk), F32), jax.ShapeDtypeStruct((nkv, s, dk), F32), jax.ShapeDtypeStruct((nkv, s, 64), F32)],
        [], ("arbitrary", "arbitrary", "arbitrary"), [q_hm, k_hm, v_hm, o, lse, dmixcat], comm)
    return res if comm is None else (res, cres)


def _lane_group_masks(w, g):
    lane = lax.broadcasted_iota(jnp.int32, (1, w), 1)
    return [((lane >= k * g) & (lane < (k + 1) * g)).astype(F32) for k in range(w // g)]


def _sgu_gate(svn, w_ref, bias):
    masks = _lane_group_masks(GROUP_W, 64)
    outs = []
    for n in range(TM // CHUNK):
        x = svn[n * CHUNK:(n + 1) * CHUNK]
        acc = bias
        for g in range(4):
            acc = acc + jnp.dot(w_ref[g].astype(BF16), (x * masks[g]).astype(BF16), preferred_element_type=F32)
        outs.append(acc)
    return jnp.concatenate(outs, axis=0)


def c_fwd(name, s, proj, g, b, w, bias):
    def fn(i, j, cin, g_ref, b_ref, w_ref, bias_ref, o_ref):
        c, _ = _gelu(cin)
        xh, _ = _ln_stats(c[:, GROUP_W:])
        svn = xh * g_ref[...] + b_ref[...]
        o_ref[...] = c[:, :GROUP_W] * _sgu_gate(svn, w_ref, bias_ref[...])

    ins = [('row', proj, 512, 2), ('full', g), ('full', b), ('full', w), ('full', bias)]
    return tiled_call(name, fn, s, TM, ins, [('row', GROUP_W, GROUP_W, 0, F32)])[0]


def c_bwd(name, s, proj, g, b, w, wt, bias, dmixcat):
    def fn(i, j, cin, do, g_ref, b_ref, w_ref, wt_ref, bias_ref, dc_ref, dg_ref, db_ref, dw_ref, dbias_ref):
        c, t = _gelu(cin)
        u = c[:, :GROUP_W]
        xh, r = _ln_stats(c[:, GROUP_W:])
        svn = xh * g_ref[...] + b_ref[...]
        gate = _sgu_gate(svn, w_ref, bias_ref[...])
        du = do * gate
        dgate = do * u
        masks = _lane_group_masks(GROUP_W, 64)
        col = lax.broadcasted_iota(jnp.int32, (CHUNK, 128), 1)
        dsvn, dbias = [], jnp.zeros((CHUNK, 128), F32)
        for n in range(TM // CHUNK):
            dgc = dgate[n * CHUNK:(n + 1) * CHUNK]
            x = svn[n * CHUNK:(n + 1) * CHUNK]
            acc = jnp.zeros((CHUNK, GROUP_W), F32)
            for gi in range(4):
                dgm = (dgc * masks[gi]).astype(BF16)
                acc = acc + jnp.dot(wt_ref[gi].astype(BF16), dgm, preferred_element_type=F32)
                dw_ref[gi] += _dot_nt(dgm, (x * masks[gi]).astype(BF16))
                dbias = dbias + jnp.where(col == gi, jnp.sum(dgc * masks[gi], axis=1, keepdims=True), 0.0)
            dsvn.append(acc)
        dbias_ref[...] += dbias
        dsv, dg, db = _ln_bwd(jnp.concatenate(dsvn, axis=0), xh, r, g_ref[...])
        dg_ref[...] += dg
        db_ref[...] += db
        dc_ref[...] = (jnp.concatenate([du, dsv], axis=1) * _gelu_grad(cin, t)).astype(BF16)

    ins = [('row', proj, 512, 2), ('row', dmixcat, GROUP_W, 2), ('full', g), ('full', b), ('full', w), ('full', wt), ('full', bias)]
    return tiled_call(name, fn, s, TM, ins, [('row', 512, 512, 0, BF16), ('acc', (1, GROUP_W)), ('acc', (1, GROUP_W)),
                                             ('acc', (4, CHUNK, CHUNK)), ('acc', (CHUNK, 128))])


def _d_common(dpart, gq_ref, gkv_ref):
    cq, ckv, kr = dpart[:, :256], dpart[:, 256:384], dpart[:, 384:]
    rq = lax.rsqrt(jnp.sum(cq * cq, axis=1, keepdims=True) * (1.0 / MLA_Q_LORA) + RMS_EPS)
    rkv = lax.rsqrt(jnp.mean(ckv * ckv, axis=1, keepdims=True) + RMS_EPS)
    return cq, ckv, kr, rq, rkv, cq * rq * gq_ref[...], ckv * rkv * gkv_ref[...]


def d_prep_fwd(name, s, proj, gq, gkv, wuq, wukv, tm_, tkr):
    cm, sam, sbm = tm_
    ck, sak, sbk = tkr

    def fn(i, j, dpart, cmv, samv, sbmv, ckv_, sakv, sbkv, gq_ref, gkv_ref, wuq_ref, wukv_ref, q_ref, k_ref, v_ref):
        cq, ckv, kr, rq, rkv, cqn, ckvn = _d_common(dpart, gq_ref, gkv_ref)
        qd = jnp.dot(cqn.astype(BF16), wuq_ref[...], preferred_element_type=F32)
        qf = _rope(qd, cmv, samv, sbmv, 8)
        kvd = jnp.dot(ckvn.astype(BF16), wukv_ref[...], preferred_element_type=F32)
        krr = _rope(kr, ckv_, sakv, sbkv, 8)
        for h in range(4):
            q_ref[h] = qf[:, h * 128:(h + 1) * 128].astype(BF16)
            k_ref[h] = (kvd[:, h * 128:(h + 1) * 128] + krr).astype(BF16)
            v_ref[h] = kvd[:, 512 + h * 64:512 + (h + 1) * 64].astype(BF16)

    ins = [('row', proj, 512, 3), ('row', cm, 512, 0), ('row', sam, 512, 0), ('row', sbm, 512, 0), ('row', ck, 128, 0),
           ('row', sak, 128, 0), ('row', sbk, 128, 0), ('full', gq), ('full', gkv), ('full', wuq), ('full', wukv)]
    return tiled_call(name, fn, s, TM, ins, [('hm', 4, 128, BF16), ('hm', 4, 128, BF16), ('hm', 4, 64, BF16)])


def d_prep_bwd(name, s, proj, gq, gkv, wuq, wukv, tm_, tkr, dq_hm, dk_hm, dv_hm):
    cm, sam, sbm = tm_
    ck, sak, sbk = tkr

    def fn(i, j, dpart, cmv, samv, sbmv, ckv_, sakv, sbkv, gq_ref, gkv_ref, wuq_ref, wukv_ref, dq_ref, dk_ref, dv_ref,
           dd_ref, dgq_ref, dgkv_ref, dwuq_ref, dwukv_ref):
        cq, ckv, kr, rq, rkv, cqn, ckvn = _d_common(dpart, gq_ref, gkv_ref)
        dqf = jnp.concatenate([dq_ref[h] for h in range(4)], axis=1)
        dkf = [dk_ref[h] for h in range(4)]
        dqd = _rope_t(dqf, cmv, samv, sbmv, 8).astype(BF16)
        dkvd = jnp.concatenate(dkf + [dv_ref[h] for h in range(4)], axis=1).astype(BF16)
        dkr = _rope_t(dkf[0] + dkf[1] + dkf[2] + dkf[3], ckv_, sakv, sbkv, 8)
        dcqn = _dot_nt(dqd, wuq_ref[...])
        dckvn = _dot_nt(dkvd, wukv_ref[...])
        dwuq_ref[...] += _dot_tn(cqn.astype(BF16), dqd)
        dwukv_ref[...] += _dot_tn(ckvn.astype(BF16), dkvd)
        gqv, gkvv = gq_ref[...], gkv_ref[...]
        dcq = gqv * rq * dcqn - cq * (rq * rq * rq) * (jnp.sum(cq * gqv * dcqn, axis=1, keepdims=True) * (1.0 / MLA_Q_LORA))
        dckv = gkvv * rkv * dckvn - ckv * (rkv * rkv * rkv) * jnp.mean(ckv * gkvv * dckvn, axis=1, keepdims=True)
        dgq_ref[...] += jnp.sum(cq * rq * dcqn, axis=0, keepdims=True)
        dgkv_ref[...] += jnp.sum(ckv * rkv * dckvn, axis=0, keepdims=True)
        dd_ref[...] = jnp.concatenate([dcq, dckv, dkr], axis=1).astype(BF16)

    ins = [('row', proj, 512, 3), ('row', cm, 512, 0), ('row', sam, 512, 0), ('row', sbm, 512, 0), ('row', ck, 128, 0),
           ('row', sak, 128, 0), ('row', sbk, 128, 0), ('full', gq), ('full', gkv), ('full', wuq), ('full', wukv),
           ('hm', dq_hm), ('hm', dk_hm), ('hm', dv_hm)]
    return tiled_call(name, fn, s, TM, ins, [('row', 512, 512, 0, BF16), ('acc', (1, 256)), ('acc', (1, 128)),
                                             ('acc', (256, 512)), ('acc', (128, 768))])


HALO_F = 8
TN_F = 256
TM_F = 512


def _conv3(ext, w_ref):
    return _roll_r(ext, -1) * w_ref[pl.ds(0, 1), :] + ext * w_ref[pl.ds(1, 1), :] + _roll_r(ext, 1) * w_ref[pl.ds(2, 1), :]


def f_fwd(name, s, upraw, cw, cb):
    ncj = D_FF // TN_F

    def fn(i, j, xa, xg, wa_ref, wg_ref, ba_ref, bg_ref, o_ref):
        ua = (_conv3(xa, wa_ref) + ba_ref[...])[HALO_F:HALO_F + TM_F]
        ug = (_conv3(xg, wg_ref) + bg_ref[...])[HALO_F:HALO_F + TM_F]
        o_ref[...] = (ua * _sigmoid(ua) * ug).astype(BF16)

    ins = [('halo', upraw, TN_F, lambda j: j, HALO_F), ('halo', upraw, TN_F, lambda j: j + ncj, HALO_F),
           ('col', cw, TN_F, lambda j: j), ('col', cw, TN_F, lambda j: j + ncj),
           ('col', cb, TN_F, lambda j: j), ('col', cb, TN_F, lambda j: j + ncj)]
    return tiled_call(name, fn, s, TM_F, ins, [('row', D_FF, TN_F, lambda j: j, BF16)], ncol=ncj)[0]


def f_bwd(name, s, upraw, dact, cw, cb):
    ncj = D_FF // TN_F

    def fn(i, j, xa, xg, da, wa_ref, wg_ref, ba_ref, bg_ref, dxa_ref, dxg_ref, dwa_ref, dwg_ref, dba_ref, dbg_ref):
        ua = _conv3(xa, wa_ref) + ba_ref[...]
        ug = _conv3(xg, wg_ref) + bg_ref[...]
        sg = _sigmoid(ua)
        dug = da * ua * sg
        dua = da * ug * sg * (1.0 + ua * (1.0 - sg))
        cen = slice(HALO_F, HALO_F + TM_F)
        for du, x, w_ref, dx_ref, dw_ref, db_ref in ((dua, xa, wa_ref, dxa_ref, dwa_ref, dba_ref),
                                                      (dug, xg, wg_ref, dxg_ref, dwg_ref, dbg_ref)):
            dx = _roll_r(du, 1) * w_ref[pl.ds(0, 1), :] + du * w_ref[pl.ds(1, 1), :] + _roll_r(du, -1) * w_ref[pl.ds(2, 1), :]
            dx_ref[...] = dx[cen].astype(BF16)
            duc = du[cen]
            for k in range(3):
                dw_ref[pl.ds(k, 1), :] += jnp.sum(duc * _roll_r(x, k - 1)[cen], axis=0, keepdims=True)
            db_ref[...] += jnp.sum(duc, axis=0, keepdims=True)

    ins = [('halo', upraw, TN_F, lambda j: j, HALO_F), ('halo', upraw, TN_F, lambda j: j + ncj, HALO_F),
           ('halo', dact, TN_F, lambda j: j, HALO_F),
           ('col', cw, TN_F, lambda j: j), ('col', cw, TN_F, lambda j: j + ncj),
           ('col', cb, TN_F, lambda j: j), ('col', cb, TN_F, lambda j: j + ncj)]
    outs = [('row', D_FF, TN_F, lambda j: j, BF16), ('row', D_FF, TN_F, lambda j: j, BF16),
            ('colacc', 8, D_FF, TN_F, lambda j: j), ('colacc', 8, D_FF, TN_F, lambda j: j),
            ('colacc', 1, D_FF, TN_F, lambda j: j), ('colacc', 1, D_FF, TN_F, lambda j: j)]
    return tiled_call(name, fn, s, TM_F, ins, outs, ncol=ncj)


def _pad_w_in(w):
    z = lambda n: jnp.zeros((w.shape[0], n), w.dtype)
    return jnp.concatenate([w[:, :1728], z(64), w[:, 1728:1856], z(64), w[:, 1856:1888], z(32)], axis=1)


def _unpad_w_in(g):
    return jnp.concatenate([g[:, :1728], g[:, 1792:1920], g[:, 1984:2016]], axis=1)


def _pad_uq(w):
    w = w.reshape(MLA_Q_LORA, 4, 96)
    w = jnp.concatenate([w, jnp.zeros((MLA_Q_LORA, 4, 32), w.dtype)], axis=2).reshape(MLA_Q_LORA, 512)
    return jnp.concatenate([w, jnp.zeros((64, 512), w.dtype)], axis=0)


def _unpad_uq(g):
    return g[:MLA_Q_LORA].reshape(MLA_Q_LORA, 4, 128)[:, :, :96].reshape(MLA_Q_LORA, 384)


def _pad_ukv(w):
    w = w.reshape(MLA_KV_LORA, 4, 128)
    kn = jnp.concatenate([w[:, :, :64], jnp.zeros((MLA_KV_LORA, 4, 64), w.dtype)], axis=2).reshape(MLA_KV_LORA, 512)
    return jnp.concatenate([kn, w[:, :, 64:].reshape(MLA_KV_LORA, 256)], axis=1)


def _unpad_ukv(g):
    kn = g[:, :512].reshape(MLA_KV_LORA, 4, 128)[:, :, :64]
    v = g[:, 512:].reshape(MLA_KV_LORA, 4, 64)
    return jnp.concatenate([kn, v], axis=2).reshape(MLA_KV_LORA, 512)


def _row(v, pad_to=None):
    v = v.reshape(1, -1)
    if pad_to is not None and v.shape[1] < pad_to:
        v = jnp.concatenate([v, jnp.zeros((1, pad_to - v.shape[1]), v.dtype)], axis=1)
    return v


LATE_NAMES = ['w_out', 'ffn_w_up', 'ffn_w_down']


def _assemble(n, part, g, me):
    return jnp.concatenate([jnp.where(me == q, part, g[q]) for q in range(N_CHIPS)], axis=SHARDED_AXIS[n])


def local_step(x, tgt, W, dist=None):
    s = x.shape[0]
    tabs = _rope_tables(s)
    gm = _group_mean_matrix(256, 64)
    alpha = DEEPNORM_ALPHA
    acts = []
    late = {(n, l): W[n][l] for n in LATE_NAMES for l in range(DEPTH)} if dist is None else {}
    h = ln_fwd("ln_in", s, x, _row(W['ln_in_g']), _row(W['ln_in_b']))
    for l in range(DEPTH):
        A = dict(h=h)
        A['wpad'] = _pad_w_in(W['w_in'][l]).astype(BF16)
        A['wuq'] = _pad_uq(W['mla_w_uq'][l]).astype(BF16)
        A['wukv'] = _pad_ukv(W['mla_w_ukv'][l]).astype(BF16)
        A['cw'] = jnp.concatenate([W['conv_a_w'][l], jnp.zeros((1, GROUP_W), F32)], axis=0)
        A['fcw'] = jnp.concatenate([W['ffn_conv_w'][l], jnp.zeros((5, 2 * D_FF), F32)], axis=0)
        A['gq'] = jnp.tile(_row(W['qk_norm_q'][l]), (1, 4))
        A['gk'] = jnp.tile(_row(W['qk_norm_k'][l]), (1, 2))
        A['sgu_bias'] = jnp.repeat(W['sgu_b'][l].T, 64, axis=1)
        A['sgu_wt'] = jnp.swapaxes(W['sgu_w'][l], 1, 2)
        proj = mm(f"proj{l}", h, A['wpad'])
        A['proj'] = proj
        A['y_a'], o_a = a_fwd(f"a_fwd{l}", s, proj, A['cw'], _row(W['conv_a_b'][l]), _row(W['ln_a_g'][l]), _row(W['ln_a_b'][l]))
        A['bq'], A['bk'], A['bv'] = b_prep_fwd(f"b_prep{l}", s, proj, A['gq'], A['gk'], tabs['gqa'], gm)
        if dist is not None and l == 0:
            lp = dist['late']
            parts = [lp['w_out'], lp['ffn_w_up'][0:1], lp['ffn_w_down'][0:1]]
            (A['o_b'], A['lse_b']), got = attn_fwd(f"b_attn{l}", s, A['bq'], A['bk'], A['bv'], HEAD_DIM ** -0.5,
                                                   comm=gather_over_chips(parts))
            full = [_assemble(n, p, g, dist['me']) for n, p, g in zip(LATE_NAMES, parts, got)]
            late[('w_out', 0)], late[('w_out', 1)] = full[0][0], full[0][1]
            late[('ffn_w_up', 0)], late[('ffn_w_down', 0)] = full[1][0], full[2][0]
        else:
            A['o_b'], A['lse_b'] = attn_fwd(f"b_attn{l}", s, A['bq'], A['bk'], A['bv'], HEAD_DIM ** -0.5)
        o_c = c_fwd(f"c_fwd{l}", s, proj, _row(W['sgu_ln_g'][l]), _row(W['sgu_ln_b'][l]), W['sgu_w'][l], A['sgu_bias'])
        A['dq'], A['dk'], A['dv'] = d_prep_fwd(f"d_prep{l}", s, proj, _row(W['mla_q_norm'][l], 256), _row(W['mla_kv_norm'][l]),
                                               A['wuq'], A['wukv'], tabs['mla'], tabs['kr'])
        if dist is not None and l == 0:
            parts = [lp['ffn_w_up'][1:2], lp['ffn_w_down'][1:2]]
            (A['o_d'], A['lse_d']), got = attn_fwd(f"d_attn{l}", s, A['dq'], A['dk'], A['dv'], (MLA_NOPE + MLA_ROPE) ** -0.5,
                                                   comm=gather_over_chips(parts))
            late[('ffn_w_up', 1)] = _assemble('ffn_w_up', parts[0], got[0], dist['me'])[0]
            late[('ffn_w_down', 1)] = _assemble('ffn_w_down', parts[1], got[1], dist['me'])[0]
        else:
            A['o_d'], A['lse_d'] = attn_fwd(f"d_attn{l}", s, A['dq'], A['dk'], A['dv'], (MLA_NOPE + MLA_ROPE) ** -0.5)
        A['wout'] = late[('w_out', l)].astype(BF16)
        A['wup'] = late[('ffn_w_up', l)].astype(BF16)
        A['wdown'] = late[('ffn_w_down', l)].astype(BF16)
        A['mixcat'] = jnp.concatenate([o_a, A['o_b'], o_c, A['o_d']], axis=1).astype(BF16)
        A['mix'] = mm(f"out_proj{l}", A['mixcat'], A['wout'])
        h1 = ln_fwd(f"ln_mix{l}", s, A['mix'], _row(W['ln_mix_g'][l]), _row(W['ln_mix_b'][l]), res=h, alpha=alpha)
        A['h1'] = h1
        A['upraw'] = mm(f"ffn_up{l}", h1, A['wup'])
        A['act'] = f_fwd(f"f_fwd{l}", s, A['upraw'], A['fcw'], _row(W['ffn_conv_b'][l]))
        A['f'] = mm(f"ffn_down{l}", A['act'], A['wdown'])
        if l < DEPTH - 1:
            h = ln_fwd(f"ln_ffn{l}", s, A['f'], _row(W['ln_ffn_g'][l]), _row(W['ln_ffn_b'][l]), res=h1, alpha=alpha)
        acts.append(A)

    per_layer = {n: [None] * DEPTH for n in WEIGHT_NAMES if n not in ('ln_in_g', 'ln_in_b')}
    dh = None
    loss_row = None
    overlap = dist is not None and DEPTH == 2
    red = {}
    grp1 = [(n, 1) for n in BIG_NAMES]
    grp2 = [('ffn_w_up', 0), ('ffn_w_down', 0)]

    def carried(fn, name, *a, comm=None, **kw):
        if comm is None:
            return fn(name, *a, **kw), None
        return fn(name, *a, comm=comm, **kw)

    def pair_sums(tag, xs, r1):
        return [pair_sum(f"pair_sum_{tag}{t}", xv, rv, dist['idx'], BF16) for t, (xv, rv) in enumerate(zip(xs, r1))]

    def chip_sums(tag, cs, ps):
        return [chip_sum(f"chip_sum_{tag}{t}", [cv], [pv], dist['idx']) for t, (cv, pv) in enumerate(zip(cs, ps))]

    for l in reversed(range(DEPTH)):
        A = acts[l]
        carry = overlap and l == 0
        g_ffn, b_ffn = _row(W['ln_ffn_g'][l]), _row(W['ln_ffn_b'][l])
        if l == DEPTH - 1:
            dz, dg, db, loss_row = loss_and_ln_bwd("loss", s, A['f'], A['h1'], g_ffn, b_ffn, tgt, alpha)
        else:
            dz, dg, db = ln_bwd(f"ln_ffn_bwd{l}", s, dh, A['f'], g_ffn, res=A['h1'], alpha=alpha)
        per_layer['ln_ffn_g'][l], per_layer['ln_ffn_b'][l] = dg[0], db[0]
        xs1 = [per_layer[n][k] for n, k in grp1] if carry else None
        dwd, r1 = carried(mm, f"dw_down{l}", A['act'], dz, ta=True, comm=pair_exchange(xs1) if carry else None)
        per_layer['ffn_w_down'][l] = dwd.reshape(N_CHIPS, D_FF // N_CHIPS, D_MODEL)
        ps1 = pair_sums("g1_", xs1, r1) if carry else None
        dact = mm(f"d_act{l}", dz, A['wdown'], tb=True)
        dxa, dxg, dwa, dwg, dba, dbg = f_bwd(f"f_bwd{l}", s, A['upraw'], dact, A['fcw'], _row(W['ffn_conv_b'][l]))
        per_layer['ffn_conv_w'][l] = jnp.concatenate([dwa[:3], dwg[:3]], axis=1)
        per_layer['ffn_conv_b'][l] = jnp.concatenate([dba[0], dbg[0]], axis=0)
        dup = jnp.concatenate([dxa, dxg], axis=1)
        per_layer['ffn_w_up'][l] = mm(f"dw_up{l}", A['h1'], dup, ta=True, chip_major=True)
        xs2 = [per_layer[n][k] for n, k in grp2] if carry else None
        dh1_mm, r2 = carried(mm, f"d_h1{l}", dup, A['wup'], tb=True, comm=pair_exchange(xs2) if carry else None)
        ps2 = pair_sums("g2_", xs2, r2) if carry else None
        dz1, dg, db = _ln_mix_bwd(l, s, dz, dh1_mm, A, W, alpha)
        per_layer['ln_mix_g'][l], per_layer['ln_mix_b'][l] = dg[0], db[0]
        per_layer['w_out'][l] = mm(f"dw_out{l}", A['mixcat'], dz1, ta=True).reshape(N_CHIPS, D_MODEL // N_CHIPS, D_MODEL)
        dmixcat = mm(f"d_mixcat{l}", dz1, A['wout'], tb=True)
        dy_a, dg, db, dcb = a_bwd1(f"a_bwd1{l}", s, A['y_a'], dmixcat, _row(W['ln_a_g'][l]), _row(W['ln_a_b'][l]))
        per_layer['ln_a_g'][l], per_layer['ln_a_b'][l], per_layer['conv_a_b'][l] = dg[0], db[0], dcb[0]
        dp_a, dcw = a_bwd2(f"a_bwd2{l}", s, A['proj'], dy_a, A['cw'])
        per_layer['conv_a_w'][l] = dcw[:CONV_A_WIDTH]
        (dq, dk, dv), cs1 = carried(attn_bwd, f"b_attn_bwd{l}", s, A['bq'], A['bk'], A['bv'], A['o_b'], A['lse_b'], dmixcat, 2,
                                    HEAD_DIM ** -0.5, comm=chip_exchange(ps1) if carry else None)
        halves1 = chip_sums("g1_", cs1, ps1) if carry else None
        dp_b, dgq, dgk = b_prep_bwd(f"b_prep_bwd{l}", s, A['proj'], A['gq'], A['gk'], tabs['gqa'], gm, dq, dk, dv)
        per_layer['qk_norm_q'][l], per_layer['qk_norm_k'][l] = dgq[0], dgk[0]
        dp_c, dg, db, dsw, dsb = c_bwd(f"c_bwd{l}", s, A['proj'], _row(W['sgu_ln_g'][l]), _row(W['sgu_ln_b'][l]), W['sgu_w'][l],
                                       A['sgu_wt'], A['sgu_bias'], dmixcat)
        per_layer['sgu_ln_g'][l], per_layer['sgu_ln_b'][l] = dg[0], db[0]
        per_layer['sgu_w'][l], per_layer['sgu_b'][l] = dsw, dsb[:, :4].T
        comms = [half_exchange(halves1), chip_exchange(ps2)] if carry else []
        (dq, dk, dv), cres = carried(attn_bwd, f"d_attn_bwd{l}", s, A['dq'], A['dk'], A['dv'], A['o_d'], A['lse_d'], dmixcat, 6,
                                     (MLA_NOPE + MLA_ROPE) ** -0.5, comm=join_comms(comms))
        if carry:
            done1, cs2 = split_comm_results(comms, cres)
            red.update(zip(grp1, done1))
            halves2 = chip_sums("g2_", cs2, ps2)
        dp_d, dgq, dgkv, dwuq, dwukv = d_prep_bwd(f"d_prep_bwd{l}", s, A['proj'], _row(W['mla_q_norm'][l], 256),
                                                  _row(W['mla_kv_norm'][l]), A['wuq'], A['wukv'], tabs['mla'], tabs['kr'],
                                                  dq, dk, dv)
        per_layer['mla_q_norm'][l], per_layer['mla_kv_norm'][l] = dgq[0, :MLA_Q_LORA], dgkv[0]
        per_layer['mla_w_uq'][l], per_layer['mla_w_ukv'][l] = _unpad_uq(dwuq), _unpad_ukv(dwukv)
        dproj = jnp.concatenate([dp_a, dp_b, dp_c, dp_d], axis=1)
        dw_in, done2 = carried(mm, f"dw_in{l}", A['h'], dproj, ta=True, comm=half_exchange(halves2) if carry else None)
        if carry:
            red.update(zip(grp2, done2))
        per_layer['w_in'][l] = _unpad_w_in(dw_in).reshape(D_MODEL, N_CHIPS, D_IN_PROJ // N_CHIPS).transpose(1, 0, 2)
        dh = mm(f"d_h{l}", dproj, A['wpad'], tb=True, add=dz1, add_scale=alpha)
    dx, dg, db = ln_bwd("ln_in_bwd", s, dh, x, _row(W['ln_in_g']))
    G = dict(per_layer)
    G['ln_in_g'], G['ln_in_b'] = dg[0], db[0]
    for n, k in red:
        G[n][k] = None
    G['reduced'] = red
    return loss_row, dx, G


def _ln_mix_bwd(l, s, dz, dh1_mm, A, W, alpha):
    ins = [('row', dz, D_MODEL, 0), ('row', dh1_mm, D_MODEL, 0), ('row', A['mix'], D_MODEL, 0), ('row', A['h'], D_MODEL, 0),
           ('full', _row(W['ln_mix_g'][l]))]

    def fn(i, j, dzv, dmv, xv, rv, g_ref, o_ref, dg_ref, db_ref):
        xh, r = _ln_stats(alpha * rv + xv)
        d, dg, db = _ln_bwd(alpha * dzv + dmv, xh, r, g_ref[...])
        o_ref[...] = d
        dg_ref[...] += dg
        db_ref[...] += db

    return tiled_call(f"ln_mix_bwd{l}", fn, s, TM, ins, [('row', D_MODEL, D_MODEL, 0, F32), ('acc', (1, D_MODEL)),
                                                         ('acc', (1, D_MODEL))])


ANY = pl.BlockSpec(memory_space=pl.ANY)


def _pos():
    return lax.axis_index("x"), lax.axis_index("y"), lax.axis_index("c")


def _rcopy(src, dst, ssem, rsem, dev):
    return pltpu.make_async_remote_copy(src_ref=src, dst_ref=dst, send_sem=ssem, recv_sem=rsem, device_id=dev,
                                        device_id_type=MESH)


class Comm:
    def __init__(self, operands, out_shape, sems, start, finish, aliases=None):
        self.operands, self.out_shape, self.sems = list(operands), list(out_shape), list(sems)
        self.start, self.finish, self.aliases = start, finish, dict(aliases or {})


def join_comms(comms):
    comms = [cm for cm in comms if cm is not None]
    if not comms:
        return None
    offs, oi, oo, os_ = [], 0, 0, 0
    for cm in comms:
        offs.append((oi, oo, os_))
        oi, oo, os_ = oi + len(cm.operands), oo + len(cm.out_shape), os_ + len(cm.sems)

    def part(fn_name):
        def run(ins, outs, sems):
            for cm, (a, b, s) in zip(comms, offs):
                getattr(cm, fn_name)(ins[a:a + len(cm.operands)], outs[b:b + len(cm.out_shape)], sems[s:s + len(cm.sems)])
        return run

    aliases = {a + i: b + o for cm, (a, b, s) in zip(comms, offs) for i, o in cm.aliases.items()}
    return Comm([v for cm in comms for v in cm.operands], [v for cm in comms for v in cm.out_shape],
                [v for cm in comms for v in cm.sems], part("start"), part("finish"), aliases)


def split_comm_results(comms, res):
    out, p = [], 0
    for cm in comms:
        if cm is None:
            out.append(None)
        else:
            out.append(list(res[p:p + len(cm.out_shape)]))
            p += len(cm.out_shape)
    return out


def run_comm(name, comm):
    n_in, n_out = len(comm.operands), len(comm.out_shape)

    def body(*refs):
        ins, outs, sems = refs[:n_in], refs[n_in:n_in + n_out], refs[n_in + n_out:]
        comm.start(ins, outs, sems)
        comm.finish(ins, outs, sems)

    return pl.pallas_call(body, name=name, in_specs=[ANY] * n_in, out_specs=[ANY] * n_out, out_shape=comm.out_shape,
                          scratch_shapes=[pltpu.SemaphoreType.DMA((k,)) for k in comm.sems],
                          input_output_aliases=comm.aliases)(*comm.operands)


def gather_over_chips(parts):
    n = len(parts)
    split = [p.shape[1] % 32 == 0 for p in parts]

    def geometry():
        x, y, c = _pos()
        return x, y, c, 2 * x + y, [(1 - x, y), (x, 1 - y), (1 - x, 1 - y)]

    def rows(t, half):
        a2 = parts[t].shape[1] // 2
        return pl.ds(half * a2, a2)

    def ici(srcs, dsts, sems, j, t, px, py, c, me):
        k = n * j + t
        if split[t]:
            return _rcopy(srcs[t].at[:, rows(t, c), :], dsts[t].at[me, :, rows(t, c), :], sems[0].at[k], sems[1].at[k], (px, py, c))
        return _rcopy(srcs[t], dsts[t].at[me], sems[0].at[k], sems[1].at[k], (px, py, c))

    def start(srcs, dsts, sems):
        x, y, c, me, chips = geometry()
        for j, (px, py) in enumerate(chips):
            for t in range(n):
                ici(srcs, dsts, sems, j, t, px, py, c, me).start()

    def finish(srcs, dsts, sems):
        x, y, c, me, chips = geometry()
        fwd = []
        for j, (px, py) in enumerate(chips):
            q = 2 * px + py
            for t in range(n):
                k = n * j + t
                if split[t]:
                    got = dsts[t].at[q, :, rows(t, c), :]
                    _rcopy(got, got, sems[0].at[k], sems[1].at[k], (px, py, c)).wait_recv()
                    fw = _rcopy(got, got, sems[2].at[k], sems[3].at[k], (x, y, 1 - c))
                    fw.start()
                    fwd.append(fw)
                else:
                    _rcopy(srcs[t], dsts[t].at[q], sems[0].at[k], sems[1].at[k], (px, py, c)).wait_recv()
        for j, (px, py) in enumerate(chips):
            q = 2 * px + py
            for t in range(n):
                if split[t]:
                    other = dsts[t].at[q, :, rows(t, 1 - c), :]
                    _rcopy(other, other, sems[2].at[n * j + t], sems[3].at[n * j + t], (x, y, 1 - c)).wait_recv()
                ici(srcs, dsts, sems, j, t, px, py, c, me).wait_send()
        for fw in fwd:
            fw.wait_send()

    return Comm(parts, [jax.ShapeDtypeStruct((N_CHIPS,) + p.shape, p.dtype) for p in parts], [3 * n] * 4, start, finish)


def pair_exchange(xs):
    n = len(xs)

    def copies(srcs, dsts, sems):
        x, y, c = _pos()
        return [_rcopy(srcs[t].at[:, pl.ds((1 - c) * (xs[t].shape[1] // 2), xs[t].shape[1] // 2), :], dsts[t],
                       sems[0].at[t], sems[1].at[t], (x, y, 1 - c)) for t in range(n)]

    def start(srcs, dsts, sems):
        for cp in copies(srcs, dsts, sems):
            cp.start()

    def finish(srcs, dsts, sems):
        for cp in copies(srcs, dsts, sems):
            cp.wait()

    return Comm(xs, [jax.ShapeDtypeStruct((a.shape[0], a.shape[1] // 2, a.shape[2]), a.dtype) for a in xs], [n, n], start, finish)


def chip_exchange(ps):
    n = len(ps)

    def geometry():
        x, y, c = _pos()
        return c, 2 * x + y, [(1 - x, y), (x, 1 - y), (1 - x, 1 - y)]

    def start(srcs, dsts, sems):
        c, me, chips = geometry()
        for j, (px, py) in enumerate(chips):
            for t in range(n):
                _rcopy(srcs[t].at[2 * px + py], dsts[t].at[me], sems[0].at[n * j + t], sems[1].at[n * j + t], (px, py, c)).start()

    def finish(srcs, dsts, sems):
        c, me, chips = geometry()
        for j, (px, py) in enumerate(chips):
            for t in range(n):
                _rcopy(srcs[t].at[2 * px + py], dsts[t].at[2 * px + py], sems[0].at[n * j + t], sems[1].at[n * j + t], (px, py, c)).wait()

    return Comm(ps, [jax.ShapeDtypeStruct(p.shape, p.dtype) for p in ps], [3 * n, 3 * n], start, finish)


def half_exchange(bufs):
    n = len(bufs)

    def copies(ins, outs, sems):
        x, y, c = _pos()
        cps = []
        for t in range(n):
            a2 = bufs[t].shape[1] // 2
            mine = pl.ds(c * a2, a2)
            cps.append(_rcopy(ins[t].at[:, mine, :], outs[t].at[:, mine, :], sems[0].at[t], sems[1].at[t], (x, y, 1 - c)))
        return cps

    def start(ins, outs, sems):
        for cp in copies(ins, outs, sems):
            cp.start()

    def finish(ins, outs, sems):
        for cp in copies(ins, outs, sems):
            cp.wait()

    return Comm(bufs, [jax.ShapeDtypeStruct(b.shape, b.dtype) for b in bufs], [n, n], start, finish, {t: t for t in range(n)})


def gather_over_devices(a):
    def peers():
        x, y, c = _pos()
        return 4 * x + 2 * y + c, [((x + (k >> 2)) % 2, (y + ((k >> 1) & 1)) % 2, (c + (k & 1)) % 2) for k in range(1, N_DEV)]

    def start(ins, outs, sems):
        me, devs = peers()
        for j, dev in enumerate(devs):
            _rcopy(ins[0], outs[0].at[me], sems[0].at[j], sems[1].at[j], dev).start()

    def finish(ins, outs, sems):
        me, devs = peers()
        for j, (px, py, pc) in enumerate(devs):
            _rcopy(ins[0], outs[0].at[4 * px + 2 * py + pc], sems[0].at[j], sems[1].at[j], (px, py, pc)).wait()

    return Comm([a], [jax.ShapeDtypeStruct((N_DEV,) + a.shape, a.dtype)], [N_DEV - 1, N_DEV - 1], start, finish)


LANES = 1024
ROW_TILE_BYTES = 2 * 1024 * 1024


def _row_tile(r, row_bytes):
    if r * row_bytes <= ROW_TILE_BYTES:
        return r
    best = None
    for t in range(16, r, 16):
        if r % t == 0 and t * row_bytes <= ROW_TILE_BYTES:
            best = t
    assert best is not None, (r, row_bytes)
    return best


def pair_sum(name, x, r1, c_arr, out_dtype):
    q, a2, b = r1.shape
    tr = _row_tile(a2, b * 4)
    nb = a2 // tr

    def body(c_ref, x_ref, r_ref, o_ref):
        o_ref[...] = (x_ref[...] + r_ref[...]).astype(out_dtype)

    grid_spec = pltpu.PrefetchScalarGridSpec(
        num_scalar_prefetch=1, grid=(q, nb),
        in_specs=[pl.BlockSpec((None, tr, b), lambda k, i, c_ref: (k, c_ref[0] * nb + i, 0)),
                  pl.BlockSpec((None, tr, b), lambda k, i, c_ref: (k, i, 0))],
        out_specs=pl.BlockSpec((None, tr, b), lambda k, i, c_ref: (k, i, 0)))
    return pl.pallas_call(body, name=name, grid_spec=grid_spec, out_shape=jax.ShapeDtypeStruct(r1.shape, out_dtype),
                          compiler_params=_cparams(("parallel", "parallel")))(c_arr, x, r1)


def device_sum(name, recv, own, idx):
    k, r, b = recv.shape

    def body(idx_ref, r_ref, o_ref, s_ref):
        me = 2 * idx_ref[1] + idx_ref[0]
        acc = None
        for q in range(k):
            term = jnp.where(me == q, o_ref[...], r_ref[q])
            acc = term if acc is None else acc + term
        s_ref[...] = acc

    grid_spec = pltpu.PrefetchScalarGridSpec(
        num_scalar_prefetch=1, grid=(1,),
        in_specs=[pl.BlockSpec((k, r, b), lambda i, idx_ref: (0, 0, 0)), pl.BlockSpec((r, b), lambda i, idx_ref: (0, 0))],
        out_specs=pl.BlockSpec((r, b), lambda i, idx_ref: (0, 0)))
    return pl.pallas_call(body, name=name, grid_spec=grid_spec, out_shape=jax.ShapeDtypeStruct((r, b), F32),
                          compiler_params=_cparams(("arbitrary",)))(idx, recv, own)


def chip_sum(name, recv, own, idx):
    nl = len(recv)
    k, a2, b = recv[0].shape
    tr = _row_tile(a2, k * b * recv[0].dtype.itemsize)
    nb = a2 // tr

    def body(idx_ref, *refs):
        rs, os_, o_ref = refs[:nl], refs[nl:2 * nl], refs[2 * nl]
        l = pl.program_id(0)
        me = idx_ref[1]
        for li in range(nl):
            @pl.when(l == li)
            def _(li=li):
                acc = None
                for q in range(k):
                    term = jnp.where(me == q, os_[li][...], rs[li][q]).astype(F32)
                    acc = term if acc is None else acc + term
                o_ref[...] = acc

    def pick(li):
        return lambda l, i, idx_ref: jnp.where(l == li, i, 0)

    in_specs = [pl.BlockSpec((k, tr, b), lambda l, i, idx_ref, _p=pick(li): (0, _p(l, i, idx_ref), 0)) for li in range(nl)]
    in_specs += [pl.BlockSpec((None, tr, b), lambda l, i, idx_ref, _p=pick(li): (idx_ref[1], _p(l, i, idx_ref), 0)) for li in range(nl)]
    grid_spec = pltpu.PrefetchScalarGridSpec(
        num_scalar_prefetch=1, grid=(nl, nb), in_specs=in_specs,
        out_specs=pl.BlockSpec((None, tr, b), lambda l, i, idx_ref: (l, idx_ref[0] * nb + i, 0)))
    return pl.pallas_call(body, name=name, grid_spec=grid_spec, out_shape=jax.ShapeDtypeStruct((nl, 2 * a2, b), F32),
                          compiler_params=_cparams(("arbitrary", "arbitrary")))(idx, *recv, *own)


def adamw(name, gs, w, m, v):
    nl, r, b = w.shape
    assert len(gs) == nl
    tr = _row_tile(r, b * 4)

    def body(*refs):
        g_refs, (w_ref, m_ref, v_ref, go_ref, d_ref, nm_ref, nv_ref) = refs[:nl], refs[nl:]
        l = pl.program_id(0)
        for li in range(nl):
            @pl.when(l == li)
            def _(li=li):
                gv = g_refs[li][...]
                go_ref[...] = gv
                mn = ADAM_B1 * m_ref[...] + (1.0 - ADAM_B1) * gv
                vn = ADAM_B2 * v_ref[...] + (1.0 - ADAM_B2) * (gv * gv)
                m_hat = mn / (1.0 - ADAM_B1 ** ADAM_STEP)
                v_hat = vn / (1.0 - ADAM_B2 ** ADAM_STEP)
                d_ref[...] = -ADAM_LR * (m_hat / (jnp.sqrt(v_hat) + ADAM_EPS) + ADAM_WD * w_ref[...])
                nm_ref[...] = mn
                nv_ref[...] = vn

    g_specs = [pl.BlockSpec((None, tr, b), lambda l, i, _li=li: (0, jnp.where(l == _li, i, 0), 0)) for li in range(nl)]
    spec = pl.BlockSpec((None, tr, b), lambda l, i: (l, i, 0))
    sds = jax.ShapeDtypeStruct(w.shape, F32)
    return pl.pallas_call(body, name=name, grid=(nl, r // tr), in_specs=g_specs + [spec] * 3, out_specs=[spec] * 4,
                          out_shape=[sds] * 4, compiler_params=_cparams(("arbitrary", "arbitrary")))(*gs, w, m, v)


def _pack_rows(shapes):
    return [8 * -(-int(np.prod(s)) // (8 * LANES)) for s in shapes]


def _pack(arrs, rows):
    pieces = []
    for a, r in zip(arrs, _pack_rows([a.shape for a in arrs])):
        flat = a.reshape(-1)
        pieces.append(jnp.pad(flat, (0, r * LANES - flat.shape[0])).reshape(r, LANES))
    used = sum(p.shape[0] for p in pieces)
    assert used <= rows
    if used < rows:
        pieces.append(jnp.zeros((rows - used, LANES), F32))
    return jnp.concatenate(pieces, axis=0)


def _unpack(pack, shapes):
    out, off = [], 0
    for shp, r in zip(shapes, _pack_rows(shapes)):
        n = int(np.prod(shp))
        out.append(pack[off:off + r].reshape(-1)[:n].reshape(shp))
        off += r
    return out


BIG_NAMES = ['w_in', 'w_out', 'ffn_w_up', 'ffn_w_down']
MED_NAMES = [n for n in SHARDED_NAMES if n not in BIG_NAMES]
MED_ROWS = 96
SMALL_ROWS = 272


def kernel(x, ln_in_g, ln_in_b, w_in, conv_a_w, conv_a_b, ln_a_g, ln_a_b, qk_norm_q, qk_norm_k, sgu_ln_g, sgu_ln_b, sgu_w, sgu_b, mla_q_norm, mla_w_uq, mla_kv_norm, mla_w_ukv, w_out, ln_mix_g, ln_mix_b, ffn_w_up, ffn_conv_w, ffn_conv_b, ffn_w_down, ln_ffn_g, ln_ffn_b, loss_target, m_ln_in_g, m_ln_in_b, m_w_in, m_conv_a_w, m_conv_a_b, m_ln_a_g, m_ln_a_b, m_qk_norm_q, m_qk_norm_k, m_sgu_ln_g, m_sgu_ln_b, m_sgu_w, m_sgu_b, m_mla_q_norm, m_mla_w_uq, m_mla_kv_norm, m_mla_w_ukv, m_w_out, m_ln_mix_g, m_ln_mix_b, m_ffn_w_up, m_ffn_conv_w, m_ffn_conv_b, m_ffn_w_down, m_ln_ffn_g, m_ln_ffn_b, v_ln_in_g, v_ln_in_b, v_w_in, v_conv_a_w, v_conv_a_b, v_ln_a_g, v_ln_a_b, v_qk_norm_q, v_qk_norm_k, v_sgu_ln_g, v_sgu_ln_b, v_sgu_w, v_sgu_b, v_mla_q_norm, v_mla_w_uq, v_mla_kv_norm, v_mla_w_ukv, v_w_out, v_ln_mix_g, v_ln_mix_b, v_ffn_w_up, v_ffn_conv_w, v_ffn_conv_b, v_ffn_w_down, v_ln_ffn_g, v_ln_ffn_b):
    loc = dict(locals())
    w_loc = {n: loc[n] for n in WEIGHT_NAMES}
    m_loc = {n: loc["m_" + n] for n in WEIGHT_NAMES}
    v_loc = {n: loc["v_" + n] for n in WEIGHT_NAMES}
    me_chip = 2 * lax.axis_index("x") + lax.axis_index("y")
    idx = jnp.stack([lax.axis_index("c"), me_chip]).astype(jnp.int32)

    def wire(n):
        return w_loc[n].astype(BF16) if n in BF16_WIRE else w_loc[n]

    early = [n for n in SHARDED_NAMES if n not in LATE_NAMES]
    parts = [wire(n) for n in early]
    W = {n: w_loc[n] for n in REPL_NAMES}
    for n, p, g in zip(early, parts, run_comm("gather_early", gather_over_chips(parts))):
        W[n] = _assemble(n, p, g, me_chip)

    loss_row, dx, G = local_step(x[0], loss_target[0], W, dict(idx=idx, me=me_chip, late={n: wire(n) for n in LATE_NAMES}))
    loss = lax.psum(loss_row[0, 0], ("x", "y", "c"))
    red = G['reduced']

    med = []
    for q in range(N_CHIPS):
        pieces = [lax.slice_in_dim(jnp.stack(G[n]), q * w_loc[n].shape[2], (q + 1) * w_loc[n].shape[2], axis=2) for n in MED_NAMES]
        med.append(_pack(pieces, MED_ROWS))
    rest = [(n, l) for n in BIG_NAMES for l in range(DEPTH) if (n, l) not in red]
    xs = [G[n][l] for n, l in rest] + [jnp.stack(med)]
    r1 = run_comm("pair_exchange", pair_exchange(xs))
    ps = [pair_sum(f"pair_sum{t}", xv, rv, idx, BF16) for t, (xv, rv) in enumerate(zip(xs, r1))]
    small_pack = _pack([jnp.stack(G[n]) if isinstance(G[n], list) else G[n] for n in REPL_NAMES], SMALL_ROWS)
    comms = [chip_exchange(ps), gather_over_devices(small_pack)]
    cs, (small_all,) = split_comm_results(comms, run_comm("chip_exchange", join_comms(comms)))
    halves = [chip_sum(f"chip_sum{t}", [cv], [pv], idx) for t, (cv, pv) in enumerate(zip(cs, ps))]
    g_small = device_sum("small_sum", small_all, small_pack, idx)
    done = run_comm("half_exchange", half_exchange(halves))
    red.update(zip(rest, done[:-1]))
    g_med = done[-1]

    outs = {k: {} for k in ("grad", "delta", "new_m", "new_v")}
    for n in BIG_NAMES:
        res = adamw(f"adamw_{n}", [red[(n, l)] for l in range(DEPTH)], w_loc[n], m_loc[n], v_loc[n])
        for k, v in zip(("grad", "delta", "new_m", "new_v"), res):
            outs[k][n] = v
    for names, g, rows in ((MED_NAMES, g_med, MED_ROWS), (REPL_NAMES, g_small[None], SMALL_ROWS)):
        res = adamw(f"adamw_{names[0]}_pack", [g], *[_pack([d[n] for n in names], rows)[None] for d in (w_loc, m_loc, v_loc)])
        for k, v in zip(("grad", "delta", "new_m", "new_v"), res):
            outs[k].update(zip(names, _unpack(v[0], [w_loc[n].shape for n in names])))
    return (loss, dx[None], *[outs["grad"][n] for n in WEIGHT_NAMES], *[outs["delta"][n] for n in WEIGHT_NAMES],
            *[outs["new_m"][n] for n in WEIGHT_NAMES], *[outs["new_v"][n] for n in WEIGHT_NAMES])
```

```python
import functools
import math

import jax
import jax.numpy as jnp
import numpy as np
from jax import lax
from jax.experimental import pallas as pl
from jax.experimental.pallas import tpu as pltpu

F32 = jnp.float32
BF16 = jnp.bfloat16

D_MODEL = 1024
DEPTH = 2
GRID_W = 64
GROUP_W = 256
HEAD_DIM = 64
CONV_A_WIDTH = 31
CHUNK = 128
MLA_Q_LORA = 192
MLA_KV_LORA = 128
MLA_NOPE = 64
MLA_ROPE = 32
D_FF = 2816
ROPE_THETA = 10000.0
DEEPNORM_ALPHA = (2 * DEPTH) ** 0.25
LN_EPS = 1e-5
RMS_EPS = 1e-6
D_IN_PROJ = 1888
PROJ_W = 2048

ADAM_LR = 0.001
ADAM_B1 = 0.9
ADAM_B2 = 0.999
ADAM_EPS = 1e-08
ADAM_WD = 0.01
ADAM_STEP = 10

VMEM_LIMIT = 56 * 1024 * 1024
MESH = pl.DeviceIdType.MESH
N_CHIPS = 4
N_DEV = 8

WEIGHT_NAMES = ['ln_in_g', 'ln_in_b', 'w_in', 'conv_a_w', 'conv_a_b', 'ln_a_g', 'ln_a_b', 'qk_norm_q', 'qk_norm_k',
                'sgu_ln_g', 'sgu_ln_b', 'sgu_w', 'sgu_b', 'mla_q_norm', 'mla_w_uq', 'mla_kv_norm', 'mla_w_ukv', 'w_out',
                'ln_mix_g', 'ln_mix_b', 'ffn_w_up', 'ffn_conv_w', 'ffn_conv_b', 'ffn_w_down', 'ln_ffn_g', 'ln_ffn_b']
SHARDED_AXIS = {'w_in': 2, 'conv_a_w': 2, 'mla_w_uq': 2, 'mla_w_ukv': 2, 'w_out': 1, 'ffn_w_up': 2, 'ffn_conv_w': 2,
                'ffn_w_down': 1}
SHARDED_NAMES = [n for n in WEIGHT_NAMES if n in SHARDED_AXIS]
REPL_NAMES = [n for n in WEIGHT_NAMES if n not in SHARDED_AXIS]
BF16_WIRE = ('w_in', 'mla_w_uq', 'mla_w_ukv', 'w_out', 'ffn_w_up', 'ffn_w_down')


def _cparams(sem):
    return pltpu.CompilerParams(dimension_semantics=sem, vmem_limit_bytes=VMEM_LIMIT)


def _pick(n, cands):
    for c in cands:
        if n % c == 0:
            return c
    return n


def carried_call(name, body, grid, in_specs, out_specs, out_shape, scratch, semantics, args, comm=None):
    if comm is None:
        res = pl.pallas_call(body, name=name, grid=grid, in_specs=in_specs, out_specs=out_specs, out_shape=out_shape,
                             scratch_shapes=scratch, compiler_params=_cparams(semantics))(*args)
        return list(res), None
    n_in, n_out, n_scr = len(in_specs), len(out_specs), len(scratch)
    nci, nco = len(comm.operands), len(comm.out_shape)

    def carried(*refs):
        o0 = n_in + nci
        s0 = o0 + n_out + nco
        ins, cins = refs[:n_in], refs[n_in:o0]
        outs, couts = refs[o0:o0 + n_out], refs[o0 + n_out:s0]
        scr, sems = refs[s0:s0 + n_scr], refs[s0 + n_scr:]
        ids = [pl.program_id(d) for d in range(len(grid))]
        first = functools.reduce(lambda u, v: u & v, [i == 0 for i in ids])
        last = functools.reduce(lambda u, v: u & v, [i == g - 1 for i, g in zip(ids, grid)])

        @pl.when(first)
        def _():
            comm.start(cins, couts, sems)

        body(*ins, *outs, *scr)

        @pl.when(last)
        def _():
            comm.finish(cins, couts, sems)

    res = pl.pallas_call(
        carried, name=name, grid=grid, in_specs=list(in_specs) + [ANY] * nci, out_specs=list(out_specs) + [ANY] * nco,
        out_shape=list(out_shape) + comm.out_shape, scratch_shapes=list(scratch) + [pltpu.SemaphoreType.DMA((k,)) for k in comm.sems],
        input_output_aliases={n_in + i: n_out + o for i, o in comm.aliases.items()},
        compiler_params=_cparams(("arbitrary",) * len(grid)))(*args, *comm.operands)
    return list(res[:n_out]), list(res[n_out:])


def mm(name, a, b, *, ta=False, tb=False, add=None, add_scale=1.0, out_dtype=F32, chip_major=False, comm=None):
    m, k = (a.shape[1], a.shape[0]) if ta else a.shape
    n = b.shape[0] if tb else b.shape[1]
    assert (b.shape[1] if tb else b.shape[0]) == k
    tm = _pick(m, (1024, 1408, 512, 256, 128))
    tn = n // N_CHIPS if chip_major else _pick(n, (1024, 1408, 512, 256, 128))
    tk = _pick(k, (1024, 1408, 512, 256, 128))
    nk = k // tk
    a_spec = pl.BlockSpec((tk, tm), lambda i, j, kk: (kk, i)) if ta else pl.BlockSpec((tm, tk), lambda i, j, kk: (i, kk))
    b_spec = pl.BlockSpec((tn, tk), lambda i, j, kk: (j, kk)) if tb else pl.BlockSpec((tk, tn), lambda i, j, kk: (kk, j))
    in_specs = [a_spec, b_spec]
    args = [a, b]
    if add is not None:
        in_specs.append(pl.BlockSpec((tm, tn), lambda i, j, kk: (i, j)))
        args.append(add)
    dims = (((0 if ta else 1,), (1 if tb else 0,)), ((), ()))

    def body(*refs):
        a_ref, b_ref = refs[0], refs[1]
        add_ref = refs[2] if add is not None else None
        o_ref, acc_ref = refs[-2], refs[-1]
        kk = pl.program_id(2)
        part = lax.dot_general(a_ref[...].astype(BF16), b_ref[...].astype(BF16), dims, preferred_element_type=F32)

        @pl.when(kk == 0)
        def _():
            acc_ref[...] = part

        @pl.when(kk > 0)
        def _():
            acc_ref[...] += part

        @pl.when(kk == nk - 1)
        def _():
            r = acc_ref[...]
            if add_ref is not None:
                r = r + add_scale * add_ref[...].astype(F32)
            o_ref[...] = r.astype(out_dtype)

    if chip_major:
        out_spec = pl.BlockSpec((None, tm, tn), lambda i, j, kk: (j, i, 0))
        out_shape = jax.ShapeDtypeStruct((N_CHIPS, m, tn), out_dtype)
    else:
        out_spec = pl.BlockSpec((tm, tn), lambda i, j, kk: (i, j))
        out_shape = jax.ShapeDtypeStruct((m, n), out_dtype)
    res, cres = carried_call(name, body, (m // tm, n // tn, nk), in_specs, [out_spec], [out_shape], [pltpu.VMEM((tm, tn), F32)],
                             ("parallel", "parallel", "arbitrary"), args, comm)
    return res[0] if comm is None else (res[0], cres)


def _cb(cb):
    return cb if callable(cb) else (lambda j, _c=cb: _c)


def tiled_call(name, fn, s, tm, ins, outs, ncol=1):
    nrow = s // tm
    in_specs, args, kinds = [], [], []
    for it in ins:
        kind = it[0]
        if kind == 'row':
            _, arr, w, cb = it
            in_specs.append(pl.BlockSpec((tm, w), lambda j, i, _c=_cb(cb): (i, _c(j))))
            args.append(arr)
            kinds.append(('row',))
        elif kind == 'halo':
            _, arr, w, cb, h = it
            r = tm // h
            nh = s // h
            in_specs.append(pl.BlockSpec((h, w), lambda j, i, _c=_cb(cb), _r=r: (jnp.maximum(i * _r - 1, 0), _c(j))))
            in_specs.append(pl.BlockSpec((tm, w), lambda j, i, _c=_cb(cb): (i, _c(j))))
            in_specs.append(pl.BlockSpec((h, w), lambda j, i, _c=_cb(cb), _r=r, _n=nh: (jnp.minimum((i + 1) * _r, _n - 1), _c(j))))
            args += [arr, arr, arr]
            kinds.append(('halo',))
        elif kind == 'full':
            arr = it[1]
            in_specs.append(pl.BlockSpec(arr.shape, lambda j, i, _n=arr.ndim: (0,) * _n))
            args.append(arr)
            kinds.append(('ref',))
        elif kind == 'col':
            _, arr, w, cb = it
            in_specs.append(pl.BlockSpec((arr.shape[0], w), lambda j, i, _c=_cb(cb): (0, _c(j))))
            args.append(arr)
            kinds.append(('ref',))
        elif kind == 'hm':
            arr = it[1]
            in_specs.append(pl.BlockSpec((arr.shape[0], tm, arr.shape[2]), lambda j, i: (0, i, 0)))
            args.append(arr)
            kinds.append(('ref',))
        else:
            raise ValueError(kind)
    out_specs, out_shapes, okinds = [], [], []
    for ot in outs:
        kind = ot[0]
        if kind == 'row':
            _, wt, w, cb, dt = ot
            out_specs.append(pl.BlockSpec((tm, w), lambda j, i, _c=_cb(cb): (i, _c(j))))
            out_shapes.append(jax.ShapeDtypeStruct((s, wt), dt))
            okinds.append('row')
        elif kind == 'hm':
            _, hh, d, dt = ot
            out_specs.append(pl.BlockSpec((hh, tm, d), lambda j, i: (0, i, 0)))
            out_shapes.append(jax.ShapeDtypeStruct((hh, s, d), dt))
            okinds.append('hm')
        elif kind == 'acc':
            shape = ot[1]
            out_specs.append(pl.BlockSpec(shape, lambda j, i, _n=len(shape): (0,) * _n))
            out_shapes.append(jax.ShapeDtypeStruct(shape, F32))
            okinds.append('acc')
        elif kind == 'colacc':
            _, r, wt, w, cb = ot
            out_specs.append(pl.BlockSpec((r, w), lambda j, i, _c=_cb(cb): (0, _c(j))))
            out_shapes.append(jax.ShapeDtypeStruct((r, wt), F32))
            okinds.append('colacc')
        else:
            raise ValueError(kind)
    n_in = len(in_specs)

    def body(*refs):
        j = pl.program_id(0)
        i = pl.program_id(1)
        in_refs, out_refs = refs[:n_in], refs[n_in:]
        items, p = [], 0
        for kd in kinds:
            if kd[0] == 'row':
                items.append(in_refs[p][...])
                p += 1
            elif kd[0] == 'halo':
                prev, cen, nxt = in_refs[p][...], in_refs[p + 1][...], in_refs[p + 2][...]
                prev = jnp.where(i == 0, jnp.zeros_like(prev), prev)
                nxt = jnp.where(i == nrow - 1, jnp.zeros_like(nxt), nxt)
                items.append(jnp.concatenate([prev, cen, nxt], axis=0))
                p += 3
            else:
                items.append(in_refs[p])
                p += 1
        for o_ref, kd in zip(out_refs, okinds):
            if kd == 'acc':
                @pl.when((i == 0) & (j == 0))
                def _(o_ref=o_ref):
                    o_ref[...] = jnp.zeros_like(o_ref)
            elif kd == 'colacc':
                @pl.when(i == 0)
                def _(o_ref=o_ref):
                    o_ref[...] = jnp.zeros_like(o_ref)
        fn(i, j, *items, *out_refs)

    res = pl.pallas_call(
        body, name=name, grid=(ncol, nrow), in_specs=in_specs, out_specs=out_specs, out_shape=out_shapes,
        compiler_params=_cparams(("arbitrary", "arbitrary")),
    )(*args)
    return res


def _sigmoid(x):
    return 1.0 / (1.0 + jnp.exp(-x))


def _ln_stats(x):
    mu = jnp.mean(x, axis=1, keepdims=True)
    xc = x - mu
    var = jnp.mean(xc * xc, axis=1, keepdims=True)
    r = lax.rsqrt(var + LN_EPS)
    return xc * r, r


def _ln_bwd(dy, xh, r, g):
    dxh = dy * g
    dx = r * (dxh - jnp.mean(dxh, axis=1, keepdims=True) - xh * jnp.mean(dxh * xh, axis=1, keepdims=True))
    return dx, jnp.sum(dy * xh, axis=0, keepdims=True), jnp.sum(dy, axis=0, keepdims=True)


def _gmean(v, gm):
    hi = v.astype(BF16)
    lo = (v - hi.astype(F32)).astype(BF16)
    return jnp.dot(hi, gm, preferred_element_type=F32) + jnp.dot(lo, gm, preferred_element_type=F32)


def _roll_l(x, sh):
    return pltpu.roll(x, sh % x.shape[1], 1)


def _rope(x, c, sa, sb, sh):
    return x * c + _roll_l(x, -sh) * sa + _roll_l(x, sh) * sb


def _rope_t(dy, c, sa, sb, sh):
    return dy * c + _roll_l(dy * sa, sh) + _roll_l(dy * sb, -sh)


def _roll_r(x, s):
    return pltpu.roll(x, (-s) % x.shape[0], 0)


_GELU_C = math.sqrt(2.0 / math.pi)


def _gelu(x):
    t = jnp.tanh(_GELU_C * (x + 0.044715 * x * x * x))
    return 0.5 * x * (1.0 + t), t


def _gelu_grad(x, t):
    return 0.5 * (1.0 + t) + 0.5 * x * (1.0 - t * t) * _GELU_C * (1.0 + 3.0 * 0.044715 * x * x)


def _dot_nt(a, b):
    return lax.dot_general(a, b, (((1,), (1,)), ((), ())), preferred_element_type=F32)


def _dot_tn(a, b):
    return lax.dot_general(a, b, (((0,), (0,)), ((), ())), preferred_element_type=F32)


def _rope_block(s, d):
    t = jnp.arange(s)
    row = (t // GRID_W).astype(F32)
    col = (t % GRID_W).astype(F32)
    half = d // 4
    inv = ROPE_THETA ** (-jnp.arange(half, dtype=F32) / half)
    z = jnp.zeros((s, half), F32)
    cs, sas, sbs = [], [], []
    for pos in (row, col):
        ang = pos[:, None] * inv[None, :]
        co, si = jnp.cos(ang), jnp.sin(ang)
        cs += [co, co]
        sas += [-si, z]
        sbs += [z, si]
    return tuple(jnp.concatenate(v, axis=1) for v in (cs, sas, sbs))


def _rope_tables(s):
    gqa = tuple(jnp.tile(a, (1, 4)) for a in _rope_block(s, HEAD_DIM))
    c32, sa32, sb32 = _rope_block(s, MLA_ROPE)

    def head128(c, fill):
        return jnp.concatenate([jnp.full((s, 64), fill, F32), c, jnp.zeros((s, 32), F32)], axis=1)

    mla = tuple(jnp.tile(a, (1, 4)) for a in (head128(c32, 1.0), head128(sa32, 0.0), head128(sb32, 0.0)))
    kr = (head128(c32, 0.0), head128(sa32, 0.0), head128(sb32, 0.0))
    return dict(gqa=gqa, mla=mla, kr=kr)


def _group_mean_matrix(w, g):
    idx = np.arange(w) // g
    return jnp.asarray((idx[:, None] == idx[None, :]).astype(np.float32) / g, dtype=BF16)


TM = 512


def ln_fwd(name, s, x, g, b, res=None, alpha=1.0):
    ins = [('row', x, D_MODEL, 0)] + ([('row', res, D_MODEL, 0)] if res is not None else []) + [('full', g), ('full', b)]

    def fn(i, j, *a):
        if res is not None:
            xv, rv, g_ref, b_ref, o_ref = a
            z = alpha * rv + xv
        else:
            xv, g_ref, b_ref, o_ref = a
            z = xv
        xh, _ = _ln_stats(z)
        o_ref[...] = xh * g_ref[...] + b_ref[...]

    return tiled_call(name, fn, s, TM, ins, [('row', D_MODEL, D_MODEL, 0, F32)])[0]


def ln_bwd(name, s, dy, x, g, res=None, alpha=1.0):
    ins = [('row', dy, D_MODEL, 0), ('row', x, D_MODEL, 0)]
    ins += ([('row', res, D_MODEL, 0)] if res is not None else []) + [('full', g)]

    def fn(i, j, *a):
        a = list(a)
        dyv = a.pop(0)
        xv = a.pop(0)
        z = alpha * a.pop(0) + xv if res is not None else xv
        g_ref, dz_ref, dg_ref, db_ref = a
        xh, r = _ln_stats(z)
        dz, dg, db = _ln_bwd(dyv, xh, r, g_ref[...])
        dz_ref[...] = dz
        dg_ref[...] += dg
        db_ref[...] += db

    return tiled_call(name, fn, s, TM, ins, [('row', D_MODEL, D_MODEL, 0, F32), ('acc', (1, D_MODEL)), ('acc', (1, D_MODEL))])


def loss_and_ln_bwd(name, s, x, res, g, b, tgt, alpha):
    ins = [('row', x, D_MODEL, 0), ('row', res, D_MODEL, 0), ('row', tgt, D_MODEL, 0), ('full', g), ('full', b)]

    def fn(i, j, xv, rv, tv, g_ref, b_ref, dz_ref, dg_ref, db_ref, loss_ref):
        xh, r = _ln_stats(alpha * rv + xv)
        y = xh * g_ref[...] + b_ref[...]
        e = y - tv
        dz, dg, db = _ln_bwd(e * (1.0 / D_MODEL), xh, r, g_ref[...])
        dz_ref[...] = dz
        dg_ref[...] += dg
        db_ref[...] += db
        loss_ref[...] += jnp.sum(jnp.sum(e * e, axis=0, keepdims=True), axis=1, keepdims=True) * (0.5 / D_MODEL)

    return tiled_call(name, fn, s, TM, ins, [('row', D_MODEL, D_MODEL, 0, F32), ('acc', (1, D_MODEL)), ('acc', (1, D_MODEL)),
                                             ('acc', (1, 128))])


HALO_A = 16


def _glu(a_in):
    return a_in[:, :GROUP_W] * _sigmoid(a_in[:, GROUP_W:])


def a_fwd(name, s, proj, cw, cb, g, b):
    def fn(i, j, a_ext, cw_ref, cb_ref, g_ref, b_ref, y_ref, o_ref):
        a = _glu(a_ext)
        acc = jnp.zeros((TM, GROUP_W), F32)
        for k in range(CONV_A_WIDTH):
            acc = acc + _roll_r(a, k - 15)[HALO_A:HALO_A + TM] * cw_ref[pl.ds(k, 1), :]
        y = acc + cb_ref[...]
        y_ref[...] = y
        xh, _ = _ln_stats(y)
        z = xh * g_ref[...] + b_ref[...]
        o_ref[...] = z * _sigmoid(z)

    ins = [('halo', proj, 2 * GROUP_W, 0, HALO_A), ('full', cw), ('full', cb), ('full', g), ('full', b)]
    return tiled_call(name, fn, s, TM, ins, [('row', GROUP_W, GROUP_W, 0, F32), ('row', GROUP_W, GROUP_W, 0, F32)])


def a_bwd1(name, s, y, dmix, g, b):
    def fn(i, j, yv, do, g_ref, b_ref, dy_ref, dg_ref, db_ref, dcb_ref):
        xh, r = _ln_stats(yv)
        z = xh * g_ref[...] + b_ref[...]
        sg = _sigmoid(z)
        dz = do * sg * (1.0 + z * (1.0 - sg))
        dy, dg, db = _ln_bwd(dz, xh, r, g_ref[...])
        dy_ref[...] = dy
        dg_ref[...] += dg
        db_ref[...] += db
        dcb_ref[...] += jnp.sum(dy, axis=0, keepdims=True)

    ins = [('row', y, GROUP_W, 0), ('row', dmix, GROUP_W, 0), ('full', g), ('full', b)]
    return tiled_call(name, fn, s, TM, ins, [('row', GROUP_W, GROUP_W, 0, F32), ('acc', (1, GROUP_W)), ('acc', (1, GROUP_W)),
                                             ('acc', (1, GROUP_W))])


def a_bwd2(name, s, proj, dy, cw):
    def fn(i, j, a_ext, dy_ext, cw_ref, da_ref, dcw_ref):
        a1, a2 = a_ext[:, :GROUP_W], a_ext[:, GROUP_W:]
        sg = _sigmoid(a2)
        a = a1 * sg
        dyc = dy_ext[HALO_A:HALO_A + TM]
        da = jnp.zeros((TM, GROUP_W), F32)
        for k in range(CONV_A_WIDTH):
            da = da + _roll_r(dy_ext, 15 - k)[HALO_A:HALO_A + TM] * cw_ref[pl.ds(k, 1), :]
            dcw_ref[pl.ds(k, 1), :] += jnp.sum(dyc * _roll_r(a, k - 15)[HALO_A:HALO_A + TM], axis=0, keepdims=True)
        a1c, sgc = a1[HALO_A:HALO_A + TM], sg[HALO_A:HALO_A + TM]
        da_ref[...] = jnp.concatenate([da * sgc, da * a1c * sgc * (1.0 - sgc)], axis=1).astype(BF16)

    ins = [('halo', proj, 2 * GROUP_W, 0, HALO_A), ('halo', dy, GROUP_W, 0, HALO_A), ('full', cw)]
    return tiled_call(name, fn, s, TM, ins, [('row', 2 * GROUP_W, 2 * GROUP_W, 0, BF16), ('acc', (32, GROUP_W))])


def b_prep_fwd(name, s, proj, gq, gk, tabs, gm):
    c, sa, sb = tabs

    def fn(i, j, qkv, cv, sav, sbv, gq_ref, gk_ref, gm_ref, q_ref, k_ref, v_ref):
        q, k, v = qkv[:, :256], qkv[:, 256:384], qkv[:, 384:]
        gmv = gm_ref[...]
        qn = q * lax.rsqrt(_gmean(q * q, gmv) + RMS_EPS) * gq_ref[...]
        kn = k * lax.rsqrt(_gmean(k * k, gmv[:128, :128]) + RMS_EPS) * gk_ref[...]
        qr = _rope(qn, cv, sav, sbv, 16)
        kr = _rope(kn, cv[:, :128], sav[:, :128], sbv[:, :128], 16)
        for h in range(4):
            q_ref[h] = (qr[:, h * 64:(h + 1) * 64] * QSCALE_GQA).astype(BF16)
        for h in range(2):
            k_ref[h] = kr[:, h * 64:(h + 1) * 64].astype(BF16)
            v_ref[h] = v[:, h * 64:(h + 1) * 64].astype(BF16)

    ins = [('row', proj, 512, 1), ('row', c, 256, 0), ('row', sa, 256, 0), ('row', sb, 256, 0), ('full', gq), ('full', gk),
           ('full', gm)]
    return tiled_call(name, fn, s, TM, ins, [('hm', 4, 64, BF16), ('hm', 2, 64, BF16), ('hm', 2, 64, BF16)])


def b_prep_bwd(name, s, proj, gq, gk, tabs, gm, dq_hm, dk_hm, dv_hm):
    c, sa, sb = tabs

    def fn(i, j, qkv, cv, sav, sbv, gq_ref, gk_ref, gm_ref, dq_ref, dk_ref, dv_ref, dp_ref, dgq_ref, dgk_ref):
        q, k = qkv[:, :256], qkv[:, 256:384]
        gmv = gm_ref[...]
        dqr = jnp.concatenate([dq_ref[h] for h in range(4)], axis=1) * QSCALE_GQA
        dkr = jnp.concatenate([dk_ref[h] for h in range(2)], axis=1)
        dv = jnp.concatenate([dv_ref[h] for h in range(2)], axis=1)
        outs = []
        for x, dxr, g_ref, gmx, w, dg_ref in ((q, dqr, gq_ref, gmv, 256, dgq_ref), (k, dkr, gk_ref, gmv[:128, :128], 128, dgk_ref)):
            dn = _rope_t(dxr, cv[:, :w], sav[:, :w], sbv[:, :w], 16)
            r = lax.rsqrt(_gmean(x * x, gmx) + RMS_EPS)
            gv = g_ref[...]
            dx = gv * r * dn - x * (r * r * r) * _gmean(x * gv * dn, gmx)
            dgt = jnp.sum(x * r * dn, axis=0, keepdims=True)
            dg = dgt[:, 0:64]
            for h in range(1, w // 64):
                dg = dg + dgt[:, h * 64:(h + 1) * 64]
            dg_ref[...] += dg
            outs.append(dx)
        dp_ref[...] = jnp.concatenate(outs + [dv], axis=1).astype(BF16)

    ins = [('row', proj, 512, 1), ('row', c, 256, 0), ('row', sa, 256, 0), ('row', sb, 256, 0), ('full', gq), ('full', gk),
           ('full', gm), ('hm', dq_hm), ('hm', dk_hm), ('hm', dv_hm)]
    return tiled_call(name, fn, s, TM, ins, [('row', 512, 512, 0, BF16), ('acc', (1, 64)), ('acc', (1, 64))])


TQ = 256


LOG2E = 1.4426950408889634
QSCALE_GQA = HEAD_DIM ** -0.5 * LOG2E
QSCALE_MLA = (MLA_NOPE + MLA_ROPE) ** -0.5 * LOG2E


def attn_fwd(name, s, q_hm, k_hm, v_hm, comm=None):
    dk = q_hm.shape[2]
    kpb = k_hm.shape[0] // 2

    def body(q_ref, k_ref, v_ref, o_ref, l_ref):
        outs = []
        for g in range(2):
            kv = g if kpb == 2 else 0
            sc = _dot_nt(q_ref[g], k_ref[kv])
            m = jnp.max(sc, axis=1, keepdims=True)
            p = jnp.exp2(sc - m)
            l = jnp.sum(p, axis=1, keepdims=True)
            o = jnp.dot(p.astype(BF16), v_ref[kv], preferred_element_type=F32)
            outs.append(o / l)
            l_ref[g] = m + jnp.log2(l)
        o_ref[...] = jnp.concatenate(outs, axis=1)

    res, cres = carried_call(
        name, body, (2, s // TQ),
        [pl.BlockSpec((2, TQ, dk), lambda p, i: (p, i, 0)),
         pl.BlockSpec((kpb, s, dk), lambda p, i: (p, 0, 0)),
         pl.BlockSpec((kpb, s, 64), lambda p, i: (p, 0, 0))],
        [pl.BlockSpec((TQ, 128), lambda p, i: (i, p)), pl.BlockSpec((2, TQ, 1), lambda p, i: (p, i, 0))],
        [jax.ShapeDtypeStruct((s, 256), F32), jax.ShapeDtypeStruct((4, s, 1), F32)], [],
        ("arbitrary", "arbitrary"), [q_hm, k_hm, v_hm], comm)
    return res if comm is None else (res, cres)


def attn_bwd(name, s, q_hm, k_hm, v_hm, o, lse, dmixcat, dcol, comm=None):
    dk = q_hm.shape[2]
    nkv = k_hm.shape[0]
    shared = nkv == 2
    kvi = (lambda p, g, i: (p, 0, 0)) if shared else (lambda p, g, i: (2 * p + g, 0, 0))

    def body(q_ref, k_ref, v_ref, o_ref, l_ref, do_ref, dq_ref, dk_ref, dv_ref):
        g = pl.program_id(1)
        i = pl.program_id(2)
        first = (i == 0) & (g == 0) if shared else (i == 0)

        @pl.when(first)
        def _():
            dk_ref[...] = jnp.zeros_like(dk_ref)
            dv_ref[...] = jnp.zeros_like(dv_ref)

        dob, ob = do_ref[...], o_ref[...]
        do = jnp.where(g == 0, dob[:, :64], dob[:, 64:])
        ov = jnp.where(g == 0, ob[:, :64], ob[:, 64:])
        q, k, v = q_ref[0], k_ref[0], v_ref[0]
        p = jnp.exp2(_dot_nt(q, k) - l_ref[0])
        dp = _dot_nt(do.astype(BF16), v)
        delta = jnp.sum(do * ov, axis=1, keepdims=True)
        ds = (p * (dp - delta) * (1.0 / LOG2E)).astype(BF16)
        dq_ref[0] = jnp.dot(ds, k, preferred_element_type=F32)
        dk_ref[0] += _dot_tn(ds, q)
        dv_ref[0] += _dot_tn(p.astype(BF16), do.astype(BF16))

    res, cres = carried_call(
        name, body, (2, 2, s // TQ),
        [pl.BlockSpec((1, TQ, dk), lambda p, g, i: (2 * p + g, i, 0)),
         pl.BlockSpec((1, s, dk), kvi), pl.BlockSpec((1, s, 64), kvi),
         pl.BlockSpec((TQ, 128), lambda p, g, i: (i, p)),
         pl.BlockSpec((1, TQ, 1), lambda p, g, i: (2 * p + g, i, 0)),
         pl.BlockSpec((TQ, 128), lambda p, g, i: (i, dcol + p))],
        [pl.BlockSpec((1, TQ, dk), lambda p, g, i: (2 * p + g, i, 0)),
         pl.BlockSpec((1, s, dk), kvi), pl.BlockSpec((1, s, 64), kvi)],
        [jax.ShapeDtypeStruct((4, s, dk), F32), jax.ShapeDtypeStruct((nkv, s, dk), F32), jax.ShapeDtypeStruct((nkv, s, 64), F32)],
        [], ("arbitrary", "arbitrary", "arbitrary"), [q_hm, k_hm, v_hm, o, lse, dmixcat], comm)
    return res if comm is None else (res, cres)


def _lane_group_masks(w, g):
    lane = lax.broadcasted_iota(jnp.int32, (1, w), 1)
    return [((lane >= k * g) & (lane < (k + 1) * g)).astype(F32) for k in range(w // g)]


def _sgu_gate(svn, w_ref, bias):
    masks = _lane_group_masks(GROUP_W, 64)
    outs = []
    for n in range(TM // CHUNK):
        x = svn[n * CHUNK:(n + 1) * CHUNK]
        acc = bias
        for g in range(4):
            acc = acc + jnp.dot(w_ref[g].astype(BF16), (x * masks[g]).astype(BF16), preferred_element_type=F32)
        outs.append(acc)
    return jnp.concatenate(outs, axis=0)


def c_fwd(name, s, proj, g, b, w, bias):
    def fn(i, j, cin, g_ref, b_ref, w_ref, bias_ref, o_ref):
        c, _ = _gelu(cin)
        xh, _ = _ln_stats(c[:, GROUP_W:])
        svn = xh * g_ref[...] + b_ref[...]
        o_ref[...] = c[:, :GROUP_W] * _sgu_gate(svn, w_ref, bias_ref[...])

    ins = [('row', proj, 512, 2), ('full', g), ('full', b), ('full', w), ('full', bias)]
    return tiled_call(name, fn, s, TM, ins, [('row', GROUP_W, GROUP_W, 0, F32)])[0]


def c_bwd(name, s, proj, g, b, w, wt, bias, dmixcat):
    def fn(i, j, cin, do, g_ref, b_ref, w_ref, wt_ref, bias_ref, dc_ref, dg_ref, db_ref, dw_ref, dbias_ref):
        c, t = _gelu(cin)
        u = c[:, :GROUP_W]
        xh, r = _ln_stats(c[:, GROUP_W:])
        svn = xh * g_ref[...] + b_ref[...]
        gate = _sgu_gate(svn, w_ref, bias_ref[...])
        du = do * gate
        dgate = do * u
        masks = _lane_group_masks(GROUP_W, 64)
        col = lax.broadcasted_iota(jnp.int32, (CHUNK, 128), 1)
        dsvn, dbias = [], jnp.zeros((CHUNK, 128), F32)
        for n in range(TM // CHUNK):
            dgc = dgate[n * CHUNK:(n + 1) * CHUNK]
            x = svn[n * CHUNK:(n + 1) * CHUNK]
            acc = jnp.zeros((CHUNK, GROUP_W), F32)
            for gi in range(4):
                dgm = (dgc * masks[gi]).astype(BF16)
                acc = acc + jnp.dot(wt_ref[gi].astype(BF16), dgm, preferred_element_type=F32)
                dw_ref[gi] += _dot_nt(dgm, (x * masks[gi]).astype(BF16))
                dbias = dbias + jnp.where(col == gi, jnp.sum(dgc * masks[gi], axis=1, keepdims=True), 0.0)
            dsvn.append(acc)
        dbias_ref[...] += dbias
        dsv, dg, db = _ln_bwd(jnp.concatenate(dsvn, axis=0), xh, r, g_ref[...])
        dg_ref[...] += dg
        db_ref[...] += db
        dc_ref[...] = (jnp.concatenate([du, dsv], axis=1) * _gelu_grad(cin, t)).astype(BF16)

    ins = [('row', proj, 512, 2), ('row', dmixcat, GROUP_W, 2), ('full', g), ('full', b), ('full', w), ('full', wt), ('full', bias)]
    return tiled_call(name, fn, s, TM, ins, [('row', 512, 512, 0, BF16), ('acc', (1, GROUP_W)), ('acc', (1, GROUP_W)),
                                             ('acc', (4, CHUNK, CHUNK)), ('acc', (CHUNK, 128))])


def _d_common(dpart, gq_ref, gkv_ref):
    cq, ckv, kr = dpart[:, :256], dpart[:, 256:384], dpart[:, 384:]
    rq = lax.rsqrt(jnp.sum(cq * cq, axis=1, keepdims=True) * (1.0 / MLA_Q_LORA) + RMS_EPS)
    rkv = lax.rsqrt(jnp.mean(ckv * ckv, axis=1, keepdims=True) + RMS_EPS)
    return cq, ckv, kr, rq, rkv, cq * rq * gq_ref[...], ckv * rkv * gkv_ref[...]


def d_prep_fwd(name, s, proj, gq, gkv, wuq, wukv, tm_, tkr):
    cm, sam, sbm = tm_
    ck, sak, sbk = tkr

    def fn(i, j, dpart, cmv, samv, sbmv, ckv_, sakv, sbkv, gq_ref, gkv_ref, wuq_ref, wukv_ref, q_ref, k_ref, v_ref):
        cq, ckv, kr, rq, rkv, cqn, ckvn = _d_common(dpart, gq_ref, gkv_ref)
        qd = jnp.dot(cqn.astype(BF16), wuq_ref[...], preferred_element_type=F32)
        qf = _rope(qd, cmv, samv, sbmv, 8)
        kvd = jnp.dot(ckvn.astype(BF16), wukv_ref[...], preferred_element_type=F32)
        krr = _rope(kr, ckv_, sakv, sbkv, 8)
        for h in range(4):
            q_ref[h] = (qf[:, h * 128:(h + 1) * 128] * QSCALE_MLA).astype(BF16)
            k_ref[h] = (kvd[:, h * 128:(h + 1) * 128] + krr).astype(BF16)
            v_ref[h] = kvd[:, 512 + h * 64:512 + (h + 1) * 64].astype(BF16)

    ins = [('row', proj, 512, 3), ('row', cm, 512, 0), ('row', sam, 512, 0), ('row', sbm, 512, 0), ('row', ck, 128, 0),
           ('row', sak, 128, 0), ('row', sbk, 128, 0), ('full', gq), ('full', gkv), ('full', wuq), ('full', wukv)]
    return tiled_call(name, fn, s, TM, ins, [('hm', 4, 128, BF16), ('hm', 4, 128, BF16), ('hm', 4, 64, BF16)])


def d_prep_bwd(name, s, proj, gq, gkv, wuq, wukv, tm_, tkr, dq_hm, dk_hm, dv_hm):
    cm, sam, sbm = tm_
    ck, sak, sbk = tkr

    def fn(i, j, dpart, cmv, samv, sbmv, ckv_, sakv, sbkv, gq_ref, gkv_ref, wuq_ref, wukv_ref, dq_ref, dk_ref, dv_ref,
           dd_ref, dgq_ref, dgkv_ref, dwuq_ref, dwukv_ref):
        cq, ckv, kr, rq, rkv, cqn, ckvn = _d_common(dpart, gq_ref, gkv_ref)
        dqf = jnp.concatenate([dq_ref[h] for h in range(4)], axis=1) * QSCALE_MLA
        dkf = [dk_ref[h] for h in range(4)]
        dqd = _rope_t(dqf, cmv, samv, sbmv, 8).astype(BF16)
        dkvd = jnp.concatenate(dkf + [dv_ref[h] for h in range(4)], axis=1).astype(BF16)
        dkr = _rope_t(dkf[0] + dkf[1] + dkf[2] + dkf[3], ckv_, sakv, sbkv, 8)
        dcqn = _dot_nt(dqd, wuq_ref[...])
        dckvn = _dot_nt(dkvd, wukv_ref[...])
        dwuq_ref[...] += _dot_tn(cqn.astype(BF16), dqd)
        dwukv_ref[...] += _dot_tn(ckvn.astype(BF16), dkvd)
        gqv, gkvv = gq_ref[...], gkv_ref[...]
        dcq = gqv * rq * dcqn - cq * (rq * rq * rq) * (jnp.sum(cq * gqv * dcqn, axis=1, keepdims=True) * (1.0 / MLA_Q_LORA))
        dckv = gkvv * rkv * dckvn - ckv * (rkv * rkv * rkv) * jnp.mean(ckv * gkvv * dckvn, axis=1, keepdims=True)
        dgq_ref[...] += jnp.sum(cq * rq * dcqn, axis=0, keepdims=True)
        dgkv_ref[...] += jnp.sum(ckv * rkv * dckvn, axis=0, keepdims=True)
        dd_ref[...] = jnp.concatenate([dcq, dckv, dkr], axis=1).astype(BF16)

    ins = [('row', proj, 512, 3), ('row', cm, 512, 0), ('row', sam, 512, 0), ('row', sbm, 512, 0), ('row', ck, 128, 0),
           ('row', sak, 128, 0), ('row', sbk, 128, 0), ('full', gq), ('full', gkv), ('full', wuq), ('full', wukv),
           ('hm', dq_hm), ('hm', dk_hm), ('hm', dv_hm)]
    return tiled_call(name, fn, s, TM, ins, [('row', 512, 512, 0, BF16), ('acc', (1, 256)), ('acc', (1, 128)),
                                             ('acc', (256, 512)), ('acc', (128, 768))])


HALO_F = 8
TN_F = 256
TM_F = 512


def _conv3(ext, w_ref):
    return _roll_r(ext, -1) * w_ref[pl.ds(0, 1), :] + ext * w_ref[pl.ds(1, 1), :] + _roll_r(ext, 1) * w_ref[pl.ds(2, 1), :]


def f_fwd(name, s, upraw, cw, cb):
    ncj = D_FF // TN_F

    def fn(i, j, xa, xg, wa_ref, wg_ref, ba_ref, bg_ref, o_ref):
        ua = (_conv3(xa, wa_ref) + ba_ref[...])[HALO_F:HALO_F + TM_F]
        ug = (_conv3(xg, wg_ref) + bg_ref[...])[HALO_F:HALO_F + TM_F]
        o_ref[...] = (ua * _sigmoid(ua) * ug).astype(BF16)

    ins = [('halo', upraw, TN_F, lambda j: j, HALO_F), ('halo', upraw, TN_F, lambda j: j + ncj, HALO_F),
           ('col', cw, TN_F, lambda j: j), ('col', cw, TN_F, lambda j: j + ncj),
           ('col', cb, TN_F, lambda j: j), ('col', cb, TN_F, lambda j: j + ncj)]
    return tiled_call(name, fn, s, TM_F, ins, [('row', D_FF, TN_F, lambda j: j, BF16)], ncol=ncj)[0]


def f_bwd(name, s, upraw, dact, cw, cb):
    ncj = D_FF // TN_F

    def fn(i, j, xa, xg, da, wa_ref, wg_ref, ba_ref, bg_ref, dxa_ref, dxg_ref, dwa_ref, dwg_ref, dba_ref, dbg_ref):
        taps_a = (_roll_r(xa, -1), xa, _roll_r(xa, 1))
        taps_g = (_roll_r(xg, -1), xg, _roll_r(xg, 1))
        ua = sum(t * wa_ref[pl.ds(k, 1), :] for k, t in enumerate(taps_a)) + ba_ref[...]
        ug = sum(t * wg_ref[pl.ds(k, 1), :] for k, t in enumerate(taps_g)) + bg_ref[...]
        sg = _sigmoid(ua)
        dug = da * ua * sg
        dua = da * ug * sg * (1.0 + ua * (1.0 - sg))
        cen = slice(HALO_F, HALO_F + TM_F)
        for du, taps, w_ref, dx_ref, dw_ref, db_ref in ((dua, taps_a, wa_ref, dxa_ref, dwa_ref, dba_ref),
                                                         (dug, taps_g, wg_ref, dxg_ref, dwg_ref, dbg_ref)):
            dx = _roll_r(du, 1) * w_ref[pl.ds(0, 1), :] + du * w_ref[pl.ds(1, 1), :] + _roll_r(du, -1) * w_ref[pl.ds(2, 1), :]
            dx_ref[...] = dx[cen].astype(BF16)
            duc = du[cen]
            for k in range(3):
                dw_ref[pl.ds(k, 1), :] += jnp.sum(duc * taps[k][cen], axis=0, keepdims=True)
            db_ref[...] += jnp.sum(duc, axis=0, keepdims=True)

    ins = [('halo', upraw, TN_F, lambda j: j, HALO_F), ('halo', upraw, TN_F, lambda j: j + ncj, HALO_F),
           ('halo', dact, TN_F, lambda j: j, HALO_F),
           ('col', cw, TN_F, lambda j: j), ('col', cw, TN_F, lambda j: j + ncj),
           ('col', cb, TN_F, lambda j: j), ('col', cb, TN_F, lambda j: j + ncj)]
    outs = [('row', D_FF, TN_F, lambda j: j, BF16), ('row', D_FF, TN_F, lambda j: j, BF16),
            ('colacc', 8, D_FF, TN_F, lambda j: j), ('colacc', 8, D_FF, TN_F, lambda j: j),
            ('colacc', 1, D_FF, TN_F, lambda j: j), ('colacc', 1, D_FF, TN_F, lambda j: j)]
    return tiled_call(name, fn, s, TM_F, ins, outs, ncol=ncj)


def _pad_w_in(w):
    z = lambda n: jnp.zeros((w.shape[0], n), w.dtype)
    return jnp.concatenate([w[:, :1728], z(64), w[:, 1728:1856], z(64), w[:, 1856:1888], z(32)], axis=1)


def _unpad_w_in(g):
    return jnp.concatenate([g[:, :1728], g[:, 1792:1920], g[:, 1984:2016]], axis=1)


def _pad_uq(w):
    w = w.reshape(MLA_Q_LORA, 4, 96)
    w = jnp.concatenate([w, jnp.zeros((MLA_Q_LORA, 4, 32), w.dtype)], axis=2).reshape(MLA_Q_LORA, 512)
    return jnp.concatenate([w, jnp.zeros((64, 512), w.dtype)], axis=0)


def _unpad_uq(g):
    return g[:MLA_Q_LORA].reshape(MLA_Q_LORA, 4, 128)[:, :, :96].reshape(MLA_Q_LORA, 384)


def _pad_ukv(w):
    w = w.reshape(MLA_KV_LORA, 4, 128)
    kn = jnp.concatenate([w[:, :, :64], jnp.zeros((MLA_KV_LORA, 4, 64), w.dtype)], axis=2).reshape(MLA_KV_LORA, 512)
    return jnp.concatenate([kn, w[:, :, 64:].reshape(MLA_KV_LORA, 256)], axis=1)


def _unpad_ukv(g):
    kn = g[:, :512].reshape(MLA_KV_LORA, 4, 128)[:, :, :64]
    v = g[:, 512:].reshape(MLA_KV_LORA, 4, 64)
    return jnp.concatenate([kn, v], axis=2).reshape(MLA_KV_LORA, 512)


def _row(v, pad_to=None):
    v = v.reshape(1, -1)
    if pad_to is not None and v.shape[1] < pad_to:
        v = jnp.concatenate([v, jnp.zeros((1, pad_to - v.shape[1]), v.dtype)], axis=1)
    return v


LATE_NAMES = ['w_out', 'ffn_w_up', 'ffn_w_down']


def _assemble(n, part, g, me):
    return jnp.concatenate([jnp.where(me == q, part, g[q]) for q in range(N_CHIPS)], axis=SHARDED_AXIS[n])


def local_step(x, tgt, W, dist=None):
    s = x.shape[0]
    tabs = _rope_tables(s)
    gm = _group_mean_matrix(256, 64)
    alpha = DEEPNORM_ALPHA
    acts = []
    late = {(n, l): W[n][l] for n in LATE_NAMES for l in range(DEPTH)} if dist is None else {}
    h = ln_fwd("ln_in", s, x, _row(W['ln_in_g']), _row(W['ln_in_b']))
    for l in range(DEPTH):
        A = dict(h=h)
        A['wpad'] = _pad_w_in(W['w_in'][l]).astype(BF16)
        A['wuq'] = _pad_uq(W['mla_w_uq'][l]).astype(BF16)
        A['wukv'] = _pad_ukv(W['mla_w_ukv'][l]).astype(BF16)
        A['cw'] = jnp.concatenate([W['conv_a_w'][l], jnp.zeros((1, GROUP_W), F32)], axis=0)
        A['fcw'] = jnp.concatenate([W['ffn_conv_w'][l], jnp.zeros((5, 2 * D_FF), F32)], axis=0)
        A['gq'] = jnp.tile(_row(W['qk_norm_q'][l]), (1, 4))
        A['gk'] = jnp.tile(_row(W['qk_norm_k'][l]), (1, 2))
        A['sgu_bias'] = jnp.repeat(W['sgu_b'][l].T, 64, axis=1)
        A['sgu_wt'] = jnp.swapaxes(W['sgu_w'][l], 1, 2)
        proj = mm(f"proj{l}", h, A['wpad'])
        A['proj'] = proj
        A['y_a'], o_a = a_fwd(f"a_fwd{l}", s, proj, A['cw'], _row(W['conv_a_b'][l]), _row(W['ln_a_g'][l]), _row(W['ln_a_b'][l]))
        A['bq'], A['bk'], A['bv'] = b_prep_fwd(f"b_prep{l}", s, proj, A['gq'], A['gk'], tabs['gqa'], gm)
        if dist is not None and l == 0:
            lp = dist['late']
            parts = [lp['w_out'], lp['ffn_w_up'][0:1], lp['ffn_w_down'][0:1]]
            nxt_names, nxt_parts = dist['next']
            (A['o_b'], A['lse_b']), got = attn_fwd(f"b_attn{l}", s, A['bq'], A['bk'], A['bv'],
                                                   comm=gather_over_chips(parts + nxt_parts))
            full = [_assemble(n, p, g, dist['me']) for n, p, g in zip(LATE_NAMES, parts, got)]
            late[('w_out', 0)], late[('w_out', 1)] = full[0][0], full[0][1]
            late[('ffn_w_up', 0)], late[('ffn_w_down', 0)] = full[1][0], full[2][0]
            for n, p, g in zip(nxt_names, nxt_parts, got[len(parts):]):
                W[n][1] = _assemble(n, p, g, dist['me'])[0]
        else:
            A['o_b'], A['lse_b'] = attn_fwd(f"b_attn{l}", s, A['bq'], A['bk'], A['bv'])
        o_c = c_fwd(f"c_fwd{l}", s, proj, _row(W['sgu_ln_g'][l]), _row(W['sgu_ln_b'][l]), W['sgu_w'][l], A['sgu_bias'])
        A['dq'], A['dk'], A['dv'] = d_prep_fwd(f"d_prep{l}", s, proj, _row(W['mla_q_norm'][l], 256), _row(W['mla_kv_norm'][l]),
                                               A['wuq'], A['wukv'], tabs['mla'], tabs['kr'])
        if dist is not None and l == 0:
            parts = [lp['ffn_w_up'][1:2], lp['ffn_w_down'][1:2]]
            (A['o_d'], A['lse_d']), got = attn_fwd(f"d_attn{l}", s, A['dq'], A['dk'], A['dv'], comm=gather_over_chips(parts))
            late[('ffn_w_up', 1)] = _assemble('ffn_w_up', parts[0], got[0], dist['me'])[0]
            late[('ffn_w_down', 1)] = _assemble('ffn_w_down', parts[1], got[1], dist['me'])[0]
        else:
            A['o_d'], A['lse_d'] = attn_fwd(f"d_attn{l}", s, A['dq'], A['dk'], A['dv'])
        A['wout'] = late[('w_out', l)].astype(BF16)
        A['wup'] = late[('ffn_w_up', l)].astype(BF16)
        A['wdown'] = late[('ffn_w_down', l)].astype(BF16)
        A['mixcat'] = jnp.concatenate([o_a, A['o_b'], o_c, A['o_d']], axis=1).astype(BF16)
        A['mix'] = mm(f"out_proj{l}", A['mixcat'], A['wout'])
        h1 = ln_fwd(f"ln_mix{l}", s, A['mix'], _row(W['ln_mix_g'][l]), _row(W['ln_mix_b'][l]), res=h, alpha=alpha)
        A['h1'] = h1
        A['upraw'] = mm(f"ffn_up{l}", h1, A['wup'])
        A['act'] = f_fwd(f"f_fwd{l}", s, A['upraw'], A['fcw'], _row(W['ffn_conv_b'][l]))
        A['f'] = mm(f"ffn_down{l}", A['act'], A['wdown'])
        if l < DEPTH - 1:
            h = ln_fwd(f"ln_ffn{l}", s, A['f'], _row(W['ln_ffn_g'][l]), _row(W['ln_ffn_b'][l]), res=h1, alpha=alpha)
        acts.append(A)

    per_layer = {n: [None] * DEPTH for n in WEIGHT_NAMES if n not in ('ln_in_g', 'ln_in_b')}
    dh = None
    loss_row = None
    overlap = dist is not None and DEPTH == 2
    red = {}
    grp1 = [(n, 1) for n in BIG_NAMES]
    grp2 = [('ffn_w_up', 0), ('ffn_w_down', 0)]

    def carried(fn, name, *a, comm=None, **kw):
        if comm is None:
            return fn(name, *a, **kw), None
        return fn(name, *a, comm=comm, **kw)

    def pair_sums(tag, xs, r1):
        return [pair_sum(f"pair_sum_{tag}{t}", xv, rv, dist['idx'], BF16) for t, (xv, rv) in enumerate(zip(xs, r1))]

    def chip_sums(tag, cs, ps):
        return [chip_sum(f"chip_sum_{tag}{t}", [cv], [pv], dist['idx']) for t, (cv, pv) in enumerate(zip(cs, ps))]

    for l in reversed(range(DEPTH)):
        A = acts[l]
        carry = overlap and l == 0
        g_ffn, b_ffn = _row(W['ln_ffn_g'][l]), _row(W['ln_ffn_b'][l])
        if l == DEPTH - 1:
            dz, dg, db, loss_row = loss_and_ln_bwd("loss", s, A['f'], A['h1'], g_ffn, b_ffn, tgt, alpha)
        else:
            dz, dg, db = ln_bwd(f"ln_ffn_bwd{l}", s, dh, A['f'], g_ffn, res=A['h1'], alpha=alpha)
        per_layer['ln_ffn_g'][l], per_layer['ln_ffn_b'][l] = dg[0], db[0]
        xs1 = [per_layer[n][k] for n, k in grp1] if carry else None
        dwd, r1 = carried(mm, f"dw_down{l}", A['act'], dz, ta=True, comm=pair_exchange(xs1) if carry else None)
        per_layer['ffn_w_down'][l] = dwd.reshape(N_CHIPS, D_FF // N_CHIPS, D_MODEL)
        ps1 = pair_sums("g1_", xs1, r1) if carry else None
        dact = mm(f"d_act{l}", dz, A['wdown'], tb=True)
        dxa, dxg, dwa, dwg, dba, dbg = f_bwd(f"f_bwd{l}", s, A['upraw'], dact, A['fcw'], _row(W['ffn_conv_b'][l]))
        per_layer['ffn_conv_w'][l] = jnp.concatenate([dwa[:3], dwg[:3]], axis=1)
        per_layer['ffn_conv_b'][l] = jnp.concatenate([dba[0], dbg[0]], axis=0)
        dup = jnp.concatenate([dxa, dxg], axis=1)
        per_layer['ffn_w_up'][l] = mm(f"dw_up{l}", A['h1'], dup, ta=True, chip_major=True)
        xs2 = [per_layer[n][k] for n, k in grp2] if carry else None
        dh1_mm, r2 = carried(mm, f"d_h1{l}", dup, A['wup'], tb=True, comm=pair_exchange(xs2) if carry else None)
        ps2 = pair_sums("g2_", xs2, r2) if carry else None
        dz1, dg, db = _ln_mix_bwd(l, s, dz, dh1_mm, A, W, alpha)
        per_layer['ln_mix_g'][l], per_layer['ln_mix_b'][l] = dg[0], db[0]
        per_layer['w_out'][l] = mm(f"dw_out{l}", A['mixcat'], dz1, ta=True).reshape(N_CHIPS, D_MODEL // N_CHIPS, D_MODEL)
        dmixcat = mm(f"d_mixcat{l}", dz1, A['wout'], tb=True)
        dy_a, dg, db, dcb = a_bwd1(f"a_bwd1{l}", s, A['y_a'], dmixcat, _row(W['ln_a_g'][l]), _row(W['ln_a_b'][l]))
        per_layer['ln_a_g'][l], per_layer['ln_a_b'][l], per_layer['conv_a_b'][l] = dg[0], db[0], dcb[0]
        dp_a, dcw = a_bwd2(f"a_bwd2{l}", s, A['proj'], dy_a, A['cw'])
        per_layer['conv_a_w'][l] = dcw[:CONV_A_WIDTH]
        (dq, dk, dv), cs1 = carried(attn_bwd, f"b_attn_bwd{l}", s, A['bq'], A['bk'], A['bv'], A['o_b'], A['lse_b'], dmixcat, 2,
                                    comm=chip_exchange(ps1) if carry else None)
        halves1 = chip_sums("g1_", cs1, ps1) if carry else None
        dp_b, dgq, dgk = b_prep_bwd(f"b_prep_bwd{l}", s, A['proj'], A['gq'], A['gk'], tabs['gqa'], gm, dq, dk, dv)
        per_layer['qk_norm_q'][l], per_layer['qk_norm_k'][l] = dgq[0], dgk[0]
        dp_c, dg, db, dsw, dsb = c_bwd(f"c_bwd{l}", s, A['proj'], _row(W['sgu_ln_g'][l]), _row(W['sgu_ln_b'][l]), W['sgu_w'][l],
                                       A['sgu_wt'], A['sgu_bias'], dmixcat)
        per_layer['sgu_ln_g'][l], per_layer['sgu_ln_b'][l] = dg[0], db[0]
        per_layer['sgu_w'][l], per_layer['sgu_b'][l] = dsw, dsb[:, :4].T
        comms = [half_exchange(halves1), chip_exchange(ps2)] if carry else []
        (dq, dk, dv), cres = carried(attn_bwd, f"d_attn_bwd{l}", s, A['dq'], A['dk'], A['dv'], A['o_d'], A['lse_d'], dmixcat, 6,
                                     comm=join_comms(comms))
        if carry:
            done1, cs2 = split_comm_results(comms, cres)
            red.update(zip(grp1, done1))
            halves2 = chip_sums("g2_", cs2, ps2)
        dp_d, dgq, dgkv, dwuq, dwukv = d_prep_bwd(f"d_prep_bwd{l}", s, A['proj'], _row(W['mla_q_norm'][l], 256),
                                                  _row(W['mla_kv_norm'][l]), A['wuq'], A['wukv'], tabs['mla'], tabs['kr'],
                                                  dq, dk, dv)
        per_layer['mla_q_norm'][l], per_layer['mla_kv_norm'][l] = dgq[0, :MLA_Q_LORA], dgkv[0]
        per_layer['mla_w_uq'][l], per_layer['mla_w_ukv'][l] = _unpad_uq(dwuq), _unpad_ukv(dwukv)
        dproj = jnp.concatenate([dp_a, dp_b, dp_c, dp_d], axis=1)
        dw_in, done2 = carried(mm, f"dw_in{l}", A['h'], dproj, ta=True, comm=half_exchange(halves2) if carry else None)
        if carry:
            red.update(zip(grp2, done2))
        per_layer['w_in'][l] = _unpad_w_in(dw_in).reshape(D_MODEL, N_CHIPS, D_IN_PROJ // N_CHIPS).transpose(1, 0, 2)
        dh = mm(f"d_h{l}", dproj, A['wpad'], tb=True, add=dz1, add_scale=alpha)
    dx, dg, db = ln_bwd("ln_in_bwd", s, dh, x, _row(W['ln_in_g']))
    G = dict(per_layer)
    G['ln_in_g'], G['ln_in_b'] = dg[0], db[0]
    for n, k in red:
        G[n][k] = None
    G['reduced'] = red
    return loss_row, dx, G


def _ln_mix_bwd(l, s, dz, dh1_mm, A, W, alpha):
    ins = [('row', dz, D_MODEL, 0), ('row', dh1_mm, D_MODEL, 0), ('row', A['mix'], D_MODEL, 0), ('row', A['h'], D_MODEL, 0),
           ('full', _row(W['ln_mix_g'][l]))]

    def fn(i, j, dzv, dmv, xv, rv, g_ref, o_ref, dg_ref, db_ref):
        xh, r = _ln_stats(alpha * rv + xv)
        d, dg, db = _ln_bwd(alpha * dzv + dmv, xh, r, g_ref[...])
        o_ref[...] = d
        dg_ref[...] += dg
        db_ref[...] += db

    return tiled_call(f"ln_mix_bwd{l}", fn, s, TM, ins, [('row', D_MODEL, D_MODEL, 0, F32), ('acc', (1, D_MODEL)),
                                                         ('acc', (1, D_MODEL))])


ANY = pl.BlockSpec(memory_space=pl.ANY)


def _pos():
    return lax.axis_index("x"), lax.axis_index("y"), lax.axis_index("c")


def _rcopy(src, dst, ssem, rsem, dev):
    return pltpu.make_async_remote_copy(src_ref=src, dst_ref=dst, send_sem=ssem, recv_sem=rsem, device_id=dev,
                                        device_id_type=MESH)


class Comm:
    def __init__(self, operands, out_shape, sems, start, finish, aliases=None):
        self.operands, self.out_shape, self.sems = list(operands), list(out_shape), list(sems)
        self.start, self.finish, self.aliases = start, finish, dict(aliases or {})


def join_comms(comms):
    comms = [cm for cm in comms if cm is not None]
    if not comms:
        return None
    offs, oi, oo, os_ = [], 0, 0, 0
    for cm in comms:
        offs.append((oi, oo, os_))
        oi, oo, os_ = oi + len(cm.operands), oo + len(cm.out_shape), os_ + len(cm.sems)

    def part(fn_name):
        def run(ins, outs, sems):
            for cm, (a, b, s) in zip(comms, offs):
                getattr(cm, fn_name)(ins[a:a + len(cm.operands)], outs[b:b + len(cm.out_shape)], sems[s:s + len(cm.sems)])
        return run

    aliases = {a + i: b + o for cm, (a, b, s) in zip(comms, offs) for i, o in cm.aliases.items()}
    return Comm([v for cm in comms for v in cm.operands], [v for cm in comms for v in cm.out_shape],
                [v for cm in comms for v in cm.sems], part("start"), part("finish"), aliases)


def split_comm_results(comms, res):
    out, p = [], 0
    for cm in comms:
        if cm is None:
            out.append(None)
        else:
            out.append(list(res[p:p + len(cm.out_shape)]))
            p += len(cm.out_shape)
    return out


def run_comm(name, comm):
    n_in, n_out = len(comm.operands), len(comm.out_shape)

    def body(*refs):
        ins, outs, sems = refs[:n_in], refs[n_in:n_in + n_out], refs[n_in + n_out:]
        comm.start(ins, outs, sems)
        comm.finish(ins, outs, sems)

    return pl.pallas_call(body, name=name, in_specs=[ANY] * n_in, out_specs=[ANY] * n_out, out_shape=comm.out_shape,
                          scratch_shapes=[pltpu.SemaphoreType.DMA((k,)) for k in comm.sems],
                          input_output_aliases=comm.aliases)(*comm.operands)


def gather_over_chips(parts):
    n = len(parts)
    split = [p.shape[1] % 32 == 0 for p in parts]

    def geometry():
        x, y, c = _pos()
        return x, y, c, 2 * x + y, [(1 - x, y), (x, 1 - y), (1 - x, 1 - y)]

    def rows(t, half):
        a2 = parts[t].shape[1] // 2
        return pl.ds(half * a2, a2)

    def ici(srcs, dsts, sems, j, t, px, py, c, me):
        k = n * j + t
        if split[t]:
            return _rcopy(srcs[t].at[:, rows(t, c), :], dsts[t].at[me, :, rows(t, c), :], sems[0].at[k], sems[1].at[k], (px, py, c))
        return _rcopy(srcs[t], dsts[t].at[me], sems[0].at[k], sems[1].at[k], (px, py, c))

    def start(srcs, dsts, sems):
        x, y, c, me, chips = geometry()
        for j, (px, py) in enumerate(chips):
            for t in range(n):
                ici(srcs, dsts, sems, j, t, px, py, c, me).start()

    def finish(srcs, dsts, sems):
        x, y, c, me, chips = geometry()
        fwd = []
        for j, (px, py) in enumerate(chips):
            q = 2 * px + py
            for t in range(n):
                k = n * j + t
                if split[t]:
                    got = dsts[t].at[q, :, rows(t, c), :]
                    _rcopy(got, got, sems[0].at[k], sems[1].at[k], (px, py, c)).wait_recv()
                    fw = _rcopy(got, got, sems[2].at[k], sems[3].at[k], (x, y, 1 - c))
                    fw.start()
                    fwd.append(fw)
                else:
                    _rcopy(srcs[t], dsts[t].at[q], sems[0].at[k], sems[1].at[k], (px, py, c)).wait_recv()
        for j, (px, py) in enumerate(chips):
            q = 2 * px + py
            for t in range(n):
                if split[t]:
                    other = dsts[t].at[q, :, rows(t, 1 - c), :]
                    _rcopy(other, other, sems[2].at[n * j + t], sems[3].at[n * j + t], (x, y, 1 - c)).wait_recv()
                ici(srcs, dsts, sems, j, t, px, py, c, me).wait_send()
        for fw in fwd:
            fw.wait_send()

    return Comm(parts, [jax.ShapeDtypeStruct((N_CHIPS,) + p.shape, p.dtype) for p in parts], [3 * n] * 4, start, finish)


def pair_exchange(xs):
    n = len(xs)

    def copies(srcs, dsts, sems):
        x, y, c = _pos()
        return [_rcopy(srcs[t].at[:, pl.ds((1 - c) * (xs[t].shape[1] // 2), xs[t].shape[1] // 2), :], dsts[t],
                       sems[0].at[t], sems[1].at[t], (x, y, 1 - c)) for t in range(n)]

    def start(srcs, dsts, sems):
        for cp in copies(srcs, dsts, sems):
            cp.start()

    def finish(srcs, dsts, sems):
        for cp in copies(srcs, dsts, sems):
            cp.wait()

    return Comm(xs, [jax.ShapeDtypeStruct((a.shape[0], a.shape[1] // 2, a.shape[2]), a.dtype) for a in xs], [n, n], start, finish)


def chip_exchange(ps):
    n = len(ps)

    def geometry():
        x, y, c = _pos()
        return c, 2 * x + y, [(1 - x, y), (x, 1 - y), (1 - x, 1 - y)]

    def start(srcs, dsts, sems):
        c, me, chips = geometry()
        for j, (px, py) in enumerate(chips):
            for t in range(n):
                _rcopy(srcs[t].at[2 * px + py], dsts[t].at[me], sems[0].at[n * j + t], sems[1].at[n * j + t], (px, py, c)).start()

    def finish(srcs, dsts, sems):
        c, me, chips = geometry()
        for j, (px, py) in enumerate(chips):
            for t in range(n):
                _rcopy(srcs[t].at[2 * px + py], dsts[t].at[2 * px + py], sems[0].at[n * j + t], sems[1].at[n * j + t], (px, py, c)).wait()

    return Comm(ps, [jax.ShapeDtypeStruct(p.shape, p.dtype) for p in ps], [3 * n, 3 * n], start, finish)


def half_exchange(bufs):
    n = len(bufs)

    def copies(ins, outs, sems):
        x, y, c = _pos()
        cps = []
        for t in range(n):
            a2 = bufs[t].shape[1] // 2
            mine = pl.ds(c * a2, a2)
            cps.append(_rcopy(ins[t].at[:, mine, :], outs[t].at[:, mine, :], sems[0].at[t], sems[1].at[t], (x, y, 1 - c)))
        return cps

    def start(ins, outs, sems):
        for cp in copies(ins, outs, sems):
            cp.start()

    def finish(ins, outs, sems):
        for cp in copies(ins, outs, sems):
            cp.wait()

    return Comm(bufs, [jax.ShapeDtypeStruct(b.shape, b.dtype) for b in bufs], [n, n], start, finish, {t: t for t in range(n)})


LANES = 1024
ROW_TILE_BYTES = 2 * 1024 * 1024


def _row_tile(r, row_bytes):
    if r * row_bytes <= ROW_TILE_BYTES:
        return r
    best = None
    for t in range(16, r, 16):
        if r % t == 0 and t * row_bytes <= ROW_TILE_BYTES:
            best = t
    assert best is not None, (r, row_bytes)
    return best


def pair_sum(name, x, r1, c_arr, out_dtype):
    q, a2, b = r1.shape
    tr = _row_tile(a2, b * 4)
    nb = a2 // tr

    def body(c_ref, x_ref, r_ref, o_ref):
        o_ref[...] = (x_ref[...] + r_ref[...]).astype(out_dtype)

    grid_spec = pltpu.PrefetchScalarGridSpec(
        num_scalar_prefetch=1, grid=(q, nb),
        in_specs=[pl.BlockSpec((None, tr, b), lambda k, i, c_ref: (k, c_ref[0] * nb + i, 0)),
                  pl.BlockSpec((None, tr, b), lambda k, i, c_ref: (k, i, 0))],
        out_specs=pl.BlockSpec((None, tr, b), lambda k, i, c_ref: (k, i, 0)))
    return pl.pallas_call(body, name=name, grid_spec=grid_spec, out_shape=jax.ShapeDtypeStruct(r1.shape, out_dtype),
                          compiler_params=_cparams(("parallel", "parallel")))(c_arr, x, r1)


def chip_sum(name, recv, own, idx):
    nl = len(recv)
    k, a2, b = recv[0].shape
    tr = _row_tile(a2, k * b * recv[0].dtype.itemsize)
    nb = a2 // tr

    def body(idx_ref, *refs):
        rs, os_, o_ref = refs[:nl], refs[nl:2 * nl], refs[2 * nl]
        l = pl.program_id(0)
        me = idx_ref[1]
        for li in range(nl):
            @pl.when(l == li)
            def _(li=li):
                acc = None
                for q in range(k):
                    term = jnp.where(me == q, os_[li][...], rs[li][q]).astype(F32)
                    acc = term if acc is None else acc + term
                o_ref[...] = acc

    def pick(li):
        return lambda l, i, idx_ref: jnp.where(l == li, i, 0)

    in_specs = [pl.BlockSpec((k, tr, b), lambda l, i, idx_ref, _p=pick(li): (0, _p(l, i, idx_ref), 0)) for li in range(nl)]
    in_specs += [pl.BlockSpec((None, tr, b), lambda l, i, idx_ref, _p=pick(li): (idx_ref[1], _p(l, i, idx_ref), 0)) for li in range(nl)]
    grid_spec = pltpu.PrefetchScalarGridSpec(
        num_scalar_prefetch=1, grid=(nl, nb), in_specs=in_specs,
        out_specs=pl.BlockSpec((None, tr, b), lambda l, i, idx_ref: (l, idx_ref[0] * nb + i, 0)))
    return pl.pallas_call(body, name=name, grid_spec=grid_spec, out_shape=jax.ShapeDtypeStruct((nl, 2 * a2, b), F32),
                          compiler_params=_cparams(("arbitrary", "arbitrary")))(idx, *recv, *own)


def adamw(name, gs, w, m, v):
    nl, r, b = w.shape
    assert len(gs) == nl
    tr = _row_tile(r, b * 4)

    def body(*refs):
        g_refs, (w_ref, m_ref, v_ref, go_ref, d_ref, nm_ref, nv_ref) = refs[:nl], refs[nl:]
        l = pl.program_id(0)
        for li in range(nl):
            @pl.when(l == li)
            def _(li=li):
                gv = g_refs[li][...]
                go_ref[...] = gv
                mn = ADAM_B1 * m_ref[...] + (1.0 - ADAM_B1) * gv
                vn = ADAM_B2 * v_ref[...] + (1.0 - ADAM_B2) * (gv * gv)
                m_hat = mn / (1.0 - ADAM_B1 ** ADAM_STEP)
                v_hat = vn / (1.0 - ADAM_B2 ** ADAM_STEP)
                d_ref[...] = -ADAM_LR * (m_hat / (jnp.sqrt(v_hat) + ADAM_EPS) + ADAM_WD * w_ref[...])
                nm_ref[...] = mn
                nv_ref[...] = vn

    g_specs = [pl.BlockSpec((None, tr, b), lambda l, i, _li=li: (0, jnp.where(l == _li, i, 0), 0)) for li in range(nl)]
    spec = pl.BlockSpec((None, tr, b), lambda l, i: (l, i, 0))
    sds = jax.ShapeDtypeStruct(w.shape, F32)
    return pl.pallas_call(body, name=name, grid=(nl, r // tr), in_specs=g_specs + [spec] * 3, out_specs=[spec] * 4,
                          out_shape=[sds] * 4, compiler_params=_cparams(("arbitrary", "arbitrary")))(*gs, w, m, v)


def _pack_rows(shapes):
    return [8 * -(-int(np.prod(s)) // (8 * LANES)) for s in shapes]


def _pack(arrs, rows):
    pieces = []
    for a, r in zip(arrs, _pack_rows([a.shape for a in arrs])):
        flat = a.reshape(-1)
        pieces.append(jnp.pad(flat, (0, r * LANES - flat.shape[0])).reshape(r, LANES))
    used = sum(p.shape[0] for p in pieces)
    assert used <= rows
    if used < rows:
        pieces.append(jnp.zeros((rows - used, LANES), F32))
    return jnp.concatenate(pieces, axis=0)


def _unpack(pack, shapes):
    out, off = [], 0
    for shp, r in zip(shapes, _pack_rows(shapes)):
        n = int(np.prod(shp))
        out.append(pack[off:off + r].reshape(-1)[:n].reshape(shp))
        off += r
    return out


BIG_NAMES = ['w_in', 'w_out', 'ffn_w_up', 'ffn_w_down']
MED_NAMES = [n for n in SHARDED_NAMES if n not in BIG_NAMES]
MED_ROWS = 96
SMALL_ROWS = 288


def kernel(x, ln_in_g, ln_in_b, w_in, conv_a_w, conv_a_b, ln_a_g, ln_a_b, qk_norm_q, qk_norm_k, sgu_ln_g, sgu_ln_b, sgu_w, sgu_b, mla_q_norm, mla_w_uq, mla_kv_norm, mla_w_ukv, w_out, ln_mix_g, ln_mix_b, ffn_w_up, ffn_conv_w, ffn_conv_b, ffn_w_down, ln_ffn_g, ln_ffn_b, loss_target, m_ln_in_g, m_ln_in_b, m_w_in, m_conv_a_w, m_conv_a_b, m_ln_a_g, m_ln_a_b, m_qk_norm_q, m_qk_norm_k, m_sgu_ln_g, m_sgu_ln_b, m_sgu_w, m_sgu_b, m_mla_q_norm, m_mla_w_uq, m_mla_kv_norm, m_mla_w_ukv, m_w_out, m_ln_mix_g, m_ln_mix_b, m_ffn_w_up, m_ffn_conv_w, m_ffn_conv_b, m_ffn_w_down, m_ln_ffn_g, m_ln_ffn_b, v_ln_in_g, v_ln_in_b, v_w_in, v_conv_a_w, v_conv_a_b, v_ln_a_g, v_ln_a_b, v_qk_norm_q, v_qk_norm_k, v_sgu_ln_g, v_sgu_ln_b, v_sgu_w, v_sgu_b, v_mla_q_norm, v_mla_w_uq, v_mla_kv_norm, v_mla_w_ukv, v_w_out, v_ln_mix_g, v_ln_mix_b, v_ffn_w_up, v_ffn_conv_w, v_ffn_conv_b, v_ffn_w_down, v_ln_ffn_g, v_ln_ffn_b):
    loc = dict(locals())
    w_loc = {n: loc[n] for n in WEIGHT_NAMES}
    m_loc = {n: loc["m_" + n] for n in WEIGHT_NAMES}
    v_loc = {n: loc["v_" + n] for n in WEIGHT_NAMES}
    me_chip = 2 * lax.axis_index("x") + lax.axis_index("y")
    idx = jnp.stack([lax.axis_index("c"), me_chip]).astype(jnp.int32)

    def wire(n):
        return w_loc[n].astype(BF16) if n in BF16_WIRE else w_loc[n]

    assert DEPTH == 2
    early = [n for n in SHARDED_NAMES if n not in LATE_NAMES]
    parts = [wire(n)[0:1] for n in early]
    W = {n: w_loc[n] for n in REPL_NAMES}
    for n, p, g in zip(early, parts, run_comm("gather_early", gather_over_chips(parts))):
        W[n] = [_assemble(n, p, g, me_chip)[0], None]

    dist = dict(idx=idx, me=me_chip, late={n: wire(n) for n in LATE_NAMES}, next=(early, [wire(n)[1:2] for n in early]))
    loss_row, dx, G = local_step(x[0], loss_target[0], W, dist)
    loss = lax.psum(loss_row[0, 0], ("x", "y", "c"))
    red = G['reduced']

    med = []
    for q in range(N_CHIPS):
        pieces = [lax.slice_in_dim(jnp.stack(G[n]), q * w_loc[n].shape[2], (q + 1) * w_loc[n].shape[2], axis=2) for n in MED_NAMES]
        med.append(_pack(pieces, MED_ROWS))
    rest = [(n, l) for n in BIG_NAMES for l in range(DEPTH) if (n, l) not in red]
    small_pack = _pack([jnp.stack(G[n]) if isinstance(G[n], list) else G[n] for n in REPL_NAMES], SMALL_ROWS)
    xs = [G[n][l] for n, l in rest] + [jnp.stack(med), jnp.broadcast_to(small_pack[None], (N_CHIPS, SMALL_ROWS, LANES))]
    wire_dt = [BF16] * (len(xs) - 1) + [F32]
    r1 = run_comm("pair_exchange", pair_exchange(xs))
    ps = [pair_sum(f"pair_sum{t}", xv, rv, idx, dt) for t, (xv, rv, dt) in enumerate(zip(xs, r1, wire_dt))]
    cs = run_comm("chip_exchange", chip_exchange(ps))
    halves = [chip_sum(f"chip_sum{t}", [cv], [pv], idx) for t, (cv, pv) in enumerate(zip(cs, ps))]
    done = run_comm("half_exchange", half_exchange(halves))
    red.update(zip(rest, done[:-2]))
    g_med, g_small = done[-2], done[-1]

    outs = {k: {} for k in ("grad", "delta", "new_m", "new_v")}
    for n in BIG_NAMES:
        res = adamw(f"adamw_{n}", [red[(n, l)] for l in range(DEPTH)], w_loc[n], m_loc[n], v_loc[n])
        for k, v in zip(("grad", "delta", "new_m", "new_v"), res):
            outs[k][n] = v
    for names, g, rows in ((MED_NAMES, g_med, MED_ROWS), (REPL_NAMES, g_small, SMALL_ROWS)):
        res = adamw(f"adamw_{names[0]}_pack", [g], *[_pack([d[n] for n in names], rows)[None] for d in (w_loc, m_loc, v_loc)])
        for k, v in zip(("grad", "delta", "new_m", "new_v"), res):
            outs[k].update(zip(names, _unpack(v[0], [w_loc[n].shape for n in names])))
    return (loss, dx[None], *[outs["grad"][n] for n in WEIGHT_NAMES], *[outs["delta"][n] for n in WEIGHT_NAMES],
            *[outs["new_m"][n] for n in WEIGHT_NAMES], *[outs["new_v"][n] for n in WEIGHT_NAMES])
```

```python
import functools
import math

import jax
import jax.numpy as jnp
import numpy as np
from jax import lax
from jax.experimental import pallas as pl
from jax.experimental.pallas import tpu as pltpu

F32 = jnp.float32
BF16 = jnp.bfloat16

D_MODEL = 1024
DEPTH = 2
GRID_W = 64
GROUP_W = 256
HEAD_DIM = 64
CONV_A_WIDTH = 31
CHUNK = 128
MLA_Q_LORA = 192
MLA_KV_LORA = 128
MLA_NOPE = 64
MLA_ROPE = 32
D_FF = 2816
ROPE_THETA = 10000.0
DEEPNORM_ALPHA = (2 * DEPTH) ** 0.25
LN_EPS = 1e-5
RMS_EPS = 1e-6
D_IN_PROJ = 1888
PROJ_W = 2048

ADAM_LR = 0.001
ADAM_B1 = 0.9
ADAM_B2 = 0.999
ADAM_EPS = 1e-08
ADAM_WD = 0.01
ADAM_STEP = 10

VMEM_LIMIT = 56 * 1024 * 1024
MESH = pl.DeviceIdType.MESH
N_CHIPS = 4
N_DEV = 8

WEIGHT_NAMES = ['ln_in_g', 'ln_in_b', 'w_in', 'conv_a_w', 'conv_a_b', 'ln_a_g', 'ln_a_b', 'qk_norm_q', 'qk_norm_k',
                'sgu_ln_g', 'sgu_ln_b', 'sgu_w', 'sgu_b', 'mla_q_norm', 'mla_w_uq', 'mla_kv_norm', 'mla_w_ukv', 'w_out',
                'ln_mix_g', 'ln_mix_b', 'ffn_w_up', 'ffn_conv_w', 'ffn_conv_b', 'ffn_w_down', 'ln_ffn_g', 'ln_ffn_b']
SHARDED_AXIS = {'w_in': 2, 'conv_a_w': 2, 'mla_w_uq': 2, 'mla_w_ukv': 2, 'w_out': 1, 'ffn_w_up': 2, 'ffn_conv_w': 2,
                'ffn_w_down': 1}
SHARDED_NAMES = [n for n in WEIGHT_NAMES if n in SHARDED_AXIS]
REPL_NAMES = [n for n in WEIGHT_NAMES if n not in SHARDED_AXIS]
BF16_WIRE = ('w_in', 'mla_w_uq', 'mla_w_ukv', 'w_out', 'ffn_w_up', 'ffn_w_down')


def _cparams(sem):
    return pltpu.CompilerParams(dimension_semantics=sem, vmem_limit_bytes=VMEM_LIMIT)


def _pick(n, cands):
    for c in cands:
        if n % c == 0:
            return c
    return n


def carried_call(name, body, grid, in_specs, out_specs, out_shape, scratch, semantics, args, comm=None):
    if comm is None:
        res = pl.pallas_call(body, name=name, grid=grid, in_specs=in_specs, out_specs=out_specs, out_shape=out_shape,
                             scratch_shapes=scratch, compiler_params=_cparams(semantics))(*args)
        return list(res), None
    n_in, n_out, n_scr = len(in_specs), len(out_specs), len(scratch)
    nci, nco = len(comm.operands), len(comm.out_shape)

    def carried(*refs):
        o0 = n_in + nci
        s0 = o0 + n_out + nco
        ins, cins = refs[:n_in], refs[n_in:o0]
        outs, couts = refs[o0:o0 + n_out], refs[o0 + n_out:s0]
        scr, sems = refs[s0:s0 + n_scr], refs[s0 + n_scr:]
        ids = [pl.program_id(d) for d in range(len(grid))]
        first = functools.reduce(lambda u, v: u & v, [i == 0 for i in ids])
        last = functools.reduce(lambda u, v: u & v, [i == g - 1 for i, g in zip(ids, grid)])

        @pl.when(first)
        def _():
            comm.start(cins, couts, sems)

        body(*ins, *outs, *scr)

        @pl.when(last)
        def _():
            comm.finish(cins, couts, sems)

    res = pl.pallas_call(
        carried, name=name, grid=grid, in_specs=list(in_specs) + [ANY] * nci, out_specs=list(out_specs) + [ANY] * nco,
        out_shape=list(out_shape) + comm.out_shape, scratch_shapes=list(scratch) + [pltpu.SemaphoreType.DMA((k,)) for k in comm.sems],
        input_output_aliases={n_in + i: n_out + o for i, o in comm.aliases.items()},
        compiler_params=_cparams(("arbitrary",) * len(grid)))(*args, *comm.operands)
    return list(res[:n_out]), list(res[n_out:])


def mm(name, a, b, *, ta=False, tb=False, add=None, add_scale=1.0, out_dtype=F32, chip_major=False, comm=None):
    m, k = (a.shape[1], a.shape[0]) if ta else a.shape
    n = b.shape[0] if tb else b.shape[1]
    assert (b.shape[1] if tb else b.shape[0]) == k
    tm = _pick(m, (1024, 1408, 512, 256, 128))
    tn = n // N_CHIPS if chip_major else _pick(n, (1024, 1408, 512, 256, 128))
    tk = _pick(k, (1024, 1408, 512, 256, 128))
    nk = k // tk
    a_spec = pl.BlockSpec((tk, tm), lambda i, j, kk: (kk, i)) if ta else pl.BlockSpec((tm, tk), lambda i, j, kk: (i, kk))
    b_spec = pl.BlockSpec((tn, tk), lambda i, j, kk: (j, kk)) if tb else pl.BlockSpec((tk, tn), lambda i, j, kk: (kk, j))
    in_specs = [a_spec, b_spec]
    args = [a, b]
    if add is not None:
        in_specs.append(pl.BlockSpec((tm, tn), lambda i, j, kk: (i, j)))
        args.append(add)
    dims = (((0 if ta else 1,), (1 if tb else 0,)), ((), ()))

    def body(*refs):
        a_ref, b_ref = refs[0], refs[1]
        add_ref = refs[2] if add is not None else None
        o_ref, acc_ref = refs[-2], refs[-1]
        kk = pl.program_id(2)
        part = lax.dot_general(a_ref[...].astype(BF16), b_ref[...].astype(BF16), dims, preferred_element_type=F32)

        @pl.when(kk == 0)
        def _():
            acc_ref[...] = part

        @pl.when(kk > 0)
        def _():
            acc_ref[...] += part

        @pl.when(kk == nk - 1)
        def _():
            r = acc_ref[...]
            if add_ref is not None:
                r = r + add_scale * add_ref[...].astype(F32)
            o_ref[...] = r.astype(out_dtype)

    if chip_major:
        out_spec = pl.BlockSpec((None, tm, tn), lambda i, j, kk: (j, i, 0))
        out_shape = jax.ShapeDtypeStruct((N_CHIPS, m, tn), out_dtype)
    else:
        out_spec = pl.BlockSpec((tm, tn), lambda i, j, kk: (i, j))
        out_shape = jax.ShapeDtypeStruct((m, n), out_dtype)
    res, cres = carried_call(name, body, (m // tm, n // tn, nk), in_specs, [out_spec], [out_shape], [pltpu.VMEM((tm, tn), F32)],
                             ("parallel", "parallel", "arbitrary"), args, comm)
    return res[0] if comm is None else (res[0], cres)


def _cb(cb):
    return cb if callable(cb) else (lambda j, _c=cb: _c)


def tiled_call(name, fn, s, tm, ins, outs, ncol=1):
    nrow = s // tm
    in_specs, args, kinds = [], [], []
    for it in ins:
        kind = it[0]
        if kind == 'row':
            _, arr, w, cb = it
            in_specs.append(pl.BlockSpec((tm, w), lambda j, i, _c=_cb(cb): (i, _c(j))))
            args.append(arr)
            kinds.append(('row',))
        elif kind == 'halo':
            _, arr, w, cb, h = it
            r = tm // h
            nh = s // h
            in_specs.append(pl.BlockSpec((h, w), lambda j, i, _c=_cb(cb), _r=r: (jnp.maximum(i * _r - 1, 0), _c(j))))
            in_specs.append(pl.BlockSpec((tm, w), lambda j, i, _c=_cb(cb): (i, _c(j))))
            in_specs.append(pl.BlockSpec((h, w), lambda j, i, _c=_cb(cb), _r=r, _n=nh: (jnp.minimum((i + 1) * _r, _n - 1), _c(j))))
            args += [arr, arr, arr]
            kinds.append(('halo',))
        elif kind == 'full':
            arr = it[1]
            in_specs.append(pl.BlockSpec(arr.shape, lambda j, i, _n=arr.ndim: (0,) * _n))
            args.append(arr)
            kinds.append(('ref',))
        elif kind == 'col':
            _, arr, w, cb = it
            in_specs.append(pl.BlockSpec((arr.shape[0], w), lambda j, i, _c=_cb(cb): (0, _c(j))))
            args.append(arr)
            kinds.append(('ref',))
        elif kind == 'hm':
            arr = it[1]
            in_specs.append(pl.BlockSpec((arr.shape[0], tm, arr.shape[2]), lambda j, i: (0, i, 0)))
            args.append(arr)
            kinds.append(('ref',))
        else:
            raise ValueError(kind)
    out_specs, out_shapes, okinds = [], [], []
    for ot in outs:
        kind = ot[0]
        if kind == 'row':
            _, wt, w, cb, dt = ot
            out_specs.append(pl.BlockSpec((tm, w), lambda j, i, _c=_cb(cb): (i, _c(j))))
            out_shapes.append(jax.ShapeDtypeStruct((s, wt), dt))
            okinds.append('row')
        elif kind == 'hm':
            _, hh, d, dt = ot
            out_specs.append(pl.BlockSpec((hh, tm, d), lambda j, i: (0, i, 0)))
            out_shapes.append(jax.ShapeDtypeStruct((hh, s, d), dt))
            okinds.append('hm')
        elif kind == 'acc':
            shape = ot[1]
            out_specs.append(pl.BlockSpec(shape, lambda j, i, _n=len(shape): (0,) * _n))
            out_shapes.append(jax.ShapeDtypeStruct(shape, F32))
            okinds.append('acc')
        elif kind == 'colacc':
            _, r, wt, w, cb = ot
            out_specs.append(pl.BlockSpec((r, w), lambda j, i, _c=_cb(cb): (0, _c(j))))
            out_shapes.append(jax.ShapeDtypeStruct((r, wt), F32))
            okinds.append('colacc')
        else:
            raise ValueError(kind)
    n_in = len(in_specs)

    def body(*refs):
        j = pl.program_id(0)
        i = pl.program_id(1)
        in_refs, out_refs = refs[:n_in], refs[n_in:]
        items, p = [], 0
        for kd in kinds:
            if kd[0] == 'row':
                items.append(in_refs[p][...])
                p += 1
            elif kd[0] == 'halo':
                prev, cen, nxt = in_refs[p][...], in_refs[p + 1][...], in_refs[p + 2][...]
                prev = jnp.where(i == 0, jnp.zeros_like(prev), prev)
                nxt = jnp.where(i == nrow - 1, jnp.zeros_like(nxt), nxt)
                items.append(jnp.concatenate([prev, cen, nxt], axis=0))
                p += 3
            else:
                items.append(in_refs[p])
                p += 1
        for o_ref, kd in zip(out_refs, okinds):
            if kd == 'acc':
                @pl.when((i == 0) & (j == 0))
                def _(o_ref=o_ref):
                    o_ref[...] = jnp.zeros_like(o_ref)
            elif kd == 'colacc':
                @pl.when(i == 0)
                def _(o_ref=o_ref):
                    o_ref[...] = jnp.zeros_like(o_ref)
        fn(i, j, *items, *out_refs)

    res = pl.pallas_call(
        body, name=name, grid=(ncol, nrow), in_specs=in_specs, out_specs=out_specs, out_shape=out_shapes,
        compiler_params=_cparams(("arbitrary", "arbitrary")),
    )(*args)
    return res


def _sigmoid(x):
    return 1.0 / (1.0 + jnp.exp(-x))


def _ln_stats(x):
    mu = jnp.mean(x, axis=1, keepdims=True)
    xc = x - mu
    var = jnp.mean(xc * xc, axis=1, keepdims=True)
    r = lax.rsqrt(var + LN_EPS)
    return xc * r, r


def _ln_bwd(dy, xh, r, g):
    dxh = dy * g
    dx = r * (dxh - jnp.mean(dxh, axis=1, keepdims=True) - xh * jnp.mean(dxh * xh, axis=1, keepdims=True))
    return dx, jnp.sum(dy * xh, axis=0, keepdims=True), jnp.sum(dy, axis=0, keepdims=True)


def _gmean(v, gm):
    hi = v.astype(BF16)
    lo = (v - hi.astype(F32)).astype(BF16)
    return jnp.dot(hi, gm, preferred_element_type=F32) + jnp.dot(lo, gm, preferred_element_type=F32)


def _roll_l(x, sh):
    return pltpu.roll(x, sh % x.shape[1], 1)


def _rope(x, c, sa, sb, sh):
    return x * c + _roll_l(x, -sh) * sa + _roll_l(x, sh) * sb


def _rope_t(dy, c, sa, sb, sh):
    return dy * c + _roll_l(dy * sa, sh) + _roll_l(dy * sb, -sh)


def _roll_r(x, s):
    return pltpu.roll(x, (-s) % x.shape[0], 0)


_GELU_C = math.sqrt(2.0 / math.pi)


def _gelu(x):
    t = jnp.tanh(_GELU_C * (x + 0.044715 * x * x * x))
    return 0.5 * x * (1.0 + t), t


def _gelu_grad(x, t):
    return 0.5 * (1.0 + t) + 0.5 * x * (1.0 - t * t) * _GELU_C * (1.0 + 3.0 * 0.044715 * x * x)


def _dot_nt(a, b):
    return lax.dot_general(a, b, (((1,), (1,)), ((), ())), preferred_element_type=F32)


def _dot_tn(a, b):
    return lax.dot_general(a, b, (((0,), (0,)), ((), ())), preferred_element_type=F32)


def _rope_block(s, d):
    t = jnp.arange(s)
    row = (t // GRID_W).astype(F32)
    col = (t % GRID_W).astype(F32)
    half = d // 4
    inv = ROPE_THETA ** (-jnp.arange(half, dtype=F32) / half)
    z = jnp.zeros((s, half), F32)
    cs, sas, sbs = [], [], []
    for pos in (row, col):
        ang = pos[:, None] * inv[None, :]
        co, si = jnp.cos(ang), jnp.sin(ang)
        cs += [co, co]
        sas += [-si, z]
        sbs += [z, si]
    return tuple(jnp.concatenate(v, axis=1) for v in (cs, sas, sbs))


def _rope_tables(s):
    gqa = tuple(jnp.tile(a, (1, 4)) for a in _rope_block(s, HEAD_DIM))
    c32, sa32, sb32 = _rope_block(s, MLA_ROPE)

    def head128(c, fill):
        return jnp.concatenate([jnp.full((s, 64), fill, F32), c, jnp.zeros((s, 32), F32)], axis=1)

    mla = tuple(jnp.tile(a, (1, 4)) for a in (head128(c32, 1.0), head128(sa32, 0.0), head128(sb32, 0.0)))
    kr = (head128(c32, 0.0), head128(sa32, 0.0), head128(sb32, 0.0))
    return dict(gqa=gqa, mla=mla, kr=kr)


def _group_mean_matrix(w, g):
    idx = np.arange(w) // g
    return jnp.asarray((idx[:, None] == idx[None, :]).astype(np.float32) / g, dtype=BF16)


TM = 512


def ln_fwd(name, s, x, g, b, res=None, alpha=1.0):
    ins = [('row', x, D_MODEL, 0)] + ([('row', res, D_MODEL, 0)] if res is not None else []) + [('full', g), ('full', b)]

    def fn(i, j, *a):
        if res is not None:
            xv, rv, g_ref, b_ref, o_ref = a
            z = alpha * rv + xv
        else:
            xv, g_ref, b_ref, o_ref = a
            z = xv
        xh, _ = _ln_stats(z)
        o_ref[...] = xh * g_ref[...] + b_ref[...]

    return tiled_call(name, fn, s, TM, ins, [('row', D_MODEL, D_MODEL, 0, F32)])[0]


def ln_bwd(name, s, dy, x, g, res=None, alpha=1.0):
    ins = [('row', dy, D_MODEL, 0), ('row', x, D_MODEL, 0)]
    ins += ([('row', res, D_MODEL, 0)] if res is not None else []) + [('full', g)]

    def fn(i, j, *a):
        a = list(a)
        dyv = a.pop(0)
        xv = a.pop(0)
        z = alpha * a.pop(0) + xv if res is not None else xv
        g_ref, dz_ref, dg_ref, db_ref = a
        xh, r = _ln_stats(z)
        dz, dg, db = _ln_bwd(dyv, xh, r, g_ref[...])
        dz_ref[...] = dz
        dg_ref[...] += dg
        db_ref[...] += db

    return tiled_call(name, fn, s, TM, ins, [('row', D_MODEL, D_MODEL, 0, F32), ('acc', (1, D_MODEL)), ('acc', (1, D_MODEL))])


def loss_and_ln_bwd(name, s, x, res, g, b, tgt, alpha):
    ins = [('row', x, D_MODEL, 0), ('row', res, D_MODEL, 0), ('row', tgt, D_MODEL, 0), ('full', g), ('full', b)]

    def fn(i, j, xv, rv, tv, g_ref, b_ref, dz_ref, dg_ref, db_ref, loss_ref):
        xh, r = _ln_stats(alpha * rv + xv)
        y = xh * g_ref[...] + b_ref[...]
        e = y - tv
        dz, dg, db = _ln_bwd(e * (1.0 / D_MODEL), xh, r, g_ref[...])
        dz_ref[...] = dz
        dg_ref[...] += dg
        db_ref[...] += db
        loss_ref[...] += jnp.sum(jnp.sum(e * e, axis=0, keepdims=True), axis=1, keepdims=True) * (0.5 / D_MODEL)

    return tiled_call(name, fn, s, TM, ins, [('row', D_MODEL, D_MODEL, 0, F32), ('acc', (1, D_MODEL)), ('acc', (1, D_MODEL)),
                                             ('acc', (1, 128))])


HALO_A = 16


def _glu(a_in):
    return a_in[:, :GROUP_W] * _sigmoid(a_in[:, GROUP_W:])


def a_fwd(name, s, proj, cw, cb, g, b):
    def fn(i, j, a_ext, cw_ref, cb_ref, g_ref, b_ref, y_ref, o_ref):
        a = _glu(a_ext)
        acc = jnp.zeros((TM, GROUP_W), F32)
        for k in range(CONV_A_WIDTH):
            acc = acc + _roll_r(a, k - 15)[HALO_A:HALO_A + TM] * cw_ref[pl.ds(k, 1), :]
        y = acc + cb_ref[...]
        y_ref[...] = y
        xh, _ = _ln_stats(y)
        z = xh * g_ref[...] + b_ref[...]
        o_ref[...] = z * _sigmoid(z)

    ins = [('halo', proj, 2 * GROUP_W, 0, HALO_A), ('full', cw), ('full', cb), ('full', g), ('full', b)]
    return tiled_call(name, fn, s, TM, ins, [('row', GROUP_W, GROUP_W, 0, F32), ('row', GROUP_W, GROUP_W, 0, F32)])


def a_bwd1(name, s, y, dmix, g, b):
    def fn(i, j, yv, do, g_ref, b_ref, dy_ref, dg_ref, db_ref, dcb_ref):
        xh, r = _ln_stats(yv)
        z = xh * g_ref[...] + b_ref[...]
        sg = _sigmoid(z)
        dz = do * sg * (1.0 + z * (1.0 - sg))
        dy, dg, db = _ln_bwd(dz, xh, r, g_ref[...])
        dy_ref[...] = dy
        dg_ref[...] += dg
        db_ref[...] += db
        dcb_ref[...] += jnp.sum(dy, axis=0, keepdims=True)

    ins = [('row', y, GROUP_W, 0), ('row', dmix, GROUP_W, 0), ('full', g), ('full', b)]
    return tiled_call(name, fn, s, TM, ins, [('row', GROUP_W, GROUP_W, 0, F32), ('acc', (1, GROUP_W)), ('acc', (1, GROUP_W)),
                                             ('acc', (1, GROUP_W))])


def a_bwd2(name, s, proj, dy, cw):
    def fn(i, j, a_ext, dy_ext, cw_ref, da_ref, dcw_ref):
        a1, a2 = a_ext[:, :GROUP_W], a_ext[:, GROUP_W:]
        sg = _sigmoid(a2)
        a = a1 * sg
        dyc = dy_ext[HALO_A:HALO_A + TM]
        da = jnp.zeros((TM, GROUP_W), F32)
        for k in range(CONV_A_WIDTH):
            da = da + _roll_r(dy_ext, 15 - k)[HALO_A:HALO_A + TM] * cw_ref[pl.ds(k, 1), :]
            dcw_ref[pl.ds(k, 1), :] += jnp.sum(dyc * _roll_r(a, k - 15)[HALO_A:HALO_A + TM], axis=0, keepdims=True)
        a1c, sgc = a1[HALO_A:HALO_A + TM], sg[HALO_A:HALO_A + TM]
        da_ref[...] = jnp.concatenate([da * sgc, da * a1c * sgc * (1.0 - sgc)], axis=1).astype(BF16)

    ins = [('halo', proj, 2 * GROUP_W, 0, HALO_A), ('halo', dy, GROUP_W, 0, HALO_A), ('full', cw)]
    return tiled_call(name, fn, s, TM, ins, [('row', 2 * GROUP_W, 2 * GROUP_W, 0, BF16), ('acc', (32, GROUP_W))])


def b_prep_fwd(name, s, proj, gq, gk, tabs, gm):
    c, sa, sb = tabs

    def fn(i, j, qkv, cv, sav, sbv, gq_ref, gk_ref, gm_ref, q_ref, k_ref, v_ref):
        q, k, v = qkv[:, :256], qkv[:, 256:384], qkv[:, 384:]
        gmv = gm_ref[...]
        qn = q * lax.rsqrt(_gmean(q * q, gmv) + RMS_EPS) * gq_ref[...]
        kn = k * lax.rsqrt(_gmean(k * k, gmv[:128, :128]) + RMS_EPS) * gk_ref[...]
        qr = _rope(qn, cv, sav, sbv, 16)
        kr = _rope(kn, cv[:, :128], sav[:, :128], sbv[:, :128], 16)
        for h in range(4):
            q_ref[h] = (qr[:, h * 64:(h + 1) * 64] * QSCALE_GQA).astype(BF16)
        for h in range(2):
            k_ref[h] = kr[:, h * 64:(h + 1) * 64].astype(BF16)
            v_ref[h] = v[:, h * 64:(h + 1) * 64].astype(BF16)

    ins = [('row', proj, 512, 1), ('row', c, 256, 0), ('row', sa, 256, 0), ('row', sb, 256, 0), ('full', gq), ('full', gk),
           ('full', gm)]
    return tiled_call(name, fn, s, TM, ins, [('hm', 4, 64, BF16), ('hm', 2, 64, BF16), ('hm', 2, 64, BF16)])


def b_prep_bwd(name, s, proj, gq, gk, tabs, gm, dq_hm, dk_hm, dv_hm):
    c, sa, sb = tabs

    def fn(i, j, qkv, cv, sav, sbv, gq_ref, gk_ref, gm_ref, dq_ref, dk_ref, dv_ref, dp_ref, dgq_ref, dgk_ref):
        q, k = qkv[:, :256], qkv[:, 256:384]
        gmv = gm_ref[...]
        dqr = jnp.concatenate([dq_ref[h] for h in range(4)], axis=1) * QSCALE_GQA
        dkr = jnp.concatenate([dk_ref[h] for h in range(2)], axis=1)
        dv = jnp.concatenate([dv_ref[h] for h in range(2)], axis=1)
        outs = []
        for x, dxr, g_ref, gmx, w, dg_ref in ((q, dqr, gq_ref, gmv, 256, dgq_ref), (k, dkr, gk_ref, gmv[:128, :128], 128, dgk_ref)):
            dn = _rope_t(dxr, cv[:, :w], sav[:, :w], sbv[:, :w], 16)
            r = lax.rsqrt(_gmean(x * x, gmx) + RMS_EPS)
            gv = g_ref[...]
            dx = gv * r * dn - x * (r * r * r) * _gmean(x * gv * dn, gmx)
            dgt = jnp.sum(x * r * dn, axis=0, keepdims=True)
            dg = dgt[:, 0:64]
            for h in range(1, w // 64):
                dg = dg + dgt[:, h * 64:(h + 1) * 64]
            dg_ref[...] += dg
            outs.append(dx)
        dp_ref[...] = jnp.concatenate(outs + [dv], axis=1).astype(BF16)

    ins = [('row', proj, 512, 1), ('row', c, 256, 0), ('row', sa, 256, 0), ('row', sb, 256, 0), ('full', gq), ('full', gk),
           ('full', gm), ('hm', dq_hm), ('hm', dk_hm), ('hm', dv_hm)]
    return tiled_call(name, fn, s, TM, ins, [('row', 512, 512, 0, BF16), ('acc', (1, 64)), ('acc', (1, 64))])


TQ = 512
TQ_FWD = 256


LOG2E = 1.4426950408889634
QSCALE_GQA = HEAD_DIM ** -0.5 * LOG2E
QSCALE_MLA = (MLA_NOPE + MLA_ROPE) ** -0.5 * LOG2E


def attn_fwd(name, s, q_hm, k_hm, v_hm, comm=None):
    dk = q_hm.shape[2]
    kpb = k_hm.shape[0] // 2
    tq = TQ_FWD if s % TQ_FWD == 0 else TQ

    def body(q_ref, k_ref, v_ref, o_ref, l_ref):
        outs = []
        for g in range(2):
            kv = g if kpb == 2 else 0
            sc = _dot_nt(q_ref[g], k_ref[kv])
            m = jnp.max(sc, axis=1, keepdims=True)
            p = jnp.exp2(sc - m)
            l = jnp.sum(p, axis=1, keepdims=True)
            o = jnp.dot(p.astype(BF16), v_ref[kv], preferred_element_type=F32)
            outs.append(o / l)
            l_ref[g] = m + jnp.log2(l)
        o_ref[...] = jnp.concatenate(outs, axis=1)

    res, cres = carried_call(
        name, body, (2, s // tq),
        [pl.BlockSpec((2, tq, dk), lambda p, i: (p, i, 0)),
         pl.BlockSpec((kpb, s, dk), lambda p, i: (p, 0, 0)),
         pl.BlockSpec((kpb, s, 64), lambda p, i: (p, 0, 0))],
        [pl.BlockSpec((tq, 128), lambda p, i: (i, p)), pl.BlockSpec((2, tq, 1), lambda p, i: (p, i, 0))],
        [jax.ShapeDtypeStruct((s, 256), F32), jax.ShapeDtypeStruct((4, s, 1), F32)], [],
        ("arbitrary", "arbitrary"), [q_hm, k_hm, v_hm], comm)
    return res if comm is None else (res, cres)


def attn_bwd(name, s, q_hm, k_hm, v_hm, o, lse, dmixcat, dcol, comm=None):
    dk = q_hm.shape[2]
    nkv = k_hm.shape[0]
    shared = nkv == 2
    kvi = (lambda p, g, i: (p, 0, 0)) if shared else (lambda p, g, i: (2 * p + g, 0, 0))

    def body(q_ref, k_ref, v_ref, o_ref, l_ref, do_ref, dq_ref, dk_ref, dv_ref):
        g = pl.program_id(1)
        i = pl.program_id(2)
        first = (i == 0) & (g == 0) if shared else (i == 0)

        @pl.when(first)
        def _():
            dk_ref[...] = jnp.zeros_like(dk_ref)
            dv_ref[...] = jnp.zeros_like(dv_ref)

        dob, ob = do_ref[...], o_ref[...]
        do = jnp.where(g == 0, dob[:, :64], dob[:, 64:])
        ov = jnp.where(g == 0, ob[:, :64], ob[:, 64:])
        q, k, v = q_ref[0], k_ref[0], v_ref[0]
        p = jnp.exp2(_dot_nt(q, k) - l_ref[0])
        dp = _dot_nt(do.astype(BF16), v)
        delta = jnp.sum(do * ov, axis=1, keepdims=True)
        ds = (p * (dp - delta) * (1.0 / LOG2E)).astype(BF16)
        dq_ref[0] = jnp.dot(ds, k, preferred_element_type=F32)
        dk_ref[0] += _dot_tn(ds, q)
        dv_ref[0] += _dot_tn(p.astype(BF16), do.astype(BF16))

    res, cres = carried_call(
        name, body, (2, 2, s // TQ),
        [pl.BlockSpec((1, TQ, dk), lambda p, g, i: (2 * p + g, i, 0)),
         pl.BlockSpec((1, s, dk), kvi), pl.BlockSpec((1, s, 64), kvi),
         pl.BlockSpec((TQ, 128), lambda p, g, i: (i, p)),
         pl.BlockSpec((1, TQ, 1), lambda p, g, i: (2 * p + g, i, 0)),
         pl.BlockSpec((TQ, 128), lambda p, g, i: (i, dcol + p))],
        [pl.BlockSpec((1, TQ, dk), lambda p, g, i: (2 * p + g, i, 0)),
         pl.BlockSpec((1, s, dk), kvi), pl.BlockSpec((1, s, 64), kvi)],
        [jax.ShapeDtypeStruct((4, s, dk), F32), jax.ShapeDtypeStruct((nkv, s, dk), F32), jax.ShapeDtypeStruct((nkv, s, 64), F32)],
        [], ("arbitrary", "arbitrary", "arbitrary"), [q_hm, k_hm, v_hm, o, lse, dmixcat], comm)
    return res if comm is None else (res, cres)


def _lane_group_masks(w, g):
    lane = lax.broadcasted_iota(jnp.int32, (1, w), 1)
    return [((lane >= k * g) & (lane < (k + 1) * g)).astype(F32) for k in range(w // g)]


def _sgu_gate(svn, w_ref, bias):
    masks = _lane_group_masks(GROUP_W, 64)
    outs = []
    for n in range(TM // CHUNK):
        x = svn[n * CHUNK:(n + 1) * CHUNK]
        acc = bias
        for g in range(4):
            acc = acc + jnp.dot(w_ref[g].astype(BF16), (x * masks[g]).astype(BF16), preferred_element_type=F32)
        outs.append(acc)
    return jnp.concatenate(outs, axis=0)


def c_fwd(name, s, proj, g, b, w, bias):
    def fn(i, j, cin, g_ref, b_ref, w_ref, bias_ref, o_ref):
        c, _ = _gelu(cin)
        xh, _ = _ln_stats(c[:, GROUP_W:])
        svn = xh * g_ref[...] + b_ref[...]
        o_ref[...] = c[:, :GROUP_W] * _sgu_gate(svn, w_ref, bias_ref[...])

    ins = [('row', proj, 512, 2), ('full', g), ('full', b), ('full', w), ('full', bias)]
    return tiled_call(name, fn, s, TM, ins, [('row', GROUP_W, GROUP_W, 0, F32)])[0]


def c_bwd(name, s, proj, g, b, w, wt, bias, dmixcat):
    def fn(i, j, cin, do, g_ref, b_ref, w_ref, wt_ref, bias_ref, dc_ref, dg_ref, db_ref, dw_ref, dbias_ref):
        c, t = _gelu(cin)
        u = c[:, :GROUP_W]
        xh, r = _ln_stats(c[:, GROUP_W:])
        svn = xh * g_ref[...] + b_ref[...]
        gate = _sgu_gate(svn, w_ref, bias_ref[...])
        du = do * gate
        dgate = do * u
        masks = _lane_group_masks(GROUP_W, 64)
        col = lax.broadcasted_iota(jnp.int32, (CHUNK, 128), 1)
        dsvn, dbias = [], jnp.zeros((CHUNK, 128), F32)
        for n in range(TM // CHUNK):
            dgc = dgate[n * CHUNK:(n + 1) * CHUNK]
            x = svn[n * CHUNK:(n + 1) * CHUNK]
            acc = jnp.zeros((CHUNK, GROUP_W), F32)
            for gi in range(4):
                dgm = (dgc * masks[gi]).astype(BF16)
                acc = acc + jnp.dot(wt_ref[gi].astype(BF16), dgm, preferred_element_type=F32)
                dw_ref[gi] += _dot_nt(dgm, (x * masks[gi]).astype(BF16))
                dbias = dbias + jnp.where(col == gi, jnp.sum(dgc * masks[gi], axis=1, keepdims=True), 0.0)
            dsvn.append(acc)
        dbias_ref[...] += dbias
        dsv, dg, db = _ln_bwd(jnp.concatenate(dsvn, axis=0), xh, r, g_ref[...])
        dg_ref[...] += dg
        db_ref[...] += db
        dc_ref[...] = (jnp.concatenate([du, dsv], axis=1) * _gelu_grad(cin, t)).astype(BF16)

    ins = [('row', proj, 512, 2), ('row', dmixcat, GROUP_W, 2), ('full', g), ('full', b), ('full', w), ('full', wt), ('full', bias)]
    return tiled_call(name, fn, s, TM, ins, [('row', 512, 512, 0, BF16), ('acc', (1, GROUP_W)), ('acc', (1, GROUP_W)),
                                             ('acc', (4, CHUNK, CHUNK)), ('acc', (CHUNK, 128))])


def _d_common(dpart, gq_ref, gkv_ref):
    cq, ckv, kr = dpart[:, :256], dpart[:, 256:384], dpart[:, 384:]
    rq = lax.rsqrt(jnp.sum(cq * cq, axis=1, keepdims=True) * (1.0 / MLA_Q_LORA) + RMS_EPS)
    rkv = lax.rsqrt(jnp.mean(ckv * ckv, axis=1, keepdims=True) + RMS_EPS)
    return cq, ckv, kr, rq, rkv, cq * rq * gq_ref[...], ckv * rkv * gkv_ref[...]


def d_prep_fwd(name, s, proj, gq, gkv, wuq, wukv, tm_, tkr):
    cm, sam, sbm = tm_
    ck, sak, sbk = tkr

    def fn(i, j, dpart, cmv, samv, sbmv, ckv_, sakv, sbkv, gq_ref, gkv_ref, wuq_ref, wukv_ref, q_ref, k_ref, v_ref):
        cq, ckv, kr, rq, rkv, cqn, ckvn = _d_common(dpart, gq_ref, gkv_ref)
        qd = jnp.dot(cqn.astype(BF16), wuq_ref[...], preferred_element_type=F32)
        qf = _rope(qd, cmv, samv, sbmv, 8)
        kvd = jnp.dot(ckvn.astype(BF16), wukv_ref[...], preferred_element_type=F32)
        krr = _rope(kr, ckv_, sakv, sbkv, 8)
        for h in range(4):
            q_ref[h] = (qf[:, h * 128:(h + 1) * 128] * QSCALE_MLA).astype(BF16)
            k_ref[h] = (kvd[:, h * 128:(h + 1) * 128] + krr).astype(BF16)
            v_ref[h] = kvd[:, 512 + h * 64:512 + (h + 1) * 64].astype(BF16)

    ins = [('row', proj, 512, 3), ('row', cm, 512, 0), ('row', sam, 512, 0), ('row', sbm, 512, 0), ('row', ck, 128, 0),
           ('row', sak, 128, 0), ('row', sbk, 128, 0), ('full', gq), ('full', gkv), ('full', wuq), ('full', wukv)]
    return tiled_call(name, fn, s, TM, ins, [('hm', 4, 128, BF16), ('hm', 4, 128, BF16), ('hm', 4, 64, BF16)])


def d_prep_bwd(name, s, proj, gq, gkv, wuq, wukv, tm_, tkr, dq_hm, dk_hm, dv_hm):
    cm, sam, sbm = tm_
    ck, sak, sbk = tkr

    def fn(i, j, dpart, cmv, samv, sbmv, ckv_, sakv, sbkv, gq_ref, gkv_ref, wuq_ref, wukv_ref, dq_ref, dk_ref, dv_ref,
           dd_ref, dgq_ref, dgkv_ref, dwuq_ref, dwukv_ref):
        cq, ckv, kr, rq, rkv, cqn, ckvn = _d_common(dpart, gq_ref, gkv_ref)
        dqf = jnp.concatenate([dq_ref[h] for h in range(4)], axis=1) * QSCALE_MLA
        dkf = [dk_ref[h] for h in range(4)]
        dqd = _rope_t(dqf, cmv, samv, sbmv, 8).astype(BF16)
        dkvd = jnp.concatenate(dkf + [dv_ref[h] for h in range(4)], axis=1).astype(BF16)
        dkr = _rope_t(dkf[0] + dkf[1] + dkf[2] + dkf[3], ckv_, sakv, sbkv, 8)
        dcqn = _dot_nt(dqd, wuq_ref[...])
        dckvn = _dot_nt(dkvd, wukv_ref[...])
        dwuq_ref[...] += _dot_tn(cqn.astype(BF16), dqd)
        dwukv_ref[...] += _dot_tn(ckvn.astype(BF16), dkvd)
        gqv, gkvv = gq_ref[...], gkv_ref[...]
        dcq = gqv * rq * dcqn - cq * (rq * rq * rq) * (jnp.sum(cq * gqv * dcqn, axis=1, keepdims=True) * (1.0 / MLA_Q_LORA))
        dckv = gkvv * rkv * dckvn - ckv * (rkv * rkv * rkv) * jnp.mean(ckv * gkvv * dckvn, axis=1, keepdims=True)
        dgq_ref[...] += jnp.sum(cq * rq * dcqn, axis=0, keepdims=True)
        dgkv_ref[...] += jnp.sum(ckv * rkv * dckvn, axis=0, keepdims=True)
        dd_ref[...] = jnp.concatenate([dcq, dckv, dkr], axis=1).astype(BF16)

    ins = [('row', proj, 512, 3), ('row', cm, 512, 0), ('row', sam, 512, 0), ('row', sbm, 512, 0), ('row', ck, 128, 0),
           ('row', sak, 128, 0), ('row', sbk, 128, 0), ('full', gq), ('full', gkv), ('full', wuq), ('full', wukv),
           ('hm', dq_hm), ('hm', dk_hm), ('hm', dv_hm)]
    return tiled_call(name, fn, s, TM, ins, [('row', 512, 512, 0, BF16), ('acc', (1, 256)), ('acc', (1, 128)),
                                             ('acc', (256, 512)), ('acc', (128, 768))])


HALO_F = 8
TN_F = 256
TM_F = 512


def _conv3(ext, w_ref):
    return _roll_r(ext, -1) * w_ref[pl.ds(0, 1), :] + ext * w_ref[pl.ds(1, 1), :] + _roll_r(ext, 1) * w_ref[pl.ds(2, 1), :]


def f_fwd(name, s, upraw, cw, cb):
    ncj = D_FF // TN_F

    def fn(i, j, xa, xg, wa_ref, wg_ref, ba_ref, bg_ref, o_ref):
        ua = (_conv3(xa, wa_ref) + ba_ref[...])[HALO_F:HALO_F + TM_F]
        ug = (_conv3(xg, wg_ref) + bg_ref[...])[HALO_F:HALO_F + TM_F]
        o_ref[...] = (ua * _sigmoid(ua) * ug).astype(BF16)

    ins = [('halo', upraw, TN_F, lambda j: j, HALO_F), ('halo', upraw, TN_F, lambda j: j + ncj, HALO_F),
           ('col', cw, TN_F, lambda j: j), ('col', cw, TN_F, lambda j: j + ncj),
           ('col', cb, TN_F, lambda j: j), ('col', cb, TN_F, lambda j: j + ncj)]
    return tiled_call(name, fn, s, TM_F, ins, [('row', D_FF, TN_F, lambda j: j, BF16)], ncol=ncj)[0]


def f_bwd(name, s, upraw, dact, cw, cb):
    ncj = D_FF // TN_F

    def fn(i, j, xa, xg, da, wa_ref, wg_ref, ba_ref, bg_ref, dxa_ref, dxg_ref, dwa_ref, dwg_ref, dba_ref, dbg_ref):
        taps_a = (_roll_r(xa, -1), xa, _roll_r(xa, 1))
        taps_g = (_roll_r(xg, -1), xg, _roll_r(xg, 1))
        ua = sum(t * wa_ref[pl.ds(k, 1), :] for k, t in enumerate(taps_a)) + ba_ref[...]
        ug = sum(t * wg_ref[pl.ds(k, 1), :] for k, t in enumerate(taps_g)) + bg_ref[...]
        sg = _sigmoid(ua)
        dug = da * ua * sg
        dua = da * ug * sg * (1.0 + ua * (1.0 - sg))
        cen = slice(HALO_F, HALO_F + TM_F)
        for du, taps, w_ref, dx_ref, dw_ref, db_ref in ((dua, taps_a, wa_ref, dxa_ref, dwa_ref, dba_ref),
                                                         (dug, taps_g, wg_ref, dxg_ref, dwg_ref, dbg_ref)):
            dx = _roll_r(du, 1) * w_ref[pl.ds(0, 1), :] + du * w_ref[pl.ds(1, 1), :] + _roll_r(du, -1) * w_ref[pl.ds(2, 1), :]
            dx_ref[...] = dx[cen].astype(BF16)
            duc = du[cen]
            for k in range(3):
                dw_ref[pl.ds(k, 1), :] += jnp.sum(duc * taps[k][cen], axis=0, keepdims=True)
            db_ref[...] += jnp.sum(duc, axis=0, keepdims=True)

    ins = [('halo', upraw, TN_F, lambda j: j, HALO_F), ('halo', upraw, TN_F, lambda j: j + ncj, HALO_F),
           ('halo', dact, TN_F, lambda j: j, HALO_F),
           ('col', cw, TN_F, lambda j: j), ('col', cw, TN_F, lambda j: j + ncj),
           ('col', cb, TN_F, lambda j: j), ('col', cb, TN_F, lambda j: j + ncj)]
    outs = [('row', D_FF, TN_F, lambda j: j, BF16), ('row', D_FF, TN_F, lambda j: j, BF16),
            ('colacc', 8, D_FF, TN_F, lambda j: j), ('colacc', 8, D_FF, TN_F, lambda j: j),
            ('colacc', 1, D_FF, TN_F, lambda j: j), ('colacc', 1, D_FF, TN_F, lambda j: j)]
    return tiled_call(name, fn, s, TM_F, ins, outs, ncol=ncj)


def _pad_w_in(w):
    z = lambda n: jnp.zeros((w.shape[0], n), w.dtype)
    return jnp.concatenate([w[:, :1728], z(64), w[:, 1728:1856], z(64), w[:, 1856:1888], z(32)], axis=1)


def _unpad_w_in(g):
    return jnp.concatenate([g[:, :1728], g[:, 1792:1920], g[:, 1984:2016]], axis=1)


def _pad_uq(w):
    w = w.reshape(MLA_Q_LORA, 4, 96)
    w = jnp.concatenate([w, jnp.zeros((MLA_Q_LORA, 4, 32), w.dtype)], axis=2).reshape(MLA_Q_LORA, 512)
    return jnp.concatenate([w, jnp.zeros((64, 512), w.dtype)], axis=0)


def _unpad_uq(g):
    return g[:MLA_Q_LORA].reshape(MLA_Q_LORA, 4, 128)[:, :, :96].reshape(MLA_Q_LORA, 384)


def _pad_ukv(w):
    w = w.reshape(MLA_KV_LORA, 4, 128)
    kn = jnp.concatenate([w[:, :, :64], jnp.zeros((MLA_KV_LORA, 4, 64), w.dtype)], axis=2).reshape(MLA_KV_LORA, 512)
    return jnp.concatenate([kn, w[:, :, 64:].reshape(MLA_KV_LORA, 256)], axis=1)


def _unpad_ukv(g):
    kn = g[:, :512].reshape(MLA_KV_LORA, 4, 128)[:, :, :64]
    v = g[:, 512:].reshape(MLA_KV_LORA, 4, 64)
    return jnp.concatenate([kn, v], axis=2).reshape(MLA_KV_LORA, 512)


def _row(v, pad_to=None):
    v = v.reshape(1, -1)
    if pad_to is not None and v.shape[1] < pad_to:
        v = jnp.concatenate([v, jnp.zeros((1, pad_to - v.shape[1]), v.dtype)], axis=1)
    return v


LATE_NAMES = ['w_out', 'ffn_w_up', 'ffn_w_down']


def _assemble(n, part, g, me):
    return jnp.concatenate([jnp.where(me == q, part, g[q]) for q in range(N_CHIPS)], axis=SHARDED_AXIS[n])


def local_step(x, tgt, W, dist=None):
    s = x.shape[0]
    tabs = _rope_tables(s)
    gm = _group_mean_matrix(256, 64)
    alpha = DEEPNORM_ALPHA
    acts = []
    late = {(n, l): W[n][l] for n in LATE_NAMES for l in range(DEPTH)} if dist is None else {}
    h = ln_fwd("ln_in", s, x, _row(W['ln_in_g']), _row(W['ln_in_b']))
    for l in range(DEPTH):
        A = dict(h=h)
        A['wpad'] = _pad_w_in(W['w_in'][l]).astype(BF16)
        A['wuq'] = _pad_uq(W['mla_w_uq'][l]).astype(BF16)
        A['wukv'] = _pad_ukv(W['mla_w_ukv'][l]).astype(BF16)
        A['cw'] = jnp.concatenate([W['conv_a_w'][l], jnp.zeros((1, GROUP_W), F32)], axis=0)
        A['fcw'] = jnp.concatenate([W['ffn_conv_w'][l], jnp.zeros((5, 2 * D_FF), F32)], axis=0)
        A['gq'] = jnp.tile(_row(W['qk_norm_q'][l]), (1, 4))
        A['gk'] = jnp.tile(_row(W['qk_norm_k'][l]), (1, 2))
        A['sgu_bias'] = jnp.repeat(W['sgu_b'][l].T, 64, axis=1)
        A['sgu_wt'] = jnp.swapaxes(W['sgu_w'][l], 1, 2)
        proj = mm(f"proj{l}", h, A['wpad'])
        A['proj'] = proj
        A['y_a'], o_a = a_fwd(f"a_fwd{l}", s, proj, A['cw'], _row(W['conv_a_b'][l]), _row(W['ln_a_g'][l]), _row(W['ln_a_b'][l]))
        A['bq'], A['bk'], A['bv'] = b_prep_fwd(f"b_prep{l}", s, proj, A['gq'], A['gk'], tabs['gqa'], gm)
        if dist is not None and l == 0:
            lp = dist['late']
            parts = [lp['w_out'], lp['ffn_w_up'][0:1], lp['ffn_w_down'][0:1]]
            nxt_names, nxt = dist['next']
            nxt_parts = [a[1:2] for a in nxt]
            wanted = [lp['w_out'], (lp['ffn_w_up'], 0), (lp['ffn_w_down'], 0)] + [(a, 1) for a in nxt]
            (A['o_b'], A['lse_b']), got = attn_fwd(f"b_attn{l}", s, A['bq'], A['bk'], A['bv'], comm=gather_over_chips(wanted))
            full = [_assemble(n, p, g, dist['me']) for n, p, g in zip(LATE_NAMES, parts, got)]
            late[('w_out', 0)], late[('w_out', 1)] = full[0][0], full[0][1]
            late[('ffn_w_up', 0)], late[('ffn_w_down', 0)] = full[1][0], full[2][0]
            for n, p, g in zip(nxt_names, nxt_parts, got[len(parts):]):
                W[n][1] = _assemble(n, p, g, dist['me'])[0]
        else:
            A['o_b'], A['lse_b'] = attn_fwd(f"b_attn{l}", s, A['bq'], A['bk'], A['bv'])
        o_c = c_fwd(f"c_fwd{l}", s, proj, _row(W['sgu_ln_g'][l]), _row(W['sgu_ln_b'][l]), W['sgu_w'][l], A['sgu_bias'])
        A['dq'], A['dk'], A['dv'] = d_prep_fwd(f"d_prep{l}", s, proj, _row(W['mla_q_norm'][l], 256), _row(W['mla_kv_norm'][l]),
                                               A['wuq'], A['wukv'], tabs['mla'], tabs['kr'])
        if dist is not None and l == 0:
            parts = [lp['ffn_w_up'][1:2], lp['ffn_w_down'][1:2]]
            (A['o_d'], A['lse_d']), got = attn_fwd(f"d_attn{l}", s, A['dq'], A['dk'], A['dv'],
                                                   comm=gather_over_chips([(lp['ffn_w_up'], 1), (lp['ffn_w_down'], 1)]))
            late[('ffn_w_up', 1)] = _assemble('ffn_w_up', parts[0], got[0], dist['me'])[0]
            late[('ffn_w_down', 1)] = _assemble('ffn_w_down', parts[1], got[1], dist['me'])[0]
        else:
            A['o_d'], A['lse_d'] = attn_fwd(f"d_attn{l}", s, A['dq'], A['dk'], A['dv'])
        A['wout'] = late[('w_out', l)].astype(BF16)
        A['wup'] = late[('ffn_w_up', l)].astype(BF16)
        A['wdown'] = late[('ffn_w_down', l)].astype(BF16)
        A['mixcat'] = jnp.concatenate([o_a, A['o_b'], o_c, A['o_d']], axis=1).astype(BF16)
        A['mix'] = mm(f"out_proj{l}", A['mixcat'], A['wout'])
        h1 = ln_fwd(f"ln_mix{l}", s, A['mix'], _row(W['ln_mix_g'][l]), _row(W['ln_mix_b'][l]), res=h, alpha=alpha)
        A['h1'] = h1
        A['upraw'] = mm(f"ffn_up{l}", h1, A['wup'])
        A['act'] = f_fwd(f"f_fwd{l}", s, A['upraw'], A['fcw'], _row(W['ffn_conv_b'][l]))
        A['f'] = mm(f"ffn_down{l}", A['act'], A['wdown'])
        if l < DEPTH - 1:
            h = ln_fwd(f"ln_ffn{l}", s, A['f'], _row(W['ln_ffn_g'][l]), _row(W['ln_ffn_b'][l]), res=h1, alpha=alpha)
        acts.append(A)

    per_layer = {n: [None] * DEPTH for n in WEIGHT_NAMES if n not in ('ln_in_g', 'ln_in_b')}
    dh = None
    loss_row = None
    overlap = dist is not None and DEPTH == 2
    red = {}
    grp1 = [(n, 1) for n in BIG_NAMES]
    grp2 = [('ffn_w_up', 0), ('ffn_w_down', 0)]

    def carried(fn, name, *a, comm=None, **kw):
        if comm is None:
            return fn(name, *a, **kw), None
        return fn(name, *a, comm=comm, **kw)

    def pair_sums(tag, xs, r1):
        return [pair_sum(f"pair_sum_{tag}{t}", xv, rv, dist['idx'], BF16) for t, (xv, rv) in enumerate(zip(xs, r1))]

    def chip_sums(tag, cs, ps):
        return [chip_sum(f"chip_sum_{tag}{t}", [cv], [pv], dist['idx']) for t, (cv, pv) in enumerate(zip(cs, ps))]

    for l in reversed(range(DEPTH)):
        A = acts[l]
        carry = overlap and l == 0
        g_ffn, b_ffn = _row(W['ln_ffn_g'][l]), _row(W['ln_ffn_b'][l])
        if l == DEPTH - 1:
            dz, dg, db, loss_row = loss_and_ln_bwd("loss", s, A['f'], A['h1'], g_ffn, b_ffn, tgt, alpha)
        else:
            dz, dg, db = ln_bwd(f"ln_ffn_bwd{l}", s, dh, A['f'], g_ffn, res=A['h1'], alpha=alpha)
        per_layer['ln_ffn_g'][l], per_layer['ln_ffn_b'][l] = dg[0], db[0]
        xs1 = [per_layer[n][k] for n, k in grp1] if carry else None
        dwd, r1 = carried(mm, f"dw_down{l}", A['act'], dz, ta=True, comm=pair_exchange(xs1) if carry else None)
        per_layer['ffn_w_down'][l] = dwd.reshape(N_CHIPS, D_FF // N_CHIPS, D_MODEL)
        ps1 = pair_sums("g1_", xs1, r1) if carry else None
        dact = mm(f"d_act{l}", dz, A['wdown'], tb=True)
        dxa, dxg, dwa, dwg, dba, dbg = f_bwd(f"f_bwd{l}", s, A['upraw'], dact, A['fcw'], _row(W['ffn_conv_b'][l]))
        per_layer['ffn_conv_w'][l] = jnp.concatenate([dwa[:3], dwg[:3]], axis=1)
        per_layer['ffn_conv_b'][l] = jnp.concatenate([dba[0], dbg[0]], axis=0)
        dup = jnp.concatenate([dxa, dxg], axis=1)
        per_layer['ffn_w_up'][l] = mm(f"dw_up{l}", A['h1'], dup, ta=True, chip_major=True)
        xs2 = [per_layer[n][k] for n, k in grp2] if carry else None
        dh1_mm, r2 = carried(mm, f"d_h1{l}", dup, A['wup'], tb=True, comm=pair_exchange(xs2) if carry else None)
        ps2 = pair_sums("g2_", xs2, r2) if carry else None
        dz1, dg, db = _ln_mix_bwd(l, s, dz, dh1_mm, A, W, alpha)
        per_layer['ln_mix_g'][l], per_layer['ln_mix_b'][l] = dg[0], db[0]
        per_layer['w_out'][l] = mm(f"dw_out{l}", A['mixcat'], dz1, ta=True).reshape(N_CHIPS, D_MODEL // N_CHIPS, D_MODEL)
        dmixcat = mm(f"d_mixcat{l}", dz1, A['wout'], tb=True)
        dy_a, dg, db, dcb = a_bwd1(f"a_bwd1{l}", s, A['y_a'], dmixcat, _row(W['ln_a_g'][l]), _row(W['ln_a_b'][l]))
        per_layer['ln_a_g'][l], per_layer['ln_a_b'][l], per_layer['conv_a_b'][l] = dg[0], db[0], dcb[0]
        dp_a, dcw = a_bwd2(f"a_bwd2{l}", s, A['proj'], dy_a, A['cw'])
        per_layer['conv_a_w'][l] = dcw[:CONV_A_WIDTH]
        (dq, dk, dv), cs1 = carried(attn_bwd, f"b_attn_bwd{l}", s, A['bq'], A['bk'], A['bv'], A['o_b'], A['lse_b'], dmixcat, 2,
                                    comm=chip_exchange(ps1) if carry else None)
        halves1 = chip_sums("g1_", cs1, ps1) if carry else None
        dp_b, dgq, dgk = b_prep_bwd(f"b_prep_bwd{l}", s, A['proj'], A['gq'], A['gk'], tabs['gqa'], gm, dq, dk, dv)
        per_layer['qk_norm_q'][l], per_layer['qk_norm_k'][l] = dgq[0], dgk[0]
        dp_c, dg, db, dsw, dsb = c_bwd(f"c_bwd{l}", s, A['proj'], _row(W['sgu_ln_g'][l]), _row(W['sgu_ln_b'][l]), W['sgu_w'][l],
                                       A['sgu_wt'], A['sgu_bias'], dmixcat)
        per_layer['sgu_ln_g'][l], per_layer['sgu_ln_b'][l] = dg[0], db[0]
        per_layer['sgu_w'][l], per_layer['sgu_b'][l] = dsw, dsb[:, :4].T
        comms = [half_exchange(halves1), chip_exchange(ps2)] if carry else []
        (dq, dk, dv), cres = carried(attn_bwd, f"d_attn_bwd{l}", s, A['dq'], A['dk'], A['dv'], A['o_d'], A['lse_d'], dmixcat, 6,
                                     comm=join_comms(comms))
        if carry:
            done1, cs2 = split_comm_results(comms, cres)
            red.update(zip(grp1, done1))
            halves2 = chip_sums("g2_", cs2, ps2)
        dp_d, dgq, dgkv, dwuq, dwukv = d_prep_bwd(f"d_prep_bwd{l}", s, A['proj'], _row(W['mla_q_norm'][l], 256),
                                                  _row(W['mla_kv_norm'][l]), A['wuq'], A['wukv'], tabs['mla'], tabs['kr'],
                                                  dq, dk, dv)
        per_layer['mla_q_norm'][l], per_layer['mla_kv_norm'][l] = dgq[0, :MLA_Q_LORA], dgkv[0]
        per_layer['mla_w_uq'][l], per_layer['mla_w_ukv'][l] = _unpad_uq(dwuq), _unpad_ukv(dwukv)
        dproj = jnp.concatenate([dp_a, dp_b, dp_c, dp_d], axis=1)
        dw_in, done2 = carried(mm, f"dw_in{l}", A['h'], dproj, ta=True, comm=half_exchange(halves2) if carry else None)
        if carry:
            red.update(zip(grp2, done2))
        per_layer['w_in'][l] = _unpad_w_in(dw_in).reshape(D_MODEL, N_CHIPS, D_IN_PROJ // N_CHIPS).transpose(1, 0, 2)
        dh = mm(f"d_h{l}", dproj, A['wpad'], tb=True, add=dz1, add_scale=alpha)
    dx, dg, db = ln_bwd("ln_in_bwd", s, dh, x, _row(W['ln_in_g']))
    G = dict(per_layer)
    G['ln_in_g'], G['ln_in_b'] = dg[0], db[0]
    for n, k in red:
        G[n][k] = None
    G['reduced'] = red
    return loss_row, dx, G


def _ln_mix_bwd(l, s, dz, dh1_mm, A, W, alpha):
    ins = [('row', dz, D_MODEL, 0), ('row', dh1_mm, D_MODEL, 0), ('row', A['mix'], D_MODEL, 0), ('row', A['h'], D_MODEL, 0),
           ('full', _row(W['ln_mix_g'][l]))]

    def fn(i, j, dzv, dmv, xv, rv, g_ref, o_ref, dg_ref, db_ref):
        xh, r = _ln_stats(alpha * rv + xv)
        d, dg, db = _ln_bwd(alpha * dzv + dmv, xh, r, g_ref[...])
        o_ref[...] = d
        dg_ref[...] += dg
        db_ref[...] += db

    return tiled_call(f"ln_mix_bwd{l}", fn, s, TM, ins, [('row', D_MODEL, D_MODEL, 0, F32), ('acc', (1, D_MODEL)),
                                                         ('acc', (1, D_MODEL))])


ANY = pl.BlockSpec(memory_space=pl.ANY)


def _pos():
    return lax.axis_index("x"), lax.axis_index("y"), lax.axis_index("c")


def _rcopy(src, dst, ssem, rsem, dev):
    return pltpu.make_async_remote_copy(src_ref=src, dst_ref=dst, send_sem=ssem, recv_sem=rsem, device_id=dev,
                                        device_id_type=MESH)


class Comm:
    def __init__(self, operands, out_shape, sems, start, finish, aliases=None):
        self.operands, self.out_shape, self.sems = list(operands), list(out_shape), list(sems)
        self.start, self.finish, self.aliases = start, finish, dict(aliases or {})


def join_comms(comms):
    comms = [cm for cm in comms if cm is not None]
    if not comms:
        return None
    offs, oi, oo, os_ = [], 0, 0, 0
    for cm in comms:
        offs.append((oi, oo, os_))
        oi, oo, os_ = oi + len(cm.operands), oo + len(cm.out_shape), os_ + len(cm.sems)

    def part(fn_name):
        def run(ins, outs, sems):
            for cm, (a, b, s) in zip(comms, offs):
                getattr(cm, fn_name)(ins[a:a + len(cm.operands)], outs[b:b + len(cm.out_shape)], sems[s:s + len(cm.sems)])
        return run

    aliases = {a + i: b + o for cm, (a, b, s) in zip(comms, offs) for i, o in cm.aliases.items()}
    return Comm([v for cm in comms for v in cm.operands], [v for cm in comms for v in cm.out_shape],
                [v for cm in comms for v in cm.sems], part("start"), part("finish"), aliases)


def split_comm_results(comms, res):
    out, p = [], 0
    for cm in comms:
        if cm is None:
            out.append(None)
        else:
            out.append(list(res[p:p + len(cm.out_shape)]))
            p += len(cm.out_shape)
    return out


def run_comm(name, comm):
    n_in, n_out = len(comm.operands), len(comm.out_shape)

    def body(*refs):
        ins, outs, sems = refs[:n_in], refs[n_in:n_in + n_out], refs[n_in + n_out:]
        comm.start(ins, outs, sems)
        comm.finish(ins, outs, sems)

    return pl.pallas_call(body, name=name, in_specs=[ANY] * n_in, out_specs=[ANY] * n_out, out_shape=comm.out_shape,
                          scratch_shapes=[pltpu.SemaphoreType.DMA((k,)) for k in comm.sems],
                          input_output_aliases=comm.aliases)(*comm.operands)


def gather_over_chips(parts):
    n = len(parts)
    layer = [p[1] if isinstance(p, tuple) else None for p in parts]
    parts = [p[0] if isinstance(p, tuple) else p for p in parts]
    shapes = [p.shape if l is None else (1,) + p.shape[1:] for p, l in zip(parts, layer)]
    split = [shp[1] % 32 == 0 for shp in shapes]

    def geometry():
        x, y, c = _pos()
        return x, y, c, 2 * x + y, [(1 - x, y), (x, 1 - y), (1 - x, 1 - y)]

    def rows(t, half):
        a2 = shapes[t][1] // 2
        return pl.ds(half * a2, a2)

    def ici(srcs, dsts, sems, j, t, px, py, c, me):
        k = n * j + t
        src = srcs[t] if layer[t] is None else srcs[t].at[pl.ds(layer[t], 1)]
        if split[t]:
            return _rcopy(src.at[:, rows(t, c), :], dsts[t].at[me, :, rows(t, c), :], sems[0].at[k], sems[1].at[k], (px, py, c))
        return _rcopy(src, dsts[t].at[me], sems[0].at[k], sems[1].at[k], (px, py, c))

    def start(srcs, dsts, sems):
        x, y, c, me, chips = geometry()
        for j, (px, py) in enumerate(chips):
            for t in range(n):
                ici(srcs, dsts, sems, j, t, px, py, c, me).start()

    def finish(srcs, dsts, sems):
        x, y, c, me, chips = geometry()
        fwd = []
        for j, (px, py) in enumerate(chips):
            q = 2 * px + py
            for t in range(n):
                k = n * j + t
                if split[t]:
                    got = dsts[t].at[q, :, rows(t, c), :]
                    _rcopy(got, got, sems[0].at[k], sems[1].at[k], (px, py, c)).wait_recv()
                    fw = _rcopy(got, got, sems[2].at[k], sems[3].at[k], (x, y, 1 - c))
                    fw.start()
                    fwd.append(fw)
                else:
                    _rcopy(dsts[t].at[q], dsts[t].at[q], sems[0].at[k], sems[1].at[k], (px, py, c)).wait_recv()
        for j, (px, py) in enumerate(chips):
            q = 2 * px + py
            for t in range(n):
                if split[t]:
                    other = dsts[t].at[q, :, rows(t, 1 - c), :]
                    _rcopy(other, other, sems[2].at[n * j + t], sems[3].at[n * j + t], (x, y, 1 - c)).wait_recv()
                ici(srcs, dsts, sems, j, t, px, py, c, me).wait_send()
        for fw in fwd:
            fw.wait_send()

    return Comm(parts, [jax.ShapeDtypeStruct((N_CHIPS,) + shp, p.dtype) for p, shp in zip(parts, shapes)], [3 * n] * 4, start, finish)


def pair_exchange(xs):
    n = len(xs)

    def copies(srcs, dsts, sems):
        x, y, c = _pos()
        return [_rcopy(srcs[t].at[:, pl.ds((1 - c) * (xs[t].shape[1] // 2), xs[t].shape[1] // 2), :], dsts[t],
                       sems[0].at[t], sems[1].at[t], (x, y, 1 - c)) for t in range(n)]

    def start(srcs, dsts, sems):
        for cp in copies(srcs, dsts, sems):
            cp.start()

    def finish(srcs, dsts, sems):
        for cp in copies(srcs, dsts, sems):
            cp.wait()

    return Comm(xs, [jax.ShapeDtypeStruct((a.shape[0], a.shape[1] // 2, a.shape[2]), a.dtype) for a in xs], [n, n], start, finish)


def chip_exchange(ps):
    n = len(ps)

    def geometry():
        x, y, c = _pos()
        return c, 2 * x + y, [(1 - x, y), (x, 1 - y), (1 - x, 1 - y)]

    def start(srcs, dsts, sems):
        c, me, chips = geometry()
        for j, (px, py) in enumerate(chips):
            for t in range(n):
                _rcopy(srcs[t].at[2 * px + py], dsts[t].at[me], sems[0].at[n * j + t], sems[1].at[n * j + t], (px, py, c)).start()

    def finish(srcs, dsts, sems):
        c, me, chips = geometry()
        for j, (px, py) in enumerate(chips):
            for t in range(n):
                _rcopy(srcs[t].at[2 * px + py], dsts[t].at[2 * px + py], sems[0].at[n * j + t], sems[1].at[n * j + t], (px, py, c)).wait()

    return Comm(ps, [jax.ShapeDtypeStruct(p.shape, p.dtype) for p in ps], [3 * n, 3 * n], start, finish)


def half_exchange(bufs):
    n = len(bufs)

    def copies(ins, outs, sems):
        x, y, c = _pos()
        cps = []
        for t in range(n):
            a2 = bufs[t].shape[1] // 2
            mine = pl.ds(c * a2, a2)
            cps.append(_rcopy(ins[t].at[:, mine, :], outs[t].at[:, mine, :], sems[0].at[t], sems[1].at[t], (x, y, 1 - c)))
        return cps

    def start(ins, outs, sems):
        for cp in copies(ins, outs, sems):
            cp.start()

    def finish(ins, outs, sems):
        for cp in copies(ins, outs, sems):
            cp.wait()

    return Comm(bufs, [jax.ShapeDtypeStruct(b.shape, b.dtype) for b in bufs], [n, n], start, finish, {t: t for t in range(n)})


LANES = 1024
ROW_TILE_BYTES = 2 * 1024 * 1024


def _row_tile(r, row_bytes):
    if r * row_bytes <= ROW_TILE_BYTES:
        return r
    best = None
    for t in range(16, r, 16):
        if r % t == 0 and t * row_bytes <= ROW_TILE_BYTES:
            best = t
    assert best is not None, (r, row_bytes)
    return best


def pair_sum(name, x, r1, c_arr, out_dtype):
    q, a2, b = r1.shape
    tr = _row_tile(a2, b * 4)
    nb = a2 // tr

    def body(c_ref, x_ref, r_ref, o_ref):
        o_ref[...] = (x_ref[...] + r_ref[...]).astype(out_dtype)

    grid_spec = pltpu.PrefetchScalarGridSpec(
        num_scalar_prefetch=1, grid=(q, nb),
        in_specs=[pl.BlockSpec((None, tr, b), lambda k, i, c_ref: (k, c_ref[0] * nb + i, 0)),
                  pl.BlockSpec((None, tr, b), lambda k, i, c_ref: (k, i, 0))],
        out_specs=pl.BlockSpec((None, tr, b), lambda k, i, c_ref: (k, i, 0)))
    return pl.pallas_call(body, name=name, grid_spec=grid_spec, out_shape=jax.ShapeDtypeStruct(r1.shape, out_dtype),
                          compiler_params=_cparams(("parallel", "parallel")))(c_arr, x, r1)


def chip_sum(name, recv, own, idx):
    nl = len(recv)
    k, a2, b = recv[0].shape
    tr = _row_tile(a2, k * b * recv[0].dtype.itemsize)
    nb = a2 // tr

    def body(idx_ref, *refs):
        rs, os_, o_ref = refs[:nl], refs[nl:2 * nl], refs[2 * nl]
        l = pl.program_id(0)
        me = idx_ref[1]
        for li in range(nl):
            @pl.when(l == li)
            def _(li=li):
                acc = None
                for q in range(k):
                    term = jnp.where(me == q, os_[li][...], rs[li][q]).astype(F32)
                    acc = term if acc is None else acc + term
                o_ref[...] = acc

    def pick(li):
        return lambda l, i, idx_ref: jnp.where(l == li, i, 0)

    in_specs = [pl.BlockSpec((k, tr, b), lambda l, i, idx_ref, _p=pick(li): (0, _p(l, i, idx_ref), 0)) for li in range(nl)]
    in_specs += [pl.BlockSpec((None, tr, b), lambda l, i, idx_ref, _p=pick(li): (idx_ref[1], _p(l, i, idx_ref), 0)) for li in range(nl)]
    grid_spec = pltpu.PrefetchScalarGridSpec(
        num_scalar_prefetch=1, grid=(nl, nb), in_specs=in_specs,
        out_specs=pl.BlockSpec((None, tr, b), lambda l, i, idx_ref: (l, idx_ref[0] * nb + i, 0)))
    return pl.pallas_call(body, name=name, grid_spec=grid_spec, out_shape=jax.ShapeDtypeStruct((nl, 2 * a2, b), F32),
                          compiler_params=_cparams(("arbitrary", "arbitrary")))(idx, *recv, *own)


def adamw(name, gs, w, m, v):
    nl, r, b = w.shape
    assert len(gs) == nl
    tr = _row_tile(r, b * 4)

    def body(*refs):
        g_refs, (w_ref, m_ref, v_ref, go_ref, d_ref, nm_ref, nv_ref) = refs[:nl], refs[nl:]
        l = pl.program_id(0)
        for li in range(nl):
            @pl.when(l == li)
            def _(li=li):
                gv = g_refs[li][...]
                go_ref[...] = gv
                mn = ADAM_B1 * m_ref[...] + (1.0 - ADAM_B1) * gv
                vn = ADAM_B2 * v_ref[...] + (1.0 - ADAM_B2) * (gv * gv)
                m_hat = mn / (1.0 - ADAM_B1 ** ADAM_STEP)
                v_hat = vn / (1.0 - ADAM_B2 ** ADAM_STEP)
                d_ref[...] = -ADAM_LR * (m_hat / (jnp.sqrt(v_hat) + ADAM_EPS) + ADAM_WD * w_ref[...])
                nm_ref[...] = mn
                nv_ref[...] = vn

    g_specs = [pl.BlockSpec((None, tr, b), lambda l, i, _li=li: (0, jnp.where(l == _li, i, 0), 0)) for li in range(nl)]
    spec = pl.BlockSpec((None, tr, b), lambda l, i: (l, i, 0))
    sds = jax.ShapeDtypeStruct(w.shape, F32)
    return pl.pallas_call(body, name=name, grid=(nl, r // tr), in_specs=g_specs + [spec] * 3, out_specs=[spec] * 4,
                          out_shape=[sds] * 4, compiler_params=_cparams(("arbitrary", "arbitrary")))(*gs, w, m, v)


def adamw_many(name, gs, ws, ms, vs):
    n = len(gs)

    def body(*refs):
        for k in range(n):
            g_ref, w_ref, m_ref, v_ref, go_ref, d_ref, nm_ref, nv_ref = (refs[j * n + k] for j in range(8))
            gv = g_ref[...]
            go_ref[...] = gv
            mn = ADAM_B1 * m_ref[...] + (1.0 - ADAM_B1) * gv
            vn = ADAM_B2 * v_ref[...] + (1.0 - ADAM_B2) * (gv * gv)
            m_hat = mn / (1.0 - ADAM_B1 ** ADAM_STEP)
            v_hat = vn / (1.0 - ADAM_B2 ** ADAM_STEP)
            d_ref[...] = -ADAM_LR * (m_hat / (jnp.sqrt(v_hat) + ADAM_EPS) + ADAM_WD * w_ref[...])
            nm_ref[...] = mn
            nv_ref[...] = vn

    vm = pl.BlockSpec(memory_space=pltpu.VMEM)
    res = pl.pallas_call(body, name=name, in_specs=[vm] * (4 * n), out_specs=[vm] * (4 * n),
                         out_shape=[jax.ShapeDtypeStruct(a.shape, F32) for _ in range(4) for a in ws],
                         compiler_params=pltpu.CompilerParams(vmem_limit_bytes=VMEM_LIMIT))(*gs, *ws, *ms, *vs)
    return [list(res[j * n:(j + 1) * n]) for j in range(4)]


def _pack_rows(shapes):
    return [8 * -(-int(np.prod(s)) // (8 * LANES)) for s in shapes]


def _pack(arrs, rows):
    pieces = []
    for a, r in zip(arrs, _pack_rows([a.shape for a in arrs])):
        flat = a.reshape(-1)
        pieces.append(jnp.pad(flat, (0, r * LANES - flat.shape[0])).reshape(r, LANES))
    used = sum(p.shape[0] for p in pieces)
    assert used <= rows
    if used < rows:
        pieces.append(jnp.zeros((rows - used, LANES), F32))
    return jnp.concatenate(pieces, axis=0)


def _unpack(pack, shapes):
    out, off = [], 0
    for shp, r in zip(shapes, _pack_rows(shapes)):
        n = int(np.prod(shp))
        out.append(pack[off:off + r].reshape(-1)[:n].reshape(shp))
        off += r
    return out


BIG_NAMES = ['w_in', 'w_out', 'ffn_w_up', 'ffn_w_down']
MED_NAMES = [n for n in SHARDED_NAMES if n not in BIG_NAMES]
MED_ROWS = 96
SMALL_ROWS = 288


def kernel(x, ln_in_g, ln_in_b, w_in, conv_a_w, conv_a_b, ln_a_g, ln_a_b, qk_norm_q, qk_norm_k, sgu_ln_g, sgu_ln_b, sgu_w, sgu_b, mla_q_norm, mla_w_uq, mla_kv_norm, mla_w_ukv, w_out, ln_mix_g, ln_mix_b, ffn_w_up, ffn_conv_w, ffn_conv_b, ffn_w_down, ln_ffn_g, ln_ffn_b, loss_target, m_ln_in_g, m_ln_in_b, m_w_in, m_conv_a_w, m_conv_a_b, m_ln_a_g, m_ln_a_b, m_qk_norm_q, m_qk_norm_k, m_sgu_ln_g, m_sgu_ln_b, m_sgu_w, m_sgu_b, m_mla_q_norm, m_mla_w_uq, m_mla_kv_norm, m_mla_w_ukv, m_w_out, m_ln_mix_g, m_ln_mix_b, m_ffn_w_up, m_ffn_conv_w, m_ffn_conv_b, m_ffn_w_down, m_ln_ffn_g, m_ln_ffn_b, v_ln_in_g, v_ln_in_b, v_w_in, v_conv_a_w, v_conv_a_b, v_ln_a_g, v_ln_a_b, v_qk_norm_q, v_qk_norm_k, v_sgu_ln_g, v_sgu_ln_b, v_sgu_w, v_sgu_b, v_mla_q_norm, v_mla_w_uq, v_mla_kv_norm, v_mla_w_ukv, v_w_out, v_ln_mix_g, v_ln_mix_b, v_ffn_w_up, v_ffn_conv_w, v_ffn_conv_b, v_ffn_w_down, v_ln_ffn_g, v_ln_ffn_b):
    loc = dict(locals())
    w_loc = {n: loc[n] for n in WEIGHT_NAMES}
    m_loc = {n: loc["m_" + n] for n in WEIGHT_NAMES}
    v_loc = {n: loc["v_" + n] for n in WEIGHT_NAMES}
    me_chip = 2 * lax.axis_index("x") + lax.axis_index("y")
    idx = jnp.stack([lax.axis_index("c"), me_chip]).astype(jnp.int32)

    def wire(n):
        return w_loc[n].astype(BF16) if n in BF16_WIRE else w_loc[n]

    assert DEPTH == 2
    early = [n for n in SHARDED_NAMES if n not in LATE_NAMES]
    wired = {n: wire(n) for n in SHARDED_NAMES}
    W = {n: w_loc[n] for n in REPL_NAMES}
    for n, g in zip(early, run_comm("gather_early", gather_over_chips([(wired[n], 0) for n in early]))):
        W[n] = [_assemble(n, wired[n][0:1], g, me_chip)[0], None]

    dist = dict(idx=idx, me=me_chip, late={n: wired[n] for n in LATE_NAMES}, next=(early, [wired[n] for n in early]))
    loss_row, dx, G = local_step(x[0], loss_target[0], W, dist)
    loss = lax.psum(loss_row[0, 0], ("x", "y", "c"))
    red = G['reduced']

    med = []
    for q in range(N_CHIPS):
        pieces = [lax.slice_in_dim(jnp.stack(G[n]), q * w_loc[n].shape[2], (q + 1) * w_loc[n].shape[2], axis=2) for n in MED_NAMES]
        med.append(_pack(pieces, MED_ROWS))
    rest = [(n, l) for n in BIG_NAMES for l in range(DEPTH) if (n, l) not in red]
    small_pack = _pack([jnp.stack(G[n]) if isinstance(G[n], list) else G[n] for n in REPL_NAMES], SMALL_ROWS)
    xs = [G[n][l] for n, l in rest] + [jnp.stack(med), jnp.broadcast_to(small_pack[None], (N_CHIPS, SMALL_ROWS, LANES))]
    wire_dt = [BF16] * (len(xs) - 1) + [F32]
    r1 = run_comm("pair_exchange", pair_exchange(xs))
    ps = [pair_sum(f"pair_sum{t}", xv, rv, idx, dt) for t, (xv, rv, dt) in enumerate(zip(xs, r1, wire_dt))]
    cs = run_comm("chip_exchange", chip_exchange(ps))
    halves = [chip_sum(f"chip_sum{t}", [cv], [pv], idx) for t, (cv, pv) in enumerate(zip(cs, ps))]
    done = run_comm("half_exchange", half_exchange(halves))
    red.update(zip(rest, done[:-2]))
    g_med, g_small = done[-2], done[-1]

    outs = {k: {} for k in ("grad", "delta", "new_m", "new_v")}
    for n in BIG_NAMES:
        res = adamw(f"adamw_{n}", [red[(n, l)] for l in range(DEPTH)], w_loc[n], m_loc[n], v_loc[n])
        for k, v in zip(("grad", "delta", "new_m", "new_v"), res):
            outs[k][n] = v
    names = MED_NAMES + REPL_NAMES
    g_nat = _unpack(g_med[0], [w_loc[n].shape for n in MED_NAMES]) + _unpack(g_small[0], [w_loc[n].shape for n in REPL_NAMES])

    def two(a):
        return a.reshape(1, -1) if a.ndim == 1 else a

    res = adamw_many("adamw_small", [two(g) for g in g_nat], *[[two(d[n]) for n in names] for d in (w_loc, m_loc, v_loc)])
    for k, vals in zip(("grad", "delta", "new_m", "new_v"), res):
        outs[k].update({n: v.reshape(w_loc[n].shape) for n, v in zip(names, vals)})
    return (loss, dx[None], *[outs["grad"][n] for n in WEIGHT_NAMES], *[outs["delta"][n] for n in WEIGHT_NAMES],
            *[outs["new_m"][n] for n in WEIGHT_NAMES], *[outs["new_v"][n] for n in WEIGHT_NAMES])
```

```python
import functools
import math

import jax
import jax.numpy as jnp
import numpy as np
from jax import lax
from jax.experimental import pallas as pl
from jax.experimental.pallas import tpu as pltpu

F32 = jnp.float32
BF16 = jnp.bfloat16

D_MODEL = 1024
DEPTH = 2
GRID_W = 64
GROUP_W = 256
HEAD_DIM = 64
CONV_A_WIDTH = 31
CHUNK = 128
MLA_Q_LORA = 192
MLA_KV_LORA = 128
MLA_NOPE = 64
MLA_ROPE = 32
D_FF = 2816
ROPE_THETA = 10000.0
DEEPNORM_ALPHA = (2 * DEPTH) ** 0.25
LN_EPS = 1e-5
RMS_EPS = 1e-6
D_IN_PROJ = 1888
PROJ_W = 2048

ADAM_LR = 0.001
ADAM_B1 = 0.9
ADAM_B2 = 0.999
ADAM_EPS = 1e-08
ADAM_WD = 0.01
ADAM_STEP = 10

VMEM_LIMIT = 56 * 1024 * 1024
MESH = pl.DeviceIdType.MESH
N_CHIPS = 4
N_DEV = 8

WEIGHT_NAMES = ['ln_in_g', 'ln_in_b', 'w_in', 'conv_a_w', 'conv_a_b', 'ln_a_g', 'ln_a_b', 'qk_norm_q', 'qk_norm_k',
                'sgu_ln_g', 'sgu_ln_b', 'sgu_w', 'sgu_b', 'mla_q_norm', 'mla_w_uq', 'mla_kv_norm', 'mla_w_ukv', 'w_out',
                'ln_mix_g', 'ln_mix_b', 'ffn_w_up', 'ffn_conv_w', 'ffn_conv_b', 'ffn_w_down', 'ln_ffn_g', 'ln_ffn_b']
SHARDED_AXIS = {'w_in': 2, 'conv_a_w': 2, 'mla_w_uq': 2, 'mla_w_ukv': 2, 'w_out': 1, 'ffn_w_up': 2, 'ffn_conv_w': 2,
                'ffn_w_down': 1}
SHARDED_NAMES = [n for n in WEIGHT_NAMES if n in SHARDED_AXIS]
REPL_NAMES = [n for n in WEIGHT_NAMES if n not in SHARDED_AXIS]
BF16_WIRE = ('w_in', 'mla_w_uq', 'mla_w_ukv', 'w_out', 'ffn_w_up', 'ffn_w_down')


def _cparams(sem):
    return pltpu.CompilerParams(dimension_semantics=sem, vmem_limit_bytes=VMEM_LIMIT)


def _pick(n, cands):
    for c in cands:
        if n % c == 0:
            return c
    return n


def carried_call(name, body, grid, in_specs, out_specs, out_shape, scratch, semantics, args, comm=None):
    if comm is None:
        res = pl.pallas_call(body, name=name, grid=grid, in_specs=in_specs, out_specs=out_specs, out_shape=out_shape,
                             scratch_shapes=scratch, compiler_params=_cparams(semantics))(*args)
        return list(res), None
    n_in, n_out, n_scr = len(in_specs), len(out_specs), len(scratch)
    nci, nco = len(comm.operands), len(comm.out_shape)

    def carried(*refs):
        o0 = n_in + nci
        s0 = o0 + n_out + nco
        ins, cins = refs[:n_in], refs[n_in:o0]
        outs, couts = refs[o0:o0 + n_out], refs[o0 + n_out:s0]
        scr, sems = refs[s0:s0 + n_scr], refs[s0 + n_scr:]
        ids = [pl.program_id(d) for d in range(len(grid))]
        first = functools.reduce(lambda u, v: u & v, [i == 0 for i in ids])
        last = functools.reduce(lambda u, v: u & v, [i == g - 1 for i, g in zip(ids, grid)])

        @pl.when(first)
        def _():
            comm.start(cins, couts, sems)

        body(*ins, *outs, *scr)

        @pl.when(last)
        def _():
            comm.finish(cins, couts, sems)

    res = pl.pallas_call(
        carried, name=name, grid=grid, in_specs=list(in_specs) + [ANY] * nci, out_specs=list(out_specs) + [ANY] * nco,
        out_shape=list(out_shape) + comm.out_shape, scratch_shapes=list(scratch) + [pltpu.SemaphoreType.DMA((k,)) for k in comm.sems],
        input_output_aliases={n_in + i: n_out + o for i, o in comm.aliases.items()},
        compiler_params=_cparams(("arbitrary",) * len(grid)))(*args, *comm.operands)
    return list(res[:n_out]), list(res[n_out:])


def mm(name, a, b, *, ta=False, tb=False, add=None, add_scale=1.0, out_dtype=F32, chip_major=False, comm=None):
    m, k = (a.shape[1], a.shape[0]) if ta else a.shape
    n = b.shape[0] if tb else b.shape[1]
    assert (b.shape[1] if tb else b.shape[0]) == k
    tm = _pick(m, (1024, 1408, 512, 256, 128))
    tn = n // N_CHIPS if chip_major else _pick(n, (1024, 1408, 512, 256, 128))
    tk = _pick(k, (1024, 1408, 512, 256, 128))
    nk = k // tk
    a_spec = pl.BlockSpec((tk, tm), lambda i, j, kk: (kk, i)) if ta else pl.BlockSpec((tm, tk), lambda i, j, kk: (i, kk))
    b_spec = pl.BlockSpec((tn, tk), lambda i, j, kk: (j, kk)) if tb else pl.BlockSpec((tk, tn), lambda i, j, kk: (kk, j))
    in_specs = [a_spec, b_spec]
    args = [a, b]
    if add is not None:
        in_specs.append(pl.BlockSpec((tm, tn), lambda i, j, kk: (i, j)))
        args.append(add)
    dims = (((0 if ta else 1,), (1 if tb else 0,)), ((), ()))

    def body(*refs):
        a_ref, b_ref = refs[0], refs[1]
        add_ref = refs[2] if add is not None else None
        o_ref, acc_ref = refs[-2], refs[-1]
        kk = pl.program_id(2)
        part = lax.dot_general(a_ref[...].astype(BF16), b_ref[...].astype(BF16), dims, preferred_element_type=F32)

        @pl.when(kk == 0)
        def _():
            acc_ref[...] = part

        @pl.when(kk > 0)
        def _():
            acc_ref[...] += part

        @pl.when(kk == nk - 1)
        def _():
            r = acc_ref[...]
            if add_ref is not None:
                r = r + add_scale * add_ref[...].astype(F32)
            o_ref[...] = r.astype(out_dtype)

    if chip_major:
        out_spec = pl.BlockSpec((None, tm, tn), lambda i, j, kk: (j, i, 0))
        out_shape = jax.ShapeDtypeStruct((N_CHIPS, m, tn), out_dtype)
    else:
        out_spec = pl.BlockSpec((tm, tn), lambda i, j, kk: (i, j))
        out_shape = jax.ShapeDtypeStruct((m, n), out_dtype)
    res, cres = carried_call(name, body, (m // tm, n // tn, nk), in_specs, [out_spec], [out_shape], [pltpu.VMEM((tm, tn), F32)],
                             ("parallel", "parallel", "arbitrary"), args, comm)
    return res[0] if comm is None else (res[0], cres)


def _cb(cb):
    return cb if callable(cb) else (lambda j, _c=cb: _c)


def tiled_call(name, fn, s, tm, ins, outs, ncol=1):
    nrow = s // tm
    in_specs, args, kinds = [], [], []
    for it in ins:
        kind = it[0]
        if kind == 'row':
            _, arr, w, cb = it
            in_specs.append(pl.BlockSpec((tm, w), lambda j, i, _c=_cb(cb): (i, _c(j))))
            args.append(arr)
            kinds.append(('row',))
        elif kind == 'halo':
            _, arr, w, cb, h = it
            r = tm // h
            nh = s // h
            in_specs.append(pl.BlockSpec((h, w), lambda j, i, _c=_cb(cb), _r=r: (jnp.maximum(i * _r - 1, 0), _c(j))))
            in_specs.append(pl.BlockSpec((tm, w), lambda j, i, _c=_cb(cb): (i, _c(j))))
            in_specs.append(pl.BlockSpec((h, w), lambda j, i, _c=_cb(cb), _r=r, _n=nh: (jnp.minimum((i + 1) * _r, _n - 1), _c(j))))
            args += [arr, arr, arr]
            kinds.append(('halo',))
        elif kind == 'full':
            arr = it[1]
            in_specs.append(pl.BlockSpec(arr.shape, lambda j, i, _n=arr.ndim: (0,) * _n))
            args.append(arr)
            kinds.append(('ref',))
        elif kind == 'col':
            _, arr, w, cb = it
            in_specs.append(pl.BlockSpec((arr.shape[0], w), lambda j, i, _c=_cb(cb): (0, _c(j))))
            args.append(arr)
            kinds.append(('ref',))
        elif kind == 'hm':
            arr = it[1]
            in_specs.append(pl.BlockSpec((arr.shape[0], tm, arr.shape[2]), lambda j, i: (0, i, 0)))
            args.append(arr)
            kinds.append(('ref',))
        else:
            raise ValueError(kind)
    out_specs, out_shapes, okinds = [], [], []
    for ot in outs:
        kind = ot[0]
        if kind == 'row':
            _, wt, w, cb, dt = ot
            out_specs.append(pl.BlockSpec((tm, w), lambda j, i, _c=_cb(cb): (i, _c(j))))
            out_shapes.append(jax.ShapeDtypeStruct((s, wt), dt))
            okinds.append('row')
        elif kind == 'hm':
            _, hh, d, dt = ot
            out_specs.append(pl.BlockSpec((hh, tm, d), lambda j, i: (0, i, 0)))
            out_shapes.append(jax.ShapeDtypeStruct((hh, s, d), dt))
            okinds.append('hm')
        elif kind == 'acc':
            shape = ot[1]
            out_specs.append(pl.BlockSpec(shape, lambda j, i, _n=len(shape): (0,) * _n))
            out_shapes.append(jax.ShapeDtypeStruct(shape, F32))
            okinds.append('acc')
        elif kind == 'colacc':
            _, r, wt, w, cb = ot
            out_specs.append(pl.BlockSpec((r, w), lambda j, i, _c=_cb(cb): (0, _c(j))))
            out_shapes.append(jax.ShapeDtypeStruct((r, wt), F32))
            okinds.append('colacc')
        else:
            raise ValueError(kind)
    n_in = len(in_specs)

    def body(*refs):
        j = pl.program_id(0)
        i = pl.program_id(1)
        in_refs, out_refs = refs[:n_in], refs[n_in:]
        items, p = [], 0
        for kd in kinds:
            if kd[0] == 'row':
                items.append(in_refs[p][...])
                p += 1
            elif kd[0] == 'halo':
                prev, cen, nxt = in_refs[p][...], in_refs[p + 1][...], in_refs[p + 2][...]
                prev = jnp.where(i == 0, jnp.zeros_like(prev), prev)
                nxt = jnp.where(i == nrow - 1, jnp.zeros_like(nxt), nxt)
                items.append(jnp.concatenate([prev, cen, nxt], axis=0))
                p += 3
            else:
                items.append(in_refs[p])
                p += 1
        for o_ref, kd in zip(out_refs, okinds):
            if kd == 'acc':
                @pl.when((i == 0) & (j == 0))
                def _(o_ref=o_ref):
                    o_ref[...] = jnp.zeros_like(o_ref)
            elif kd == 'colacc':
                @pl.when(i == 0)
                def _(o_ref=o_ref):
                    o_ref[...] = jnp.zeros_like(o_ref)
        fn(i, j, *items, *out_refs)

    res = pl.pallas_call(
        body, name=name, grid=(ncol, nrow), in_specs=in_specs, out_specs=out_specs, out_shape=out_shapes,
        compiler_params=_cparams(("arbitrary", "arbitrary")),
    )(*args)
    return res


def _sigmoid(x):
    return 1.0 / (1.0 + jnp.exp(-x))


def _ln_stats(x):
    mu = jnp.mean(x, axis=1, keepdims=True)
    xc = x - mu
    var = jnp.mean(xc * xc, axis=1, keepdims=True)
    r = lax.rsqrt(var + LN_EPS)
    return xc * r, r


def _ln_bwd(dy, xh, r, g):
    dxh = dy * g
    dx = r * (dxh - jnp.mean(dxh, axis=1, keepdims=True) - xh * jnp.mean(dxh * xh, axis=1, keepdims=True))
    return dx, jnp.sum(dy * xh, axis=0, keepdims=True), jnp.sum(dy, axis=0, keepdims=True)


def _gmean(v, gm):
    hi = v.astype(BF16)
    lo = (v - hi.astype(F32)).astype(BF16)
    return jnp.dot(hi, gm, preferred_element_type=F32) + jnp.dot(lo, gm, preferred_element_type=F32)


def _roll_l(x, sh):
    return pltpu.roll(x, sh % x.shape[1], 1)


def _rope(x, c, sa, sb, sh):
    return x * c + _roll_l(x, -sh) * sa + _roll_l(x, sh) * sb


def _rope_t(dy, c, sa, sb, sh):
    return dy * c + _roll_l(dy * sa, sh) + _roll_l(dy * sb, -sh)


def _roll_r(x, s):
    return pltpu.roll(x, (-s) % x.shape[0], 0)


_GELU_C = math.sqrt(2.0 / math.pi)


def _gelu(x):
    t = jnp.tanh(_GELU_C * (x + 0.044715 * x * x * x))
    return 0.5 * x * (1.0 + t), t


def _gelu_grad(x, t):
    return 0.5 * (1.0 + t) + 0.5 * x * (1.0 - t * t) * _GELU_C * (1.0 + 3.0 * 0.044715 * x * x)


def _dot_nt(a, b):
    return lax.dot_general(a, b, (((1,), (1,)), ((), ())), preferred_element_type=F32)


def _dot_tn(a, b):
    return lax.dot_general(a, b, (((0,), (0,)), ((), ())), preferred_element_type=F32)


def _rope_block(s, d):
    t = jnp.arange(s)
    row = (t // GRID_W).astype(F32)
    col = (t % GRID_W).astype(F32)
    half = d // 4
    inv = ROPE_THETA ** (-jnp.arange(half, dtype=F32) / half)
    z = jnp.zeros((s, half), F32)
    cs, sas, sbs = [], [], []
    for pos in (row, col):
        ang = pos[:, None] * inv[None, :]
        co, si = jnp.cos(ang), jnp.sin(ang)
        cs += [co, co]
        sas += [-si, z]
        sbs += [z, si]
    return tuple(jnp.concatenate(v, axis=1) for v in (cs, sas, sbs))


def _rope_tables(s):
    gqa = tuple(jnp.tile(a, (1, 4)) for a in _rope_block(s, HEAD_DIM))
    c32, sa32, sb32 = _rope_block(s, MLA_ROPE)

    def head128(c, fill):
        return jnp.concatenate([jnp.full((s, 64), fill, F32), c, jnp.zeros((s, 32), F32)], axis=1)

    mla = tuple(jnp.tile(a, (1, 4)) for a in (head128(c32, 1.0), head128(sa32, 0.0), head128(sb32, 0.0)))
    kr = (head128(c32, 0.0), head128(sa32, 0.0), head128(sb32, 0.0))
    return dict(gqa=gqa, mla=mla, kr=kr)


def _group_mean_matrix(w, g):
    idx = np.arange(w) // g
    return jnp.asarray((idx[:, None] == idx[None, :]).astype(np.float32) / g, dtype=BF16)


TM = 512


def ln_fwd(name, s, x, g, b, res=None, alpha=1.0):
    ins = [('row', x, D_MODEL, 0)] + ([('row', res, D_MODEL, 0)] if res is not None else []) + [('full', g), ('full', b)]

    def fn(i, j, *a):
        if res is not None:
            xv, rv, g_ref, b_ref, o_ref = a
            z = alpha * rv + xv
        else:
            xv, g_ref, b_ref, o_ref = a
            z = xv
        xh, _ = _ln_stats(z)
        o_ref[...] = xh * g_ref[...] + b_ref[...]

    return tiled_call(name, fn, s, TM, ins, [('row', D_MODEL, D_MODEL, 0, F32)])[0]


def ln_bwd(name, s, dy, x, g, res=None, alpha=1.0):
    ins = [('row', dy, D_MODEL, 0), ('row', x, D_MODEL, 0)]
    ins += ([('row', res, D_MODEL, 0)] if res is not None else []) + [('full', g)]

    def fn(i, j, *a):
        a = list(a)
        dyv = a.pop(0)
        xv = a.pop(0)
        z = alpha * a.pop(0) + xv if res is not None else xv
        g_ref, dz_ref, dg_ref, db_ref = a
        xh, r = _ln_stats(z)
        dz, dg, db = _ln_bwd(dyv, xh, r, g_ref[...])
        dz_ref[...] = dz
        dg_ref[...] += dg
        db_ref[...] += db

    return tiled_call(name, fn, s, TM, ins, [('row', D_MODEL, D_MODEL, 0, F32), ('acc', (1, D_MODEL)), ('acc', (1, D_MODEL))])


def loss_and_ln_bwd(name, s, x, res, g, b, tgt, alpha):
    ins = [('row', x, D_MODEL, 0), ('row', res, D_MODEL, 0), ('row', tgt, D_MODEL, 0), ('full', g), ('full', b)]

    def fn(i, j, xv, rv, tv, g_ref, b_ref, dz_ref, dg_ref, db_ref, loss_ref):
        xh, r = _ln_stats(alpha * rv + xv)
        y = xh * g_ref[...] + b_ref[...]
        e = y - tv
        dz, dg, db = _ln_bwd(e * (1.0 / D_MODEL), xh, r, g_ref[...])
        dz_ref[...] = dz
        dg_ref[...] += dg
        db_ref[...] += db
        loss_ref[...] += jnp.sum(jnp.sum(e * e, axis=0, keepdims=True), axis=1, keepdims=True) * (0.5 / D_MODEL)

    return tiled_call(name, fn, s, TM, ins, [('row', D_MODEL, D_MODEL, 0, F32), ('acc', (1, D_MODEL)), ('acc', (1, D_MODEL)),
                                             ('acc', (1, 128))])


HALO_A = 16


def _glu(a_in):
    return a_in[:, :GROUP_W] * _sigmoid(a_in[:, GROUP_W:])


def _row_windows(x, halo, tm):
    rolled = {0: x}

    def window(s):
        b = s % 8
        if b not in rolled:
            rolled[b] = _roll_r(x, b)
        off = halo + s - b
        return rolled[b][off:off + tm]

    return window


def a_fwd(name, s, proj, cw, cb, g, b):
    def fn(i, j, a_ext, cw_ref, cb_ref, g_ref, b_ref, y_ref, o_ref):
        a_at = _row_windows(_glu(a_ext), HALO_A, TM)
        acc = jnp.zeros((TM, GROUP_W), F32)
        for k in range(CONV_A_WIDTH):
            acc = acc + a_at(k - 15) * cw_ref[pl.ds(k, 1), :]
        y = acc + cb_ref[...]
        y_ref[...] = y
        xh, _ = _ln_stats(y)
        z = xh * g_ref[...] + b_ref[...]
        o_ref[...] = z * _sigmoid(z)

    ins = [('halo', proj, 2 * GROUP_W, 0, HALO_A), ('full', cw), ('full', cb), ('full', g), ('full', b)]
    return tiled_call(name, fn, s, TM, ins, [('row', GROUP_W, GROUP_W, 0, F32), ('row', GROUP_W, GROUP_W, 0, F32)])


def a_bwd1(name, s, y, dmix, g, b):
    def fn(i, j, yv, do, g_ref, b_ref, dy_ref, dg_ref, db_ref, dcb_ref):
        xh, r = _ln_stats(yv)
        z = xh * g_ref[...] + b_ref[...]
        sg = _sigmoid(z)
        dz = do * sg * (1.0 + z * (1.0 - sg))
        dy, dg, db = _ln_bwd(dz, xh, r, g_ref[...])
        dy_ref[...] = dy
        dg_ref[...] += dg
        db_ref[...] += db
        dcb_ref[...] += jnp.sum(dy, axis=0, keepdims=True)

    ins = [('row', y, GROUP_W, 0), ('row', dmix, GROUP_W, 0), ('full', g), ('full', b)]
    return tiled_call(name, fn, s, TM, ins, [('row', GROUP_W, GROUP_W, 0, F32), ('acc', (1, GROUP_W)), ('acc', (1, GROUP_W)),
                                             ('acc', (1, GROUP_W))])


def a_bwd2(name, s, proj, dy, cw):
    def fn(i, j, a_ext, dy_ext, cw_ref, da_ref, dcw_ref):
        a1, a2 = a_ext[:, :GROUP_W], a_ext[:, GROUP_W:]
        sg = _sigmoid(a2)
        a = a1 * sg
        dyc = dy_ext[HALO_A:HALO_A + TM]
        dy_at, a_at = _row_windows(dy_ext, HALO_A, TM), _row_windows(a, HALO_A, TM)
        da = jnp.zeros((TM, GROUP_W), F32)
        for k in range(CONV_A_WIDTH):
            da = da + dy_at(15 - k) * cw_ref[pl.ds(k, 1), :]
            dcw_ref[pl.ds(k, 1), :] += jnp.sum(dyc * a_at(k - 15), axis=0, keepdims=True)
        a1c, sgc = a1[HALO_A:HALO_A + TM], sg[HALO_A:HALO_A + TM]
        da_ref[...] = jnp.concatenate([da * sgc, da * a1c * sgc * (1.0 - sgc)], axis=1).astype(BF16)

    ins = [('halo', proj, 2 * GROUP_W, 0, HALO_A), ('halo', dy, GROUP_W, 0, HALO_A), ('full', cw)]
    return tiled_call(name, fn, s, TM, ins, [('row', 2 * GROUP_W, 2 * GROUP_W, 0, BF16), ('acc', (32, GROUP_W))])


def b_prep_fwd(name, s, proj, gq, gk, tabs, gm):
    c, sa, sb = tabs

    def fn(i, j, qkv, cv, sav, sbv, gq_ref, gk_ref, gm_ref, q_ref, k_ref, v_ref):
        q, k, v = qkv[:, :256], qkv[:, 256:384], qkv[:, 384:]
        gmv = gm_ref[...]
        qn = q * lax.rsqrt(_gmean(q * q, gmv) + RMS_EPS) * gq_ref[...]
        kn = k * lax.rsqrt(_gmean(k * k, gmv[:128, :128]) + RMS_EPS) * gk_ref[...]
        qr = _rope(qn, cv, sav, sbv, 16)
        kr = _rope(kn, cv[:, :128], sav[:, :128], sbv[:, :128], 16)
        for h in range(4):
            q_ref[h] = (qr[:, h * 64:(h + 1) * 64] * QSCALE_GQA).astype(BF16)
        for h in range(2):
            k_ref[h] = kr[:, h * 64:(h + 1) * 64].astype(BF16)
            v_ref[h] = v[:, h * 64:(h + 1) * 64].astype(BF16)

    ins = [('row', proj, 512, 1), ('row', c, 256, 0), ('row', sa, 256, 0), ('row', sb, 256, 0), ('full', gq), ('full', gk),
           ('full', gm)]
    return tiled_call(name, fn, s, TM, ins, [('hm', 4, 64, BF16), ('hm', 2, 64, BF16), ('hm', 2, 64, BF16)])


def b_prep_bwd(name, s, proj, gq, gk, tabs, gm, dq_hm, dk_hm, dv_hm):
    c, sa, sb = tabs

    def fn(i, j, qkv, cv, sav, sbv, gq_ref, gk_ref, gm_ref, dq_ref, dk_ref, dv_ref, dp_ref, dgq_ref, dgk_ref):
        q, k = qkv[:, :256], qkv[:, 256:384]
        gmv = gm_ref[...]
        dqr = jnp.concatenate([dq_ref[h] for h in range(4)], axis=1) * QSCALE_GQA
        dkr = jnp.concatenate([dk_ref[h] for h in range(2)], axis=1)
        dv = jnp.concatenate([dv_ref[h] for h in range(2)], axis=1)
        outs = []
        for x, dxr, g_ref, gmx, w, dg_ref in ((q, dqr, gq_ref, gmv, 256, dgq_ref), (k, dkr, gk_ref, gmv[:128, :128], 128, dgk_ref)):
            dn = _rope_t(dxr, cv[:, :w], sav[:, :w], sbv[:, :w], 16)
            r = lax.rsqrt(_gmean(x * x, gmx) + RMS_EPS)
            gv = g_ref[...]
            dx = gv * r * dn - x * (r * r * r) * _gmean(x * gv * dn, gmx)
            dgt = jnp.sum(x * r * dn, axis=0, keepdims=True)
            dg = dgt[:, 0:64]
            for h in range(1, w // 64):
                dg = dg + dgt[:, h * 64:(h + 1) * 64]
            dg_ref[...] += dg
            outs.append(dx)
        dp_ref[...] = jnp.concatenate(outs + [dv], axis=1).astype(BF16)

    ins = [('row', proj, 512, 1), ('row', c, 256, 0), ('row', sa, 256, 0), ('row', sb, 256, 0), ('full', gq), ('full', gk),
           ('full', gm), ('hm', dq_hm), ('hm', dk_hm), ('hm', dv_hm)]
    return tiled_call(name, fn, s, TM, ins, [('row', 512, 512, 0, BF16), ('acc', (1, 64)), ('acc', (1, 64))])


TQ = 512
TQ_FWD = 256


LOG2E = 1.4426950408889634
QSCALE_GQA = HEAD_DIM ** -0.5 * LOG2E
QSCALE_MLA = (MLA_NOPE + MLA_ROPE) ** -0.5 * LOG2E


def attn_fwd(name, s, q_hm, k_hm, v_hm, comm=None):
    dk = q_hm.shape[2]
    nkv = k_hm.shape[0]
    tq = TQ_FWD if s % TQ_FWD == 0 else TQ

    def body(q_ref, k_ref, v_ref, o_ref, l_ref):
        outs = []
        for g in range(4):
            kv = g * nkv // 4
            sc = _dot_nt(q_ref[g], k_ref[kv])
            m = jnp.max(sc, axis=1, keepdims=True)
            p = jnp.exp2(sc - m)
            l = jnp.sum(p, axis=1, keepdims=True)
            o = jnp.dot(p.astype(BF16), v_ref[kv], preferred_element_type=F32)
            outs.append(o / l)
            l_ref[g] = m + jnp.log2(l)
        o_ref[...] = jnp.concatenate(outs, axis=1)

    res, cres = carried_call(
        name, body, (s // tq,),
        [pl.BlockSpec((4, tq, dk), lambda i: (0, i, 0)),
         pl.BlockSpec((nkv, s, dk), lambda i: (0, 0, 0)),
         pl.BlockSpec((nkv, s, 64), lambda i: (0, 0, 0))],
        [pl.BlockSpec((tq, 256), lambda i: (i, 0)), pl.BlockSpec((4, tq, 1), lambda i: (0, i, 0))],
        [jax.ShapeDtypeStruct((s, 256), F32), jax.ShapeDtypeStruct((4, s, 1), F32)], [],
        ("arbitrary",), [q_hm, k_hm, v_hm], comm)
    return res if comm is None else (res, cres)


def attn_bwd(name, s, q_hm, k_hm, v_hm, o, lse, dmixcat, dcol, comm=None):
    dk = q_hm.shape[2]
    nkv = k_hm.shape[0]
    shared = nkv == 2
    kvi = (lambda p, g, i: (p, 0, 0)) if shared else (lambda p, g, i: (2 * p + g, 0, 0))

    def body(q_ref, k_ref, v_ref, o_ref, l_ref, do_ref, dq_ref, dk_ref, dv_ref):
        g = pl.program_id(1)
        i = pl.program_id(2)
        first = (i == 0) & (g == 0) if shared else (i == 0)

        @pl.when(first)
        def _():
            dk_ref[...] = jnp.zeros_like(dk_ref)
            dv_ref[...] = jnp.zeros_like(dv_ref)

        dob, ob = do_ref[...], o_ref[...]
        do = jnp.where(g == 0, dob[:, :64], dob[:, 64:])
        ov = jnp.where(g == 0, ob[:, :64], ob[:, 64:])
        q, k, v = q_ref[0], k_ref[0], v_ref[0]
        p = jnp.exp2(_dot_nt(q, k) - l_ref[0])
        dp = _dot_nt(do.astype(BF16), v)
        delta = jnp.sum(do * ov, axis=1, keepdims=True)
        ds = (p * (dp - delta) * (1.0 / LOG2E)).astype(BF16)
        dq_ref[0] = jnp.dot(ds, k, preferred_element_type=F32)
        dk_ref[0] += _dot_tn(ds, q)
        dv_ref[0] += _dot_tn(p.astype(BF16), do.astype(BF16))

    res, cres = carried_call(
        name, body, (2, 2, s // TQ),
        [pl.BlockSpec((1, TQ, dk), lambda p, g, i: (2 * p + g, i, 0)),
         pl.BlockSpec((1, s, dk), kvi), pl.BlockSpec((1, s, 64), kvi),
         pl.BlockSpec((TQ, 128), lambda p, g, i: (i, p)),
         pl.BlockSpec((1, TQ, 1), lambda p, g, i: (2 * p + g, i, 0)),
         pl.BlockSpec((TQ, 128), lambda p, g, i: (i, dcol + p))],
        [pl.BlockSpec((1, TQ, dk), lambda p, g, i: (2 * p + g, i, 0)),
         pl.BlockSpec((1, s, dk), kvi), pl.BlockSpec((1, s, 64), kvi)],
        [jax.ShapeDtypeStruct((4, s, dk), F32), jax.ShapeDtypeStruct((nkv, s, dk), F32), jax.ShapeDtypeStruct((nkv, s, 64), F32)],
        [], ("arbitrary", "arbitrary", "arbitrary"), [q_hm, k_hm, v_hm, o, lse, dmixcat], comm)
    return res if comm is None else (res, cres)


def _lane_group_masks(w, g):
    lane = lax.broadcasted_iota(jnp.int32, (1, w), 1)
    return [((lane >= k * g) & (lane < (k + 1) * g)).astype(F32) for k in range(w // g)]


def _sgu_gate(svn, w_ref, bias):
    masks = _lane_group_masks(GROUP_W, 64)
    outs = []
    for n in range(TM // CHUNK):
        x = svn[n * CHUNK:(n + 1) * CHUNK]
        acc = bias
        for g in range(4):
            acc = acc + jnp.dot(w_ref[g].astype(BF16), (x * masks[g]).astype(BF16), preferred_element_type=F32)
        outs.append(acc)
    return jnp.concatenate(outs, axis=0)


def c_fwd(name, s, proj, g, b, w, bias):
    def fn(i, j, cin, g_ref, b_ref, w_ref, bias_ref, o_ref):
        c, _ = _gelu(cin)
        xh, _ = _ln_stats(c[:, GROUP_W:])
        svn = xh * g_ref[...] + b_ref[...]
        o_ref[...] = c[:, :GROUP_W] * _sgu_gate(svn, w_ref, bias_ref[...])

    ins = [('row', proj, 512, 2), ('full', g), ('full', b), ('full', w), ('full', bias)]
    return tiled_call(name, fn, s, TM, ins, [('row', GROUP_W, GROUP_W, 0, F32)])[0]


def c_bwd(name, s, proj, g, b, w, wt, bias, dmixcat):
    def fn(i, j, cin, do, g_ref, b_ref, w_ref, wt_ref, bias_ref, dc_ref, dg_ref, db_ref, dw_ref, dbias_ref):
        c, t = _gelu(cin)
        u = c[:, :GROUP_W]
        xh, r = _ln_stats(c[:, GROUP_W:])
        svn = xh * g_ref[...] + b_ref[...]
        gate = _sgu_gate(svn, w_ref, bias_ref[...])
        du = do * gate
        dgate = do * u
        masks = _lane_group_masks(GROUP_W, 64)
        col = lax.broadcasted_iota(jnp.int32, (CHUNK, 128), 1)
        dsvn, dbias = [], jnp.zeros((CHUNK, 128), F32)
        for n in range(TM // CHUNK):
            dgc = dgate[n * CHUNK:(n + 1) * CHUNK]
            x = svn[n * CHUNK:(n + 1) * CHUNK]
            acc = jnp.zeros((CHUNK, GROUP_W), F32)
            for gi in range(4):
                dgm = (dgc * masks[gi]).astype(BF16)
                acc = acc + jnp.dot(wt_ref[gi].astype(BF16), dgm, preferred_element_type=F32)
                dw_ref[gi] += _dot_nt(dgm, (x * masks[gi]).astype(BF16))
                dbias = dbias + jnp.where(col == gi, jnp.sum(dgc * masks[gi], axis=1, keepdims=True), 0.0)
            dsvn.append(acc)
        dbias_ref[...] += dbias
        dsv, dg, db = _ln_bwd(jnp.concatenate(dsvn, axis=0), xh, r, g_ref[...])
        dg_ref[...] += dg
        db_ref[...] += db
        dc_ref[...] = (jnp.concatenate([du, dsv], axis=1) * _gelu_grad(cin, t)).astype(BF16)

    ins = [('row', proj, 512, 2), ('row', dmixcat, GROUP_W, 2), ('full', g), ('full', b), ('full', w), ('full', wt), ('full', bias)]
    return tiled_call(name, fn, s, TM, ins, [('row', 512, 512, 0, BF16), ('acc', (1, GROUP_W)), ('acc', (1, GROUP_W)),
                                             ('acc', (4, CHUNK, CHUNK)), ('acc', (CHUNK, 128))])


def _d_common(dpart, gq_ref, gkv_ref):
    cq, ckv, kr = dpart[:, :256], dpart[:, 256:384], dpart[:, 384:]
    rq = lax.rsqrt(jnp.sum(cq * cq, axis=1, keepdims=True) * (1.0 / MLA_Q_LORA) + RMS_EPS)
    rkv = lax.rsqrt(jnp.mean(ckv * ckv, axis=1, keepdims=True) + RMS_EPS)
    return cq, ckv, kr, rq, rkv, cq * rq * gq_ref[...], ckv * rkv * gkv_ref[...]


def d_prep_fwd(name, s, proj, gq, gkv, wuq, wukv, tm_, tkr):
    cm, sam, sbm = tm_
    ck, sak, sbk = tkr

    def fn(i, j, dpart, cmv, samv, sbmv, ckv_, sakv, sbkv, gq_ref, gkv_ref, wuq_ref, wukv_ref, q_ref, k_ref, v_ref):
        cq, ckv, kr, rq, rkv, cqn, ckvn = _d_common(dpart, gq_ref, gkv_ref)
        qd = jnp.dot(cqn.astype(BF16), wuq_ref[...], preferred_element_type=F32)
        qf = _rope(qd, cmv, samv, sbmv, 8)
        kvd = jnp.dot(ckvn.astype(BF16), wukv_ref[...], preferred_element_type=F32)
        krr = _rope(kr, ckv_, sakv, sbkv, 8)
        for h in range(4):
            q_ref[h] = (qf[:, h * 128:(h + 1) * 128] * QSCALE_MLA).astype(BF16)
            k_ref[h] = (kvd[:, h * 128:(h + 1) * 128] + krr).astype(BF16)
            v_ref[h] = kvd[:, 512 + h * 64:512 + (h + 1) * 64].astype(BF16)

    ins = [('row', proj, 512, 3), ('row', cm, 512, 0), ('row', sam, 512, 0), ('row', sbm, 512, 0), ('row', ck, 128, 0),
           ('row', sak, 128, 0), ('row', sbk, 128, 0), ('full', gq), ('full', gkv), ('full', wuq), ('full', wukv)]
    return tiled_call(name, fn, s, TM, ins, [('hm', 4, 128, BF16), ('hm', 4, 128, BF16), ('hm', 4, 64, BF16)])


def d_prep_bwd(name, s, proj, gq, gkv, wuq, wukv, tm_, tkr, dq_hm, dk_hm, dv_hm):
    cm, sam, sbm = tm_
    ck, sak, sbk = tkr

    def fn(i, j, dpart, cmv, samv, sbmv, ckv_, sakv, sbkv, gq_ref, gkv_ref, wuq_ref, wukv_ref, dq_ref, dk_ref, dv_ref,
           dd_ref, dgq_ref, dgkv_ref, dwuq_ref, dwukv_ref):
        cq, ckv, kr, rq, rkv, cqn, ckvn = _d_common(dpart, gq_ref, gkv_ref)
        dqf = jnp.concatenate([dq_ref[h] for h in range(4)], axis=1) * QSCALE_MLA
        dkf = [dk_ref[h] for h in range(4)]
        dqd = _rope_t(dqf, cmv, samv, sbmv, 8).astype(BF16)
        dkvd = jnp.concatenate(dkf + [dv_ref[h] for h in range(4)], axis=1).astype(BF16)
        dkr = _rope_t(dkf[0] + dkf[1] + dkf[2] + dkf[3], ckv_, sakv, sbkv, 8)
        dcqn = _dot_nt(dqd, wuq_ref[...])
        dckvn = _dot_nt(dkvd, wukv_ref[...])
        dwuq_ref[...] += _dot_tn(cqn.astype(BF16), dqd)
        dwukv_ref[...] += _dot_tn(ckvn.astype(BF16), dkvd)
        gqv, gkvv = gq_ref[...], gkv_ref[...]
        dcq = gqv * rq * dcqn - cq * (rq * rq * rq) * (jnp.sum(cq * gqv * dcqn, axis=1, keepdims=True) * (1.0 / MLA_Q_LORA))
        dckv = gkvv * rkv * dckvn - ckv * (rkv * rkv * rkv) * jnp.mean(ckv * gkvv * dckvn, axis=1, keepdims=True)
        dgq_ref[...] += jnp.sum(cq * rq * dcqn, axis=0, keepdims=True)
        dgkv_ref[...] += jnp.sum(ckv * rkv * dckvn, axis=0, keepdims=True)
        dd_ref[...] = jnp.concatenate([dcq, dckv, dkr], axis=1).astype(BF16)

    ins = [('row', proj, 512, 3), ('row', cm, 512, 0), ('row', sam, 512, 0), ('row', sbm, 512, 0), ('row', ck, 128, 0),
           ('row', sak, 128, 0), ('row', sbk, 128, 0), ('full', gq), ('full', gkv), ('full', wuq), ('full', wukv),
           ('hm', dq_hm), ('hm', dk_hm), ('hm', dv_hm)]
    return tiled_call(name, fn, s, TM, ins, [('row', 512, 512, 0, BF16), ('acc', (1, 256)), ('acc', (1, 128)),
                                             ('acc', (256, 512)), ('acc', (128, 768))])


HALO_F = 8
TN_F = 1408
TM_F = 512
TM_FB = 256


def _conv3(ext, w_ref):
    return _roll_r(ext, -1) * w_ref[pl.ds(0, 1), :] + ext * w_ref[pl.ds(1, 1), :] + _roll_r(ext, 1) * w_ref[pl.ds(2, 1), :]


def f_fwd(name, s, upraw, cw, cb):
    ncj = D_FF // TN_F

    def fn(i, j, xa, xg, wa_ref, wg_ref, ba_ref, bg_ref, o_ref):
        ua = (_conv3(xa, wa_ref) + ba_ref[...])[HALO_F:HALO_F + TM_F]
        ug = (_conv3(xg, wg_ref) + bg_ref[...])[HALO_F:HALO_F + TM_F]
        o_ref[...] = (ua * _sigmoid(ua) * ug).astype(BF16)

    ins = [('halo', upraw, TN_F, lambda j: j, HALO_F), ('halo', upraw, TN_F, lambda j: j + ncj, HALO_F),
           ('col', cw, TN_F, lambda j: j), ('col', cw, TN_F, lambda j: j + ncj),
           ('col', cb, TN_F, lambda j: j), ('col', cb, TN_F, lambda j: j + ncj)]
    return tiled_call(name, fn, s, TM_F, ins, [('row', D_FF, TN_F, lambda j: j, BF16)], ncol=ncj)[0]


def f_bwd(name, s, upraw, dact, cw, cb):
    ncj = D_FF // TN_F

    def fn(i, j, xa, xg, da, wa_ref, wg_ref, ba_ref, bg_ref, dxa_ref, dxg_ref, dwa_ref, dwg_ref, dba_ref, dbg_ref):
        taps_a = (_roll_r(xa, -1), xa, _roll_r(xa, 1))
        taps_g = (_roll_r(xg, -1), xg, _roll_r(xg, 1))
        ua = sum(t * wa_ref[pl.ds(k, 1), :] for k, t in enumerate(taps_a)) + ba_ref[...]
        ug = sum(t * wg_ref[pl.ds(k, 1), :] for k, t in enumerate(taps_g)) + bg_ref[...]
        sg = _sigmoid(ua)
        dug = da * ua * sg
        dua = da * ug * sg * (1.0 + ua * (1.0 - sg))
        cen = slice(HALO_F, HALO_F + TM_FB)
        for du, taps, w_ref, dx_ref, dw_ref, db_ref in ((dua, taps_a, wa_ref, dxa_ref, dwa_ref, dba_ref),
                                                         (dug, taps_g, wg_ref, dxg_ref, dwg_ref, dbg_ref)):
            dx = _roll_r(du, 1) * w_ref[pl.ds(0, 1), :] + du * w_ref[pl.ds(1, 1), :] + _roll_r(du, -1) * w_ref[pl.ds(2, 1), :]
            dx_ref[...] = dx[cen].astype(BF16)
            duc = du[cen]
            for k in range(3):
                dw_ref[pl.ds(k, 1), :] += jnp.sum(duc * taps[k][cen], axis=0, keepdims=True)
            db_ref[...] += jnp.sum(duc, axis=0, keepdims=True)

    ins = [('halo', upraw, TN_F, lambda j: j, HALO_F), ('halo', upraw, TN_F, lambda j: j + ncj, HALO_F),
           ('halo', dact, TN_F, lambda j: j, HALO_F),
           ('col', cw, TN_F, lambda j: j), ('col', cw, TN_F, lambda j: j + ncj),
           ('col', cb, TN_F, lambda j: j), ('col', cb, TN_F, lambda j: j + ncj)]
    outs = [('row', D_FF, TN_F, lambda j: j, BF16), ('row', D_FF, TN_F, lambda j: j, BF16),
            ('colacc', 8, D_FF, TN_F, lambda j: j), ('colacc', 8, D_FF, TN_F, lambda j: j),
            ('colacc', 1, D_FF, TN_F, lambda j: j), ('colacc', 1, D_FF, TN_F, lambda j: j)]
    return tiled_call(name, fn, s, TM_FB, ins, outs, ncol=ncj)


def _pad_w_in(w):
    z = lambda n: jnp.zeros((w.shape[0], n), w.dtype)
    return jnp.concatenate([w[:, :1728], z(64), w[:, 1728:1856], z(64), w[:, 1856:1888], z(32)], axis=1)


def _unpad_w_in(g):
    return jnp.concatenate([g[:, :1728], g[:, 1792:1920], g[:, 1984:2016]], axis=1)


def _pad_uq(w):
    w = w.reshape(MLA_Q_LORA, 4, 96)
    w = jnp.concatenate([w, jnp.zeros((MLA_Q_LORA, 4, 32), w.dtype)], axis=2).reshape(MLA_Q_LORA, 512)
    return jnp.concatenate([w, jnp.zeros((64, 512), w.dtype)], axis=0)


def _unpad_uq(g):
    return g[:MLA_Q_LORA].reshape(MLA_Q_LORA, 4, 128)[:, :, :96].reshape(MLA_Q_LORA, 384)


def _pad_ukv(w):
    w = w.reshape(MLA_KV_LORA, 4, 128)
    kn = jnp.concatenate([w[:, :, :64], jnp.zeros((MLA_KV_LORA, 4, 64), w.dtype)], axis=2).reshape(MLA_KV_LORA, 512)
    return jnp.concatenate([kn, w[:, :, 64:].reshape(MLA_KV_LORA, 256)], axis=1)


def _unpad_ukv(g):
    kn = g[:, :512].reshape(MLA_KV_LORA, 4, 128)[:, :, :64]
    v = g[:, 512:].reshape(MLA_KV_LORA, 4, 64)
    return jnp.concatenate([kn, v], axis=2).reshape(MLA_KV_LORA, 512)


def _row(v, pad_to=None):
    v = v.reshape(1, -1)
    if pad_to is not None and v.shape[1] < pad_to:
        v = jnp.concatenate([v, jnp.zeros((1, pad_to - v.shape[1]), v.dtype)], axis=1)
    return v


LATE_NAMES = ['w_out', 'ffn_w_up', 'ffn_w_down']


def _assemble(n, part, g, me):
    return jnp.concatenate([jnp.where(me == q, part, g[q]) for q in range(N_CHIPS)], axis=SHARDED_AXIS[n])


def local_step(x, tgt, W, dist=None):
    s = x.shape[0]
    tabs = _rope_tables(s)
    gm = _group_mean_matrix(256, 64)
    alpha = DEEPNORM_ALPHA
    acts = []
    late = {(n, l): W[n][l] for n in LATE_NAMES for l in range(DEPTH)} if dist is None else {}
    h = ln_fwd("ln_in", s, x, _row(W['ln_in_g']), _row(W['ln_in_b']))
    for l in range(DEPTH):
        A = dict(h=h)
        A['wpad'] = _pad_w_in(W['w_in'][l]).astype(BF16)
        A['wuq'] = _pad_uq(W['mla_w_uq'][l]).astype(BF16)
        A['wukv'] = _pad_ukv(W['mla_w_ukv'][l]).astype(BF16)
        A['cw'] = jnp.concatenate([W['conv_a_w'][l], jnp.zeros((1, GROUP_W), F32)], axis=0)
        A['fcw'] = jnp.concatenate([W['ffn_conv_w'][l], jnp.zeros((5, 2 * D_FF), F32)], axis=0)
        A['gq'] = jnp.tile(_row(W['qk_norm_q'][l]), (1, 4))
        A['gk'] = jnp.tile(_row(W['qk_norm_k'][l]), (1, 2))
        A['sgu_bias'] = jnp.repeat(W['sgu_b'][l].T, 64, axis=1)
        A['sgu_wt'] = jnp.swapaxes(W['sgu_w'][l], 1, 2)
        proj = mm(f"proj{l}", h, A['wpad'])
        A['proj'] = proj
        A['y_a'], o_a = a_fwd(f"a_fwd{l}", s, proj, A['cw'], _row(W['conv_a_b'][l]), _row(W['ln_a_g'][l]), _row(W['ln_a_b'][l]))
        A['bq'], A['bk'], A['bv'] = b_prep_fwd(f"b_prep{l}", s, proj, A['gq'], A['gk'], tabs['gqa'], gm)
        if dist is not None and l == 0:
            lp = dist['late']
            parts = [lp['w_out'], lp['ffn_w_up'][0:1], lp['ffn_w_down'][0:1]]
            nxt_names, nxt = dist['next']
            nxt_parts = [a[1:2] for a in nxt]
            wanted = [lp['w_out'], (lp['ffn_w_up'], 0), (lp['ffn_w_down'], 0)] + [(a, 1) for a in nxt]
            (A['o_b'], A['lse_b']), got = attn_fwd(f"b_attn{l}", s, A['bq'], A['bk'], A['bv'], comm=gather_over_chips(wanted))
            full = [_assemble(n, p, g, dist['me']) for n, p, g in zip(LATE_NAMES, parts, got)]
            late[('w_out', 0)], late[('w_out', 1)] = full[0][0], full[0][1]
            late[('ffn_w_up', 0)], late[('ffn_w_down', 0)] = full[1][0], full[2][0]
            for n, p, g in zip(nxt_names, nxt_parts, got[len(parts):]):
                W[n][1] = _assemble(n, p, g, dist['me'])[0]
        else:
            A['o_b'], A['lse_b'] = attn_fwd(f"b_attn{l}", s, A['bq'], A['bk'], A['bv'])
        o_c = c_fwd(f"c_fwd{l}", s, proj, _row(W['sgu_ln_g'][l]), _row(W['sgu_ln_b'][l]), W['sgu_w'][l], A['sgu_bias'])
        A['dq'], A['dk'], A['dv'] = d_prep_fwd(f"d_prep{l}", s, proj, _row(W['mla_q_norm'][l], 256), _row(W['mla_kv_norm'][l]),
                                               A['wuq'], A['wukv'], tabs['mla'], tabs['kr'])
        if dist is not None and l == 0:
            parts = [lp['ffn_w_up'][1:2], lp['ffn_w_down'][1:2]]
            (A['o_d'], A['lse_d']), got = attn_fwd(f"d_attn{l}", s, A['dq'], A['dk'], A['dv'],
                                                   comm=gather_over_chips([(lp['ffn_w_up'], 1), (lp['ffn_w_down'], 1)]))
            late[('ffn_w_up', 1)] = _assemble('ffn_w_up', parts[0], got[0], dist['me'])[0]
            late[('ffn_w_down', 1)] = _assemble('ffn_w_down', parts[1], got[1], dist['me'])[0]
        else:
            A['o_d'], A['lse_d'] = attn_fwd(f"d_attn{l}", s, A['dq'], A['dk'], A['dv'])
        A['wout'] = late[('w_out', l)].astype(BF16)
        A['wup'] = late[('ffn_w_up', l)].astype(BF16)
        A['wdown'] = late[('ffn_w_down', l)].astype(BF16)
        A['mixcat'] = jnp.concatenate([o_a, A['o_b'], o_c, A['o_d']], axis=1).astype(BF16)
        A['mix'] = mm(f"out_proj{l}", A['mixcat'], A['wout'])
        h1 = ln_fwd(f"ln_mix{l}", s, A['mix'], _row(W['ln_mix_g'][l]), _row(W['ln_mix_b'][l]), res=h, alpha=alpha)
        A['h1'] = h1
        A['upraw'] = mm(f"ffn_up{l}", h1, A['wup'])
        A['act'] = f_fwd(f"f_fwd{l}", s, A['upraw'], A['fcw'], _row(W['ffn_conv_b'][l]))
        A['f'] = mm(f"ffn_down{l}", A['act'], A['wdown'])
        if l < DEPTH - 1:
            h = ln_fwd(f"ln_ffn{l}", s, A['f'], _row(W['ln_ffn_g'][l]), _row(W['ln_ffn_b'][l]), res=h1, alpha=alpha)
        acts.append(A)

    per_layer = {n: [None] * DEPTH for n in WEIGHT_NAMES if n not in ('ln_in_g', 'ln_in_b')}
    dh = None
    loss_row = None
    overlap = dist is not None and DEPTH == 2
    red = {}
    grp1 = [(n, 1) for n in BIG_NAMES]
    grp2 = [('ffn_w_up', 0), ('ffn_w_down', 0)]

    def carried(fn, name, *a, comm=None, **kw):
        if comm is None:
            return fn(name, *a, **kw), None
        return fn(name, *a, comm=comm, **kw)

    def pair_sums(tag, xs, r1):
        return [pair_sum(f"pair_sum_{tag}{t}", xv, rv, dist['idx'], BF16) for t, (xv, rv) in enumerate(zip(xs, r1))]

    def chip_sums(tag, cs, ps):
        return [chip_sum(f"chip_sum_{tag}{t}", [cv], [pv], dist['idx']) for t, (cv, pv) in enumerate(zip(cs, ps))]

    for l in reversed(range(DEPTH)):
        A = acts[l]
        carry = overlap and l == 0
        g_ffn, b_ffn = _row(W['ln_ffn_g'][l]), _row(W['ln_ffn_b'][l])
        if l == DEPTH - 1:
            dz, dg, db, loss_row = loss_and_ln_bwd("loss", s, A['f'], A['h1'], g_ffn, b_ffn, tgt, alpha)
        else:
            dz, dg, db = ln_bwd(f"ln_ffn_bwd{l}", s, dh, A['f'], g_ffn, res=A['h1'], alpha=alpha)
        per_layer['ln_ffn_g'][l], per_layer['ln_ffn_b'][l] = dg[0], db[0]
        xs1 = [per_layer[n][k] for n, k in grp1] if carry else None
        dwd, r1 = carried(mm, f"dw_down{l}", A['act'], dz, ta=True, comm=pair_exchange(xs1) if carry else None)
        per_layer['ffn_w_down'][l] = dwd.reshape(N_CHIPS, D_FF // N_CHIPS, D_MODEL)
        ps1 = pair_sums("g1_", xs1, r1) if carry else None
        dact = mm(f"d_act{l}", dz, A['wdown'], tb=True)
        dxa, dxg, dwa, dwg, dba, dbg = f_bwd(f"f_bwd{l}", s, A['upraw'], dact, A['fcw'], _row(W['ffn_conv_b'][l]))
        per_layer['ffn_conv_w'][l] = jnp.concatenate([dwa[:3], dwg[:3]], axis=1)
        per_layer['ffn_conv_b'][l] = jnp.concatenate([dba[0], dbg[0]], axis=0)
        dup = jnp.concatenate([dxa, dxg], axis=1)
        per_layer['ffn_w_up'][l] = mm(f"dw_up{l}", A['h1'], dup, ta=True, chip_major=True)
        xs2 = [per_layer[n][k] for n, k in grp2] if carry else None
        dh1_mm, r2 = carried(mm, f"d_h1{l}", dup, A['wup'], tb=True, comm=pair_exchange(xs2) if carry else None)
        ps2 = pair_sums("g2_", xs2, r2) if carry else None
        dz1, dg, db = _ln_mix_bwd(l, s, dz, dh1_mm, A, W, alpha)
        per_layer['ln_mix_g'][l], per_layer['ln_mix_b'][l] = dg[0], db[0]
        per_layer['w_out'][l] = mm(f"dw_out{l}", A['mixcat'], dz1, ta=True).reshape(N_CHIPS, D_MODEL // N_CHIPS, D_MODEL)
        dmixcat = mm(f"d_mixcat{l}", dz1, A['wout'], tb=True)
        dy_a, dg, db, dcb = a_bwd1(f"a_bwd1{l}", s, A['y_a'], dmixcat, _row(W['ln_a_g'][l]), _row(W['ln_a_b'][l]))
        per_layer['ln_a_g'][l], per_layer['ln_a_b'][l], per_layer['conv_a_b'][l] = dg[0], db[0], dcb[0]
        dp_a, dcw = a_bwd2(f"a_bwd2{l}", s, A['proj'], dy_a, A['cw'])
        per_layer['conv_a_w'][l] = dcw[:CONV_A_WIDTH]
        (dq, dk, dv), cs1 = carried(attn_bwd, f"b_attn_bwd{l}", s, A['bq'], A['bk'], A['bv'], A['o_b'], A['lse_b'], dmixcat, 2,
                                    comm=chip_exchange(ps1) if carry else None)
        halves1 = chip_sums("g1_", cs1, ps1) if carry else None
        dp_b, dgq, dgk = b_prep_bwd(f"b_prep_bwd{l}", s, A['proj'], A['gq'], A['gk'], tabs['gqa'], gm, dq, dk, dv)
        per_layer['qk_norm_q'][l], per_layer['qk_norm_k'][l] = dgq[0], dgk[0]
        dp_c, dg, db, dsw, dsb = c_bwd(f"c_bwd{l}", s, A['proj'], _row(W['sgu_ln_g'][l]), _row(W['sgu_ln_b'][l]), W['sgu_w'][l],
                                       A['sgu_wt'], A['sgu_bias'], dmixcat)
        per_layer['sgu_ln_g'][l], per_layer['sgu_ln_b'][l] = dg[0], db[0]
        per_layer['sgu_w'][l], per_layer['sgu_b'][l] = dsw, dsb[:, :4].T
        comms = [half_exchange(halves1), chip_exchange(ps2)] if carry else []
        (dq, dk, dv), cres = carried(attn_bwd, f"d_attn_bwd{l}", s, A['dq'], A['dk'], A['dv'], A['o_d'], A['lse_d'], dmixcat, 6,
                                     comm=join_comms(comms))
        if carry:
            done1, cs2 = split_comm_results(comms, cres)
            red.update(zip(grp1, done1))
            halves2 = chip_sums("g2_", cs2, ps2)
        dp_d, dgq, dgkv, dwuq, dwukv = d_prep_bwd(f"d_prep_bwd{l}", s, A['proj'], _row(W['mla_q_norm'][l], 256),
                                                  _row(W['mla_kv_norm'][l]), A['wuq'], A['wukv'], tabs['mla'], tabs['kr'],
                                                  dq, dk, dv)
        per_layer['mla_q_norm'][l], per_layer['mla_kv_norm'][l] = dgq[0, :MLA_Q_LORA], dgkv[0]
        per_layer['mla_w_uq'][l], per_layer['mla_w_ukv'][l] = _unpad_uq(dwuq), _unpad_ukv(dwukv)
        dproj = jnp.concatenate([dp_a, dp_b, dp_c, dp_d], axis=1)
        dw_in, done2 = carried(mm, f"dw_in{l}", A['h'], dproj, ta=True, comm=half_exchange(halves2) if carry else None)
        if carry:
            red.update(zip(grp2, done2))
        per_layer['w_in'][l] = _unpad_w_in(dw_in).reshape(D_MODEL, N_CHIPS, D_IN_PROJ // N_CHIPS).transpose(1, 0, 2)
        dh = mm(f"d_h{l}", dproj, A['wpad'], tb=True, add=dz1, add_scale=alpha)
    dx, dg, db = ln_bwd("ln_in_bwd", s, dh, x, _row(W['ln_in_g']))
    G = dict(per_layer)
    G['ln_in_g'], G['ln_in_b'] = dg[0], db[0]
    for n, k in red:
        G[n][k] = None
    G['reduced'] = red
    return loss_row, dx, G


def _ln_mix_bwd(l, s, dz, dh1_mm, A, W, alpha):
    ins = [('row', dz, D_MODEL, 0), ('row', dh1_mm, D_MODEL, 0), ('row', A['mix'], D_MODEL, 0), ('row', A['h'], D_MODEL, 0),
           ('full', _row(W['ln_mix_g'][l]))]

    def fn(i, j, dzv, dmv, xv, rv, g_ref, o_ref, dg_ref, db_ref):
        xh, r = _ln_stats(alpha * rv + xv)
        d, dg, db = _ln_bwd(alpha * dzv + dmv, xh, r, g_ref[...])
        o_ref[...] = d
        dg_ref[...] += dg
        db_ref[...] += db

    return tiled_call(f"ln_mix_bwd{l}", fn, s, TM, ins, [('row', D_MODEL, D_MODEL, 0, F32), ('acc', (1, D_MODEL)),
                                                         ('acc', (1, D_MODEL))])


ANY = pl.BlockSpec(memory_space=pl.ANY)


def _pos():
    return lax.axis_index("x"), lax.axis_index("y"), lax.axis_index("c")


def _rcopy(src, dst, ssem, rsem, dev):
    return pltpu.make_async_remote_copy(src_ref=src, dst_ref=dst, send_sem=ssem, recv_sem=rsem, device_id=dev,
                                        device_id_type=MESH)


class Comm:
    def __init__(self, operands, out_shape, sems, start, finish, aliases=None):
        self.operands, self.out_shape, self.sems = list(operands), list(out_shape), list(sems)
        self.start, self.finish, self.aliases = start, finish, dict(aliases or {})


def join_comms(comms):
    comms = [cm for cm in comms if cm is not None]
    if not comms:
        return None
    offs, oi, oo, os_ = [], 0, 0, 0
    for cm in comms:
        offs.append((oi, oo, os_))
        oi, oo, os_ = oi + len(cm.operands), oo + len(cm.out_shape), os_ + len(cm.sems)

    def part(fn_name):
        def run(ins, outs, sems):
            for cm, (a, b, s) in zip(comms, offs):
                getattr(cm, fn_name)(ins[a:a + len(cm.operands)], outs[b:b + len(cm.out_shape)], sems[s:s + len(cm.sems)])
        return run

    aliases = {a + i: b + o for cm, (a, b, s) in zip(comms, offs) for i, o in cm.aliases.items()}
    return Comm([v for cm in comms for v in cm.operands], [v for cm in comms for v in cm.out_shape],
                [v for cm in comms for v in cm.sems], part("start"), part("finish"), aliases)


def split_comm_results(comms, res):
    out, p = [], 0
    for cm in comms:
        if cm is None:
            out.append(None)
        else:
            out.append(list(res[p:p + len(cm.out_shape)]))
            p += len(cm.out_shape)
    return out


def run_comm(name, comm):
    n_in, n_out = len(comm.operands), len(comm.out_shape)

    def body(*refs):
        ins, outs, sems = refs[:n_in], refs[n_in:n_in + n_out], refs[n_in + n_out:]
        comm.start(ins, outs, sems)
        comm.finish(ins, outs, sems)

    return pl.pallas_call(body, name=name, in_specs=[ANY] * n_in, out_specs=[ANY] * n_out, out_shape=comm.out_shape,
                          scratch_shapes=[pltpu.SemaphoreType.DMA((k,)) for k in comm.sems],
                          input_output_aliases=comm.aliases)(*comm.operands)


def gather_over_chips(parts):
    n = len(parts)
    layer = [p[1] if isinstance(p, tuple) else None for p in parts]
    parts = [p[0] if isinstance(p, tuple) else p for p in parts]
    shapes = [p.shape if l is None else (1,) + p.shape[1:] for p, l in zip(parts, layer)]
    split = [shp[1] % 32 == 0 for shp in shapes]

    def geometry():
        x, y, c = _pos()
        return x, y, c, 2 * x + y, [(1 - x, y), (x, 1 - y), (1 - x, 1 - y)]

    def rows(t, half):
        a2 = shapes[t][1] // 2
        return pl.ds(half * a2, a2)

    def ici(srcs, dsts, sems, j, t, px, py, c, me):
        k = n * j + t
        src = srcs[t] if layer[t] is None else srcs[t].at[pl.ds(layer[t], 1)]
        if split[t]:
            return _rcopy(src.at[:, rows(t, c), :], dsts[t].at[me, :, rows(t, c), :], sems[0].at[k], sems[1].at[k], (px, py, c))
        return _rcopy(src, dsts[t].at[me], sems[0].at[k], sems[1].at[k], (px, py, c))

    def start(srcs, dsts, sems):
        x, y, c, me, chips = geometry()
        for j, (px, py) in enumerate(chips):
            for t in range(n):
                ici(srcs, dsts, sems, j, t, px, py, c, me).start()

    def finish(srcs, dsts, sems):
        x, y, c, me, chips = geometry()
        fwd = []
        for j, (px, py) in enumerate(chips):
            q = 2 * px + py
            for t in range(n):
                k = n * j + t
                if split[t]:
                    got = dsts[t].at[q, :, rows(t, c), :]
                    _rcopy(got, got, sems[0].at[k], sems[1].at[k], (px, py, c)).wait_recv()
                    fw = _rcopy(got, got, sems[2].at[k], sems[3].at[k], (x, y, 1 - c))
                    fw.start()
                    fwd.append(fw)
                else:
                    _rcopy(dsts[t].at[q], dsts[t].at[q], sems[0].at[k], sems[1].at[k], (px, py, c)).wait_recv()
        for j, (px, py) in enumerate(chips):
            q = 2 * px + py
            for t in range(n):
                if split[t]:
                    other = dsts[t].at[q, :, rows(t, 1 - c), :]
                    _rcopy(other, other, sems[2].at[n * j + t], sems[3].at[n * j + t], (x, y, 1 - c)).wait_recv()
                ici(srcs, dsts, sems, j, t, px, py, c, me).wait_send()
        for fw in fwd:
            fw.wait_send()

    return Comm(parts, [jax.ShapeDtypeStruct((N_CHIPS,) + shp, p.dtype) for p, shp in zip(parts, shapes)], [3 * n] * 4, start, finish)


def pair_exchange(xs):
    n = len(xs)

    def copies(srcs, dsts, sems):
        x, y, c = _pos()
        return [_rcopy(srcs[t].at[:, pl.ds((1 - c) * (xs[t].shape[1] // 2), xs[t].shape[1] // 2), :], dsts[t],
                       sems[0].at[t], sems[1].at[t], (x, y, 1 - c)) for t in range(n)]

    def start(srcs, dsts, sems):
        for cp in copies(srcs, dsts, sems):
            cp.start()

    def finish(srcs, dsts, sems):
        for cp in copies(srcs, dsts, sems):
            cp.wait()

    return Comm(xs, [jax.ShapeDtypeStruct((a.shape[0], a.shape[1] // 2, a.shape[2]), a.dtype) for a in xs], [n, n], start, finish)


def chip_exchange(ps):
    n = len(ps)

    def geometry():
        x, y, c = _pos()
        return c, 2 * x + y, [(1 - x, y), (x, 1 - y), (1 - x, 1 - y)]

    def start(srcs, dsts, sems):
        c, me, chips = geometry()
        for j, (px, py) in enumerate(chips):
            for t in range(n):
                _rcopy(srcs[t].at[2 * px + py], dsts[t].at[me], sems[0].at[n * j + t], sems[1].at[n * j + t], (px, py, c)).start()

    def finish(srcs, dsts, sems):
        c, me, chips = geometry()
        for j, (px, py) in enumerate(chips):
            for t in range(n):
                _rcopy(srcs[t].at[2 * px + py], dsts[t].at[2 * px + py], sems[0].at[n * j + t], sems[1].at[n * j + t], (px, py, c)).wait()

    return Comm(ps, [jax.ShapeDtypeStruct(p.shape, p.dtype) for p in ps], [3 * n, 3 * n], start, finish)


def half_exchange(bufs):
    n = len(bufs)

    def copies(ins, outs, sems):
        x, y, c = _pos()
        cps = []
        for t in range(n):
            a2 = bufs[t].shape[1] // 2
            mine = pl.ds(c * a2, a2)
            cps.append(_rcopy(ins[t].at[:, mine, :], outs[t].at[:, mine, :], sems[0].at[t], sems[1].at[t], (x, y, 1 - c)))
        return cps

    def start(ins, outs, sems):
        for cp in copies(ins, outs, sems):
            cp.start()

    def finish(ins, outs, sems):
        for cp in copies(ins, outs, sems):
            cp.wait()

    return Comm(bufs, [jax.ShapeDtypeStruct(b.shape, b.dtype) for b in bufs], [n, n], start, finish, {t: t for t in range(n)})


LANES = 1024
ROW_TILE_BYTES = 2 * 1024 * 1024


def _row_tile(r, row_bytes):
    if r * row_bytes <= ROW_TILE_BYTES:
        return r
    best = None
    for t in range(16, r, 16):
        if r % t == 0 and t * row_bytes <= ROW_TILE_BYTES:
            best = t
    assert best is not None, (r, row_bytes)
    return best


def pair_sum(name, x, r1, c_arr, out_dtype):
    q, a2, b = r1.shape
    tr = _row_tile(a2, b * 4)
    nb = a2 // tr

    def body(c_ref, x_ref, r_ref, o_ref):
        o_ref[...] = (x_ref[...] + r_ref[...]).astype(out_dtype)

    grid_spec = pltpu.PrefetchScalarGridSpec(
        num_scalar_prefetch=1, grid=(q, nb),
        in_specs=[pl.BlockSpec((None, tr, b), lambda k, i, c_ref: (k, c_ref[0] * nb + i, 0)),
                  pl.BlockSpec((None, tr, b), lambda k, i, c_ref: (k, i, 0))],
        out_specs=pl.BlockSpec((None, tr, b), lambda k, i, c_ref: (k, i, 0)))
    return pl.pallas_call(body, name=name, grid_spec=grid_spec, out_shape=jax.ShapeDtypeStruct(r1.shape, out_dtype),
                          compiler_params=_cparams(("parallel", "parallel")))(c_arr, x, r1)


def chip_sum(name, recv, own, idx):
    nl = len(recv)
    k, a2, b = recv[0].shape
    tr = _row_tile(a2, k * b * recv[0].dtype.itemsize)
    nb = a2 // tr

    def body(idx_ref, *refs):
        rs, os_, o_ref = refs[:nl], refs[nl:2 * nl], refs[2 * nl]
        l = pl.program_id(0)
        me = idx_ref[1]
        for li in range(nl):
            @pl.when(l == li)
            def _(li=li):
                acc = None
                for q in range(k):
                    term = jnp.where(me == q, os_[li][...], rs[li][q]).astype(F32)
                    acc = term if acc is None else acc + term
                o_ref[...] = acc

    def pick(li):
        return lambda l, i, idx_ref: jnp.where(l == li, i, 0)

    in_specs = [pl.BlockSpec((k, tr, b), lambda l, i, idx_ref, _p=pick(li): (0, _p(l, i, idx_ref), 0)) for li in range(nl)]
    in_specs += [pl.BlockSpec((None, tr, b), lambda l, i, idx_ref, _p=pick(li): (idx_ref[1], _p(l, i, idx_ref), 0)) for li in range(nl)]
    grid_spec = pltpu.PrefetchScalarGridSpec(
        num_scalar_prefetch=1, grid=(nl, nb), in_specs=in_specs,
        out_specs=pl.BlockSpec((None, tr, b), lambda l, i, idx_ref: (l, idx_ref[0] * nb + i, 0)))
    return pl.pallas_call(body, name=name, grid_spec=grid_spec, out_shape=jax.ShapeDtypeStruct((nl, 2 * a2, b), F32),
                          compiler_params=_cparams(("arbitrary", "arbitrary")))(idx, *recv, *own)


def adamw(name, gs, w, m, v):
    nl, r, b = w.shape
    assert len(gs) == nl
    tr = _row_tile(r, b * 4)

    def body(*refs):
        g_refs, (w_ref, m_ref, v_ref, go_ref, d_ref, nm_ref, nv_ref) = refs[:nl], refs[nl:]
        l = pl.program_id(0)
        for li in range(nl):
            @pl.when(l == li)
            def _(li=li):
                gv = g_refs[li][...]
                go_ref[...] = gv
                mn = ADAM_B1 * m_ref[...] + (1.0 - ADAM_B1) * gv
                vn = ADAM_B2 * v_ref[...] + (1.0 - ADAM_B2) * (gv * gv)
                m_hat = mn / (1.0 - ADAM_B1 ** ADAM_STEP)
                v_hat = vn / (1.0 - ADAM_B2 ** ADAM_STEP)
                d_ref[...] = -ADAM_LR * (m_hat / (jnp.sqrt(v_hat) + ADAM_EPS) + ADAM_WD * w_ref[...])
                nm_ref[...] = mn
                nv_ref[...] = vn

    g_specs = [pl.BlockSpec((None, tr, b), lambda l, i, _li=li: (0, jnp.where(l == _li, i, 0), 0)) for li in range(nl)]
    spec = pl.BlockSpec((None, tr, b), lambda l, i: (l, i, 0))
    sds = jax.ShapeDtypeStruct(w.shape, F32)
    return pl.pallas_call(body, name=name, grid=(nl, r // tr), in_specs=g_specs + [spec] * 3, out_specs=[spec] * 4,
                          out_shape=[sds] * 4, compiler_params=_cparams(("arbitrary", "arbitrary")))(*gs, w, m, v)


def adamw_many(name, gs, ws, ms, vs):
    n = len(gs)

    def body(*refs):
        for k in range(n):
            g_ref, w_ref, m_ref, v_ref, go_ref, d_ref, nm_ref, nv_ref = (refs[j * n + k] for j in range(8))
            gv = g_ref[...]
            go_ref[...] = gv
            mn = ADAM_B1 * m_ref[...] + (1.0 - ADAM_B1) * gv
            vn = ADAM_B2 * v_ref[...] + (1.0 - ADAM_B2) * (gv * gv)
            m_hat = mn / (1.0 - ADAM_B1 ** ADAM_STEP)
            v_hat = vn / (1.0 - ADAM_B2 ** ADAM_STEP)
            d_ref[...] = -ADAM_LR * (m_hat / (jnp.sqrt(v_hat) + ADAM_EPS) + ADAM_WD * w_ref[...])
            nm_ref[...] = mn
            nv_ref[...] = vn

    vm = pl.BlockSpec(memory_space=pltpu.VMEM)
    res = pl.pallas_call(body, name=name, in_specs=[vm] * (4 * n), out_specs=[vm] * (4 * n),
                         out_shape=[jax.ShapeDtypeStruct(a.shape, F32) for _ in range(4) for a in ws],
                         compiler_params=pltpu.CompilerParams(vmem_limit_bytes=VMEM_LIMIT))(*gs, *ws, *ms, *vs)
    return [list(res[j * n:(j + 1) * n]) for j in range(4)]


def _pack_rows(shapes):
    return [8 * -(-int(np.prod(s)) // (8 * LANES)) for s in shapes]


def _pack(arrs, rows):
    pieces = []
    for a, r in zip(arrs, _pack_rows([a.shape for a in arrs])):
        flat = a.reshape(-1)
        pieces.append(jnp.pad(flat, (0, r * LANES - flat.shape[0])).reshape(r, LANES))
    used = sum(p.shape[0] for p in pieces)
    assert used <= rows
    if used < rows:
        pieces.append(jnp.zeros((rows - used, LANES), F32))
    return jnp.concatenate(pieces, axis=0)


def _unpack(pack, shapes):
    out, off = [], 0
    for shp, r in zip(shapes, _pack_rows(shapes)):
        n = int(np.prod(shp))
        out.append(pack[off:off + r].reshape(-1)[:n].reshape(shp))
        off += r
    return out


BIG_NAMES = ['w_in', 'w_out', 'ffn_w_up', 'ffn_w_down']
MED_NAMES = [n for n in SHARDED_NAMES if n not in BIG_NAMES]
MED_ROWS = 96
SMALL_ROWS = 288


def kernel(x, ln_in_g, ln_in_b, w_in, conv_a_w, conv_a_b, ln_a_g, ln_a_b, qk_norm_q, qk_norm_k, sgu_ln_g, sgu_ln_b, sgu_w, sgu_b, mla_q_norm, mla_w_uq, mla_kv_norm, mla_w_ukv, w_out, ln_mix_g, ln_mix_b, ffn_w_up, ffn_conv_w, ffn_conv_b, ffn_w_down, ln_ffn_g, ln_ffn_b, loss_target, m_ln_in_g, m_ln_in_b, m_w_in, m_conv_a_w, m_conv_a_b, m_ln_a_g, m_ln_a_b, m_qk_norm_q, m_qk_norm_k, m_sgu_ln_g, m_sgu_ln_b, m_sgu_w, m_sgu_b, m_mla_q_norm, m_mla_w_uq, m_mla_kv_norm, m_mla_w_ukv, m_w_out, m_ln_mix_g, m_ln_mix_b, m_ffn_w_up, m_ffn_conv_w, m_ffn_conv_b, m_ffn_w_down, m_ln_ffn_g, m_ln_ffn_b, v_ln_in_g, v_ln_in_b, v_w_in, v_conv_a_w, v_conv_a_b, v_ln_a_g, v_ln_a_b, v_qk_norm_q, v_qk_norm_k, v_sgu_ln_g, v_sgu_ln_b, v_sgu_w, v_sgu_b, v_mla_q_norm, v_mla_w_uq, v_mla_kv_norm, v_mla_w_ukv, v_w_out, v_ln_mix_g, v_ln_mix_b, v_ffn_w_up, v_ffn_conv_w, v_ffn_conv_b, v_ffn_w_down, v_ln_ffn_g, v_ln_ffn_b):
    loc = dict(locals())
    w_loc = {n: loc[n] for n in WEIGHT_NAMES}
    m_loc = {n: loc["m_" + n] for n in WEIGHT_NAMES}
    v_loc = {n: loc["v_" + n] for n in WEIGHT_NAMES}
    me_chip = 2 * lax.axis_index("x") + lax.axis_index("y")
    idx = jnp.stack([lax.axis_index("c"), me_chip]).astype(jnp.int32)

    def wire(n):
        return w_loc[n].astype(BF16) if n in BF16_WIRE else w_loc[n]

    assert DEPTH == 2
    early = [n for n in SHARDED_NAMES if n not in LATE_NAMES]
    wired = {n: wire(n) for n in SHARDED_NAMES}
    W = {n: w_loc[n] for n in REPL_NAMES}
    for n, g in zip(early, run_comm("gather_early", gather_over_chips([(wired[n], 0) for n in early]))):
        W[n] = [_assemble(n, wired[n][0:1], g, me_chip)[0], None]

    dist = dict(idx=idx, me=me_chip, late={n: wired[n] for n in LATE_NAMES}, next=(early, [wired[n] for n in early]))
    loss_row, dx, G = local_step(x[0], loss_target[0], W, dist)
    loss = lax.psum(loss_row[0, 0], ("x", "y", "c"))
    red = G['reduced']

    med = []
    for q in range(N_CHIPS):
        pieces = [lax.slice_in_dim(jnp.stack(G[n]), q * w_loc[n].shape[2], (q + 1) * w_loc[n].shape[2], axis=2) for n in MED_NAMES]
        med.append(_pack(pieces, MED_ROWS))
    rest = [(n, l) for n in BIG_NAMES for l in range(DEPTH) if (n, l) not in red]
    small_pack = _pack([jnp.stack(G[n]) if isinstance(G[n], list) else G[n] for n in REPL_NAMES], SMALL_ROWS)
    xs = [G[n][l] for n, l in rest] + [jnp.stack(med), jnp.broadcast_to(small_pack[None], (N_CHIPS, SMALL_ROWS, LANES))]
    wire_dt = [BF16] * (len(xs) - 1) + [F32]
    r1 = run_comm("pair_exchange", pair_exchange(xs))
    ps = [pair_sum(f"pair_sum{t}", xv, rv, idx, dt) for t, (xv, rv, dt) in enumerate(zip(xs, r1, wire_dt))]
    cs = run_comm("chip_exchange", chip_exchange(ps))
    halves = [chip_sum(f"chip_sum{t}", [cv], [pv], idx) for t, (cv, pv) in enumerate(zip(cs, ps))]
    done = run_comm("half_exchange", half_exchange(halves))
    red.update(zip(rest, done[:-2]))
    g_med, g_small = done[-2], done[-1]

    outs = {k: {} for k in ("grad", "delta", "new_m", "new_v")}
    for n in BIG_NAMES:
        res = adamw(f"adamw_{n}", [red[(n, l)] for l in range(DEPTH)], w_loc[n], m_loc[n], v_loc[n])
        for k, v in zip(("grad", "delta", "new_m", "new_v"), res):
            outs[k][n] = v
    names = MED_NAMES + REPL_NAMES
    g_nat = _unpack(g_med[0], [w_loc[n].shape for n in MED_NAMES]) + _unpack(g_small[0], [w_loc[n].shape for n in REPL_NAMES])

    def two(a):
        return a.reshape(1, -1) if a.ndim == 1 else a

    res = adamw_many("adamw_small", [two(g) for g in g_nat], *[[two(d[n]) for n in names] for d in (w_loc, m_loc, v_loc)])
    for k, vals in zip(("grad", "delta", "new_m", "new_v"), res):
        outs[k].update({n: v.reshape(w_loc[n].shape) for n, v in zip(names, vals)})
    return (loss, dx[None], *[outs["grad"][n] for n in WEIGHT_NAMES], *[outs["delta"][n] for n in WEIGHT_NAMES],
            *[outs["new_m"][n] for n in WEIGHT_NAMES], *[outs["new_v"][n] for n in WEIGHT_NAMES])
```

```python
import functools
import math

import jax
import jax.numpy as jnp
import numpy as np
from jax import lax
from jax.experimental import pallas as pl
from jax.experimental.pallas import tpu as pltpu

F32 = jnp.float32
BF16 = jnp.bfloat16

D_MODEL = 1024
DEPTH = 2
GRID_W = 64
GROUP_W = 256
HEAD_DIM = 64
CONV_A_WIDTH = 31
CHUNK = 128
MLA_Q_LORA = 192
MLA_KV_LORA = 128
MLA_NOPE = 64
MLA_ROPE = 32
D_FF = 2816
ROPE_THETA = 10000.0
DEEPNORM_ALPHA = (2 * DEPTH) ** 0.25
LN_EPS = 1e-5
RMS_EPS = 1e-6
D_IN_PROJ = 1888
PROJ_W = 2048

ADAM_LR = 0.001
ADAM_B1 = 0.9
ADAM_B2 = 0.999
ADAM_EPS = 1e-08
ADAM_WD = 0.01
ADAM_STEP = 10

VMEM_LIMIT = 56 * 1024 * 1024
MESH = pl.DeviceIdType.MESH
N_CHIPS = 4
N_DEV = 8

WEIGHT_NAMES = ['ln_in_g', 'ln_in_b', 'w_in', 'conv_a_w', 'conv_a_b', 'ln_a_g', 'ln_a_b', 'qk_norm_q', 'qk_norm_k',
                'sgu_ln_g', 'sgu_ln_b', 'sgu_w', 'sgu_b', 'mla_q_norm', 'mla_w_uq', 'mla_kv_norm', 'mla_w_ukv', 'w_out',
                'ln_mix_g', 'ln_mix_b', 'ffn_w_up', 'ffn_conv_w', 'ffn_conv_b', 'ffn_w_down', 'ln_ffn_g', 'ln_ffn_b']
SHARDED_AXIS = {'w_in': 2, 'conv_a_w': 2, 'mla_w_uq': 2, 'mla_w_ukv': 2, 'w_out': 1, 'ffn_w_up': 2, 'ffn_conv_w': 2,
                'ffn_w_down': 1}
SHARDED_NAMES = [n for n in WEIGHT_NAMES if n in SHARDED_AXIS]
REPL_NAMES = [n for n in WEIGHT_NAMES if n not in SHARDED_AXIS]
BF16_WIRE = ('w_in', 'mla_w_uq', 'mla_w_ukv', 'w_out', 'ffn_w_up', 'ffn_w_down')


def _cparams(sem):
    return pltpu.CompilerParams(dimension_semantics=sem, vmem_limit_bytes=VMEM_LIMIT)


def _pick(n, cands):
    for c in cands:
        if n % c == 0:
            return c
    return n


def carried_call(name, body, grid, in_specs, out_specs, out_shape, scratch, semantics, args, comm=None):
    if comm is None:
        res = pl.pallas_call(body, name=name, grid=grid, in_specs=in_specs, out_specs=out_specs, out_shape=out_shape,
                             scratch_shapes=scratch, compiler_params=_cparams(semantics))(*args)
        return list(res), None
    n_in, n_out, n_scr = len(in_specs), len(out_specs), len(scratch)
    nci, nco = len(comm.operands), len(comm.out_shape)

    def carried(*refs):
        o0 = n_in + nci
        s0 = o0 + n_out + nco
        ins, cins = refs[:n_in], refs[n_in:o0]
        outs, couts = refs[o0:o0 + n_out], refs[o0 + n_out:s0]
        scr, sems = refs[s0:s0 + n_scr], refs[s0 + n_scr:]
        ids = [pl.program_id(d) for d in range(len(grid))]
        first = functools.reduce(lambda u, v: u & v, [i == 0 for i in ids])
        last = functools.reduce(lambda u, v: u & v, [i == g - 1 for i, g in zip(ids, grid)])

        @pl.when(first)
        def _():
            comm.start(cins, couts, sems)

        body(*ins, *outs, *scr)

        @pl.when(last)
        def _():
            comm.finish(cins, couts, sems)

    res = pl.pallas_call(
        carried, name=name, grid=grid, in_specs=list(in_specs) + [ANY] * nci, out_specs=list(out_specs) + [ANY] * nco,
        out_shape=list(out_shape) + comm.out_shape, scratch_shapes=list(scratch) + [pltpu.SemaphoreType.DMA((k,)) for k in comm.sems],
        input_output_aliases={n_in + i: n_out + o for i, o in comm.aliases.items()},
        compiler_params=_cparams(("arbitrary",) * len(grid)))(*args, *comm.operands)
    return list(res[:n_out]), list(res[n_out:])


def mm(name, a, b, *, ta=False, tb=False, add=None, add_scale=1.0, out_dtype=F32, chip_major=False, comm=None):
    m, k = (a.shape[1], a.shape[0]) if ta else a.shape
    n = b.shape[0] if tb else b.shape[1]
    assert (b.shape[1] if tb else b.shape[0]) == k
    tm = _pick(m, (1024, 1408, 512, 256, 128))
    tn = n // N_CHIPS if chip_major else _pick(n, (1024, 1408, 512, 256, 128))
    tk = _pick(k, (1024, 1408, 512, 256, 128))
    nk = k // tk
    a_spec = pl.BlockSpec((tk, tm), lambda i, j, kk: (kk, i)) if ta else pl.BlockSpec((tm, tk), lambda i, j, kk: (i, kk))
    b_spec = pl.BlockSpec((tn, tk), lambda i, j, kk: (j, kk)) if tb else pl.BlockSpec((tk, tn), lambda i, j, kk: (kk, j))
    in_specs = [a_spec, b_spec]
    args = [a, b]
    if add is not None:
        in_specs.append(pl.BlockSpec((tm, tn), lambda i, j, kk: (i, j)))
        args.append(add)
    dims = (((0 if ta else 1,), (1 if tb else 0,)), ((), ()))

    def body(*refs):
        a_ref, b_ref = refs[0], refs[1]
        add_ref = refs[2] if add is not None else None
        o_ref, acc_ref = refs[-2], refs[-1]
        kk = pl.program_id(2)
        part = lax.dot_general(a_ref[...].astype(BF16), b_ref[...].astype(BF16), dims, preferred_element_type=F32)

        @pl.when(kk == 0)
        def _():
            acc_ref[...] = part

        @pl.when(kk > 0)
        def _():
            acc_ref[...] += part

        @pl.when(kk == nk - 1)
        def _():
            r = acc_ref[...]
            if add_ref is not None:
                r = r + add_scale * add_ref[...].astype(F32)
            o_ref[...] = r.astype(out_dtype)

    if chip_major:
        out_spec = pl.BlockSpec((None, tm, tn), lambda i, j, kk: (j, i, 0))
        out_shape = jax.ShapeDtypeStruct((N_CHIPS, m, tn), out_dtype)
    else:
        out_spec = pl.BlockSpec((tm, tn), lambda i, j, kk: (i, j))
        out_shape = jax.ShapeDtypeStruct((m, n), out_dtype)
    res, cres = carried_call(name, body, (m // tm, n // tn, nk), in_specs, [out_spec], [out_shape], [pltpu.VMEM((tm, tn), F32)],
                             ("parallel", "parallel", "arbitrary"), args, comm)
    return res[0] if comm is None else (res[0], cres)


def _cb(cb):
    return cb if callable(cb) else (lambda j, _c=cb: _c)


def tiled_call(name, fn, s, tm, ins, outs, ncol=1):
    nrow = s // tm
    in_specs, args, kinds = [], [], []
    for it in ins:
        kind = it[0]
        if kind == 'row':
            _, arr, w, cb = it
            in_specs.append(pl.BlockSpec((tm, w), lambda j, i, _c=_cb(cb): (i, _c(j))))
            args.append(arr)
            kinds.append(('row',))
        elif kind == 'halo':
            _, arr, w, cb, h = it
            r = tm // h
            nh = s // h
            in_specs.append(pl.BlockSpec((h, w), lambda j, i, _c=_cb(cb), _r=r: (jnp.maximum(i * _r - 1, 0), _c(j))))
            in_specs.append(pl.BlockSpec((tm, w), lambda j, i, _c=_cb(cb): (i, _c(j))))
            in_specs.append(pl.BlockSpec((h, w), lambda j, i, _c=_cb(cb), _r=r, _n=nh: (jnp.minimum((i + 1) * _r, _n - 1), _c(j))))
            args += [arr, arr, arr]
            kinds.append(('halo',))
        elif kind == 'full':
            arr = it[1]
            in_specs.append(pl.BlockSpec(arr.shape, lambda j, i, _n=arr.ndim: (0,) * _n))
            args.append(arr)
            kinds.append(('ref',))
        elif kind == 'col':
            _, arr, w, cb = it
            in_specs.append(pl.BlockSpec((arr.shape[0], w), lambda j, i, _c=_cb(cb): (0, _c(j))))
            args.append(arr)
            kinds.append(('ref',))
        elif kind == 'hm':
            arr = it[1]
            in_specs.append(pl.BlockSpec((arr.shape[0], tm, arr.shape[2]), lambda j, i: (0, i, 0)))
            args.append(arr)
            kinds.append(('ref',))
        else:
            raise ValueError(kind)
    out_specs, out_shapes, okinds = [], [], []
    for ot in outs:
        kind = ot[0]
        if kind == 'row':
            _, wt, w, cb, dt = ot
            out_specs.append(pl.BlockSpec((tm, w), lambda j, i, _c=_cb(cb): (i, _c(j))))
            out_shapes.append(jax.ShapeDtypeStruct((s, wt), dt))
            okinds.append('row')
        elif kind == 'hm':
            _, hh, d, dt = ot
            out_specs.append(pl.BlockSpec((hh, tm, d), lambda j, i: (0, i, 0)))
            out_shapes.append(jax.ShapeDtypeStruct((hh, s, d), dt))
            okinds.append('hm')
        elif kind == 'acc':
            shape = ot[1]
            out_specs.append(pl.BlockSpec(shape, lambda j, i, _n=len(shape): (0,) * _n))
            out_shapes.append(jax.ShapeDtypeStruct(shape, F32))
            okinds.append('acc')
        elif kind == 'colacc':
            _, r, wt, w, cb = ot
            out_specs.append(pl.BlockSpec((r, w), lambda j, i, _c=_cb(cb): (0, _c(j))))
            out_shapes.append(jax.ShapeDtypeStruct((r, wt), F32))
            okinds.append('colacc')
        else:
            raise ValueError(kind)
    n_in = len(in_specs)

    def body(*refs):
        j = pl.program_id(0)
        i = pl.program_id(1)
        in_refs, out_refs = refs[:n_in], refs[n_in:]
        items, p = [], 0
        for kd in kinds:
            if kd[0] == 'row':
                items.append(in_refs[p][...])
                p += 1
            elif kd[0] == 'halo':
                prev, cen, nxt = in_refs[p][...], in_refs[p + 1][...], in_refs[p + 2][...]
                prev = jnp.where(i == 0, jnp.zeros_like(prev), prev)
                nxt = jnp.where(i == nrow - 1, jnp.zeros_like(nxt), nxt)
                items.append(jnp.concatenate([prev, cen, nxt], axis=0))
                p += 3
            else:
                items.append(in_refs[p])
                p += 1
        for o_ref, kd in zip(out_refs, okinds):
            if kd == 'acc':
                @pl.when((i == 0) & (j == 0))
                def _(o_ref=o_ref):
                    o_ref[...] = jnp.zeros_like(o_ref)
            elif kd == 'colacc':
                @pl.when(i == 0)
                def _(o_ref=o_ref):
                    o_ref[...] = jnp.zeros_like(o_ref)
        fn(i, j, *items, *out_refs)

    res = pl.pallas_call(
        body, name=name, grid=(ncol, nrow), in_specs=in_specs, out_specs=out_specs, out_shape=out_shapes,
        compiler_params=_cparams(("arbitrary", "arbitrary")),
    )(*args)
    return res


def _sigmoid(x):
    return 1.0 / (1.0 + jnp.exp(-x))


def _ln_stats(x):
    mu = jnp.mean(x, axis=1, keepdims=True)
    xc = x - mu
    var = jnp.mean(xc * xc, axis=1, keepdims=True)
    r = lax.rsqrt(var + LN_EPS)
    return xc * r, r


def _ln_bwd(dy, xh, r, g):
    dxh = dy * g
    dx = r * (dxh - jnp.mean(dxh, axis=1, keepdims=True) - xh * jnp.mean(dxh * xh, axis=1, keepdims=True))
    return dx, jnp.sum(dy * xh, axis=0, keepdims=True), jnp.sum(dy, axis=0, keepdims=True)


def _gmean(v, gm):
    hi = v.astype(BF16)
    lo = (v - hi.astype(F32)).astype(BF16)
    return jnp.dot(hi, gm, preferred_element_type=F32) + jnp.dot(lo, gm, preferred_element_type=F32)


def _roll_l(x, sh):
    return pltpu.roll(x, sh % x.shape[1], 1)


def _rope(x, c, sa, sb, sh):
    return x * c + _roll_l(x, -sh) * sa + _roll_l(x, sh) * sb


def _rope_t(dy, c, sa, sb, sh):
    return dy * c + _roll_l(dy * sa, sh) + _roll_l(dy * sb, -sh)


def _roll_r(x, s):
    return pltpu.roll(x, (-s) % x.shape[0], 0)


_GELU_C = math.sqrt(2.0 / math.pi)


def _gelu(x):
    t = jnp.tanh(_GELU_C * (x + 0.044715 * x * x * x))
    return 0.5 * x * (1.0 + t), t


def _gelu_grad(x, t):
    return 0.5 * (1.0 + t) + 0.5 * x * (1.0 - t * t) * _GELU_C * (1.0 + 3.0 * 0.044715 * x * x)


def _dot_nt(a, b):
    return lax.dot_general(a, b, (((1,), (1,)), ((), ())), preferred_element_type=F32)


def _dot_tn(a, b):
    return lax.dot_general(a, b, (((0,), (0,)), ((), ())), preferred_element_type=F32)


def _rope_block(s, d):
    t = jnp.arange(s)
    row = (t // GRID_W).astype(F32)
    col = (t % GRID_W).astype(F32)
    half = d // 4
    inv = ROPE_THETA ** (-jnp.arange(half, dtype=F32) / half)
    z = jnp.zeros((s, half), F32)
    cs, sas, sbs = [], [], []
    for pos in (row, col):
        ang = pos[:, None] * inv[None, :]
        co, si = jnp.cos(ang), jnp.sin(ang)
        cs += [co, co]
        sas += [-si, z]
        sbs += [z, si]
    return tuple(jnp.concatenate(v, axis=1) for v in (cs, sas, sbs))


def _rope_tables(s):
    gqa = tuple(jnp.tile(a, (1, 4)) for a in _rope_block(s, HEAD_DIM))
    c32, sa32, sb32 = _rope_block(s, MLA_ROPE)

    def head128(c, fill):
        return jnp.concatenate([jnp.full((s, 64), fill, F32), c, jnp.zeros((s, 32), F32)], axis=1)

    mla = tuple(jnp.tile(a, (1, 4)) for a in (head128(c32, 1.0), head128(sa32, 0.0), head128(sb32, 0.0)))
    kr = (head128(c32, 0.0), head128(sa32, 0.0), head128(sb32, 0.0))
    return dict(gqa=gqa, mla=mla, kr=kr)


def _group_mean_matrix(w, g):
    idx = np.arange(w) // g
    return jnp.asarray((idx[:, None] == idx[None, :]).astype(np.float32) / g, dtype=BF16)


TM = 512


def ln_fwd(name, s, x, g, b, res=None, alpha=1.0):
    ins = [('row', x, D_MODEL, 0)] + ([('row', res, D_MODEL, 0)] if res is not None else []) + [('full', g), ('full', b)]

    def fn(i, j, *a):
        if res is not None:
            xv, rv, g_ref, b_ref, o_ref = a
            z = alpha * rv + xv
        else:
            xv, g_ref, b_ref, o_ref = a
            z = xv
        xh, _ = _ln_stats(z)
        o_ref[...] = xh * g_ref[...] + b_ref[...]

    return tiled_call(name, fn, s, TM, ins, [('row', D_MODEL, D_MODEL, 0, F32)])[0]


def ln_bwd(name, s, dy, x, g, res=None, alpha=1.0):
    ins = [('row', dy, D_MODEL, 0), ('row', x, D_MODEL, 0)]
    ins += ([('row', res, D_MODEL, 0)] if res is not None else []) + [('full', g)]

    def fn(i, j, *a):
        a = list(a)
        dyv = a.pop(0)
        xv = a.pop(0)
        z = alpha * a.pop(0) + xv if res is not None else xv
        g_ref, dz_ref, dg_ref, db_ref = a
        xh, r = _ln_stats(z)
        dz, dg, db = _ln_bwd(dyv, xh, r, g_ref[...])
        dz_ref[...] = dz
        dg_ref[...] += dg
        db_ref[...] += db

    return tiled_call(name, fn, s, TM, ins, [('row', D_MODEL, D_MODEL, 0, F32), ('acc', (1, D_MODEL)), ('acc', (1, D_MODEL))])


def loss_and_ln_bwd(name, s, x, res, g, b, tgt, alpha):
    ins = [('row', x, D_MODEL, 0), ('row', res, D_MODEL, 0), ('row', tgt, D_MODEL, 0), ('full', g), ('full', b)]

    def fn(i, j, xv, rv, tv, g_ref, b_ref, dz_ref, dg_ref, db_ref, loss_ref):
        xh, r = _ln_stats(alpha * rv + xv)
        y = xh * g_ref[...] + b_ref[...]
        e = y - tv
        dz, dg, db = _ln_bwd(e * (1.0 / D_MODEL), xh, r, g_ref[...])
        dz_ref[...] = dz
        dg_ref[...] += dg
        db_ref[...] += db
        loss_ref[...] += jnp.sum(jnp.sum(e * e, axis=0, keepdims=True), axis=1, keepdims=True) * (0.5 / D_MODEL)

    return tiled_call(name, fn, s, TM, ins, [('row', D_MODEL, D_MODEL, 0, F32), ('acc', (1, D_MODEL)), ('acc', (1, D_MODEL)),
                                             ('acc', (1, 128))])


HALO_A = 16


def _glu(a_in):
    return a_in[:, :GROUP_W] * _sigmoid(a_in[:, GROUP_W:])


def _row_windows(x, halo, tm):
    rolled = {0: x}

    def window(s):
        b = s % 8
        if b not in rolled:
            rolled[b] = _roll_r(x, b)
        off = halo + s - b
        return rolled[b][off:off + tm]

    return window


def a_fwd(name, s, proj, cw, cb, g, b):
    def fn(i, j, a_ext, cw_ref, cb_ref, g_ref, b_ref, y_ref, o_ref):
        a_at = _row_windows(_glu(a_ext), HALO_A, TM)
        acc = jnp.zeros((TM, GROUP_W), F32)
        for k in range(CONV_A_WIDTH):
            acc = acc + a_at(k - 15) * cw_ref[pl.ds(k, 1), :]
        y = acc + cb_ref[...]
        y_ref[...] = y
        xh, _ = _ln_stats(y)
        z = xh * g_ref[...] + b_ref[...]
        o_ref[...] = z * _sigmoid(z)

    ins = [('halo', proj, 2 * GROUP_W, 0, HALO_A), ('full', cw), ('full', cb), ('full', g), ('full', b)]
    return tiled_call(name, fn, s, TM, ins, [('row', GROUP_W, GROUP_W, 0, F32), ('row', GROUP_W, GROUP_W, 0, F32)])


def a_bwd1(name, s, y, dmix, g, b):
    def fn(i, j, yv, do, g_ref, b_ref, dy_ref, dg_ref, db_ref, dcb_ref):
        xh, r = _ln_stats(yv)
        z = xh * g_ref[...] + b_ref[...]
        sg = _sigmoid(z)
        dz = do * sg * (1.0 + z * (1.0 - sg))
        dy, dg, db = _ln_bwd(dz, xh, r, g_ref[...])
        dy_ref[...] = dy
        dg_ref[...] += dg
        db_ref[...] += db
        dcb_ref[...] += jnp.sum(dy, axis=0, keepdims=True)

    ins = [('row', y, GROUP_W, 0), ('row', dmix, GROUP_W, 0), ('full', g), ('full', b)]
    return tiled_call(name, fn, s, TM, ins, [('row', GROUP_W, GROUP_W, 0, F32), ('acc', (1, GROUP_W)), ('acc', (1, GROUP_W)),
                                             ('acc', (1, GROUP_W))])


def a_bwd2(name, s, proj, dy, cw):
    def fn(i, j, a_ext, dy_ext, cw_ref, da_ref, dcw_ref):
        a1, a2 = a_ext[:, :GROUP_W], a_ext[:, GROUP_W:]
        sg = _sigmoid(a2)
        a = a1 * sg
        dyc = dy_ext[HALO_A:HALO_A + TM]
        dy_at, a_at = _row_windows(dy_ext, HALO_A, TM), _row_windows(a, HALO_A, TM)
        da = jnp.zeros((TM, GROUP_W), F32)
        for k in range(CONV_A_WIDTH):
            da = da + dy_at(15 - k) * cw_ref[pl.ds(k, 1), :]
            dcw_ref[pl.ds(k, 1), :] += jnp.sum(dyc * a_at(k - 15), axis=0, keepdims=True)
        a1c, sgc = a1[HALO_A:HALO_A + TM], sg[HALO_A:HALO_A + TM]
        da_ref[...] = jnp.concatenate([da * sgc, da * a1c * sgc * (1.0 - sgc)], axis=1).astype(BF16)

    ins = [('halo', proj, 2 * GROUP_W, 0, HALO_A), ('halo', dy, GROUP_W, 0, HALO_A), ('full', cw)]
    return tiled_call(name, fn, s, TM, ins, [('row', 2 * GROUP_W, 2 * GROUP_W, 0, BF16), ('acc', (32, GROUP_W))])


def b_prep_fwd(name, s, proj, gq, gk, tabs, gm):
    c, sa, sb = tabs

    def fn(i, j, qkv, cv, sav, sbv, gq_ref, gk_ref, gm_ref, q_ref, k_ref, v_ref):
        q, k, v = qkv[:, :256], qkv[:, 256:384], qkv[:, 384:]
        gmv = gm_ref[...]
        qn = q * lax.rsqrt(_gmean(q * q, gmv) + RMS_EPS) * gq_ref[...]
        kn = k * lax.rsqrt(_gmean(k * k, gmv[:128, :128]) + RMS_EPS) * gk_ref[...]
        qr = _rope(qn, cv, sav, sbv, 16)
        kr = _rope(kn, cv[:, :128], sav[:, :128], sbv[:, :128], 16)
        for h in range(4):
            q_ref[h] = (qr[:, h * 64:(h + 1) * 64] * QSCALE_GQA).astype(BF16)
        for h in range(2):
            k_ref[h] = kr[:, h * 64:(h + 1) * 64].astype(BF16)
            v_ref[h] = v[:, h * 64:(h + 1) * 64].astype(BF16)

    ins = [('row', proj, 512, 1), ('row', c, 256, 0), ('row', sa, 256, 0), ('row', sb, 256, 0), ('full', gq), ('full', gk),
           ('full', gm)]
    return tiled_call(name, fn, s, TM, ins, [('hm', 4, 64, BF16), ('hm', 2, 64, BF16), ('hm', 2, 64, BF16)])


def b_prep_bwd(name, s, proj, gq, gk, tabs, gm, dq_hm, dk_hm, dv_hm):
    c, sa, sb = tabs

    def fn(i, j, qkv, cv, sav, sbv, gq_ref, gk_ref, gm_ref, dq_ref, dk_ref, dv_ref, dp_ref, dgq_ref, dgk_ref):
        q, k = qkv[:, :256], qkv[:, 256:384]
        gmv = gm_ref[...]
        dqr = jnp.concatenate([dq_ref[h] for h in range(4)], axis=1) * QSCALE_GQA
        dkr = jnp.concatenate([dk_ref[h] for h in range(2)], axis=1)
        dv = jnp.concatenate([dv_ref[h] for h in range(2)], axis=1)
        outs = []
        for x, dxr, g_ref, gmx, w, dg_ref in ((q, dqr, gq_ref, gmv, 256, dgq_ref), (k, dkr, gk_ref, gmv[:128, :128], 128, dgk_ref)):
            dn = _rope_t(dxr, cv[:, :w], sav[:, :w], sbv[:, :w], 16)
            r = lax.rsqrt(_gmean(x * x, gmx) + RMS_EPS)
            gv = g_ref[...]
            dx = gv * r * dn - x * (r * r * r) * _gmean(x * gv * dn, gmx)
            dgt = jnp.sum(x * r * dn, axis=0, keepdims=True)
            dg = dgt[:, 0:64]
            for h in range(1, w // 64):
                dg = dg + dgt[:, h * 64:(h + 1) * 64]
            dg_ref[...] += dg
            outs.append(dx)
        dp_ref[...] = jnp.concatenate(outs + [dv], axis=1).astype(BF16)

    ins = [('row', proj, 512, 1), ('row', c, 256, 0), ('row', sa, 256, 0), ('row', sb, 256, 0), ('full', gq), ('full', gk),
           ('full', gm), ('hm', dq_hm), ('hm', dk_hm), ('hm', dv_hm)]
    return tiled_call(name, fn, s, TM, ins, [('row', 512, 512, 0, BF16), ('acc', (1, 64)), ('acc', (1, 64))])


TQ = 512
TQ_FWD = 256


LOG2E = 1.4426950408889634
QSCALE_GQA = HEAD_DIM ** -0.5 * LOG2E
QSCALE_MLA = (MLA_NOPE + MLA_ROPE) ** -0.5 * LOG2E


def attn_fwd(name, s, q_hm, k_hm, v_hm, comm=None):
    dk = q_hm.shape[2]
    nkv = k_hm.shape[0]
    tq = TQ_FWD if s % TQ_FWD == 0 else TQ

    def body(q_ref, k_ref, v_ref, o_ref, l_ref):
        outs = []
        for g in range(4):
            kv = g * nkv // 4
            sc = _dot_nt(q_ref[g], k_ref[kv])
            m = jnp.max(sc, axis=1, keepdims=True)
            p = jnp.exp2(sc - m)
            l = jnp.sum(p, axis=1, keepdims=True)
            o = jnp.dot(p.astype(BF16), v_ref[kv], preferred_element_type=F32)
            outs.append(o / l)
            l_ref[g] = m + jnp.log2(l)
        o_ref[...] = jnp.concatenate(outs, axis=1)

    res, cres = carried_call(
        name, body, (s // tq,),
        [pl.BlockSpec((4, tq, dk), lambda i: (0, i, 0)),
         pl.BlockSpec((nkv, s, dk), lambda i: (0, 0, 0)),
         pl.BlockSpec((nkv, s, 64), lambda i: (0, 0, 0))],
        [pl.BlockSpec((tq, 256), lambda i: (i, 0)), pl.BlockSpec((4, tq, 1), lambda i: (0, i, 0))],
        [jax.ShapeDtypeStruct((s, 256), F32), jax.ShapeDtypeStruct((4, s, 1), F32)], [],
        ("arbitrary",), [q_hm, k_hm, v_hm], comm)
    return res if comm is None else (res, cres)


def attn_bwd(name, s, q_hm, k_hm, v_hm, o, lse, dmixcat, dcol, comm=None):
    dk = q_hm.shape[2]
    nkv = k_hm.shape[0]
    shared = nkv == 2
    kvi = (lambda p, g, i: (p, 0, 0)) if shared else (lambda p, g, i: (2 * p + g, 0, 0))

    def body(q_ref, k_ref, v_ref, o_ref, l_ref, do_ref, dq_ref, dk_ref, dv_ref):
        g = pl.program_id(1)
        i = pl.program_id(2)
        first = (i == 0) & (g == 0) if shared else (i == 0)

        @pl.when(first)
        def _():
            dk_ref[...] = jnp.zeros_like(dk_ref)
            dv_ref[...] = jnp.zeros_like(dv_ref)

        dob, ob = do_ref[...], o_ref[...]
        do = jnp.where(g == 0, dob[:, :64], dob[:, 64:])
        ov = jnp.where(g == 0, ob[:, :64], ob[:, 64:])
        q, k, v = q_ref[0], k_ref[0], v_ref[0]
        p = jnp.exp2(_dot_nt(q, k) - l_ref[0])
        dp = _dot_nt(do.astype(BF16), v)
        delta = jnp.sum(do * ov, axis=1, keepdims=True)
        ds = (p * (dp - delta) * (1.0 / LOG2E)).astype(BF16)
        dq_ref[0] = jnp.dot(ds, k, preferred_element_type=F32)
        dk_ref[0] += _dot_tn(ds, q)
        dv_ref[0] += _dot_tn(p.astype(BF16), do.astype(BF16))

    res, cres = carried_call(
        name, body, (2, 2, s // TQ),
        [pl.BlockSpec((1, TQ, dk), lambda p, g, i: (2 * p + g, i, 0)),
         pl.BlockSpec((1, s, dk), kvi), pl.BlockSpec((1, s, 64), kvi),
         pl.BlockSpec((TQ, 128), lambda p, g, i: (i, p)),
         pl.BlockSpec((1, TQ, 1), lambda p, g, i: (2 * p + g, i, 0)),
         pl.BlockSpec((TQ, 128), lambda p, g, i: (i, dcol + p))],
        [pl.BlockSpec((1, TQ, dk), lambda p, g, i: (2 * p + g, i, 0)),
         pl.BlockSpec((1, s, dk), kvi), pl.BlockSpec((1, s, 64), kvi)],
        [jax.ShapeDtypeStruct((4, s, dk), F32), jax.ShapeDtypeStruct((nkv, s, dk), F32), jax.ShapeDtypeStruct((nkv, s, 64), F32)],
        [], ("arbitrary", "arbitrary", "arbitrary"), [q_hm, k_hm, v_hm, o, lse, dmixcat], comm)
    return res if comm is None else (res, cres)


def _lane_group_masks(w, g):
    lane = lax.broadcasted_iota(jnp.int32, (1, w), 1)
    return [((lane >= k * g) & (lane < (k + 1) * g)).astype(F32) for k in range(w // g)]


def _sgu_gate(svn, w_ref, bias):
    masks = _lane_group_masks(GROUP_W, 64)
    outs = []
    for n in range(TM // CHUNK):
        x = svn[n * CHUNK:(n + 1) * CHUNK]
        acc = bias
        for g in range(4):
            acc = acc + jnp.dot(w_ref[g].astype(BF16), (x * masks[g]).astype(BF16), preferred_element_type=F32)
        outs.append(acc)
    return jnp.concatenate(outs, axis=0)


def c_fwd(name, s, proj, g, b, w, bias):
    def fn(i, j, cin, g_ref, b_ref, w_ref, bias_ref, o_ref):
        c, _ = _gelu(cin)
        xh, _ = _ln_stats(c[:, GROUP_W:])
        svn = xh * g_ref[...] + b_ref[...]
        o_ref[...] = c[:, :GROUP_W] * _sgu_gate(svn, w_ref, bias_ref[...])

    ins = [('row', proj, 512, 2), ('full', g), ('full', b), ('full', w), ('full', bias)]
    return tiled_call(name, fn, s, TM, ins, [('row', GROUP_W, GROUP_W, 0, F32)])[0]


def c_bwd(name, s, proj, g, b, w, wt, bias, dmixcat):
    def fn(i, j, cin, do, g_ref, b_ref, w_ref, wt_ref, bias_ref, dc_ref, dg_ref, db_ref, dw_ref, dbias_ref):
        c, t = _gelu(cin)
        u = c[:, :GROUP_W]
        xh, r = _ln_stats(c[:, GROUP_W:])
        svn = xh * g_ref[...] + b_ref[...]
        gate = _sgu_gate(svn, w_ref, bias_ref[...])
        du = do * gate
        dgate = do * u
        masks = _lane_group_masks(GROUP_W, 64)
        col = lax.broadcasted_iota(jnp.int32, (CHUNK, 128), 1)
        dsvn, dbias = [], jnp.zeros((CHUNK, 128), F32)
        for n in range(TM // CHUNK):
            dgc = dgate[n * CHUNK:(n + 1) * CHUNK]
            x = svn[n * CHUNK:(n + 1) * CHUNK]
            acc = jnp.zeros((CHUNK, GROUP_W), F32)
            for gi in range(4):
                dgm = (dgc * masks[gi]).astype(BF16)
                acc = acc + jnp.dot(wt_ref[gi].astype(BF16), dgm, preferred_element_type=F32)
                dw_ref[gi] += _dot_nt(dgm, (x * masks[gi]).astype(BF16))
                dbias = dbias + jnp.where(col == gi, jnp.sum(dgc * masks[gi], axis=1, keepdims=True), 0.0)
            dsvn.append(acc)
        dbias_ref[...] += dbias
        dsv, dg, db = _ln_bwd(jnp.concatenate(dsvn, axis=0), xh, r, g_ref[...])
        dg_ref[...] += dg
        db_ref[...] += db
        dc_ref[...] = (jnp.concatenate([du, dsv], axis=1) * _gelu_grad(cin, t)).astype(BF16)

    ins = [('row', proj, 512, 2), ('row', dmixcat, GROUP_W, 2), ('full', g), ('full', b), ('full', w), ('full', wt), ('full', bias)]
    return tiled_call(name, fn, s, TM, ins, [('row', 512, 512, 0, BF16), ('acc', (1, GROUP_W)), ('acc', (1, GROUP_W)),
                                             ('acc', (4, CHUNK, CHUNK)), ('acc', (CHUNK, 128))])


def _d_common(dpart, gq_ref, gkv_ref):
    cq, ckv, kr = dpart[:, :256], dpart[:, 256:384], dpart[:, 384:]
    rq = lax.rsqrt(jnp.sum(cq * cq, axis=1, keepdims=True) * (1.0 / MLA_Q_LORA) + RMS_EPS)
    rkv = lax.rsqrt(jnp.mean(ckv * ckv, axis=1, keepdims=True) + RMS_EPS)
    return cq, ckv, kr, rq, rkv, cq * rq * gq_ref[...], ckv * rkv * gkv_ref[...]


def d_prep_fwd(name, s, proj, gq, gkv, wuq, wukv, tm_, tkr):
    cm, sam, sbm = tm_
    ck, sak, sbk = tkr

    def fn(i, j, dpart, cmv, samv, sbmv, ckv_, sakv, sbkv, gq_ref, gkv_ref, wuq_ref, wukv_ref, q_ref, k_ref, v_ref):
        cq, ckv, kr, rq, rkv, cqn, ckvn = _d_common(dpart, gq_ref, gkv_ref)
        qd = jnp.dot(cqn.astype(BF16), wuq_ref[...], preferred_element_type=F32)
        qf = _rope(qd, cmv, samv, sbmv, 8)
        kvd = jnp.dot(ckvn.astype(BF16), wukv_ref[...], preferred_element_type=F32)
        krr = _rope(kr, ckv_, sakv, sbkv, 8)
        for h in range(4):
            q_ref[h] = (qf[:, h * 128:(h + 1) * 128] * QSCALE_MLA).astype(BF16)
            k_ref[h] = (kvd[:, h * 128:(h + 1) * 128] + krr).astype(BF16)
            v_ref[h] = kvd[:, 512 + h * 64:512 + (h + 1) * 64].astype(BF16)

    ins = [('row', proj, 512, 3), ('row', cm, 512, 0), ('row', sam, 512, 0), ('row', sbm, 512, 0), ('row', ck, 128, 0),
           ('row', sak, 128, 0), ('row', sbk, 128, 0), ('full', gq), ('full', gkv), ('full', wuq), ('full', wukv)]
    return tiled_call(name, fn, s, TM, ins, [('hm', 4, 128, BF16), ('hm', 4, 128, BF16), ('hm', 4, 64, BF16)])


def d_prep_bwd(name, s, proj, gq, gkv, wuq, wukv, tm_, tkr, dq_hm, dk_hm, dv_hm):
    cm, sam, sbm = tm_
    ck, sak, sbk = tkr

    def fn(i, j, dpart, cmv, samv, sbmv, ckv_, sakv, sbkv, gq_ref, gkv_ref, wuq_ref, wukv_ref, dq_ref, dk_ref, dv_ref,
           dd_ref, dgq_ref, dgkv_ref, dwuq_ref, dwukv_ref):
        cq, ckv, kr, rq, rkv, cqn, ckvn = _d_common(dpart, gq_ref, gkv_ref)
        dqf = jnp.concatenate([dq_ref[h] for h in range(4)], axis=1) * QSCALE_MLA
        dkf = [dk_ref[h] for h in range(4)]
        dqd = _rope_t(dqf, cmv, samv, sbmv, 8).astype(BF16)
        dkvd = jnp.concatenate(dkf + [dv_ref[h] for h in range(4)], axis=1).astype(BF16)
        dkr = _rope_t(dkf[0] + dkf[1] + dkf[2] + dkf[3], ckv_, sakv, sbkv, 8)
        dcqn = _dot_nt(dqd, wuq_ref[...])
        dckvn = _dot_nt(dkvd, wukv_ref[...])
        dwuq_ref[...] += _dot_tn(cqn.astype(BF16), dqd)
        dwukv_ref[...] += _dot_tn(ckvn.astype(BF16), dkvd)
        gqv, gkvv = gq_ref[...], gkv_ref[...]
        dcq = gqv * rq * dcqn - cq * (rq * rq * rq) * (jnp.sum(cq * gqv * dcqn, axis=1, keepdims=True) * (1.0 / MLA_Q_LORA))
        dckv = gkvv * rkv * dckvn - ckv * (rkv * rkv * rkv) * jnp.mean(ckv * gkvv * dckvn, axis=1, keepdims=True)
        dgq_ref[...] += jnp.sum(cq * rq * dcqn, axis=0, keepdims=True)
        dgkv_ref[...] += jnp.sum(ckv * rkv * dckvn, axis=0, keepdims=True)
        dd_ref[...] = jnp.concatenate([dcq, dckv, dkr], axis=1).astype(BF16)

    ins = [('row', proj, 512, 3), ('row', cm, 512, 0), ('row', sam, 512, 0), ('row', sbm, 512, 0), ('row', ck, 128, 0),
           ('row', sak, 128, 0), ('row', sbk, 128, 0), ('full', gq), ('full', gkv), ('full', wuq), ('full', wukv),
           ('hm', dq_hm), ('hm', dk_hm), ('hm', dv_hm)]
    return tiled_call(name, fn, s, TM, ins, [('row', 512, 512, 0, BF16), ('acc', (1, 256)), ('acc', (1, 128)),
                                             ('acc', (256, 512)), ('acc', (128, 768))])


HALO_F = 8
TN_F = 1408
TM_F = 512
TM_FB, TN_FB = 512, 256


def _conv3(ext, w_ref):
    return _roll_r(ext, -1) * w_ref[pl.ds(0, 1), :] + ext * w_ref[pl.ds(1, 1), :] + _roll_r(ext, 1) * w_ref[pl.ds(2, 1), :]


def f_fwd(name, s, upraw, cw, cb):
    ncj = D_FF // TN_F

    def fn(i, j, xa, xg, wa_ref, wg_ref, ba_ref, bg_ref, o_ref):
        ua = (_conv3(xa, wa_ref) + ba_ref[...])[HALO_F:HALO_F + TM_F]
        ug = (_conv3(xg, wg_ref) + bg_ref[...])[HALO_F:HALO_F + TM_F]
        o_ref[...] = (ua * _sigmoid(ua) * ug).astype(BF16)

    ins = [('halo', upraw, TN_F, lambda j: j, HALO_F), ('halo', upraw, TN_F, lambda j: j + ncj, HALO_F),
           ('col', cw, TN_F, lambda j: j), ('col', cw, TN_F, lambda j: j + ncj),
           ('col', cb, TN_F, lambda j: j), ('col', cb, TN_F, lambda j: j + ncj)]
    return tiled_call(name, fn, s, TM_F, ins, [('row', D_FF, TN_F, lambda j: j, BF16)], ncol=ncj)[0]


def f_bwd(name, s, upraw, dact, cw, cb):
    ncj = D_FF // TN_FB

    def fn(i, j, xa, xg, da, wa_ref, wg_ref, ba_ref, bg_ref, dxa_ref, dxg_ref, dwa_ref, dwg_ref, dba_ref, dbg_ref):
        taps_a = (_roll_r(xa, -1), xa, _roll_r(xa, 1))
        taps_g = (_roll_r(xg, -1), xg, _roll_r(xg, 1))
        ua = sum(t * wa_ref[pl.ds(k, 1), :] for k, t in enumerate(taps_a)) + ba_ref[...]
        ug = sum(t * wg_ref[pl.ds(k, 1), :] for k, t in enumerate(taps_g)) + bg_ref[...]
        sg = _sigmoid(ua)
        dug = da * ua * sg
        dua = da * ug * sg * (1.0 + ua * (1.0 - sg))
        cen = slice(HALO_F, HALO_F + TM_FB)
        for du, taps, w_ref, dx_ref, dw_ref, db_ref in ((dua, taps_a, wa_ref, dxa_ref, dwa_ref, dba_ref),
                                                         (dug, taps_g, wg_ref, dxg_ref, dwg_ref, dbg_ref)):
            dx = _roll_r(du, 1) * w_ref[pl.ds(0, 1), :] + du * w_ref[pl.ds(1, 1), :] + _roll_r(du, -1) * w_ref[pl.ds(2, 1), :]
            dx_ref[...] = dx[cen].astype(BF16)
            duc = du[cen]
            for k in range(3):
                dw_ref[pl.ds(k, 1), :] += jnp.sum(duc * taps[k][cen], axis=0, keepdims=True)
            db_ref[...] += jnp.sum(duc, axis=0, keepdims=True)

    tn = TN_FB
    ins = [('halo', upraw, tn, lambda j: j, HALO_F), ('halo', upraw, tn, lambda j: j + ncj, HALO_F),
           ('halo', dact, tn, lambda j: j, HALO_F),
           ('col', cw, tn, lambda j: j), ('col', cw, tn, lambda j: j + ncj),
           ('col', cb, tn, lambda j: j), ('col', cb, tn, lambda j: j + ncj)]
    outs = [('row', D_FF, tn, lambda j: j, BF16), ('row', D_FF, tn, lambda j: j, BF16),
            ('colacc', 8, D_FF, tn, lambda j: j), ('colacc', 8, D_FF, tn, lambda j: j),
            ('colacc', 1, D_FF, tn, lambda j: j), ('colacc', 1, D_FF, tn, lambda j: j)]
    return tiled_call(name, fn, s, TM_FB, ins, outs, ncol=ncj)


def _pad_w_in(w):
    z = lambda n: jnp.zeros((w.shape[0], n), w.dtype)
    return jnp.concatenate([w[:, :1728], z(64), w[:, 1728:1856], z(64), w[:, 1856:1888], z(32)], axis=1)


def _unpad_w_in(g):
    return jnp.concatenate([g[:, :1728], g[:, 1792:1920], g[:, 1984:2016]], axis=1)


def _pad_uq(w):
    w = w.reshape(MLA_Q_LORA, 4, 96)
    w = jnp.concatenate([w, jnp.zeros((MLA_Q_LORA, 4, 32), w.dtype)], axis=2).reshape(MLA_Q_LORA, 512)
    return jnp.concatenate([w, jnp.zeros((64, 512), w.dtype)], axis=0)


def _unpad_uq(g):
    return g[:MLA_Q_LORA].reshape(MLA_Q_LORA, 4, 128)[:, :, :96].reshape(MLA_Q_LORA, 384)


def _pad_ukv(w):
    w = w.reshape(MLA_KV_LORA, 4, 128)
    kn = jnp.concatenate([w[:, :, :64], jnp.zeros((MLA_KV_LORA, 4, 64), w.dtype)], axis=2).reshape(MLA_KV_LORA, 512)
    return jnp.concatenate([kn, w[:, :, 64:].reshape(MLA_KV_LORA, 256)], axis=1)


def _unpad_ukv(g):
    kn = g[:, :512].reshape(MLA_KV_LORA, 4, 128)[:, :, :64]
    v = g[:, 512:].reshape(MLA_KV_LORA, 4, 64)
    return jnp.concatenate([kn, v], axis=2).reshape(MLA_KV_LORA, 512)


def _row(v, pad_to=None):
    v = v.reshape(1, -1)
    if pad_to is not None and v.shape[1] < pad_to:
        v = jnp.concatenate([v, jnp.zeros((1, pad_to - v.shape[1]), v.dtype)], axis=1)
    return v


LATE_NAMES = ['w_out', 'ffn_w_up', 'ffn_w_down']


def _assemble(n, part, g, me):
    return jnp.concatenate([jnp.where(me == q, part, g[q]) for q in range(N_CHIPS)], axis=SHARDED_AXIS[n])


def local_step(x, tgt, W, dist=None):
    s = x.shape[0]
    tabs = _rope_tables(s)
    gm = _group_mean_matrix(256, 64)
    alpha = DEEPNORM_ALPHA
    acts = []
    late = {(n, l): W[n][l] for n in LATE_NAMES for l in range(DEPTH)} if dist is None else {}
    h = ln_fwd("ln_in", s, x, _row(W['ln_in_g']), _row(W['ln_in_b']))
    for l in range(DEPTH):
        A = dict(h=h)
        A['wpad'] = _pad_w_in(W['w_in'][l]).astype(BF16)
        A['wuq'] = _pad_uq(W['mla_w_uq'][l]).astype(BF16)
        A['wukv'] = _pad_ukv(W['mla_w_ukv'][l]).astype(BF16)
        A['cw'] = jnp.concatenate([W['conv_a_w'][l], jnp.zeros((1, GROUP_W), F32)], axis=0)
        A['fcw'] = jnp.concatenate([W['ffn_conv_w'][l], jnp.zeros((5, 2 * D_FF), F32)], axis=0)
        A['gq'] = jnp.tile(_row(W['qk_norm_q'][l]), (1, 4))
        A['gk'] = jnp.tile(_row(W['qk_norm_k'][l]), (1, 2))
        A['sgu_bias'] = jnp.repeat(W['sgu_b'][l].T, 64, axis=1)
        A['sgu_wt'] = jnp.swapaxes(W['sgu_w'][l], 1, 2)
        proj = mm(f"proj{l}", h, A['wpad'])
        A['proj'] = proj
        A['y_a'], o_a = a_fwd(f"a_fwd{l}", s, proj, A['cw'], _row(W['conv_a_b'][l]), _row(W['ln_a_g'][l]), _row(W['ln_a_b'][l]))
        A['bq'], A['bk'], A['bv'] = b_prep_fwd(f"b_prep{l}", s, proj, A['gq'], A['gk'], tabs['gqa'], gm)
        if dist is not None and l == 0:
            lp = dist['late']
            parts = [lp['w_out'], lp['ffn_w_up'][0:1], lp['ffn_w_down'][0:1]]
            nxt_names, nxt = dist['next']
            nxt_parts = [a[1:2] for a in nxt]
            wanted = [lp['w_out'], (lp['ffn_w_up'], 0), (lp['ffn_w_down'], 0)] + [(a, 1) for a in nxt]
            (A['o_b'], A['lse_b']), got = attn_fwd(f"b_attn{l}", s, A['bq'], A['bk'], A['bv'], comm=gather_over_chips(wanted))
            full = [_assemble(n, p, g, dist['me']) for n, p, g in zip(LATE_NAMES, parts, got)]
            late[('w_out', 0)], late[('w_out', 1)] = full[0][0], full[0][1]
            late[('ffn_w_up', 0)], late[('ffn_w_down', 0)] = full[1][0], full[2][0]
            for n, p, g in zip(nxt_names, nxt_parts, got[len(parts):]):
                W[n][1] = _assemble(n, p, g, dist['me'])[0]
        else:
            A['o_b'], A['lse_b'] = attn_fwd(f"b_attn{l}", s, A['bq'], A['bk'], A['bv'])
        o_c = c_fwd(f"c_fwd{l}", s, proj, _row(W['sgu_ln_g'][l]), _row(W['sgu_ln_b'][l]), W['sgu_w'][l], A['sgu_bias'])
        A['dq'], A['dk'], A['dv'] = d_prep_fwd(f"d_prep{l}", s, proj, _row(W['mla_q_norm'][l], 256), _row(W['mla_kv_norm'][l]),
                                               A['wuq'], A['wukv'], tabs['mla'], tabs['kr'])
        if dist is not None and l == 0:
            parts = [lp['ffn_w_up'][1:2], lp['ffn_w_down'][1:2]]
            (A['o_d'], A['lse_d']), got = attn_fwd(f"d_attn{l}", s, A['dq'], A['dk'], A['dv'],
                                                   comm=gather_over_chips([(lp['ffn_w_up'], 1), (lp['ffn_w_down'], 1)]))
            late[('ffn_w_up', 1)] = _assemble('ffn_w_up', parts[0], got[0], dist['me'])[0]
            late[('ffn_w_down', 1)] = _assemble('ffn_w_down', parts[1], got[1], dist['me'])[0]
        else:
            A['o_d'], A['lse_d'] = attn_fwd(f"d_attn{l}", s, A['dq'], A['dk'], A['dv'])
        A['wout'] = late[('w_out', l)].astype(BF16)
        A['wup'] = late[('ffn_w_up', l)].astype(BF16)
        A['wdown'] = late[('ffn_w_down', l)].astype(BF16)
        A['mixcat'] = jnp.concatenate([o_a, A['o_b'], o_c, A['o_d']], axis=1).astype(BF16)
        A['mix'] = mm(f"out_proj{l}", A['mixcat'], A['wout'])
        h1 = ln_fwd(f"ln_mix{l}", s, A['mix'], _row(W['ln_mix_g'][l]), _row(W['ln_mix_b'][l]), res=h, alpha=alpha)
        A['h1'] = h1
        A['upraw'] = mm(f"ffn_up{l}", h1, A['wup'])
        A['act'] = f_fwd(f"f_fwd{l}", s, A['upraw'], A['fcw'], _row(W['ffn_conv_b'][l]))
        A['f'] = mm(f"ffn_down{l}", A['act'], A['wdown'])
        if l < DEPTH - 1:
            h = ln_fwd(f"ln_ffn{l}", s, A['f'], _row(W['ln_ffn_g'][l]), _row(W['ln_ffn_b'][l]), res=h1, alpha=alpha)
        acts.append(A)

    per_layer = {n: [None] * DEPTH for n in WEIGHT_NAMES if n not in ('ln_in_g', 'ln_in_b')}
    dh = None
    loss_row = None
    overlap = dist is not None and DEPTH == 2
    red = {}
    grp1 = [(n, 1) for n in BIG_NAMES]
    grp2 = [('ffn_w_up', 0), ('ffn_w_down', 0)]

    def carried(fn, name, *a, comm=None, **kw):
        if comm is None:
            return fn(name, *a, **kw), None
        return fn(name, *a, comm=comm, **kw)

    def pair_sums(tag, xs, r1):
        return [pair_sum(f"pair_sum_{tag}{t}", xv, rv, dist['idx'], BF16) for t, (xv, rv) in enumerate(zip(xs, r1))]

    def chip_sums(tag, cs, ps):
        return [chip_sum(f"chip_sum_{tag}{t}", [cv], [pv], dist['idx']) for t, (cv, pv) in enumerate(zip(cs, ps))]

    for l in reversed(range(DEPTH)):
        A = acts[l]
        carry = overlap and l == 0
        g_ffn, b_ffn = _row(W['ln_ffn_g'][l]), _row(W['ln_ffn_b'][l])
        if l == DEPTH - 1:
            dz, dg, db, loss_row = loss_and_ln_bwd("loss", s, A['f'], A['h1'], g_ffn, b_ffn, tgt, alpha)
        else:
            dz, dg, db = ln_bwd(f"ln_ffn_bwd{l}", s, dh, A['f'], g_ffn, res=A['h1'], alpha=alpha)
        per_layer['ln_ffn_g'][l], per_layer['ln_ffn_b'][l] = dg[0], db[0]
        xs1 = [per_layer[n][k] for n, k in grp1] if carry else None
        dwd, r1 = carried(mm, f"dw_down{l}", A['act'], dz, ta=True, comm=pair_exchange(xs1) if carry else None)
        per_layer['ffn_w_down'][l] = dwd.reshape(N_CHIPS, D_FF // N_CHIPS, D_MODEL)
        ps1 = pair_sums("g1_", xs1, r1) if carry else None
        dact = mm(f"d_act{l}", dz, A['wdown'], tb=True)
        dxa, dxg, dwa, dwg, dba, dbg = f_bwd(f"f_bwd{l}", s, A['upraw'], dact, A['fcw'], _row(W['ffn_conv_b'][l]))
        per_layer['ffn_conv_w'][l] = jnp.concatenate([dwa[:3], dwg[:3]], axis=1)
        per_layer['ffn_conv_b'][l] = jnp.concatenate([dba[0], dbg[0]], axis=0)
        dup = jnp.concatenate([dxa, dxg], axis=1)
        per_layer['ffn_w_up'][l] = mm(f"dw_up{l}", A['h1'], dup, ta=True, chip_major=True)
        xs2 = [per_layer[n][k] for n, k in grp2] if carry else None
        dh1_mm, r2 = carried(mm, f"d_h1{l}", dup, A['wup'], tb=True, comm=pair_exchange(xs2) if carry else None)
        ps2 = pair_sums("g2_", xs2, r2) if carry else None
        dz1, dg, db = _ln_mix_bwd(l, s, dz, dh1_mm, A, W, alpha)
        per_layer['ln_mix_g'][l], per_layer['ln_mix_b'][l] = dg[0], db[0]
        per_layer['w_out'][l] = mm(f"dw_out{l}", A['mixcat'], dz1, ta=True).reshape(N_CHIPS, D_MODEL // N_CHIPS, D_MODEL)
        dmixcat = mm(f"d_mixcat{l}", dz1, A['wout'], tb=True)
        dy_a, dg, db, dcb = a_bwd1(f"a_bwd1{l}", s, A['y_a'], dmixcat, _row(W['ln_a_g'][l]), _row(W['ln_a_b'][l]))
        per_layer['ln_a_g'][l], per_layer['ln_a_b'][l], per_layer['conv_a_b'][l] = dg[0], db[0], dcb[0]
        dp_a, dcw = a_bwd2(f"a_bwd2{l}", s, A['proj'], dy_a, A['cw'])
        per_layer['conv_a_w'][l] = dcw[:CONV_A_WIDTH]
        (dq, dk, dv), cs1 = carried(attn_bwd, f"b_attn_bwd{l}", s, A['bq'], A['bk'], A['bv'], A['o_b'], A['lse_b'], dmixcat, 2,
                                    comm=chip_exchange(ps1) if carry else None)
        halves1 = chip_sums("g1_", cs1, ps1) if carry else None
        dp_b, dgq, dgk = b_prep_bwd(f"b_prep_bwd{l}", s, A['proj'], A['gq'], A['gk'], tabs['gqa'], gm, dq, dk, dv)
        per_layer['qk_norm_q'][l], per_layer['qk_norm_k'][l] = dgq[0], dgk[0]
        dp_c, dg, db, dsw, dsb = c_bwd(f"c_bwd{l}", s, A['proj'], _row(W['sgu_ln_g'][l]), _row(W['sgu_ln_b'][l]), W['sgu_w'][l],
                                       A['sgu_wt'], A['sgu_bias'], dmixcat)
        per_layer['sgu_ln_g'][l], per_layer['sgu_ln_b'][l] = dg[0], db[0]
        per_layer['sgu_w'][l], per_layer['sgu_b'][l] = dsw, dsb[:, :4].T
        comms = [half_exchange(halves1), chip_exchange(ps2)] if carry else []
        (dq, dk, dv), cres = carried(attn_bwd, f"d_attn_bwd{l}", s, A['dq'], A['dk'], A['dv'], A['o_d'], A['lse_d'], dmixcat, 6,
                                     comm=join_comms(comms))
        if carry:
            done1, cs2 = split_comm_results(comms, cres)
            red.update(zip(grp1, done1))
            halves2 = chip_sums("g2_", cs2, ps2)
        dp_d, dgq, dgkv, dwuq, dwukv = d_prep_bwd(f"d_prep_bwd{l}", s, A['proj'], _row(W['mla_q_norm'][l], 256),
                                                  _row(W['mla_kv_norm'][l]), A['wuq'], A['wukv'], tabs['mla'], tabs['kr'],
                                                  dq, dk, dv)
        per_layer['mla_q_norm'][l], per_layer['mla_kv_norm'][l] = dgq[0, :MLA_Q_LORA], dgkv[0]
        per_layer['mla_w_uq'][l], per_layer['mla_w_ukv'][l] = _unpad_uq(dwuq), _unpad_ukv(dwukv)
        dproj = jnp.concatenate([dp_a, dp_b, dp_c, dp_d], axis=1)
        dw_in, done2 = carried(mm, f"dw_in{l}", A['h'], dproj, ta=True, comm=half_exchange(halves2) if carry else None)
        if carry:
            red.update(zip(grp2, done2))
        per_layer['w_in'][l] = _unpad_w_in(dw_in).reshape(D_MODEL, N_CHIPS, D_IN_PROJ // N_CHIPS).transpose(1, 0, 2)
        dh = mm(f"d_h{l}", dproj, A['wpad'], tb=True, add=dz1, add_scale=alpha)
    dx, dg, db = ln_bwd("ln_in_bwd", s, dh, x, _row(W['ln_in_g']))
    G = dict(per_layer)
    G['ln_in_g'], G['ln_in_b'] = dg[0], db[0]
    for n, k in red:
        G[n][k] = None
    G['reduced'] = red
    return loss_row, dx, G


def _ln_mix_bwd(l, s, dz, dh1_mm, A, W, alpha):
    ins = [('row', dz, D_MODEL, 0), ('row', dh1_mm, D_MODEL, 0), ('row', A['mix'], D_MODEL, 0), ('row', A['h'], D_MODEL, 0),
           ('full', _row(W['ln_mix_g'][l]))]

    def fn(i, j, dzv, dmv, xv, rv, g_ref, o_ref, dg_ref, db_ref):
        xh, r = _ln_stats(alpha * rv + xv)
        d, dg, db = _ln_bwd(alpha * dzv + dmv, xh, r, g_ref[...])
        o_ref[...] = d
        dg_ref[...] += dg
        db_ref[...] += db

    return tiled_call(f"ln_mix_bwd{l}", fn, s, TM, ins, [('row', D_MODEL, D_MODEL, 0, F32), ('acc', (1, D_MODEL)),
                                                         ('acc', (1, D_MODEL))])


ANY = pl.BlockSpec(memory_space=pl.ANY)


def _pos():
    return lax.axis_index("x"), lax.axis_index("y"), lax.axis_index("c")


def _rcopy(src, dst, ssem, rsem, dev):
    return pltpu.make_async_remote_copy(src_ref=src, dst_ref=dst, send_sem=ssem, recv_sem=rsem, device_id=dev,
                                        device_id_type=MESH)


class Comm:
    def __init__(self, operands, out_shape, sems, start, finish, aliases=None):
        self.operands, self.out_shape, self.sems = list(operands), list(out_shape), list(sems)
        self.start, self.finish, self.aliases = start, finish, dict(aliases or {})


def join_comms(comms):
    comms = [cm for cm in comms if cm is not None]
    if not comms:
        return None
    offs, oi, oo, os_ = [], 0, 0, 0
    for cm in comms:
        offs.append((oi, oo, os_))
        oi, oo, os_ = oi + len(cm.operands), oo + len(cm.out_shape), os_ + len(cm.sems)

    def part(fn_name):
        def run(ins, outs, sems):
            for cm, (a, b, s) in zip(comms, offs):
                getattr(cm, fn_name)(ins[a:a + len(cm.operands)], outs[b:b + len(cm.out_shape)], sems[s:s + len(cm.sems)])
        return run

    aliases = {a + i: b + o for cm, (a, b, s) in zip(comms, offs) for i, o in cm.aliases.items()}
    return Comm([v for cm in comms for v in cm.operands], [v for cm in comms for v in cm.out_shape],
                [v for cm in comms for v in cm.sems], part("start"), part("finish"), aliases)


def split_comm_results(comms, res):
    out, p = [], 0
    for cm in comms:
        if cm is None:
            out.append(None)
        else:
            out.append(list(res[p:p + len(cm.out_shape)]))
            p += len(cm.out_shape)
    return out


def run_comm(name, comm):
    n_in, n_out = len(comm.operands), len(comm.out_shape)

    def body(*refs):
        ins, outs, sems = refs[:n_in], refs[n_in:n_in + n_out], refs[n_in + n_out:]
        comm.start(ins, outs, sems)
        comm.finish(ins, outs, sems)

    return pl.pallas_call(body, name=name, in_specs=[ANY] * n_in, out_specs=[ANY] * n_out, out_shape=comm.out_shape,
                          scratch_shapes=[pltpu.SemaphoreType.DMA((k,)) for k in comm.sems],
                          input_output_aliases=comm.aliases)(*comm.operands)


def gather_over_chips(parts):
    n = len(parts)
    layer = [p[1] if isinstance(p, tuple) else None for p in parts]
    parts = [p[0] if isinstance(p, tuple) else p for p in parts]
    shapes = [p.shape if l is None else (1,) + p.shape[1:] for p, l in zip(parts, layer)]
    split = [shp[1] % 32 == 0 for shp in shapes]

    def geometry():
        x, y, c = _pos()
        return x, y, c, 2 * x + y, [(1 - x, y), (x, 1 - y), (1 - x, 1 - y)]

    def rows(t, half):
        a2 = shapes[t][1] // 2
        return pl.ds(half * a2, a2)

    def ici(srcs, dsts, sems, j, t, px, py, c, me):
        k = n * j + t
        src = srcs[t] if layer[t] is None else srcs[t].at[pl.ds(layer[t], 1)]
        if split[t]:
            return _rcopy(src.at[:, rows(t, c), :], dsts[t].at[me, :, rows(t, c), :], sems[0].at[k], sems[1].at[k], (px, py, c))
        return _rcopy(src, dsts[t].at[me], sems[0].at[k], sems[1].at[k], (px, py, c))

    def start(srcs, dsts, sems):
        x, y, c, me, chips = geometry()
        for j, (px, py) in enumerate(chips):
            for t in range(n):
                ici(srcs, dsts, sems, j, t, px, py, c, me).start()

    def finish(srcs, dsts, sems):
        x, y, c, me, chips = geometry()
        fwd = []
        for j, (px, py) in enumerate(chips):
            q = 2 * px + py
            for t in range(n):
                k = n * j + t
                if split[t]:
                    got = dsts[t].at[q, :, rows(t, c), :]
                    _rcopy(got, got, sems[0].at[k], sems[1].at[k], (px, py, c)).wait_recv()
                    fw = _rcopy(got, got, sems[2].at[k], sems[3].at[k], (x, y, 1 - c))
                    fw.start()
                    fwd.append(fw)
                else:
                    _rcopy(dsts[t].at[q], dsts[t].at[q], sems[0].at[k], sems[1].at[k], (px, py, c)).wait_recv()
        for j, (px, py) in enumerate(chips):
            q = 2 * px + py
            for t in range(n):
                if split[t]:
                    other = dsts[t].at[q, :, rows(t, 1 - c), :]
                    _rcopy(other, other, sems[2].at[n * j + t], sems[3].at[n * j + t], (x, y, 1 - c)).wait_recv()
                ici(srcs, dsts, sems, j, t, px, py, c, me).wait_send()
        for fw in fwd:
            fw.wait_send()

    return Comm(parts, [jax.ShapeDtypeStruct((N_CHIPS,) + shp, p.dtype) for p, shp in zip(parts, shapes)], [3 * n] * 4, start, finish)


def pair_exchange(xs):
    n = len(xs)

    def copies(srcs, dsts, sems):
        x, y, c = _pos()
        return [_rcopy(srcs[t].at[:, pl.ds((1 - c) * (xs[t].shape[1] // 2), xs[t].shape[1] // 2), :], dsts[t],
                       sems[0].at[t], sems[1].at[t], (x, y, 1 - c)) for t in range(n)]

    def start(srcs, dsts, sems):
        for cp in copies(srcs, dsts, sems):
            cp.start()

    def finish(srcs, dsts, sems):
        for cp in copies(srcs, dsts, sems):
            cp.wait()

    return Comm(xs, [jax.ShapeDtypeStruct((a.shape[0], a.shape[1] // 2, a.shape[2]), a.dtype) for a in xs], [n, n], start, finish)


def chip_exchange(ps):
    n = len(ps)

    def geometry():
        x, y, c = _pos()
        return c, 2 * x + y, [(1 - x, y), (x, 1 - y), (1 - x, 1 - y)]

    def start(srcs, dsts, sems):
        c, me, chips = geometry()
        for j, (px, py) in enumerate(chips):
            for t in range(n):
                _rcopy(srcs[t].at[2 * px + py], dsts[t].at[me], sems[0].at[n * j + t], sems[1].at[n * j + t], (px, py, c)).start()

    def finish(srcs, dsts, sems):
        c, me, chips = geometry()
        for j, (px, py) in enumerate(chips):
            for t in range(n):
                _rcopy(srcs[t].at[2 * px + py], dsts[t].at[2 * px + py], sems[0].at[n * j + t], sems[1].at[n * j + t], (px, py, c)).wait()

    return Comm(ps, [jax.ShapeDtypeStruct(p.shape, p.dtype) for p in ps], [3 * n, 3 * n], start, finish)


def half_exchange(bufs):
    n = len(bufs)

    def copies(ins, outs, sems):
        x, y, c = _pos()
        cps = []
        for t in range(n):
            a2 = bufs[t].shape[1] // 2
            mine = pl.ds(c * a2, a2)
            cps.append(_rcopy(ins[t].at[:, mine, :], outs[t].at[:, mine, :], sems[0].at[t], sems[1].at[t], (x, y, 1 - c)))
        return cps

    def start(ins, outs, sems):
        for cp in copies(ins, outs, sems):
            cp.start()

    def finish(ins, outs, sems):
        for cp in copies(ins, outs, sems):
            cp.wait()

    return Comm(bufs, [jax.ShapeDtypeStruct(b.shape, b.dtype) for b in bufs], [n, n], start, finish, {t: t for t in range(n)})


LANES = 1024
ROW_TILE_BYTES = 2 * 1024 * 1024


def _row_tile(r, row_bytes):
    if r * row_bytes <= ROW_TILE_BYTES:
        return r
    best = None
    for t in range(16, r, 16):
        if r % t == 0 and t * row_bytes <= ROW_TILE_BYTES:
            best = t
    assert best is not None, (r, row_bytes)
    return best


def pair_sum(name, x, r1, c_arr, out_dtype):
    q, a2, b = r1.shape
    tr = _row_tile(a2, b * 4)
    nb = a2 // tr

    def body(c_ref, x_ref, r_ref, o_ref):
        o_ref[...] = (x_ref[...] + r_ref[...]).astype(out_dtype)

    grid_spec = pltpu.PrefetchScalarGridSpec(
        num_scalar_prefetch=1, grid=(q, nb),
        in_specs=[pl.BlockSpec((None, tr, b), lambda k, i, c_ref: (k, c_ref[0] * nb + i, 0)),
                  pl.BlockSpec((None, tr, b), lambda k, i, c_ref: (k, i, 0))],
        out_specs=pl.BlockSpec((None, tr, b), lambda k, i, c_ref: (k, i, 0)))
    return pl.pallas_call(body, name=name, grid_spec=grid_spec, out_shape=jax.ShapeDtypeStruct(r1.shape, out_dtype),
                          compiler_params=_cparams(("parallel", "parallel")))(c_arr, x, r1)


def chip_sum(name, recv, own, idx):
    nl = len(recv)
    k, a2, b = recv[0].shape
    tr = _row_tile(a2, k * b * recv[0].dtype.itemsize)
    nb = a2 // tr

    def body(idx_ref, *refs):
        rs, os_, o_ref = refs[:nl], refs[nl:2 * nl], refs[2 * nl]
        l = pl.program_id(0)
        me = idx_ref[1]
        for li in range(nl):
            @pl.when(l == li)
            def _(li=li):
                acc = None
                for q in range(k):
                    term = jnp.where(me == q, os_[li][...], rs[li][q]).astype(F32)
                    acc = term if acc is None else acc + term
                o_ref[...] = acc

    def pick(li):
        return lambda l, i, idx_ref: jnp.where(l == li, i, 0)

    in_specs = [pl.BlockSpec((k, tr, b), lambda l, i, idx_ref, _p=pick(li): (0, _p(l, i, idx_ref), 0)) for li in range(nl)]
    in_specs += [pl.BlockSpec((None, tr, b), lambda l, i, idx_ref, _p=pick(li): (idx_ref[1], _p(l, i, idx_ref), 0)) for li in range(nl)]
    grid_spec = pltpu.PrefetchScalarGridSpec(
        num_scalar_prefetch=1, grid=(nl, nb), in_specs=in_specs,
        out_specs=pl.BlockSpec((None, tr, b), lambda l, i, idx_ref: (l, idx_ref[0] * nb + i, 0)))
    return pl.pallas_call(body, name=name, grid_spec=grid_spec, out_shape=jax.ShapeDtypeStruct((nl, 2 * a2, b), F32),
                          compiler_params=_cparams(("arbitrary", "arbitrary")))(idx, *recv, *own)


def adamw(name, gs, w, m, v):
    nl, r, b = w.shape
    assert len(gs) == nl
    tr = _row_tile(r, b * 4)

    def body(*refs):
        g_refs, (w_ref, m_ref, v_ref, go_ref, d_ref, nm_ref, nv_ref) = refs[:nl], refs[nl:]
        l = pl.program_id(0)
        for li in range(nl):
            @pl.when(l == li)
            def _(li=li):
                gv = g_refs[li][...]
                go_ref[...] = gv
                mn = ADAM_B1 * m_ref[...] + (1.0 - ADAM_B1) * gv
                vn = ADAM_B2 * v_ref[...] + (1.0 - ADAM_B2) * (gv * gv)
                m_hat = mn / (1.0 - ADAM_B1 ** ADAM_STEP)
                v_hat = vn / (1.0 - ADAM_B2 ** ADAM_STEP)
                d_ref[...] = -ADAM_LR * (m_hat / (jnp.sqrt(v_hat) + ADAM_EPS) + ADAM_WD * w_ref[...])
                nm_ref[...] = mn
                nv_ref[...] = vn

    g_specs = [pl.BlockSpec((None, tr, b), lambda l, i, _li=li: (0, jnp.where(l == _li, i, 0), 0)) for li in range(nl)]
    spec = pl.BlockSpec((None, tr, b), lambda l, i: (l, i, 0))
    sds = jax.ShapeDtypeStruct(w.shape, F32)
    return pl.pallas_call(body, name=name, grid=(nl, r // tr), in_specs=g_specs + [spec] * 3, out_specs=[spec] * 4,
                          out_shape=[sds] * 4, compiler_params=_cparams(("arbitrary", "arbitrary")))(*gs, w, m, v)


def adamw_many(name, gs, ws, ms, vs):
    n = len(gs)

    def body(*refs):
        for k in range(n):
            g_ref, w_ref, m_ref, v_ref, go_ref, d_ref, nm_ref, nv_ref = (refs[j * n + k] for j in range(8))
            gv = g_ref[...]
            go_ref[...] = gv
            mn = ADAM_B1 * m_ref[...] + (1.0 - ADAM_B1) * gv
            vn = ADAM_B2 * v_ref[...] + (1.0 - ADAM_B2) * (gv * gv)
            m_hat = mn / (1.0 - ADAM_B1 ** ADAM_STEP)
            v_hat = vn / (1.0 - ADAM_B2 ** ADAM_STEP)
            d_ref[...] = -ADAM_LR * (m_hat / (jnp.sqrt(v_hat) + ADAM_EPS) + ADAM_WD * w_ref[...])
            nm_ref[...] = mn
            nv_ref[...] = vn

    vm = pl.BlockSpec(memory_space=pltpu.VMEM)
    res = pl.pallas_call(body, name=name, in_specs=[vm] * (4 * n), out_specs=[vm] * (4 * n),
                         out_shape=[jax.ShapeDtypeStruct(a.shape, F32) for _ in range(4) for a in ws],
                         compiler_params=pltpu.CompilerParams(vmem_limit_bytes=VMEM_LIMIT))(*gs, *ws, *ms, *vs)
    return [list(res[j * n:(j + 1) * n]) for j in range(4)]


def _small_layout(shapes):
    views, windows, r0 = [], [], 0
    for shp in shapes:
        if len(shp) == 4:
            nblk = shp[0] * shp[1]
            assert shp[2:] == (CHUNK, 128) and nblk * 128 <= LANES
            views.append((nblk, CHUNK, 128))
            windows.append([((j,), r0, CHUNK, 128 * j, 128) for j in range(nblk)])
            r0 += CHUNK
            continue
        a, c = (1, shp[0]) if len(shp) == 1 else (int(np.prod(shp[:-1])), shp[-1])
        views.append((a, c))
        wins = []
        for j in range(-(-c // LANES)):
            w = min(LANES, c - j * LANES)
            wins.append(((slice(None), slice(j * LANES, j * LANES + w)), r0, a, 0, w))
            r0 += 8 * -(-a // 8)
        windows.append(wins)
    return 32 * -(-r0 // 32), views, windows


def pack_small(name, arrs, rows, windows):
    def body(*refs):
        ins, o_ref = refs[:-1], refs[-1]
        o_ref[...] = jnp.zeros_like(o_ref)
        for x_ref, wins in zip(ins, windows):
            for idx, r0, nr, l0, nl in wins:
                o_ref[r0:r0 + nr, l0:l0 + nl] = x_ref[idx]

    vm = pl.BlockSpec(memory_space=pltpu.VMEM)
    return pl.pallas_call(body, name=name, in_specs=[vm] * len(arrs), out_specs=vm,
                          out_shape=jax.ShapeDtypeStruct((rows, LANES), F32))(*arrs)


def adamw_windows(name, g_pack, windows, ws, ms, vs):
    n = len(ws)

    def body(g_ref, *refs):
        for k in range(n):
            w_ref, m_ref, v_ref, go_ref, d_ref, nm_ref, nv_ref = (refs[j * n + k] for j in range(7))
            for idx, r0, nr, l0, nl in windows[k]:
                gv = g_ref[r0:r0 + nr, l0:l0 + nl]
                go_ref[idx] = gv
                mn = ADAM_B1 * m_ref[idx] + (1.0 - ADAM_B1) * gv
                vn = ADAM_B2 * v_ref[idx] + (1.0 - ADAM_B2) * (gv * gv)
                m_hat = mn / (1.0 - ADAM_B1 ** ADAM_STEP)
                v_hat = vn / (1.0 - ADAM_B2 ** ADAM_STEP)
                d_ref[idx] = -ADAM_LR * (m_hat / (jnp.sqrt(v_hat) + ADAM_EPS) + ADAM_WD * w_ref[idx])
                nm_ref[idx] = mn
                nv_ref[idx] = vn

    vm = pl.BlockSpec(memory_space=pltpu.VMEM)
    res = pl.pallas_call(body, name=name, in_specs=[vm] * (1 + 3 * n), out_specs=[vm] * (4 * n),
                         out_shape=[jax.ShapeDtypeStruct(a.shape, F32) for _ in range(4) for a in ws],
                         compiler_params=pltpu.CompilerParams(vmem_limit_bytes=VMEM_LIMIT))(g_pack, *ws, *ms, *vs)
    return [list(res[j * n:(j + 1) * n]) for j in range(4)]


def _pack_rows(shapes):
    return [8 * -(-int(np.prod(s)) // (8 * LANES)) for s in shapes]


def _pack(arrs, rows):
    pieces = []
    for a, r in zip(arrs, _pack_rows([a.shape for a in arrs])):
        flat = a.reshape(-1)
        pieces.append(jnp.pad(flat, (0, r * LANES - flat.shape[0])).reshape(r, LANES))
    used = sum(p.shape[0] for p in pieces)
    assert used <= rows
    if used < rows:
        pieces.append(jnp.zeros((rows - used, LANES), F32))
    return jnp.concatenate(pieces, axis=0)


def _unpack(pack, shapes):
    out, off = [], 0
    for shp, r in zip(shapes, _pack_rows(shapes)):
        n = int(np.prod(shp))
        out.append(pack[off:off + r].reshape(-1)[:n].reshape(shp))
        off += r
    return out


BIG_NAMES = ['w_in', 'w_out', 'ffn_w_up', 'ffn_w_down']
MED_NAMES = [n for n in SHARDED_NAMES if n not in BIG_NAMES]
MED_ROWS = 96


def kernel(x, ln_in_g, ln_in_b, w_in, conv_a_w, conv_a_b, ln_a_g, ln_a_b, qk_norm_q, qk_norm_k, sgu_ln_g, sgu_ln_b, sgu_w, sgu_b, mla_q_norm, mla_w_uq, mla_kv_norm, mla_w_ukv, w_out, ln_mix_g, ln_mix_b, ffn_w_up, ffn_conv_w, ffn_conv_b, ffn_w_down, ln_ffn_g, ln_ffn_b, loss_target, m_ln_in_g, m_ln_in_b, m_w_in, m_conv_a_w, m_conv_a_b, m_ln_a_g, m_ln_a_b, m_qk_norm_q, m_qk_norm_k, m_sgu_ln_g, m_sgu_ln_b, m_sgu_w, m_sgu_b, m_mla_q_norm, m_mla_w_uq, m_mla_kv_norm, m_mla_w_ukv, m_w_out, m_ln_mix_g, m_ln_mix_b, m_ffn_w_up, m_ffn_conv_w, m_ffn_conv_b, m_ffn_w_down, m_ln_ffn_g, m_ln_ffn_b, v_ln_in_g, v_ln_in_b, v_w_in, v_conv_a_w, v_conv_a_b, v_ln_a_g, v_ln_a_b, v_qk_norm_q, v_qk_norm_k, v_sgu_ln_g, v_sgu_ln_b, v_sgu_w, v_sgu_b, v_mla_q_norm, v_mla_w_uq, v_mla_kv_norm, v_mla_w_ukv, v_w_out, v_ln_mix_g, v_ln_mix_b, v_ffn_w_up, v_ffn_conv_w, v_ffn_conv_b, v_ffn_w_down, v_ln_ffn_g, v_ln_ffn_b):
    loc = dict(locals())
    w_loc = {n: loc[n] for n in WEIGHT_NAMES}
    m_loc = {n: loc["m_" + n] for n in WEIGHT_NAMES}
    v_loc = {n: loc["v_" + n] for n in WEIGHT_NAMES}
    me_chip = 2 * lax.axis_index("x") + lax.axis_index("y")
    idx = jnp.stack([lax.axis_index("c"), me_chip]).astype(jnp.int32)

    def wire(n):
        return w_loc[n].astype(BF16) if n in BF16_WIRE else w_loc[n]

    assert DEPTH == 2
    early = [n for n in SHARDED_NAMES if n not in LATE_NAMES]
    wired = {n: wire(n) for n in SHARDED_NAMES}
    W = {n: w_loc[n] for n in REPL_NAMES}
    for n, g in zip(early, run_comm("gather_early", gather_over_chips([(wired[n], 0) for n in early]))):
        W[n] = [_assemble(n, wired[n][0:1], g, me_chip)[0], None]

    dist = dict(idx=idx, me=me_chip, late={n: wired[n] for n in LATE_NAMES}, next=(early, [wired[n] for n in early]))
    loss_row, dx, G = local_step(x[0], loss_target[0], W, dist)
    loss = lax.psum(loss_row[0, 0], ("x", "y", "c"))
    red = G['reduced']

    med = []
    for q in range(N_CHIPS):
        pieces = [lax.slice_in_dim(jnp.stack(G[n]), q * w_loc[n].shape[2], (q + 1) * w_loc[n].shape[2], axis=2) for n in MED_NAMES]
        med.append(_pack(pieces, MED_ROWS))
    rest = [(n, l) for n in BIG_NAMES for l in range(DEPTH) if (n, l) not in red]
    small_rows, views, windows = _small_layout([w_loc[n].shape for n in REPL_NAMES])
    small = [(jnp.stack(G[n]) if isinstance(G[n], list) else G[n]).reshape(v) for n, v in zip(REPL_NAMES, views)]
    small_pack = pack_small("pack_small", small, small_rows, windows)
    xs = [G[n][l] for n, l in rest] + [jnp.stack(med), jnp.broadcast_to(small_pack[None], (N_CHIPS, small_rows, LANES))]
    wire_dt = [BF16] * (len(xs) - 1) + [F32]
    r1 = run_comm("pair_exchange", pair_exchange(xs))
    ps = [pair_sum(f"pair_sum{t}", xv, rv, idx, dt) for t, (xv, rv, dt) in enumerate(zip(xs, r1, wire_dt))]
    cs = run_comm("chip_exchange", chip_exchange(ps))
    halves = [chip_sum(f"chip_sum{t}", [cv], [pv], idx) for t, (cv, pv) in enumerate(zip(cs, ps))]
    done = run_comm("half_exchange", half_exchange(halves))
    red.update(zip(rest, done[:-2]))
    g_med, g_small = done[-2], done[-1]

    outs = {k: {} for k in ("grad", "delta", "new_m", "new_v")}
    for n in BIG_NAMES:
        res = adamw(f"adamw_{n}", [red[(n, l)] for l in range(DEPTH)], w_loc[n], m_loc[n], v_loc[n])
        for k, v in zip(("grad", "delta", "new_m", "new_v"), res):
            outs[k][n] = v
    res = adamw_many("adamw_med", _unpack(g_med[0], [w_loc[n].shape for n in MED_NAMES]),
                     *[[d[n] for n in MED_NAMES] for d in (w_loc, m_loc, v_loc)])
    for k, vals in zip(("grad", "delta", "new_m", "new_v"), res):
        outs[k].update(zip(MED_NAMES, vals))
    res = adamw_windows("adamw_small", g_small[0], windows,
                        *[[d[n].reshape(v) for n, v in zip(REPL_NAMES, views)] for d in (w_loc, m_loc, v_loc)])
    for k, vals in zip(("grad", "delta", "new_m", "new_v"), res):
        outs[k].update({n: v.reshape(w_loc[n].shape) for n, v in zip(REPL_NAMES, vals)})
    return (loss, dx[None], *[outs["grad"][n] for n in WEIGHT_NAMES], *[outs["delta"][n] for n in WEIGHT_NAMES],
            *[outs["new_m"][n] for n in WEIGHT_NAMES], *[outs["new_v"][n] for n in WEIGHT_NAMES])
```

```python
import functools
import math

import jax
import jax.numpy as jnp
import numpy as np
from jax import lax
from jax.experimental import pallas as pl
from jax.experimental.pallas import tpu as pltpu

F32 = jnp.float32
BF16 = jnp.bfloat16

D_MODEL = 1024
DEPTH = 2
GRID_W = 64
GROUP_W = 256
HEAD_DIM = 64
CONV_A_WIDTH = 31
CHUNK = 128
MLA_Q_LORA = 192
MLA_KV_LORA = 128
MLA_NOPE = 64
MLA_ROPE = 32
D_FF = 2816
ROPE_THETA = 10000.0
DEEPNORM_ALPHA = (2 * DEPTH) ** 0.25
LN_EPS = 1e-5
RMS_EPS = 1e-6
D_IN_PROJ = 1888
PROJ_W = 2048

ADAM_LR = 0.001
ADAM_B1 = 0.9
ADAM_B2 = 0.999
ADAM_EPS = 1e-08
ADAM_WD = 0.01
ADAM_STEP = 10

VMEM_LIMIT = 56 * 1024 * 1024
MESH = pl.DeviceIdType.MESH
N_CHIPS = 4
N_DEV = 8

WEIGHT_NAMES = ['ln_in_g', 'ln_in_b', 'w_in', 'conv_a_w', 'conv_a_b', 'ln_a_g', 'ln_a_b', 'qk_norm_q', 'qk_norm_k',
                'sgu_ln_g', 'sgu_ln_b', 'sgu_w', 'sgu_b', 'mla_q_norm', 'mla_w_uq', 'mla_kv_norm', 'mla_w_ukv', 'w_out',
                'ln_mix_g', 'ln_mix_b', 'ffn_w_up', 'ffn_conv_w', 'ffn_conv_b', 'ffn_w_down', 'ln_ffn_g', 'ln_ffn_b']
SHARDED_AXIS = {'w_in': 2, 'conv_a_w': 2, 'mla_w_uq': 2, 'mla_w_ukv': 2, 'w_out': 1, 'ffn_w_up': 2, 'ffn_conv_w': 2,
                'ffn_w_down': 1}
SHARDED_NAMES = [n for n in WEIGHT_NAMES if n in SHARDED_AXIS]
REPL_NAMES = [n for n in WEIGHT_NAMES if n not in SHARDED_AXIS]
BF16_WIRE = ('w_in', 'mla_w_uq', 'mla_w_ukv', 'w_out', 'ffn_w_up', 'ffn_w_down')


def _cparams(sem):
    return pltpu.CompilerParams(dimension_semantics=sem, vmem_limit_bytes=VMEM_LIMIT)


def _pick(n, cands):
    for c in cands:
        if n % c == 0:
            return c
    return n


def carried_call(name, body, grid, in_specs, out_specs, out_shape, scratch, semantics, args, comm=None):
    if comm is None:
        res = pl.pallas_call(body, name=name, grid=grid, in_specs=in_specs, out_specs=out_specs, out_shape=out_shape,
                             scratch_shapes=scratch, compiler_params=_cparams(semantics))(*args)
        return list(res), None
    n_in, n_out, n_scr = len(in_specs), len(out_specs), len(scratch)
    nci, nco = len(comm.operands), len(comm.out_shape)

    def carried(*refs):
        o0 = n_in + nci
        s0 = o0 + n_out + nco
        ins, cins = refs[:n_in], refs[n_in:o0]
        outs, couts = refs[o0:o0 + n_out], refs[o0 + n_out:s0]
        scr, sems = refs[s0:s0 + n_scr], refs[s0 + n_scr:]
        ids = [pl.program_id(d) for d in range(len(grid))]
        first = functools.reduce(lambda u, v: u & v, [i == 0 for i in ids])
        last = functools.reduce(lambda u, v: u & v, [i == g - 1 for i, g in zip(ids, grid)])

        @pl.when(first)
        def _():
            comm.start(cins, couts, sems)

        body(*ins, *outs, *scr)

        @pl.when(last)
        def _():
            comm.finish(cins, couts, sems)

    res = pl.pallas_call(
        carried, name=name, grid=grid, in_specs=list(in_specs) + [ANY] * nci, out_specs=list(out_specs) + [ANY] * nco,
        out_shape=list(out_shape) + comm.out_shape, scratch_shapes=list(scratch) + [pltpu.SemaphoreType.DMA((k,)) for k in comm.sems],
        input_output_aliases={n_in + i: n_out + o for i, o in comm.aliases.items()},
        compiler_params=_cparams(("arbitrary",) * len(grid)))(*args, *comm.operands)
    return list(res[:n_out]), list(res[n_out:])


def mm(name, a, b, *, ta=False, tb=False, add=None, add_scale=1.0, out_dtype=F32, chip_major=False, comm=None,
       a_pair=False, b_pair=False):
    assert not (a_pair and ta) and not (b_pair and tb)
    a_shape = (a.shape[1], 2 * a.shape[2]) if a_pair else a.shape
    b_shape = (b.shape[1], 2 * b.shape[2]) if b_pair else b.shape
    m, k = (a_shape[1], a_shape[0]) if ta else a_shape
    n = b_shape[0] if tb else b_shape[1]
    assert (b_shape[1] if tb else b_shape[0]) == k
    tm = _pick(m, (1024, 1408, 512, 256, 128))
    tn = n // N_CHIPS if chip_major else _pick(n, (1024, 1408, 512, 256, 128))
    tk = _pick(k, (1024, 1408, 512, 256, 128))
    nk = k // tk
    a_spec = pl.BlockSpec((tk, tm), lambda i, j, kk: (kk, i)) if ta else pl.BlockSpec((tm, tk), lambda i, j, kk: (i, kk))
    b_spec = pl.BlockSpec((tn, tk), lambda i, j, kk: (j, kk)) if tb else pl.BlockSpec((tk, tn), lambda i, j, kk: (kk, j))
    if a_pair:
        per = a.shape[2] // tk
        assert per * tk == a.shape[2]
        a_spec = pl.BlockSpec((None, tm, tk), lambda i, j, kk: (kk // per, i, kk % per))
    if b_pair:
        per = b.shape[2] // tn
        assert per * tn == b.shape[2]
        b_spec = pl.BlockSpec((None, tk, tn), lambda i, j, kk: (j // per, kk, j % per))
    in_specs = [a_spec, b_spec]
    args = [a, b]
    if add is not None:
        in_specs.append(pl.BlockSpec((tm, tn), lambda i, j, kk: (i, j)))
        args.append(add)
    dims = (((0 if ta else 1,), (1 if tb else 0,)), ((), ()))

    def body(*refs):
        a_ref, b_ref = refs[0], refs[1]
        add_ref = refs[2] if add is not None else None
        o_ref, acc_ref = refs[-2], refs[-1]
        kk = pl.program_id(2)
        part = lax.dot_general(a_ref[...].astype(BF16), b_ref[...].astype(BF16), dims, preferred_element_type=F32)

        @pl.when(kk == 0)
        def _():
            acc_ref[...] = part

        @pl.when(kk > 0)
        def _():
            acc_ref[...] += part

        @pl.when(kk == nk - 1)
        def _():
            r = acc_ref[...]
            if add_ref is not None:
                r = r + add_scale * add_ref[...].astype(F32)
            o_ref[...] = r.astype(out_dtype)

    if chip_major:
        out_spec = pl.BlockSpec((None, tm, tn), lambda i, j, kk: (j, i, 0))
        out_shape = jax.ShapeDtypeStruct((N_CHIPS, m, tn), out_dtype)
    else:
        out_spec = pl.BlockSpec((tm, tn), lambda i, j, kk: (i, j))
        out_shape = jax.ShapeDtypeStruct((m, n), out_dtype)
    res, cres = carried_call(name, body, (m // tm, n // tn, nk), in_specs, [out_spec], [out_shape], [pltpu.VMEM((tm, tn), F32)],
                             ("parallel", "parallel", "arbitrary"), args, comm)
    return res[0] if comm is None else (res[0], cres)


def _cb(cb):
    return cb if callable(cb) else (lambda j, _c=cb: _c)


def tiled_call(name, fn, s, tm, ins, outs, ncol=1):
    nrow = s // tm
    in_specs, args, kinds = [], [], []
    for it in ins:
        kind = it[0]
        if kind == 'row':
            _, arr, w, cb = it
            in_specs.append(pl.BlockSpec((tm, w), lambda j, i, _c=_cb(cb): (i, _c(j))))
            args.append(arr)
            kinds.append(('row',))
        elif kind == 'halo':
            _, arr, w, cb, h = it
            r = tm // h
            nh = s // h
            in_specs.append(pl.BlockSpec((h, w), lambda j, i, _c=_cb(cb), _r=r: (jnp.maximum(i * _r - 1, 0), _c(j))))
            in_specs.append(pl.BlockSpec((tm, w), lambda j, i, _c=_cb(cb): (i, _c(j))))
            in_specs.append(pl.BlockSpec((h, w), lambda j, i, _c=_cb(cb), _r=r, _n=nh: (jnp.minimum((i + 1) * _r, _n - 1), _c(j))))
            args += [arr, arr, arr]
            kinds.append(('halo',))
        elif kind == 'full':
            arr = it[1]
            in_specs.append(pl.BlockSpec(arr.shape, lambda j, i, _n=arr.ndim: (0,) * _n))
            args.append(arr)
            kinds.append(('ref',))
        elif kind == 'col':
            _, arr, w, cb = it
            in_specs.append(pl.BlockSpec((arr.shape[0], w), lambda j, i, _c=_cb(cb): (0, _c(j))))
            args.append(arr)
            kinds.append(('ref',))
        elif kind == 'hm':
            arr = it[1]
            in_specs.append(pl.BlockSpec((arr.shape[0], tm, arr.shape[2]), lambda j, i: (0, i, 0)))
            args.append(arr)
            kinds.append(('ref',))
        else:
            raise ValueError(kind)
    out_specs, out_shapes, okinds, into = [], [], [], []
    for ot in outs:
        kind = ot[0]
        if kind == 'row':
            _, wt, w, cb, dt = ot
            out_specs.append(pl.BlockSpec((tm, w), lambda j, i, _c=_cb(cb): (i, _c(j))))
            out_shapes.append(jax.ShapeDtypeStruct((s, wt), dt))
            okinds.append('row')
        elif kind == 'hm':
            _, hh, d, dt = ot
            out_specs.append(pl.BlockSpec((hh, tm, d), lambda j, i: (0, i, 0)))
            out_shapes.append(jax.ShapeDtypeStruct((hh, s, d), dt))
            okinds.append('hm')
        elif kind == 'acc':
            shape = ot[1]
            out_specs.append(pl.BlockSpec(shape, lambda j, i, _n=len(shape): (0,) * _n))
            out_shapes.append(jax.ShapeDtypeStruct(shape, F32))
            okinds.append('acc')
        elif kind == 'colacc':
            _, r, wt, w, cb = ot
            out_specs.append(pl.BlockSpec((r, w), lambda j, i, _c=_cb(cb): (0, _c(j))))
            out_shapes.append(jax.ShapeDtypeStruct((r, wt), F32))
            okinds.append('colacc')
        elif kind == 'row_into':
            _, buf, w, cb = ot
            out_specs.append(pl.BlockSpec((tm, w), lambda j, i, _c=_cb(cb): (i, _c(j))))
            out_shapes.append(jax.ShapeDtypeStruct(buf.shape, buf.dtype))
            okinds.append('row')
            into.append((len(out_specs) - 1, buf))
        elif kind == 'row3':
            _, n0, wt, w, cb, dt = ot
            out_specs.append(pl.BlockSpec((n0, tm, w), lambda j, i, _c=_cb(cb): (0, i, _c(j))))
            out_shapes.append(jax.ShapeDtypeStruct((n0, s, wt), dt))
            okinds.append('row')
        else:
            raise ValueError(kind)
    n_in = len(in_specs)
    aliases = {n_in + e: o for e, (o, _) in enumerate(into)}
    in_specs = in_specs + [pl.BlockSpec(memory_space=pl.ANY)] * len(into)
    args = args + [buf for _, buf in into]

    def body(*refs):
        j = pl.program_id(0)
        i = pl.program_id(1)
        in_refs, out_refs = refs[:n_in], refs[n_in + len(into):]
        items, p = [], 0
        for kd in kinds:
            if kd[0] == 'row':
                items.append(in_refs[p][...])
                p += 1
            elif kd[0] == 'halo':
                prev, cen, nxt = in_refs[p][...], in_refs[p + 1][...], in_refs[p + 2][...]
                prev = jnp.where(i == 0, jnp.zeros_like(prev), prev)
                nxt = jnp.where(i == nrow - 1, jnp.zeros_like(nxt), nxt)
                items.append(jnp.concatenate([prev, cen, nxt], axis=0))
                p += 3
            else:
                items.append(in_refs[p])
                p += 1
        for o_ref, kd in zip(out_refs, okinds):
            if kd == 'acc':
                @pl.when((i == 0) & (j == 0))
                def _(o_ref=o_ref):
                    o_ref[...] = jnp.zeros_like(o_ref)
            elif kd == 'colacc':
                @pl.when(i == 0)
                def _(o_ref=o_ref):
                    o_ref[...] = jnp.zeros_like(o_ref)
        fn(i, j, *items, *out_refs)

    res = pl.pallas_call(
        body, name=name, grid=(ncol, nrow), in_specs=in_specs, out_specs=out_specs, out_shape=out_shapes,
        input_output_aliases=aliases, compiler_params=_cparams(("arbitrary", "arbitrary")),
    )(*args)
    return res


def _sigmoid(x):
    return 1.0 / (1.0 + jnp.exp(-x))


def _ln_stats(x):
    mu = jnp.mean(x, axis=1, keepdims=True)
    xc = x - mu
    var = jnp.mean(xc * xc, axis=1, keepdims=True)
    r = lax.rsqrt(var + LN_EPS)
    return xc * r, r


def _ln_bwd(dy, xh, r, g):
    dxh = dy * g
    dx = r * (dxh - jnp.mean(dxh, axis=1, keepdims=True) - xh * jnp.mean(dxh * xh, axis=1, keepdims=True))
    return dx, jnp.sum(dy * xh, axis=0, keepdims=True), jnp.sum(dy, axis=0, keepdims=True)


def _gmean(v, gm):
    hi = v.astype(BF16)
    lo = (v - hi.astype(F32)).astype(BF16)
    return jnp.dot(hi, gm, preferred_element_type=F32) + jnp.dot(lo, gm, preferred_element_type=F32)


def _roll_l(x, sh):
    return pltpu.roll(x, sh % x.shape[1], 1)


def _rope(x, c, sa, sb, sh):
    return x * c + _roll_l(x, -sh) * sa + _roll_l(x, sh) * sb


def _rope_t(dy, c, sa, sb, sh):
    return dy * c + _roll_l(dy * sa, sh) + _roll_l(dy * sb, -sh)


def _roll_r(x, s):
    return pltpu.roll(x, (-s) % x.shape[0], 0)


_GELU_C = math.sqrt(2.0 / math.pi)


def _gelu(x):
    t = jnp.tanh(_GELU_C * (x + 0.044715 * x * x * x))
    return 0.5 * x * (1.0 + t), t


def _gelu_grad(x, t):
    return 0.5 * (1.0 + t) + 0.5 * x * (1.0 - t * t) * _GELU_C * (1.0 + 3.0 * 0.044715 * x * x)


def _dot_nt(a, b):
    return lax.dot_general(a, b, (((1,), (1,)), ((), ())), preferred_element_type=F32)


def _dot_tn(a, b):
    return lax.dot_general(a, b, (((0,), (0,)), ((), ())), preferred_element_type=F32)


def _rope_block(s, d):
    t = jnp.arange(s)
    row = (t // GRID_W).astype(F32)
    col = (t % GRID_W).astype(F32)
    half = d // 4
    inv = ROPE_THETA ** (-jnp.arange(half, dtype=F32) / half)
    z = jnp.zeros((s, half), F32)
    cs, sas, sbs = [], [], []
    for pos in (row, col):
        ang = pos[:, None] * inv[None, :]
        co, si = jnp.cos(ang), jnp.sin(ang)
        cs += [co, co]
        sas += [-si, z]
        sbs += [z, si]
    return tuple(jnp.concatenate(v, axis=1) for v in (cs, sas, sbs))


def _rope_tables(s):
    gqa = tuple(jnp.tile(a, (1, 4)) for a in _rope_block(s, HEAD_DIM))
    c32, sa32, sb32 = _rope_block(s, MLA_ROPE)

    def head128(c, fill):
        return jnp.concatenate([jnp.full((s, 64), fill, F32), c, jnp.zeros((s, 32), F32)], axis=1)

    mla = tuple(jnp.tile(a, (1, 4)) for a in (head128(c32, 1.0), head128(sa32, 0.0), head128(sb32, 0.0)))
    kr = (head128(c32, 0.0), head128(sa32, 0.0), head128(sb32, 0.0))
    return dict(gqa=gqa, mla=mla, kr=kr)


def _group_mean_matrix(w, g):
    idx = np.arange(w) // g
    return jnp.asarray((idx[:, None] == idx[None, :]).astype(np.float32) / g, dtype=BF16)


TM = 512


def ln_fwd(name, s, x, g, b, res=None, alpha=1.0):
    ins = [('row', x, D_MODEL, 0)] + ([('row', res, D_MODEL, 0)] if res is not None else []) + [('full', g), ('full', b)]

    def fn(i, j, *a):
        if res is not None:
            xv, rv, g_ref, b_ref, o_ref = a
            z = alpha * rv + xv
        else:
            xv, g_ref, b_ref, o_ref = a
            z = xv
        xh, _ = _ln_stats(z)
        o_ref[...] = xh * g_ref[...] + b_ref[...]

    return tiled_call(name, fn, s, TM, ins, [('row', D_MODEL, D_MODEL, 0, F32)])[0]


def ln_bwd(name, s, dy, x, g, res=None, alpha=1.0):
    ins = [('row', dy, D_MODEL, 0), ('row', x, D_MODEL, 0)]
    ins += ([('row', res, D_MODEL, 0)] if res is not None else []) + [('full', g)]

    def fn(i, j, *a):
        a = list(a)
        dyv = a.pop(0)
        xv = a.pop(0)
        z = alpha * a.pop(0) + xv if res is not None else xv
        g_ref, dz_ref, dg_ref, db_ref = a
        xh, r = _ln_stats(z)
        dz, dg, db = _ln_bwd(dyv, xh, r, g_ref[...])
        dz_ref[...] = dz
        dg_ref[...] += dg
        db_ref[...] += db

    return tiled_call(name, fn, s, TM, ins, [('row', D_MODEL, D_MODEL, 0, F32), ('acc', (1, D_MODEL)), ('acc', (1, D_MODEL))])


def loss_and_ln_bwd(name, s, x, res, g, b, tgt, alpha):
    ins = [('row', x, D_MODEL, 0), ('row', res, D_MODEL, 0), ('row', tgt, D_MODEL, 0), ('full', g), ('full', b)]

    def fn(i, j, xv, rv, tv, g_ref, b_ref, dz_ref, dg_ref, db_ref, loss_ref):
        xh, r = _ln_stats(alpha * rv + xv)
        y = xh * g_ref[...] + b_ref[...]
        e = y - tv
        dz, dg, db = _ln_bwd(e * (1.0 / D_MODEL), xh, r, g_ref[...])
        dz_ref[...] = dz
        dg_ref[...] += dg
        db_ref[...] += db
        loss_ref[...] += jnp.sum(jnp.sum(e * e, axis=0, keepdims=True), axis=1, keepdims=True) * (0.5 / D_MODEL)

    return tiled_call(name, fn, s, TM, ins, [('row', D_MODEL, D_MODEL, 0, F32), ('acc', (1, D_MODEL)), ('acc', (1, D_MODEL)),
                                             ('acc', (1, 128))])


HALO_A = 16


def _glu(a_in):
    return a_in[:, :GROUP_W] * _sigmoid(a_in[:, GROUP_W:])


def _row_windows(x, halo, tm):
    rolled = {0: x}

    def window(s):
        b = s % 8
        if b not in rolled:
            rolled[b] = _roll_r(x, b)
        off = halo + s - b
        return rolled[b][off:off + tm]

    return window


def a_fwd(name, s, proj, cw, cb, g, b):
    def fn(i, j, a_ext, cw_ref, cb_ref, g_ref, b_ref, y_ref, o_ref):
        a_at = _row_windows(_glu(a_ext), HALO_A, TM)
        acc = jnp.zeros((TM, GROUP_W), F32)
        for k in range(CONV_A_WIDTH):
            acc = acc + a_at(k - 15) * cw_ref[pl.ds(k, 1), :]
        y = acc + cb_ref[...]
        y_ref[...] = y
        xh, _ = _ln_stats(y)
        z = xh * g_ref[...] + b_ref[...]
        o_ref[...] = z * _sigmoid(z)

    ins = [('halo', proj, 2 * GROUP_W, 0, HALO_A), ('full', cw), ('full', cb), ('full', g), ('full', b)]
    return tiled_call(name, fn, s, TM, ins, [('row', GROUP_W, GROUP_W, 0, F32), ('row', GROUP_W, GROUP_W, 0, F32)])


def a_bwd1(name, s, y, dmix, g, b):
    def fn(i, j, yv, do, g_ref, b_ref, dy_ref, dg_ref, db_ref, dcb_ref):
        xh, r = _ln_stats(yv)
        z = xh * g_ref[...] + b_ref[...]
        sg = _sigmoid(z)
        dz = do * sg * (1.0 + z * (1.0 - sg))
        dy, dg, db = _ln_bwd(dz, xh, r, g_ref[...])
        dy_ref[...] = dy
        dg_ref[...] += dg
        db_ref[...] += db
        dcb_ref[...] += jnp.sum(dy, axis=0, keepdims=True)

    ins = [('row', y, GROUP_W, 0), ('row', dmix, GROUP_W, 0), ('full', g), ('full', b)]
    return tiled_call(name, fn, s, TM, ins, [('row', GROUP_W, GROUP_W, 0, F32), ('acc', (1, GROUP_W)), ('acc', (1, GROUP_W)),
                                             ('acc', (1, GROUP_W))])


def a_bwd2(name, s, proj, dy, cw):
    def fn(i, j, a_ext, dy_ext, cw_ref, da_ref, dcw_ref):
        a1, a2 = a_ext[:, :GROUP_W], a_ext[:, GROUP_W:]
        sg = _sigmoid(a2)
        a = a1 * sg
        dyc = dy_ext[HALO_A:HALO_A + TM]
        dy_at, a_at = _row_windows(dy_ext, HALO_A, TM), _row_windows(a, HALO_A, TM)
        da = jnp.zeros((TM, GROUP_W), F32)
        for k in range(CONV_A_WIDTH):
            da = da + dy_at(15 - k) * cw_ref[pl.ds(k, 1), :]
            dcw_ref[pl.ds(k, 1), :] += jnp.sum(dyc * a_at(k - 15), axis=0, keepdims=True)
        a1c, sgc = a1[HALO_A:HALO_A + TM], sg[HALO_A:HALO_A + TM]
        da_ref[...] = jnp.concatenate([da * sgc, da * a1c * sgc * (1.0 - sgc)], axis=1).astype(BF16)

    ins = [('halo', proj, 2 * GROUP_W, 0, HALO_A), ('halo', dy, GROUP_W, 0, HALO_A), ('full', cw)]
    return tiled_call(name, fn, s, TM, ins, [('row', PROJ_W, 2 * GROUP_W, 0, BF16), ('acc', (32, GROUP_W))])


def b_prep_fwd(name, s, proj, gq, gk, tabs, gm):
    c, sa, sb = tabs

    def fn(i, j, qkv, cv, sav, sbv, gq_ref, gk_ref, gm_ref, q_ref, k_ref, v_ref):
        q, k, v = qkv[:, :256], qkv[:, 256:384], qkv[:, 384:]
        gmv = gm_ref[...]
        qn = q * lax.rsqrt(_gmean(q * q, gmv) + RMS_EPS) * gq_ref[...]
        kn = k * lax.rsqrt(_gmean(k * k, gmv[:128, :128]) + RMS_EPS) * gk_ref[...]
        qr = _rope(qn, cv, sav, sbv, 16)
        kr = _rope(kn, cv[:, :128], sav[:, :128], sbv[:, :128], 16)
        for h in range(4):
            q_ref[h] = (qr[:, h * 64:(h + 1) * 64] * QSCALE_GQA).astype(BF16)
        for h in range(2):
            k_ref[h] = kr[:, h * 64:(h + 1) * 64].astype(BF16)
            v_ref[h] = v[:, h * 64:(h + 1) * 64].astype(BF16)

    ins = [('row', proj, 512, 1), ('row', c, 256, 0), ('row', sa, 256, 0), ('row', sb, 256, 0), ('full', gq), ('full', gk),
           ('full', gm)]
    return tiled_call(name, fn, s, TM, ins, [('hm', 4, 64, BF16), ('hm', 2, 64, BF16), ('hm', 2, 64, BF16)])


def b_prep_bwd(name, s, proj, gq, gk, tabs, gm, dq_hm, dk_hm, dv_hm, dproj):
    c, sa, sb = tabs

    def fn(i, j, qkv, cv, sav, sbv, gq_ref, gk_ref, gm_ref, dq_ref, dk_ref, dv_ref, dp_ref, dgq_ref, dgk_ref):
        q, k = qkv[:, :256], qkv[:, 256:384]
        gmv = gm_ref[...]
        dqr = jnp.concatenate([dq_ref[h] for h in range(4)], axis=1) * QSCALE_GQA
        dkr = jnp.concatenate([dk_ref[h] for h in range(2)], axis=1)
        dv = jnp.concatenate([dv_ref[h] for h in range(2)], axis=1)
        outs = []
        for x, dxr, g_ref, gmx, w, dg_ref in ((q, dqr, gq_ref, gmv, 256, dgq_ref), (k, dkr, gk_ref, gmv[:128, :128], 128, dgk_ref)):
            dn = _rope_t(dxr, cv[:, :w], sav[:, :w], sbv[:, :w], 16)
            r = lax.rsqrt(_gmean(x * x, gmx) + RMS_EPS)
            gv = g_ref[...]
            dx = gv * r * dn - x * (r * r * r) * _gmean(x * gv * dn, gmx)
            dgt = jnp.sum(x * r * dn, axis=0, keepdims=True)
            dg = dgt[:, 0:64]
            for h in range(1, w // 64):
                dg = dg + dgt[:, h * 64:(h + 1) * 64]
            dg_ref[...] += dg
            outs.append(dx)
        dp_ref[...] = jnp.concatenate(outs + [dv], axis=1).astype(BF16)

    ins = [('row', proj, 512, 1), ('row', c, 256, 0), ('row', sa, 256, 0), ('row', sb, 256, 0), ('full', gq), ('full', gk),
           ('full', gm), ('hm', dq_hm), ('hm', dk_hm), ('hm', dv_hm)]
    return tiled_call(name, fn, s, TM, ins, [('row_into', dproj, 512, 1), ('acc', (1, 64)), ('acc', (1, 64))])


TQ = 512
TQ_FWD = 256


LOG2E = 1.4426950408889634
QSCALE_GQA = HEAD_DIM ** -0.5 * LOG2E
QSCALE_MLA = (MLA_NOPE + MLA_ROPE) ** -0.5 * LOG2E


def attn_fwd(name, s, q_hm, k_hm, v_hm, comm=None):
    dk = q_hm.shape[2]
    nkv = k_hm.shape[0]
    tq = TQ_FWD if s % TQ_FWD == 0 else TQ

    def body(q_ref, k_ref, v_ref, o_ref, l_ref):
        outs = []
        for g in range(4):
            kv = g * nkv // 4
            sc = _dot_nt(q_ref[g], k_ref[kv])
            m = jnp.max(sc, axis=1, keepdims=True)
            p = jnp.exp2(sc - m)
            l = jnp.sum(p, axis=1, keepdims=True)
            o = jnp.dot(p.astype(BF16), v_ref[kv], preferred_element_type=F32)
            outs.append(o / l)
            l_ref[g] = m + jnp.log2(l)
        o_ref[...] = jnp.concatenate(outs, axis=1)

    res, cres = carried_call(
        name, body, (s // tq,),
        [pl.BlockSpec((4, tq, dk), lambda i: (0, i, 0)),
         pl.BlockSpec((nkv, s, dk), lambda i: (0, 0, 0)),
         pl.BlockSpec((nkv, s, 64), lambda i: (0, 0, 0))],
        [pl.BlockSpec((tq, 256), lambda i: (i, 0)), pl.BlockSpec((4, tq, 1), lambda i: (0, i, 0))],
        [jax.ShapeDtypeStruct((s, 256), F32), jax.ShapeDtypeStruct((4, s, 1), F32)], [],
        ("arbitrary",), [q_hm, k_hm, v_hm], comm)
    return res if comm is None else (res, cres)


def attn_bwd(name, s, q_hm, k_hm, v_hm, o, lse, dmixcat, dcol, comm=None):
    dk = q_hm.shape[2]
    nkv = k_hm.shape[0]
    shared = nkv == 2
    kvi = (lambda p, g, i: (p, 0, 0)) if shared else (lambda p, g, i: (2 * p + g, 0, 0))

    def body(q_ref, k_ref, v_ref, o_ref, l_ref, do_ref, dq_ref, dk_ref, dv_ref):
        g = pl.program_id(1)
        i = pl.program_id(2)
        first = (i == 0) & (g == 0) if shared else (i == 0)

        @pl.when(first)
        def _():
            dk_ref[...] = jnp.zeros_like(dk_ref)
            dv_ref[...] = jnp.zeros_like(dv_ref)

        dob, ob = do_ref[...], o_ref[...]
        do = jnp.where(g == 0, dob[:, :64], dob[:, 64:])
        ov = jnp.where(g == 0, ob[:, :64], ob[:, 64:])
        q, k, v = q_ref[0], k_ref[0], v_ref[0]
        p = jnp.exp2(_dot_nt(q, k) - l_ref[0])
        dp = _dot_nt(do.astype(BF16), v)
        delta = jnp.sum(do * ov, axis=1, keepdims=True)
        ds = (p * (dp - delta) * (1.0 / LOG2E)).astype(BF16)
        dq_ref[0] = jnp.dot(ds, k, preferred_element_type=F32)
        dk_ref[0] += _dot_tn(ds, q)
        dv_ref[0] += _dot_tn(p.astype(BF16), do.astype(BF16))

    res, cres = carried_call(
        name, body, (2, 2, s // TQ),
        [pl.BlockSpec((1, TQ, dk), lambda p, g, i: (2 * p + g, i, 0)),
         pl.BlockSpec((1, s, dk), kvi), pl.BlockSpec((1, s, 64), kvi),
         pl.BlockSpec((TQ, 128), lambda p, g, i: (i, p)),
         pl.BlockSpec((1, TQ, 1), lambda p, g, i: (2 * p + g, i, 0)),
         pl.BlockSpec((TQ, 128), lambda p, g, i: (i, dcol + p))],
        [pl.BlockSpec((1, TQ, dk), lambda p, g, i: (2 * p + g, i, 0)),
         pl.BlockSpec((1, s, dk), kvi), pl.BlockSpec((1, s, 64), kvi)],
        [jax.ShapeDtypeStruct((4, s, dk), F32), jax.ShapeDtypeStruct((nkv, s, dk), F32), jax.ShapeDtypeStruct((nkv, s, 64), F32)],
        [], ("arbitrary", "arbitrary", "arbitrary"), [q_hm, k_hm, v_hm, o, lse, dmixcat], comm)
    return res if comm is None else (res, cres)


def _lane_group_masks(w, g):
    lane = lax.broadcasted_iota(jnp.int32, (1, w), 1)
    return [((lane >= k * g) & (lane < (k + 1) * g)).astype(F32) for k in range(w // g)]


def _sgu_gate(svn, w_ref, bias):
    masks = _lane_group_masks(GROUP_W, 64)
    outs = []
    for n in range(TM // CHUNK):
        x = svn[n * CHUNK:(n + 1) * CHUNK]
        acc = bias
        for g in range(4):
            acc = acc + jnp.dot(w_ref[g].astype(BF16), (x * masks[g]).astype(BF16), preferred_element_type=F32)
        outs.append(acc)
    return jnp.concatenate(outs, axis=0)


def c_fwd(name, s, proj, g, b, w, bias):
    def fn(i, j, cin, g_ref, b_ref, w_ref, bias_ref, o_ref):
        c, _ = _gelu(cin)
        xh, _ = _ln_stats(c[:, GROUP_W:])
        svn = xh * g_ref[...] + b_ref[...]
        o_ref[...] = c[:, :GROUP_W] * _sgu_gate(svn, w_ref, bias_ref[...])

    ins = [('row', proj, 512, 2), ('full', g), ('full', b), ('full', w), ('full', bias)]
    return tiled_call(name, fn, s, TM, ins, [('row', GROUP_W, GROUP_W, 0, F32)])[0]


def c_bwd(name, s, proj, g, b, w, wt, bias, dmixcat, dproj):
    def fn(i, j, cin, do, g_ref, b_ref, w_ref, wt_ref, bias_ref, dc_ref, dg_ref, db_ref, dw_ref, dbias_ref):
        c, t = _gelu(cin)
        u = c[:, :GROUP_W]
        xh, r = _ln_stats(c[:, GROUP_W:])
        svn = xh * g_ref[...] + b_ref[...]
        gate = _sgu_gate(svn, w_ref, bias_ref[...])
        du = do * gate
        dgate = do * u
        masks = _lane_group_masks(GROUP_W, 64)
        col = lax.broadcasted_iota(jnp.int32, (CHUNK, 128), 1)
        dsvn, dbias = [], jnp.zeros((CHUNK, 128), F32)
        for n in range(TM // CHUNK):
            dgc = dgate[n * CHUNK:(n + 1) * CHUNK]
            x = svn[n * CHUNK:(n + 1) * CHUNK]
            acc = jnp.zeros((CHUNK, GROUP_W), F32)
            for gi in range(4):
                dgm = (dgc * masks[gi]).astype(BF16)
                acc = acc + jnp.dot(wt_ref[gi].astype(BF16), dgm, preferred_element_type=F32)
                dw_ref[gi] += _dot_nt(dgm, (x * masks[gi]).astype(BF16))
                dbias = dbias + jnp.where(col == gi, jnp.sum(dgc * masks[gi], axis=1, keepdims=True), 0.0)
            dsvn.append(acc)
        dbias_ref[...] += dbias
        dsv, dg, db = _ln_bwd(jnp.concatenate(dsvn, axis=0), xh, r, g_ref[...])
        dg_ref[...] += dg
        db_ref[...] += db
        dc_ref[...] = (jnp.concatenate([du, dsv], axis=1) * _gelu_grad(cin, t)).astype(BF16)

    ins = [('row', proj, 512, 2), ('row', dmixcat, GROUP_W, 2), ('full', g), ('full', b), ('full', w), ('full', wt), ('full', bias)]
    return tiled_call(name, fn, s, TM, ins, [('row_into', dproj, 512, 2), ('acc', (1, GROUP_W)), ('acc', (1, GROUP_W)),
                                             ('acc', (4, CHUNK, CHUNK)), ('acc', (CHUNK, 128))])


def _d_common(dpart, gq_ref, gkv_ref):
    cq, ckv, kr = dpart[:, :256], dpart[:, 256:384], dpart[:, 384:]
    rq = lax.rsqrt(jnp.sum(cq * cq, axis=1, keepdims=True) * (1.0 / MLA_Q_LORA) + RMS_EPS)
    rkv = lax.rsqrt(jnp.mean(ckv * ckv, axis=1, keepdims=True) + RMS_EPS)
    return cq, ckv, kr, rq, rkv, cq * rq * gq_ref[...], ckv * rkv * gkv_ref[...]


def d_prep_fwd(name, s, proj, gq, gkv, wuq, wukv, tm_, tkr):
    cm, sam, sbm = tm_
    ck, sak, sbk = tkr

    def fn(i, j, dpart, cmv, samv, sbmv, ckv_, sakv, sbkv, gq_ref, gkv_ref, wuq_ref, wukv_ref, q_ref, k_ref, v_ref):
        cq, ckv, kr, rq, rkv, cqn, ckvn = _d_common(dpart, gq_ref, gkv_ref)
        qd = jnp.dot(cqn.astype(BF16), wuq_ref[...], preferred_element_type=F32)
        qf = _rope(qd, cmv, samv, sbmv, 8)
        kvd = jnp.dot(ckvn.astype(BF16), wukv_ref[...], preferred_element_type=F32)
        krr = _rope(kr, ckv_, sakv, sbkv, 8)
        for h in range(4):
            q_ref[h] = (qf[:, h * 128:(h + 1) * 128] * QSCALE_MLA).astype(BF16)
            k_ref[h] = (kvd[:, h * 128:(h + 1) * 128] + krr).astype(BF16)
            v_ref[h] = kvd[:, 512 + h * 64:512 + (h + 1) * 64].astype(BF16)

    ins = [('row', proj, 512, 3), ('row', cm, 512, 0), ('row', sam, 512, 0), ('row', sbm, 512, 0), ('row', ck, 128, 0),
           ('row', sak, 128, 0), ('row', sbk, 128, 0), ('full', gq), ('full', gkv), ('full', wuq), ('full', wukv)]
    return tiled_call(name, fn, s, TM, ins, [('hm', 4, 128, BF16), ('hm', 4, 128, BF16), ('hm', 4, 64, BF16)])


def d_prep_bwd(name, s, proj, gq, gkv, wuq, wukv, tm_, tkr, dq_hm, dk_hm, dv_hm, dproj):
    cm, sam, sbm = tm_
    ck, sak, sbk = tkr

    def fn(i, j, dpart, cmv, samv, sbmv, ckv_, sakv, sbkv, gq_ref, gkv_ref, wuq_ref, wukv_ref, dq_ref, dk_ref, dv_ref,
           dd_ref, dgq_ref, dgkv_ref, dwuq_ref, dwukv_ref):
        cq, ckv, kr, rq, rkv, cqn, ckvn = _d_common(dpart, gq_ref, gkv_ref)
        dqf = jnp.concatenate([dq_ref[h] for h in range(4)], axis=1) * QSCALE_MLA
        dkf = [dk_ref[h] for h in range(4)]
        dqd = _rope_t(dqf, cmv, samv, sbmv, 8).astype(BF16)
        dkvd = jnp.concatenate(dkf + [dv_ref[h] for h in range(4)], axis=1).astype(BF16)
        dkr = _rope_t(dkf[0] + dkf[1] + dkf[2] + dkf[3], ckv_, sakv, sbkv, 8)
        dcqn = _dot_nt(dqd, wuq_ref[...])
        dckvn = _dot_nt(dkvd, wukv_ref[...])
        dwuq_ref[...] += _dot_tn(cqn.astype(BF16), dqd)
        dwukv_ref[...] += _dot_tn(ckvn.astype(BF16), dkvd)
        gqv, gkvv = gq_ref[...], gkv_ref[...]
        dcq = gqv * rq * dcqn - cq * (rq * rq * rq) * (jnp.sum(cq * gqv * dcqn, axis=1, keepdims=True) * (1.0 / MLA_Q_LORA))
        dckv = gkvv * rkv * dckvn - ckv * (rkv * rkv * rkv) * jnp.mean(ckv * gkvv * dckvn, axis=1, keepdims=True)
        dgq_ref[...] += jnp.sum(cq * rq * dcqn, axis=0, keepdims=True)
        dgkv_ref[...] += jnp.sum(ckv * rkv * dckvn, axis=0, keepdims=True)
        dd_ref[...] = jnp.concatenate([dcq, dckv, dkr], axis=1).astype(BF16)

    ins = [('row', proj, 512, 3), ('row', cm, 512, 0), ('row', sam, 512, 0), ('row', sbm, 512, 0), ('row', ck, 128, 0),
           ('row', sak, 128, 0), ('row', sbk, 128, 0), ('full', gq), ('full', gkv), ('full', wuq), ('full', wukv),
           ('hm', dq_hm), ('hm', dk_hm), ('hm', dv_hm)]
    return tiled_call(name, fn, s, TM, ins, [('row_into', dproj, 512, 3), ('acc', (1, 256)), ('acc', (1, 128)),
                                             ('acc', (256, 512)), ('acc', (128, 768))])


HALO_F = 8
TN_F = 1408
TM_F = 512
TM_FB, TN_FB = 512, 256


def _conv3(ext, w_ref):
    return _roll_r(ext, -1) * w_ref[pl.ds(0, 1), :] + ext * w_ref[pl.ds(1, 1), :] + _roll_r(ext, 1) * w_ref[pl.ds(2, 1), :]


def f_fwd(name, s, upraw, cw, cb):
    ncj = D_FF // TN_F

    def fn(i, j, xa, xg, wa_ref, wg_ref, ba_ref, bg_ref, o_ref):
        ua = (_conv3(xa, wa_ref) + ba_ref[...])[HALO_F:HALO_F + TM_F]
        ug = (_conv3(xg, wg_ref) + bg_ref[...])[HALO_F:HALO_F + TM_F]
        o_ref[...] = (ua * _sigmoid(ua) * ug).astype(BF16)

    ins = [('halo', upraw, TN_F, lambda j: j, HALO_F), ('halo', upraw, TN_F, lambda j: j + ncj, HALO_F),
           ('col', cw, TN_F, lambda j: j), ('col', cw, TN_F, lambda j: j + ncj),
           ('col', cb, TN_F, lambda j: j), ('col', cb, TN_F, lambda j: j + ncj)]
    return tiled_call(name, fn, s, TM_F, ins, [('row', D_FF, TN_F, lambda j: j, BF16)], ncol=ncj)[0]


def f_bwd(name, s, upraw, dact, cw, cb):
    ncj = D_FF // TN_FB

    def fn(i, j, xa, xg, da, wa_ref, wg_ref, ba_ref, bg_ref, dx_ref, dwa_ref, dwg_ref, dba_ref, dbg_ref):
        taps_a = (_roll_r(xa, -1), xa, _roll_r(xa, 1))
        taps_g = (_roll_r(xg, -1), xg, _roll_r(xg, 1))
        ua = sum(t * wa_ref[pl.ds(k, 1), :] for k, t in enumerate(taps_a)) + ba_ref[...]
        ug = sum(t * wg_ref[pl.ds(k, 1), :] for k, t in enumerate(taps_g)) + bg_ref[...]
        sg = _sigmoid(ua)
        dug = da * ua * sg
        dua = da * ug * sg * (1.0 + ua * (1.0 - sg))
        cen = slice(HALO_F, HALO_F + TM_FB)
        for half, (du, taps, w_ref, dw_ref, db_ref) in enumerate(((dua, taps_a, wa_ref, dwa_ref, dba_ref),
                                                                   (dug, taps_g, wg_ref, dwg_ref, dbg_ref))):
            dx = _roll_r(du, 1) * w_ref[pl.ds(0, 1), :] + du * w_ref[pl.ds(1, 1), :] + _roll_r(du, -1) * w_ref[pl.ds(2, 1), :]
            dx_ref[half] = dx[cen].astype(BF16)
            duc = du[cen]
            for k in range(3):
                dw_ref[pl.ds(k, 1), :] += jnp.sum(duc * taps[k][cen], axis=0, keepdims=True)
            db_ref[...] += jnp.sum(duc, axis=0, keepdims=True)

    tn = TN_FB
    ins = [('halo', upraw, tn, lambda j: j, HALO_F), ('halo', upraw, tn, lambda j: j + ncj, HALO_F),
           ('halo', dact, tn, lambda j: j, HALO_F),
           ('col', cw, tn, lambda j: j), ('col', cw, tn, lambda j: j + ncj),
           ('col', cb, tn, lambda j: j), ('col', cb, tn, lambda j: j + ncj)]
    outs = [('row3', 2, D_FF, tn, lambda j: j, BF16),
            ('colacc', 8, D_FF, tn, lambda j: j), ('colacc', 8, D_FF, tn, lambda j: j),
            ('colacc', 1, D_FF, tn, lambda j: j), ('colacc', 1, D_FF, tn, lambda j: j)]
    return tiled_call(name, fn, s, TM_FB, ins, outs, ncol=ncj)


def _pad_w_in(w):
    z = lambda n: jnp.zeros((w.shape[0], n), w.dtype)
    return jnp.concatenate([w[:, :1728], z(64), w[:, 1728:1856], z(64), w[:, 1856:1888], z(32)], axis=1)


def _unpad_w_in(g):
    return jnp.concatenate([g[:, :1728], g[:, 1792:1920], g[:, 1984:2016]], axis=1)


def _pad_uq(w):
    w = w.reshape(MLA_Q_LORA, 4, 96)
    w = jnp.concatenate([w, jnp.zeros((MLA_Q_LORA, 4, 32), w.dtype)], axis=2).reshape(MLA_Q_LORA, 512)
    return jnp.concatenate([w, jnp.zeros((64, 512), w.dtype)], axis=0)


def _unpad_uq(g):
    return g[:MLA_Q_LORA].reshape(MLA_Q_LORA, 4, 128)[:, :, :96].reshape(MLA_Q_LORA, 384)


def _pad_ukv(w):
    w = w.reshape(MLA_KV_LORA, 4, 128)
    kn = jnp.concatenate([w[:, :, :64], jnp.zeros((MLA_KV_LORA, 4, 64), w.dtype)], axis=2).reshape(MLA_KV_LORA, 512)
    return jnp.concatenate([kn, w[:, :, 64:].reshape(MLA_KV_LORA, 256)], axis=1)


def _unpad_ukv(g):
    kn = g[:, :512].reshape(MLA_KV_LORA, 4, 128)[:, :, :64]
    v = g[:, 512:].reshape(MLA_KV_LORA, 4, 64)
    return jnp.concatenate([kn, v], axis=2).reshape(MLA_KV_LORA, 512)


def _row(v, pad_to=None):
    v = v.reshape(1, -1)
    if pad_to is not None and v.shape[1] < pad_to:
        v = jnp.concatenate([v, jnp.zeros((1, pad_to - v.shape[1]), v.dtype)], axis=1)
    return v


LATE_NAMES = ['w_out', 'ffn_w_up', 'ffn_w_down']


def _assemble(n, part, g, me):
    return jnp.concatenate([jnp.where(me == q, part, g[q]) for q in range(N_CHIPS)], axis=SHARDED_AXIS[n])


def local_step(x, tgt, W, dist=None):
    s = x.shape[0]
    tabs = _rope_tables(s)
    gm = _group_mean_matrix(256, 64)
    alpha = DEEPNORM_ALPHA
    acts = []
    late = {(n, l): W[n][l] for n in LATE_NAMES for l in range(DEPTH)} if dist is None else {}
    h = ln_fwd("ln_in", s, x, _row(W['ln_in_g']), _row(W['ln_in_b']))
    for l in range(DEPTH):
        A = dict(h=h)
        A['wpad'] = _pad_w_in(W['w_in'][l]).astype(BF16)
        A['wuq'] = _pad_uq(W['mla_w_uq'][l]).astype(BF16)
        A['wukv'] = _pad_ukv(W['mla_w_ukv'][l]).astype(BF16)
        A['cw'] = jnp.concatenate([W['conv_a_w'][l], jnp.zeros((1, GROUP_W), F32)], axis=0)
        A['fcw'] = jnp.concatenate([W['ffn_conv_w'][l], jnp.zeros((5, 2 * D_FF), F32)], axis=0)
        A['gq'] = jnp.tile(_row(W['qk_norm_q'][l]), (1, 4))
        A['gk'] = jnp.tile(_row(W['qk_norm_k'][l]), (1, 2))
        A['sgu_bias'] = jnp.repeat(W['sgu_b'][l].T, 64, axis=1)
        A['sgu_wt'] = jnp.swapaxes(W['sgu_w'][l], 1, 2)
        proj = mm(f"proj{l}", h, A['wpad'])
        A['proj'] = proj
        A['y_a'], o_a = a_fwd(f"a_fwd{l}", s, proj, A['cw'], _row(W['conv_a_b'][l]), _row(W['ln_a_g'][l]), _row(W['ln_a_b'][l]))
        A['bq'], A['bk'], A['bv'] = b_prep_fwd(f"b_prep{l}", s, proj, A['gq'], A['gk'], tabs['gqa'], gm)
        if dist is not None and l == 0:
            lp = dist['late']
            parts = [lp['w_out'], lp['ffn_w_up'][0:1], lp['ffn_w_down'][0:1]]
            nxt_names, nxt = dist['next']
            nxt_parts = [a[1:2] for a in nxt]
            wanted = [lp['w_out'], (lp['ffn_w_up'], 0), (lp['ffn_w_down'], 0)] + [(a, 1) for a in nxt]
            (A['o_b'], A['lse_b']), got = attn_fwd(f"b_attn{l}", s, A['bq'], A['bk'], A['bv'], comm=gather_over_chips(wanted))
            full = [_assemble(n, p, g, dist['me']) for n, p, g in zip(LATE_NAMES, parts, got)]
            late[('w_out', 0)], late[('w_out', 1)] = full[0][0], full[0][1]
            late[('ffn_w_up', 0)], late[('ffn_w_down', 0)] = full[1][0], full[2][0]
            for n, p, g in zip(nxt_names, nxt_parts, got[len(parts):]):
                W[n][1] = _assemble(n, p, g, dist['me'])[0]
        else:
            A['o_b'], A['lse_b'] = attn_fwd(f"b_attn{l}", s, A['bq'], A['bk'], A['bv'])
        o_c = c_fwd(f"c_fwd{l}", s, proj, _row(W['sgu_ln_g'][l]), _row(W['sgu_ln_b'][l]), W['sgu_w'][l], A['sgu_bias'])
        A['dq'], A['dk'], A['dv'] = d_prep_fwd(f"d_prep{l}", s, proj, _row(W['mla_q_norm'][l], 256), _row(W['mla_kv_norm'][l]),
                                               A['wuq'], A['wukv'], tabs['mla'], tabs['kr'])
        if dist is not None and l == 0:
            parts = [lp['ffn_w_up'][1:2], lp['ffn_w_down'][1:2]]
            (A['o_d'], A['lse_d']), got = attn_fwd(f"d_attn{l}", s, A['dq'], A['dk'], A['dv'],
                                                   comm=gather_over_chips([(lp['ffn_w_up'], 1), (lp['ffn_w_down'], 1)]))
            late[('ffn_w_up', 1)] = _assemble('ffn_w_up', parts[0], got[0], dist['me'])[0]
            late[('ffn_w_down', 1)] = _assemble('ffn_w_down', parts[1], got[1], dist['me'])[0]
        else:
            A['o_d'], A['lse_d'] = attn_fwd(f"d_attn{l}", s, A['dq'], A['dk'], A['dv'])
        A['wout'] = late[('w_out', l)].astype(BF16)
        A['wup'] = late[('ffn_w_up', l)].astype(BF16)
        A['wdown'] = late[('ffn_w_down', l)].astype(BF16)
        A['mixcat'] = jnp.concatenate([o_a, A['o_b'], o_c, A['o_d']], axis=1).astype(BF16)
        A['mix'] = mm(f"out_proj{l}", A['mixcat'], A['wout'])
        h1 = ln_fwd(f"ln_mix{l}", s, A['mix'], _row(W['ln_mix_g'][l]), _row(W['ln_mix_b'][l]), res=h, alpha=alpha)
        A['h1'] = h1
        A['upraw'] = mm(f"ffn_up{l}", h1, A['wup'])
        A['act'] = f_fwd(f"f_fwd{l}", s, A['upraw'], A['fcw'], _row(W['ffn_conv_b'][l]))
        A['f'] = mm(f"ffn_down{l}", A['act'], A['wdown'])
        if l < DEPTH - 1:
            h = ln_fwd(f"ln_ffn{l}", s, A['f'], _row(W['ln_ffn_g'][l]), _row(W['ln_ffn_b'][l]), res=h1, alpha=alpha)
        acts.append(A)

    per_layer = {n: [None] * DEPTH for n in WEIGHT_NAMES if n not in ('ln_in_g', 'ln_in_b')}
    dh = None
    loss_row = None
    overlap = dist is not None and DEPTH == 2
    red = {}
    grp1 = [(n, 1) for n in BIG_NAMES]
    grp2 = [('ffn_w_up', 0), ('ffn_w_down', 0)]

    def carried(fn, name, *a, comm=None, **kw):
        if comm is None:
            return fn(name, *a, **kw), None
        return fn(name, *a, comm=comm, **kw)

    def pair_sums(tag, xs, r1):
        return [pair_sum(f"pair_sum_{tag}{t}", xv, rv, dist['idx'], BF16) for t, (xv, rv) in enumerate(zip(xs, r1))]

    def chip_sums(tag, cs, ps):
        return [chip_sum(f"chip_sum_{tag}{t}", [cv], [pv], dist['idx']) for t, (cv, pv) in enumerate(zip(cs, ps))]

    for l in reversed(range(DEPTH)):
        A = acts[l]
        carry = overlap and l == 0
        g_ffn, b_ffn = _row(W['ln_ffn_g'][l]), _row(W['ln_ffn_b'][l])
        if l == DEPTH - 1:
            dz, dg, db, loss_row = loss_and_ln_bwd("loss", s, A['f'], A['h1'], g_ffn, b_ffn, tgt, alpha)
        else:
            dz, dg, db = ln_bwd(f"ln_ffn_bwd{l}", s, dh, A['f'], g_ffn, res=A['h1'], alpha=alpha)
        per_layer['ln_ffn_g'][l], per_layer['ln_ffn_b'][l] = dg[0], db[0]
        xs1 = [per_layer[n][k] for n, k in grp1] if carry else None
        dwd, r1 = carried(mm, f"dw_down{l}", A['act'], dz, ta=True, comm=pair_exchange(xs1) if carry else None)
        per_layer['ffn_w_down'][l] = dwd.reshape(N_CHIPS, D_FF // N_CHIPS, D_MODEL)
        ps1 = pair_sums("g1_", xs1, r1) if carry else None
        dact = mm(f"d_act{l}", dz, A['wdown'], tb=True)
        dup, dwa, dwg, dba, dbg = f_bwd(f"f_bwd{l}", s, A['upraw'], dact, A['fcw'], _row(W['ffn_conv_b'][l]))
        per_layer['ffn_conv_w'][l] = jnp.concatenate([dwa[:3], dwg[:3]], axis=1)
        per_layer['ffn_conv_b'][l] = jnp.concatenate([dba[0], dbg[0]], axis=0)
        per_layer['ffn_w_up'][l] = mm(f"dw_up{l}", A['h1'], dup, ta=True, chip_major=True, b_pair=True)
        xs2 = [per_layer[n][k] for n, k in grp2] if carry else None
        dh1_mm, r2 = carried(mm, f"d_h1{l}", dup, A['wup'], tb=True, a_pair=True, comm=pair_exchange(xs2) if carry else None)
        ps2 = pair_sums("g2_", xs2, r2) if carry else None
        dz1, dg, db = _ln_mix_bwd(l, s, dz, dh1_mm, A, W, alpha)
        per_layer['ln_mix_g'][l], per_layer['ln_mix_b'][l] = dg[0], db[0]
        per_layer['w_out'][l] = mm(f"dw_out{l}", A['mixcat'], dz1, ta=True).reshape(N_CHIPS, D_MODEL // N_CHIPS, D_MODEL)
        dmixcat = mm(f"d_mixcat{l}", dz1, A['wout'], tb=True)
        dy_a, dg, db, dcb = a_bwd1(f"a_bwd1{l}", s, A['y_a'], dmixcat, _row(W['ln_a_g'][l]), _row(W['ln_a_b'][l]))
        per_layer['ln_a_g'][l], per_layer['ln_a_b'][l], per_layer['conv_a_b'][l] = dg[0], db[0], dcb[0]
        dproj, dcw = a_bwd2(f"a_bwd2{l}", s, A['proj'], dy_a, A['cw'])
        per_layer['conv_a_w'][l] = dcw[:CONV_A_WIDTH]
        (dq, dk, dv), cs1 = carried(attn_bwd, f"b_attn_bwd{l}", s, A['bq'], A['bk'], A['bv'], A['o_b'], A['lse_b'], dmixcat, 2,
                                    comm=chip_exchange(ps1) if carry else None)
        halves1 = chip_sums("g1_", cs1, ps1) if carry else None
        dproj, dgq, dgk = b_prep_bwd(f"b_prep_bwd{l}", s, A['proj'], A['gq'], A['gk'], tabs['gqa'], gm, dq, dk, dv, dproj)
        per_layer['qk_norm_q'][l], per_layer['qk_norm_k'][l] = dgq[0], dgk[0]
        dproj, dg, db, dsw, dsb = c_bwd(f"c_bwd{l}", s, A['proj'], _row(W['sgu_ln_g'][l]), _row(W['sgu_ln_b'][l]), W['sgu_w'][l],
                                        A['sgu_wt'], A['sgu_bias'], dmixcat, dproj)
        per_layer['sgu_ln_g'][l], per_layer['sgu_ln_b'][l] = dg[0], db[0]
        per_layer['sgu_w'][l], per_layer['sgu_b'][l] = dsw, dsb[:, :4].T
        comms = [half_exchange(halves1), chip_exchange(ps2)] if carry else []
        (dq, dk, dv), cres = carried(attn_bwd, f"d_attn_bwd{l}", s, A['dq'], A['dk'], A['dv'], A['o_d'], A['lse_d'], dmixcat, 6,
                                     comm=join_comms(comms))
        if carry:
            done1, cs2 = split_comm_results(comms, cres)
            red.update(zip(grp1, done1))
            halves2 = chip_sums("g2_", cs2, ps2)
        dproj, dgq, dgkv, dwuq, dwukv = d_prep_bwd(f"d_prep_bwd{l}", s, A['proj'], _row(W['mla_q_norm'][l], 256),
                                                   _row(W['mla_kv_norm'][l]), A['wuq'], A['wukv'], tabs['mla'], tabs['kr'],
                                                   dq, dk, dv, dproj)
        per_layer['mla_q_norm'][l], per_layer['mla_kv_norm'][l] = dgq[0, :MLA_Q_LORA], dgkv[0]
        per_layer['mla_w_uq'][l], per_layer['mla_w_ukv'][l] = _unpad_uq(dwuq), _unpad_ukv(dwukv)
        dw_in, done2 = carried(mm, f"dw_in{l}", A['h'], dproj, ta=True, comm=half_exchange(halves2) if carry else None)
        if carry:
            red.update(zip(grp2, done2))
        per_layer['w_in'][l] = _unpad_w_in(dw_in).reshape(D_MODEL, N_CHIPS, D_IN_PROJ // N_CHIPS).transpose(1, 0, 2)
        dh = mm(f"d_h{l}", dproj, A['wpad'], tb=True, add=dz1, add_scale=alpha)
    dx, dg, db = ln_bwd("ln_in_bwd", s, dh, x, _row(W['ln_in_g']))
    G = dict(per_layer)
    G['ln_in_g'], G['ln_in_b'] = dg[0], db[0]
    for n, k in red:
        G[n][k] = None
    G['reduced'] = red
    return loss_row, dx, G


def _ln_mix_bwd(l, s, dz, dh1_mm, A, W, alpha):
    ins = [('row', dz, D_MODEL, 0), ('row', dh1_mm, D_MODEL, 0), ('row', A['mix'], D_MODEL, 0), ('row', A['h'], D_MODEL, 0),
           ('full', _row(W['ln_mix_g'][l]))]

    def fn(i, j, dzv, dmv, xv, rv, g_ref, o_ref, dg_ref, db_ref):
        xh, r = _ln_stats(alpha * rv + xv)
        d, dg, db = _ln_bwd(alpha * dzv + dmv, xh, r, g_ref[...])
        o_ref[...] = d
        dg_ref[...] += dg
        db_ref[...] += db

    return tiled_call(f"ln_mix_bwd{l}", fn, s, TM, ins, [('row', D_MODEL, D_MODEL, 0, F32), ('acc', (1, D_MODEL)),
                                                         ('acc', (1, D_MODEL))])


ANY = pl.BlockSpec(memory_space=pl.ANY)


def _pos():
    return lax.axis_index("x"), lax.axis_index("y"), lax.axis_index("c")


def _rcopy(src, dst, ssem, rsem, dev):
    return pltpu.make_async_remote_copy(src_ref=src, dst_ref=dst, send_sem=ssem, recv_sem=rsem, device_id=dev,
                                        device_id_type=MESH)


class Comm:
    def __init__(self, operands, out_shape, sems, start, finish, aliases=None):
        self.operands, self.out_shape, self.sems = list(operands), list(out_shape), list(sems)
        self.start, self.finish, self.aliases = start, finish, dict(aliases or {})


def join_comms(comms):
    comms = [cm for cm in comms if cm is not None]
    if not comms:
        return None
    offs, oi, oo, os_ = [], 0, 0, 0
    for cm in comms:
        offs.append((oi, oo, os_))
        oi, oo, os_ = oi + len(cm.operands), oo + len(cm.out_shape), os_ + len(cm.sems)

    def part(fn_name):
        def run(ins, outs, sems):
            for cm, (a, b, s) in zip(comms, offs):
                getattr(cm, fn_name)(ins[a:a + len(cm.operands)], outs[b:b + len(cm.out_shape)], sems[s:s + len(cm.sems)])
        return run

    aliases = {a + i: b + o for cm, (a, b, s) in zip(comms, offs) for i, o in cm.aliases.items()}
    return Comm([v for cm in comms for v in cm.operands], [v for cm in comms for v in cm.out_shape],
                [v for cm in comms for v in cm.sems], part("start"), part("finish"), aliases)


def split_comm_results(comms, res):
    out, p = [], 0
    for cm in comms:
        if cm is None:
            out.append(None)
        else:
            out.append(list(res[p:p + len(cm.out_shape)]))
            p += len(cm.out_shape)
    return out


def run_comm(name, comm):
    n_in, n_out = len(comm.operands), len(comm.out_shape)

    def body(*refs):
        ins, outs, sems = refs[:n_in], refs[n_in:n_in + n_out], refs[n_in + n_out:]
        comm.start(ins, outs, sems)
        comm.finish(ins, outs, sems)

    return pl.pallas_call(body, name=name, in_specs=[ANY] * n_in, out_specs=[ANY] * n_out, out_shape=comm.out_shape,
                          scratch_shapes=[pltpu.SemaphoreType.DMA((k,)) for k in comm.sems],
                          input_output_aliases=comm.aliases)(*comm.operands)


def gather_over_chips(parts):
    n = len(parts)
    layer = [p[1] if isinstance(p, tuple) else None for p in parts]
    parts = [p[0] if isinstance(p, tuple) else p for p in parts]
    shapes = [p.shape if l is None else (1,) + p.shape[1:] for p, l in zip(parts, layer)]
    split = [shp[1] % 32 == 0 for shp in shapes]

    def geometry():
        x, y, c = _pos()
        return x, y, c, 2 * x + y, [(1 - x, y), (x, 1 - y), (1 - x, 1 - y)]

    def rows(t, half):
        a2 = shapes[t][1] // 2
        return pl.ds(half * a2, a2)

    def ici(srcs, dsts, sems, j, t, px, py, c, me):
        k = n * j + t
        src = srcs[t] if layer[t] is None else srcs[t].at[pl.ds(layer[t], 1)]
        if split[t]:
            return _rcopy(src.at[:, rows(t, c), :], dsts[t].at[me, :, rows(t, c), :], sems[0].at[k], sems[1].at[k], (px, py, c))
        return _rcopy(src, dsts[t].at[me], sems[0].at[k], sems[1].at[k], (px, py, c))

    def start(srcs, dsts, sems):
        x, y, c, me, chips = geometry()
        for j, (px, py) in enumerate(chips):
            for t in range(n):
                ici(srcs, dsts, sems, j, t, px, py, c, me).start()

    def finish(srcs, dsts, sems):
        x, y, c, me, chips = geometry()
        fwd = []
        for j, (px, py) in enumerate(chips):
            q = 2 * px + py
            for t in range(n):
                k = n * j + t
                if split[t]:
                    got = dsts[t].at[q, :, rows(t, c), :]
                    _rcopy(got, got, sems[0].at[k], sems[1].at[k], (px, py, c)).wait_recv()
                    fw = _rcopy(got, got, sems[2].at[k], sems[3].at[k], (x, y, 1 - c))
                    fw.start()
                    fwd.append(fw)
                else:
                    _rcopy(dsts[t].at[q], dsts[t].at[q], sems[0].at[k], sems[1].at[k], (px, py, c)).wait_recv()
        for j, (px, py) in enumerate(chips):
            q = 2 * px + py
            for t in range(n):
                if split[t]:
                    other = dsts[t].at[q, :, rows(t, 1 - c), :]
                    _rcopy(other, other, sems[2].at[n * j + t], sems[3].at[n * j + t], (x, y, 1 - c)).wait_recv()
                ici(srcs, dsts, sems, j, t, px, py, c, me).wait_send()
        for fw in fwd:
            fw.wait_send()

    return Comm(parts, [jax.ShapeDtypeStruct((N_CHIPS,) + shp, p.dtype) for p, shp in zip(parts, shapes)], [3 * n] * 4, start, finish)


def pair_exchange(xs):
    n = len(xs)

    def copies(srcs, dsts, sems):
        x, y, c = _pos()
        return [_rcopy(srcs[t].at[:, pl.ds((1 - c) * (xs[t].shape[1] // 2), xs[t].shape[1] // 2), :], dsts[t],
                       sems[0].at[t], sems[1].at[t], (x, y, 1 - c)) for t in range(n)]

    def start(srcs, dsts, sems):
        for cp in copies(srcs, dsts, sems):
            cp.start()

    def finish(srcs, dsts, sems):
        for cp in copies(srcs, dsts, sems):
            cp.wait()

    return Comm(xs, [jax.ShapeDtypeStruct((a.shape[0], a.shape[1] // 2, a.shape[2]), a.dtype) for a in xs], [n, n], start, finish)


def chip_exchange(ps):
    n = len(ps)

    def geometry():
        x, y, c = _pos()
        return c, 2 * x + y, [(1 - x, y), (x, 1 - y), (1 - x, 1 - y)]

    def start(srcs, dsts, sems):
        c, me, chips = geometry()
        for j, (px, py) in enumerate(chips):
            for t in range(n):
                _rcopy(srcs[t].at[2 * px + py], dsts[t].at[me], sems[0].at[n * j + t], sems[1].at[n * j + t], (px, py, c)).start()

    def finish(srcs, dsts, sems):
        c, me, chips = geometry()
        for j, (px, py) in enumerate(chips):
            for t in range(n):
                _rcopy(srcs[t].at[2 * px + py], dsts[t].at[2 * px + py], sems[0].at[n * j + t], sems[1].at[n * j + t], (px, py, c)).wait()

    return Comm(ps, [jax.ShapeDtypeStruct(p.shape, p.dtype) for p in ps], [3 * n, 3 * n], start, finish)


def half_exchange(bufs):
    n = len(bufs)

    def copies(ins, outs, sems):
        x, y, c = _pos()
        cps = []
        for t in range(n):
            a2 = bufs[t].shape[1] // 2
            mine = pl.ds(c * a2, a2)
            cps.append(_rcopy(ins[t].at[:, mine, :], outs[t].at[:, mine, :], sems[0].at[t], sems[1].at[t], (x, y, 1 - c)))
        return cps

    def start(ins, outs, sems):
        for cp in copies(ins, outs, sems):
            cp.start()

    def finish(ins, outs, sems):
        for cp in copies(ins, outs, sems):
            cp.wait()

    return Comm(bufs, [jax.ShapeDtypeStruct(b.shape, b.dtype) for b in bufs], [n, n], start, finish, {t: t for t in range(n)})


LANES = 1024
ROW_TILE_BYTES = 2 * 1024 * 1024


def _row_tile(r, row_bytes):
    if r * row_bytes <= ROW_TILE_BYTES:
        return r
    best = None
    for t in range(16, r, 16):
        if r % t == 0 and t * row_bytes <= ROW_TILE_BYTES:
            best = t
    assert best is not None, (r, row_bytes)
    return best


def pair_sum(name, x, r1, c_arr, out_dtype):
    q, a2, b = r1.shape
    tr = _row_tile(a2, b * 4)
    nb = a2 // tr

    def body(c_ref, x_ref, r_ref, o_ref):
        o_ref[...] = (x_ref[...] + r_ref[...]).astype(out_dtype)

    grid_spec = pltpu.PrefetchScalarGridSpec(
        num_scalar_prefetch=1, grid=(q, nb),
        in_specs=[pl.BlockSpec((None, tr, b), lambda k, i, c_ref: (k, c_ref[0] * nb + i, 0)),
                  pl.BlockSpec((None, tr, b), lambda k, i, c_ref: (k, i, 0))],
        out_specs=pl.BlockSpec((None, tr, b), lambda k, i, c_ref: (k, i, 0)))
    return pl.pallas_call(body, name=name, grid_spec=grid_spec, out_shape=jax.ShapeDtypeStruct(r1.shape, out_dtype),
                          compiler_params=_cparams(("parallel", "parallel")))(c_arr, x, r1)


def chip_sum(name, recv, own, idx):
    nl = len(recv)
    k, a2, b = recv[0].shape
    tr = _row_tile(a2, k * b * recv[0].dtype.itemsize)
    nb = a2 // tr

    def body(idx_ref, *refs):
        rs, os_, o_ref = refs[:nl], refs[nl:2 * nl], refs[2 * nl]
        l = pl.program_id(0)
        me = idx_ref[1]
        for li in range(nl):
            @pl.when(l == li)
            def _(li=li):
                acc = None
                for q in range(k):
                    term = jnp.where(me == q, os_[li][...], rs[li][q]).astype(F32)
                    acc = term if acc is None else acc + term
                o_ref[...] = acc

    def pick(li):
        return lambda l, i, idx_ref: jnp.where(l == li, i, 0)

    in_specs = [pl.BlockSpec((k, tr, b), lambda l, i, idx_ref, _p=pick(li): (0, _p(l, i, idx_ref), 0)) for li in range(nl)]
    in_specs += [pl.BlockSpec((None, tr, b), lambda l, i, idx_ref, _p=pick(li): (idx_ref[1], _p(l, i, idx_ref), 0)) for li in range(nl)]
    grid_spec = pltpu.PrefetchScalarGridSpec(
        num_scalar_prefetch=1, grid=(nl, nb), in_specs=in_specs,
        out_specs=pl.BlockSpec((None, tr, b), lambda l, i, idx_ref: (l, idx_ref[0] * nb + i, 0)))
    return pl.pallas_call(body, name=name, grid_spec=grid_spec, out_shape=jax.ShapeDtypeStruct((nl, 2 * a2, b), F32),
                          compiler_params=_cparams(("arbitrary", "arbitrary")))(idx, *recv, *own)


def adamw(name, gs, w, m, v):
    nl, r, b = w.shape
    assert len(gs) == nl
    tr = _row_tile(r, b * 4)

    def body(*refs):
        g_refs, (w_ref, m_ref, v_ref, go_ref, d_ref, nm_ref, nv_ref) = refs[:nl], refs[nl:]
        l = pl.program_id(0)
        for li in range(nl):
            @pl.when(l == li)
            def _(li=li):
                gv = g_refs[li][...]
                go_ref[...] = gv
                mn = ADAM_B1 * m_ref[...] + (1.0 - ADAM_B1) * gv
                vn = ADAM_B2 * v_ref[...] + (1.0 - ADAM_B2) * (gv * gv)
                m_hat = mn / (1.0 - ADAM_B1 ** ADAM_STEP)
                v_hat = vn / (1.0 - ADAM_B2 ** ADAM_STEP)
                d_ref[...] = -ADAM_LR * (m_hat / (jnp.sqrt(v_hat) + ADAM_EPS) + ADAM_WD * w_ref[...])
                nm_ref[...] = mn
                nv_ref[...] = vn

    g_specs = [pl.BlockSpec((None, tr, b), lambda l, i, _li=li: (0, jnp.where(l == _li, i, 0), 0)) for li in range(nl)]
    spec = pl.BlockSpec((None, tr, b), lambda l, i: (l, i, 0))
    sds = jax.ShapeDtypeStruct(w.shape, F32)
    return pl.pallas_call(body, name=name, grid=(nl, r // tr), in_specs=g_specs + [spec] * 3, out_specs=[spec] * 4,
                          out_shape=[sds] * 4, compiler_params=_cparams(("arbitrary", "arbitrary")))(*gs, w, m, v)


def adamw_many(name, gs, ws, ms, vs):
    n = len(gs)

    def body(*refs):
        for k in range(n):
            g_ref, w_ref, m_ref, v_ref, go_ref, d_ref, nm_ref, nv_ref = (refs[j * n + k] for j in range(8))
            gv = g_ref[...]
            go_ref[...] = gv
            mn = ADAM_B1 * m_ref[...] + (1.0 - ADAM_B1) * gv
            vn = ADAM_B2 * v_ref[...] + (1.0 - ADAM_B2) * (gv * gv)
            m_hat = mn / (1.0 - ADAM_B1 ** ADAM_STEP)
            v_hat = vn / (1.0 - ADAM_B2 ** ADAM_STEP)
            d_ref[...] = -ADAM_LR * (m_hat / (jnp.sqrt(v_hat) + ADAM_EPS) + ADAM_WD * w_ref[...])
            nm_ref[...] = mn
            nv_ref[...] = vn

    vm = pl.BlockSpec(memory_space=pltpu.VMEM)
    res = pl.pallas_call(body, name=name, in_specs=[vm] * (4 * n), out_specs=[vm] * (4 * n),
                         out_shape=[jax.ShapeDtypeStruct(a.shape, F32) for _ in range(4) for a in ws],
                         compiler_params=pltpu.CompilerParams(vmem_limit_bytes=VMEM_LIMIT))(*gs, *ws, *ms, *vs)
    return [list(res[j * n:(j + 1) * n]) for j in range(4)]


def _small_layout(shapes):
    views, windows, r0 = [], [], 0
    for shp in shapes:
        if len(shp) == 4:
            nblk = shp[0] * shp[1]
            assert shp[2:] == (CHUNK, 128) and nblk * 128 <= LANES
            views.append((nblk, CHUNK, 128))
            windows.append([((j,), r0, CHUNK, 128 * j, 128) for j in range(nblk)])
            r0 += CHUNK
            continue
        a, c = (1, shp[0]) if len(shp) == 1 else (int(np.prod(shp[:-1])), shp[-1])
        views.append((a, c))
        wins = []
        for j in range(-(-c // LANES)):
            w = min(LANES, c - j * LANES)
            wins.append(((slice(None), slice(j * LANES, j * LANES + w)), r0, a, 0, w))
            r0 += 8 * -(-a // 8)
        windows.append(wins)
    return 32 * -(-r0 // 32), views, windows


def pack_small(name, arrs, rows, windows):
    def body(*refs):
        ins, o_ref = refs[:-1], refs[-1]
        o_ref[...] = jnp.zeros_like(o_ref)
        for x_ref, wins in zip(ins, windows):
            for idx, r0, nr, l0, nl in wins:
                o_ref[r0:r0 + nr, l0:l0 + nl] = x_ref[idx]

    vm = pl.BlockSpec(memory_space=pltpu.VMEM)
    return pl.pallas_call(body, name=name, in_specs=[vm] * len(arrs), out_specs=vm,
                          out_shape=jax.ShapeDtypeStruct((rows, LANES), F32))(*arrs)


def adamw_windows(name, g_pack, windows, ws, ms, vs):
    n = len(ws)

    def body(g_ref, *refs):
        for k in range(n):
            w_ref, m_ref, v_ref, go_ref, d_ref, nm_ref, nv_ref = (refs[j * n + k] for j in range(7))
            for idx, r0, nr, l0, nl in windows[k]:
                gv = g_ref[r0:r0 + nr, l0:l0 + nl]
                go_ref[idx] = gv
                mn = ADAM_B1 * m_ref[idx] + (1.0 - ADAM_B1) * gv
                vn = ADAM_B2 * v_ref[idx] + (1.0 - ADAM_B2) * (gv * gv)
                m_hat = mn / (1.0 - ADAM_B1 ** ADAM_STEP)
                v_hat = vn / (1.0 - ADAM_B2 ** ADAM_STEP)
                d_ref[idx] = -ADAM_LR * (m_hat / (jnp.sqrt(v_hat) + ADAM_EPS) + ADAM_WD * w_ref[idx])
                nm_ref[idx] = mn
                nv_ref[idx] = vn

    vm = pl.BlockSpec(memory_space=pltpu.VMEM)
    res = pl.pallas_call(body, name=name, in_specs=[vm] * (1 + 3 * n), out_specs=[vm] * (4 * n),
                         out_shape=[jax.ShapeDtypeStruct(a.shape, F32) for _ in range(4) for a in ws],
                         compiler_params=pltpu.CompilerParams(vmem_limit_bytes=VMEM_LIMIT))(g_pack, *ws, *ms, *vs)
    return [list(res[j * n:(j + 1) * n]) for j in range(4)]


def _pack_rows(shapes):
    return [8 * -(-int(np.prod(s)) // (8 * LANES)) for s in shapes]


def _pack(arrs, rows):
    pieces = []
    for a, r in zip(arrs, _pack_rows([a.shape for a in arrs])):
        flat = a.reshape(-1)
        pieces.append(jnp.pad(flat, (0, r * LANES - flat.shape[0])).reshape(r, LANES))
    used = sum(p.shape[0] for p in pieces)
    assert used <= rows
    if used < rows:
        pieces.append(jnp.zeros((rows - used, LANES), F32))
    return jnp.concatenate(pieces, axis=0)


def _unpack(pack, shapes):
    out, off = [], 0
    for shp, r in zip(shapes, _pack_rows(shapes)):
        n = int(np.prod(shp))
        out.append(pack[off:off + r].reshape(-1)[:n].reshape(shp))
        off += r
    return out


BIG_NAMES = ['w_in', 'w_out', 'ffn_w_up', 'ffn_w_down']
MED_NAMES = [n for n in SHARDED_NAMES if n not in BIG_NAMES]
MED_ROWS = 96


def kernel(x, ln_in_g, ln_in_b, w_in, conv_a_w, conv_a_b, ln_a_g, ln_a_b, qk_norm_q, qk_norm_k, sgu_ln_g, sgu_ln_b, sgu_w, sgu_b, mla_q_norm, mla_w_uq, mla_kv_norm, mla_w_ukv, w_out, ln_mix_g, ln_mix_b, ffn_w_up, ffn_conv_w, ffn_conv_b, ffn_w_down, ln_ffn_g, ln_ffn_b, loss_target, m_ln_in_g, m_ln_in_b, m_w_in, m_conv_a_w, m_conv_a_b, m_ln_a_g, m_ln_a_b, m_qk_norm_q, m_qk_norm_k, m_sgu_ln_g, m_sgu_ln_b, m_sgu_w, m_sgu_b, m_mla_q_norm, m_mla_w_uq, m_mla_kv_norm, m_mla_w_ukv, m_w_out, m_ln_mix_g, m_ln_mix_b, m_ffn_w_up, m_ffn_conv_w, m_ffn_conv_b, m_ffn_w_down, m_ln_ffn_g, m_ln_ffn_b, v_ln_in_g, v_ln_in_b, v_w_in, v_conv_a_w, v_conv_a_b, v_ln_a_g, v_ln_a_b, v_qk_norm_q, v_qk_norm_k, v_sgu_ln_g, v_sgu_ln_b, v_sgu_w, v_sgu_b, v_mla_q_norm, v_mla_w_uq, v_mla_kv_norm, v_mla_w_ukv, v_w_out, v_ln_mix_g, v_ln_mix_b, v_ffn_w_up, v_ffn_conv_w, v_ffn_conv_b, v_ffn_w_down, v_ln_ffn_g, v_ln_ffn_b):
    loc = dict(locals())
    w_loc = {n: loc[n] for n in WEIGHT_NAMES}
    m_loc = {n: loc["m_" + n] for n in WEIGHT_NAMES}
    v_loc = {n: loc["v_" + n] for n in WEIGHT_NAMES}
    me_chip = 2 * lax.axis_index("x") + lax.axis_index("y")
    idx = jnp.stack([lax.axis_index("c"), me_chip]).astype(jnp.int32)

    def wire(n):
        return w_loc[n].astype(BF16) if n in BF16_WIRE else w_loc[n]

    assert DEPTH == 2
    early = [n for n in SHARDED_NAMES if n not in LATE_NAMES]
    wired = {n: wire(n) for n in SHARDED_NAMES}
    W = {n: w_loc[n] for n in REPL_NAMES}
    for n, g in zip(early, run_comm("gather_early", gather_over_chips([(wired[n], 0) for n in early]))):
        W[n] = [_assemble(n, wired[n][0:1], g, me_chip)[0], None]

    dist = dict(idx=idx, me=me_chip, late={n: wired[n] for n in LATE_NAMES}, next=(early, [wired[n] for n in early]))
    loss_row, dx, G = local_step(x[0], loss_target[0], W, dist)
    loss = lax.psum(loss_row[0, 0], ("x", "y", "c"))
    red = G['reduced']

    med = []
    for q in range(N_CHIPS):
        pieces = [lax.slice_in_dim(jnp.stack(G[n]), q * w_loc[n].shape[2], (q + 1) * w_loc[n].shape[2], axis=2) for n in MED_NAMES]
        med.append(_pack(pieces, MED_ROWS))
    rest = [(n, l) for n in BIG_NAMES for l in range(DEPTH) if (n, l) not in red]
    small_rows, views, windows = _small_layout([w_loc[n].shape for n in REPL_NAMES])
    small = [(jnp.stack(G[n]) if isinstance(G[n], list) else G[n]).reshape(v) for n, v in zip(REPL_NAMES, views)]
    small_pack = pack_small("pack_small", small, small_rows, windows)
    xs = [G[n][l] for n, l in rest] + [jnp.stack(med), jnp.broadcast_to(small_pack[None], (N_CHIPS, small_rows, LANES))]
    wire_dt = [BF16] * (len(xs) - 1) + [F32]
    r1 = run_comm("pair_exchange", pair_exchange(xs))
    ps = [pair_sum(f"pair_sum{t}", xv, rv, idx, dt) for t, (xv, rv, dt) in enumerate(zip(xs, r1, wire_dt))]
    cs = run_comm("chip_exchange", chip_exchange(ps))
    halves = [chip_sum(f"chip_sum{t}", [cv], [pv], idx) for t, (cv, pv) in enumerate(zip(cs, ps))]
    done = run_comm("half_exchange", half_exchange(halves))
    red.update(zip(rest, done[:-2]))
    g_med, g_small = done[-2], done[-1]

    outs = {k: {} for k in ("grad", "delta", "new_m", "new_v")}
    for n in BIG_NAMES:
        res = adamw(f"adamw_{n}", [red[(n, l)] for l in range(DEPTH)], w_loc[n], m_loc[n], v_loc[n])
        for k, v in zip(("grad", "delta", "new_m", "new_v"), res):
            outs[k][n] = v
    res = adamw_many("adamw_med", _unpack(g_med[0], [w_loc[n].shape for n in MED_NAMES]),
                     *[[d[n] for n in MED_NAMES] for d in (w_loc, m_loc, v_loc)])
    for k, vals in zip(("grad", "delta", "new_m", "new_v"), res):
        outs[k].update(zip(MED_NAMES, vals))
    res = adamw_windows("adamw_small", g_small[0], windows,
                        *[[d[n].reshape(v) for n, v in zip(REPL_NAMES, views)] for d in (w_loc, m_loc, v_loc)])
    for k, vals in zip(("grad", "delta", "new_m", "new_v"), res):
        outs[k].update({n: v.reshape(w_loc[n].shape) for n, v in zip(REPL_NAMES, vals)})
    return (loss, dx[None], *[outs["grad"][n] for n in WEIGHT_NAMES], *[outs["delta"][n] for n in WEIGHT_NAMES],
            *[outs["new_m"][n] for n in WEIGHT_NAMES], *[outs["new_v"][n] for n in WEIGHT_NAMES])
```

```python
import functools
import math

import jax
import jax.numpy as jnp
import numpy as np
from jax import lax
from jax.experimental import pallas as pl
from jax.experimental.pallas import tpu as pltpu

F32 = jnp.float32
BF16 = jnp.bfloat16

D_MODEL = 1024
DEPTH = 2
GRID_W = 64
GROUP_W = 256
HEAD_DIM = 64
CONV_A_WIDTH = 31
CHUNK = 128
MLA_Q_LORA = 192
MLA_KV_LORA = 128
MLA_NOPE = 64
MLA_ROPE = 32
D_FF = 2816
ROPE_THETA = 10000.0
DEEPNORM_ALPHA = (2 * DEPTH) ** 0.25
LN_EPS = 1e-5
RMS_EPS = 1e-6
D_IN_PROJ = 1888
PROJ_W = 2048

ADAM_LR = 0.001
ADAM_B1 = 0.9
ADAM_B2 = 0.999
ADAM_EPS = 1e-08
ADAM_WD = 0.01
ADAM_STEP = 10

VMEM_LIMIT = 56 * 1024 * 1024
MESH = pl.DeviceIdType.MESH
N_CHIPS = 4
N_DEV = 8

WEIGHT_NAMES = ['ln_in_g', 'ln_in_b', 'w_in', 'conv_a_w', 'conv_a_b', 'ln_a_g', 'ln_a_b', 'qk_norm_q', 'qk_norm_k',
                'sgu_ln_g', 'sgu_ln_b', 'sgu_w', 'sgu_b', 'mla_q_norm', 'mla_w_uq', 'mla_kv_norm', 'mla_w_ukv', 'w_out',
                'ln_mix_g', 'ln_mix_b', 'ffn_w_up', 'ffn_conv_w', 'ffn_conv_b', 'ffn_w_down', 'ln_ffn_g', 'ln_ffn_b']
SHARDED_AXIS = {'w_in': 2, 'conv_a_w': 2, 'mla_w_uq': 2, 'mla_w_ukv': 2, 'w_out': 1, 'ffn_w_up': 2, 'ffn_conv_w': 2,
                'ffn_w_down': 1}
SHARDED_NAMES = [n for n in WEIGHT_NAMES if n in SHARDED_AXIS]
REPL_NAMES = [n for n in WEIGHT_NAMES if n not in SHARDED_AXIS]
BF16_WIRE = ('w_in', 'mla_w_uq', 'mla_w_ukv', 'w_out', 'ffn_w_up', 'ffn_w_down')


def _cparams(sem):
    return pltpu.CompilerParams(dimension_semantics=sem, vmem_limit_bytes=VMEM_LIMIT)


def _pick(n, cands):
    for c in cands:
        if n % c == 0:
            return c
    return n


def carried_call(name, body, grid, in_specs, out_specs, out_shape, scratch, semantics, args, comm=None):
    if comm is None:
        res = pl.pallas_call(body, name=name, grid=grid, in_specs=in_specs, out_specs=out_specs, out_shape=out_shape,
                             scratch_shapes=scratch, compiler_params=_cparams(semantics))(*args)
        return list(res), None
    n_in, n_out, n_scr = len(in_specs), len(out_specs), len(scratch)
    nci, nco = len(comm.operands), len(comm.out_shape)

    def carried(*refs):
        o0 = n_in + nci
        s0 = o0 + n_out + nco
        ins, cins = refs[:n_in], refs[n_in:o0]
        outs, couts = refs[o0:o0 + n_out], refs[o0 + n_out:s0]
        scr, sems = refs[s0:s0 + n_scr], refs[s0 + n_scr:]
        ids = [pl.program_id(d) for d in range(len(grid))]
        first = functools.reduce(lambda u, v: u & v, [i == 0 for i in ids])
        last = functools.reduce(lambda u, v: u & v, [i == g - 1 for i, g in zip(ids, grid)])

        @pl.when(first)
        def _():
            comm.start(cins, couts, sems)

        body(*ins, *outs, *scr)

        @pl.when(last)
        def _():
            comm.finish(cins, couts, sems)

    res = pl.pallas_call(
        carried, name=name, grid=grid, in_specs=list(in_specs) + [ANY] * nci, out_specs=list(out_specs) + [ANY] * nco,
        out_shape=list(out_shape) + comm.out_shape, scratch_shapes=list(scratch) + [pltpu.SemaphoreType.DMA((k,)) for k in comm.sems],
        input_output_aliases={n_in + i: n_out + o for i, o in comm.aliases.items()},
        compiler_params=_cparams(("arbitrary",) * len(grid)))(*args, *comm.operands)
    return list(res[:n_out]), list(res[n_out:])


def mm(name, a, b, *, ta=False, tb=False, add=None, add_scale=1.0, out_dtype=F32, chip_major=False, comm=None,
       a_pair=False, b_pair=False):
    assert not (a_pair and ta) and not (b_pair and tb)
    a_shape = (a.shape[1], 2 * a.shape[2]) if a_pair else a.shape
    b_shape = (b.shape[1], 2 * b.shape[2]) if b_pair else b.shape
    m, k = (a_shape[1], a_shape[0]) if ta else a_shape
    n = b_shape[0] if tb else b_shape[1]
    assert (b_shape[1] if tb else b_shape[0]) == k
    tm = _pick(m, (1024, 1408, 512, 256, 128))
    tn = n // N_CHIPS if chip_major else _pick(n, (1024, 1408, 512, 256, 128))
    tk = _pick(k, (1024, 1408, 512, 256, 128))
    nk = k // tk
    a_spec = pl.BlockSpec((tk, tm), lambda i, j, kk: (kk, i)) if ta else pl.BlockSpec((tm, tk), lambda i, j, kk: (i, kk))
    b_spec = pl.BlockSpec((tn, tk), lambda i, j, kk: (j, kk)) if tb else pl.BlockSpec((tk, tn), lambda i, j, kk: (kk, j))
    if a_pair:
        per = a.shape[2] // tk
        assert per * tk == a.shape[2]
        a_spec = pl.BlockSpec((None, tm, tk), lambda i, j, kk: (kk // per, i, kk % per))
    if b_pair:
        per = b.shape[2] // tn
        assert per * tn == b.shape[2]
        b_spec = pl.BlockSpec((None, tk, tn), lambda i, j, kk: (j // per, kk, j % per))
    in_specs = [a_spec, b_spec]
    args = [a, b]
    if add is not None:
        in_specs.append(pl.BlockSpec((tm, tn), lambda i, j, kk: (i, j)))
        args.append(add)
    dims = (((0 if ta else 1,), (1 if tb else 0,)), ((), ()))

    def body(*refs):
        a_ref, b_ref = refs[0], refs[1]
        add_ref = refs[2] if add is not None else None
        o_ref, acc_ref = refs[-2], refs[-1]
        kk = pl.program_id(2)
        part = lax.dot_general(a_ref[...].astype(BF16), b_ref[...].astype(BF16), dims, preferred_element_type=F32)

        @pl.when(kk == 0)
        def _():
            acc_ref[...] = part

        @pl.when(kk > 0)
        def _():
            acc_ref[...] += part

        @pl.when(kk == nk - 1)
        def _():
            r = acc_ref[...]
            if add_ref is not None:
                r = r + add_scale * add_ref[...].astype(F32)
            o_ref[...] = r.astype(out_dtype)

    if chip_major:
        out_spec = pl.BlockSpec((None, tm, tn), lambda i, j, kk: (j, i, 0))
        out_shape = jax.ShapeDtypeStruct((N_CHIPS, m, tn), out_dtype)
    else:
        out_spec = pl.BlockSpec((tm, tn), lambda i, j, kk: (i, j))
        out_shape = jax.ShapeDtypeStruct((m, n), out_dtype)
    res, cres = carried_call(name, body, (m // tm, n // tn, nk), in_specs, [out_spec], [out_shape], [pltpu.VMEM((tm, tn), F32)],
                             ("parallel", "parallel", "arbitrary"), args, comm)
    return res[0] if comm is None else (res[0], cres)


def _cb(cb):
    return cb if callable(cb) else (lambda j, _c=cb: _c)


def tiled_call(name, fn, s, tm, ins, outs, ncol=1):
    nrow = s // tm
    in_specs, args, kinds = [], [], []
    for it in ins:
        kind = it[0]
        if kind == 'row':
            _, arr, w, cb = it
            in_specs.append(pl.BlockSpec((tm, w), lambda j, i, _c=_cb(cb): (i, _c(j))))
            args.append(arr)
            kinds.append(('row',))
        elif kind == 'halo':
            _, arr, w, cb, h = it
            r = tm // h
            nh = s // h
            in_specs.append(pl.BlockSpec((h, w), lambda j, i, _c=_cb(cb), _r=r: (jnp.maximum(i * _r - 1, 0), _c(j))))
            in_specs.append(pl.BlockSpec((tm, w), lambda j, i, _c=_cb(cb): (i, _c(j))))
            in_specs.append(pl.BlockSpec((h, w), lambda j, i, _c=_cb(cb), _r=r, _n=nh: (jnp.minimum((i + 1) * _r, _n - 1), _c(j))))
            args += [arr, arr, arr]
            kinds.append(('halo',))
        elif kind == 'full':
            arr = it[1]
            in_specs.append(pl.BlockSpec(arr.shape, lambda j, i, _n=arr.ndim: (0,) * _n))
            args.append(arr)
            kinds.append(('ref',))
        elif kind == 'col':
            _, arr, w, cb = it
            in_specs.append(pl.BlockSpec((arr.shape[0], w), lambda j, i, _c=_cb(cb): (0, _c(j))))
            args.append(arr)
            kinds.append(('ref',))
        elif kind == 'hm':
            arr = it[1]
            in_specs.append(pl.BlockSpec((arr.shape[0], tm, arr.shape[2]), lambda j, i: (0, i, 0)))
            args.append(arr)
            kinds.append(('ref',))
        else:
            raise ValueError(kind)
    out_specs, out_shapes, okinds, into = [], [], [], []
    for ot in outs:
        kind = ot[0]
        if kind == 'row':
            _, wt, w, cb, dt = ot
            out_specs.append(pl.BlockSpec((tm, w), lambda j, i, _c=_cb(cb): (i, _c(j))))
            out_shapes.append(jax.ShapeDtypeStruct((s, wt), dt))
            okinds.append('row')
        elif kind == 'hm':
            _, hh, d, dt = ot
            out_specs.append(pl.BlockSpec((hh, tm, d), lambda j, i: (0, i, 0)))
            out_shapes.append(jax.ShapeDtypeStruct((hh, s, d), dt))
            okinds.append('hm')
        elif kind == 'acc':
            shape = ot[1]
            out_specs.append(pl.BlockSpec(shape, lambda j, i, _n=len(shape): (0,) * _n))
            out_shapes.append(jax.ShapeDtypeStruct(shape, F32))
            okinds.append('acc')
        elif kind == 'colacc':
            _, r, wt, w, cb = ot
            out_specs.append(pl.BlockSpec((r, w), lambda j, i, _c=_cb(cb): (0, _c(j))))
            out_shapes.append(jax.ShapeDtypeStruct((r, wt), F32))
            okinds.append('colacc')
        elif kind == 'row_into':
            _, buf, w, cb = ot
            out_specs.append(pl.BlockSpec((tm, w), lambda j, i, _c=_cb(cb): (i, _c(j))))
            out_shapes.append(jax.ShapeDtypeStruct(buf.shape, buf.dtype))
            okinds.append('row')
            into.append((len(out_specs) - 1, buf))
        elif kind == 'row3':
            _, n0, wt, w, cb, dt = ot
            out_specs.append(pl.BlockSpec((n0, tm, w), lambda j, i, _c=_cb(cb): (0, i, _c(j))))
            out_shapes.append(jax.ShapeDtypeStruct((n0, s, wt), dt))
            okinds.append('row')
        else:
            raise ValueError(kind)
    n_in = len(in_specs)
    aliases = {n_in + e: o for e, (o, _) in enumerate(into)}
    in_specs = in_specs + [pl.BlockSpec(memory_space=pl.ANY)] * len(into)
    args = args + [buf for _, buf in into]

    def body(*refs):
        j = pl.program_id(0)
        i = pl.program_id(1)
        in_refs, out_refs = refs[:n_in], refs[n_in + len(into):]
        items, p = [], 0
        for kd in kinds:
            if kd[0] == 'row':
                items.append(in_refs[p][...])
                p += 1
            elif kd[0] == 'halo':
                prev, cen, nxt = in_refs[p][...], in_refs[p + 1][...], in_refs[p + 2][...]
                prev = jnp.where(i == 0, jnp.zeros_like(prev), prev)
                nxt = jnp.where(i == nrow - 1, jnp.zeros_like(nxt), nxt)
                items.append(jnp.concatenate([prev, cen, nxt], axis=0))
                p += 3
            else:
                items.append(in_refs[p])
                p += 1
        for o_ref, kd in zip(out_refs, okinds):
            if kd == 'acc':
                @pl.when((i == 0) & (j == 0))
                def _(o_ref=o_ref):
                    o_ref[...] = jnp.zeros_like(o_ref)
            elif kd == 'colacc':
                @pl.when(i == 0)
                def _(o_ref=o_ref):
                    o_ref[...] = jnp.zeros_like(o_ref)
        fn(i, j, *items, *out_refs)

    res = pl.pallas_call(
        body, name=name, grid=(ncol, nrow), in_specs=in_specs, out_specs=out_specs, out_shape=out_shapes,
        input_output_aliases=aliases, compiler_params=_cparams(("arbitrary", "arbitrary")),
    )(*args)
    return res


def _sigmoid(x):
    return 1.0 / (1.0 + jnp.exp(-x))


def _ln_stats(x):
    mu = jnp.mean(x, axis=1, keepdims=True)
    xc = x - mu
    var = jnp.mean(xc * xc, axis=1, keepdims=True)
    r = lax.rsqrt(var + LN_EPS)
    return xc * r, r


def _ln_bwd(dy, xh, r, g):
    dxh = dy * g
    dx = r * (dxh - jnp.mean(dxh, axis=1, keepdims=True) - xh * jnp.mean(dxh * xh, axis=1, keepdims=True))
    return dx, jnp.sum(dy * xh, axis=0, keepdims=True), jnp.sum(dy, axis=0, keepdims=True)


def _gmean(v, gm):
    hi = v.astype(BF16)
    lo = (v - hi.astype(F32)).astype(BF16)
    return jnp.dot(hi, gm, preferred_element_type=F32) + jnp.dot(lo, gm, preferred_element_type=F32)


def _roll_l(x, sh):
    return pltpu.roll(x, sh % x.shape[1], 1)


def _rope(x, c, sa, sb, sh):
    return x * c + _roll_l(x, -sh) * sa + _roll_l(x, sh) * sb


def _rope_t(dy, c, sa, sb, sh):
    return dy * c + _roll_l(dy * sa, sh) + _roll_l(dy * sb, -sh)


def _roll_r(x, s):
    return pltpu.roll(x, (-s) % x.shape[0], 0)


_GELU_C = math.sqrt(2.0 / math.pi)


def _gelu(x):
    t = jnp.tanh(_GELU_C * (x + 0.044715 * x * x * x))
    return 0.5 * x * (1.0 + t), t


def _gelu_grad(x, t):
    return 0.5 * (1.0 + t) + 0.5 * x * (1.0 - t * t) * _GELU_C * (1.0 + 3.0 * 0.044715 * x * x)


def _dot_nt(a, b):
    return lax.dot_general(a, b, (((1,), (1,)), ((), ())), preferred_element_type=F32)


def _dot_tn(a, b):
    return lax.dot_general(a, b, (((0,), (0,)), ((), ())), preferred_element_type=F32)


def _rope_block(s, d):
    t = jnp.arange(s)
    row = (t // GRID_W).astype(F32)
    col = (t % GRID_W).astype(F32)
    half = d // 4
    inv = ROPE_THETA ** (-jnp.arange(half, dtype=F32) / half)
    z = jnp.zeros((s, half), F32)
    cs, sas, sbs = [], [], []
    for pos in (row, col):
        ang = pos[:, None] * inv[None, :]
        co, si = jnp.cos(ang), jnp.sin(ang)
        cs += [co, co]
        sas += [-si, z]
        sbs += [z, si]
    return tuple(jnp.concatenate(v, axis=1) for v in (cs, sas, sbs))


def _rope_tables(s):
    gqa = tuple(jnp.tile(a, (1, 4)) for a in _rope_block(s, HEAD_DIM))
    c32, sa32, sb32 = _rope_block(s, MLA_ROPE)

    def head128(c, fill):
        return jnp.concatenate([jnp.full((s, 64), fill, F32), c, jnp.zeros((s, 32), F32)], axis=1)

    mla = tuple(jnp.tile(a, (1, 4)) for a in (head128(c32, 1.0), head128(sa32, 0.0), head128(sb32, 0.0)))
    kr = (head128(c32, 0.0), head128(sa32, 0.0), head128(sb32, 0.0))
    return dict(gqa=gqa, mla=mla, kr=kr)


def _group_mean_matrix(w, g):
    idx = np.arange(w) // g
    return jnp.asarray((idx[:, None] == idx[None, :]).astype(np.float32) / g, dtype=BF16)


TM = 512


def ln_fwd(name, s, x, g, b, res=None, alpha=1.0):
    ins = [('row', x, D_MODEL, 0)] + ([('row', res, D_MODEL, 0)] if res is not None else []) + [('full', g), ('full', b)]

    def fn(i, j, *a):
        if res is not None:
            xv, rv, g_ref, b_ref, o_ref = a
            z = alpha * rv + xv
        else:
            xv, g_ref, b_ref, o_ref = a
            z = xv
        xh, _ = _ln_stats(z)
        o_ref[...] = xh * g_ref[...] + b_ref[...]

    return tiled_call(name, fn, s, TM, ins, [('row', D_MODEL, D_MODEL, 0, F32)])[0]


def ln_bwd(name, s, dy, x, g, res=None, alpha=1.0):
    ins = [('row', dy, D_MODEL, 0), ('row', x, D_MODEL, 0)]
    ins += ([('row', res, D_MODEL, 0)] if res is not None else []) + [('full', g)]

    def fn(i, j, *a):
        a = list(a)
        dyv = a.pop(0)
        xv = a.pop(0)
        z = alpha * a.pop(0) + xv if res is not None else xv
        g_ref, dz_ref, dg_ref, db_ref = a
        xh, r = _ln_stats(z)
        dz, dg, db = _ln_bwd(dyv, xh, r, g_ref[...])
        dz_ref[...] = dz
        dg_ref[...] += dg
        db_ref[...] += db

    return tiled_call(name, fn, s, TM, ins, [('row', D_MODEL, D_MODEL, 0, F32), ('acc', (1, D_MODEL)), ('acc', (1, D_MODEL))])


def loss_and_ln_bwd(name, s, x, res, g, b, tgt, alpha):
    ins = [('row', x, D_MODEL, 0), ('row', res, D_MODEL, 0), ('row', tgt, D_MODEL, 0), ('full', g), ('full', b)]

    def fn(i, j, xv, rv, tv, g_ref, b_ref, dz_ref, dg_ref, db_ref, loss_ref):
        xh, r = _ln_stats(alpha * rv + xv)
        y = xh * g_ref[...] + b_ref[...]
        e = y - tv
        dz, dg, db = _ln_bwd(e * (1.0 / D_MODEL), xh, r, g_ref[...])
        dz_ref[...] = dz
        dg_ref[...] += dg
        db_ref[...] += db
        loss_ref[...] += jnp.sum(jnp.sum(e * e, axis=0, keepdims=True), axis=1, keepdims=True) * (0.5 / D_MODEL)

    return tiled_call(name, fn, s, TM, ins, [('row', D_MODEL, D_MODEL, 0, F32), ('acc', (1, D_MODEL)), ('acc', (1, D_MODEL)),
                                             ('acc', (1, 128))])


HALO_A = 16


def _glu(a_in):
    return a_in[:, :GROUP_W] * _sigmoid(a_in[:, GROUP_W:])


def _row_windows(x, halo, tm):
    rolled = {0: x}

    def window(s):
        b = s % 8
        if b not in rolled:
            rolled[b] = _roll_r(x, b)
        off = halo + s - b
        return rolled[b][off:off + tm]

    return window


def a_fwd(name, s, proj, cw, cb, g, b):
    def fn(i, j, a_ext, cw_ref, cb_ref, g_ref, b_ref, y_ref, o_ref):
        a_at = _row_windows(_glu(a_ext), HALO_A, TM)
        acc = jnp.zeros((TM, GROUP_W), F32)
        for k in range(CONV_A_WIDTH):
            acc = acc + a_at(k - 15) * cw_ref[pl.ds(k, 1), :]
        y = acc + cb_ref[...]
        y_ref[...] = y
        xh, _ = _ln_stats(y)
        z = xh * g_ref[...] + b_ref[...]
        o_ref[...] = z * _sigmoid(z)

    ins = [('halo', proj, 2 * GROUP_W, 0, HALO_A), ('full', cw), ('full', cb), ('full', g), ('full', b)]
    return tiled_call(name, fn, s, TM, ins, [('row', GROUP_W, GROUP_W, 0, F32), ('row', GROUP_W, GROUP_W, 0, F32)])


def a_bwd1(name, s, y, dmix, g, b):
    def fn(i, j, yv, do, g_ref, b_ref, dy_ref, dg_ref, db_ref, dcb_ref):
        xh, r = _ln_stats(yv)
        z = xh * g_ref[...] + b_ref[...]
        sg = _sigmoid(z)
        dz = do * sg * (1.0 + z * (1.0 - sg))
        dy, dg, db = _ln_bwd(dz, xh, r, g_ref[...])
        dy_ref[...] = dy
        dg_ref[...] += dg
        db_ref[...] += db
        dcb_ref[...] += jnp.sum(dy, axis=0, keepdims=True)

    ins = [('row', y, GROUP_W, 0), ('row', dmix, GROUP_W, 0), ('full', g), ('full', b)]
    return tiled_call(name, fn, s, TM, ins, [('row', GROUP_W, GROUP_W, 0, F32), ('acc', (1, GROUP_W)), ('acc', (1, GROUP_W)),
                                             ('acc', (1, GROUP_W))])


def a_bwd2(name, s, proj, dy, cw):
    def fn(i, j, a_ext, dy_ext, cw_ref, da_ref, dcw_ref):
        a1, a2 = a_ext[:, :GROUP_W], a_ext[:, GROUP_W:]
        sg = _sigmoid(a2)
        a = a1 * sg
        dyc = dy_ext[HALO_A:HALO_A + TM]
        dy_at, a_at = _row_windows(dy_ext, HALO_A, TM), _row_windows(a, HALO_A, TM)
        da = jnp.zeros((TM, GROUP_W), F32)
        for k in range(CONV_A_WIDTH):
            da = da + dy_at(15 - k) * cw_ref[pl.ds(k, 1), :]
            dcw_ref[pl.ds(k, 1), :] += jnp.sum(dyc * a_at(k - 15), axis=0, keepdims=True)
        a1c, sgc = a1[HALO_A:HALO_A + TM], sg[HALO_A:HALO_A + TM]
        da_ref[...] = jnp.concatenate([da * sgc, da * a1c * sgc * (1.0 - sgc)], axis=1).astype(BF16)

    ins = [('halo', proj, 2 * GROUP_W, 0, HALO_A), ('halo', dy, GROUP_W, 0, HALO_A), ('full', cw)]
    return tiled_call(name, fn, s, TM, ins, [('row', PROJ_W, 2 * GROUP_W, 0, BF16), ('acc', (32, GROUP_W))])


def b_prep_fwd(name, s, proj, gq, gk, tabs, gm):
    c, sa, sb = tabs

    def fn(i, j, qkv, cv, sav, sbv, gq_ref, gk_ref, gm_ref, q_ref, k_ref, v_ref):
        q, k, v = qkv[:, :256], qkv[:, 256:384], qkv[:, 384:]
        gmv = gm_ref[...]
        qn = q * lax.rsqrt(_gmean(q * q, gmv) + RMS_EPS) * gq_ref[...]
        kn = k * lax.rsqrt(_gmean(k * k, gmv[:128, :128]) + RMS_EPS) * gk_ref[...]
        qr = _rope(qn, cv, sav, sbv, 16)
        kr = _rope(kn, cv[:, :128], sav[:, :128], sbv[:, :128], 16)
        for h in range(4):
            q_ref[h] = (qr[:, h * 64:(h + 1) * 64] * QSCALE_GQA).astype(BF16)
        for h in range(2):
            k_ref[h] = kr[:, h * 64:(h + 1) * 64].astype(BF16)
            v_ref[h] = v[:, h * 64:(h + 1) * 64].astype(BF16)

    ins = [('row', proj, 512, 1), ('row', c, 256, 0), ('row', sa, 256, 0), ('row', sb, 256, 0), ('full', gq), ('full', gk),
           ('full', gm)]
    return tiled_call(name, fn, s, TM, ins, [('hm', 4, 64, BF16), ('hm', 2, 64, BF16), ('hm', 2, 64, BF16)])


def b_prep_bwd(name, s, proj, gq, gk, tabs, gm, dq_hm, dk_hm, dv_hm, dproj):
    c, sa, sb = tabs

    def fn(i, j, qkv, cv, sav, sbv, gq_ref, gk_ref, gm_ref, dq_ref, dk_ref, dv_ref, dp_ref, dgq_ref, dgk_ref):
        q, k = qkv[:, :256], qkv[:, 256:384]
        gmv = gm_ref[...]
        dqr = jnp.concatenate([dq_ref[h] for h in range(4)], axis=1) * QSCALE_GQA
        dkr = jnp.concatenate([dk_ref[h] for h in range(2)], axis=1)
        dv = jnp.concatenate([dv_ref[h] for h in range(2)], axis=1)
        outs = []
        for x, dxr, g_ref, gmx, w, dg_ref in ((q, dqr, gq_ref, gmv, 256, dgq_ref), (k, dkr, gk_ref, gmv[:128, :128], 128, dgk_ref)):
            dn = _rope_t(dxr, cv[:, :w], sav[:, :w], sbv[:, :w], 16)
            r = lax.rsqrt(_gmean(x * x, gmx) + RMS_EPS)
            gv = g_ref[...]
            dx = gv * r * dn - x * (r * r * r) * _gmean(x * gv * dn, gmx)
            dgt = jnp.sum(x * r * dn, axis=0, keepdims=True)
            dg = dgt[:, 0:64]
            for h in range(1, w // 64):
                dg = dg + dgt[:, h * 64:(h + 1) * 64]
            dg_ref[...] += dg
            outs.append(dx)
        dp_ref[...] = jnp.concatenate(outs + [dv], axis=1).astype(BF16)

    ins = [('row', proj, 512, 1), ('row', c, 256, 0), ('row', sa, 256, 0), ('row', sb, 256, 0), ('full', gq), ('full', gk),
           ('full', gm), ('hm', dq_hm), ('hm', dk_hm), ('hm', dv_hm)]
    return tiled_call(name, fn, s, TM, ins, [('row_into', dproj, 512, 1), ('acc', (1, 64)), ('acc', (1, 64))])


TQ_BWD = 512
TQ_FWD = 256


LOG2E = 1.4426950408889634
QSCALE_GQA = HEAD_DIM ** -0.5 * LOG2E
QSCALE_MLA = (MLA_NOPE + MLA_ROPE) ** -0.5 * LOG2E


def attn_fwd(name, s, q_hm, k_hm, v_hm, comm=None):
    dk = q_hm.shape[2]
    nkv = k_hm.shape[0]
    tq = TQ_FWD

    def body(q_ref, k_ref, v_ref, o_ref, l_ref):
        outs = []
        for g in range(4):
            kv = g * nkv // 4
            sc = _dot_nt(q_ref[g], k_ref[kv])
            m = jnp.max(sc, axis=1, keepdims=True)
            p = jnp.exp2(sc - m)
            l = jnp.sum(p, axis=1, keepdims=True)
            o = jnp.dot(p.astype(BF16), v_ref[kv], preferred_element_type=F32)
            outs.append(o / l)
            l_ref[g] = m + jnp.log2(l)
        o_ref[...] = jnp.concatenate(outs, axis=1)

    res, cres = carried_call(
        name, body, (s // tq,),
        [pl.BlockSpec((4, tq, dk), lambda i: (0, i, 0)),
         pl.BlockSpec((nkv, s, dk), lambda i: (0, 0, 0)),
         pl.BlockSpec((nkv, s, 64), lambda i: (0, 0, 0))],
        [pl.BlockSpec((tq, 256), lambda i: (i, 0)), pl.BlockSpec((4, tq, 1), lambda i: (0, i, 0))],
        [jax.ShapeDtypeStruct((s, 256), F32), jax.ShapeDtypeStruct((4, s, 1), F32)], [],
        ("arbitrary",), [q_hm, k_hm, v_hm], comm)
    return res if comm is None else (res, cres)


def attn_bwd(name, s, q_hm, k_hm, v_hm, o, lse, dmixcat, dcol, comm=None):
    dk = q_hm.shape[2]
    nkv = k_hm.shape[0]
    kpb = nkv // 2

    def body(q_ref, k_ref, v_ref, o_ref, l_ref, do_ref, dq_ref, dk_ref, dv_ref):
        @pl.when(pl.program_id(1) == 0)
        def _():
            dk_ref[...] = jnp.zeros_like(dk_ref)
            dv_ref[...] = jnp.zeros_like(dv_ref)

        dob, ob = do_ref[...], o_ref[...]
        for g in range(2):
            kv = g if kpb == 2 else 0
            do, ov = dob[:, 64 * g:64 * (g + 1)], ob[:, 64 * g:64 * (g + 1)]
            q, k, v = q_ref[g], k_ref[kv], v_ref[kv]
            p = jnp.exp2(_dot_nt(q, k) - l_ref[g])
            dp = _dot_nt(do.astype(BF16), v)
            delta = jnp.sum(do * ov, axis=1, keepdims=True)
            ds = (p * (dp - delta) * (1.0 / LOG2E)).astype(BF16)
            dq_ref[g] = jnp.dot(ds, k, preferred_element_type=F32)
            dk_ref[kv] += _dot_tn(ds, q)
            dv_ref[kv] += _dot_tn(p.astype(BF16), do.astype(BF16))

    tq = TQ_BWD if s % TQ_BWD == 0 else TQ_FWD
    res, cres = carried_call(
        name, body, (2, s // tq),
        [pl.BlockSpec((2, tq, dk), lambda p, i: (p, i, 0)),
         pl.BlockSpec((kpb, s, dk), lambda p, i: (p, 0, 0)), pl.BlockSpec((kpb, s, 64), lambda p, i: (p, 0, 0)),
         pl.BlockSpec((tq, 128), lambda p, i: (i, p)),
         pl.BlockSpec((2, tq, 1), lambda p, i: (p, i, 0)),
         pl.BlockSpec((tq, 128), lambda p, i: (i, dcol + p))],
        [pl.BlockSpec((2, tq, dk), lambda p, i: (p, i, 0)),
         pl.BlockSpec((kpb, s, dk), lambda p, i: (p, 0, 0)), pl.BlockSpec((kpb, s, 64), lambda p, i: (p, 0, 0))],
        [jax.ShapeDtypeStruct((4, s, dk), F32), jax.ShapeDtypeStruct((nkv, s, dk), F32), jax.ShapeDtypeStruct((nkv, s, 64), F32)],
        [], ("arbitrary", "arbitrary"), [q_hm, k_hm, v_hm, o, lse, dmixcat], comm)
    return res if comm is None else (res, cres)


def _lane_group_masks(w, g):
    lane = lax.broadcasted_iota(jnp.int32, (1, w), 1)
    return [((lane >= k * g) & (lane < (k + 1) * g)).astype(F32) for k in range(w // g)]


def _sgu_gate(svn, w_ref, bias):
    masks = _lane_group_masks(GROUP_W, 64)
    outs = []
    for n in range(TM // CHUNK):
        x = svn[n * CHUNK:(n + 1) * CHUNK]
        acc = bias
        for g in range(4):
            acc = acc + jnp.dot(w_ref[g].astype(BF16), (x * masks[g]).astype(BF16), preferred_element_type=F32)
        outs.append(acc)
    return jnp.concatenate(outs, axis=0)


def c_fwd(name, s, proj, g, b, w, bias):
    def fn(i, j, cin, g_ref, b_ref, w_ref, bias_ref, o_ref):
        c, _ = _gelu(cin)
        xh, _ = _ln_stats(c[:, GROUP_W:])
        svn = xh * g_ref[...] + b_ref[...]
        o_ref[...] = c[:, :GROUP_W] * _sgu_gate(svn, w_ref, bias_ref[...])

    ins = [('row', proj, 512, 2), ('full', g), ('full', b), ('full', w), ('full', bias)]
    return tiled_call(name, fn, s, TM, ins, [('row', GROUP_W, GROUP_W, 0, F32)])[0]


def c_bwd(name, s, proj, g, b, w, wt, bias, dmixcat, dproj):
    def fn(i, j, cin, do, g_ref, b_ref, w_ref, wt_ref, bias_ref, dc_ref, dg_ref, db_ref, dw_ref, dbias_ref):
        c, t = _gelu(cin)
        u = c[:, :GROUP_W]
        xh, r = _ln_stats(c[:, GROUP_W:])
        svn = xh * g_ref[...] + b_ref[...]
        gate = _sgu_gate(svn, w_ref, bias_ref[...])
        du = do * gate
        dgate = do * u
        masks = _lane_group_masks(GROUP_W, 64)
        col = lax.broadcasted_iota(jnp.int32, (CHUNK, 128), 1)
        dsvn, dbias = [], jnp.zeros((CHUNK, 128), F32)
        for n in range(TM // CHUNK):
            dgc = dgate[n * CHUNK:(n + 1) * CHUNK]
            x = svn[n * CHUNK:(n + 1) * CHUNK]
            acc = jnp.zeros((CHUNK, GROUP_W), F32)
            for gi in range(4):
                dgm = (dgc * masks[gi]).astype(BF16)
                acc = acc + jnp.dot(wt_ref[gi].astype(BF16), dgm, preferred_element_type=F32)
                dw_ref[gi] += _dot_nt(dgm, (x * masks[gi]).astype(BF16))
                dbias = dbias + jnp.where(col == gi, jnp.sum(dgc * masks[gi], axis=1, keepdims=True), 0.0)
            dsvn.append(acc)
        dbias_ref[...] += dbias
        dsv, dg, db = _ln_bwd(jnp.concatenate(dsvn, axis=0), xh, r, g_ref[...])
        dg_ref[...] += dg
        db_ref[...] += db
        dc_ref[...] = (jnp.concatenate([du, dsv], axis=1) * _gelu_grad(cin, t)).astype(BF16)

    ins = [('row', proj, 512, 2), ('row', dmixcat, GROUP_W, 2), ('full', g), ('full', b), ('full', w), ('full', wt), ('full', bias)]
    return tiled_call(name, fn, s, TM, ins, [('row_into', dproj, 512, 2), ('acc', (1, GROUP_W)), ('acc', (1, GROUP_W)),
                                             ('acc', (4, CHUNK, CHUNK)), ('acc', (CHUNK, 128))])


def _d_common(dpart, gq_ref, gkv_ref):
    cq, ckv, kr = dpart[:, :256], dpart[:, 256:384], dpart[:, 384:]
    rq = lax.rsqrt(jnp.sum(cq * cq, axis=1, keepdims=True) * (1.0 / MLA_Q_LORA) + RMS_EPS)
    rkv = lax.rsqrt(jnp.mean(ckv * ckv, axis=1, keepdims=True) + RMS_EPS)
    return cq, ckv, kr, rq, rkv, cq * rq * gq_ref[...], ckv * rkv * gkv_ref[...]


def d_prep_fwd(name, s, proj, gq, gkv, wuq, wukv, tm_, tkr):
    cm, sam, sbm = tm_
    ck, sak, sbk = tkr

    def fn(i, j, dpart, cmv, samv, sbmv, ckv_, sakv, sbkv, gq_ref, gkv_ref, wuq_ref, wukv_ref, q_ref, k_ref, v_ref):
        cq, ckv, kr, rq, rkv, cqn, ckvn = _d_common(dpart, gq_ref, gkv_ref)
        qd = jnp.dot(cqn.astype(BF16), wuq_ref[...], preferred_element_type=F32)
        qf = _rope(qd, cmv, samv, sbmv, 8)
        kvd = jnp.dot(ckvn.astype(BF16), wukv_ref[...], preferred_element_type=F32)
        krr = _rope(kr, ckv_, sakv, sbkv, 8)
        for h in range(4):
            q_ref[h] = (qf[:, h * 128:(h + 1) * 128] * QSCALE_MLA).astype(BF16)
            k_ref[h] = (kvd[:, h * 128:(h + 1) * 128] + krr).astype(BF16)
            v_ref[h] = kvd[:, 512 + h * 64:512 + (h + 1) * 64].astype(BF16)

    ins = [('row', proj, 512, 3), ('row', cm, 512, 0), ('row', sam, 512, 0), ('row', sbm, 512, 0), ('row', ck, 128, 0),
           ('row', sak, 128, 0), ('row', sbk, 128, 0), ('full', gq), ('full', gkv), ('full', wuq), ('full', wukv)]
    return tiled_call(name, fn, s, TM, ins, [('hm', 4, 128, BF16), ('hm', 4, 128, BF16), ('hm', 4, 64, BF16)])


def d_prep_bwd(name, s, proj, gq, gkv, wuq, wukv, tm_, tkr, dq_hm, dk_hm, dv_hm, dproj):
    cm, sam, sbm = tm_
    ck, sak, sbk = tkr

    def fn(i, j, dpart, cmv, samv, sbmv, ckv_, sakv, sbkv, gq_ref, gkv_ref, wuq_ref, wukv_ref, dq_ref, dk_ref, dv_ref,
           dd_ref, dgq_ref, dgkv_ref, dwuq_ref, dwukv_ref):
        cq, ckv, kr, rq, rkv, cqn, ckvn = _d_common(dpart, gq_ref, gkv_ref)
        dqf = jnp.concatenate([dq_ref[h] for h in range(4)], axis=1) * QSCALE_MLA
        dkf = [dk_ref[h] for h in range(4)]
        dqd = _rope_t(dqf, cmv, samv, sbmv, 8).astype(BF16)
        dkvd = jnp.concatenate(dkf + [dv_ref[h] for h in range(4)], axis=1).astype(BF16)
        dkr = _rope_t(dkf[0] + dkf[1] + dkf[2] + dkf[3], ckv_, sakv, sbkv, 8)
        dcqn = _dot_nt(dqd, wuq_ref[...])
        dckvn = _dot_nt(dkvd, wukv_ref[...])
        dwuq_ref[...] += _dot_tn(cqn.astype(BF16), dqd)
        dwukv_ref[...] += _dot_tn(ckvn.astype(BF16), dkvd)
        gqv, gkvv = gq_ref[...], gkv_ref[...]
        dcq = gqv * rq * dcqn - cq * (rq * rq * rq) * (jnp.sum(cq * gqv * dcqn, axis=1, keepdims=True) * (1.0 / MLA_Q_LORA))
        dckv = gkvv * rkv * dckvn - ckv * (rkv * rkv * rkv) * jnp.mean(ckv * gkvv * dckvn, axis=1, keepdims=True)
        dgq_ref[...] += jnp.sum(cq * rq * dcqn, axis=0, keepdims=True)
        dgkv_ref[...] += jnp.sum(ckv * rkv * dckvn, axis=0, keepdims=True)
        dd_ref[...] = jnp.concatenate([dcq, dckv, dkr], axis=1).astype(BF16)

    ins = [('row', proj, 512, 3), ('row', cm, 512, 0), ('row', sam, 512, 0), ('row', sbm, 512, 0), ('row', ck, 128, 0),
           ('row', sak, 128, 0), ('row', sbk, 128, 0), ('full', gq), ('full', gkv), ('full', wuq), ('full', wukv),
           ('hm', dq_hm), ('hm', dk_hm), ('hm', dv_hm)]
    return tiled_call(name, fn, s, TM, ins, [('row_into', dproj, 512, 3), ('acc', (1, 256)), ('acc', (1, 128)),
                                             ('acc', (256, 512)), ('acc', (128, 768))])


HALO_F = 8
TN_F = 1408
TM_F = 512
TM_FB, TN_FB = 512, 256


def _conv3(ext, w_ref):
    return _roll_r(ext, -1) * w_ref[pl.ds(0, 1), :] + ext * w_ref[pl.ds(1, 1), :] + _roll_r(ext, 1) * w_ref[pl.ds(2, 1), :]


def f_fwd(name, s, upraw, cw, cb):
    ncj = D_FF // TN_F

    def fn(i, j, xa, xg, wa_ref, wg_ref, ba_ref, bg_ref, o_ref):
        ua = (_conv3(xa, wa_ref) + ba_ref[...])[HALO_F:HALO_F + TM_F]
        ug = (_conv3(xg, wg_ref) + bg_ref[...])[HALO_F:HALO_F + TM_F]
        o_ref[...] = (ua * _sigmoid(ua) * ug).astype(BF16)

    ins = [('halo', upraw, TN_F, lambda j: j, HALO_F), ('halo', upraw, TN_F, lambda j: j + ncj, HALO_F),
           ('col', cw, TN_F, lambda j: j), ('col', cw, TN_F, lambda j: j + ncj),
           ('col', cb, TN_F, lambda j: j), ('col', cb, TN_F, lambda j: j + ncj)]
    return tiled_call(name, fn, s, TM_F, ins, [('row', D_FF, TN_F, lambda j: j, BF16)], ncol=ncj)[0]


def f_bwd(name, s, upraw, dact, cw, cb):
    ncj = D_FF // TN_FB

    def fn(i, j, xa, xg, da, wa_ref, wg_ref, ba_ref, bg_ref, dx_ref, dwa_ref, dwg_ref, dba_ref, dbg_ref):
        taps_a = (_roll_r(xa, -1), xa, _roll_r(xa, 1))
        taps_g = (_roll_r(xg, -1), xg, _roll_r(xg, 1))
        ua = sum(t * wa_ref[pl.ds(k, 1), :] for k, t in enumerate(taps_a)) + ba_ref[...]
        ug = sum(t * wg_ref[pl.ds(k, 1), :] for k, t in enumerate(taps_g)) + bg_ref[...]
        sg = _sigmoid(ua)
        dug = da * ua * sg
        dua = da * ug * sg * (1.0 + ua * (1.0 - sg))
        cen = slice(HALO_F, HALO_F + TM_FB)
        for half, (du, taps, w_ref, dw_ref, db_ref) in enumerate(((dua, taps_a, wa_ref, dwa_ref, dba_ref),
                                                                   (dug, taps_g, wg_ref, dwg_ref, dbg_ref))):
            dx = _roll_r(du, 1) * w_ref[pl.ds(0, 1), :] + du * w_ref[pl.ds(1, 1), :] + _roll_r(du, -1) * w_ref[pl.ds(2, 1), :]
            dx_ref[half] = dx[cen].astype(BF16)
            duc = du[cen]
            for k in range(3):
                dw_ref[pl.ds(k, 1), :] += jnp.sum(duc * taps[k][cen], axis=0, keepdims=True)
            db_ref[...] += jnp.sum(duc, axis=0, keepdims=True)

    tn = TN_FB
    ins = [('halo', upraw, tn, lambda j: j, HALO_F), ('halo', upraw, tn, lambda j: j + ncj, HALO_F),
           ('halo', dact, tn, lambda j: j, HALO_F),
           ('col', cw, tn, lambda j: j), ('col', cw, tn, lambda j: j + ncj),
           ('col', cb, tn, lambda j: j), ('col', cb, tn, lambda j: j + ncj)]
    outs = [('row3', 2, D_FF, tn, lambda j: j, BF16),
            ('colacc', 8, D_FF, tn, lambda j: j), ('colacc', 8, D_FF, tn, lambda j: j),
            ('colacc', 1, D_FF, tn, lambda j: j), ('colacc', 1, D_FF, tn, lambda j: j)]
    return tiled_call(name, fn, s, TM_FB, ins, outs, ncol=ncj)


def _pad_w_in(w):
    z = lambda n: jnp.zeros((w.shape[0], n), w.dtype)
    return jnp.concatenate([w[:, :1728], z(64), w[:, 1728:1856], z(64), w[:, 1856:1888], z(32)], axis=1)


def _unpad_w_in(g):
    return jnp.concatenate([g[:, :1728], g[:, 1792:1920], g[:, 1984:2016]], axis=1)


def _pad_uq(w):
    w = w.reshape(MLA_Q_LORA, 4, 96)
    w = jnp.concatenate([w, jnp.zeros((MLA_Q_LORA, 4, 32), w.dtype)], axis=2).reshape(MLA_Q_LORA, 512)
    return jnp.concatenate([w, jnp.zeros((64, 512), w.dtype)], axis=0)


def _unpad_uq(g):
    return g[:MLA_Q_LORA].reshape(MLA_Q_LORA, 4, 128)[:, :, :96].reshape(MLA_Q_LORA, 384)


def _pad_ukv(w):
    w = w.reshape(MLA_KV_LORA, 4, 128)
    kn = jnp.concatenate([w[:, :, :64], jnp.zeros((MLA_KV_LORA, 4, 64), w.dtype)], axis=2).reshape(MLA_KV_LORA, 512)
    return jnp.concatenate([kn, w[:, :, 64:].reshape(MLA_KV_LORA, 256)], axis=1)


def _unpad_ukv(g):
    kn = g[:, :512].reshape(MLA_KV_LORA, 4, 128)[:, :, :64]
    v = g[:, 512:].reshape(MLA_KV_LORA, 4, 64)
    return jnp.concatenate([kn, v], axis=2).reshape(MLA_KV_LORA, 512)


def _row(v, pad_to=None):
    v = v.reshape(1, -1)
    if pad_to is not None and v.shape[1] < pad_to:
        v = jnp.concatenate([v, jnp.zeros((1, pad_to - v.shape[1]), v.dtype)], axis=1)
    return v


LATE_NAMES = ['w_out', 'ffn_w_up', 'ffn_w_down']


def _assemble(n, part, g, me):
    return jnp.concatenate([jnp.where(me == q, part, g[q]) for q in range(N_CHIPS)], axis=SHARDED_AXIS[n])


def local_step(x, tgt, W, dist=None):
    s = x.shape[0]
    tabs = _rope_tables(s)
    gm = _group_mean_matrix(256, 64)
    alpha = DEEPNORM_ALPHA
    acts = []
    late = {(n, l): W[n][l] for n in LATE_NAMES for l in range(DEPTH)} if dist is None else {}
    h = ln_fwd("ln_in", s, x, _row(W['ln_in_g']), _row(W['ln_in_b']))
    for l in range(DEPTH):
        A = dict(h=h)
        A['wpad'] = _pad_w_in(W['w_in'][l]).astype(BF16)
        A['wuq'] = _pad_uq(W['mla_w_uq'][l]).astype(BF16)
        A['wukv'] = _pad_ukv(W['mla_w_ukv'][l]).astype(BF16)
        A['cw'] = jnp.concatenate([W['conv_a_w'][l], jnp.zeros((1, GROUP_W), F32)], axis=0)
        A['fcw'] = jnp.concatenate([W['ffn_conv_w'][l], jnp.zeros((5, 2 * D_FF), F32)], axis=0)
        A['gq'] = jnp.tile(_row(W['qk_norm_q'][l]), (1, 4))
        A['gk'] = jnp.tile(_row(W['qk_norm_k'][l]), (1, 2))
        A['sgu_bias'] = jnp.repeat(W['sgu_b'][l].T, 64, axis=1)
        A['sgu_wt'] = jnp.swapaxes(W['sgu_w'][l], 1, 2)
        proj = mm(f"proj{l}", h, A['wpad'])
        A['proj'] = proj
        A['y_a'], o_a = a_fwd(f"a_fwd{l}", s, proj, A['cw'], _row(W['conv_a_b'][l]), _row(W['ln_a_g'][l]), _row(W['ln_a_b'][l]))
        A['bq'], A['bk'], A['bv'] = b_prep_fwd(f"b_prep{l}", s, proj, A['gq'], A['gk'], tabs['gqa'], gm)
        if dist is not None and l == 0:
            lp = dist['late']
            parts = [lp['w_out'], lp['ffn_w_up'][0:1], lp['ffn_w_down'][0:1]]
            nxt_names, nxt = dist['next']
            nxt_parts = [a[1:2] for a in nxt]
            wanted = [lp['w_out'], (lp['ffn_w_up'], 0), (lp['ffn_w_down'], 0)] + [(a, 1) for a in nxt]
            (A['o_b'], A['lse_b']), got = attn_fwd(f"b_attn{l}", s, A['bq'], A['bk'], A['bv'], comm=gather_over_chips(wanted))
            full = [_assemble(n, p, g, dist['me']) for n, p, g in zip(LATE_NAMES, parts, got)]
            late[('w_out', 0)], late[('w_out', 1)] = full[0][0], full[0][1]
            late[('ffn_w_up', 0)], late[('ffn_w_down', 0)] = full[1][0], full[2][0]
            for n, p, g in zip(nxt_names, nxt_parts, got[len(parts):]):
                W[n][1] = _assemble(n, p, g, dist['me'])[0]
        else:
            A['o_b'], A['lse_b'] = attn_fwd(f"b_attn{l}", s, A['bq'], A['bk'], A['bv'])
        o_c = c_fwd(f"c_fwd{l}", s, proj, _row(W['sgu_ln_g'][l]), _row(W['sgu_ln_b'][l]), W['sgu_w'][l], A['sgu_bias'])
        A['dq'], A['dk'], A['dv'] = d_prep_fwd(f"d_prep{l}", s, proj, _row(W['mla_q_norm'][l], 256), _row(W['mla_kv_norm'][l]),
                                               A['wuq'], A['wukv'], tabs['mla'], tabs['kr'])
        if dist is not None and l == 0:
            parts = [lp['ffn_w_up'][1:2], lp['ffn_w_down'][1:2]]
            (A['o_d'], A['lse_d']), got = attn_fwd(f"d_attn{l}", s, A['dq'], A['dk'], A['dv'],
                                                   comm=gather_over_chips([(lp['ffn_w_up'], 1), (lp['ffn_w_down'], 1)]))
            late[('ffn_w_up', 1)] = _assemble('ffn_w_up', parts[0], got[0], dist['me'])[0]
            late[('ffn_w_down', 1)] = _assemble('ffn_w_down', parts[1], got[1], dist['me'])[0]
        else:
            A['o_d'], A['lse_d'] = attn_fwd(f"d_attn{l}", s, A['dq'], A['dk'], A['dv'])
        A['wout'] = late[('w_out', l)].astype(BF16)
        A['wup'] = late[('ffn_w_up', l)].astype(BF16)
        A['wdown'] = late[('ffn_w_down', l)].astype(BF16)
        A['mixcat'] = jnp.concatenate([o_a, A['o_b'], o_c, A['o_d']], axis=1).astype(BF16)
        A['mix'] = mm(f"out_proj{l}", A['mixcat'], A['wout'])
        h1 = ln_fwd(f"ln_mix{l}", s, A['mix'], _row(W['ln_mix_g'][l]), _row(W['ln_mix_b'][l]), res=h, alpha=alpha)
        A['h1'] = h1
        A['upraw'] = mm(f"ffn_up{l}", h1, A['wup'])
        A['act'] = f_fwd(f"f_fwd{l}", s, A['upraw'], A['fcw'], _row(W['ffn_conv_b'][l]))
        A['f'] = mm(f"ffn_down{l}", A['act'], A['wdown'])
        if l < DEPTH - 1:
            h = ln_fwd(f"ln_ffn{l}", s, A['f'], _row(W['ln_ffn_g'][l]), _row(W['ln_ffn_b'][l]), res=h1, alpha=alpha)
        acts.append(A)

    per_layer = {n: [None] * DEPTH for n in WEIGHT_NAMES if n not in ('ln_in_g', 'ln_in_b')}
    dh = None
    loss_row = None
    overlap = dist is not None and DEPTH == 2
    red = {}
    grp1 = [(n, 1) for n in BIG_NAMES]
    grp2 = [('ffn_w_up', 0), ('ffn_w_down', 0)]

    def carried(fn, name, *a, comm=None, **kw):
        if comm is None:
            return fn(name, *a, **kw), None
        return fn(name, *a, comm=comm, **kw)

    def pair_sums(tag, xs, r1):
        return [pair_sum(f"pair_sum_{tag}{t}", xv, rv, dist['idx'], BF16) for t, (xv, rv) in enumerate(zip(xs, r1))]

    def chip_sums(tag, cs, ps):
        return [chip_sum(f"chip_sum_{tag}{t}", [cv], [pv], dist['idx']) for t, (cv, pv) in enumerate(zip(cs, ps))]

    for l in reversed(range(DEPTH)):
        A = acts[l]
        carry = overlap and l == 0
        g_ffn, b_ffn = _row(W['ln_ffn_g'][l]), _row(W['ln_ffn_b'][l])
        if l == DEPTH - 1:
            dz, dg, db, loss_row = loss_and_ln_bwd("loss", s, A['f'], A['h1'], g_ffn, b_ffn, tgt, alpha)
        else:
            dz, dg, db = ln_bwd(f"ln_ffn_bwd{l}", s, dh, A['f'], g_ffn, res=A['h1'], alpha=alpha)
        per_layer['ln_ffn_g'][l], per_layer['ln_ffn_b'][l] = dg[0], db[0]
        xs1 = [per_layer[n][k] for n, k in grp1] if carry else None
        dwd, r1 = carried(mm, f"dw_down{l}", A['act'], dz, ta=True, comm=pair_exchange(xs1) if carry else None)
        per_layer['ffn_w_down'][l] = dwd.reshape(N_CHIPS, D_FF // N_CHIPS, D_MODEL)
        ps1 = pair_sums("g1_", xs1, r1) if carry else None
        dact = mm(f"d_act{l}", dz, A['wdown'], tb=True)
        dup, dwa, dwg, dba, dbg = f_bwd(f"f_bwd{l}", s, A['upraw'], dact, A['fcw'], _row(W['ffn_conv_b'][l]))
        per_layer['ffn_conv_w'][l] = jnp.concatenate([dwa[:3], dwg[:3]], axis=1)
        per_layer['ffn_conv_b'][l] = jnp.concatenate([dba[0], dbg[0]], axis=0)
        per_layer['ffn_w_up'][l] = mm(f"dw_up{l}", A['h1'], dup, ta=True, chip_major=True, b_pair=True)
        xs2 = [per_layer[n][k] for n, k in grp2] if carry else None
        dh1_mm, r2 = carried(mm, f"d_h1{l}", dup, A['wup'], tb=True, a_pair=True, comm=pair_exchange(xs2) if carry else None)
        ps2 = pair_sums("g2_", xs2, r2) if carry else None
        dz1, dg, db = _ln_mix_bwd(l, s, dz, dh1_mm, A, W, alpha)
        per_layer['ln_mix_g'][l], per_layer['ln_mix_b'][l] = dg[0], db[0]
        per_layer['w_out'][l] = mm(f"dw_out{l}", A['mixcat'], dz1, ta=True).reshape(N_CHIPS, D_MODEL // N_CHIPS, D_MODEL)
        dmixcat = mm(f"d_mixcat{l}", dz1, A['wout'], tb=True)
        dy_a, dg, db, dcb = a_bwd1(f"a_bwd1{l}", s, A['y_a'], dmixcat, _row(W['ln_a_g'][l]), _row(W['ln_a_b'][l]))
        per_layer['ln_a_g'][l], per_layer['ln_a_b'][l], per_layer['conv_a_b'][l] = dg[0], db[0], dcb[0]
        dproj, dcw = a_bwd2(f"a_bwd2{l}", s, A['proj'], dy_a, A['cw'])
        per_layer['conv_a_w'][l] = dcw[:CONV_A_WIDTH]
        (dq, dk, dv), cs1 = carried(attn_bwd, f"b_attn_bwd{l}", s, A['bq'], A['bk'], A['bv'], A['o_b'], A['lse_b'], dmixcat, 2,
                                    comm=chip_exchange(ps1) if carry else None)
        halves1 = chip_sums("g1_", cs1, ps1) if carry else None
        dproj, dgq, dgk = b_prep_bwd(f"b_prep_bwd{l}", s, A['proj'], A['gq'], A['gk'], tabs['gqa'], gm, dq, dk, dv, dproj)
        per_layer['qk_norm_q'][l], per_layer['qk_norm_k'][l] = dgq[0], dgk[0]
        dproj, dg, db, dsw, dsb = c_bwd(f"c_bwd{l}", s, A['proj'], _row(W['sgu_ln_g'][l]), _row(W['sgu_ln_b'][l]), W['sgu_w'][l],
                                        A['sgu_wt'], A['sgu_bias'], dmixcat, dproj)
        per_layer['sgu_ln_g'][l], per_layer['sgu_ln_b'][l] = dg[0], db[0]
        per_layer['sgu_w'][l], per_layer['sgu_b'][l] = dsw, dsb[:, :4].T
        comms = [half_exchange(halves1), chip_exchange(ps2)] if carry else []
        (dq, dk, dv), cres = carried(attn_bwd, f"d_attn_bwd{l}", s, A['dq'], A['dk'], A['dv'], A['o_d'], A['lse_d'], dmixcat, 6,
                                     comm=join_comms(comms))
        if carry:
            done1, cs2 = split_comm_results(comms, cres)
            red.update(zip(grp1, done1))
            halves2 = chip_sums("g2_", cs2, ps2)
        dproj, dgq, dgkv, dwuq, dwukv = d_prep_bwd(f"d_prep_bwd{l}", s, A['proj'], _row(W['mla_q_norm'][l], 256),
                                                   _row(W['mla_kv_norm'][l]), A['wuq'], A['wukv'], tabs['mla'], tabs['kr'],
                                                   dq, dk, dv, dproj)
        per_layer['mla_q_norm'][l], per_layer['mla_kv_norm'][l] = dgq[0, :MLA_Q_LORA], dgkv[0]
        per_layer['mla_w_uq'][l], per_layer['mla_w_ukv'][l] = _unpad_uq(dwuq), _unpad_ukv(dwukv)
        dw_in, done2 = carried(mm, f"dw_in{l}", A['h'], dproj, ta=True, comm=half_exchange(halves2) if carry else None)
        if carry:
            red.update(zip(grp2, done2))
        per_layer['w_in'][l] = _unpad_w_in(dw_in).reshape(D_MODEL, N_CHIPS, D_IN_PROJ // N_CHIPS).transpose(1, 0, 2)
        dh = mm(f"d_h{l}", dproj, A['wpad'], tb=True, add=dz1, add_scale=alpha)
    dx, dg, db = ln_bwd("ln_in_bwd", s, dh, x, _row(W['ln_in_g']))
    G = dict(per_layer)
    G['ln_in_g'], G['ln_in_b'] = dg[0], db[0]
    for n, k in red:
        G[n][k] = None
    G['reduced'] = red
    return loss_row, dx, G


def _ln_mix_bwd(l, s, dz, dh1_mm, A, W, alpha):
    ins = [('row', dz, D_MODEL, 0), ('row', dh1_mm, D_MODEL, 0), ('row', A['mix'], D_MODEL, 0), ('row', A['h'], D_MODEL, 0),
           ('full', _row(W['ln_mix_g'][l]))]

    def fn(i, j, dzv, dmv, xv, rv, g_ref, o_ref, dg_ref, db_ref):
        xh, r = _ln_stats(alpha * rv + xv)
        d, dg, db = _ln_bwd(alpha * dzv + dmv, xh, r, g_ref[...])
        o_ref[...] = d
        dg_ref[...] += dg
        db_ref[...] += db

    return tiled_call(f"ln_mix_bwd{l}", fn, s, TM, ins, [('row', D_MODEL, D_MODEL, 0, F32), ('acc', (1, D_MODEL)),
                                                         ('acc', (1, D_MODEL))])


ANY = pl.BlockSpec(memory_space=pl.ANY)


def _pos():
    return lax.axis_index("x"), lax.axis_index("y"), lax.axis_index("c")


def _rcopy(src, dst, ssem, rsem, dev):
    return pltpu.make_async_remote_copy(src_ref=src, dst_ref=dst, send_sem=ssem, recv_sem=rsem, device_id=dev,
                                        device_id_type=MESH)


class Comm:
    def __init__(self, operands, out_shape, sems, start, finish, aliases=None):
        self.operands, self.out_shape, self.sems = list(operands), list(out_shape), list(sems)
        self.start, self.finish, self.aliases = start, finish, dict(aliases or {})


def join_comms(comms):
    comms = [cm for cm in comms if cm is not None]
    if not comms:
        return None
    offs, oi, oo, os_ = [], 0, 0, 0
    for cm in comms:
        offs.append((oi, oo, os_))
        oi, oo, os_ = oi + len(cm.operands), oo + len(cm.out_shape), os_ + len(cm.sems)

    def part(fn_name):
        def run(ins, outs, sems):
            for cm, (a, b, s) in zip(comms, offs):
                getattr(cm, fn_name)(ins[a:a + len(cm.operands)], outs[b:b + len(cm.out_shape)], sems[s:s + len(cm.sems)])
        return run

    aliases = {a + i: b + o for cm, (a, b, s) in zip(comms, offs) for i, o in cm.aliases.items()}
    return Comm([v for cm in comms for v in cm.operands], [v for cm in comms for v in cm.out_shape],
                [v for cm in comms for v in cm.sems], part("start"), part("finish"), aliases)


def split_comm_results(comms, res):
    out, p = [], 0
    for cm in comms:
        if cm is None:
            out.append(None)
        else:
            out.append(list(res[p:p + len(cm.out_shape)]))
            p += len(cm.out_shape)
    return out


def run_comm(name, comm):
    n_in, n_out = len(comm.operands), len(comm.out_shape)

    def body(*refs):
        ins, outs, sems = refs[:n_in], refs[n_in:n_in + n_out], refs[n_in + n_out:]
        comm.start(ins, outs, sems)
        comm.finish(ins, outs, sems)

    return pl.pallas_call(body, name=name, in_specs=[ANY] * n_in, out_specs=[ANY] * n_out, out_shape=comm.out_shape,
                          scratch_shapes=[pltpu.SemaphoreType.DMA((k,)) for k in comm.sems],
                          input_output_aliases=comm.aliases)(*comm.operands)


def gather_over_chips(parts):
    n = len(parts)
    layer = [p[1] if isinstance(p, tuple) else None for p in parts]
    parts = [p[0] if isinstance(p, tuple) else p for p in parts]
    shapes = [p.shape if l is None else (1,) + p.shape[1:] for p, l in zip(parts, layer)]
    split = [shp[1] % 32 == 0 for shp in shapes]

    def geometry():
        x, y, c = _pos()
        return x, y, c, 2 * x + y, [(1 - x, y), (x, 1 - y), (1 - x, 1 - y)]

    def rows(t, half):
        a2 = shapes[t][1] // 2
        return pl.ds(half * a2, a2)

    def ici(srcs, dsts, sems, j, t, px, py, c, me):
        k = n * j + t
        src = srcs[t] if layer[t] is None else srcs[t].at[pl.ds(layer[t], 1)]
        if split[t]:
            return _rcopy(src.at[:, rows(t, c), :], dsts[t].at[me, :, rows(t, c), :], sems[0].at[k], sems[1].at[k], (px, py, c))
        return _rcopy(src, dsts[t].at[me], sems[0].at[k], sems[1].at[k], (px, py, c))

    def start(srcs, dsts, sems):
        x, y, c, me, chips = geometry()
        for j, (px, py) in enumerate(chips):
            for t in range(n):
                ici(srcs, dsts, sems, j, t, px, py, c, me).start()

    def finish(srcs, dsts, sems):
        x, y, c, me, chips = geometry()
        fwd = []
        for j, (px, py) in enumerate(chips):
            q = 2 * px + py
            for t in range(n):
                k = n * j + t
                if split[t]:
                    got = dsts[t].at[q, :, rows(t, c), :]
                    _rcopy(got, got, sems[0].at[k], sems[1].at[k], (px, py, c)).wait_recv()
                    fw = _rcopy(got, got, sems[2].at[k], sems[3].at[k], (x, y, 1 - c))
                    fw.start()
                    fwd.append(fw)
                else:
                    _rcopy(dsts[t].at[q], dsts[t].at[q], sems[0].at[k], sems[1].at[k], (px, py, c)).wait_recv()
        for j, (px, py) in enumerate(chips):
            q = 2 * px + py
            for t in range(n):
                if split[t]:
                    other = dsts[t].at[q, :, rows(t, 1 - c), :]
                    _rcopy(other, other, sems[2].at[n * j + t], sems[3].at[n * j + t], (x, y, 1 - c)).wait_recv()
                ici(srcs, dsts, sems, j, t, px, py, c, me).wait_send()
        for fw in fwd:
            fw.wait_send()

    return Comm(parts, [jax.ShapeDtypeStruct((N_CHIPS,) + shp, p.dtype) for p, shp in zip(parts, shapes)], [3 * n] * 4, start, finish)


def pair_exchange(xs):
    n = len(xs)

    def copies(srcs, dsts, sems):
        x, y, c = _pos()
        return [_rcopy(srcs[t].at[:, pl.ds((1 - c) * (xs[t].shape[1] // 2), xs[t].shape[1] // 2), :], dsts[t],
                       sems[0].at[t], sems[1].at[t], (x, y, 1 - c)) for t in range(n)]

    def start(srcs, dsts, sems):
        for cp in copies(srcs, dsts, sems):
            cp.start()

    def finish(srcs, dsts, sems):
        for cp in copies(srcs, dsts, sems):
            cp.wait()

    return Comm(xs, [jax.ShapeDtypeStruct((a.shape[0], a.shape[1] // 2, a.shape[2]), a.dtype) for a in xs], [n, n], start, finish)


def chip_exchange(ps):
    n = len(ps)

    def geometry():
        x, y, c = _pos()
        return c, 2 * x + y, [(1 - x, y), (x, 1 - y), (1 - x, 1 - y)]

    def start(srcs, dsts, sems):
        c, me, chips = geometry()
        for j, (px, py) in enumerate(chips):
            for t in range(n):
                _rcopy(srcs[t].at[2 * px + py], dsts[t].at[me], sems[0].at[n * j + t], sems[1].at[n * j + t], (px, py, c)).start()

    def finish(srcs, dsts, sems):
        c, me, chips = geometry()
        for j, (px, py) in enumerate(chips):
            for t in range(n):
                _rcopy(srcs[t].at[2 * px + py], dsts[t].at[2 * px + py], sems[0].at[n * j + t], sems[1].at[n * j + t], (px, py, c)).wait()

    return Comm(ps, [jax.ShapeDtypeStruct(p.shape, p.dtype) for p in ps], [3 * n, 3 * n], start, finish)


def half_exchange(bufs):
    n = len(bufs)

    def copies(ins, outs, sems):
        x, y, c = _pos()
        cps = []
        for t in range(n):
            a2 = bufs[t].shape[1] // 2
            mine = pl.ds(c * a2, a2)
            cps.append(_rcopy(ins[t].at[:, mine, :], outs[t].at[:, mine, :], sems[0].at[t], sems[1].at[t], (x, y, 1 - c)))
        return cps

    def start(ins, outs, sems):
        for cp in copies(ins, outs, sems):
            cp.start()

    def finish(ins, outs, sems):
        for cp in copies(ins, outs, sems):
            cp.wait()

    return Comm(bufs, [jax.ShapeDtypeStruct(b.shape, b.dtype) for b in bufs], [n, n], start, finish, {t: t for t in range(n)})


LANES = 1024
ROW_TILE_BYTES = 2 * 1024 * 1024


def _row_tile(r, row_bytes):
    if r * row_bytes <= ROW_TILE_BYTES:
        return r
    best = None
    for t in range(16, r, 16):
        if r % t == 0 and t * row_bytes <= ROW_TILE_BYTES:
            best = t
    assert best is not None, (r, row_bytes)
    return best


def pair_sum(name, x, r1, c_arr, out_dtype):
    q, a2, b = r1.shape
    tr = _row_tile(a2, b * 4)
    nb = a2 // tr

    def body(c_ref, x_ref, r_ref, o_ref):
        o_ref[...] = (x_ref[...] + r_ref[...]).astype(out_dtype)

    grid_spec = pltpu.PrefetchScalarGridSpec(
        num_scalar_prefetch=1, grid=(q, nb),
        in_specs=[pl.BlockSpec((None, tr, b), lambda k, i, c_ref: (k, c_ref[0] * nb + i, 0)),
                  pl.BlockSpec((None, tr, b), lambda k, i, c_ref: (k, i, 0))],
        out_specs=pl.BlockSpec((None, tr, b), lambda k, i, c_ref: (k, i, 0)))
    return pl.pallas_call(body, name=name, grid_spec=grid_spec, out_shape=jax.ShapeDtypeStruct(r1.shape, out_dtype),
                          compiler_params=_cparams(("parallel", "parallel")))(c_arr, x, r1)


def chip_sum(name, recv, own, idx):
    nl = len(recv)
    k, a2, b = recv[0].shape
    tr = _row_tile(a2, k * b * recv[0].dtype.itemsize)
    nb = a2 // tr

    def body(idx_ref, *refs):
        rs, os_, o_ref = refs[:nl], refs[nl:2 * nl], refs[2 * nl]
        l = pl.program_id(0)
        me = idx_ref[1]
        for li in range(nl):
            @pl.when(l == li)
            def _(li=li):
                acc = None
                for q in range(k):
                    term = jnp.where(me == q, os_[li][...], rs[li][q]).astype(F32)
                    acc = term if acc is None else acc + term
                o_ref[...] = acc

    def pick(li):
        return lambda l, i, idx_ref: jnp.where(l == li, i, 0)

    in_specs = [pl.BlockSpec((k, tr, b), lambda l, i, idx_ref, _p=pick(li): (0, _p(l, i, idx_ref), 0)) for li in range(nl)]
    in_specs += [pl.BlockSpec((None, tr, b), lambda l, i, idx_ref, _p=pick(li): (idx_ref[1], _p(l, i, idx_ref), 0)) for li in range(nl)]
    grid_spec = pltpu.PrefetchScalarGridSpec(
        num_scalar_prefetch=1, grid=(nl, nb), in_specs=in_specs,
        out_specs=pl.BlockSpec((None, tr, b), lambda l, i, idx_ref: (l, idx_ref[0] * nb + i, 0)))
    return pl.pallas_call(body, name=name, grid_spec=grid_spec, out_shape=jax.ShapeDtypeStruct((nl, 2 * a2, b), F32),
                          compiler_params=_cparams(("arbitrary", "arbitrary")))(idx, *recv, *own)


def adamw(name, gs, w, m, v):
    nl, r, b = w.shape
    assert len(gs) == nl
    tr = _row_tile(r, b * 4)

    def body(*refs):
        g_refs, (w_ref, m_ref, v_ref, go_ref, d_ref, nm_ref, nv_ref) = refs[:nl], refs[nl:]
        l = pl.program_id(0)
        for li in range(nl):
            @pl.when(l == li)
            def _(li=li):
                gv = g_refs[li][...]
                go_ref[...] = gv
                mn = ADAM_B1 * m_ref[...] + (1.0 - ADAM_B1) * gv
                vn = ADAM_B2 * v_ref[...] + (1.0 - ADAM_B2) * (gv * gv)
                m_hat = mn / (1.0 - ADAM_B1 ** ADAM_STEP)
                v_hat = vn / (1.0 - ADAM_B2 ** ADAM_STEP)
                d_ref[...] = -ADAM_LR * (m_hat / (jnp.sqrt(v_hat) + ADAM_EPS) + ADAM_WD * w_ref[...])
                nm_ref[...] = mn
                nv_ref[...] = vn

    g_specs = [pl.BlockSpec((None, tr, b), lambda l, i, _li=li: (0, jnp.where(l == _li, i, 0), 0)) for li in range(nl)]
    spec = pl.BlockSpec((None, tr, b), lambda l, i: (l, i, 0))
    sds = jax.ShapeDtypeStruct(w.shape, F32)
    return pl.pallas_call(body, name=name, grid=(nl, r // tr), in_specs=g_specs + [spec] * 3, out_specs=[spec] * 4,
                          out_shape=[sds] * 4, compiler_params=_cparams(("arbitrary", "arbitrary")))(*gs, w, m, v)


def adamw_many(name, gs, ws, ms, vs):
    n = len(gs)

    def body(*refs):
        for k in range(n):
            g_ref, w_ref, m_ref, v_ref, go_ref, d_ref, nm_ref, nv_ref = (refs[j * n + k] for j in range(8))
            gv = g_ref[...]
            go_ref[...] = gv
            mn = ADAM_B1 * m_ref[...] + (1.0 - ADAM_B1) * gv
            vn = ADAM_B2 * v_ref[...] + (1.0 - ADAM_B2) * (gv * gv)
            m_hat = mn / (1.0 - ADAM_B1 ** ADAM_STEP)
            v_hat = vn / (1.0 - ADAM_B2 ** ADAM_STEP)
            d_ref[...] = -ADAM_LR * (m_hat / (jnp.sqrt(v_hat) + ADAM_EPS) + ADAM_WD * w_ref[...])
            nm_ref[...] = mn
            nv_ref[...] = vn

    vm = pl.BlockSpec(memory_space=pltpu.VMEM)
    res = pl.pallas_call(body, name=name, in_specs=[vm] * (4 * n), out_specs=[vm] * (4 * n),
                         out_shape=[jax.ShapeDtypeStruct(a.shape, F32) for _ in range(4) for a in ws],
                         compiler_params=pltpu.CompilerParams(vmem_limit_bytes=VMEM_LIMIT))(*gs, *ws, *ms, *vs)
    return [list(res[j * n:(j + 1) * n]) for j in range(4)]


def _small_layout(shapes):
    views, windows, r0 = [], [], 0
    for shp in shapes:
        if len(shp) == 4:
            nblk = shp[0] * shp[1]
            assert shp[2:] == (CHUNK, 128) and nblk * 128 <= LANES
            views.append((nblk, CHUNK, 128))
            windows.append([((j,), r0, CHUNK, 128 * j, 128) for j in range(nblk)])
            r0 += CHUNK
            continue
        a, c = (1, shp[0]) if len(shp) == 1 else (int(np.prod(shp[:-1])), shp[-1])
        views.append((a, c))
        wins = []
        for j in range(-(-c // LANES)):
            w = min(LANES, c - j * LANES)
            wins.append(((slice(None), slice(j * LANES, j * LANES + w)), r0, a, 0, w))
            r0 += 8 * -(-a // 8)
        windows.append(wins)
    return 32 * -(-r0 // 32), views, windows


def pack_small(name, arrs, rows, windows):
    def body(*refs):
        ins, o_ref = refs[:-1], refs[-1]
        o_ref[...] = jnp.zeros_like(o_ref)
        for x_ref, wins in zip(ins, windows):
            for idx, r0, nr, l0, nl in wins:
                o_ref[r0:r0 + nr, l0:l0 + nl] = x_ref[idx]

    vm = pl.BlockSpec(memory_space=pltpu.VMEM)
    return pl.pallas_call(body, name=name, in_specs=[vm] * len(arrs), out_specs=vm,
                          out_shape=jax.ShapeDtypeStruct((rows, LANES), F32))(*arrs)


def adamw_windows(name, g_pack, windows, ws, ms, vs):
    n = len(ws)

    def body(g_ref, *refs):
        for k in range(n):
            w_ref, m_ref, v_ref, go_ref, d_ref, nm_ref, nv_ref = (refs[j * n + k] for j in range(7))
            for idx, r0, nr, l0, nl in windows[k]:
                gv = g_ref[r0:r0 + nr, l0:l0 + nl]
                go_ref[idx] = gv
                mn = ADAM_B1 * m_ref[idx] + (1.0 - ADAM_B1) * gv
                vn = ADAM_B2 * v_ref[idx] + (1.0 - ADAM_B2) * (gv * gv)
                m_hat = mn / (1.0 - ADAM_B1 ** ADAM_STEP)
                v_hat = vn / (1.0 - ADAM_B2 ** ADAM_STEP)
                d_ref[idx] = -ADAM_LR * (m_hat / (jnp.sqrt(v_hat) + ADAM_EPS) + ADAM_WD * w_ref[idx])
                nm_ref[idx] = mn
                nv_ref[idx] = vn

    vm = pl.BlockSpec(memory_space=pltpu.VMEM)
    res = pl.pallas_call(body, name=name, in_specs=[vm] * (1 + 3 * n), out_specs=[vm] * (4 * n),
                         out_shape=[jax.ShapeDtypeStruct(a.shape, F32) for _ in range(4) for a in ws],
                         compiler_params=pltpu.CompilerParams(vmem_limit_bytes=VMEM_LIMIT))(g_pack, *ws, *ms, *vs)
    return [list(res[j * n:(j + 1) * n]) for j in range(4)]


def _pack_rows(shapes):
    return [8 * -(-int(np.prod(s)) // (8 * LANES)) for s in shapes]


def _pack(arrs, rows):
    pieces = []
    for a, r in zip(arrs, _pack_rows([a.shape for a in arrs])):
        flat = a.reshape(-1)
        pieces.append(jnp.pad(flat, (0, r * LANES - flat.shape[0])).reshape(r, LANES))
    used = sum(p.shape[0] for p in pieces)
    assert used <= rows
    if used < rows:
        pieces.append(jnp.zeros((rows - used, LANES), F32))
    return jnp.concatenate(pieces, axis=0)


def _unpack(pack, shapes):
    out, off = [], 0
    for shp, r in zip(shapes, _pack_rows(shapes)):
        n = int(np.prod(shp))
        out.append(pack[off:off + r].reshape(-1)[:n].reshape(shp))
        off += r
    return out


BIG_NAMES = ['w_in', 'w_out', 'ffn_w_up', 'ffn_w_down']
MED_NAMES = [n for n in SHARDED_NAMES if n not in BIG_NAMES]
MED_ROWS = 96


def kernel(x, ln_in_g, ln_in_b, w_in, conv_a_w, conv_a_b, ln_a_g, ln_a_b, qk_norm_q, qk_norm_k, sgu_ln_g, sgu_ln_b, sgu_w, sgu_b, mla_q_norm, mla_w_uq, mla_kv_norm, mla_w_ukv, w_out, ln_mix_g, ln_mix_b, ffn_w_up, ffn_conv_w, ffn_conv_b, ffn_w_down, ln_ffn_g, ln_ffn_b, loss_target, m_ln_in_g, m_ln_in_b, m_w_in, m_conv_a_w, m_conv_a_b, m_ln_a_g, m_ln_a_b, m_qk_norm_q, m_qk_norm_k, m_sgu_ln_g, m_sgu_ln_b, m_sgu_w, m_sgu_b, m_mla_q_norm, m_mla_w_uq, m_mla_kv_norm, m_mla_w_ukv, m_w_out, m_ln_mix_g, m_ln_mix_b, m_ffn_w_up, m_ffn_conv_w, m_ffn_conv_b, m_ffn_w_down, m_ln_ffn_g, m_ln_ffn_b, v_ln_in_g, v_ln_in_b, v_w_in, v_conv_a_w, v_conv_a_b, v_ln_a_g, v_ln_a_b, v_qk_norm_q, v_qk_norm_k, v_sgu_ln_g, v_sgu_ln_b, v_sgu_w, v_sgu_b, v_mla_q_norm, v_mla_w_uq, v_mla_kv_norm, v_mla_w_ukv, v_w_out, v_ln_mix_g, v_ln_mix_b, v_ffn_w_up, v_ffn_conv_w, v_ffn_conv_b, v_ffn_w_down, v_ln_ffn_g, v_ln_ffn_b):
    loc = dict(locals())
    w_loc = {n: loc[n] for n in WEIGHT_NAMES}
    m_loc = {n: loc["m_" + n] for n in WEIGHT_NAMES}
    v_loc = {n: loc["v_" + n] for n in WEIGHT_NAMES}
    me_chip = 2 * lax.axis_index("x") + lax.axis_index("y")
    idx = jnp.stack([lax.axis_index("c"), me_chip]).astype(jnp.int32)

    def wire(n):
        return w_loc[n].astype(BF16) if n in BF16_WIRE else w_loc[n]

    assert DEPTH == 2
    early = [n for n in SHARDED_NAMES if n not in LATE_NAMES]
    wired = {n: wire(n) for n in SHARDED_NAMES}
    W = {n: w_loc[n] for n in REPL_NAMES}
    for n, g in zip(early, run_comm("gather_early", gather_over_chips([(wired[n], 0) for n in early]))):
        W[n] = [_assemble(n, wired[n][0:1], g, me_chip)[0], None]

    dist = dict(idx=idx, me=me_chip, late={n: wired[n] for n in LATE_NAMES}, next=(early, [wired[n] for n in early]))
    loss_row, dx, G = local_step(x[0], loss_target[0], W, dist)
    loss = lax.psum(loss_row[0, 0], ("x", "y", "c"))
    red = G['reduced']

    med = []
    for q in range(N_CHIPS):
        pieces = [lax.slice_in_dim(jnp.stack(G[n]), q * w_loc[n].shape[2], (q + 1) * w_loc[n].shape[2], axis=2) for n in MED_NAMES]
        med.append(_pack(pieces, MED_ROWS))
    rest = [(n, l) for n in BIG_NAMES for l in range(DEPTH) if (n, l) not in red]
    small_rows, views, windows = _small_layout([w_loc[n].shape for n in REPL_NAMES])
    small = [(jnp.stack(G[n]) if isinstance(G[n], list) else G[n]).reshape(v) for n, v in zip(REPL_NAMES, views)]
    small_pack = pack_small("pack_small", small, small_rows, windows)
    xs = [G[n][l] for n, l in rest] + [jnp.stack(med), jnp.broadcast_to(small_pack[None], (N_CHIPS, small_rows, LANES))]
    wire_dt = [BF16] * (len(xs) - 1) + [F32]
    r1 = run_comm("pair_exchange", pair_exchange(xs))
    ps = [pair_sum(f"pair_sum{t}", xv, rv, idx, dt) for t, (xv, rv, dt) in enumerate(zip(xs, r1, wire_dt))]
    cs = run_comm("chip_exchange", chip_exchange(ps))
    halves = [chip_sum(f"chip_sum{t}", [cv], [pv], idx) for t, (cv, pv) in enumerate(zip(cs, ps))]
    done = run_comm("half_exchange", half_exchange(halves))
    red.update(zip(rest, done[:-2]))
    g_med, g_small = done[-2], done[-1]

    outs = {k: {} for k in ("grad", "delta", "new_m", "new_v")}
    for n in BIG_NAMES:
        res = adamw(f"adamw_{n}", [red[(n, l)] for l in range(DEPTH)], w_loc[n], m_loc[n], v_loc[n])
        for k, v in zip(("grad", "delta", "new_m", "new_v"), res):
            outs[k][n] = v
    res = adamw_many("adamw_med", _unpack(g_med[0], [w_loc[n].shape for n in MED_NAMES]),
                     *[[d[n] for n in MED_NAMES] for d in (w_loc, m_loc, v_loc)])
    for k, vals in zip(("grad", "delta", "new_m", "new_v"), res):
        outs[k].update(zip(MED_NAMES, vals))
    res = adamw_windows("adamw_small", g_small[0], windows,
                        *[[d[n].reshape(v) for n, v in zip(REPL_NAMES, views)] for d in (w_loc, m_loc, v_loc)])
    for k, vals in zip(("grad", "delta", "new_m", "new_v"), res):
        outs[k].update({n: v.reshape(w_loc[n].shape) for n, v in zip(REPL_NAMES, vals)})
    return (loss, dx[None], *[outs["grad"][n] for n in WEIGHT_NAMES], *[outs["delta"][n] for n in WEIGHT_NAMES],
            *[outs["new_m"][n] for n in WEIGHT_NAMES], *[outs["new_v"][n] for n in WEIGHT_NAMES])
```

```python
import functools
import math

import jax
import jax.numpy as jnp
import numpy as np
from jax import lax
from jax.experimental import pallas as pl
from jax.experimental.pallas import tpu as pltpu

F32 = jnp.float32
BF16 = jnp.bfloat16

D_MODEL = 1024
DEPTH = 2
GRID_W = 64
GROUP_W = 256
HEAD_DIM = 64
CONV_A_WIDTH = 31
CHUNK = 128
MLA_Q_LORA = 192
MLA_KV_LORA = 128
MLA_NOPE = 64
MLA_ROPE = 32
D_FF = 2816
ROPE_THETA = 10000.0
DEEPNORM_ALPHA = (2 * DEPTH) ** 0.25
LN_EPS = 1e-5
RMS_EPS = 1e-6
D_IN_PROJ = 1888
PROJ_W = 2048

ADAM_LR = 0.001
ADAM_B1 = 0.9
ADAM_B2 = 0.999
ADAM_EPS = 1e-08
ADAM_WD = 0.01
ADAM_STEP = 10

VMEM_LIMIT = 56 * 1024 * 1024
MESH = pl.DeviceIdType.MESH
N_CHIPS = 4
N_DEV = 8

WEIGHT_NAMES = ['ln_in_g', 'ln_in_b', 'w_in', 'conv_a_w', 'conv_a_b', 'ln_a_g', 'ln_a_b', 'qk_norm_q', 'qk_norm_k',
                'sgu_ln_g', 'sgu_ln_b', 'sgu_w', 'sgu_b', 'mla_q_norm', 'mla_w_uq', 'mla_kv_norm', 'mla_w_ukv', 'w_out',
                'ln_mix_g', 'ln_mix_b', 'ffn_w_up', 'ffn_conv_w', 'ffn_conv_b', 'ffn_w_down', 'ln_ffn_g', 'ln_ffn_b']
SHARDED_AXIS = {'w_in': 2, 'conv_a_w': 2, 'mla_w_uq': 2, 'mla_w_ukv': 2, 'w_out': 1, 'ffn_w_up': 2, 'ffn_conv_w': 2,
                'ffn_w_down': 1}
SHARDED_NAMES = [n for n in WEIGHT_NAMES if n in SHARDED_AXIS]
REPL_NAMES = [n for n in WEIGHT_NAMES if n not in SHARDED_AXIS]
BF16_WIRE = ('w_in', 'mla_w_uq', 'mla_w_ukv', 'w_out', 'ffn_w_up', 'ffn_w_down')


def _cparams(sem):
    return pltpu.CompilerParams(dimension_semantics=sem, vmem_limit_bytes=VMEM_LIMIT)


def _pick(n, cands):
    for c in cands:
        if n % c == 0:
            return c
    return n


def carried_call(name, body, grid, in_specs, out_specs, out_shape, scratch, semantics, args, comm=None):
    if comm is None:
        res = pl.pallas_call(body, name=name, grid=grid, in_specs=in_specs, out_specs=out_specs, out_shape=out_shape,
                             scratch_shapes=scratch, compiler_params=_cparams(semantics))(*args)
        return list(res), None
    n_in, n_out, n_scr = len(in_specs), len(out_specs), len(scratch)
    nci, nco = len(comm.operands), len(comm.out_shape)

    def carried(*refs):
        o0 = n_in + nci
        s0 = o0 + n_out + nco
        ins, cins = refs[:n_in], refs[n_in:o0]
        outs, couts = refs[o0:o0 + n_out], refs[o0 + n_out:s0]
        scr, sems = refs[s0:s0 + n_scr], refs[s0 + n_scr:]
        ids = [pl.program_id(d) for d in range(len(grid))]
        first = functools.reduce(lambda u, v: u & v, [i == 0 for i in ids])
        last = functools.reduce(lambda u, v: u & v, [i == g - 1 for i, g in zip(ids, grid)])

        @pl.when(first)
        def _():
            comm.start(cins, couts, sems)

        body(*ins, *outs, *scr)

        @pl.when(last)
        def _():
            comm.finish(cins, couts, sems)

    res = pl.pallas_call(
        carried, name=name, grid=grid, in_specs=list(in_specs) + [ANY] * nci, out_specs=list(out_specs) + [ANY] * nco,
        out_shape=list(out_shape) + comm.out_shape, scratch_shapes=list(scratch) + [pltpu.SemaphoreType.DMA((k,)) for k in comm.sems],
        input_output_aliases={n_in + i: n_out + o for i, o in comm.aliases.items()},
        compiler_params=_cparams(("arbitrary",) * len(grid)))(*args, *comm.operands)
    return list(res[:n_out]), list(res[n_out:])


def mm(name, a, b, *, ta=False, tb=False, add=None, add_scale=1.0, out_dtype=F32, chip_major=False, comm=None,
       a_pair=False, b_pair=False):
    assert not (a_pair and ta) and not (b_pair and tb)
    a_shape = (a.shape[1], 2 * a.shape[2]) if a_pair else a.shape
    b_shape = (b.shape[1], 2 * b.shape[2]) if b_pair else b.shape
    m, k = (a_shape[1], a_shape[0]) if ta else a_shape
    n = b_shape[0] if tb else b_shape[1]
    assert (b_shape[1] if tb else b_shape[0]) == k
    tm = _pick(m, (1024, 1408, 512, 256, 128))
    tn = n // N_CHIPS if chip_major else _pick(n, (1024, 1408, 512, 256, 128))
    tk = _pick(k, (1024, 1408, 512, 256, 128))
    nk = k // tk
    a_spec = pl.BlockSpec((tk, tm), lambda i, j, kk: (kk, i)) if ta else pl.BlockSpec((tm, tk), lambda i, j, kk: (i, kk))
    b_spec = pl.BlockSpec((tn, tk), lambda i, j, kk: (j, kk)) if tb else pl.BlockSpec((tk, tn), lambda i, j, kk: (kk, j))
    if a_pair:
        per = a.shape[2] // tk
        assert per * tk == a.shape[2]
        a_spec = pl.BlockSpec((None, tm, tk), lambda i, j, kk: (kk // per, i, kk % per))
    if b_pair:
        per = b.shape[2] // tn
        assert per * tn == b.shape[2]
        b_spec = pl.BlockSpec((None, tk, tn), lambda i, j, kk: (j // per, kk, j % per))
    in_specs = [a_spec, b_spec]
    args = [a, b]
    if add is not None:
        in_specs.append(pl.BlockSpec((tm, tn), lambda i, j, kk: (i, j)))
        args.append(add)
    dims = (((0 if ta else 1,), (1 if tb else 0,)), ((), ()))

    def body(*refs):
        a_ref, b_ref = refs[0], refs[1]
        add_ref = refs[2] if add is not None else None
        o_ref, acc_ref = refs[-2], refs[-1]
        kk = pl.program_id(2)
        part = lax.dot_general(a_ref[...].astype(BF16), b_ref[...].astype(BF16), dims, preferred_element_type=F32)

        @pl.when(kk == 0)
        def _():
            acc_ref[...] = part

        @pl.when(kk > 0)
        def _():
            acc_ref[...] += part

        @pl.when(kk == nk - 1)
        def _():
            r = acc_ref[...]
            if add_ref is not None:
                r = r + add_scale * add_ref[...].astype(F32)
            o_ref[...] = r.astype(out_dtype)

    if chip_major:
        out_spec = pl.BlockSpec((None, tm, tn), lambda i, j, kk: (j, i, 0))
        out_shape = jax.ShapeDtypeStruct((N_CHIPS, m, tn), out_dtype)
    else:
        out_spec = pl.BlockSpec((tm, tn), lambda i, j, kk: (i, j))
        out_shape = jax.ShapeDtypeStruct((m, n), out_dtype)
    res, cres = carried_call(name, body, (m // tm, n // tn, nk), in_specs, [out_spec], [out_shape], [pltpu.VMEM((tm, tn), F32)],
                             ("parallel", "parallel", "arbitrary"), args, comm)
    return res[0] if comm is None else (res[0], cres)


def _cb(cb):
    return cb if callable(cb) else (lambda j, _c=cb: _c)


def tiled_call(name, fn, s, tm, ins, outs, ncol=1):
    nrow = s // tm
    in_specs, args, kinds = [], [], []
    for it in ins:
        kind = it[0]
        if kind == 'row':
            _, arr, w, cb = it
            in_specs.append(pl.BlockSpec((tm, w), lambda j, i, _c=_cb(cb): (i, _c(j))))
            args.append(arr)
            kinds.append(('row',))
        elif kind == 'halo':
            _, arr, w, cb, h = it
            r = tm // h
            nh = s // h
            in_specs.append(pl.BlockSpec((h, w), lambda j, i, _c=_cb(cb), _r=r: (jnp.maximum(i * _r - 1, 0), _c(j))))
            in_specs.append(pl.BlockSpec((tm, w), lambda j, i, _c=_cb(cb): (i, _c(j))))
            in_specs.append(pl.BlockSpec((h, w), lambda j, i, _c=_cb(cb), _r=r, _n=nh: (jnp.minimum((i + 1) * _r, _n - 1), _c(j))))
            args += [arr, arr, arr]
            kinds.append(('halo',))
        elif kind == 'full':
            arr = it[1]
            in_specs.append(pl.BlockSpec(arr.shape, lambda j, i, _n=arr.ndim: (0,) * _n))
            args.append(arr)
            kinds.append(('ref',))
        elif kind == 'col':
            _, arr, w, cb = it
            in_specs.append(pl.BlockSpec((arr.shape[0], w), lambda j, i, _c=_cb(cb): (0, _c(j))))
            args.append(arr)
            kinds.append(('ref',))
        elif kind == 'hm':
            arr = it[1]
            in_specs.append(pl.BlockSpec((arr.shape[0], tm, arr.shape[2]), lambda j, i: (0, i, 0)))
            args.append(arr)
            kinds.append(('ref',))
        else:
            raise ValueError(kind)
    out_specs, out_shapes, okinds, into = [], [], [], []
    for ot in outs:
        kind = ot[0]
        if kind == 'row':
            _, wt, w, cb, dt = ot
            out_specs.append(pl.BlockSpec((tm, w), lambda j, i, _c=_cb(cb): (i, _c(j))))
            out_shapes.append(jax.ShapeDtypeStruct((s, wt), dt))
            okinds.append('row')
        elif kind == 'hm':
            _, hh, d, dt = ot
            out_specs.append(pl.BlockSpec((hh, tm, d), lambda j, i: (0, i, 0)))
            out_shapes.append(jax.ShapeDtypeStruct((hh, s, d), dt))
            okinds.append('hm')
        elif kind == 'acc':
            shape = ot[1]
            out_specs.append(pl.BlockSpec(shape, lambda j, i, _n=len(shape): (0,) * _n))
            out_shapes.append(jax.ShapeDtypeStruct(shape, F32))
            okinds.append('acc')
        elif kind == 'colacc':
            _, r, wt, w, cb = ot
            out_specs.append(pl.BlockSpec((r, w), lambda j, i, _c=_cb(cb): (0, _c(j))))
            out_shapes.append(jax.ShapeDtypeStruct((r, wt), F32))
            okinds.append('colacc')
        elif kind == 'row_into':
            _, buf, w, cb = ot
            out_specs.append(pl.BlockSpec((tm, w), lambda j, i, _c=_cb(cb): (i, _c(j))))
            out_shapes.append(jax.ShapeDtypeStruct(buf.shape, buf.dtype))
            okinds.append('row')
            into.append((len(out_specs) - 1, buf))
        elif kind == 'row3':
            _, n0, wt, w, cb, dt = ot
            out_specs.append(pl.BlockSpec((n0, tm, w), lambda j, i, _c=_cb(cb): (0, i, _c(j))))
            out_shapes.append(jax.ShapeDtypeStruct((n0, s, wt), dt))
            okinds.append('row')
        else:
            raise ValueError(kind)
    n_in = len(in_specs)
    aliases = {n_in + e: o for e, (o, _) in enumerate(into)}
    in_specs = in_specs + [pl.BlockSpec(memory_space=pl.ANY)] * len(into)
    args = args + [buf for _, buf in into]

    def body(*refs):
        j = pl.program_id(0)
        i = pl.program_id(1)
        in_refs, out_refs = refs[:n_in], refs[n_in + len(into):]
        items, p = [], 0
        for kd in kinds:
            if kd[0] == 'row':
                items.append(in_refs[p][...])
                p += 1
            elif kd[0] == 'halo':
                prev, cen, nxt = in_refs[p][...], in_refs[p + 1][...], in_refs[p + 2][...]
                prev = jnp.where(i == 0, jnp.zeros_like(prev), prev)
                nxt = jnp.where(i == nrow - 1, jnp.zeros_like(nxt), nxt)
                items.append(jnp.concatenate([prev, cen, nxt], axis=0))
                p += 3
            else:
                items.append(in_refs[p])
                p += 1
        for o_ref, kd in zip(out_refs, okinds):
            if kd == 'acc':
                @pl.when((i == 0) & (j == 0))
                def _(o_ref=o_ref):
                    o_ref[...] = jnp.zeros_like(o_ref)
            elif kd == 'colacc':
                @pl.when(i == 0)
                def _(o_ref=o_ref):
                    o_ref[...] = jnp.zeros_like(o_ref)
        fn(i, j, *items, *out_refs)

    res = pl.pallas_call(
        body, name=name, grid=(ncol, nrow), in_specs=in_specs, out_specs=out_specs, out_shape=out_shapes,
        input_output_aliases=aliases, compiler_params=_cparams(("arbitrary", "arbitrary")),
    )(*args)
    return res


def _sigmoid(x):
    return 1.0 / (1.0 + jnp.exp(-x))


def _ln_stats(x):
    mu = jnp.mean(x, axis=1, keepdims=True)
    xc = x - mu
    var = jnp.mean(xc * xc, axis=1, keepdims=True)
    r = lax.rsqrt(var + LN_EPS)
    return xc * r, r


def _ln_bwd(dy, xh, r, g):
    dxh = dy * g
    dx = r * (dxh - jnp.mean(dxh, axis=1, keepdims=True) - xh * jnp.mean(dxh * xh, axis=1, keepdims=True))
    return dx, jnp.sum(dy * xh, axis=0, keepdims=True), jnp.sum(dy, axis=0, keepdims=True)


def _gmean(v, gm):
    hi = v.astype(BF16)
    lo = (v - hi.astype(F32)).astype(BF16)
    return jnp.dot(hi, gm, preferred_element_type=F32) + jnp.dot(lo, gm, preferred_element_type=F32)


def _roll_l(x, sh):
    return pltpu.roll(x, sh % x.shape[1], 1)


def _rope(x, c, sa, sb, sh):
    return x * c + _roll_l(x, -sh) * sa + _roll_l(x, sh) * sb


def _rope_t(dy, c, sa, sb, sh):
    return dy * c + _roll_l(dy * sa, sh) + _roll_l(dy * sb, -sh)


def _roll_r(x, s):
    return pltpu.roll(x, (-s) % x.shape[0], 0)


_GELU_C = math.sqrt(2.0 / math.pi)


def _gelu(x):
    t = jnp.tanh(_GELU_C * (x + 0.044715 * x * x * x))
    return 0.5 * x * (1.0 + t), t


def _gelu_grad(x, t):
    return 0.5 * (1.0 + t) + 0.5 * x * (1.0 - t * t) * _GELU_C * (1.0 + 3.0 * 0.044715 * x * x)


def _dot_nt(a, b):
    return lax.dot_general(a, b, (((1,), (1,)), ((), ())), preferred_element_type=F32)


def _dot_tn(a, b):
    return lax.dot_general(a, b, (((0,), (0,)), ((), ())), preferred_element_type=F32)


def _rope_block(s, d):
    t = jnp.arange(s)
    row = (t // GRID_W).astype(F32)
    col = (t % GRID_W).astype(F32)
    half = d // 4
    inv = ROPE_THETA ** (-jnp.arange(half, dtype=F32) / half)
    z = jnp.zeros((s, half), F32)
    cs, sas, sbs = [], [], []
    for pos in (row, col):
        ang = pos[:, None] * inv[None, :]
        co, si = jnp.cos(ang), jnp.sin(ang)
        cs += [co, co]
        sas += [-si, z]
        sbs += [z, si]
    return tuple(jnp.concatenate(v, axis=1) for v in (cs, sas, sbs))


def _rope_tables(s):
    gqa = tuple(jnp.tile(a, (1, 4)) for a in _rope_block(s, HEAD_DIM))
    c32, sa32, sb32 = _rope_block(s, MLA_ROPE)

    def head128(c, fill):
        return jnp.concatenate([jnp.full((s, 64), fill, F32), c, jnp.zeros((s, 32), F32)], axis=1)

    mla = tuple(jnp.tile(a, (1, 4)) for a in (head128(c32, 1.0), head128(sa32, 0.0), head128(sb32, 0.0)))
    kr = (head128(c32, 0.0), head128(sa32, 0.0), head128(sb32, 0.0))
    return dict(gqa=gqa, mla=mla, kr=kr)


def _group_mean_matrix(w, g):
    idx = np.arange(w) // g
    return jnp.asarray((idx[:, None] == idx[None, :]).astype(np.float32) / g, dtype=BF16)


TM = 512


def ln_fwd(name, s, x, g, b, res=None, alpha=1.0):
    ins = [('row', x, D_MODEL, 0)] + ([('row', res, D_MODEL, 0)] if res is not None else []) + [('full', g), ('full', b)]

    def fn(i, j, *a):
        if res is not None:
            xv, rv, g_ref, b_ref, o_ref = a
            z = alpha * rv + xv
        else:
            xv, g_ref, b_ref, o_ref = a
            z = xv
        xh, _ = _ln_stats(z)
        o_ref[...] = xh * g_ref[...] + b_ref[...]

    return tiled_call(name, fn, s, TM, ins, [('row', D_MODEL, D_MODEL, 0, F32)])[0]


def ln_bwd(name, s, dy, x, g, res=None, alpha=1.0):
    ins = [('row', dy, D_MODEL, 0), ('row', x, D_MODEL, 0)]
    ins += ([('row', res, D_MODEL, 0)] if res is not None else []) + [('full', g)]

    def fn(i, j, *a):
        a = list(a)
        dyv = a.pop(0)
        xv = a.pop(0)
        z = alpha * a.pop(0) + xv if res is not None else xv
        g_ref, dz_ref, dg_ref, db_ref = a
        xh, r = _ln_stats(z)
        dz, dg, db = _ln_bwd(dyv, xh, r, g_ref[...])
        dz_ref[...] = dz
        dg_ref[...] += dg
        db_ref[...] += db

    return tiled_call(name, fn, s, TM, ins, [('row', D_MODEL, D_MODEL, 0, F32), ('acc', (1, D_MODEL)), ('acc', (1, D_MODEL))])


def loss_and_ln_bwd(name, s, x, res, g, b, tgt, alpha):
    ins = [('row', x, D_MODEL, 0), ('row', res, D_MODEL, 0), ('row', tgt, D_MODEL, 0), ('full', g), ('full', b)]

    def fn(i, j, xv, rv, tv, g_ref, b_ref, dz_ref, dg_ref, db_ref, loss_ref):
        xh, r = _ln_stats(alpha * rv + xv)
        y = xh * g_ref[...] + b_ref[...]
        e = y - tv
        dz, dg, db = _ln_bwd(e * (1.0 / D_MODEL), xh, r, g_ref[...])
        dz_ref[...] = dz
        dg_ref[...] += dg
        db_ref[...] += db
        loss_ref[...] += jnp.sum(jnp.sum(e * e, axis=0, keepdims=True), axis=1, keepdims=True) * (0.5 / D_MODEL)

    return tiled_call(name, fn, s, TM, ins, [('row', D_MODEL, D_MODEL, 0, F32), ('acc', (1, D_MODEL)), ('acc', (1, D_MODEL)),
                                             ('acc', (1, 128))])


HALO_A = 16


def _glu(a_in):
    return a_in[:, :GROUP_W] * _sigmoid(a_in[:, GROUP_W:])


def _row_windows(x, halo, tm):
    rolled = {0: x}

    def window(s):
        b = s % 8
        if b not in rolled:
            rolled[b] = _roll_r(x, b)
        off = halo + s - b
        return rolled[b][off:off + tm]

    return window


def a_fwd(name, s, proj, cw, cb, g, b):
    def fn(i, j, a_ext, cw_ref, cb_ref, g_ref, b_ref, y_ref, o_ref):
        a_at = _row_windows(_glu(a_ext), HALO_A, TM)
        acc = jnp.zeros((TM, GROUP_W), F32)
        for k in range(CONV_A_WIDTH):
            acc = acc + a_at(k - 15) * cw_ref[pl.ds(k, 1), :]
        y = acc + cb_ref[...]
        y_ref[...] = y
        xh, _ = _ln_stats(y)
        z = xh * g_ref[...] + b_ref[...]
        o_ref[...] = z * _sigmoid(z)

    ins = [('halo', proj, 2 * GROUP_W, 0, HALO_A), ('full', cw), ('full', cb), ('full', g), ('full', b)]
    return tiled_call(name, fn, s, TM, ins, [('row', GROUP_W, GROUP_W, 0, F32), ('row', GROUP_W, GROUP_W, 0, F32)])


def a_bwd1(name, s, y, dmix, g, b):
    def fn(i, j, yv, do, g_ref, b_ref, dy_ref, dg_ref, db_ref, dcb_ref):
        xh, r = _ln_stats(yv)
        z = xh * g_ref[...] + b_ref[...]
        sg = _sigmoid(z)
        dz = do * sg * (1.0 + z * (1.0 - sg))
        dy, dg, db = _ln_bwd(dz, xh, r, g_ref[...])
        dy_ref[...] = dy
        dg_ref[...] += dg
        db_ref[...] += db
        dcb_ref[...] += jnp.sum(dy, axis=0, keepdims=True)

    ins = [('row', y, GROUP_W, 0), ('row', dmix, GROUP_W, 0), ('full', g), ('full', b)]
    return tiled_call(name, fn, s, TM, ins, [('row', GROUP_W, GROUP_W, 0, F32), ('acc', (1, GROUP_W)), ('acc', (1, GROUP_W)),
                                             ('acc', (1, GROUP_W))])


def a_bwd2(name, s, proj, dy, cw):
    def fn(i, j, a_ext, dy_ext, cw_ref, da_ref, dcw_ref):
        a1, a2 = a_ext[:, :GROUP_W], a_ext[:, GROUP_W:]
        sg = _sigmoid(a2)
        a = a1 * sg
        dyc = dy_ext[HALO_A:HALO_A + TM]
        dy_at, a_at = _row_windows(dy_ext, HALO_A, TM), _row_windows(a, HALO_A, TM)
        da = jnp.zeros((TM, GROUP_W), F32)
        for k in range(CONV_A_WIDTH):
            da = da + dy_at(15 - k) * cw_ref[pl.ds(k, 1), :]
            dcw_ref[pl.ds(k, 1), :] += jnp.sum(dyc * a_at(k - 15), axis=0, keepdims=True)
        a1c, sgc = a1[HALO_A:HALO_A + TM], sg[HALO_A:HALO_A + TM]
        da_ref[...] = jnp.concatenate([da * sgc, da * a1c * sgc * (1.0 - sgc)], axis=1).astype(BF16)

    ins = [('halo', proj, 2 * GROUP_W, 0, HALO_A), ('halo', dy, GROUP_W, 0, HALO_A), ('full', cw)]
    return tiled_call(name, fn, s, TM, ins, [('row', PROJ_W, 2 * GROUP_W, 0, BF16), ('acc', (32, GROUP_W))])


def b_prep_fwd(name, s, proj, gq, gk, tabs, gm):
    c, sa, sb = tabs

    def fn(i, j, qkv, cv, sav, sbv, gq_ref, gk_ref, gm_ref, q_ref, k_ref, v_ref):
        q, k, v = qkv[:, :256], qkv[:, 256:384], qkv[:, 384:]
        gmv = gm_ref[...]
        qn = q * lax.rsqrt(_gmean(q * q, gmv) + RMS_EPS) * gq_ref[...]
        kn = k * lax.rsqrt(_gmean(k * k, gmv[:128, :128]) + RMS_EPS) * gk_ref[...]
        qr = _rope(qn, cv, sav, sbv, 16)
        kr = _rope(kn, cv[:, :128], sav[:, :128], sbv[:, :128], 16)
        for h in range(4):
            q_ref[h] = (qr[:, h * 64:(h + 1) * 64] * QSCALE_GQA).astype(BF16)
        for h in range(2):
            k_ref[h] = kr[:, h * 64:(h + 1) * 64].astype(BF16)
            v_ref[h] = v[:, h * 64:(h + 1) * 64].astype(BF16)

    ins = [('row', proj, 512, 1), ('row', c, 256, 0), ('row', sa, 256, 0), ('row', sb, 256, 0), ('full', gq), ('full', gk),
           ('full', gm)]
    return tiled_call(name, fn, s, TM, ins, [('hm', 4, 64, BF16), ('hm', 2, 64, BF16), ('hm', 2, 64, BF16)])


def b_prep_bwd(name, s, proj, gq, gk, tabs, gm, dq_hm, dk_hm, dv_hm, dproj):
    c, sa, sb = tabs

    def fn(i, j, qkv, cv, sav, sbv, gq_ref, gk_ref, gm_ref, dq_ref, dk_ref, dv_ref, dp_ref, dgq_ref, dgk_ref):
        q, k = qkv[:, :256], qkv[:, 256:384]
        gmv = gm_ref[...]
        dqr = jnp.concatenate([dq_ref[h] for h in range(4)], axis=1) * QSCALE_GQA
        dkr = jnp.concatenate([dk_ref[h] for h in range(2)], axis=1)
        dv = jnp.concatenate([dv_ref[h] for h in range(2)], axis=1)
        outs = []
        for x, dxr, g_ref, gmx, w, dg_ref in ((q, dqr, gq_ref, gmv, 256, dgq_ref), (k, dkr, gk_ref, gmv[:128, :128], 128, dgk_ref)):
            dn = _rope_t(dxr, cv[:, :w], sav[:, :w], sbv[:, :w], 16)
            r = lax.rsqrt(_gmean(x * x, gmx) + RMS_EPS)
            gv = g_ref[...]
            dx = gv * r * dn - x * (r * r * r) * _gmean(x * gv * dn, gmx)
            dgt = jnp.sum(x * r * dn, axis=0, keepdims=True)
            dg = dgt[:, 0:64]
            for h in range(1, w // 64):
                dg = dg + dgt[:, h * 64:(h + 1) * 64]
            dg_ref[...] += dg
            outs.append(dx)
        dp_ref[...] = jnp.concatenate(outs + [dv], axis=1).astype(BF16)

    ins = [('row', proj, 512, 1), ('row', c, 256, 0), ('row', sa, 256, 0), ('row', sb, 256, 0), ('full', gq), ('full', gk),
           ('full', gm), ('hm', dq_hm), ('hm', dk_hm), ('hm', dv_hm)]
    return tiled_call(name, fn, s, TM, ins, [('row_into', dproj, 512, 1), ('acc', (1, 64)), ('acc', (1, 64))])


TQ_BWD = 512
TQ_FWD = 256


LOG2E = 1.4426950408889634
QSCALE_GQA = HEAD_DIM ** -0.5 * LOG2E
QSCALE_MLA = (MLA_NOPE + MLA_ROPE) ** -0.5 * LOG2E


def attn_fwd(name, s, q_hm, k_hm, v_hm, comm=None):
    dk = q_hm.shape[2]
    nkv = k_hm.shape[0]
    tq = TQ_FWD

    def body(q_ref, k_ref, v_ref, o_ref, l_ref):
        outs = []
        for g in range(4):
            kv = g * nkv // 4
            sc = _dot_nt(q_ref[g], k_ref[kv])
            m = jnp.max(sc, axis=1, keepdims=True)
            p = jnp.exp2(sc - m)
            l = jnp.sum(p, axis=1, keepdims=True)
            o = jnp.dot(p.astype(BF16), v_ref[kv], preferred_element_type=F32)
            outs.append(o / l)
            l_ref[g] = m + jnp.log2(l)
        o_ref[...] = jnp.concatenate(outs, axis=1)

    res, cres = carried_call(
        name, body, (s // tq,),
        [pl.BlockSpec((4, tq, dk), lambda i: (0, i, 0)),
         pl.BlockSpec((nkv, s, dk), lambda i: (0, 0, 0)),
         pl.BlockSpec((nkv, s, 64), lambda i: (0, 0, 0))],
        [pl.BlockSpec((tq, 256), lambda i: (i, 0)), pl.BlockSpec((4, tq, 1), lambda i: (0, i, 0))],
        [jax.ShapeDtypeStruct((s, 256), F32), jax.ShapeDtypeStruct((4, s, 1), F32)], [],
        ("arbitrary",), [q_hm, k_hm, v_hm], comm)
    return res if comm is None else (res, cres)


def attn_bwd(name, s, q_hm, k_hm, v_hm, o, lse, dmixcat, dcol, comm=None):
    dk = q_hm.shape[2]
    nkv = k_hm.shape[0]
    kpb = nkv // 2

    def body(q_ref, k_ref, v_ref, o_ref, l_ref, do_ref, dq_ref, dk_ref, dv_ref):
        @pl.when(pl.program_id(1) == 0)
        def _():
            dk_ref[...] = jnp.zeros_like(dk_ref)
            dv_ref[...] = jnp.zeros_like(dv_ref)

        dob, ob = do_ref[...], o_ref[...]
        for g in range(2):
            kv = g if kpb == 2 else 0
            do, ov = dob[:, 64 * g:64 * (g + 1)], ob[:, 64 * g:64 * (g + 1)]
            q, k, v = q_ref[g], k_ref[kv], v_ref[kv]
            p = jnp.exp2(_dot_nt(q, k) - l_ref[g])
            dp = _dot_nt(do.astype(BF16), v)
            delta = jnp.sum(do * ov, axis=1, keepdims=True)
            ds = (p * (dp - delta) * (1.0 / LOG2E)).astype(BF16)
            dq_ref[g] = jnp.dot(ds, k, preferred_element_type=F32)
            dk_ref[kv] += _dot_tn(ds, q)
            dv_ref[kv] += _dot_tn(p.astype(BF16), do.astype(BF16))

    tq = TQ_BWD if s % TQ_BWD == 0 else TQ_FWD
    res, cres = carried_call(
        name, body, (2, s // tq),
        [pl.BlockSpec((2, tq, dk), lambda p, i: (p, i, 0)),
         pl.BlockSpec((kpb, s, dk), lambda p, i: (p, 0, 0)), pl.BlockSpec((kpb, s, 64), lambda p, i: (p, 0, 0)),
         pl.BlockSpec((tq, 128), lambda p, i: (i, p)),
         pl.BlockSpec((2, tq, 1), lambda p, i: (p, i, 0)),
         pl.BlockSpec((tq, 128), lambda p, i: (i, dcol + p))],
        [pl.BlockSpec((2, tq, dk), lambda p, i: (p, i, 0)),
         pl.BlockSpec((kpb, s, dk), lambda p, i: (p, 0, 0)), pl.BlockSpec((kpb, s, 64), lambda p, i: (p, 0, 0))],
        [jax.ShapeDtypeStruct((4, s, dk), F32), jax.ShapeDtypeStruct((nkv, s, dk), F32), jax.ShapeDtypeStruct((nkv, s, 64), F32)],
        [], ("arbitrary", "arbitrary"), [q_hm, k_hm, v_hm, o, lse, dmixcat], comm)
    return res if comm is None else (res, cres)


def _lane_group_masks(w, g):
    lane = lax.broadcasted_iota(jnp.int32, (1, w), 1)
    return [((lane >= k * g) & (lane < (k + 1) * g)).astype(F32) for k in range(w // g)]


def _sgu_gate(svn, w_ref, bias):
    masks = _lane_group_masks(GROUP_W, 64)
    outs = []
    for n in range(TM // CHUNK):
        x = svn[n * CHUNK:(n + 1) * CHUNK]
        acc = bias
        for g in range(4):
            acc = acc + jnp.dot(w_ref[g].astype(BF16), (x * masks[g]).astype(BF16), preferred_element_type=F32)
        outs.append(acc)
    return jnp.concatenate(outs, axis=0)


def c_fwd(name, s, proj, g, b, w, bias):
    def fn(i, j, cin, g_ref, b_ref, w_ref, bias_ref, o_ref):
        c, _ = _gelu(cin)
        xh, _ = _ln_stats(c[:, GROUP_W:])
        svn = xh * g_ref[...] + b_ref[...]
        o_ref[...] = c[:, :GROUP_W] * _sgu_gate(svn, w_ref, bias_ref[...])

    ins = [('row', proj, 512, 2), ('full', g), ('full', b), ('full', w), ('full', bias)]
    return tiled_call(name, fn, s, TM, ins, [('row', GROUP_W, GROUP_W, 0, F32)])[0]


def c_bwd(name, s, proj, g, b, w, wt, bias, dmixcat, dproj):
    def fn(i, j, cin, do, g_ref, b_ref, w_ref, wt_ref, bias_ref, dc_ref, dg_ref, db_ref, dw_ref, dbias_ref):
        c, t = _gelu(cin)
        u = c[:, :GROUP_W]
        xh, r = _ln_stats(c[:, GROUP_W:])
        svn = xh * g_ref[...] + b_ref[...]
        gate = _sgu_gate(svn, w_ref, bias_ref[...])
        du = do * gate
        dgate = do * u
        masks = _lane_group_masks(GROUP_W, 64)
        col = lax.broadcasted_iota(jnp.int32, (CHUNK, 128), 1)
        dsvn, dbias = [], jnp.zeros((CHUNK, 128), F32)
        for n in range(TM // CHUNK):
            dgc = dgate[n * CHUNK:(n + 1) * CHUNK]
            x = svn[n * CHUNK:(n + 1) * CHUNK]
            acc = jnp.zeros((CHUNK, GROUP_W), F32)
            for gi in range(4):
                dgm = (dgc * masks[gi]).astype(BF16)
                acc = acc + jnp.dot(wt_ref[gi].astype(BF16), dgm, preferred_element_type=F32)
                dw_ref[gi] += _dot_nt(dgm, (x * masks[gi]).astype(BF16))
                dbias = dbias + jnp.where(col == gi, jnp.sum(dgc * masks[gi], axis=1, keepdims=True), 0.0)
            dsvn.append(acc)
        dbias_ref[...] += dbias
        dsv, dg, db = _ln_bwd(jnp.concatenate(dsvn, axis=0), xh, r, g_ref[...])
        dg_ref[...] += dg
        db_ref[...] += db
        dc_ref[...] = (jnp.concatenate([du, dsv], axis=1) * _gelu_grad(cin, t)).astype(BF16)

    ins = [('row', proj, 512, 2), ('row', dmixcat, GROUP_W, 2), ('full', g), ('full', b), ('full', w), ('full', wt), ('full', bias)]
    return tiled_call(name, fn, s, TM, ins, [('row_into', dproj, 512, 2), ('acc', (1, GROUP_W)), ('acc', (1, GROUP_W)),
                                             ('acc', (4, CHUNK, CHUNK)), ('acc', (CHUNK, 128))])


def _d_common(dpart, gq_ref, gkv_ref):
    cq, ckv, kr = dpart[:, :256], dpart[:, 256:384], dpart[:, 384:]
    rq = lax.rsqrt(jnp.sum(cq * cq, axis=1, keepdims=True) * (1.0 / MLA_Q_LORA) + RMS_EPS)
    rkv = lax.rsqrt(jnp.mean(ckv * ckv, axis=1, keepdims=True) + RMS_EPS)
    return cq, ckv, kr, rq, rkv, cq * rq * gq_ref[...], ckv * rkv * gkv_ref[...]


def d_prep_fwd(name, s, proj, gq, gkv, wuq, wukv, tm_, tkr):
    cm, sam, sbm = tm_
    ck, sak, sbk = tkr

    def fn(i, j, dpart, cmv, samv, sbmv, ckv_, sakv, sbkv, gq_ref, gkv_ref, wuq_ref, wukv_ref, q_ref, k_ref, v_ref):
        cq, ckv, kr, rq, rkv, cqn, ckvn = _d_common(dpart, gq_ref, gkv_ref)
        qd = jnp.dot(cqn.astype(BF16), wuq_ref[...], preferred_element_type=F32)
        qf = _rope(qd, cmv, samv, sbmv, 8)
        kvd = jnp.dot(ckvn.astype(BF16), wukv_ref[...], preferred_element_type=F32)
        krr = _rope(kr, ckv_, sakv, sbkv, 8)
        for h in range(4):
            q_ref[h] = (qf[:, h * 128:(h + 1) * 128] * QSCALE_MLA).astype(BF16)
            k_ref[h] = (kvd[:, h * 128:(h + 1) * 128] + krr).astype(BF16)
            v_ref[h] = kvd[:, 512 + h * 64:512 + (h + 1) * 64].astype(BF16)

    ins = [('row', proj, 512, 3), ('row', cm, 512, 0), ('row', sam, 512, 0), ('row', sbm, 512, 0), ('row', ck, 128, 0),
           ('row', sak, 128, 0), ('row', sbk, 128, 0), ('full', gq), ('full', gkv), ('full', wuq), ('full', wukv)]
    return tiled_call(name, fn, s, TM, ins, [('hm', 4, 128, BF16), ('hm', 4, 128, BF16), ('hm', 4, 64, BF16)])


def d_prep_bwd(name, s, proj, gq, gkv, wuq, wukv, tm_, tkr, dq_hm, dk_hm, dv_hm, dproj):
    cm, sam, sbm = tm_
    ck, sak, sbk = tkr

    def fn(i, j, dpart, cmv, samv, sbmv, ckv_, sakv, sbkv, gq_ref, gkv_ref, wuq_ref, wukv_ref, dq_ref, dk_ref, dv_ref,
           dd_ref, dgq_ref, dgkv_ref, dwuq_ref, dwukv_ref):
        cq, ckv, kr, rq, rkv, cqn, ckvn = _d_common(dpart, gq_ref, gkv_ref)
        dqf = jnp.concatenate([dq_ref[h] for h in range(4)], axis=1) * QSCALE_MLA
        dkf = [dk_ref[h] for h in range(4)]
        dqd = _rope_t(dqf, cmv, samv, sbmv, 8).astype(BF16)
        dkvd = jnp.concatenate(dkf + [dv_ref[h] for h in range(4)], axis=1).astype(BF16)
        dkr = _rope_t(dkf[0] + dkf[1] + dkf[2] + dkf[3], ckv_, sakv, sbkv, 8)
        dcqn = _dot_nt(dqd, wuq_ref[...])
        dckvn = _dot_nt(dkvd, wukv_ref[...])
        dwuq_ref[...] += _dot_tn(cqn.astype(BF16), dqd)
        dwukv_ref[...] += _dot_tn(ckvn.astype(BF16), dkvd)
        gqv, gkvv = gq_ref[...], gkv_ref[...]
        dcq = gqv * rq * dcqn - cq * (rq * rq * rq) * (jnp.sum(cq * gqv * dcqn, axis=1, keepdims=True) * (1.0 / MLA_Q_LORA))
        dckv = gkvv * rkv * dckvn - ckv * (rkv * rkv * rkv) * jnp.mean(ckv * gkvv * dckvn, axis=1, keepdims=True)
        dgq_ref[...] += jnp.sum(cq * rq * dcqn, axis=0, keepdims=True)
        dgkv_ref[...] += jnp.sum(ckv * rkv * dckvn, axis=0, keepdims=True)
        dd_ref[...] = jnp.concatenate([dcq, dckv, dkr], axis=1).astype(BF16)

    ins = [('row', proj, 512, 3), ('row', cm, 512, 0), ('row', sam, 512, 0), ('row', sbm, 512, 0), ('row', ck, 128, 0),
           ('row', sak, 128, 0), ('row', sbk, 128, 0), ('full', gq), ('full', gkv), ('full', wuq), ('full', wukv),
           ('hm', dq_hm), ('hm', dk_hm), ('hm', dv_hm)]
    return tiled_call(name, fn, s, TM, ins, [('row_into', dproj, 512, 3), ('acc', (1, 256)), ('acc', (1, 128)),
                                             ('acc', (256, 512)), ('acc', (128, 768))])


HALO_F = 8
TN_F = 1408
TM_F = 512
TM_FB, TN_FB = 512, 256


def _conv3(ext, w_ref):
    return _roll_r(ext, -1) * w_ref[pl.ds(0, 1), :] + ext * w_ref[pl.ds(1, 1), :] + _roll_r(ext, 1) * w_ref[pl.ds(2, 1), :]


def f_fwd(name, s, upraw, cw, cb):
    ncj = D_FF // TN_F

    def fn(i, j, xa, xg, wa_ref, wg_ref, ba_ref, bg_ref, o_ref):
        ua = (_conv3(xa, wa_ref) + ba_ref[...])[HALO_F:HALO_F + TM_F]
        ug = (_conv3(xg, wg_ref) + bg_ref[...])[HALO_F:HALO_F + TM_F]
        o_ref[...] = (ua * _sigmoid(ua) * ug).astype(BF16)

    ins = [('halo', upraw, TN_F, lambda j: j, HALO_F), ('halo', upraw, TN_F, lambda j: j + ncj, HALO_F),
           ('col', cw, TN_F, lambda j: j), ('col', cw, TN_F, lambda j: j + ncj),
           ('col', cb, TN_F, lambda j: j), ('col', cb, TN_F, lambda j: j + ncj)]
    return tiled_call(name, fn, s, TM_F, ins, [('row', D_FF, TN_F, lambda j: j, BF16)], ncol=ncj)[0]


def f_bwd(name, s, upraw, dact, cw, cb):
    ncj = D_FF // TN_FB

    def fn(i, j, xa, xg, da, wa_ref, wg_ref, ba_ref, bg_ref, dx_ref, dwa_ref, dwg_ref, dba_ref, dbg_ref):
        taps_a = (_roll_r(xa, -1), xa, _roll_r(xa, 1))
        taps_g = (_roll_r(xg, -1), xg, _roll_r(xg, 1))
        ua = sum(t * wa_ref[pl.ds(k, 1), :] for k, t in enumerate(taps_a)) + ba_ref[...]
        ug = sum(t * wg_ref[pl.ds(k, 1), :] for k, t in enumerate(taps_g)) + bg_ref[...]
        sg = _sigmoid(ua)
        dug = da * ua * sg
        dua = da * ug * sg * (1.0 + ua * (1.0 - sg))
        cen = slice(HALO_F, HALO_F + TM_FB)
        for half, (du, taps, w_ref, dw_ref, db_ref) in enumerate(((dua, taps_a, wa_ref, dwa_ref, dba_ref),
                                                                   (dug, taps_g, wg_ref, dwg_ref, dbg_ref))):
            dx = _roll_r(du, 1) * w_ref[pl.ds(0, 1), :] + du * w_ref[pl.ds(1, 1), :] + _roll_r(du, -1) * w_ref[pl.ds(2, 1), :]
            dx_ref[half] = dx[cen].astype(BF16)
            duc = du[cen]
            for k in range(3):
                dw_ref[pl.ds(k, 1), :] += jnp.sum(duc * taps[k][cen], axis=0, keepdims=True)
            db_ref[...] += jnp.sum(duc, axis=0, keepdims=True)

    tn = TN_FB
    ins = [('halo', upraw, tn, lambda j: j, HALO_F), ('halo', upraw, tn, lambda j: j + ncj, HALO_F),
           ('halo', dact, tn, lambda j: j, HALO_F),
           ('col', cw, tn, lambda j: j), ('col', cw, tn, lambda j: j + ncj),
           ('col', cb, tn, lambda j: j), ('col', cb, tn, lambda j: j + ncj)]
    outs = [('row3', 2, D_FF, tn, lambda j: j, BF16),
            ('colacc', 8, D_FF, tn, lambda j: j), ('colacc', 8, D_FF, tn, lambda j: j),
            ('colacc', 1, D_FF, tn, lambda j: j), ('colacc', 1, D_FF, tn, lambda j: j)]
    return tiled_call(name, fn, s, TM_FB, ins, outs, ncol=ncj)


def _pad_w_in(w):
    z = lambda n: jnp.zeros((w.shape[0], n), w.dtype)
    return jnp.concatenate([w[:, :1728], z(64), w[:, 1728:1856], z(64), w[:, 1856:1888], z(32)], axis=1)


def _unpad_w_in(g):
    return jnp.concatenate([g[:, :1728], g[:, 1792:1920], g[:, 1984:2016]], axis=1)


def _pad_uq(w):
    w = w.reshape(MLA_Q_LORA, 4, 96)
    w = jnp.concatenate([w, jnp.zeros((MLA_Q_LORA, 4, 32), w.dtype)], axis=2).reshape(MLA_Q_LORA, 512)
    return jnp.concatenate([w, jnp.zeros((64, 512), w.dtype)], axis=0)


def _unpad_uq(g):
    return g[:MLA_Q_LORA].reshape(MLA_Q_LORA, 4, 128)[:, :, :96].reshape(MLA_Q_LORA, 384)


def _pad_ukv(w):
    w = w.reshape(MLA_KV_LORA, 4, 128)
    kn = jnp.concatenate([w[:, :, :64], jnp.zeros((MLA_KV_LORA, 4, 64), w.dtype)], axis=2).reshape(MLA_KV_LORA, 512)
    return jnp.concatenate([kn, w[:, :, 64:].reshape(MLA_KV_LORA, 256)], axis=1)


def _unpad_ukv(g):
    kn = g[:, :512].reshape(MLA_KV_LORA, 4, 128)[:, :, :64]
    v = g[:, 512:].reshape(MLA_KV_LORA, 4, 64)
    return jnp.concatenate([kn, v], axis=2).reshape(MLA_KV_LORA, 512)


def _row(v, pad_to=None):
    v = v.reshape(1, -1)
    if pad_to is not None and v.shape[1] < pad_to:
        v = jnp.concatenate([v, jnp.zeros((1, pad_to - v.shape[1]), v.dtype)], axis=1)
    return v


LATE_NAMES = ['w_out', 'ffn_w_up', 'ffn_w_down']


def _assemble(n, part, g, me):
    return jnp.concatenate([jnp.where(me == q, part, g[q]) for q in range(N_CHIPS)], axis=SHARDED_AXIS[n])


def local_step(x, tgt, W, dist=None):
    s = x.shape[0]
    tabs = _rope_tables(s)
    gm = _group_mean_matrix(256, 64)
    alpha = DEEPNORM_ALPHA
    acts = []
    late = {(n, l): W[n][l] for n in LATE_NAMES for l in range(DEPTH)} if dist is None else {}
    h = ln_fwd("ln_in", s, x, _row(W['ln_in_g']), _row(W['ln_in_b']))
    for l in range(DEPTH):
        A = dict(h=h)
        A['wpad'] = _pad_w_in(W['w_in'][l]).astype(BF16)
        A['wuq'] = _pad_uq(W['mla_w_uq'][l]).astype(BF16)
        A['wukv'] = _pad_ukv(W['mla_w_ukv'][l]).astype(BF16)
        A['cw'] = jnp.concatenate([W['conv_a_w'][l], jnp.zeros((1, GROUP_W), F32)], axis=0)
        A['fcw'] = jnp.concatenate([W['ffn_conv_w'][l], jnp.zeros((5, 2 * D_FF), F32)], axis=0)
        A['gq'] = jnp.tile(_row(W['qk_norm_q'][l]), (1, 4))
        A['gk'] = jnp.tile(_row(W['qk_norm_k'][l]), (1, 2))
        A['sgu_bias'] = jnp.repeat(W['sgu_b'][l].T, 64, axis=1)
        A['sgu_wt'] = jnp.swapaxes(W['sgu_w'][l], 1, 2)
        proj = mm(f"proj{l}", h, A['wpad'])
        A['proj'] = proj
        A['y_a'], o_a = a_fwd(f"a_fwd{l}", s, proj, A['cw'], _row(W['conv_a_b'][l]), _row(W['ln_a_g'][l]), _row(W['ln_a_b'][l]))
        A['bq'], A['bk'], A['bv'] = b_prep_fwd(f"b_prep{l}", s, proj, A['gq'], A['gk'], tabs['gqa'], gm)
        if dist is not None and l == 0:
            lp = dist['late']
            parts = [lp['w_out'], lp['ffn_w_up'][0:1], lp['ffn_w_down'][0:1]]
            nxt_names, nxt = dist['next']
            nxt_parts = [a[1:2] for a in nxt]
            wanted = [lp['w_out'], (lp['ffn_w_up'], 0), (lp['ffn_w_down'], 0)] + [(a, 1) for a in nxt]
            (A['o_b'], A['lse_b']), got = attn_fwd(f"b_attn{l}", s, A['bq'], A['bk'], A['bv'], comm=gather_over_chips(wanted))
            full = [_assemble(n, p, g, dist['me']) for n, p, g in zip(LATE_NAMES, parts, got)]
            late[('w_out', 0)], late[('w_out', 1)] = full[0][0], full[0][1]
            late[('ffn_w_up', 0)], late[('ffn_w_down', 0)] = full[1][0], full[2][0]
            for n, p, g in zip(nxt_names, nxt_parts, got[len(parts):]):
                W[n][1] = _assemble(n, p, g, dist['me'])[0]
        else:
            A['o_b'], A['lse_b'] = attn_fwd(f"b_attn{l}", s, A['bq'], A['bk'], A['bv'])
        o_c = c_fwd(f"c_fwd{l}", s, proj, _row(W['sgu_ln_g'][l]), _row(W['sgu_ln_b'][l]), W['sgu_w'][l], A['sgu_bias'])
        A['dq'], A['dk'], A['dv'] = d_prep_fwd(f"d_prep{l}", s, proj, _row(W['mla_q_norm'][l], 256), _row(W['mla_kv_norm'][l]),
                                               A['wuq'], A['wukv'], tabs['mla'], tabs['kr'])
        if dist is not None and l == 0:
            parts = [lp['ffn_w_up'][1:2], lp['ffn_w_down'][1:2]]
            (A['o_d'], A['lse_d']), got = attn_fwd(f"d_attn{l}", s, A['dq'], A['dk'], A['dv'],
                                                   comm=gather_over_chips([(lp['ffn_w_up'], 1), (lp['ffn_w_down'], 1)]))
            late[('ffn_w_up', 1)] = _assemble('ffn_w_up', parts[0], got[0], dist['me'])[0]
            late[('ffn_w_down', 1)] = _assemble('ffn_w_down', parts[1], got[1], dist['me'])[0]
        else:
            A['o_d'], A['lse_d'] = attn_fwd(f"d_attn{l}", s, A['dq'], A['dk'], A['dv'])
        A['wout'] = late[('w_out', l)].astype(BF16)
        A['wup'] = late[('ffn_w_up', l)].astype(BF16)
        A['wdown'] = late[('ffn_w_down', l)].astype(BF16)
        A['mixcat'] = jnp.concatenate([o_a, A['o_b'], o_c, A['o_d']], axis=1).astype(BF16)
        A['mix'] = mm(f"out_proj{l}", A['mixcat'], A['wout'])
        h1 = ln_fwd(f"ln_mix{l}", s, A['mix'], _row(W['ln_mix_g'][l]), _row(W['ln_mix_b'][l]), res=h, alpha=alpha)
        A['h1'] = h1
        A['upraw'] = mm(f"ffn_up{l}", h1, A['wup'])
        A['act'] = f_fwd(f"f_fwd{l}", s, A['upraw'], A['fcw'], _row(W['ffn_conv_b'][l]))
        A['f'] = mm(f"ffn_down{l}", A['act'], A['wdown'])
        if l < DEPTH - 1:
            h = ln_fwd(f"ln_ffn{l}", s, A['f'], _row(W['ln_ffn_g'][l]), _row(W['ln_ffn_b'][l]), res=h1, alpha=alpha)
        acts.append(A)

    per_layer = {n: [None] * DEPTH for n in WEIGHT_NAMES if n not in ('ln_in_g', 'ln_in_b')}
    dh = None
    loss_row = None
    overlap = dist is not None and DEPTH == 2
    red = {}
    grp1 = [(n, 1) for n in BIG_NAMES]
    grp2 = [('ffn_w_up', 0), ('ffn_w_down', 0)]

    def carried(fn, name, *a, comm=None, **kw):
        if comm is None:
            return fn(name, *a, **kw), None
        return fn(name, *a, comm=comm, **kw)

    def pair_sums(tag, xs, r1):
        return [pair_sum(f"pair_sum_{tag}{t}", xv, rv, dist['idx'], BF16) for t, (xv, rv) in enumerate(zip(xs, r1))]

    def chip_sums(tag, cs, ps):
        return [chip_sum(f"chip_sum_{tag}{t}", [cv], [pv], dist['idx']) for t, (cv, pv) in enumerate(zip(cs, ps))]

    for l in reversed(range(DEPTH)):
        A = acts[l]
        carry = overlap and l == 0
        g_ffn, b_ffn = _row(W['ln_ffn_g'][l]), _row(W['ln_ffn_b'][l])
        if l == DEPTH - 1:
            dz, dg, db, loss_row = loss_and_ln_bwd("loss", s, A['f'], A['h1'], g_ffn, b_ffn, tgt, alpha)
        else:
            dz, dg, db = ln_bwd(f"ln_ffn_bwd{l}", s, dh, A['f'], g_ffn, res=A['h1'], alpha=alpha)
        per_layer['ln_ffn_g'][l], per_layer['ln_ffn_b'][l] = dg[0], db[0]
        xs1 = [per_layer[n][k] for n, k in grp1] if carry else None
        dwd, r1 = carried(mm, f"dw_down{l}", A['act'], dz, ta=True, comm=pair_exchange(xs1) if carry else None)
        per_layer['ffn_w_down'][l] = dwd.reshape(N_CHIPS, D_FF // N_CHIPS, D_MODEL)
        ps1 = pair_sums("g1_", xs1, r1) if carry else None
        dact = mm(f"d_act{l}", dz, A['wdown'], tb=True)
        dup, dwa, dwg, dba, dbg = f_bwd(f"f_bwd{l}", s, A['upraw'], dact, A['fcw'], _row(W['ffn_conv_b'][l]))
        per_layer['ffn_conv_w'][l] = jnp.concatenate([dwa[:3], dwg[:3]], axis=1)
        per_layer['ffn_conv_b'][l] = jnp.concatenate([dba[0], dbg[0]], axis=0)
        per_layer['ffn_w_up'][l] = mm(f"dw_up{l}", A['h1'], dup, ta=True, chip_major=True, b_pair=True)
        xs2 = [per_layer[n][k] for n, k in grp2] if carry else None
        dh1_mm, r2 = carried(mm, f"d_h1{l}", dup, A['wup'], tb=True, a_pair=True, comm=pair_exchange(xs2) if carry else None)
        ps2 = pair_sums("g2_", xs2, r2) if carry else None
        dz1, dg, db = _ln_mix_bwd(l, s, dz, dh1_mm, A, W, alpha)
        per_layer['ln_mix_g'][l], per_layer['ln_mix_b'][l] = dg[0], db[0]
        per_layer['w_out'][l] = mm(f"dw_out{l}", A['mixcat'], dz1, ta=True).reshape(N_CHIPS, D_MODEL // N_CHIPS, D_MODEL)
        xs3 = [per_layer['w_out'][l]] if carry else None
        dmixcat, r3 = carried(mm, f"d_mixcat{l}", dz1, A['wout'], tb=True, comm=pair_exchange(xs3) if carry else None)
        if carry:
            ps2, grp2 = ps2 + pair_sums("g3_", xs3, r3), grp2 + [('w_out', 0)]
        dy_a, dg, db, dcb = a_bwd1(f"a_bwd1{l}", s, A['y_a'], dmixcat, _row(W['ln_a_g'][l]), _row(W['ln_a_b'][l]))
        per_layer['ln_a_g'][l], per_layer['ln_a_b'][l], per_layer['conv_a_b'][l] = dg[0], db[0], dcb[0]
        dproj, dcw = a_bwd2(f"a_bwd2{l}", s, A['proj'], dy_a, A['cw'])
        per_layer['conv_a_w'][l] = dcw[:CONV_A_WIDTH]
        (dq, dk, dv), cs1 = carried(attn_bwd, f"b_attn_bwd{l}", s, A['bq'], A['bk'], A['bv'], A['o_b'], A['lse_b'], dmixcat, 2,
                                    comm=chip_exchange(ps1) if carry else None)
        halves1 = chip_sums("g1_", cs1, ps1) if carry else None
        dproj, dgq, dgk = b_prep_bwd(f"b_prep_bwd{l}", s, A['proj'], A['gq'], A['gk'], tabs['gqa'], gm, dq, dk, dv, dproj)
        per_layer['qk_norm_q'][l], per_layer['qk_norm_k'][l] = dgq[0], dgk[0]
        dproj, dg, db, dsw, dsb = c_bwd(f"c_bwd{l}", s, A['proj'], _row(W['sgu_ln_g'][l]), _row(W['sgu_ln_b'][l]), W['sgu_w'][l],
                                        A['sgu_wt'], A['sgu_bias'], dmixcat, dproj)
        per_layer['sgu_ln_g'][l], per_layer['sgu_ln_b'][l] = dg[0], db[0]
        per_layer['sgu_w'][l], per_layer['sgu_b'][l] = dsw, dsb[:, :4].T
        comms = [half_exchange(halves1), chip_exchange(ps2)] if carry else []
        (dq, dk, dv), cres = carried(attn_bwd, f"d_attn_bwd{l}", s, A['dq'], A['dk'], A['dv'], A['o_d'], A['lse_d'], dmixcat, 6,
                                     comm=join_comms(comms))
        if carry:
            done1, cs2 = split_comm_results(comms, cres)
            red.update(zip(grp1, done1))
            halves2 = chip_sums("g2_", cs2, ps2)
        dproj, dgq, dgkv, dwuq, dwukv = d_prep_bwd(f"d_prep_bwd{l}", s, A['proj'], _row(W['mla_q_norm'][l], 256),
                                                   _row(W['mla_kv_norm'][l]), A['wuq'], A['wukv'], tabs['mla'], tabs['kr'],
                                                   dq, dk, dv, dproj)
        per_layer['mla_q_norm'][l], per_layer['mla_kv_norm'][l] = dgq[0, :MLA_Q_LORA], dgkv[0]
        per_layer['mla_w_uq'][l], per_layer['mla_w_ukv'][l] = _unpad_uq(dwuq), _unpad_ukv(dwukv)
        dw_in, done2 = carried(mm, f"dw_in{l}", A['h'], dproj, ta=True, comm=half_exchange(halves2) if carry else None)
        if carry:
            red.update(zip(grp2, done2))
        per_layer['w_in'][l] = _unpad_w_in(dw_in).reshape(D_MODEL, N_CHIPS, D_IN_PROJ // N_CHIPS).transpose(1, 0, 2)
        dh = mm(f"d_h{l}", dproj, A['wpad'], tb=True, add=dz1, add_scale=alpha)
    dx, dg, db = ln_bwd("ln_in_bwd", s, dh, x, _row(W['ln_in_g']))
    G = dict(per_layer)
    G['ln_in_g'], G['ln_in_b'] = dg[0], db[0]
    for n, k in red:
        G[n][k] = None
    G['reduced'] = red
    return loss_row, dx, G


def _ln_mix_bwd(l, s, dz, dh1_mm, A, W, alpha):
    ins = [('row', dz, D_MODEL, 0), ('row', dh1_mm, D_MODEL, 0), ('row', A['mix'], D_MODEL, 0), ('row', A['h'], D_MODEL, 0),
           ('full', _row(W['ln_mix_g'][l]))]

    def fn(i, j, dzv, dmv, xv, rv, g_ref, o_ref, dg_ref, db_ref):
        xh, r = _ln_stats(alpha * rv + xv)
        d, dg, db = _ln_bwd(alpha * dzv + dmv, xh, r, g_ref[...])
        o_ref[...] = d
        dg_ref[...] += dg
        db_ref[...] += db

    return tiled_call(f"ln_mix_bwd{l}", fn, s, TM, ins, [('row', D_MODEL, D_MODEL, 0, F32), ('acc', (1, D_MODEL)),
                                                         ('acc', (1, D_MODEL))])


ANY = pl.BlockSpec(memory_space=pl.ANY)


def _pos():
    return lax.axis_index("x"), lax.axis_index("y"), lax.axis_index("c")


def _rcopy(src, dst, ssem, rsem, dev):
    return pltpu.make_async_remote_copy(src_ref=src, dst_ref=dst, send_sem=ssem, recv_sem=rsem, device_id=dev,
                                        device_id_type=MESH)


class Comm:
    def __init__(self, operands, out_shape, sems, start, finish, aliases=None):
        self.operands, self.out_shape, self.sems = list(operands), list(out_shape), list(sems)
        self.start, self.finish, self.aliases = start, finish, dict(aliases or {})


def join_comms(comms):
    comms = [cm for cm in comms if cm is not None]
    if not comms:
        return None
    offs, oi, oo, os_ = [], 0, 0, 0
    for cm in comms:
        offs.append((oi, oo, os_))
        oi, oo, os_ = oi + len(cm.operands), oo + len(cm.out_shape), os_ + len(cm.sems)

    def part(fn_name):
        def run(ins, outs, sems):
            for cm, (a, b, s) in zip(comms, offs):
                getattr(cm, fn_name)(ins[a:a + len(cm.operands)], outs[b:b + len(cm.out_shape)], sems[s:s + len(cm.sems)])
        return run

    aliases = {a + i: b + o for cm, (a, b, s) in zip(comms, offs) for i, o in cm.aliases.items()}
    return Comm([v for cm in comms for v in cm.operands], [v for cm in comms for v in cm.out_shape],
                [v for cm in comms for v in cm.sems], part("start"), part("finish"), aliases)


def split_comm_results(comms, res):
    out, p = [], 0
    for cm in comms:
        if cm is None:
            out.append(None)
        else:
            out.append(list(res[p:p + len(cm.out_shape)]))
            p += len(cm.out_shape)
    return out


def run_comm(name, comm):
    n_in, n_out = len(comm.operands), len(comm.out_shape)

    def body(*refs):
        ins, outs, sems = refs[:n_in], refs[n_in:n_in + n_out], refs[n_in + n_out:]
        comm.start(ins, outs, sems)
        comm.finish(ins, outs, sems)

    return pl.pallas_call(body, name=name, in_specs=[ANY] * n_in, out_specs=[ANY] * n_out, out_shape=comm.out_shape,
                          scratch_shapes=[pltpu.SemaphoreType.DMA((k,)) for k in comm.sems],
                          input_output_aliases=comm.aliases)(*comm.operands)


def gather_over_chips(parts):
    n = len(parts)
    layer = [p[1] if isinstance(p, tuple) else None for p in parts]
    parts = [p[0] if isinstance(p, tuple) else p for p in parts]
    shapes = [p.shape if l is None else (1,) + p.shape[1:] for p, l in zip(parts, layer)]
    split = [shp[1] % 32 == 0 for shp in shapes]

    def geometry():
        x, y, c = _pos()
        return x, y, c, 2 * x + y, [(1 - x, y), (x, 1 - y), (1 - x, 1 - y)]

    def rows(t, half):
        a2 = shapes[t][1] // 2
        return pl.ds(half * a2, a2)

    def ici(srcs, dsts, sems, j, t, px, py, c, me):
        k = n * j + t
        src = srcs[t] if layer[t] is None else srcs[t].at[pl.ds(layer[t], 1)]
        if split[t]:
            return _rcopy(src.at[:, rows(t, c), :], dsts[t].at[me, :, rows(t, c), :], sems[0].at[k], sems[1].at[k], (px, py, c))
        return _rcopy(src, dsts[t].at[me], sems[0].at[k], sems[1].at[k], (px, py, c))

    def start(srcs, dsts, sems):
        x, y, c, me, chips = geometry()
        for j, (px, py) in enumerate(chips):
            for t in range(n):
                ici(srcs, dsts, sems, j, t, px, py, c, me).start()

    def finish(srcs, dsts, sems):
        x, y, c, me, chips = geometry()
        fwd = []
        for j, (px, py) in enumerate(chips):
            q = 2 * px + py
            for t in range(n):
                k = n * j + t
                if split[t]:
                    got = dsts[t].at[q, :, rows(t, c), :]
                    _rcopy(got, got, sems[0].at[k], sems[1].at[k], (px, py, c)).wait_recv()
                    fw = _rcopy(got, got, sems[2].at[k], sems[3].at[k], (x, y, 1 - c))
                    fw.start()
                    fwd.append(fw)
                else:
                    _rcopy(dsts[t].at[q], dsts[t].at[q], sems[0].at[k], sems[1].at[k], (px, py, c)).wait_recv()
        for j, (px, py) in enumerate(chips):
            q = 2 * px + py
            for t in range(n):
                if split[t]:
                    other = dsts[t].at[q, :, rows(t, 1 - c), :]
                    _rcopy(other, other, sems[2].at[n * j + t], sems[3].at[n * j + t], (x, y, 1 - c)).wait_recv()
                ici(srcs, dsts, sems, j, t, px, py, c, me).wait_send()
        for fw in fwd:
            fw.wait_send()

    return Comm(parts, [jax.ShapeDtypeStruct((N_CHIPS,) + shp, p.dtype) for p, shp in zip(parts, shapes)], [3 * n] * 4, start, finish)


def pair_exchange(xs):
    n = len(xs)

    def copies(srcs, dsts, sems):
        x, y, c = _pos()
        return [_rcopy(srcs[t].at[:, pl.ds((1 - c) * (xs[t].shape[1] // 2), xs[t].shape[1] // 2), :], dsts[t],
                       sems[0].at[t], sems[1].at[t], (x, y, 1 - c)) for t in range(n)]

    def start(srcs, dsts, sems):
        for cp in copies(srcs, dsts, sems):
            cp.start()

    def finish(srcs, dsts, sems):
        for cp in copies(srcs, dsts, sems):
            cp.wait()

    return Comm(xs, [jax.ShapeDtypeStruct((a.shape[0], a.shape[1] // 2, a.shape[2]), a.dtype) for a in xs], [n, n], start, finish)


def chip_exchange(ps):
    n = len(ps)

    def geometry():
        x, y, c = _pos()
        return c, 2 * x + y, [(1 - x, y), (x, 1 - y), (1 - x, 1 - y)]

    def start(srcs, dsts, sems):
        c, me, chips = geometry()
        for j, (px, py) in enumerate(chips):
            for t in range(n):
                _rcopy(srcs[t].at[2 * px + py], dsts[t].at[me], sems[0].at[n * j + t], sems[1].at[n * j + t], (px, py, c)).start()

    def finish(srcs, dsts, sems):
        c, me, chips = geometry()
        for j, (px, py) in enumerate(chips):
            for t in range(n):
                _rcopy(srcs[t].at[2 * px + py], dsts[t].at[2 * px + py], sems[0].at[n * j + t], sems[1].at[n * j + t], (px, py, c)).wait()

    return Comm(ps, [jax.ShapeDtypeStruct(p.shape, p.dtype) for p in ps], [3 * n, 3 * n], start, finish)


def half_exchange(bufs):
    n = len(bufs)

    def copies(ins, outs, sems):
        x, y, c = _pos()
        cps = []
        for t in range(n):
            a2 = bufs[t].shape[1] // 2
            mine = pl.ds(c * a2, a2)
            cps.append(_rcopy(ins[t].at[:, mine, :], outs[t].at[:, mine, :], sems[0].at[t], sems[1].at[t], (x, y, 1 - c)))
        return cps

    def start(ins, outs, sems):
        for cp in copies(ins, outs, sems):
            cp.start()

    def finish(ins, outs, sems):
        for cp in copies(ins, outs, sems):
            cp.wait()

    return Comm(bufs, [jax.ShapeDtypeStruct(b.shape, b.dtype) for b in bufs], [n, n], start, finish, {t: t for t in range(n)})


LANES = 1024
ROW_TILE_BYTES = 2 * 1024 * 1024


def _row_tile(r, row_bytes):
    if r * row_bytes <= ROW_TILE_BYTES:
        return r
    best = None
    for t in range(16, r, 16):
        if r % t == 0 and t * row_bytes <= ROW_TILE_BYTES:
            best = t
    assert best is not None, (r, row_bytes)
    return best


def pair_sum(name, x, r1, c_arr, out_dtype):
    q, a2, b = r1.shape
    tr = _row_tile(a2, b * 4)
    nb = a2 // tr

    def body(c_ref, x_ref, r_ref, o_ref):
        o_ref[...] = (x_ref[...] + r_ref[...]).astype(out_dtype)

    grid_spec = pltpu.PrefetchScalarGridSpec(
        num_scalar_prefetch=1, grid=(q, nb),
        in_specs=[pl.BlockSpec((None, tr, b), lambda k, i, c_ref: (k, c_ref[0] * nb + i, 0)),
                  pl.BlockSpec((None, tr, b), lambda k, i, c_ref: (k, i, 0))],
        out_specs=pl.BlockSpec((None, tr, b), lambda k, i, c_ref: (k, i, 0)))
    return pl.pallas_call(body, name=name, grid_spec=grid_spec, out_shape=jax.ShapeDtypeStruct(r1.shape, out_dtype),
                          compiler_params=_cparams(("parallel", "parallel")))(c_arr, x, r1)


def chip_sum(name, recv, own, idx):
    nl = len(recv)
    k, a2, b = recv[0].shape
    tr = _row_tile(a2, k * b * recv[0].dtype.itemsize)
    nb = a2 // tr

    def body(idx_ref, *refs):
        rs, os_, o_ref = refs[:nl], refs[nl:2 * nl], refs[2 * nl]
        l = pl.program_id(0)
        me = idx_ref[1]
        for li in range(nl):
            @pl.when(l == li)
            def _(li=li):
                acc = None
                for q in range(k):
                    term = jnp.where(me == q, os_[li][...], rs[li][q]).astype(F32)
                    acc = term if acc is None else acc + term
                o_ref[...] = acc

    def pick(li):
        return lambda l, i, idx_ref: jnp.where(l == li, i, 0)

    in_specs = [pl.BlockSpec((k, tr, b), lambda l, i, idx_ref, _p=pick(li): (0, _p(l, i, idx_ref), 0)) for li in range(nl)]
    in_specs += [pl.BlockSpec((None, tr, b), lambda l, i, idx_ref, _p=pick(li): (idx_ref[1], _p(l, i, idx_ref), 0)) for li in range(nl)]
    grid_spec = pltpu.PrefetchScalarGridSpec(
        num_scalar_prefetch=1, grid=(nl, nb), in_specs=in_specs,
        out_specs=pl.BlockSpec((None, tr, b), lambda l, i, idx_ref: (l, idx_ref[0] * nb + i, 0)))
    return pl.pallas_call(body, name=name, grid_spec=grid_spec, out_shape=jax.ShapeDtypeStruct((nl, 2 * a2, b), F32),
                          compiler_params=_cparams(("arbitrary", "arbitrary")))(idx, *recv, *own)


def adamw(name, gs, w, m, v):
    nl, r, b = w.shape
    assert len(gs) == nl
    tr = _row_tile(r, b * 4)

    def body(*refs):
        g_refs, (w_ref, m_ref, v_ref, go_ref, d_ref, nm_ref, nv_ref) = refs[:nl], refs[nl:]
        l = pl.program_id(0)
        for li in range(nl):
            @pl.when(l == li)
            def _(li=li):
                gv = g_refs[li][...]
                go_ref[...] = gv
                mn = ADAM_B1 * m_ref[...] + (1.0 - ADAM_B1) * gv
                vn = ADAM_B2 * v_ref[...] + (1.0 - ADAM_B2) * (gv * gv)
                m_hat = mn / (1.0 - ADAM_B1 ** ADAM_STEP)
                v_hat = vn / (1.0 - ADAM_B2 ** ADAM_STEP)
                d_ref[...] = -ADAM_LR * (m_hat / (jnp.sqrt(v_hat) + ADAM_EPS) + ADAM_WD * w_ref[...])
                nm_ref[...] = mn
                nv_ref[...] = vn

    g_specs = [pl.BlockSpec((None, tr, b), lambda l, i, _li=li: (0, jnp.where(l == _li, i, 0), 0)) for li in range(nl)]
    spec = pl.BlockSpec((None, tr, b), lambda l, i: (l, i, 0))
    sds = jax.ShapeDtypeStruct(w.shape, F32)
    return pl.pallas_call(body, name=name, grid=(nl, r // tr), in_specs=g_specs + [spec] * 3, out_specs=[spec] * 4,
                          out_shape=[sds] * 4, compiler_params=_cparams(("arbitrary", "arbitrary")))(*gs, w, m, v)


def adamw_many(name, gs, ws, ms, vs):
    n = len(gs)

    def body(*refs):
        for k in range(n):
            g_ref, w_ref, m_ref, v_ref, go_ref, d_ref, nm_ref, nv_ref = (refs[j * n + k] for j in range(8))
            gv = g_ref[...]
            go_ref[...] = gv
            mn = ADAM_B1 * m_ref[...] + (1.0 - ADAM_B1) * gv
            vn = ADAM_B2 * v_ref[...] + (1.0 - ADAM_B2) * (gv * gv)
            m_hat = mn / (1.0 - ADAM_B1 ** ADAM_STEP)
            v_hat = vn / (1.0 - ADAM_B2 ** ADAM_STEP)
            d_ref[...] = -ADAM_LR * (m_hat / (jnp.sqrt(v_hat) + ADAM_EPS) + ADAM_WD * w_ref[...])
            nm_ref[...] = mn
            nv_ref[...] = vn

    vm = pl.BlockSpec(memory_space=pltpu.VMEM)
    res = pl.pallas_call(body, name=name, in_specs=[vm] * (4 * n), out_specs=[vm] * (4 * n),
                         out_shape=[jax.ShapeDtypeStruct(a.shape, F32) for _ in range(4) for a in ws],
                         compiler_params=pltpu.CompilerParams(vmem_limit_bytes=VMEM_LIMIT))(*gs, *ws, *ms, *vs)
    return [list(res[j * n:(j + 1) * n]) for j in range(4)]


def _small_layout(shapes):
    views, windows, r0 = [], [], 0
    for shp in shapes:
        if len(shp) == 4:
            nblk = shp[0] * shp[1]
            assert shp[2:] == (CHUNK, 128) and nblk * 128 <= LANES
            views.append((nblk, CHUNK, 128))
            windows.append([((j,), r0, CHUNK, 128 * j, 128) for j in range(nblk)])
            r0 += CHUNK
            continue
        a, c = (1, shp[0]) if len(shp) == 1 else (int(np.prod(shp[:-1])), shp[-1])
        views.append((a, c))
        wins = []
        for j in range(-(-c // LANES)):
            w = min(LANES, c - j * LANES)
            wins.append(((slice(None), slice(j * LANES, j * LANES + w)), r0, a, 0, w))
            r0 += 8 * -(-a // 8)
        windows.append(wins)
    return 32 * -(-r0 // 32), views, windows


def pack_small(name, arrs, rows, windows):
    def body(*refs):
        ins, o_ref = refs[:-1], refs[-1]
        o_ref[...] = jnp.zeros_like(o_ref)
        for x_ref, wins in zip(ins, windows):
            for idx, r0, nr, l0, nl in wins:
                o_ref[r0:r0 + nr, l0:l0 + nl] = x_ref[idx]

    vm = pl.BlockSpec(memory_space=pltpu.VMEM)
    return pl.pallas_call(body, name=name, in_specs=[vm] * len(arrs), out_specs=vm,
                          out_shape=jax.ShapeDtypeStruct((rows, LANES), F32))(*arrs)


def adamw_windows(name, g_pack, windows, ws, ms, vs):
    n = len(ws)

    def body(g_ref, *refs):
        for k in range(n):
            w_ref, m_ref, v_ref, go_ref, d_ref, nm_ref, nv_ref = (refs[j * n + k] for j in range(7))
            for idx, r0, nr, l0, nl in windows[k]:
                gv = g_ref[r0:r0 + nr, l0:l0 + nl]
                go_ref[idx] = gv
                mn = ADAM_B1 * m_ref[idx] + (1.0 - ADAM_B1) * gv
                vn = ADAM_B2 * v_ref[idx] + (1.0 - ADAM_B2) * (gv * gv)
                m_hat = mn / (1.0 - ADAM_B1 ** ADAM_STEP)
                v_hat = vn / (1.0 - ADAM_B2 ** ADAM_STEP)
                d_ref[idx] = -ADAM_LR * (m_hat / (jnp.sqrt(v_hat) + ADAM_EPS) + ADAM_WD * w_ref[idx])
                nm_ref[idx] = mn
                nv_ref[idx] = vn

    vm = pl.BlockSpec(memory_space=pltpu.VMEM)
    res = pl.pallas_call(body, name=name, in_specs=[vm] * (1 + 3 * n), out_specs=[vm] * (4 * n),
                         out_shape=[jax.ShapeDtypeStruct(a.shape, F32) for _ in range(4) for a in ws],
                         compiler_params=pltpu.CompilerParams(vmem_limit_bytes=VMEM_LIMIT))(g_pack, *ws, *ms, *vs)
    return [list(res[j * n:(j + 1) * n]) for j in range(4)]


def _pack_rows(shapes):
    return [8 * -(-int(np.prod(s)) // (8 * LANES)) for s in shapes]


def _pack(arrs, rows):
    pieces = []
    for a, r in zip(arrs, _pack_rows([a.shape for a in arrs])):
        flat = a.reshape(-1)
        pieces.append(jnp.pad(flat, (0, r * LANES - flat.shape[0])).reshape(r, LANES))
    used = sum(p.shape[0] for p in pieces)
    assert used <= rows
    if used < rows:
        pieces.append(jnp.zeros((rows - used, LANES), F32))
    return jnp.concatenate(pieces, axis=0)


def _unpack(pack, shapes):
    out, off = [], 0
    for shp, r in zip(shapes, _pack_rows(shapes)):
        n = int(np.prod(shp))
        out.append(pack[off:off + r].reshape(-1)[:n].reshape(shp))
        off += r
    return out


BIG_NAMES = ['w_in', 'w_out', 'ffn_w_up', 'ffn_w_down']
MED_NAMES = [n for n in SHARDED_NAMES if n not in BIG_NAMES]
MED_ROWS = 96


def kernel(x, ln_in_g, ln_in_b, w_in, conv_a_w, conv_a_b, ln_a_g, ln_a_b, qk_norm_q, qk_norm_k, sgu_ln_g, sgu_ln_b, sgu_w, sgu_b, mla_q_norm, mla_w_uq, mla_kv_norm, mla_w_ukv, w_out, ln_mix_g, ln_mix_b, ffn_w_up, ffn_conv_w, ffn_conv_b, ffn_w_down, ln_ffn_g, ln_ffn_b, loss_target, m_ln_in_g, m_ln_in_b, m_w_in, m_conv_a_w, m_conv_a_b, m_ln_a_g, m_ln_a_b, m_qk_norm_q, m_qk_norm_k, m_sgu_ln_g, m_sgu_ln_b, m_sgu_w, m_sgu_b, m_mla_q_norm, m_mla_w_uq, m_mla_kv_norm, m_mla_w_ukv, m_w_out, m_ln_mix_g, m_ln_mix_b, m_ffn_w_up, m_ffn_conv_w, m_ffn_conv_b, m_ffn_w_down, m_ln_ffn_g, m_ln_ffn_b, v_ln_in_g, v_ln_in_b, v_w_in, v_conv_a_w, v_conv_a_b, v_ln_a_g, v_ln_a_b, v_qk_norm_q, v_qk_norm_k, v_sgu_ln_g, v_sgu_ln_b, v_sgu_w, v_sgu_b, v_mla_q_norm, v_mla_w_uq, v_mla_kv_norm, v_mla_w_ukv, v_w_out, v_ln_mix_g, v_ln_mix_b, v_ffn_w_up, v_ffn_conv_w, v_ffn_conv_b, v_ffn_w_down, v_ln_ffn_g, v_ln_ffn_b):
    loc = dict(locals())
    w_loc = {n: loc[n] for n in WEIGHT_NAMES}
    m_loc = {n: loc["m_" + n] for n in WEIGHT_NAMES}
    v_loc = {n: loc["v_" + n] for n in WEIGHT_NAMES}
    me_chip = 2 * lax.axis_index("x") + lax.axis_index("y")
    idx = jnp.stack([lax.axis_index("c"), me_chip]).astype(jnp.int32)

    def wire(n):
        return w_loc[n].astype(BF16) if n in BF16_WIRE else w_loc[n]

    assert DEPTH == 2
    early = [n for n in SHARDED_NAMES if n not in LATE_NAMES]
    wired = {n: wire(n) for n in SHARDED_NAMES}
    W = {n: w_loc[n] for n in REPL_NAMES}
    for n, g in zip(early, run_comm("gather_early", gather_over_chips([(wired[n], 0) for n in early]))):
        W[n] = [_assemble(n, wired[n][0:1], g, me_chip)[0], None]

    dist = dict(idx=idx, me=me_chip, late={n: wired[n] for n in LATE_NAMES}, next=(early, [wired[n] for n in early]))
    loss_row, dx, G = local_step(x[0], loss_target[0], W, dist)
    loss = lax.psum(loss_row[0, 0], ("x", "y", "c"))
    red = G['reduced']

    med = []
    for q in range(N_CHIPS):
        pieces = [lax.slice_in_dim(jnp.stack(G[n]), q * w_loc[n].shape[2], (q + 1) * w_loc[n].shape[2], axis=2) for n in MED_NAMES]
        med.append(_pack(pieces, MED_ROWS))
    rest = [(n, l) for n in BIG_NAMES for l in range(DEPTH) if (n, l) not in red]
    small_rows, views, windows = _small_layout([w_loc[n].shape for n in REPL_NAMES])
    small = [(jnp.stack(G[n]) if isinstance(G[n], list) else G[n]).reshape(v) for n, v in zip(REPL_NAMES, views)]
    small_pack = pack_small("pack_small", small, small_rows, windows)
    xs = [G[n][l] for n, l in rest] + [jnp.stack(med), jnp.broadcast_to(small_pack[None], (N_CHIPS, small_rows, LANES))]
    wire_dt = [BF16] * (len(xs) - 1) + [F32]
    r1 = run_comm("pair_exchange", pair_exchange(xs))
    ps = [pair_sum(f"pair_sum{t}", xv, rv, idx, dt) for t, (xv, rv, dt) in enumerate(zip(xs, r1, wire_dt))]
    cs = run_comm("chip_exchange", chip_exchange(ps))
    halves = [chip_sum(f"chip_sum{t}", [cv], [pv], idx) for t, (cv, pv) in enumerate(zip(cs, ps))]
    done = run_comm("half_exchange", half_exchange(halves))
    red.update(zip(rest, done[:-2]))
    g_med, g_small = done[-2], done[-1]

    outs = {k: {} for k in ("grad", "delta", "new_m", "new_v")}
    for n in BIG_NAMES:
        res = adamw(f"adamw_{n}", [red[(n, l)] for l in range(DEPTH)], w_loc[n], m_loc[n], v_loc[n])
        for k, v in zip(("grad", "delta", "new_m", "new_v"), res):
            outs[k][n] = v
    res = adamw_many("adamw_med", _unpack(g_med[0], [w_loc[n].shape for n in MED_NAMES]),
                     *[[d[n] for n in MED_NAMES] for d in (w_loc, m_loc, v_loc)])
    for k, vals in zip(("grad", "delta", "new_m", "new_v"), res):
        outs[k].update(zip(MED_NAMES, vals))
    res = adamw_windows("adamw_small", g_small[0], windows,
                        *[[d[n].reshape(v) for n, v in zip(REPL_NAMES, views)] for d in (w_loc, m_loc, v_loc)])
    for k, vals in zip(("grad", "delta", "new_m", "new_v"), res):
        outs[k].update({n: v.reshape(w_loc[n].shape) for n, v in zip(REPL_NAMES, vals)})
    return (loss, dx[None], *[outs["grad"][n] for n in WEIGHT_NAMES], *[outs["delta"][n] for n in WEIGHT_NAMES],
            *[outs["new_m"][n] for n in WEIGHT_NAMES], *[outs["new_v"][n] for n in WEIGHT_NAMES])
```

```python
import functools
import math

import jax
import jax.numpy as jnp
import numpy as np
from jax import lax
from jax.experimental import pallas as pl
from jax.experimental.pallas import tpu as pltpu

F32 = jnp.float32
BF16 = jnp.bfloat16

D_MODEL = 1024
DEPTH = 2
GRID_W = 64
GROUP_W = 256
HEAD_DIM = 64
CONV_A_WIDTH = 31
CHUNK = 128
MLA_Q_LORA = 192
MLA_KV_LORA = 128
MLA_NOPE = 64
MLA_ROPE = 32
D_FF = 2816
ROPE_THETA = 10000.0
DEEPNORM_ALPHA = (2 * DEPTH) ** 0.25
LN_EPS = 1e-5
RMS_EPS = 1e-6
D_IN_PROJ = 1888
PROJ_W = 2048

ADAM_LR = 0.001
ADAM_B1 = 0.9
ADAM_B2 = 0.999
ADAM_EPS = 1e-08
ADAM_WD = 0.01
ADAM_STEP = 10

VMEM_LIMIT = 56 * 1024 * 1024
MESH = pl.DeviceIdType.MESH
N_CHIPS = 4
N_DEV = 8

WEIGHT_NAMES = ['ln_in_g', 'ln_in_b', 'w_in', 'conv_a_w', 'conv_a_b', 'ln_a_g', 'ln_a_b', 'qk_norm_q', 'qk_norm_k',
                'sgu_ln_g', 'sgu_ln_b', 'sgu_w', 'sgu_b', 'mla_q_norm', 'mla_w_uq', 'mla_kv_norm', 'mla_w_ukv', 'w_out',
                'ln_mix_g', 'ln_mix_b', 'ffn_w_up', 'ffn_conv_w', 'ffn_conv_b', 'ffn_w_down', 'ln_ffn_g', 'ln_ffn_b']
SHARDED_AXIS = {'w_in': 2, 'conv_a_w': 2, 'mla_w_uq': 2, 'mla_w_ukv': 2, 'w_out': 1, 'ffn_w_up': 2, 'ffn_conv_w': 2,
                'ffn_w_down': 1}
SHARDED_NAMES = [n for n in WEIGHT_NAMES if n in SHARDED_AXIS]
REPL_NAMES = [n for n in WEIGHT_NAMES if n not in SHARDED_AXIS]
BF16_WIRE = ('w_in', 'mla_w_uq', 'mla_w_ukv', 'w_out', 'ffn_w_up', 'ffn_w_down')


def _cparams(sem):
    return pltpu.CompilerParams(dimension_semantics=sem, vmem_limit_bytes=VMEM_LIMIT)


def _pick(n, cands):
    for c in cands:
        if n % c == 0:
            return c
    return n


def carried_call(name, body, grid, in_specs, out_specs, out_shape, scratch, semantics, args, comm=None):
    if comm is None:
        res = pl.pallas_call(body, name=name, grid=grid, in_specs=in_specs, out_specs=out_specs, out_shape=out_shape,
                             scratch_shapes=scratch, compiler_params=_cparams(semantics))(*args)
        return list(res), None
    n_in, n_out, n_scr = len(in_specs), len(out_specs), len(scratch)
    nci, nco = len(comm.operands), len(comm.out_shape)

    def carried(*refs):
        o0 = n_in + nci
        s0 = o0 + n_out + nco
        ins, cins = refs[:n_in], refs[n_in:o0]
        outs, couts = refs[o0:o0 + n_out], refs[o0 + n_out:s0]
        scr, sems = refs[s0:s0 + n_scr], refs[s0 + n_scr:]
        ids = [pl.program_id(d) for d in range(len(grid))]
        first = functools.reduce(lambda u, v: u & v, [i == 0 for i in ids])
        last = functools.reduce(lambda u, v: u & v, [i == g - 1 for i, g in zip(ids, grid)])

        @pl.when(first)
        def _():
            comm.start(cins, couts, sems)

        body(*ins, *outs, *scr)

        @pl.when(last)
        def _():
            comm.finish(cins, couts, sems)

    res = pl.pallas_call(
        carried, name=name, grid=grid, in_specs=list(in_specs) + [ANY] * nci, out_specs=list(out_specs) + [ANY] * nco,
        out_shape=list(out_shape) + comm.out_shape, scratch_shapes=list(scratch) + [pltpu.SemaphoreType.DMA((k,)) for k in comm.sems],
        input_output_aliases={n_in + i: n_out + o for i, o in comm.aliases.items()},
        compiler_params=_cparams(("arbitrary",) * len(grid)))(*args, *comm.operands)
    return list(res[:n_out]), list(res[n_out:])


def mm(name, a, b, *, ta=False, tb=False, add=None, add_scale=1.0, out_dtype=F32, chip_major=False, comm=None,
       a_pair=False, b_pair=False):
    assert not (a_pair and ta) and not (b_pair and tb)
    a_shape = (a.shape[1], 2 * a.shape[2]) if a_pair else a.shape
    b_shape = (b.shape[1], 2 * b.shape[2]) if b_pair else b.shape
    m, k = (a_shape[1], a_shape[0]) if ta else a_shape
    n = b_shape[0] if tb else b_shape[1]
    assert (b_shape[1] if tb else b_shape[0]) == k
    tm = _pick(m, (1024, 1408, 512, 256, 128))
    tn = n // N_CHIPS if chip_major else _pick(n, (1024, 1408, 512, 256, 128))
    tk = _pick(k, (1024, 1408, 512, 256, 128))
    nk = k // tk
    a_spec = pl.BlockSpec((tk, tm), lambda i, j, kk: (kk, i)) if ta else pl.BlockSpec((tm, tk), lambda i, j, kk: (i, kk))
    b_spec = pl.BlockSpec((tn, tk), lambda i, j, kk: (j, kk)) if tb else pl.BlockSpec((tk, tn), lambda i, j, kk: (kk, j))
    if a_pair:
        per = a.shape[2] // tk
        assert per * tk == a.shape[2]
        a_spec = pl.BlockSpec((None, tm, tk), lambda i, j, kk: (kk // per, i, kk % per))
    if b_pair:
        per = b.shape[2] // tn
        assert per * tn == b.shape[2]
        b_spec = pl.BlockSpec((None, tk, tn), lambda i, j, kk: (j // per, kk, j % per))
    in_specs = [a_spec, b_spec]
    args = [a, b]
    if add is not None:
        in_specs.append(pl.BlockSpec((tm, tn), lambda i, j, kk: (i, j)))
        args.append(add)
    dims = (((0 if ta else 1,), (1 if tb else 0,)), ((), ()))

    def body(*refs):
        a_ref, b_ref = refs[0], refs[1]
        add_ref = refs[2] if add is not None else None
        o_ref, acc_ref = refs[-2], refs[-1]
        kk = pl.program_id(2)
        part = lax.dot_general(a_ref[...].astype(BF16), b_ref[...].astype(BF16), dims, preferred_element_type=F32)
        if nk == 1:
            if add_ref is not None:
                part = part + add_scale * add_ref[...].astype(F32)
            o_ref[...] = part.astype(out_dtype)
            return

        @pl.when(kk == 0)
        def _():
            acc_ref[...] = part

        @pl.when(kk > 0)
        def _():
            acc_ref[...] += part

        @pl.when(kk == nk - 1)
        def _():
            r = acc_ref[...]
            if add_ref is not None:
                r = r + add_scale * add_ref[...].astype(F32)
            o_ref[...] = r.astype(out_dtype)

    if chip_major:
        out_spec = pl.BlockSpec((None, tm, tn), lambda i, j, kk: (j, i, 0))
        out_shape = jax.ShapeDtypeStruct((N_CHIPS, m, tn), out_dtype)
    else:
        out_spec = pl.BlockSpec((tm, tn), lambda i, j, kk: (i, j))
        out_shape = jax.ShapeDtypeStruct((m, n), out_dtype)
    res, cres = carried_call(name, body, (m // tm, n // tn, nk), in_specs, [out_spec], [out_shape], [pltpu.VMEM((tm, tn), F32)],
                             ("parallel", "parallel", "arbitrary"), args, comm)
    return res[0] if comm is None else (res[0], cres)


def _cb(cb):
    return cb if callable(cb) else (lambda j, _c=cb: _c)


def tiled_call(name, fn, s, tm, ins, outs, ncol=1):
    nrow = s // tm
    in_specs, args, kinds = [], [], []
    for it in ins:
        kind = it[0]
        if kind == 'row':
            _, arr, w, cb = it
            in_specs.append(pl.BlockSpec((tm, w), lambda j, i, _c=_cb(cb): (i, _c(j))))
            args.append(arr)
            kinds.append(('row',))
        elif kind == 'halo':
            _, arr, w, cb, h = it
            r = tm // h
            nh = s // h
            in_specs.append(pl.BlockSpec((h, w), lambda j, i, _c=_cb(cb), _r=r: (jnp.maximum(i * _r - 1, 0), _c(j))))
            in_specs.append(pl.BlockSpec((tm, w), lambda j, i, _c=_cb(cb): (i, _c(j))))
            in_specs.append(pl.BlockSpec((h, w), lambda j, i, _c=_cb(cb), _r=r, _n=nh: (jnp.minimum((i + 1) * _r, _n - 1), _c(j))))
            args += [arr, arr, arr]
            kinds.append(('halo',))
        elif kind == 'full':
            arr = it[1]
            in_specs.append(pl.BlockSpec(arr.shape, lambda j, i, _n=arr.ndim: (0,) * _n))
            args.append(arr)
            kinds.append(('ref',))
        elif kind == 'col':
            _, arr, w, cb = it
            in_specs.append(pl.BlockSpec((arr.shape[0], w), lambda j, i, _c=_cb(cb): (0, _c(j))))
            args.append(arr)
            kinds.append(('ref',))
        elif kind == 'hm':
            arr = it[1]
            in_specs.append(pl.BlockSpec((arr.shape[0], tm, arr.shape[2]), lambda j, i: (0, i, 0)))
            args.append(arr)
            kinds.append(('ref',))
        else:
            raise ValueError(kind)
    out_specs, out_shapes, okinds, into = [], [], [], []
    for ot in outs:
        kind = ot[0]
        if kind == 'row':
            _, wt, w, cb, dt = ot
            out_specs.append(pl.BlockSpec((tm, w), lambda j, i, _c=_cb(cb): (i, _c(j))))
            out_shapes.append(jax.ShapeDtypeStruct((s, wt), dt))
            okinds.append('row')
        elif kind == 'hm':
            _, hh, d, dt = ot
            out_specs.append(pl.BlockSpec((hh, tm, d), lambda j, i: (0, i, 0)))
            out_shapes.append(jax.ShapeDtypeStruct((hh, s, d), dt))
            okinds.append('hm')
        elif kind == 'acc':
            shape = ot[1]
            out_specs.append(pl.BlockSpec(shape, lambda j, i, _n=len(shape): (0,) * _n))
            out_shapes.append(jax.ShapeDtypeStruct(shape, F32))
            okinds.append('acc')
        elif kind == 'colacc':
            _, r, wt, w, cb = ot
            out_specs.append(pl.BlockSpec((r, w), lambda j, i, _c=_cb(cb): (0, _c(j))))
            out_shapes.append(jax.ShapeDtypeStruct((r, wt), F32))
            okinds.append('colacc')
        elif kind == 'row_into':
            _, buf, w, cb = ot
            out_specs.append(pl.BlockSpec((tm, w), lambda j, i, _c=_cb(cb): (i, _c(j))))
            out_shapes.append(jax.ShapeDtypeStruct(buf.shape, buf.dtype))
            okinds.append('row')
            into.append((len(out_specs) - 1, buf))
        elif kind == 'row3':
            _, n0, wt, w, cb, dt = ot
            out_specs.append(pl.BlockSpec((n0, tm, w), lambda j, i, _c=_cb(cb): (0, i, _c(j))))
            out_shapes.append(jax.ShapeDtypeStruct((n0, s, wt), dt))
            okinds.append('row')
        else:
            raise ValueError(kind)
    n_in = len(in_specs)
    aliases = {n_in + e: o for e, (o, _) in enumerate(into)}
    in_specs = in_specs + [pl.BlockSpec(memory_space=pl.ANY)] * len(into)
    args = args + [buf for _, buf in into]

    def body(*refs):
        j = pl.program_id(0)
        i = pl.program_id(1)
        in_refs, out_refs = refs[:n_in], refs[n_in + len(into):]
        items, p = [], 0
        for kd in kinds:
            if kd[0] == 'row':
                items.append(in_refs[p][...])
                p += 1
            elif kd[0] == 'halo':
                prev, cen, nxt = in_refs[p][...], in_refs[p + 1][...], in_refs[p + 2][...]
                prev = jnp.where(i == 0, jnp.zeros_like(prev), prev)
                nxt = jnp.where(i == nrow - 1, jnp.zeros_like(nxt), nxt)
                items.append(jnp.concatenate([prev, cen, nxt], axis=0))
                p += 3
            else:
                items.append(in_refs[p])
                p += 1
        for o_ref, kd in zip(out_refs, okinds):
            if kd == 'acc':
                @pl.when((i == 0) & (j == 0))
                def _(o_ref=o_ref):
                    o_ref[...] = jnp.zeros_like(o_ref)
            elif kd == 'colacc':
                @pl.when(i == 0)
                def _(o_ref=o_ref):
                    o_ref[...] = jnp.zeros_like(o_ref)
        fn(i, j, *items, *out_refs)

    res = pl.pallas_call(
        body, name=name, grid=(ncol, nrow), in_specs=in_specs, out_specs=out_specs, out_shape=out_shapes,
        input_output_aliases=aliases, compiler_params=_cparams(("arbitrary", "arbitrary")),
    )(*args)
    return res


def _sigmoid(x):
    return 1.0 / (1.0 + jnp.exp(-x))


def _ln_stats(x):
    mu = jnp.mean(x, axis=1, keepdims=True)
    xc = x - mu
    var = jnp.mean(xc * xc, axis=1, keepdims=True)
    r = lax.rsqrt(var + LN_EPS)
    return xc * r, r


def _ln_bwd(dy, xh, r, g):
    dxh = dy * g
    dx = r * (dxh - jnp.mean(dxh, axis=1, keepdims=True) - xh * jnp.mean(dxh * xh, axis=1, keepdims=True))
    return dx, jnp.sum(dy * xh, axis=0, keepdims=True), jnp.sum(dy, axis=0, keepdims=True)


def _gmean(v, gm):
    hi = v.astype(BF16)
    lo = (v - hi.astype(F32)).astype(BF16)
    return jnp.dot(hi, gm, preferred_element_type=F32) + jnp.dot(lo, gm, preferred_element_type=F32)


def _roll_l(x, sh):
    return pltpu.roll(x, sh % x.shape[1], 1)


def _rope(x, c, sa, sb, sh):
    return x * c + _roll_l(x, -sh) * sa + _roll_l(x, sh) * sb


def _rope_t(dy, c, sa, sb, sh):
    return dy * c + _roll_l(dy * sa, sh) + _roll_l(dy * sb, -sh)


def _roll_r(x, s):
    return pltpu.roll(x, (-s) % x.shape[0], 0)


_GELU_C = math.sqrt(2.0 / math.pi)


def _gelu(x):
    t = jnp.tanh(_GELU_C * (x + 0.044715 * x * x * x))
    return 0.5 * x * (1.0 + t), t


def _gelu_grad(x, t):
    return 0.5 * (1.0 + t) + 0.5 * x * (1.0 - t * t) * _GELU_C * (1.0 + 3.0 * 0.044715 * x * x)


def _dot_nt(a, b):
    return lax.dot_general(a, b, (((1,), (1,)), ((), ())), preferred_element_type=F32)


def _dot_tn(a, b):
    return lax.dot_general(a, b, (((0,), (0,)), ((), ())), preferred_element_type=F32)


def _rope_block(s, d):
    t = jnp.arange(s)
    row = (t // GRID_W).astype(F32)
    col = (t % GRID_W).astype(F32)
    half = d // 4
    inv = ROPE_THETA ** (-jnp.arange(half, dtype=F32) / half)
    z = jnp.zeros((s, half), F32)
    cs, sas, sbs = [], [], []
    for pos in (row, col):
        ang = pos[:, None] * inv[None, :]
        co, si = jnp.cos(ang), jnp.sin(ang)
        cs += [co, co]
        sas += [-si, z]
        sbs += [z, si]
    return tuple(jnp.concatenate(v, axis=1) for v in (cs, sas, sbs))


def _rope_tables(s):
    gqa = tuple(jnp.tile(a, (1, 4)) for a in _rope_block(s, HEAD_DIM))
    c32, sa32, sb32 = _rope_block(s, MLA_ROPE)

    def head128(c, fill):
        return jnp.concatenate([jnp.full((s, 64), fill, F32), c, jnp.zeros((s, 32), F32)], axis=1)

    mla = tuple(jnp.tile(a, (1, 4)) for a in (head128(c32, 1.0), head128(sa32, 0.0), head128(sb32, 0.0)))
    kr = (head128(c32, 0.0), head128(sa32, 0.0), head128(sb32, 0.0))
    return dict(gqa=gqa, mla=mla, kr=kr)


def _group_mean_matrix(w, g):
    idx = np.arange(w) // g
    return jnp.asarray((idx[:, None] == idx[None, :]).astype(np.float32) / g, dtype=BF16)


TM = 512


def ln_fwd(name, s, x, g, b, res=None, alpha=1.0):
    ins = [('row', x, D_MODEL, 0)] + ([('row', res, D_MODEL, 0)] if res is not None else []) + [('full', g), ('full', b)]

    def fn(i, j, *a):
        if res is not None:
            xv, rv, g_ref, b_ref, o_ref = a
            z = alpha * rv + xv
        else:
            xv, g_ref, b_ref, o_ref = a
            z = xv
        xh, _ = _ln_stats(z)
        o_ref[...] = xh * g_ref[...] + b_ref[...]

    return tiled_call(name, fn, s, TM, ins, [('row', D_MODEL, D_MODEL, 0, F32)])[0]


def ln_bwd(name, s, dy, x, g, res=None, alpha=1.0):
    ins = [('row', dy, D_MODEL, 0), ('row', x, D_MODEL, 0)]
    ins += ([('row', res, D_MODEL, 0)] if res is not None else []) + [('full', g)]

    def fn(i, j, *a):
        a = list(a)
        dyv = a.pop(0)
        xv = a.pop(0)
        z = alpha * a.pop(0) + xv if res is not None else xv
        g_ref, dz_ref, dg_ref, db_ref = a
        xh, r = _ln_stats(z)
        dz, dg, db = _ln_bwd(dyv, xh, r, g_ref[...])
        dz_ref[...] = dz
        dg_ref[...] += dg
        db_ref[...] += db

    return tiled_call(name, fn, s, TM, ins, [('row', D_MODEL, D_MODEL, 0, F32), ('acc', (1, D_MODEL)), ('acc', (1, D_MODEL))])


def loss_and_ln_bwd(name, s, x, res, g, b, tgt, alpha):
    ins = [('row', x, D_MODEL, 0), ('row', res, D_MODEL, 0), ('row', tgt, D_MODEL, 0), ('full', g), ('full', b)]

    def fn(i, j, xv, rv, tv, g_ref, b_ref, dz_ref, dg_ref, db_ref, loss_ref):
        xh, r = _ln_stats(alpha * rv + xv)
        y = xh * g_ref[...] + b_ref[...]
        e = y - tv
        dz, dg, db = _ln_bwd(e * (1.0 / D_MODEL), xh, r, g_ref[...])
        dz_ref[...] = dz
        dg_ref[...] += dg
        db_ref[...] += db
        loss_ref[...] += jnp.sum(jnp.sum(e * e, axis=0, keepdims=True), axis=1, keepdims=True) * (0.5 / D_MODEL)

    return tiled_call(name, fn, s, TM, ins, [('row', D_MODEL, D_MODEL, 0, F32), ('acc', (1, D_MODEL)), ('acc', (1, D_MODEL)),
                                             ('acc', (1, 128))])


HALO_A = 16


def _glu(a_in):
    return a_in[:, :GROUP_W] * _sigmoid(a_in[:, GROUP_W:])


def _row_windows(x, halo, tm):
    rolled = {0: x}

    def window(s):
        b = s % 8
        if b not in rolled:
            rolled[b] = _roll_r(x, b)
        off = halo + s - b
        return rolled[b][off:off + tm]

    return window


def a_fwd(name, s, proj, cw, cb, g, b):
    def fn(i, j, a_ext, cw_ref, cb_ref, g_ref, b_ref, y_ref, o_ref):
        a_at = _row_windows(_glu(a_ext), HALO_A, TM)
        acc = jnp.zeros((TM, GROUP_W), F32)
        for k in range(CONV_A_WIDTH):
            acc = acc + a_at(k - 15) * cw_ref[pl.ds(k, 1), :]
        y = acc + cb_ref[...]
        y_ref[...] = y
        xh, _ = _ln_stats(y)
        z = xh * g_ref[...] + b_ref[...]
        o_ref[...] = z * _sigmoid(z)

    ins = [('halo', proj, 2 * GROUP_W, 0, HALO_A), ('full', cw), ('full', cb), ('full', g), ('full', b)]
    return tiled_call(name, fn, s, TM, ins, [('row', GROUP_W, GROUP_W, 0, F32), ('row', GROUP_W, GROUP_W, 0, F32)])


def a_bwd1(name, s, y, dmix, g, b):
    def fn(i, j, yv, do, g_ref, b_ref, dy_ref, dg_ref, db_ref, dcb_ref):
        xh, r = _ln_stats(yv)
        z = xh * g_ref[...] + b_ref[...]
        sg = _sigmoid(z)
        dz = do * sg * (1.0 + z * (1.0 - sg))
        dy, dg, db = _ln_bwd(dz, xh, r, g_ref[...])
        dy_ref[...] = dy
        dg_ref[...] += dg
        db_ref[...] += db
        dcb_ref[...] += jnp.sum(dy, axis=0, keepdims=True)

    ins = [('row', y, GROUP_W, 0), ('row', dmix, GROUP_W, 0), ('full', g), ('full', b)]
    return tiled_call(name, fn, s, TM, ins, [('row', GROUP_W, GROUP_W, 0, F32), ('acc', (1, GROUP_W)), ('acc', (1, GROUP_W)),
                                             ('acc', (1, GROUP_W))])


def a_bwd2(name, s, proj, dy, cw):
    def fn(i, j, a_ext, dy_ext, cw_ref, da_ref, dcw_ref):
        a1, a2 = a_ext[:, :GROUP_W], a_ext[:, GROUP_W:]
        sg = _sigmoid(a2)
        a = a1 * sg
        dyc = dy_ext[HALO_A:HALO_A + TM]
        dy_at, a_at = _row_windows(dy_ext, HALO_A, TM), _row_windows(a, HALO_A, TM)
        da = jnp.zeros((TM, GROUP_W), F32)
        for k in range(CONV_A_WIDTH):
            da = da + dy_at(15 - k) * cw_ref[pl.ds(k, 1), :]
            dcw_ref[pl.ds(k, 1), :] += jnp.sum(dyc * a_at(k - 15), axis=0, keepdims=True)
        a1c, sgc = a1[HALO_A:HALO_A + TM], sg[HALO_A:HALO_A + TM]
        da_ref[...] = jnp.concatenate([da * sgc, da * a1c * sgc * (1.0 - sgc)], axis=1).astype(BF16)

    ins = [('halo', proj, 2 * GROUP_W, 0, HALO_A), ('halo', dy, GROUP_W, 0, HALO_A), ('full', cw)]
    return tiled_call(name, fn, s, TM, ins, [('row', PROJ_W, 2 * GROUP_W, 0, BF16), ('acc', (32, GROUP_W))])


def b_prep_fwd(name, s, proj, gq, gk, tabs, gm):
    c, sa, sb = tabs

    def fn(i, j, qkv, cv, sav, sbv, gq_ref, gk_ref, gm_ref, q_ref, k_ref, v_ref):
        q, k, v = qkv[:, :256], qkv[:, 256:384], qkv[:, 384:]
        gmv = gm_ref[...]
        qn = q * lax.rsqrt(_gmean(q * q, gmv) + RMS_EPS) * gq_ref[...]
        kn = k * lax.rsqrt(_gmean(k * k, gmv[:128, :128]) + RMS_EPS) * gk_ref[...]
        qr = _rope(qn, cv, sav, sbv, 16)
        kr = _rope(kn, cv[:, :128], sav[:, :128], sbv[:, :128], 16)
        for h in range(4):
            q_ref[h] = (qr[:, h * 64:(h + 1) * 64] * QSCALE_GQA).astype(BF16)
        for h in range(2):
            k_ref[h] = kr[:, h * 64:(h + 1) * 64].astype(BF16)
            v_ref[h] = v[:, h * 64:(h + 1) * 64].astype(BF16)

    ins = [('row', proj, 512, 1), ('row', c, 256, 0), ('row', sa, 256, 0), ('row', sb, 256, 0), ('full', gq), ('full', gk),
           ('full', gm)]
    return tiled_call(name, fn, s, TM, ins, [('hm', 4, 64, BF16), ('hm', 2, 64, BF16), ('hm', 2, 64, BF16)])


def b_prep_bwd(name, s, proj, gq, gk, tabs, gm, dq_hm, dk_hm, dv_hm, dproj):
    c, sa, sb = tabs

    def fn(i, j, qkv, cv, sav, sbv, gq_ref, gk_ref, gm_ref, dq_ref, dk_ref, dv_ref, dp_ref, dgq_ref, dgk_ref):
        q, k = qkv[:, :256], qkv[:, 256:384]
        gmv = gm_ref[...]
        dqr = jnp.concatenate([dq_ref[h] for h in range(4)], axis=1) * QSCALE_GQA
        dkr = jnp.concatenate([dk_ref[h] for h in range(2)], axis=1)
        dv = jnp.concatenate([dv_ref[h] for h in range(2)], axis=1)
        outs = []
        for x, dxr, g_ref, gmx, w, dg_ref in ((q, dqr, gq_ref, gmv, 256, dgq_ref), (k, dkr, gk_ref, gmv[:128, :128], 128, dgk_ref)):
            dn = _rope_t(dxr, cv[:, :w], sav[:, :w], sbv[:, :w], 16)
            r = lax.rsqrt(_gmean(x * x, gmx) + RMS_EPS)
            gv = g_ref[...]
            dx = gv * r * dn - x * (r * r * r) * _gmean(x * gv * dn, gmx)
            dgt = jnp.sum(x * r * dn, axis=0, keepdims=True)
            dg = dgt[:, 0:64]
            for h in range(1, w // 64):
                dg = dg + dgt[:, h * 64:(h + 1) * 64]
            dg_ref[...] += dg
            outs.append(dx)
        dp_ref[...] = jnp.concatenate(outs + [dv], axis=1).astype(BF16)

    ins = [('row', proj, 512, 1), ('row', c, 256, 0), ('row', sa, 256, 0), ('row', sb, 256, 0), ('full', gq), ('full', gk),
           ('full', gm), ('hm', dq_hm), ('hm', dk_hm), ('hm', dv_hm)]
    return tiled_call(name, fn, s, TM, ins, [('row_into', dproj, 512, 1), ('acc', (1, 64)), ('acc', (1, 64))])


TQ_BWD = 512
TQ_FWD = 256


LOG2E = 1.4426950408889634
QSCALE_GQA = HEAD_DIM ** -0.5 * LOG2E
QSCALE_MLA = (MLA_NOPE + MLA_ROPE) ** -0.5 * LOG2E


def attn_fwd(name, s, q_hm, k_hm, v_hm, comm=None):
    dk = q_hm.shape[2]
    nkv = k_hm.shape[0]
    tq = TQ_FWD

    def body(q_ref, k_ref, v_ref, o_ref, l_ref):
        outs = []
        for g in range(4):
            kv = g * nkv // 4
            sc = _dot_nt(q_ref[g], k_ref[kv])
            m = jnp.max(sc, axis=1, keepdims=True)
            p = jnp.exp2(sc - m)
            l = jnp.sum(p, axis=1, keepdims=True)
            o = jnp.dot(p.astype(BF16), v_ref[kv], preferred_element_type=F32)
            outs.append(o / l)
            l_ref[g] = m + jnp.log2(l)
        o_ref[...] = jnp.concatenate(outs, axis=1)

    res, cres = carried_call(
        name, body, (s // tq,),
        [pl.BlockSpec((4, tq, dk), lambda i: (0, i, 0)),
         pl.BlockSpec((nkv, s, dk), lambda i: (0, 0, 0)),
         pl.BlockSpec((nkv, s, 64), lambda i: (0, 0, 0))],
        [pl.BlockSpec((tq, 256), lambda i: (i, 0)), pl.BlockSpec((4, tq, 1), lambda i: (0, i, 0))],
        [jax.ShapeDtypeStruct((s, 256), F32), jax.ShapeDtypeStruct((4, s, 1), F32)], [],
        ("arbitrary",), [q_hm, k_hm, v_hm], comm)
    return res if comm is None else (res, cres)


def attn_bwd(name, s, q_hm, k_hm, v_hm, o, lse, dmixcat, dcol, comm=None):
    dk = q_hm.shape[2]
    nkv = k_hm.shape[0]
    kpb = nkv // 2

    def body(q_ref, k_ref, v_ref, o_ref, l_ref, do_ref, dq_ref, dk_ref, dv_ref):
        @pl.when(pl.program_id(1) == 0)
        def _():
            dk_ref[...] = jnp.zeros_like(dk_ref)
            dv_ref[...] = jnp.zeros_like(dv_ref)

        dob, ob = do_ref[...], o_ref[...]
        for g in range(2):
            kv = g if kpb == 2 else 0
            do, ov = dob[:, 64 * g:64 * (g + 1)], ob[:, 64 * g:64 * (g + 1)]
            q, k, v = q_ref[g], k_ref[kv], v_ref[kv]
            p = jnp.exp2(_dot_nt(q, k) - l_ref[g])
            dp = _dot_nt(do.astype(BF16), v)
            delta = jnp.sum(do * ov, axis=1, keepdims=True)
            ds = (p * (dp - delta) * (1.0 / LOG2E)).astype(BF16)
            dq_ref[g] = jnp.dot(ds, k, preferred_element_type=F32)
            dk_ref[kv] += _dot_tn(ds, q)
            dv_ref[kv] += _dot_tn(p.astype(BF16), do.astype(BF16))

    tq = TQ_BWD if s % TQ_BWD == 0 else TQ_FWD
    res, cres = carried_call(
        name, body, (2, s // tq),
        [pl.BlockSpec((2, tq, dk), lambda p, i: (p, i, 0)),
         pl.BlockSpec((kpb, s, dk), lambda p, i: (p, 0, 0)), pl.BlockSpec((kpb, s, 64), lambda p, i: (p, 0, 0)),
         pl.BlockSpec((tq, 128), lambda p, i: (i, p)),
         pl.BlockSpec((2, tq, 1), lambda p, i: (p, i, 0)),
         pl.BlockSpec((tq, 128), lambda p, i: (i, dcol + p))],
        [pl.BlockSpec((2, tq, dk), lambda p, i: (p, i, 0)),
         pl.BlockSpec((kpb, s, dk), lambda p, i: (p, 0, 0)), pl.BlockSpec((kpb, s, 64), lambda p, i: (p, 0, 0))],
        [jax.ShapeDtypeStruct((4, s, dk), F32), jax.ShapeDtypeStruct((nkv, s, dk), F32), jax.ShapeDtypeStruct((nkv, s, 64), F32)],
        [], ("arbitrary", "arbitrary"), [q_hm, k_hm, v_hm, o, lse, dmixcat], comm)
    return res if comm is None else (res, cres)


def _lane_group_masks(w, g):
    lane = lax.broadcasted_iota(jnp.int32, (1, w), 1)
    return [((lane >= k * g) & (lane < (k + 1) * g)).astype(F32) for k in range(w // g)]


def _sgu_gate(svn, w_ref, bias):
    masks = _lane_group_masks(GROUP_W, 64)
    outs = []
    for n in range(TM // CHUNK):
        x = svn[n * CHUNK:(n + 1) * CHUNK]
        acc = bias
        for g in range(4):
            acc = acc + jnp.dot(w_ref[g].astype(BF16), (x * masks[g]).astype(BF16), preferred_element_type=F32)
        outs.append(acc)
    return jnp.concatenate(outs, axis=0)


def c_fwd(name, s, proj, g, b, w, bias):
    def fn(i, j, cin, g_ref, b_ref, w_ref, bias_ref, o_ref):
        c, _ = _gelu(cin)
        xh, _ = _ln_stats(c[:, GROUP_W:])
        svn = xh * g_ref[...] + b_ref[...]
        o_ref[...] = c[:, :GROUP_W] * _sgu_gate(svn, w_ref, bias_ref[...])

    ins = [('row', proj, 512, 2), ('full', g), ('full', b), ('full', w), ('full', bias)]
    return tiled_call(name, fn, s, TM, ins, [('row', GROUP_W, GROUP_W, 0, F32)])[0]


def c_bwd(name, s, proj, g, b, w, wt, bias, dmixcat, dproj):
    def fn(i, j, cin, do, g_ref, b_ref, w_ref, wt_ref, bias_ref, dc_ref, dg_ref, db_ref, dw_ref, dbias_ref):
        c, t = _gelu(cin)
        u = c[:, :GROUP_W]
        xh, r = _ln_stats(c[:, GROUP_W:])
        svn = xh * g_ref[...] + b_ref[...]
        gate = _sgu_gate(svn, w_ref, bias_ref[...])
        du = do * gate
        dgate = do * u
        masks = _lane_group_masks(GROUP_W, 64)
        col = lax.broadcasted_iota(jnp.int32, (CHUNK, 128), 1)
        dsvn, dbias = [], jnp.zeros((CHUNK, 128), F32)
        for n in range(TM // CHUNK):
            dgc = dgate[n * CHUNK:(n + 1) * CHUNK]
            x = svn[n * CHUNK:(n + 1) * CHUNK]
            acc = jnp.zeros((CHUNK, GROUP_W), F32)
            for gi in range(4):
                dgm = (dgc * masks[gi]).astype(BF16)
                acc = acc + jnp.dot(wt_ref[gi].astype(BF16), dgm, preferred_element_type=F32)
                dw_ref[gi] += _dot_nt(dgm, (x * masks[gi]).astype(BF16))
                dbias = dbias + jnp.where(col == gi, jnp.sum(dgc * masks[gi], axis=1, keepdims=True), 0.0)
            dsvn.append(acc)
        dbias_ref[...] += dbias
        dsv, dg, db = _ln_bwd(jnp.concatenate(dsvn, axis=0), xh, r, g_ref[...])
        dg_ref[...] += dg
        db_ref[...] += db
        dc_ref[...] = (jnp.concatenate([du, dsv], axis=1) * _gelu_grad(cin, t)).astype(BF16)

    ins = [('row', proj, 512, 2), ('row', dmixcat, GROUP_W, 2), ('full', g), ('full', b), ('full', w), ('full', wt), ('full', bias)]
    return tiled_call(name, fn, s, TM, ins, [('row_into', dproj, 512, 2), ('acc', (1, GROUP_W)), ('acc', (1, GROUP_W)),
                                             ('acc', (4, CHUNK, CHUNK)), ('acc', (CHUNK, 128))])


def _d_common(dpart, gq_ref, gkv_ref):
    cq, ckv, kr = dpart[:, :256], dpart[:, 256:384], dpart[:, 384:]
    rq = lax.rsqrt(jnp.sum(cq * cq, axis=1, keepdims=True) * (1.0 / MLA_Q_LORA) + RMS_EPS)
    rkv = lax.rsqrt(jnp.mean(ckv * ckv, axis=1, keepdims=True) + RMS_EPS)
    return cq, ckv, kr, rq, rkv, cq * rq * gq_ref[...], ckv * rkv * gkv_ref[...]


def d_prep_fwd(name, s, proj, gq, gkv, wuq, wukv, tm_, tkr):
    cm, sam, sbm = tm_
    ck, sak, sbk = tkr

    def fn(i, j, dpart, cmv, samv, sbmv, ckv_, sakv, sbkv, gq_ref, gkv_ref, wuq_ref, wukv_ref, q_ref, k_ref, v_ref):
        cq, ckv, kr, rq, rkv, cqn, ckvn = _d_common(dpart, gq_ref, gkv_ref)
        qd = jnp.dot(cqn.astype(BF16), wuq_ref[...], preferred_element_type=F32)
        qf = _rope(qd, cmv, samv, sbmv, 8)
        kvd = jnp.dot(ckvn.astype(BF16), wukv_ref[...], preferred_element_type=F32)
        krr = _rope(kr, ckv_, sakv, sbkv, 8)
        for h in range(4):
            q_ref[h] = (qf[:, h * 128:(h + 1) * 128] * QSCALE_MLA).astype(BF16)
            k_ref[h] = (kvd[:, h * 128:(h + 1) * 128] + krr).astype(BF16)
            v_ref[h] = kvd[:, 512 + h * 64:512 + (h + 1) * 64].astype(BF16)

    ins = [('row', proj, 512, 3), ('row', cm, 512, 0), ('row', sam, 512, 0), ('row', sbm, 512, 0), ('row', ck, 128, 0),
           ('row', sak, 128, 0), ('row', sbk, 128, 0), ('full', gq), ('full', gkv), ('full', wuq), ('full', wukv)]
    return tiled_call(name, fn, s, TM, ins, [('hm', 4, 128, BF16), ('hm', 4, 128, BF16), ('hm', 4, 64, BF16)])


def d_prep_bwd(name, s, proj, gq, gkv, wuq, wukv, tm_, tkr, dq_hm, dk_hm, dv_hm, dproj):
    cm, sam, sbm = tm_
    ck, sak, sbk = tkr

    def fn(i, j, dpart, cmv, samv, sbmv, ckv_, sakv, sbkv, gq_ref, gkv_ref, wuq_ref, wukv_ref, dq_ref, dk_ref, dv_ref,
           dd_ref, dgq_ref, dgkv_ref, dwuq_ref, dwukv_ref):
        cq, ckv, kr, rq, rkv, cqn, ckvn = _d_common(dpart, gq_ref, gkv_ref)
        dqf = jnp.concatenate([dq_ref[h] for h in range(4)], axis=1) * QSCALE_MLA
        dkf = [dk_ref[h] for h in range(4)]
        dqd = _rope_t(dqf, cmv, samv, sbmv, 8).astype(BF16)
        dkvd = jnp.concatenate(dkf + [dv_ref[h] for h in range(4)], axis=1).astype(BF16)
        dkr = _rope_t(dkf[0] + dkf[1] + dkf[2] + dkf[3], ckv_, sakv, sbkv, 8)
        dcqn = _dot_nt(dqd, wuq_ref[...])
        dckvn = _dot_nt(dkvd, wukv_ref[...])
        dwuq_ref[...] += _dot_tn(cqn.astype(BF16), dqd)
        dwukv_ref[...] += _dot_tn(ckvn.astype(BF16), dkvd)
        gqv, gkvv = gq_ref[...], gkv_ref[...]
        dcq = gqv * rq * dcqn - cq * (rq * rq * rq) * (jnp.sum(cq * gqv * dcqn, axis=1, keepdims=True) * (1.0 / MLA_Q_LORA))
        dckv = gkvv * rkv * dckvn - ckv * (rkv * rkv * rkv) * jnp.mean(ckv * gkvv * dckvn, axis=1, keepdims=True)
        dgq_ref[...] += jnp.sum(cq * rq * dcqn, axis=0, keepdims=True)
        dgkv_ref[...] += jnp.sum(ckv * rkv * dckvn, axis=0, keepdims=True)
        dd_ref[...] = jnp.concatenate([dcq, dckv, dkr], axis=1).astype(BF16)

    ins = [('row', proj, 512, 3), ('row', cm, 512, 0), ('row', sam, 512, 0), ('row', sbm, 512, 0), ('row', ck, 128, 0),
           ('row', sak, 128, 0), ('row', sbk, 128, 0), ('full', gq), ('full', gkv), ('full', wuq), ('full', wukv),
           ('hm', dq_hm), ('hm', dk_hm), ('hm', dv_hm)]
    return tiled_call(name, fn, s, TM, ins, [('row_into', dproj, 512, 3), ('acc', (1, 256)), ('acc', (1, 128)),
                                             ('acc', (256, 512)), ('acc', (128, 768))])


HALO_F = 8
TN_F = 1408
TM_F = 512
TM_FB, TN_FB = 512, 256


def _conv3(ext, w_ref):
    return _roll_r(ext, -1) * w_ref[pl.ds(0, 1), :] + ext * w_ref[pl.ds(1, 1), :] + _roll_r(ext, 1) * w_ref[pl.ds(2, 1), :]


def f_fwd(name, s, upraw, cw, cb):
    ncj = D_FF // TN_F

    def fn(i, j, xa, xg, wa_ref, wg_ref, ba_ref, bg_ref, o_ref):
        ua = (_conv3(xa, wa_ref) + ba_ref[...])[HALO_F:HALO_F + TM_F]
        ug = (_conv3(xg, wg_ref) + bg_ref[...])[HALO_F:HALO_F + TM_F]
        o_ref[...] = (ua * _sigmoid(ua) * ug).astype(BF16)

    ins = [('halo', upraw, TN_F, lambda j: j, HALO_F), ('halo', upraw, TN_F, lambda j: j + ncj, HALO_F),
           ('col', cw, TN_F, lambda j: j), ('col', cw, TN_F, lambda j: j + ncj),
           ('col', cb, TN_F, lambda j: j), ('col', cb, TN_F, lambda j: j + ncj)]
    return tiled_call(name, fn, s, TM_F, ins, [('row', D_FF, TN_F, lambda j: j, BF16)], ncol=ncj)[0]


def f_bwd(name, s, upraw, dact, cw, cb):
    ncj = D_FF // TN_FB

    def fn(i, j, xa, xg, da, wa_ref, wg_ref, ba_ref, bg_ref, dx_ref, dwa_ref, dwg_ref, dba_ref, dbg_ref):
        taps_a = (_roll_r(xa, -1), xa, _roll_r(xa, 1))
        taps_g = (_roll_r(xg, -1), xg, _roll_r(xg, 1))
        ua = sum(t * wa_ref[pl.ds(k, 1), :] for k, t in enumerate(taps_a)) + ba_ref[...]
        ug = sum(t * wg_ref[pl.ds(k, 1), :] for k, t in enumerate(taps_g)) + bg_ref[...]
        sg = _sigmoid(ua)
        dug = da * ua * sg
        dua = da * ug * sg * (1.0 + ua * (1.0 - sg))
        cen = slice(HALO_F, HALO_F + TM_FB)
        for half, (du, taps, w_ref, dw_ref, db_ref) in enumerate(((dua, taps_a, wa_ref, dwa_ref, dba_ref),
                                                                   (dug, taps_g, wg_ref, dwg_ref, dbg_ref))):
            dx = _roll_r(du, 1) * w_ref[pl.ds(0, 1), :] + du * w_ref[pl.ds(1, 1), :] + _roll_r(du, -1) * w_ref[pl.ds(2, 1), :]
            dx_ref[half] = dx[cen].astype(BF16)
            duc = du[cen]
            for k in range(3):
                dw_ref[pl.ds(k, 1), :] += jnp.sum(duc * taps[k][cen], axis=0, keepdims=True)
            db_ref[...] += jnp.sum(duc, axis=0, keepdims=True)

    tn = TN_FB
    ins = [('halo', upraw, tn, lambda j: j, HALO_F), ('halo', upraw, tn, lambda j: j + ncj, HALO_F),
           ('halo', dact, tn, lambda j: j, HALO_F),
           ('col', cw, tn, lambda j: j), ('col', cw, tn, lambda j: j + ncj),
           ('col', cb, tn, lambda j: j), ('col', cb, tn, lambda j: j + ncj)]
    outs = [('row3', 2, D_FF, tn, lambda j: j, BF16),
            ('colacc', 8, D_FF, tn, lambda j: j), ('colacc', 8, D_FF, tn, lambda j: j),
            ('colacc', 1, D_FF, tn, lambda j: j), ('colacc', 1, D_FF, tn, lambda j: j)]
    return tiled_call(name, fn, s, TM_FB, ins, outs, ncol=ncj)


def _pad_w_in(w):
    z = lambda n: jnp.zeros((w.shape[0], n), w.dtype)
    return jnp.concatenate([w[:, :1728], z(64), w[:, 1728:1856], z(64), w[:, 1856:1888], z(32)], axis=1)


def _unpad_w_in(g):
    return jnp.concatenate([g[:, :1728], g[:, 1792:1920], g[:, 1984:2016]], axis=1)


def _pad_uq(w):
    w = w.reshape(MLA_Q_LORA, 4, 96)
    w = jnp.concatenate([w, jnp.zeros((MLA_Q_LORA, 4, 32), w.dtype)], axis=2).reshape(MLA_Q_LORA, 512)
    return jnp.concatenate([w, jnp.zeros((64, 512), w.dtype)], axis=0)


def _unpad_uq(g):
    return g[:MLA_Q_LORA].reshape(MLA_Q_LORA, 4, 128)[:, :, :96].reshape(MLA_Q_LORA, 384)


def _pad_ukv(w):
    w = w.reshape(MLA_KV_LORA, 4, 128)
    kn = jnp.concatenate([w[:, :, :64], jnp.zeros((MLA_KV_LORA, 4, 64), w.dtype)], axis=2).reshape(MLA_KV_LORA, 512)
    return jnp.concatenate([kn, w[:, :, 64:].reshape(MLA_KV_LORA, 256)], axis=1)


def _unpad_ukv(g):
    kn = g[:, :512].reshape(MLA_KV_LORA, 4, 128)[:, :, :64]
    v = g[:, 512:].reshape(MLA_KV_LORA, 4, 64)
    return jnp.concatenate([kn, v], axis=2).reshape(MLA_KV_LORA, 512)


def _row(v, pad_to=None):
    v = v.reshape(1, -1)
    if pad_to is not None and v.shape[1] < pad_to:
        v = jnp.concatenate([v, jnp.zeros((1, pad_to - v.shape[1]), v.dtype)], axis=1)
    return v


LATE_NAMES = ['w_out', 'ffn_w_up', 'ffn_w_down']


def _assemble(n, part, g, me):
    return jnp.concatenate([jnp.where(me == q, part, g[q]) for q in range(N_CHIPS)], axis=SHARDED_AXIS[n])


def local_step(x, tgt, W, dist=None):
    s = x.shape[0]
    tabs = _rope_tables(s)
    gm = _group_mean_matrix(256, 64)
    alpha = DEEPNORM_ALPHA
    acts = []
    late = {(n, l): W[n][l] for n in LATE_NAMES for l in range(DEPTH)} if dist is None else {}
    h = ln_fwd("ln_in", s, x, _row(W['ln_in_g']), _row(W['ln_in_b']))
    for l in range(DEPTH):
        A = dict(h=h)
        A['wpad'] = _pad_w_in(W['w_in'][l]).astype(BF16)
        A['wuq'] = _pad_uq(W['mla_w_uq'][l]).astype(BF16)
        A['wukv'] = _pad_ukv(W['mla_w_ukv'][l]).astype(BF16)
        A['cw'] = jnp.concatenate([W['conv_a_w'][l], jnp.zeros((1, GROUP_W), F32)], axis=0)
        A['fcw'] = jnp.concatenate([W['ffn_conv_w'][l], jnp.zeros((5, 2 * D_FF), F32)], axis=0)
        A['gq'] = jnp.tile(_row(W['qk_norm_q'][l]), (1, 4))
        A['gk'] = jnp.tile(_row(W['qk_norm_k'][l]), (1, 2))
        A['sgu_bias'] = jnp.repeat(W['sgu_b'][l].T, 64, axis=1)
        A['sgu_wt'] = jnp.swapaxes(W['sgu_w'][l], 1, 2)
        proj = mm(f"proj{l}", h, A['wpad'])
        A['proj'] = proj
        A['y_a'], o_a = a_fwd(f"a_fwd{l}", s, proj, A['cw'], _row(W['conv_a_b'][l]), _row(W['ln_a_g'][l]), _row(W['ln_a_b'][l]))
        A['bq'], A['bk'], A['bv'] = b_prep_fwd(f"b_prep{l}", s, proj, A['gq'], A['gk'], tabs['gqa'], gm)
        if dist is not None and l == 0:
            lp = dist['late']
            parts = [lp['w_out'], lp['ffn_w_up'][0:1], lp['ffn_w_down'][0:1]]
            nxt_names, nxt = dist['next']
            nxt_parts = [a[1:2] for a in nxt]
            wanted = [lp['w_out'], (lp['ffn_w_up'], 0), (lp['ffn_w_down'], 0)] + [(a, 1) for a in nxt]
            (A['o_b'], A['lse_b']), got = attn_fwd(f"b_attn{l}", s, A['bq'], A['bk'], A['bv'], comm=gather_over_chips(wanted))
            full = [_assemble(n, p, g, dist['me']) for n, p, g in zip(LATE_NAMES, parts, got)]
            late[('w_out', 0)], late[('w_out', 1)] = full[0][0], full[0][1]
            late[('ffn_w_up', 0)], late[('ffn_w_down', 0)] = full[1][0], full[2][0]
            for n, p, g in zip(nxt_names, nxt_parts, got[len(parts):]):
                W[n][1] = _assemble(n, p, g, dist['me'])[0]
        else:
            A['o_b'], A['lse_b'] = attn_fwd(f"b_attn{l}", s, A['bq'], A['bk'], A['bv'])
        o_c = c_fwd(f"c_fwd{l}", s, proj, _row(W['sgu_ln_g'][l]), _row(W['sgu_ln_b'][l]), W['sgu_w'][l], A['sgu_bias'])
        A['dq'], A['dk'], A['dv'] = d_prep_fwd(f"d_prep{l}", s, proj, _row(W['mla_q_norm'][l], 256), _row(W['mla_kv_norm'][l]),
                                               A['wuq'], A['wukv'], tabs['mla'], tabs['kr'])
        if dist is not None and l == 0:
            parts = [lp['ffn_w_up'][1:2], lp['ffn_w_down'][1:2]]
            (A['o_d'], A['lse_d']), got = attn_fwd(f"d_attn{l}", s, A['dq'], A['dk'], A['dv'],
                                                   comm=gather_over_chips([(lp['ffn_w_up'], 1), (lp['ffn_w_down'], 1)]))
            late[('ffn_w_up', 1)] = _assemble('ffn_w_up', parts[0], got[0], dist['me'])[0]
            late[('ffn_w_down', 1)] = _assemble('ffn_w_down', parts[1], got[1], dist['me'])[0]
        else:
            A['o_d'], A['lse_d'] = attn_fwd(f"d_attn{l}", s, A['dq'], A['dk'], A['dv'])
        A['wout'] = late[('w_out', l)].astype(BF16)
        A['wup'] = late[('ffn_w_up', l)].astype(BF16)
        A['wdown'] = late[('ffn_w_down', l)].astype(BF16)
        A['mixcat'] = jnp.concatenate([o_a, A['o_b'], o_c, A['o_d']], axis=1).astype(BF16)
        A['mix'] = mm(f"out_proj{l}", A['mixcat'], A['wout'])
        h1 = ln_fwd(f"ln_mix{l}", s, A['mix'], _row(W['ln_mix_g'][l]), _row(W['ln_mix_b'][l]), res=h, alpha=alpha)
        A['h1'] = h1
        A['upraw'] = mm(f"ffn_up{l}", h1, A['wup'])
        A['act'] = f_fwd(f"f_fwd{l}", s, A['upraw'], A['fcw'], _row(W['ffn_conv_b'][l]))
        A['f'] = mm(f"ffn_down{l}", A['act'], A['wdown'])
        if l < DEPTH - 1:
            h = ln_fwd(f"ln_ffn{l}", s, A['f'], _row(W['ln_ffn_g'][l]), _row(W['ln_ffn_b'][l]), res=h1, alpha=alpha)
        acts.append(A)

    per_layer = {n: [None] * DEPTH for n in WEIGHT_NAMES if n not in ('ln_in_g', 'ln_in_b')}
    dh = None
    loss_row = None
    overlap = dist is not None and DEPTH == 2
    red = {}
    grp1 = [(n, 1) for n in BIG_NAMES]
    grp2 = [('ffn_w_up', 0), ('ffn_w_down', 0)]

    def carried(fn, name, *a, comm=None, **kw):
        if comm is None:
            return fn(name, *a, **kw), None
        return fn(name, *a, comm=comm, **kw)

    def pair_sums(tag, xs, r1):
        return [pair_sum(f"pair_sum_{tag}{t}", xv, rv, dist['idx'], BF16) for t, (xv, rv) in enumerate(zip(xs, r1))]

    def chip_sums(tag, cs, ps):
        return [chip_sum(f"chip_sum_{tag}{t}", [cv], [pv], dist['idx']) for t, (cv, pv) in enumerate(zip(cs, ps))]

    for l in reversed(range(DEPTH)):
        A = acts[l]
        carry = overlap and l == 0
        g_ffn, b_ffn = _row(W['ln_ffn_g'][l]), _row(W['ln_ffn_b'][l])
        if l == DEPTH - 1:
            dz, dg, db, loss_row = loss_and_ln_bwd("loss", s, A['f'], A['h1'], g_ffn, b_ffn, tgt, alpha)
        else:
            dz, dg, db = ln_bwd(f"ln_ffn_bwd{l}", s, dh, A['f'], g_ffn, res=A['h1'], alpha=alpha)
        per_layer['ln_ffn_g'][l], per_layer['ln_ffn_b'][l] = dg[0], db[0]
        xs1 = [per_layer[n][k] for n, k in grp1] if carry else None
        dwd, r1 = carried(mm, f"dw_down{l}", A['act'], dz, ta=True, comm=pair_exchange(xs1) if carry else None)
        per_layer['ffn_w_down'][l] = dwd.reshape(N_CHIPS, D_FF // N_CHIPS, D_MODEL)
        ps1 = pair_sums("g1_", xs1, r1) if carry else None
        dact = mm(f"d_act{l}", dz, A['wdown'], tb=True)
        dup, dwa, dwg, dba, dbg = f_bwd(f"f_bwd{l}", s, A['upraw'], dact, A['fcw'], _row(W['ffn_conv_b'][l]))
        per_layer['ffn_conv_w'][l] = jnp.concatenate([dwa[:3], dwg[:3]], axis=1)
        per_layer['ffn_conv_b'][l] = jnp.concatenate([dba[0], dbg[0]], axis=0)
        per_layer['ffn_w_up'][l] = mm(f"dw_up{l}", A['h1'], dup, ta=True, chip_major=True, b_pair=True)
        xs2 = [per_layer[n][k] for n, k in grp2] if carry else None
        dh1_mm, r2 = carried(mm, f"d_h1{l}", dup, A['wup'], tb=True, a_pair=True, comm=pair_exchange(xs2) if carry else None)
        ps2 = pair_sums("g2_", xs2, r2) if carry else None
        dz1, dg, db = _ln_mix_bwd(l, s, dz, dh1_mm, A, W, alpha)
        per_layer['ln_mix_g'][l], per_layer['ln_mix_b'][l] = dg[0], db[0]
        per_layer['w_out'][l] = mm(f"dw_out{l}", A['mixcat'], dz1, ta=True).reshape(N_CHIPS, D_MODEL // N_CHIPS, D_MODEL)
        xs3 = [per_layer['w_out'][l]] if carry else None
        dmixcat, r3 = carried(mm, f"d_mixcat{l}", dz1, A['wout'], tb=True, comm=pair_exchange(xs3) if carry else None)
        if carry:
            ps2, grp2 = ps2 + pair_sums("g3_", xs3, r3), grp2 + [('w_out', 0)]
        dy_a, dg, db, dcb = a_bwd1(f"a_bwd1{l}", s, A['y_a'], dmixcat, _row(W['ln_a_g'][l]), _row(W['ln_a_b'][l]))
        per_layer['ln_a_g'][l], per_layer['ln_a_b'][l], per_layer['conv_a_b'][l] = dg[0], db[0], dcb[0]
        dproj, dcw = a_bwd2(f"a_bwd2{l}", s, A['proj'], dy_a, A['cw'])
        per_layer['conv_a_w'][l] = dcw[:CONV_A_WIDTH]
        (dq, dk, dv), cs1 = carried(attn_bwd, f"b_attn_bwd{l}", s, A['bq'], A['bk'], A['bv'], A['o_b'], A['lse_b'], dmixcat, 2,
                                    comm=chip_exchange(ps1) if carry else None)
        halves1 = chip_sums("g1_", cs1, ps1) if carry else None
        dproj, dgq, dgk = b_prep_bwd(f"b_prep_bwd{l}", s, A['proj'], A['gq'], A['gk'], tabs['gqa'], gm, dq, dk, dv, dproj)
        per_layer['qk_norm_q'][l], per_layer['qk_norm_k'][l] = dgq[0], dgk[0]
        dproj, dg, db, dsw, dsb = c_bwd(f"c_bwd{l}", s, A['proj'], _row(W['sgu_ln_g'][l]), _row(W['sgu_ln_b'][l]), W['sgu_w'][l],
                                        A['sgu_wt'], A['sgu_bias'], dmixcat, dproj)
        per_layer['sgu_ln_g'][l], per_layer['sgu_ln_b'][l] = dg[0], db[0]
        per_layer['sgu_w'][l], per_layer['sgu_b'][l] = dsw, dsb[:, :4].T
        comms = [half_exchange(halves1), chip_exchange(ps2)] if carry else []
        (dq, dk, dv), cres = carried(attn_bwd, f"d_attn_bwd{l}", s, A['dq'], A['dk'], A['dv'], A['o_d'], A['lse_d'], dmixcat, 6,
                                     comm=join_comms(comms))
        if carry:
            done1, cs2 = split_comm_results(comms, cres)
            red.update(zip(grp1, done1))
            halves2 = chip_sums("g2_", cs2, ps2)
        dproj, dgq, dgkv, dwuq, dwukv = d_prep_bwd(f"d_prep_bwd{l}", s, A['proj'], _row(W['mla_q_norm'][l], 256),
                                                   _row(W['mla_kv_norm'][l]), A['wuq'], A['wukv'], tabs['mla'], tabs['kr'],
                                                   dq, dk, dv, dproj)
        per_layer['mla_q_norm'][l], per_layer['mla_kv_norm'][l] = dgq[0, :MLA_Q_LORA], dgkv[0]
        per_layer['mla_w_uq'][l], per_layer['mla_w_ukv'][l] = _unpad_uq(dwuq), _unpad_ukv(dwukv)
        dw_in, done2 = carried(mm, f"dw_in{l}", A['h'], dproj, ta=True, comm=half_exchange(halves2) if carry else None)
        if carry:
            red.update(zip(grp2, done2))
        per_layer['w_in'][l] = _unpad_w_in(dw_in).reshape(D_MODEL, N_CHIPS, D_IN_PROJ // N_CHIPS).transpose(1, 0, 2)
        dh = mm(f"d_h{l}", dproj, A['wpad'], tb=True, add=dz1, add_scale=alpha)
    dx, dg, db = ln_bwd("ln_in_bwd", s, dh, x, _row(W['ln_in_g']))
    G = dict(per_layer)
    G['ln_in_g'], G['ln_in_b'] = dg[0], db[0]
    for n, k in red:
        G[n][k] = None
    G['reduced'] = red
    return loss_row, dx, G


def _ln_mix_bwd(l, s, dz, dh1_mm, A, W, alpha):
    ins = [('row', dz, D_MODEL, 0), ('row', dh1_mm, D_MODEL, 0), ('row', A['mix'], D_MODEL, 0), ('row', A['h'], D_MODEL, 0),
           ('full', _row(W['ln_mix_g'][l]))]

    def fn(i, j, dzv, dmv, xv, rv, g_ref, o_ref, dg_ref, db_ref):
        xh, r = _ln_stats(alpha * rv + xv)
        d, dg, db = _ln_bwd(alpha * dzv + dmv, xh, r, g_ref[...])
        o_ref[...] = d
        dg_ref[...] += dg
        db_ref[...] += db

    return tiled_call(f"ln_mix_bwd{l}", fn, s, TM, ins, [('row', D_MODEL, D_MODEL, 0, F32), ('acc', (1, D_MODEL)),
                                                         ('acc', (1, D_MODEL))])


ANY = pl.BlockSpec(memory_space=pl.ANY)


def _pos():
    return lax.axis_index("x"), lax.axis_index("y"), lax.axis_index("c")


def _rcopy(src, dst, ssem, rsem, dev):
    return pltpu.make_async_remote_copy(src_ref=src, dst_ref=dst, send_sem=ssem, recv_sem=rsem, device_id=dev,
                                        device_id_type=MESH)


class Comm:
    def __init__(self, operands, out_shape, sems, start, finish, aliases=None):
        self.operands, self.out_shape, self.sems = list(operands), list(out_shape), list(sems)
        self.start, self.finish, self.aliases = start, finish, dict(aliases or {})


def join_comms(comms):
    comms = [cm for cm in comms if cm is not None]
    if not comms:
        return None
    offs, oi, oo, os_ = [], 0, 0, 0
    for cm in comms:
        offs.append((oi, oo, os_))
        oi, oo, os_ = oi + len(cm.operands), oo + len(cm.out_shape), os_ + len(cm.sems)

    def part(fn_name):
        def run(ins, outs, sems):
            for cm, (a, b, s) in zip(comms, offs):
                getattr(cm, fn_name)(ins[a:a + len(cm.operands)], outs[b:b + len(cm.out_shape)], sems[s:s + len(cm.sems)])
        return run

    aliases = {a + i: b + o for cm, (a, b, s) in zip(comms, offs) for i, o in cm.aliases.items()}
    return Comm([v for cm in comms for v in cm.operands], [v for cm in comms for v in cm.out_shape],
                [v for cm in comms for v in cm.sems], part("start"), part("finish"), aliases)


def split_comm_results(comms, res):
    out, p = [], 0
    for cm in comms:
        if cm is None:
            out.append(None)
        else:
            out.append(list(res[p:p + len(cm.out_shape)]))
            p += len(cm.out_shape)
    return out


def run_comm(name, comm):
    n_in, n_out = len(comm.operands), len(comm.out_shape)

    def body(*refs):
        ins, outs, sems = refs[:n_in], refs[n_in:n_in + n_out], refs[n_in + n_out:]
        comm.start(ins, outs, sems)
        comm.finish(ins, outs, sems)

    return pl.pallas_call(body, name=name, in_specs=[ANY] * n_in, out_specs=[ANY] * n_out, out_shape=comm.out_shape,
                          scratch_shapes=[pltpu.SemaphoreType.DMA((k,)) for k in comm.sems],
                          input_output_aliases=comm.aliases)(*comm.operands)


def gather_over_chips(parts):
    n = len(parts)
    layer = [p[1] if isinstance(p, tuple) else None for p in parts]
    parts = [p[0] if isinstance(p, tuple) else p for p in parts]
    shapes = [p.shape if l is None else (1,) + p.shape[1:] for p, l in zip(parts, layer)]
    split = [shp[1] % 32 == 0 for shp in shapes]

    def geometry():
        x, y, c = _pos()
        return x, y, c, 2 * x + y, [(1 - x, y), (x, 1 - y), (1 - x, 1 - y)]

    def rows(t, half):
        a2 = shapes[t][1] // 2
        return pl.ds(half * a2, a2)

    def ici(srcs, dsts, sems, j, t, px, py, c, me):
        k = n * j + t
        src = srcs[t] if layer[t] is None else srcs[t].at[pl.ds(layer[t], 1)]
        if split[t]:
            return _rcopy(src.at[:, rows(t, c), :], dsts[t].at[me, :, rows(t, c), :], sems[0].at[k], sems[1].at[k], (px, py, c))
        return _rcopy(src, dsts[t].at[me], sems[0].at[k], sems[1].at[k], (px, py, c))

    def start(srcs, dsts, sems):
        x, y, c, me, chips = geometry()
        for j, (px, py) in enumerate(chips):
            for t in range(n):
                ici(srcs, dsts, sems, j, t, px, py, c, me).start()

    def finish(srcs, dsts, sems):
        x, y, c, me, chips = geometry()
        fwd = []
        for j, (px, py) in enumerate(chips):
            q = 2 * px + py
            for t in range(n):
                k = n * j + t
                if split[t]:
                    got = dsts[t].at[q, :, rows(t, c), :]
                    _rcopy(got, got, sems[0].at[k], sems[1].at[k], (px, py, c)).wait_recv()
                    fw = _rcopy(got, got, sems[2].at[k], sems[3].at[k], (x, y, 1 - c))
                    fw.start()
                    fwd.append(fw)
                else:
                    _rcopy(dsts[t].at[q], dsts[t].at[q], sems[0].at[k], sems[1].at[k], (px, py, c)).wait_recv()
        for j, (px, py) in enumerate(chips):
            q = 2 * px + py
            for t in range(n):
                if split[t]:
                    other = dsts[t].at[q, :, rows(t, 1 - c), :]
                    _rcopy(other, other, sems[2].at[n * j + t], sems[3].at[n * j + t], (x, y, 1 - c)).wait_recv()
                ici(srcs, dsts, sems, j, t, px, py, c, me).wait_send()
        for fw in fwd:
            fw.wait_send()

    return Comm(parts, [jax.ShapeDtypeStruct((N_CHIPS,) + shp, p.dtype) for p, shp in zip(parts, shapes)], [3 * n] * 4, start, finish)


def pair_exchange(xs):
    n = len(xs)

    def copies(srcs, dsts, sems):
        x, y, c = _pos()
        return [_rcopy(srcs[t].at[:, pl.ds((1 - c) * (xs[t].shape[1] // 2), xs[t].shape[1] // 2), :], dsts[t],
                       sems[0].at[t], sems[1].at[t], (x, y, 1 - c)) for t in range(n)]

    def start(srcs, dsts, sems):
        for cp in copies(srcs, dsts, sems):
            cp.start()

    def finish(srcs, dsts, sems):
        for cp in copies(srcs, dsts, sems):
            cp.wait()

    return Comm(xs, [jax.ShapeDtypeStruct((a.shape[0], a.shape[1] // 2, a.shape[2]), a.dtype) for a in xs], [n, n], start, finish)


def chip_exchange(ps):
    n = len(ps)

    def geometry():
        x, y, c = _pos()
        return c, 2 * x + y, [(1 - x, y), (x, 1 - y), (1 - x, 1 - y)]

    def start(srcs, dsts, sems):
        c, me, chips = geometry()
        for j, (px, py) in enumerate(chips):
            for t in range(n):
                _rcopy(srcs[t].at[2 * px + py], dsts[t].at[me], sems[0].at[n * j + t], sems[1].at[n * j + t], (px, py, c)).start()

    def finish(srcs, dsts, sems):
        c, me, chips = geometry()
        for j, (px, py) in enumerate(chips):
            for t in range(n):
                _rcopy(srcs[t].at[2 * px + py], dsts[t].at[2 * px + py], sems[0].at[n * j + t], sems[1].at[n * j + t], (px, py, c)).wait()

    return Comm(ps, [jax.ShapeDtypeStruct(p.shape, p.dtype) for p in ps], [3 * n, 3 * n], start, finish)


def half_exchange(bufs):
    n = len(bufs)

    def copies(ins, outs, sems):
        x, y, c = _pos()
        cps = []
        for t in range(n):
            a2 = bufs[t].shape[1] // 2
            mine = pl.ds(c * a2, a2)
            cps.append(_rcopy(ins[t].at[:, mine, :], outs[t].at[:, mine, :], sems[0].at[t], sems[1].at[t], (x, y, 1 - c)))
        return cps

    def start(ins, outs, sems):
        for cp in copies(ins, outs, sems):
            cp.start()

    def finish(ins, outs, sems):
        for cp in copies(ins, outs, sems):
            cp.wait()

    return Comm(bufs, [jax.ShapeDtypeStruct(b.shape, b.dtype) for b in bufs], [n, n], start, finish, {t: t for t in range(n)})


LANES = 1024
ROW_TILE_BYTES = 2 * 1024 * 1024


def _row_tile(r, row_bytes):
    if r * row_bytes <= ROW_TILE_BYTES:
        return r
    best = None
    for t in range(16, r, 16):
        if r % t == 0 and t * row_bytes <= ROW_TILE_BYTES:
            best = t
    assert best is not None, (r, row_bytes)
    return best


def pair_sum(name, x, r1, c_arr, out_dtype):
    q, a2, b = r1.shape
    tr = _row_tile(a2, b * 4)
    nb = a2 // tr

    def body(c_ref, x_ref, r_ref, o_ref):
        o_ref[...] = (x_ref[...] + r_ref[...]).astype(out_dtype)

    grid_spec = pltpu.PrefetchScalarGridSpec(
        num_scalar_prefetch=1, grid=(q, nb),
        in_specs=[pl.BlockSpec((None, tr, b), lambda k, i, c_ref: (k, c_ref[0] * nb + i, 0)),
                  pl.BlockSpec((None, tr, b), lambda k, i, c_ref: (k, i, 0))],
        out_specs=pl.BlockSpec((None, tr, b), lambda k, i, c_ref: (k, i, 0)))
    return pl.pallas_call(body, name=name, grid_spec=grid_spec, out_shape=jax.ShapeDtypeStruct(r1.shape, out_dtype),
                          compiler_params=_cparams(("parallel", "parallel")))(c_arr, x, r1)


def chip_sum(name, recv, own, idx):
    nl = len(recv)
    k, a2, b = recv[0].shape
    tr = _row_tile(a2, k * b * recv[0].dtype.itemsize)
    nb = a2 // tr

    def body(idx_ref, *refs):
        rs, os_, o_ref = refs[:nl], refs[nl:2 * nl], refs[2 * nl]
        l = pl.program_id(0)
        me = idx_ref[1]
        for li in range(nl):
            @pl.when(l == li)
            def _(li=li):
                acc = None
                for q in range(k):
                    term = jnp.where(me == q, os_[li][...], rs[li][q]).astype(F32)
                    acc = term if acc is None else acc + term
                o_ref[...] = acc

    def pick(li):
        return lambda l, i, idx_ref: jnp.where(l == li, i, 0)

    in_specs = [pl.BlockSpec((k, tr, b), lambda l, i, idx_ref, _p=pick(li): (0, _p(l, i, idx_ref), 0)) for li in range(nl)]
    in_specs += [pl.BlockSpec((None, tr, b), lambda l, i, idx_ref, _p=pick(li): (idx_ref[1], _p(l, i, idx_ref), 0)) for li in range(nl)]
    grid_spec = pltpu.PrefetchScalarGridSpec(
        num_scalar_prefetch=1, grid=(nl, nb), in_specs=in_specs,
        out_specs=pl.BlockSpec((None, tr, b), lambda l, i, idx_ref: (l, idx_ref[0] * nb + i, 0)))
    return pl.pallas_call(body, name=name, grid_spec=grid_spec, out_shape=jax.ShapeDtypeStruct((nl, 2 * a2, b), F32),
                          compiler_params=_cparams(("arbitrary", "arbitrary")))(idx, *recv, *own)


def adamw(name, gs, w, m, v):
    nl, r, b = w.shape
    assert len(gs) == nl
    tr = _row_tile(r, b * 4)

    def body(*refs):
        g_refs, (w_ref, m_ref, v_ref, go_ref, d_ref, nm_ref, nv_ref) = refs[:nl], refs[nl:]
        l = pl.program_id(0)
        for li in range(nl):
            @pl.when(l == li)
            def _(li=li):
                gv = g_refs[li][...]
                go_ref[...] = gv
                mn = ADAM_B1 * m_ref[...] + (1.0 - ADAM_B1) * gv
                vn = ADAM_B2 * v_ref[...] + (1.0 - ADAM_B2) * (gv * gv)
                m_hat = mn / (1.0 - ADAM_B1 ** ADAM_STEP)
                v_hat = vn / (1.0 - ADAM_B2 ** ADAM_STEP)
                d_ref[...] = -ADAM_LR * (m_hat / (jnp.sqrt(v_hat) + ADAM_EPS) + ADAM_WD * w_ref[...])
                nm_ref[...] = mn
                nv_ref[...] = vn

    g_specs = [pl.BlockSpec((None, tr, b), lambda l, i, _li=li: (0, jnp.where(l == _li, i, 0), 0)) for li in range(nl)]
    spec = pl.BlockSpec((None, tr, b), lambda l, i: (l, i, 0))
    sds = jax.ShapeDtypeStruct(w.shape, F32)
    return pl.pallas_call(body, name=name, grid=(nl, r // tr), in_specs=g_specs + [spec] * 3, out_specs=[spec] * 4,
                          out_shape=[sds] * 4, compiler_params=_cparams(("arbitrary", "arbitrary")))(*gs, w, m, v)


def adamw_many(name, gs, ws, ms, vs):
    n = len(gs)

    def body(*refs):
        for k in range(n):
            g_ref, w_ref, m_ref, v_ref, go_ref, d_ref, nm_ref, nv_ref = (refs[j * n + k] for j in range(8))
            gv = g_ref[...]
            go_ref[...] = gv
            mn = ADAM_B1 * m_ref[...] + (1.0 - ADAM_B1) * gv
            vn = ADAM_B2 * v_ref[...] + (1.0 - ADAM_B2) * (gv * gv)
            m_hat = mn / (1.0 - ADAM_B1 ** ADAM_STEP)
            v_hat = vn / (1.0 - ADAM_B2 ** ADAM_STEP)
            d_ref[...] = -ADAM_LR * (m_hat / (jnp.sqrt(v_hat) + ADAM_EPS) + ADAM_WD * w_ref[...])
            nm_ref[...] = mn
            nv_ref[...] = vn

    vm = pl.BlockSpec(memory_space=pltpu.VMEM)
    res = pl.pallas_call(body, name=name, in_specs=[vm] * (4 * n), out_specs=[vm] * (4 * n),
                         out_shape=[jax.ShapeDtypeStruct(a.shape, F32) for _ in range(4) for a in ws],
                         compiler_params=pltpu.CompilerParams(vmem_limit_bytes=VMEM_LIMIT))(*gs, *ws, *ms, *vs)
    return [list(res[j * n:(j + 1) * n]) for j in range(4)]


def _small_layout(shapes):
    views, windows, r0 = [], [], 0
    for shp in shapes:
        if len(shp) == 4:
            nblk = shp[0] * shp[1]
            assert shp[2:] == (CHUNK, 128) and nblk * 128 <= LANES
            views.append((nblk, CHUNK, 128))
            windows.append([((j,), r0, CHUNK, 128 * j, 128) for j in range(nblk)])
            r0 += CHUNK
            continue
        a, c = (1, shp[0]) if len(shp) == 1 else (int(np.prod(shp[:-1])), shp[-1])
        views.append((a, c))
        wins = []
        for j in range(-(-c // LANES)):
            w = min(LANES, c - j * LANES)
            wins.append(((slice(None), slice(j * LANES, j * LANES + w)), r0, a, 0, w))
            r0 += 8 * -(-a // 8)
        windows.append(wins)
    return 32 * -(-r0 // 32), views, windows


def pack_small(name, arrs, rows, windows):
    def body(*refs):
        ins, o_ref = refs[:-1], refs[-1]
        o_ref[...] = jnp.zeros_like(o_ref)
        for x_ref, wins in zip(ins, windows):
            for idx, r0, nr, l0, nl in wins:
                o_ref[r0:r0 + nr, l0:l0 + nl] = x_ref[idx]

    vm = pl.BlockSpec(memory_space=pltpu.VMEM)
    return pl.pallas_call(body, name=name, in_specs=[vm] * len(arrs), out_specs=vm,
                          out_shape=jax.ShapeDtypeStruct((rows, LANES), F32))(*arrs)


def adamw_windows(name, g_pack, windows, ws, ms, vs):
    n = len(ws)

    def body(g_ref, *refs):
        for k in range(n):
            w_ref, m_ref, v_ref, go_ref, d_ref, nm_ref, nv_ref = (refs[j * n + k] for j in range(7))
            for idx, r0, nr, l0, nl in windows[k]:
                gv = g_ref[r0:r0 + nr, l0:l0 + nl]
                go_ref[idx] = gv
                mn = ADAM_B1 * m_ref[idx] + (1.0 - ADAM_B1) * gv
                vn = ADAM_B2 * v_ref[idx] + (1.0 - ADAM_B2) * (gv * gv)
                m_hat = mn / (1.0 - ADAM_B1 ** ADAM_STEP)
                v_hat = vn / (1.0 - ADAM_B2 ** ADAM_STEP)
                d_ref[idx] = -ADAM_LR * (m_hat / (jnp.sqrt(v_hat) + ADAM_EPS) + ADAM_WD * w_ref[idx])
                nm_ref[idx] = mn
                nv_ref[idx] = vn

    vm = pl.BlockSpec(memory_space=pltpu.VMEM)
    res = pl.pallas_call(body, name=name, in_specs=[vm] * (1 + 3 * n), out_specs=[vm] * (4 * n),
                         out_shape=[jax.ShapeDtypeStruct(a.shape, F32) for _ in range(4) for a in ws],
                         compiler_params=pltpu.CompilerParams(vmem_limit_bytes=VMEM_LIMIT))(g_pack, *ws, *ms, *vs)
    return [list(res[j * n:(j + 1) * n]) for j in range(4)]


def _pack_rows(shapes):
    return [8 * -(-int(np.prod(s)) // (8 * LANES)) for s in shapes]


def _pack(arrs, rows):
    pieces = []
    for a, r in zip(arrs, _pack_rows([a.shape for a in arrs])):
        flat = a.reshape(-1)
        pieces.append(jnp.pad(flat, (0, r * LANES - flat.shape[0])).reshape(r, LANES))
    used = sum(p.shape[0] for p in pieces)
    assert used <= rows
    if used < rows:
        pieces.append(jnp.zeros((rows - used, LANES), F32))
    return jnp.concatenate(pieces, axis=0)


def _unpack(pack, shapes):
    out, off = [], 0
    for shp, r in zip(shapes, _pack_rows(shapes)):
        n = int(np.prod(shp))
        out.append(pack[off:off + r].reshape(-1)[:n].reshape(shp))
        off += r
    return out


BIG_NAMES = ['w_in', 'w_out', 'ffn_w_up', 'ffn_w_down']
MED_NAMES = [n for n in SHARDED_NAMES if n not in BIG_NAMES]
MED_ROWS = 96


def kernel(x, ln_in_g, ln_in_b, w_in, conv_a_w, conv_a_b, ln_a_g, ln_a_b, qk_norm_q, qk_norm_k, sgu_ln_g, sgu_ln_b, sgu_w, sgu_b, mla_q_norm, mla_w_uq, mla_kv_norm, mla_w_ukv, w_out, ln_mix_g, ln_mix_b, ffn_w_up, ffn_conv_w, ffn_conv_b, ffn_w_down, ln_ffn_g, ln_ffn_b, loss_target, m_ln_in_g, m_ln_in_b, m_w_in, m_conv_a_w, m_conv_a_b, m_ln_a_g, m_ln_a_b, m_qk_norm_q, m_qk_norm_k, m_sgu_ln_g, m_sgu_ln_b, m_sgu_w, m_sgu_b, m_mla_q_norm, m_mla_w_uq, m_mla_kv_norm, m_mla_w_ukv, m_w_out, m_ln_mix_g, m_ln_mix_b, m_ffn_w_up, m_ffn_conv_w, m_ffn_conv_b, m_ffn_w_down, m_ln_ffn_g, m_ln_ffn_b, v_ln_in_g, v_ln_in_b, v_w_in, v_conv_a_w, v_conv_a_b, v_ln_a_g, v_ln_a_b, v_qk_norm_q, v_qk_norm_k, v_sgu_ln_g, v_sgu_ln_b, v_sgu_w, v_sgu_b, v_mla_q_norm, v_mla_w_uq, v_mla_kv_norm, v_mla_w_ukv, v_w_out, v_ln_mix_g, v_ln_mix_b, v_ffn_w_up, v_ffn_conv_w, v_ffn_conv_b, v_ffn_w_down, v_ln_ffn_g, v_ln_ffn_b):
    loc = dict(locals())
    w_loc = {n: loc[n] for n in WEIGHT_NAMES}
    m_loc = {n: loc["m_" + n] for n in WEIGHT_NAMES}
    v_loc = {n: loc["v_" + n] for n in WEIGHT_NAMES}
    me_chip = 2 * lax.axis_index("x") + lax.axis_index("y")
    idx = jnp.stack([lax.axis_index("c"), me_chip]).astype(jnp.int32)

    def wire(n):
        return w_loc[n].astype(BF16) if n in BF16_WIRE else w_loc[n]

    assert DEPTH == 2
    early = [n for n in SHARDED_NAMES if n not in LATE_NAMES]
    wired = {n: wire(n) for n in SHARDED_NAMES}
    W = {n: w_loc[n] for n in REPL_NAMES}
    for n, g in zip(early, run_comm("gather_early", gather_over_chips([(wired[n], 0) for n in early]))):
        W[n] = [_assemble(n, wired[n][0:1], g, me_chip)[0], None]

    dist = dict(idx=idx, me=me_chip, late={n: wired[n] for n in LATE_NAMES}, next=(early, [wired[n] for n in early]))
    loss_row, dx, G = local_step(x[0], loss_target[0], W, dist)
    loss = lax.psum(loss_row[0, 0], ("x", "y", "c"))
    red = G['reduced']

    med = []
    for q in range(N_CHIPS):
        pieces = [lax.slice_in_dim(jnp.stack(G[n]), q * w_loc[n].shape[2], (q + 1) * w_loc[n].shape[2], axis=2) for n in MED_NAMES]
        med.append(_pack(pieces, MED_ROWS))
    rest = [(n, l) for n in BIG_NAMES for l in range(DEPTH) if (n, l) not in red]
    small_rows, views, windows = _small_layout([w_loc[n].shape for n in REPL_NAMES])
    small = [(jnp.stack(G[n]) if isinstance(G[n], list) else G[n]).reshape(v) for n, v in zip(REPL_NAMES, views)]
    small_pack = pack_small("pack_small", small, small_rows, windows)
    xs = [G[n][l] for n, l in rest] + [jnp.stack(med), jnp.broadcast_to(small_pack[None], (N_CHIPS, small_rows, LANES))]
    wire_dt = [BF16] * (len(xs) - 1) + [F32]
    r1 = run_comm("pair_exchange", pair_exchange(xs))
    ps = [pair_sum(f"pair_sum{t}", xv, rv, idx, dt) for t, (xv, rv, dt) in enumerate(zip(xs, r1, wire_dt))]
    cs = run_comm("chip_exchange", chip_exchange(ps))
    halves = [chip_sum(f"chip_sum{t}", [cv], [pv], idx) for t, (cv, pv) in enumerate(zip(cs, ps))]
    done = run_comm("half_exchange", half_exchange(halves))
    red.update(zip(rest, done[:-2]))
    g_med, g_small = done[-2], done[-1]

    outs = {k: {} for k in ("grad", "delta", "new_m", "new_v")}
    for n in BIG_NAMES:
        res = adamw(f"adamw_{n}", [red[(n, l)] for l in range(DEPTH)], w_loc[n], m_loc[n], v_loc[n])
        for k, v in zip(("grad", "delta", "new_m", "new_v"), res):
            outs[k][n] = v
    res = adamw_many("adamw_med", _unpack(g_med[0], [w_loc[n].shape for n in MED_NAMES]),
                     *[[d[n] for n in MED_NAMES] for d in (w_loc, m_loc, v_loc)])
    for k, vals in zip(("grad", "delta", "new_m", "new_v"), res):
        outs[k].update(zip(MED_NAMES, vals))
    res = adamw_windows("adamw_small", g_small[0], windows,
                        *[[d[n].reshape(v) for n, v in zip(REPL_NAMES, views)] for d in (w_loc, m_loc, v_loc)])
    for k, vals in zip(("grad", "delta", "new_m", "new_v"), res):
        outs[k].update({n: v.reshape(w_loc[n].shape) for n, v in zip(REPL_NAMES, vals)})
    return (loss, dx[None], *[outs["grad"][n] for n in WEIGHT_NAMES], *[outs["delta"][n] for n in WEIGHT_NAMES],
            *[outs["new_m"][n] for n in WEIGHT_NAMES], *[outs["new_v"][n] for n in WEIGHT_NAMES])
```
